```python
import jax, jax.numpy as jnp
from jax import lax
import numpy as np

D_MODEL = 1024
BATCH = 16
SEQ = 256
DEPTH = 2
DEC_BATCH = 2
DEC_SEQ = 1024
PAST_LEN = 512

GRID_W = 64
N_ATTN_LAYERS = (DEPTH + 1) // 2
N_FOURIER_LAYERS = DEPTH // 2
HEAD_DIM = 64
ATTN_WIDTH = D_MODEL // 2
N_HEADS = ATTN_WIDTH // HEAD_DIM
N_KV_HEADS = N_HEADS // 4
KV_GROUPS = N_HEADS // N_KV_HEADS
KV_WIDTH = N_KV_HEADS * HEAD_DIM
WINDOW = 128
Q_BLOCK = 128
POOL_WINDOWS = (2, 4, 8, 16)
POOL_GROUPS = len(POOL_WINDOWS)
POOL_WIDTH = D_MODEL // 2
POOL_GROUP_DIM = POOL_WIDTH // POOL_GROUPS
MIX_IN_A = ATTN_WIDTH + 2 * KV_WIDTH + POOL_WIDTH
MIX_OUT_A = ATTN_WIDTH + POOL_WIDTH
FOURIER_GROUPS = 4
FOURIER_WIDTH = D_MODEL
FOURIER_GROUP_DIM = FOURIER_WIDTH // FOURIER_GROUPS
N_EXPERT_GROUPS = 4
EXPERTS_PER_GROUP = 4
N_EXPERTS = N_EXPERT_GROUPS * EXPERTS_PER_GROUP
TOP_K = 2
D_EXPERT = D_MODEL // 2
ROPE_THETA = 10000.0
EPS = 1e-6
NEG = -1e30

kernel_name = "hybrid_dit_window_pool_fourier_hmoe_step"


def rmsnorm(x, g):
    xf = x.astype(jnp.float32)
    y = xf * lax.rsqrt(jnp.mean(xf * xf, axis=-1, keepdims=True) + EPS)
    return (y * g.astype(jnp.float32)).astype(x.dtype)


def adaln(cond, w, b):
    m = jax.nn.silu(cond) @ w + b
    return jnp.split(m[:, None, :], 6, axis=-1)


def modulate(x, g, shift, scale):
    return rmsnorm(x, g) * (1 + scale) + shift


def axial_rope(x):
    T = x.shape[1]
    rows = T // GRID_W
    row = jnp.repeat(jnp.arange(rows), GRID_W).astype(jnp.float32)
    col = jnp.tile(jnp.arange(GRID_W), rows).astype(jnp.float32)
    half = HEAD_DIM // 2
    nf = half // 2
    freqs = ROPE_THETA ** (-jnp.arange(nf, dtype=jnp.float32) / nf)
    xf = x.astype(jnp.float32)

    def rot(xh, pos):
        ang = pos[:, None] * freqs[None, :]
        cos = jnp.cos(ang)[None, :, None, :]
        sin = jnp.sin(ang)[None, :, None, :]
        x1, x2 = xh[..., :nf], xh[..., nf:]
        return jnp.concatenate([x1 * cos - x2 * sin, x1 * sin + x2 * cos], axis=-1)

    out = jnp.concatenate([rot(xf[..., :half], row), rot(xf[..., half:], col)], axis=-1)
    return out.astype(x.dtype)


def sink_softmax(s, sink):
    sk = sink.astype(jnp.float32).reshape(1, N_KV_HEADS, KV_GROUPS, 1, 1)
    m = jnp.maximum(jnp.max(s, axis=-1, keepdims=True), sk)
    p = jnp.exp(s - m)
    return p / (jnp.sum(p, axis=-1, keepdims=True) + jnp.exp(sk - m))


def context_attention(q, k, v, sink):
    B, L = q.shape[:2]
    nb = L // Q_BLOCK
    scale = HEAD_DIM ** -0.5
    qb = q.reshape(B, nb, Q_BLOCK, N_KV_HEADS, KV_GROUPS, HEAD_DIM).transpose(1, 0, 2, 3, 4, 5)

    def block(qblk):
        s = jnp.einsum('bqkgd,bskd->bkgqs', qblk, k).astype(jnp.float32) * scale
        p = sink_softmax(s, sink).astype(v.dtype)
        return jnp.einsum('bkgqs,bskd->bqkgd', p, v)

    o = lax.map(block, qb)
    return o.transpose(1, 0, 2, 3, 4, 5).reshape(B, L, ATTN_WIDTH)


def latent_attention(q, k, v, k_ctx, v_ctx, sink):
    B, T = q.shape[:2]
    nb = T // Q_BLOCK
    span = Q_BLOCK + 2 * WINDOW
    scale = HEAD_DIM ** -0.5
    pad = ((0, 0), (WINDOW, WINDOW), (0, 0), (0, 0))
    kp = jnp.pad(k, pad)
    vp = jnp.pad(v, pad)
    qg = q.reshape(B, T, N_KV_HEADS, KV_GROUPS, HEAD_DIM)

    def block(b):
        start = b * Q_BLOCK
        qblk = lax.dynamic_slice_in_dim(qg, start, Q_BLOCK, axis=1)
        kw = lax.dynamic_slice_in_dim(kp, start, span, axis=1)
        vw = lax.dynamic_slice_in_dim(vp, start, span, axis=1)
        qpos = start + jnp.arange(Q_BLOCK)
        kpos = start - WINDOW + jnp.arange(span)
        valid = (kpos[None, :] >= 0) & (kpos[None, :] < T) & (jnp.abs(qpos[:, None] - kpos[None, :]) <= WINDOW)
        s_win = jnp.einsum('bqkgd,bskd->bkgqs', qblk, kw).astype(jnp.float32) * scale
        s_win = jnp.where(valid[None, None, None], s_win, NEG)
        s_ctx = jnp.einsum('bqkgd,bskd->bkgqs', qblk, k_ctx).astype(jnp.float32) * scale
        p = sink_softmax(jnp.concatenate([s_win, s_ctx], axis=-1), sink).astype(v.dtype)
        return (jnp.einsum('bkgqs,bskd->bqkgd', p[..., :span], vw)
                + jnp.einsum('bkgqs,bskd->bqkgd', p[..., span:], v_ctx))

    o = lax.map(block, jnp.arange(nb))
    return o.transpose(1, 0, 2, 3, 4, 5).reshape(B, T, ATTN_WIDTH)


def multiscale_pool(u, pool_w, pool_scale):
    B, T, _ = u.shape
    uf = u.reshape(B, T, POOL_GROUPS, POOL_GROUP_DIM).astype(jnp.float32)
    cs = jnp.concatenate([jnp.zeros((B, 1, POOL_GROUPS, POOL_GROUP_DIM), jnp.float32),
                          jnp.cumsum(uf, axis=1)], axis=1)
    half = jnp.asarray(np.array(POOL_WINDOWS) // 2)[None, :]
    t = jnp.arange(T)[:, None]
    lo = jnp.clip(t - half, 0, T)
    hi = jnp.clip(t + half, 0, T)
    gidx = jnp.arange(POOL_GROUPS)[None, :]
    ssum = cs[:, hi, gidx, :] - cs[:, lo, gidx, :]
    cnt = (hi - lo).astype(jnp.float32)[None, :, :, None]
    pooled = (ssum / cnt - uf).astype(u.dtype)
    y = jnp.einsum('btgc,gcd->btgd', pooled, pool_w)
    return y.reshape(B, T, POOL_WIDTH) * pool_scale


def fourier_mix(u):
    B, T, _ = u.shape
    ug = u.reshape(B, T, FOURIER_GROUPS, FOURIER_GROUP_DIM).astype(jnp.float32)
    f = jnp.fft.fft2(ug, axes=(1, 3), norm='ortho').real
    return f.astype(u.dtype).reshape(B, T, FOURIER_WIDTH)


def attn_pool_project(h, w_in, q_g, k_g):
    B, T, _ = h.shape
    z = h @ w_in
    q, k, v, u = jnp.split(z, [ATTN_WIDTH, ATTN_WIDTH + KV_WIDTH, ATTN_WIDTH + 2 * KV_WIDTH], axis=-1)
    q = rmsnorm(q.reshape(B, T, N_HEADS, HEAD_DIM), q_g)
    k = rmsnorm(k.reshape(B, T, N_KV_HEADS, HEAD_DIM), k_g)
    v = v.reshape(B, T, N_KV_HEADS, HEAD_DIM)
    return q, k, v, u


def hier_moe(h, wg, bg, we, be, w1, w3, w2):
    B, T, D = h.shape
    x = h.reshape(-1, D)
    pg = jax.nn.softmax((x @ wg).astype(jnp.float32) + bg.astype(jnp.float32), axis=-1)
    pg_top, g_idx = lax.top_k(pg, 1)
    le = ((x @ we).astype(jnp.float32) + be.astype(jnp.float32)).reshape(-1, N_EXPERT_GROUPS, EXPERTS_PER_GROUP)
    le_g = jnp.einsum('ng,nge->ne', jax.nn.one_hot(g_idx[:, 0], N_EXPERT_GROUPS, dtype=jnp.float32), le)
    pe = jax.nn.softmax(le_g, axis=-1)
    pe_top, e_idx = lax.top_k(pe, TOP_K)
    wts = pg_top * pe_top / jnp.sum(pe_top, axis=-1, keepdims=True)
    ids = g_idx * EXPERTS_PER_GROUP + e_idx
    comb = jnp.einsum('nk,nke->ne', wts, jax.nn.one_hot(ids, N_EXPERTS, dtype=jnp.float32)).astype(x.dtype)
    h1 = jnp.einsum('nd,edf->nef', x, w1)
    h3 = jnp.einsum('nd,edf->nef', x, w3)
    a = jax.nn.silu(h1) * h3 * comb[:, :, None]
    return jnp.einsum('nef,efd->nd', a, w2).reshape(B, T, D)


def setup_inputs(seed: int = 0) -> dict:
    key = jax.random.key(seed)
    ks = jax.random.split(key, 26)
    n = jax.random.normal
    D = D_MODEL
    return {
        'x_prompt': n(ks[0], (BATCH, SEQ, D), jnp.float32),
        'x_sample': n(ks[1], (DEC_BATCH, DEC_SEQ, D), jnp.float32),
        'cache_k': n(ks[2], (DEC_BATCH, N_ATTN_LAYERS, PAST_LEN, N_KV_HEADS, HEAD_DIM), jnp.float32),
        'cache_v': n(ks[3], (DEC_BATCH, N_ATTN_LAYERS, PAST_LEN, N_KV_HEADS, HEAD_DIM), jnp.float32),
        'c': n(ks[4], (DEC_BATCH, D), jnp.float32),
        'c_ctx': n(ks[5], (D,), jnp.float32),
        'ada_w': n(ks[6], (DEPTH, D, 6 * D), jnp.float32) * (0.5 * D ** -0.5),
        'ada_b': n(ks[7], (DEPTH, 6 * D), jnp.float32) * 0.01,
        'norm_mix': 1.0 + 0.02 * n(ks[8], (DEPTH, D), jnp.float32),
        'norm_ffn': 1.0 + 0.02 * n(ks[9], (DEPTH, D), jnp.float32),
        'a_w_in': n(ks[10], (N_ATTN_LAYERS, D, MIX_IN_A), jnp.float32) * D ** -0.5,
        'a_q_norm': 1.0 + 0.02 * n(ks[11], (N_ATTN_LAYERS, HEAD_DIM), jnp.float32),
        'a_k_norm': 1.0 + 0.02 * n(ks[12], (N_ATTN_LAYERS, HEAD_DIM), jnp.float32),
        'a_sink': 0.5 * n(ks[13], (N_ATTN_LAYERS, N_HEADS), jnp.float32),
        'pool_w': n(ks[14], (N_ATTN_LAYERS, POOL_GROUPS, POOL_GROUP_DIM, POOL_GROUP_DIM), jnp.float32) * POOL_GROUP_DIM ** -0.5,
        'pool_scale': 1.0 + 0.1 * n(ks[15], (N_ATTN_LAYERS, POOL_WIDTH), jnp.float32),
        'a_w_out': n(ks[16], (N_ATTN_LAYERS, MIX_OUT_A, D), jnp.float32) * MIX_OUT_A ** -0.5,
        'f_w_in': n(ks[17], (N_FOURIER_LAYERS, D, FOURIER_WIDTH), jnp.float32) * D ** -0.5,
        'f_w_out': n(ks[18], (N_FOURIER_LAYERS, FOURIER_WIDTH, D), jnp.float32) * FOURIER_WIDTH ** -0.5,
        'router_g_w': n(ks[19], (DEPTH, D, N_EXPERT_GROUPS), jnp.float32) * D ** -0.5,
        'router_g_b': n(ks[20], (DEPTH, N_EXPERT_GROUPS), jnp.float32) * 0.01,
        'router_e_w': n(ks[21], (DEPTH, D, N_EXPERTS), jnp.float32) * D ** -0.5,
        'router_e_b': n(ks[22], (DEPTH, N_EXPERTS), jnp.float32) * 0.01,
        'moe_w1': n(ks[23], (DEPTH, N_EXPERTS, D, D_EXPERT), jnp.float32) * D ** -0.5,
        'moe_w3': n(ks[24], (DEPTH, N_EXPERTS, D, D_EXPERT), jnp.float32) * D ** -0.5,
        'moe_w2': n(ks[25], (DEPTH, N_EXPERTS, D_EXPERT, D), jnp.float32) * D_EXPERT ** -0.5,
    }


def reference(x_prompt, x_sample, cache_k, cache_v, c, c_ctx, ada_w, ada_b, norm_mix, norm_ffn,
              a_w_in, a_q_norm, a_k_norm, a_sink, pool_w, pool_scale, a_w_out, f_w_in, f_w_out,
              router_g_w, router_g_b, router_e_w, router_e_b, moe_w1, moe_w3, moe_w2):
    xp, xs = x_prompt, x_sample
    cond_ctx = c_ctx[None, :]
    new_k, new_v = [], []
    for l in range(DEPTH):
        i = l // 2
        mp = adaln(cond_ctx, ada_w[l], ada_b[l])
        ms = adaln(c, ada_w[l], ada_b[l])
        hp = modulate(xp, norm_mix[l], mp[0], mp[1])
        hs = modulate(xs, norm_mix[l], ms[0], ms[1])
        if l % 2 == 0:
            qp, kp, vp, up = attn_pool_project(hp, a_w_in[i], a_q_norm[i], a_k_norm[i])
            new_k.append(kp)
            new_v.append(vp)
            op = jnp.concatenate([context_attention(qp, kp, vp, a_sink[i]),
                                  multiscale_pool(up, pool_w[i], pool_scale[i])], axis=-1) @ a_w_out[i]
            qs, ks_, vs, us = attn_pool_project(hs, a_w_in[i], a_q_norm[i], a_k_norm[i])
            qs = axial_rope(qs)
            ks_ = axial_rope(ks_)
            os_ = jnp.concatenate([latent_attention(qs, ks_, vs, cache_k[:, i], cache_v[:, i], a_sink[i]),
                                   multiscale_pool(us, pool_w[i], pool_scale[i])], axis=-1) @ a_w_out[i]
        else:
            op = fourier_mix(hp @ f_w_in[i]) @ f_w_out[i]
            os_ = fourier_mix(hs @ f_w_in[i]) @ f_w_out[i]
        xp = xp + mp[2] * op
        xs = xs + ms[2] * os_
        hp = modulate(xp, norm_ffn[l], mp[3], mp[4])
        hs = modulate(xs, norm_ffn[l], ms[3], ms[4])
        xp = xp + mp[5] * hier_moe(hp, router_g_w[l], router_g_b[l], router_e_w[l], router_e_b[l],
                                   moe_w1[l], moe_w3[l], moe_w2[l])
        xs = xs + ms[5] * hier_moe(hs, router_g_w[l], router_g_b[l], router_e_w[l], router_e_b[l],
                                   moe_w1[l], moe_w3[l], moe_w2[l])
    new_cache_k = jnp.stack(new_k, axis=1)
    new_cache_v = jnp.stack(new_v, axis=1)
    return (xp, xs, new_cache_k, new_cache_v)
```

```python
import functools

import numpy as np
import jax
import jax.numpy as jnp
from jax import lax
from jax.experimental import pallas as pl
from jax.experimental.pallas import tpu as pltpu

F32 = jnp.float32
BF16 = jnp.bfloat16
I32 = jnp.int32

D = 1024
DEPTH = 2
N_CTX_B, T_CTX = 16, 256
N_LAT_B, T_LAT = 2, 1024
N_CTX = N_CTX_B * T_CTX
N_LAT = N_LAT_B * T_LAT
N_TOK = N_CTX + N_LAT
PAST = 512
GRID_W = 64
HEAD_DIM = 64
N_HEADS = 8
ATTN_W = 512
KV_W = 128
POOL_W = 512
POOL_WINDOWS = (2, 4, 8, 16)
MIX_IN = ATTN_W + 2 * KV_W + POOL_W
WINDOW = 128
N_GROUPS = 4
PER_GROUP = 4
N_EXPERTS = 16
D_EXPERT = 512
ROPE_THETA = 10000.0
EPS = 1e-6
NEG = -1e30

LANES = 128
SUBLANES = 8
TB = 256
NT = N_TOK // TB
NT_CTX = N_CTX // TB
TB_MIX = 1024
TM = 256
NT_EXP = 2 * N_TOK // TM + N_EXPERTS
P_ROWS = NT_EXP * TM
ROUTE_ROWS = 32
SLABS = D // LANES

VMEM_LIMIT = 56 * 1024 * 1024


def _cparams(n_axes=1, vmem=None):
    return pltpu.CompilerParams(dimension_semantics=("arbitrary",) * n_axes,
                                vmem_limit_bytes=vmem)


def _modulate(x, g, shift, scale):
    ms = jnp.mean(x * x, axis=-1, keepdims=True)
    return (x * lax.rsqrt(ms + EPS) * g) * (1.0 + scale) + shift


def _mod_row(tile, tiles_ctx, tiles_per_lat):
    return (tile >= tiles_ctx).astype(I32) + (tile >= tiles_ctx + tiles_per_lat).astype(I32)


def _rope_tables():
    t = np.arange(T_LAT)
    row = (t // GRID_W).astype(np.float64)
    col = (t % GRID_W).astype(np.float64)
    nf = HEAD_DIM // 4
    freqs = ROPE_THETA ** (-np.arange(nf, dtype=np.float64) / nf)
    d = np.arange(HEAD_DIM)
    pos = np.where(d[None, :] < HEAD_DIM // 2, row[:, None], col[:, None])
    ang = pos * freqs[d % nf][None, :]
    first = (d % (HEAD_DIM // 2)) < nf
    cos = np.cos(ang)
    sin_a = np.where(first[None, :], -np.sin(ang), 0.0)
    sin_b = np.where(first[None, :], 0.0, np.sin(ang))
    ident = (np.ones((TB, HEAD_DIM)), np.zeros((TB, HEAD_DIM)), np.zeros((TB, HEAD_DIM)))
    out = []
    for tab, idt in zip((cos, sin_a, sin_b), ident):
        full = np.concatenate([tab, idt], axis=0)
        out.append(jnp.asarray(np.tile(full, (1, LANES // HEAD_DIM)), F32))
    return out


def _dft_tables(t):
    m = np.outer(np.arange(t), np.arange(t)) % t
    ang = 2.0 * np.pi * m / t
    s = 1.0 / np.sqrt(t)
    return jnp.asarray(np.cos(ang) * s, F32).astype(BF16), jnp.asarray(np.sin(ang) * s, F32).astype(BF16)


def _adaln_body(cond_ref, w_ref, b_ref, o_ref):
    c = cond_ref[...]
    s = (c * jax.nn.sigmoid(c)).astype(BF16)
    o_ref[0] = jnp.dot(s, w_ref[0].astype(BF16), preferred_element_type=F32) + b_ref[0]


def _adaln(cond8, ada_w, ada_b):
    tn = 1536
    return pl.pallas_call(
        _adaln_body,
        grid=(DEPTH, 6 * D // tn),
        in_specs=[pl.BlockSpec((SUBLANES, D), lambda l, j: (0, 0)),
                  pl.BlockSpec((1, D, tn), lambda l, j: (l, 0, j)),
                  pl.BlockSpec((1, 1, tn), lambda l, j: (l, 0, j))],
        out_specs=pl.BlockSpec((1, SUBLANES, tn), lambda l, j: (l, 0, j)),
        out_shape=jax.ShapeDtypeStruct((DEPTH, SUBLANES, 6 * D), F32),
        compiler_params=_cparams(2),
        name="adaln",
    )(cond8, ada_w, ada_b.reshape(DEPTH, 1, 6 * D))


def _l0_in_body(xp_ref, xs_ref, mod_ref, g_ref, w_ref, qg_ref, kg_ref, bd_ref,
                cos_ref, sa_ref, sb_ref, q_ref, k_ref, v_ref, u_ref):
    i = pl.program_id(0)
    x = jnp.where(i < NT_CTX, xp_ref[...], xs_ref[...])
    m = mod_ref[0]
    h = _modulate(x, g_ref[...], m[:, 0:D], m[:, D:2 * D])
    z = jnp.dot(h.astype(BF16), w_ref[...], preferred_element_type=F32)
    cos, sa, sb, bd = cos_ref[...], sa_ref[...], sb_ref[...], bd_ref[...]

    def head_norm_rope(zz, gain):
        ss = jnp.dot((zz * zz).astype(BF16), bd, preferred_element_type=F32)
        y = zz * lax.rsqrt(ss * (1.0 / HEAD_DIM) + EPS) * gain
        return (y * cos + pltpu.roll(y, LANES - 16, axis=1) * sa
                + pltpu.roll(y, 16, axis=1) * sb)

    for s in range(ATTN_W // LANES):
        qs = head_norm_rope(z[:, s * LANES:(s + 1) * LANES], qg_ref[...])
        q_ref[:, s * LANES:(s + 1) * LANES] = (qs * (HEAD_DIM ** -0.5)).astype(BF16)
    k_ref[...] = head_norm_rope(z[:, ATTN_W:ATTN_W + KV_W], kg_ref[...])
    v_ref[...] = z[:, ATTN_W + KV_W:ATTN_W + 2 * KV_W]
    u_ref[...] = z[:, ATTN_W + 2 * KV_W:MIX_IN]


def _l0_in(xp, xs, mods, g, w_in, qg, kg, bd, tabs):
    tab_spec = pl.BlockSpec(
        (TB, LANES), lambda i: (jnp.where(i < NT_CTX, T_LAT // TB, (i - NT_CTX) % (T_LAT // TB)), 0))
    const = lambda shape: pl.BlockSpec(shape, lambda i: (0,) * len(shape))
    return pl.pallas_call(
        _l0_in_body,
        grid=(NT,),
        in_specs=[pl.BlockSpec((TB, D), lambda i: (jnp.minimum(i, NT_CTX - 1), 0)),
                  pl.BlockSpec((TB, D), lambda i: (jnp.maximum(i - NT_CTX, 0), 0)),
                  pl.BlockSpec((1, 1, 6 * D), lambda i: (_mod_row(i, NT_CTX, T_LAT // TB), 0, 0)),
                  const((1, D)), const((D, MIX_IN)), const((1, LANES)), const((1, LANES)),
                  const((LANES, LANES)), tab_spec, tab_spec, tab_spec],
        out_specs=[pl.BlockSpec((TB, ATTN_W), lambda i: (i, 0)),
                   pl.BlockSpec((TB, KV_W), lambda i: (i, 0)),
                   pl.BlockSpec((TB, KV_W), lambda i: (i, 0)),
                   pl.BlockSpec((TB, POOL_W), lambda i: (i, 0))],
        out_shape=[jax.ShapeDtypeStruct((N_TOK, ATTN_W), BF16),
                   jax.ShapeDtypeStruct((N_TOK, KV_W), F32),
                   jax.ShapeDtypeStruct((N_TOK, KV_W), F32),
                   jax.ShapeDtypeStruct((N_TOK, POOL_W), F32)],
        compiler_params=_cparams(1),
        name="l0_in",
    )(xp, xs, mods, g, w_in, qg, kg, bd, *tabs)


def _head_halves(x):
    z = jnp.zeros_like(x)
    return jnp.concatenate([x, z], axis=1), jnp.concatenate([z, x], axis=1)


_NT_DIMS = (((1,), (1,)), ((), ()))


def _ctx_attn_body(q_ref, k_ref, v_ref, sink_ref, o_ref):
    k = k_ref[...].astype(BF16)
    v = v_ref[...].astype(BF16)
    lo = lax.broadcasted_iota(I32, (T_CTX, LANES), 1) < HEAD_DIM
    for j in range(KV_W // HEAD_DIM):
        kj = k[:, j * HEAD_DIM:(j + 1) * HEAD_DIM]
        vj = v[:, j * HEAD_DIM:(j + 1) * HEAD_DIM]
        k_halves = _head_halves(kj)
        vd = jnp.concatenate([vj, vj], axis=1)
        for s2 in range(2):
            s = 2 * j + s2
            qs = q_ref[:, s * LANES:(s + 1) * LANES]
            outs = []
            for half in range(2):
                hh = 2 * s + half
                sc = lax.dot_general(qs, k_halves[half], _NT_DIMS, preferred_element_type=F32)
                sk = sink_ref[hh:hh + 1, 0:1]
                mx = jnp.maximum(jnp.max(sc, axis=-1, keepdims=True), sk)
                p = jnp.exp(sc - mx)
                den = jnp.sum(p, axis=-1, keepdims=True) + jnp.exp(sk - mx)
                outs.append(jnp.dot((p / den).astype(BF16), vd, preferred_element_type=F32))
            o_ref[:, s * LANES:(s + 1) * LANES] = jnp.where(lo, outs[0], outs[1]).astype(BF16)


def _ctx_attn(q, k, v, sink_b):
    return pl.pallas_call(
        _ctx_attn_body,
        grid=(N_CTX_B,),
        in_specs=[pl.BlockSpec((T_CTX, ATTN_W), lambda b: (b, 0)),
                  pl.BlockSpec((T_CTX, KV_W), lambda b: (b, 0)),
                  pl.BlockSpec((T_CTX, KV_W), lambda b: (b, 0)),
                  pl.BlockSpec((SUBLANES, LANES), lambda b: (0, 0))],
        out_specs=pl.BlockSpec((T_CTX, ATTN_W), lambda b: (b, 0)),
        out_shape=jax.ShapeDtypeStruct((N_CTX, ATTN_W), BF16),
        compiler_params=_cparams(1),
        name="ctx_attn",
    )(q, k, v, sink_b)


QB = 128
SPAN = QB + 2 * WINDOW


def _lat_attn_body(q_ref, k_ref, v_ref, ck_ref, cv_ref, sink_ref, o_ref):
    qb = pl.program_id(1)
    start = qb * QB
    kws, vws = [], []
    for c in (-1, 0, 1):
        cs = pl.multiple_of(jnp.clip(start + c * QB, 0, T_LAT - QB), QB)
        kws.append(k_ref[pl.ds(cs, QB), :])
        vws.append(v_ref[pl.ds(cs, QB), :])
    kw = jnp.concatenate(kws, axis=0).astype(BF16)
    vw = jnp.concatenate(vws, axis=0).astype(BF16)
    ck = ck_ref[0].astype(BF16)
    cv = cv_ref[0].astype(BF16)
    qpos = start + lax.broadcasted_iota(I32, (QB, SPAN), 0)
    kpos = start - WINDOW + lax.broadcasted_iota(I32, (QB, SPAN), 1)
    valid = (kpos >= 0) & (kpos < T_LAT) & (jnp.abs(qpos - kpos) <= WINDOW)
    lo = lax.broadcasted_iota(I32, (QB, LANES), 1) < HEAD_DIM
    for j in range(KV_W // HEAD_DIM):
        sl = slice(j * HEAD_DIM, (j + 1) * HEAD_DIM)
        kw_halves = _head_halves(kw[:, sl])
        ck_halves = _head_halves(ck[:, sl])
        vwd = jnp.concatenate([vw[:, sl], vw[:, sl]], axis=1)
        cvd = jnp.concatenate([cv[:, sl], cv[:, sl]], axis=1)
        for s2 in range(2):
            s = 2 * j + s2
            qs = q_ref[:, s * LANES:(s + 1) * LANES]
            outs = []
            for half in range(2):
                hh = 2 * s + half
                s_win = lax.dot_general(qs, kw_halves[half], _NT_DIMS, preferred_element_type=F32)
                s_win = jnp.where(valid, s_win, NEG)
                s_ctx = lax.dot_general(qs, ck_halves[half], _NT_DIMS, preferred_element_type=F32)
                sk = sink_ref[hh:hh + 1, 0:1]
                mx = jnp.maximum(jnp.maximum(jnp.max(s_win, axis=-1, keepdims=True),
                                             jnp.max(s_ctx, axis=-1, keepdims=True)), sk)
                pw = jnp.exp(s_win - mx)
                pc = jnp.exp(s_ctx - mx)
                den = (jnp.sum(pw, axis=-1, keepdims=True) + jnp.sum(pc, axis=-1, keepdims=True)
                       + jnp.exp(sk - mx))
                outs.append(jnp.dot((pw / den).astype(BF16), vwd, preferred_element_type=F32)
                            + jnp.dot((pc / den).astype(BF16), cvd, preferred_element_type=F32))
            o_ref[:, s * LANES:(s + 1) * LANES] = jnp.where(lo, outs[0], outs[1]).astype(BF16)


def _lat_attn(q, k, v, ck, cv, sink_b):
    lat0 = N_CTX // T_LAT
    return pl.pallas_call(
        _lat_attn_body,
        grid=(N_LAT_B, T_LAT // QB),
        in_specs=[pl.BlockSpec((QB, ATTN_W), lambda b, i: (N_CTX // QB + b * (T_LAT // QB) + i, 0)),
                  pl.BlockSpec((T_LAT, KV_W), lambda b, i: (lat0 + b, 0)),
                  pl.BlockSpec((T_LAT, KV_W), lambda b, i: (lat0 + b, 0)),
                  pl.BlockSpec((1, PAST, KV_W), lambda b, i: (b, 0, 0)),
                  pl.BlockSpec((1, PAST, KV_W), lambda b, i: (b, 0, 0)),
                  pl.BlockSpec((SUBLANES, LANES), lambda b, i: (0, 0))],
        out_specs=pl.BlockSpec((QB, ATTN_W), lambda b, i: (b * (T_LAT // QB) + i, 0)),
        out_shape=jax.ShapeDtypeStruct((N_LAT, ATTN_W), BF16),
        compiler_params=_cparams(2),
        name="lat_attn",
    )(q, k, v, ck, cv, sink_b)


def _l0_out_body(oc_ref, ol_ref, u_ref, xp_ref, xs_ref, mod_ref, pw_ref, ps_ref, wo_ref, x1_ref):
    i = pl.program_id(0)
    is_ctx = i < N_CTX // TB_MIX
    o = jnp.where(is_ctx, oc_ref[...], ol_ref[...])
    x = jnp.where(is_ctx, xp_ref[...], xs_ref[...])
    tseq = jnp.where(is_ctx, T_CTX, T_LAT)
    pos = lax.broadcasted_iota(I32, (TB_MIX, LANES), 0) & (tseq - 1)
    ys = []
    for g, win in enumerate(POOL_WINDOWS):
        hw = win // 2
        ug = u_ref[:, g * LANES:(g + 1) * LANES]
        acc = ug
        for jj in range(-hw, hw):
            if jj == 0:
                continue
            sh = pltpu.roll(ug, (-jj) % TB_MIX, axis=0)
            ok = (pos + jj >= 0) if jj < 0 else (pos + jj < tseq)
            acc = acc + jnp.where(ok, sh, 0.0)
        cnt = (jnp.minimum(pos + hw, tseq) - jnp.maximum(pos - hw, 0)).astype(F32)
        pooled = acc / cnt - ug
        ys.append(jnp.dot(pooled.astype(BF16), pw_ref[g], preferred_element_type=F32))
    y = jnp.concatenate(ys, axis=1) * ps_ref[...]
    out = (jnp.dot(o, wo_ref[0:ATTN_W, :], preferred_element_type=F32)
           + jnp.dot(y.astype(BF16), wo_ref[ATTN_W:ATTN_W + POOL_W, :], preferred_element_type=F32))
    x1_ref[...] = x + mod_ref[0][:, 2 * D:3 * D] * out


def _l0_out(o_ctx, o_lat, u, xp, xs, mods, pool_w, pool_scale, w_out):
    ntc = N_CTX // TB_MIX
    const = lambda shape: pl.BlockSpec(shape, lambda i: (0,) * len(shape))
    ctx_map = lambda i: (jnp.minimum(i, ntc - 1), 0)
    lat_map = lambda i: (jnp.maximum(i - ntc, 0), 0)
    return pl.pallas_call(
        _l0_out_body,
        grid=(N_TOK // TB_MIX,),
        in_specs=[pl.BlockSpec((TB_MIX, ATTN_W), ctx_map),
                  pl.BlockSpec((TB_MIX, ATTN_W), lat_map),
                  pl.BlockSpec((TB_MIX, POOL_W), lambda i: (i, 0)),
                  pl.BlockSpec((TB_MIX, D), ctx_map),
                  pl.BlockSpec((TB_MIX, D), lat_map),
                  pl.BlockSpec((1, 1, 6 * D), lambda i: (_mod_row(i, ntc, 1), 0, 0)),
                  const((len(POOL_WINDOWS), LANES, LANES)), const((1, POOL_W)), const((D, D))],
        out_specs=pl.BlockSpec((TB_MIX, D), lambda i: (i, 0)),
        out_shape=jax.ShapeDtypeStruct((N_TOK, D), F32),
        compiler_params=_cparams(1, VMEM_LIMIT),
        name="l0_out",
    )(o_ctx, o_lat, u, xp, xs, mods, pool_w, pool_scale, w_out)


def _first_max(vals):
    best, idx = vals[0], jnp.zeros(vals[0].shape, I32)
    for r in range(1, len(vals)):
        better = vals[r] > best
        idx = jnp.where(better, r, idx)
        best = jnp.where(better, vals[r], best)
    return best, idx


def _softmax_rows(rows):
    mx = functools.reduce(jnp.maximum, rows)
    ex = [jnp.exp(r - mx) for r in rows]
    tot = functools.reduce(lambda a, b: a + b, ex)
    return [e / tot for e in ex]


def _route_body(x_ref, mod_ref, g_ref, wr_ref, br_ref, tri_ref,
                h3_ref, info_ref, tok_ref, cnt_ref, base_ref):
    i = pl.program_id(0)

    @pl.when(i == 0)
    def _():
        base_ref[...] = jnp.zeros_like(base_ref)

    m = mod_ref[0]
    h = _modulate(x_ref[...], g_ref[...], m[:, 3 * D:4 * D], m[:, 4 * D:5 * D])
    for s in range(SLABS):
        h3_ref[:, s, :] = h[:, s * LANES:(s + 1) * LANES]

    hh = h.astype(BF16)
    hl = (h - hh.astype(F32)).astype(BF16)
    wh, wl = wr_ref[0], wr_ref[1]
    lg = (lax.dot_general(wh, hh, _NT_DIMS, preferred_element_type=F32)
          + lax.dot_general(wl, hh, _NT_DIMS, preferred_element_type=F32)
          + lax.dot_general(wh, hl, _NT_DIMS, preferred_element_type=F32)) + br_ref[:, 0:1]

    pg = _softmax_rows([lg[N_EXPERTS + r:N_EXPERTS + r + 1] for r in range(N_GROUPS)])
    pg_top, gi = _first_max(pg)
    le = []
    for j in range(PER_GROUP):
        sel = lg[(N_GROUPS - 1) * PER_GROUP + j:(N_GROUPS - 1) * PER_GROUP + j + 1]
        for g in range(N_GROUPS - 2, -1, -1):
            sel = jnp.where(gi == g, lg[g * PER_GROUP + j:g * PER_GROUP + j + 1], sel)
        le.append(sel)
    pe = _softmax_rows(le)
    p1, i1 = _first_max(pe)
    p2, i2 = _first_max([jnp.where(i1 == j, -1.0, pe[j]) for j in range(PER_GROUP)])
    den = p1 + p2
    w1 = pg_top * p1 / den
    w2 = pg_top * p2 / den
    id1 = gi * PER_GROUP + i1
    id2 = gi * PER_GROUP + i2

    erow = lax.broadcasted_iota(I32, (N_EXPERTS, TB), 0)
    hit1 = erow == id1
    hit2 = erow == id2
    onehot = jnp.where(hit1, 1.0, jnp.where(hit2, 1.0, 0.0))
    before = jnp.dot(onehot.astype(BF16), tri_ref[...], preferred_element_type=F32)
    before = before + base_ref[:, 0:1]
    r1 = jnp.sum(jnp.where(hit1, before, 0.0), axis=0, keepdims=True)
    r2 = jnp.sum(jnp.where(hit2, before, 0.0), axis=0, keepdims=True)
    base_ref[...] = base_ref[...] + jnp.sum(onehot, axis=1, keepdims=True)
    cnt_ref[...] = base_ref[...]

    zero = jnp.zeros_like(w1)
    info = jnp.concatenate([id1.astype(F32), id2.astype(F32), r1, r2, w1, w2, zero, zero], axis=0)
    info_ref[...] = info
    tok_ref[...] = jnp.concatenate([info, jnp.zeros((LANES - SUBLANES, TB), F32)], axis=0).T


def _route(x, mods, layer, g, wr, br, tri):
    const = lambda shape: pl.BlockSpec(shape, lambda i: (0,) * len(shape))
    return pl.pallas_call(
        _route_body,
        grid=(NT,),
        in_specs=[pl.BlockSpec((TB, D), lambda i: (i, 0)),
                  pl.BlockSpec((1, 1, 6 * D),
                               lambda i: (layer * SUBLANES + _mod_row(i, NT_CTX, T_LAT // TB), 0, 0)),
                  const((1, D)), const((2, ROUTE_ROWS, D)), const((ROUTE_ROWS, LANES)),
                  const((TB, TB))],
        out_specs=[pl.BlockSpec((TB, SLABS, LANES), lambda i: (i, 0, 0)),
                   pl.BlockSpec((SUBLANES, TB), lambda i: (0, i)),
                   pl.BlockSpec((TB, LANES), lambda i: (i, 0)),
                   pl.BlockSpec((N_EXPERTS, LANES), lambda i: (0, 0))],
        out_shape=[jax.ShapeDtypeStruct((N_TOK, SLABS, LANES), F32),
                   jax.ShapeDtypeStruct((SUBLANES, N_TOK), F32),
                   jax.ShapeDtypeStruct((N_TOK, LANES), F32),
                   jax.ShapeDtypeStruct((N_EXPERTS, LANES), F32)],
        scratch_shapes=[pltpu.VMEM((N_EXPERTS, LANES), F32)],
        compiler_params=_cparams(1),
        name=f"route{layer}",
    )(x, mods, g, wr, br, tri)


def _plan(info, counts):
    ids = info[0:2].astype(I32)
    ranks = info[2:4].astype(I32)
    cnt = counts[:, 0].astype(I32)
    tiles = (cnt + TM - 1) // TM
    tend = jnp.cumsum(tiles)
    tstart = tend - tiles
    n_used = tend[-1]
    pos = (jnp.take(tstart * TM, ids) + ranks).reshape(2 * N_TOK)
    n = jnp.arange(NT_EXP, dtype=I32)
    last = jnp.maximum(n_used - 1, 0)
    tile_blk = jnp.minimum(n, last)
    tile_e = jnp.minimum(jnp.sum((tile_blk[:, None] >= tend[None, :]).astype(I32), axis=1),
                         N_EXPERTS - 1)
    partial = jnp.any((n[:, None] == tend[None, :] - 1) & (tiles[None, :] > 0), axis=1)
    zflag = (partial | (n >= n_used)).astype(I32)
    return pos, tile_e, tile_blk, n_used.reshape(1), zflag


DISPATCH_WAIT_ROWS = 1024


def _dispatch_body(pos_ref, zf_ref, h3_hbm, xs3_hbm, zbuf, sem_z, sem):
    zbuf[...] = jnp.zeros_like(zbuf)
    for n in range(NT_EXP):
        @pl.when(zf_ref[n] > 0)
        def _():
            pltpu.make_async_copy(zbuf, xs3_hbm.at[pl.ds(n * TM, TM)], sem_z).start()
    for n in range(NT_EXP):
        @pl.when(zf_ref[n] > 0)
        def _():
            pltpu.make_async_copy(zbuf, xs3_hbm.at[pl.ds(n * TM, TM)], sem_z).wait()

    def issue(p, carry):
        t = jnp.where(p >= N_TOK, p - N_TOK, p)
        pltpu.make_async_copy(h3_hbm.at[t], xs3_hbm.at[pos_ref[p]], sem).start()
        return carry

    lax.fori_loop(0, 2 * N_TOK, issue, 0, unroll=8)
    for _ in range(2 * N_TOK // DISPATCH_WAIT_ROWS):
        pltpu.make_async_copy(h3_hbm.at[pl.ds(0, DISPATCH_WAIT_ROWS)],
                              xs3_hbm.at[pl.ds(0, DISPATCH_WAIT_ROWS)], sem).wait()


def _dispatch(pos, zflag, h3, layer):
    return pl.pallas_call(
        _dispatch_body,
        grid_spec=pltpu.PrefetchScalarGridSpec(
            num_scalar_prefetch=2, grid=(1,),
            in_specs=[pl.BlockSpec(memory_space=pl.ANY)],
            out_specs=pl.BlockSpec(memory_space=pl.ANY),
            scratch_shapes=[pltpu.VMEM((TM, SLABS, LANES), F32),
                            pltpu.SemaphoreType.DMA, pltpu.SemaphoreType.DMA]),
        out_shape=jax.ShapeDtypeStruct((P_ROWS, SLABS, LANES), F32),
        compiler_params=_cparams(1),
        name=f"dispatch{layer}",
    )(pos, zflag, h3)


def _experts_body(te_ref, tb_ref, nu_ref, x3_ref, w1_ref, w3_ref, w2_ref, y3_ref):
    n = pl.program_id(0)

    @pl.when(n < nu_ref[0])
    def _():
        x = jnp.concatenate([x3_ref[:, s, :] for s in range(SLABS)], axis=1).astype(BF16)
        h1 = jnp.dot(x, w1_ref[0, 0].astype(BF16), preferred_element_type=F32)
        h3 = jnp.dot(x, w3_ref[0, 0].astype(BF16), preferred_element_type=F32)
        a = (h1 * jax.nn.sigmoid(h1)) * h3
        y = jnp.dot(a.astype(BF16), w2_ref[0, 0].astype(BF16), preferred_element_type=F32)
        for s in range(SLABS):
            y3_ref[:, s, :] = y[:, s * LANES:(s + 1) * LANES]

    @pl.when(n >= nu_ref[0])
    def _():
        y3_ref[...] = jnp.zeros_like(y3_ref)


def _experts(tile_e, tile_blk, n_used, xs3, w1, w3, w2, layer):
    wmap = lambda n, te, tb, nu: (layer, te[n], 0, 0)
    xmap = lambda n, te, tb, nu: (tb[n], 0, 0)
    ymap = lambda n, te, tb, nu: (n, 0, 0)
    return pl.pallas_call(
        _experts_body,
        grid_spec=pltpu.PrefetchScalarGridSpec(
            num_scalar_prefetch=3, grid=(NT_EXP,),
            in_specs=[pl.BlockSpec((TM, SLABS, LANES), xmap),
                      pl.BlockSpec((1, 1, D, D_EXPERT), wmap),
                      pl.BlockSpec((1, 1, D, D_EXPERT), wmap),
                      pl.BlockSpec((1, 1, D_EXPERT, D), wmap)],
            out_specs=pl.BlockSpec((TM, SLABS, LANES), ymap)),
        out_shape=jax.ShapeDtypeStruct((P_ROWS, SLABS, LANES), F32),
        compiler_params=_cparams(1, VMEM_LIMIT),
        name=f"experts{layer}",
    )(tile_e, tile_blk, n_used, xs3, w1, w3, w2)


def _combine_body(split_out, pos_ref, y3_hbm, x_ref, tok_ref, mod_ref, *rest):
    if split_out:
        oc_ref, ol_ref, buf, sem = rest
    else:
        o_ref, buf, sem = rest
    i = pl.program_id(0)
    base = i * TB

    def issue(r, carry):
        for k in range(2):
            pltpu.make_async_copy(y3_hbm.at[pos_ref[k * N_TOK + base + r]], buf.at[k, r], sem).start()
        return carry

    lax.fori_loop(0, TB, issue, 0, unroll=8)
    for k in range(2):
        pltpu.make_async_copy(y3_hbm.at[pl.ds(0, TB)], buf.at[k], sem).wait()

    w1 = tok_ref[:, 4:5]
    w2 = tok_ref[:, 5:6]
    gate = mod_ref[0][:, 5 * D:6 * D]
    for s in range(SLABS):
        sl = slice(s * LANES, (s + 1) * LANES)
        yy = w1 * buf[0, :, s, :] + w2 * buf[1, :, s, :]
        res = x_ref[:, sl] + gate[:, sl] * yy
        if split_out:
            @pl.when(i < NT_CTX)
            def _():
                oc_ref[:, sl] = res

            @pl.when(i >= NT_CTX)
            def _():
                ol_ref[:, sl] = res
        else:
            o_ref[:, sl] = res


def _combine(pos, y3, x, tok, mods, layer, split_out):
    if split_out:
        out_specs = [pl.BlockSpec((TB, D), lambda i, p: (jnp.minimum(i, NT_CTX - 1), 0)),
                     pl.BlockSpec((TB, D), lambda i, p: (jnp.maximum(i - NT_CTX, 0), 0))]
        out_shape = [jax.ShapeDtypeStruct((N_CTX, D), F32), jax.ShapeDtypeStruct((N_LAT, D), F32)]
    else:
        out_specs = pl.BlockSpec((TB, D), lambda i, p: (i, 0))
        out_shape = jax.ShapeDtypeStruct((N_TOK, D), F32)
    return pl.pallas_call(
        functools.partial(_combine_body, split_out),
        grid_spec=pltpu.PrefetchScalarGridSpec(
            num_scalar_prefetch=1, grid=(NT,),
            in_specs=[pl.BlockSpec(memory_space=pl.ANY),
                      pl.BlockSpec((TB, D), lambda i, p: (i, 0)),
                      pl.BlockSpec((TB, LANES), lambda i, p: (i, 0)),
                      pl.BlockSpec((1, 1, 6 * D),
                                   lambda i, p: (layer * SUBLANES + _mod_row(i, NT_CTX, T_LAT // TB), 0, 0))],
            out_specs=out_specs,
            scratch_shapes=[pltpu.VMEM((2, TB, SLABS, LANES), F32), pltpu.SemaphoreType.DMA]),
        out_shape=out_shape,
        compiler_params=_cparams(1),
        name=f"combine{layer}",
    )(pos, y3, x, tok, mods)


def _moe(x, mods, layer, g, wr, br, tri, w1, w3, w2, split_out):
    h3, info, tok, counts = _route(x, mods, layer, g, wr, br, tri)
    pos, tile_e, tile_blk, n_used, zflag = _plan(info, counts)
    xs3 = _dispatch(pos, zflag, h3, layer)
    y3 = _experts(tile_e, tile_blk, n_used, xs3, w1, w3, w2, layer)
    return _combine(pos, y3, x, tok, mods, layer, split_out)


FG = 256


def _l1_in_body(x_ref, mod_ref, g_ref, w_ref, c_ref, s_ref, zc_ref, zs_ref):
    m = mod_ref[0]
    h = _modulate(x_ref[...], g_ref[...], m[:, 0:D], m[:, D:2 * D])
    z = jnp.dot(h.astype(BF16), w_ref[...], preferred_element_type=F32).astype(BF16)
    for g in range(D // FG):
        zg = z[:, g * FG:(g + 1) * FG]
        zc_ref[:, g * FG:(g + 1) * FG] = jnp.dot(zg, c_ref[...], preferred_element_type=F32).astype(BF16)
        zs_ref[:, g * FG:(g + 1) * FG] = jnp.dot(zg, s_ref[...], preferred_element_type=F32).astype(BF16)


def _l1_in(x, mods, g, w, c256, s256):
    const = lambda shape: pl.BlockSpec(shape, lambda i: (0,) * len(shape))
    return pl.pallas_call(
        _l1_in_body,
        grid=(NT,),
        in_specs=[pl.BlockSpec((TB, D), lambda i: (i, 0)),
                  pl.BlockSpec((1, 1, 6 * D),
                               lambda i: (SUBLANES + _mod_row(i, NT_CTX, T_LAT // TB), 0, 0)),
                  const((1, D)), const((D, D)), const((FG, FG)), const((FG, FG))],
        out_specs=[pl.BlockSpec((TB, D), lambda i: (i, 0)), pl.BlockSpec((TB, D), lambda i: (i, 0))],
        out_shape=[jax.ShapeDtypeStruct((N_TOK, D), BF16), jax.ShapeDtypeStruct((N_TOK, D), BF16)],
        compiler_params=_cparams(1),
        name="l1_in",
    )(x, mods, g, w, c256, s256)


def _l1_out_body(zc_t_ref, zs_t_ref, zc_q_ref, zs_q_ref, c256_ref, s256_ref, c1k_ref, s1k_ref,
                 x_ref, mod_ref, wo_ref, o_ref, f_ref):
    i = pl.program_id(0)

    @pl.when(i < NT_CTX)
    def _():
        f = (jnp.dot(c256_ref[...], zc_t_ref[...], preferred_element_type=F32)
             - jnp.dot(s256_ref[...], zs_t_ref[...], preferred_element_type=F32))
        f_ref[...] = f.astype(BF16)

    @pl.when(i >= NT_CTX)
    def _():
        f = (jnp.dot(c1k_ref[...], zc_q_ref[...], preferred_element_type=F32)
             - jnp.dot(s1k_ref[...], zs_q_ref[...], preferred_element_type=F32))
        f_ref[...] = f.astype(BF16)

    out = jnp.dot(f_ref[...], wo_ref[...], preferred_element_type=F32)
    o_ref[...] = x_ref[...] + mod_ref[0][:, 2 * D:3 * D] * out


def _l1_out(zc, zs, c256, s256, c1k, s1k, x, mods, w_out):
    const = lambda shape: pl.BlockSpec(shape, lambda i: (0,) * len(shape))
    tile_map = lambda i: (jnp.minimum(i, NT_CTX - 1), 0)
    seq_map = lambda i: (N_CTX // T_LAT + jnp.maximum(i - NT_CTX, 0) // (T_LAT // TB), 0)
    row_map = lambda i: (jnp.maximum(i - NT_CTX, 0) % (T_LAT // TB), 0)
    return pl.pallas_call(
        _l1_out_body,
        grid=(NT,),
        in_specs=[pl.BlockSpec((TB, D), tile_map), pl.BlockSpec((TB, D), tile_map),
                  pl.BlockSpec((T_LAT, D), seq_map), pl.BlockSpec((T_LAT, D), seq_map),
                  const((T_CTX, T_CTX)), const((T_CTX, T_CTX)),
                  pl.BlockSpec((TB, T_LAT), row_map), pl.BlockSpec((TB, T_LAT), row_map),
                  pl.BlockSpec((TB, D), lambda i: (i, 0)),
                  pl.BlockSpec((1, 1, 6 * D),
                               lambda i: (SUBLANES + _mod_row(i, NT_CTX, T_LAT // TB), 0, 0)),
                  const((D, D))],
        out_specs=pl.BlockSpec((TB, D), lambda i: (i, 0)),
        out_shape=jax.ShapeDtypeStruct((N_TOK, D), F32),
        scratch_shapes=[pltpu.VMEM((TB, D), BF16)],
        compiler_params=_cparams(1),
        name="l1_out",
    )(zc, zs, zc, zs, c256, s256, c1k, s1k, x, mods, w_out)


def _split_hi_lo(w):
    hi = w.astype(BF16)
    lo = (w - hi.astype(F32)).astype(BF16)
    return jnp.stack([hi, lo])


def kernel(x_prompt, x_sample, cache_k, cache_v, c, c_ctx, ada_w, ada_b, norm_mix, norm_ffn, a_w_in, a_q_norm, a_k_norm, a_sink, pool_w, pool_scale, a_w_out, f_w_in, f_w_out, router_g_w, router_g_b, router_e_w, router_e_b, moe_w1, moe_w3, moe_w2):
    xp = x_prompt.reshape(N_CTX, D)
    xs = x_sample.reshape(N_LAT, D)

    cond8 = jnp.zeros((SUBLANES, D), F32).at[0].set(c_ctx).at[1:1 + N_LAT_B].set(c)
    mods = _adaln(cond8, ada_w, ada_b).reshape(DEPTH * SUBLANES, 1, 6 * D)

    tabs = _rope_tables()
    lane = np.arange(LANES)
    bd = jnp.asarray((lane[:, None] // HEAD_DIM) == (lane[None, :] // HEAD_DIM), BF16)
    tri = jnp.asarray(np.arange(TB)[:, None] < np.arange(TB)[None, :], BF16)
    c256, s256 = _dft_tables(T_CTX)
    c1k, s1k = _dft_tables(T_LAT)

    def router_operands(l):
        w = jnp.concatenate([router_e_w[l], router_g_w[l]], axis=1).T
        w = jnp.pad(w, ((0, ROUTE_ROWS - w.shape[0]), (0, 0)))
        b = jnp.concatenate([router_e_b[l], router_g_b[l]])
        b = jnp.pad(b, (0, ROUTE_ROWS - b.shape[0]))
        return _split_hi_lo(w), jnp.broadcast_to(b[:, None], (ROUTE_ROWS, LANES))

    qg = jnp.tile(a_q_norm[0], LANES // HEAD_DIM)[None, :]
    kg = jnp.tile(a_k_norm[0], LANES // HEAD_DIM)[None, :]
    q, k, v, u = _l0_in(xp, xs, mods, norm_mix[0][None, :], a_w_in[0].astype(BF16), qg, kg, bd, tabs)
    sink_b = jnp.broadcast_to(a_sink[0][:, None], (N_HEADS, LANES))
    o_ctx = _ctx_attn(q, k, v, sink_b)
    ck = cache_k[:, 0].reshape(N_LAT_B, PAST, KV_W)
    cv = cache_v[:, 0].reshape(N_LAT_B, PAST, KV_W)
    o_lat = _lat_attn(q, k, v, ck, cv, sink_b)
    x1 = _l0_out(o_ctx, o_lat, u, xp, xs, mods, pool_w[0].astype(BF16), pool_scale[0][None, :],
                 a_w_out[0].astype(BF16))
    wr, br = router_operands(0)
    x2 = _moe(x1, mods, 0, norm_ffn[0][None, :], wr, br, tri, moe_w1, moe_w3, moe_w2, False)

    zc, zs = _l1_in(x2, mods, norm_mix[1][None, :], f_w_in[0].astype(BF16), c256, s256)
    x3 = _l1_out(zc, zs, c256, s256, c1k, s1k, x2, mods, f_w_out[0].astype(BF16))
    wr, br = router_operands(1)
    yp, ys = _moe(x3, mods, 1, norm_ffn[1][None, :], wr, br, tri, moe_w1, moe_w3, moe_w2, True)

    new_k = k[:N_CTX].reshape(N_CTX_B, 1, T_CTX, KV_W // HEAD_DIM, HEAD_DIM)
    new_v = v[:N_CTX].reshape(N_CTX_B, 1, T_CTX, KV_W // HEAD_DIM, HEAD_DIM)
    return (yp.reshape(N_CTX_B, T_CTX, D), ys.reshape(N_LAT_B, T_LAT, D), new_k, new_v)
```

```python
import functools

import numpy as np
import jax
import jax.numpy as jnp
from jax import lax
from jax.experimental import pallas as pl
from jax.experimental.pallas import tpu as pltpu

F32 = jnp.float32
BF16 = jnp.bfloat16
I32 = jnp.int32

D = 1024
DEPTH = 2
N_CTX_B, T_CTX = 16, 256
N_LAT_B, T_LAT = 2, 1024
N_CTX = N_CTX_B * T_CTX
N_LAT = N_LAT_B * T_LAT
N_TOK = N_CTX + N_LAT
PAST = 512
GRID_W = 64
HEAD_DIM = 64
N_HEADS = 8
ATTN_W = 512
KV_W = 128
POOL_W = 512
POOL_WINDOWS = (2, 4, 8, 16)
MIX_IN = ATTN_W + 2 * KV_W + POOL_W
WINDOW = 128
N_GROUPS = 4
PER_GROUP = 4
N_EXPERTS = 16
D_EXPERT = 512
ROPE_THETA = 10000.0
EPS = 1e-6
NEG = -1e30

LANES = 128
SUBLANES = 8
TB = 256
NT = N_TOK // TB
NT_CTX = N_CTX // TB
TB_MIX = 1024
TM = 256
NT_EXP = 2 * N_TOK // TM + N_EXPERTS
P_ROWS = NT_EXP * TM
ROUTE_ROWS = 32
SLABS = D // LANES

VMEM_LIMIT = 56 * 1024 * 1024


def _cparams(n_axes=1, vmem=None):
    return pltpu.CompilerParams(dimension_semantics=("arbitrary",) * n_axes,
                                vmem_limit_bytes=vmem)


def _modulate(x, g, shift, scale):
    ms = jnp.mean(x * x, axis=-1, keepdims=True)
    return (x * lax.rsqrt(ms + EPS) * g) * (1.0 + scale) + shift


def _mod_row(tile, tiles_ctx, tiles_per_lat):
    return (tile >= tiles_ctx).astype(I32) + (tile >= tiles_ctx + tiles_per_lat).astype(I32)


def _rope_tables():
    t = np.arange(T_LAT)
    row = (t // GRID_W).astype(np.float64)
    col = (t % GRID_W).astype(np.float64)
    nf = HEAD_DIM // 4
    freqs = ROPE_THETA ** (-np.arange(nf, dtype=np.float64) / nf)
    d = np.arange(HEAD_DIM)
    pos = np.where(d[None, :] < HEAD_DIM // 2, row[:, None], col[:, None])
    ang = pos * freqs[d % nf][None, :]
    first = (d % (HEAD_DIM // 2)) < nf
    cos = np.cos(ang)
    sin_a = np.where(first[None, :], -np.sin(ang), 0.0)
    sin_b = np.where(first[None, :], 0.0, np.sin(ang))
    ident = (np.ones((TB, HEAD_DIM)), np.zeros((TB, HEAD_DIM)), np.zeros((TB, HEAD_DIM)))
    out = []
    for tab, idt in zip((cos, sin_a, sin_b), ident):
        full = np.concatenate([tab, idt], axis=0)
        out.append(jnp.asarray(np.tile(full, (1, LANES // HEAD_DIM)), F32))
    return out


def _dft_tables(t):
    m = np.outer(np.arange(t), np.arange(t)) % t
    ang = 2.0 * np.pi * m / t
    s = 1.0 / np.sqrt(t)
    return jnp.asarray(np.cos(ang) * s, F32).astype(BF16), jnp.asarray(np.sin(ang) * s, F32).astype(BF16)


def _adaln_body(cond_ref, w_ref, b_ref, o_ref):
    c = cond_ref[...]
    s = (c * jax.nn.sigmoid(c)).astype(BF16)
    o_ref[0] = jnp.dot(s, w_ref[0].astype(BF16), preferred_element_type=F32) + b_ref[0]


def _adaln(cond8, ada_w, ada_b):
    tn = 1536
    return pl.pallas_call(
        _adaln_body,
        grid=(DEPTH, 6 * D // tn),
        in_specs=[pl.BlockSpec((SUBLANES, D), lambda l, j: (0, 0)),
                  pl.BlockSpec((1, D, tn), lambda l, j: (l, 0, j)),
                  pl.BlockSpec((1, 1, tn), lambda l, j: (l, 0, j))],
        out_specs=pl.BlockSpec((1, SUBLANES, tn), lambda l, j: (l, 0, j)),
        out_shape=jax.ShapeDtypeStruct((DEPTH, SUBLANES, 6 * D), F32),
        compiler_params=_cparams(2),
        name="adaln",
    )(cond8, ada_w, ada_b.reshape(DEPTH, 1, 6 * D))


def _l0_in_body(xp_ref, xs_ref, mod_ref, g_ref, w_ref, qg_ref, kg_ref, bd_ref,
                cos_ref, sa_ref, sb_ref, q_ref, k_ref, v_ref, u_ref):
    i = pl.program_id(0)
    x = jnp.where(i < NT_CTX, xp_ref[...], xs_ref[...])
    m = mod_ref[0]
    h = _modulate(x, g_ref[...], m[:, 0:D], m[:, D:2 * D])
    z = jnp.dot(h.astype(BF16), w_ref[...], preferred_element_type=F32)
    cos, sa, sb, bd = cos_ref[...], sa_ref[...], sb_ref[...], bd_ref[...]

    def head_norm_rope(zz, gain):
        ss = jnp.dot((zz * zz).astype(BF16), bd, preferred_element_type=F32)
        y = zz * lax.rsqrt(ss * (1.0 / HEAD_DIM) + EPS) * gain
        return (y * cos + pltpu.roll(y, LANES - 16, axis=1) * sa
                + pltpu.roll(y, 16, axis=1) * sb)

    for s in range(ATTN_W // LANES):
        qs = head_norm_rope(z[:, s * LANES:(s + 1) * LANES], qg_ref[...])
        q_ref[:, s * LANES:(s + 1) * LANES] = (qs * (HEAD_DIM ** -0.5)).astype(BF16)
    k_ref[...] = head_norm_rope(z[:, ATTN_W:ATTN_W + KV_W], kg_ref[...])
    v_ref[...] = z[:, ATTN_W + KV_W:ATTN_W + 2 * KV_W]
    u_ref[...] = z[:, ATTN_W + 2 * KV_W:MIX_IN]


def _l0_in(xp, xs, mods, g, w_in, qg, kg, bd, tabs):
    tab_spec = pl.BlockSpec(
        (TB, LANES), lambda i: (jnp.where(i < NT_CTX, T_LAT // TB, (i - NT_CTX) % (T_LAT // TB)), 0))
    const = lambda shape: pl.BlockSpec(shape, lambda i: (0,) * len(shape))
    return pl.pallas_call(
        _l0_in_body,
        grid=(NT,),
        in_specs=[pl.BlockSpec((TB, D), lambda i: (jnp.minimum(i, NT_CTX - 1), 0)),
                  pl.BlockSpec((TB, D), lambda i: (jnp.maximum(i - NT_CTX, 0), 0)),
                  pl.BlockSpec((1, 1, 6 * D), lambda i: (_mod_row(i, NT_CTX, T_LAT // TB), 0, 0)),
                  const((1, D)), const((D, MIX_IN)), const((1, LANES)), const((1, LANES)),
                  const((LANES, LANES)), tab_spec, tab_spec, tab_spec],
        out_specs=[pl.BlockSpec((TB, ATTN_W), lambda i: (i, 0)),
                   pl.BlockSpec((TB, KV_W), lambda i: (i, 0)),
                   pl.BlockSpec((TB, KV_W), lambda i: (i, 0)),
                   pl.BlockSpec((TB, POOL_W), lambda i: (i, 0))],
        out_shape=[jax.ShapeDtypeStruct((N_TOK, ATTN_W), BF16),
                   jax.ShapeDtypeStruct((N_TOK, KV_W), F32),
                   jax.ShapeDtypeStruct((N_TOK, KV_W), F32),
                   jax.ShapeDtypeStruct((N_TOK, POOL_W), F32)],
        compiler_params=_cparams(1),
        name="l0_in",
    )(xp, xs, mods, g, w_in, qg, kg, bd, *tabs)


def _head_halves(x):
    z = jnp.zeros_like(x)
    return jnp.concatenate([x, z], axis=1), jnp.concatenate([z, x], axis=1)


_NT_DIMS = (((1,), (1,)), ((), ()))


def _ctx_attn_body(q_ref, k_ref, v_ref, sink_ref, o_ref):
    k = k_ref[...].astype(BF16)
    v = v_ref[...].astype(BF16)
    lo = lax.broadcasted_iota(I32, (T_CTX, LANES), 1) < HEAD_DIM
    for j in range(KV_W // HEAD_DIM):
        kj = k[:, j * HEAD_DIM:(j + 1) * HEAD_DIM]
        vj = v[:, j * HEAD_DIM:(j + 1) * HEAD_DIM]
        k_halves = _head_halves(kj)
        vd = jnp.concatenate([vj, vj], axis=1)
        for s2 in range(2):
            s = 2 * j + s2
            qs = q_ref[:, s * LANES:(s + 1) * LANES]
            outs = []
            for half in range(2):
                hh = 2 * s + half
                sc = lax.dot_general(qs, k_halves[half], _NT_DIMS, preferred_element_type=F32)
                sk = sink_ref[hh:hh + 1, 0:1]
                mx = jnp.maximum(jnp.max(sc, axis=-1, keepdims=True), sk)
                p = jnp.exp(sc - mx)
                den = jnp.sum(p, axis=-1, keepdims=True) + jnp.exp(sk - mx)
                outs.append(jnp.dot((p / den).astype(BF16), vd, preferred_element_type=F32))
            o_ref[:, s * LANES:(s + 1) * LANES] = jnp.where(lo, outs[0], outs[1]).astype(BF16)


def _ctx_attn(q, k, v, sink_b):
    return pl.pallas_call(
        _ctx_attn_body,
        grid=(N_CTX_B,),
        in_specs=[pl.BlockSpec((T_CTX, ATTN_W), lambda b: (b, 0)),
                  pl.BlockSpec((T_CTX, KV_W), lambda b: (b, 0)),
                  pl.BlockSpec((T_CTX, KV_W), lambda b: (b, 0)),
                  pl.BlockSpec((SUBLANES, LANES), lambda b: (0, 0))],
        out_specs=pl.BlockSpec((T_CTX, ATTN_W), lambda b: (b, 0)),
        out_shape=jax.ShapeDtypeStruct((N_CTX, ATTN_W), BF16),
        compiler_params=_cparams(1),
        name="ctx_attn",
    )(q, k, v, sink_b)


QB = 128
SPAN = QB + 2 * WINDOW


def _lat_attn_body(q_ref, k_ref, v_ref, ck_ref, cv_ref, sink_ref, o_ref):
    qb = pl.program_id(1)
    start = qb * QB
    kws, vws = [], []
    for c in (-1, 0, 1):
        cs = pl.multiple_of(jnp.clip(start + c * QB, 0, T_LAT - QB), QB)
        kws.append(k_ref[pl.ds(cs, QB), :])
        vws.append(v_ref[pl.ds(cs, QB), :])
    kw = jnp.concatenate(kws, axis=0).astype(BF16)
    vw = jnp.concatenate(vws, axis=0).astype(BF16)
    ck = ck_ref[0].astype(BF16)
    cv = cv_ref[0].astype(BF16)
    qpos = start + lax.broadcasted_iota(I32, (QB, SPAN), 0)
    kpos = start - WINDOW + lax.broadcasted_iota(I32, (QB, SPAN), 1)
    valid = (kpos >= 0) & (kpos < T_LAT) & (jnp.abs(qpos - kpos) <= WINDOW)
    lo = lax.broadcasted_iota(I32, (QB, LANES), 1) < HEAD_DIM
    for j in range(KV_W // HEAD_DIM):
        sl = slice(j * HEAD_DIM, (j + 1) * HEAD_DIM)
        kw_halves = _head_halves(kw[:, sl])
        ck_halves = _head_halves(ck[:, sl])
        vwd = jnp.concatenate([vw[:, sl], vw[:, sl]], axis=1)
        cvd = jnp.concatenate([cv[:, sl], cv[:, sl]], axis=1)
        for s2 in range(2):
            s = 2 * j + s2
            qs = q_ref[:, s * LANES:(s + 1) * LANES]
            outs = []
            for half in range(2):
                hh = 2 * s + half
                s_win = lax.dot_general(qs, kw_halves[half], _NT_DIMS, preferred_element_type=F32)
                s_win = jnp.where(valid, s_win, NEG)
                s_ctx = lax.dot_general(qs, ck_halves[half], _NT_DIMS, preferred_element_type=F32)
                sk = sink_ref[hh:hh + 1, 0:1]
                mx = jnp.maximum(jnp.maximum(jnp.max(s_win, axis=-1, keepdims=True),
                                             jnp.max(s_ctx, axis=-1, keepdims=True)), sk)
                pw = jnp.exp(s_win - mx)
                pc = jnp.exp(s_ctx - mx)
                den = (jnp.sum(pw, axis=-1, keepdims=True) + jnp.sum(pc, axis=-1, keepdims=True)
                       + jnp.exp(sk - mx))
                outs.append(jnp.dot((pw / den).astype(BF16), vwd, preferred_element_type=F32)
                            + jnp.dot((pc / den).astype(BF16), cvd, preferred_element_type=F32))
            o_ref[:, s * LANES:(s + 1) * LANES] = jnp.where(lo, outs[0], outs[1]).astype(BF16)


def _lat_attn(q, k, v, ck, cv, sink_b):
    lat0 = N_CTX // T_LAT
    return pl.pallas_call(
        _lat_attn_body,
        grid=(N_LAT_B, T_LAT // QB),
        in_specs=[pl.BlockSpec((QB, ATTN_W), lambda b, i: (N_CTX // QB + b * (T_LAT // QB) + i, 0)),
                  pl.BlockSpec((T_LAT, KV_W), lambda b, i: (lat0 + b, 0)),
                  pl.BlockSpec((T_LAT, KV_W), lambda b, i: (lat0 + b, 0)),
                  pl.BlockSpec((1, PAST, KV_W), lambda b, i: (b, 0, 0)),
                  pl.BlockSpec((1, PAST, KV_W), lambda b, i: (b, 0, 0)),
                  pl.BlockSpec((SUBLANES, LANES), lambda b, i: (0, 0))],
        out_specs=pl.BlockSpec((QB, ATTN_W), lambda b, i: (b * (T_LAT // QB) + i, 0)),
        out_shape=jax.ShapeDtypeStruct((N_LAT, ATTN_W), BF16),
        compiler_params=_cparams(2),
        name="lat_attn",
    )(q, k, v, ck, cv, sink_b)


def _l0_out_body(oc_ref, ol_ref, u_ref, xp_ref, xs_ref, mod_ref, pw_ref, ps_ref, wo_ref, x1_ref):
    i = pl.program_id(0)
    is_ctx = i < N_CTX // TB_MIX
    o = jnp.where(is_ctx, oc_ref[...], ol_ref[...])
    x = jnp.where(is_ctx, xp_ref[...], xs_ref[...])
    tseq = jnp.where(is_ctx, T_CTX, T_LAT)
    pos = lax.broadcasted_iota(I32, (TB_MIX, LANES), 0) & (tseq - 1)
    ys = []
    for g, win in enumerate(POOL_WINDOWS):
        hw = win // 2
        ug = u_ref[:, g * LANES:(g + 1) * LANES]
        acc = ug
        for jj in range(-hw, hw):
            if jj == 0:
                continue
            sh = pltpu.roll(ug, (-jj) % TB_MIX, axis=0)
            ok = (pos + jj >= 0) if jj < 0 else (pos + jj < tseq)
            acc = acc + jnp.where(ok, sh, 0.0)
        cnt = (jnp.minimum(pos + hw, tseq) - jnp.maximum(pos - hw, 0)).astype(F32)
        pooled = acc / cnt - ug
        ys.append(jnp.dot(pooled.astype(BF16), pw_ref[g], preferred_element_type=F32))
    y = jnp.concatenate(ys, axis=1) * ps_ref[...]
    out = (jnp.dot(o, wo_ref[0:ATTN_W, :], preferred_element_type=F32)
           + jnp.dot(y.astype(BF16), wo_ref[ATTN_W:ATTN_W + POOL_W, :], preferred_element_type=F32))
    x1_ref[...] = x + mod_ref[0][:, 2 * D:3 * D] * out


def _l0_out(o_ctx, o_lat, u, xp, xs, mods, pool_w, pool_scale, w_out):
    ntc = N_CTX // TB_MIX
    const = lambda shape: pl.BlockSpec(shape, lambda i: (0,) * len(shape))
    ctx_map = lambda i: (jnp.minimum(i, ntc - 1), 0)
    lat_map = lambda i: (jnp.maximum(i - ntc, 0), 0)
    return pl.pallas_call(
        _l0_out_body,
        grid=(N_TOK // TB_MIX,),
        in_specs=[pl.BlockSpec((TB_MIX, ATTN_W), ctx_map),
                  pl.BlockSpec((TB_MIX, ATTN_W), lat_map),
                  pl.BlockSpec((TB_MIX, POOL_W), lambda i: (i, 0)),
                  pl.BlockSpec((TB_MIX, D), ctx_map),
                  pl.BlockSpec((TB_MIX, D), lat_map),
                  pl.BlockSpec((1, 1, 6 * D), lambda i: (_mod_row(i, ntc, 1), 0, 0)),
                  const((len(POOL_WINDOWS), LANES, LANES)), const((1, POOL_W)), const((D, D))],
        out_specs=pl.BlockSpec((TB_MIX, D), lambda i: (i, 0)),
        out_shape=jax.ShapeDtypeStruct((N_TOK, D), F32),
        compiler_params=_cparams(1, VMEM_LIMIT),
        name="l0_out",
    )(o_ctx, o_lat, u, xp, xs, mods, pool_w, pool_scale, w_out)


def _first_max(vals):
    best, idx = vals[0], jnp.zeros(vals[0].shape, I32)
    for r in range(1, len(vals)):
        better = vals[r] > best
        idx = jnp.where(better, r, idx)
        best = jnp.where(better, vals[r], best)
    return best, idx


def _softmax_rows(rows):
    mx = functools.reduce(jnp.maximum, rows)
    ex = [jnp.exp(r - mx) for r in rows]
    tot = functools.reduce(lambda a, b: a + b, ex)
    return [e / tot for e in ex]


def _route_body(x_ref, mod_ref, g_ref, wr_ref, br_ref, tri_ref,
                h3_ref, info_ref, tok_ref, cnt_ref, base_ref):
    i = pl.program_id(0)

    @pl.when(i == 0)
    def _():
        base_ref[...] = jnp.zeros_like(base_ref)

    m = mod_ref[0]
    h = _modulate(x_ref[...], g_ref[...], m[:, 3 * D:4 * D], m[:, 4 * D:5 * D])
    for s in range(SLABS):
        h3_ref[:, s, :] = h[:, s * LANES:(s + 1) * LANES]

    hh = h.astype(BF16)
    hl = (h - hh.astype(F32)).astype(BF16)
    wh, wl = wr_ref[0], wr_ref[1]
    lg = (lax.dot_general(wh, hh, _NT_DIMS, preferred_element_type=F32)
          + lax.dot_general(wl, hh, _NT_DIMS, preferred_element_type=F32)
          + lax.dot_general(wh, hl, _NT_DIMS, preferred_element_type=F32)) + br_ref[:, 0:1]

    pg = _softmax_rows([lg[N_EXPERTS + r:N_EXPERTS + r + 1] for r in range(N_GROUPS)])
    pg_top, gi = _first_max(pg)
    le = []
    for j in range(PER_GROUP):
        sel = lg[(N_GROUPS - 1) * PER_GROUP + j:(N_GROUPS - 1) * PER_GROUP + j + 1]
        for g in range(N_GROUPS - 2, -1, -1):
            sel = jnp.where(gi == g, lg[g * PER_GROUP + j:g * PER_GROUP + j + 1], sel)
        le.append(sel)
    pe = _softmax_rows(le)
    p1, i1 = _first_max(pe)
    p2, i2 = _first_max([jnp.where(i1 == j, -1.0, pe[j]) for j in range(PER_GROUP)])
    den = p1 + p2
    w1 = pg_top * p1 / den
    w2 = pg_top * p2 / den
    id1 = gi * PER_GROUP + i1
    id2 = gi * PER_GROUP + i2

    erow = lax.broadcasted_iota(I32, (N_EXPERTS, TB), 0)
    hit1 = erow == id1
    hit2 = erow == id2
    onehot = jnp.where(hit1, 1.0, jnp.where(hit2, 1.0, 0.0))
    before = jnp.dot(onehot.astype(BF16), tri_ref[...], preferred_element_type=F32)
    before = before + base_ref[:, 0:1]
    r1 = jnp.sum(jnp.where(hit1, before, 0.0), axis=0, keepdims=True)
    r2 = jnp.sum(jnp.where(hit2, before, 0.0), axis=0, keepdims=True)
    base_ref[...] = base_ref[...] + jnp.sum(onehot, axis=1, keepdims=True)
    cnt_ref[...] = base_ref[...]

    zero = jnp.zeros_like(w1)
    info = jnp.concatenate([id1.astype(F32), id2.astype(F32), r1, r2, w1, w2, zero, zero], axis=0)
    info_ref[...] = info
    tok_ref[...] = jnp.concatenate([info, jnp.zeros((LANES - SUBLANES, TB), F32)], axis=0).T


def _route(x, mods, layer, g, wr, br, tri):
    const = lambda shape: pl.BlockSpec(shape, lambda i: (0,) * len(shape))
    return pl.pallas_call(
        _route_body,
        grid=(NT,),
        in_specs=[pl.BlockSpec((TB, D), lambda i: (i, 0)),
                  pl.BlockSpec((1, 1, 6 * D),
                               lambda i: (layer * SUBLANES + _mod_row(i, NT_CTX, T_LAT // TB), 0, 0)),
                  const((1, D)), const((2, ROUTE_ROWS, D)), const((ROUTE_ROWS, LANES)),
                  const((TB, TB))],
        out_specs=[pl.BlockSpec((TB, SLABS, LANES), lambda i: (i, 0, 0)),
                   pl.BlockSpec((SUBLANES, TB), lambda i: (0, i)),
                   pl.BlockSpec((TB, LANES), lambda i: (i, 0)),
                   pl.BlockSpec((N_EXPERTS, LANES), lambda i: (0, 0))],
        out_shape=[jax.ShapeDtypeStruct((N_TOK, SLABS, LANES), F32),
                   jax.ShapeDtypeStruct((SUBLANES, N_TOK), F32),
                   jax.ShapeDtypeStruct((N_TOK, LANES), F32),
                   jax.ShapeDtypeStruct((N_EXPERTS, LANES), F32)],
        scratch_shapes=[pltpu.VMEM((N_EXPERTS, LANES), F32)],
        compiler_params=_cparams(1),
        name=f"route{layer}",
    )(x, mods, g, wr, br, tri)


def _plan(info, counts):
    ids = info[0:2].astype(I32)
    ranks = info[2:4].astype(I32)
    cnt = counts[:, 0].astype(I32)
    tiles = (cnt + TM - 1) // TM
    tend = jnp.cumsum(tiles)
    tstart = tend - tiles
    n_used = tend[-1]
    off = jnp.sum(jnp.where(ids[:, :, None] == jnp.arange(N_EXPERTS, dtype=I32), tstart * TM, 0), axis=-1)
    pos = (off + ranks).reshape(2 * N_TOK)
    n = jnp.arange(NT_EXP, dtype=I32)
    last = jnp.maximum(n_used - 1, 0)
    tile_blk = jnp.minimum(n, last)
    tile_e = jnp.minimum(jnp.sum((tile_blk[:, None] >= tend[None, :]).astype(I32), axis=1),
                         N_EXPERTS - 1)
    partial = jnp.any((n[:, None] == tend[None, :] - 1) & (tiles[None, :] > 0), axis=1)
    zflag = (partial | (n >= n_used)).astype(I32)
    return pos, tile_e, tile_blk, n_used.reshape(1), zflag


def _dispatch_body(pos_ref, zf_ref, h3_ref, xs3_hbm, zbuf, sem_z, sem):
    i = pl.program_id(0)

    @pl.when(i == 0)
    def _():
        zbuf[...] = jnp.zeros_like(zbuf)
        for n in range(NT_EXP):
            @pl.when(zf_ref[n] > 0)
            def _():
                pltpu.make_async_copy(zbuf, xs3_hbm.at[pl.ds(n * TM, TM)], sem_z).start()
        for n in range(NT_EXP):
            @pl.when(zf_ref[n] > 0)
            def _():
                pltpu.make_async_copy(zbuf, xs3_hbm.at[pl.ds(n * TM, TM)], sem_z).wait()

    base = i * TB

    def issue(r, carry):
        for k in range(2):
            pltpu.make_async_copy(h3_ref.at[r], xs3_hbm.at[pos_ref[k * N_TOK + base + r]], sem).start()
        return carry

    lax.fori_loop(0, TB, issue, 0, unroll=8)
    for k in range(2):
        pltpu.make_async_copy(h3_ref, xs3_hbm.at[pl.ds(0, TB)], sem).wait()


def _dispatch(pos, zflag, h3, layer):
    return pl.pallas_call(
        _dispatch_body,
        grid_spec=pltpu.PrefetchScalarGridSpec(
            num_scalar_prefetch=2, grid=(NT,),
            in_specs=[pl.BlockSpec((TB, SLABS, LANES), lambda i, p, z: (i, 0, 0))],
            out_specs=pl.BlockSpec(memory_space=pl.ANY),
            scratch_shapes=[pltpu.VMEM((TM, SLABS, LANES), F32),
                            pltpu.SemaphoreType.DMA, pltpu.SemaphoreType.DMA]),
        out_shape=jax.ShapeDtypeStruct((P_ROWS, SLABS, LANES), F32),
        compiler_params=_cparams(1),
        name=f"dispatch{layer}",
    )(pos, zflag, h3)


def _experts_body(te_ref, tb_ref, nu_ref, x3_ref, w1_ref, w3_ref, w2_ref, y3_ref):
    n = pl.program_id(0)

    @pl.when(n < nu_ref[0])
    def _():
        x = jnp.concatenate([x3_ref[:, s, :] for s in range(SLABS)], axis=1).astype(BF16)
        h1 = jnp.dot(x, w1_ref[0, 0].astype(BF16), preferred_element_type=F32)
        h3 = jnp.dot(x, w3_ref[0, 0].astype(BF16), preferred_element_type=F32)
        a = (h1 * jax.nn.sigmoid(h1)) * h3
        y = jnp.dot(a.astype(BF16), w2_ref[0, 0].astype(BF16), preferred_element_type=F32)
        for s in range(SLABS):
            y3_ref[:, s, :] = y[:, s * LANES:(s + 1) * LANES]

    @pl.when(n >= nu_ref[0])
    def _():
        y3_ref[...] = jnp.zeros_like(y3_ref)


def _experts(tile_e, tile_blk, n_used, xs3, w1, w3, w2, layer):
    wmap = lambda n, te, tb, nu: (layer, te[n], 0, 0)
    xmap = lambda n, te, tb, nu: (tb[n], 0, 0)
    ymap = lambda n, te, tb, nu: (n, 0, 0)
    return pl.pallas_call(
        _experts_body,
        grid_spec=pltpu.PrefetchScalarGridSpec(
            num_scalar_prefetch=3, grid=(NT_EXP,),
            in_specs=[pl.BlockSpec((TM, SLABS, LANES), xmap),
                      pl.BlockSpec((1, 1, D, D_EXPERT), wmap),
                      pl.BlockSpec((1, 1, D, D_EXPERT), wmap),
                      pl.BlockSpec((1, 1, D_EXPERT, D), wmap)],
            out_specs=pl.BlockSpec((TM, SLABS, LANES), ymap)),
        out_shape=jax.ShapeDtypeStruct((P_ROWS, SLABS, LANES), F32),
        compiler_params=_cparams(1, VMEM_LIMIT),
        name=f"experts{layer}",
    )(tile_e, tile_blk, n_used, xs3, w1, w3, w2)


def _combine_body(split_out, pos_ref, y3_hbm, x_ref, tok_ref, mod_ref, *rest):
    if split_out:
        oc_ref, ol_ref, buf, sem = rest
    else:
        o_ref, buf, sem = rest
    i = pl.program_id(0)
    base = i * TB

    def issue(r, carry):
        for k in range(2):
            pltpu.make_async_copy(y3_hbm.at[pos_ref[k * N_TOK + base + r]], buf.at[k, r], sem).start()
        return carry

    lax.fori_loop(0, TB, issue, 0, unroll=8)
    for k in range(2):
        pltpu.make_async_copy(y3_hbm.at[pl.ds(0, TB)], buf.at[k], sem).wait()

    w1 = tok_ref[:, 4:5]
    w2 = tok_ref[:, 5:6]
    gate = mod_ref[0][:, 5 * D:6 * D]
    for s in range(SLABS):
        sl = slice(s * LANES, (s + 1) * LANES)
        yy = w1 * buf[0, :, s, :] + w2 * buf[1, :, s, :]
        res = x_ref[:, sl] + gate[:, sl] * yy
        if split_out:
            @pl.when(i < NT_CTX)
            def _():
                oc_ref[:, sl] = res

            @pl.when(i >= NT_CTX)
            def _():
                ol_ref[:, sl] = res
        else:
            o_ref[:, sl] = res


def _combine(pos, y3, x, tok, mods, layer, split_out):
    if split_out:
        out_specs = [pl.BlockSpec((TB, D), lambda i, p: (jnp.minimum(i, NT_CTX - 1), 0)),
                     pl.BlockSpec((TB, D), lambda i, p: (jnp.maximum(i - NT_CTX, 0), 0))]
        out_shape = [jax.ShapeDtypeStruct((N_CTX, D), F32), jax.ShapeDtypeStruct((N_LAT, D), F32)]
    else:
        out_specs = pl.BlockSpec((TB, D), lambda i, p: (i, 0))
        out_shape = jax.ShapeDtypeStruct((N_TOK, D), F32)
    return pl.pallas_call(
        functools.partial(_combine_body, split_out),
        grid_spec=pltpu.PrefetchScalarGridSpec(
            num_scalar_prefetch=1, grid=(NT,),
            in_specs=[pl.BlockSpec(memory_space=pl.ANY),
                      pl.BlockSpec((TB, D), lambda i, p: (i, 0)),
                      pl.BlockSpec((TB, LANES), lambda i, p: (i, 0)),
                      pl.BlockSpec((1, 1, 6 * D),
                                   lambda i, p: (layer * SUBLANES + _mod_row(i, NT_CTX, T_LAT // TB), 0, 0))],
            out_specs=out_specs,
            scratch_shapes=[pltpu.VMEM((2, TB, SLABS, LANES), F32), pltpu.SemaphoreType.DMA]),
        out_shape=out_shape,
        compiler_params=_cparams(1),
        name=f"combine{layer}",
    )(pos, y3, x, tok, mods)


def _moe(x, mods, layer, g, wr, br, tri, w1, w3, w2, split_out):
    h3, info, tok, counts = _route(x, mods, layer, g, wr, br, tri)
    pos, tile_e, tile_blk, n_used, zflag = _plan(info, counts)
    xs3 = _dispatch(pos, zflag, h3, layer)
    y3 = _experts(tile_e, tile_blk, n_used, xs3, w1, w3, w2, layer)
    return _combine(pos, y3, x, tok, mods, layer, split_out)


FG = 256


def _l1_in_body(x_ref, mod_ref, g_ref, w_ref, c_ref, s_ref, zc_ref, zs_ref):
    m = mod_ref[0]
    h = _modulate(x_ref[...], g_ref[...], m[:, 0:D], m[:, D:2 * D])
    z = jnp.dot(h.astype(BF16), w_ref[...], preferred_element_type=F32).astype(BF16)
    for g in range(D // FG):
        zg = z[:, g * FG:(g + 1) * FG]
        zc_ref[:, g * FG:(g + 1) * FG] = jnp.dot(zg, c_ref[...], preferred_element_type=F32).astype(BF16)
        zs_ref[:, g * FG:(g + 1) * FG] = jnp.dot(zg, s_ref[...], preferred_element_type=F32).astype(BF16)


def _l1_in(x, mods, g, w, c256, s256):
    const = lambda shape: pl.BlockSpec(shape, lambda i: (0,) * len(shape))
    return pl.pallas_call(
        _l1_in_body,
        grid=(NT,),
        in_specs=[pl.BlockSpec((TB, D), lambda i: (i, 0)),
                  pl.BlockSpec((1, 1, 6 * D),
                               lambda i: (SUBLANES + _mod_row(i, NT_CTX, T_LAT // TB), 0, 0)),
                  const((1, D)), const((D, D)), const((FG, FG)), const((FG, FG))],
        out_specs=[pl.BlockSpec((TB, D), lambda i: (i, 0)), pl.BlockSpec((TB, D), lambda i: (i, 0))],
        out_shape=[jax.ShapeDtypeStruct((N_TOK, D), BF16), jax.ShapeDtypeStruct((N_TOK, D), BF16)],
        compiler_params=_cparams(1),
        name="l1_in",
    )(x, mods, g, w, c256, s256)


def _l1_out_body(zc_t_ref, zs_t_ref, zc_q_ref, zs_q_ref, c256_ref, s256_ref, c1k_ref, s1k_ref,
                 x_ref, mod_ref, wo_ref, o_ref, f_ref):
    i = pl.program_id(0)

    @pl.when(i < NT_CTX)
    def _():
        f = (jnp.dot(c256_ref[...], zc_t_ref[...], preferred_element_type=F32)
             - jnp.dot(s256_ref[...], zs_t_ref[...], preferred_element_type=F32))
        f_ref[...] = f.astype(BF16)

    @pl.when(i >= NT_CTX)
    def _():
        f = (jnp.dot(c1k_ref[...], zc_q_ref[...], preferred_element_type=F32)
             - jnp.dot(s1k_ref[...], zs_q_ref[...], preferred_element_type=F32))
        f_ref[...] = f.astype(BF16)

    out = jnp.dot(f_ref[...], wo_ref[...], preferred_element_type=F32)
    o_ref[...] = x_ref[...] + mod_ref[0][:, 2 * D:3 * D] * out


def _l1_out(zc, zs, c256, s256, c1k, s1k, x, mods, w_out):
    const = lambda shape: pl.BlockSpec(shape, lambda i: (0,) * len(shape))
    tile_map = lambda i: (jnp.minimum(i, NT_CTX - 1), 0)
    seq_map = lambda i: (N_CTX // T_LAT + jnp.maximum(i - NT_CTX, 0) // (T_LAT // TB), 0)
    row_map = lambda i: (jnp.maximum(i - NT_CTX, 0) % (T_LAT // TB), 0)
    return pl.pallas_call(
        _l1_out_body,
        grid=(NT,),
        in_specs=[pl.BlockSpec((TB, D), tile_map), pl.BlockSpec((TB, D), tile_map),
                  pl.BlockSpec((T_LAT, D), seq_map), pl.BlockSpec((T_LAT, D), seq_map),
                  const((T_CTX, T_CTX)), const((T_CTX, T_CTX)),
                  pl.BlockSpec((TB, T_LAT), row_map), pl.BlockSpec((TB, T_LAT), row_map),
                  pl.BlockSpec((TB, D), lambda i: (i, 0)),
                  pl.BlockSpec((1, 1, 6 * D),
                               lambda i: (SUBLANES + _mod_row(i, NT_CTX, T_LAT // TB), 0, 0)),
                  const((D, D))],
        out_specs=pl.BlockSpec((TB, D), lambda i: (i, 0)),
        out_shape=jax.ShapeDtypeStruct((N_TOK, D), F32),
        scratch_shapes=[pltpu.VMEM((TB, D), BF16)],
        compiler_params=_cparams(1),
        name="l1_out",
    )(zc, zs, zc, zs, c256, s256, c1k, s1k, x, mods, w_out)


def _split_hi_lo(w):
    hi = w.astype(BF16)
    lo = (w - hi.astype(F32)).astype(BF16)
    return jnp.stack([hi, lo])


def kernel(x_prompt, x_sample, cache_k, cache_v, c, c_ctx, ada_w, ada_b, norm_mix, norm_ffn, a_w_in, a_q_norm, a_k_norm, a_sink, pool_w, pool_scale, a_w_out, f_w_in, f_w_out, router_g_w, router_g_b, router_e_w, router_e_b, moe_w1, moe_w3, moe_w2):
    xp = x_prompt.reshape(N_CTX, D)
    xs = x_sample.reshape(N_LAT, D)

    cond8 = jnp.zeros((SUBLANES, D), F32).at[0].set(c_ctx).at[1:1 + N_LAT_B].set(c)
    mods = _adaln(cond8, ada_w, ada_b).reshape(DEPTH * SUBLANES, 1, 6 * D)

    tabs = _rope_tables()
    lane = np.arange(LANES)
    bd = jnp.asarray((lane[:, None] // HEAD_DIM) == (lane[None, :] // HEAD_DIM), BF16)
    tri = jnp.asarray(np.arange(TB)[:, None] < np.arange(TB)[None, :], BF16)
    c256, s256 = _dft_tables(T_CTX)
    c1k, s1k = _dft_tables(T_LAT)

    def router_operands(l):
        w = jnp.concatenate([router_e_w[l], router_g_w[l]], axis=1).T
        w = jnp.pad(w, ((0, ROUTE_ROWS - w.shape[0]), (0, 0)))
        b = jnp.concatenate([router_e_b[l], router_g_b[l]])
        b = jnp.pad(b, (0, ROUTE_ROWS - b.shape[0]))
        return _split_hi_lo(w), jnp.broadcast_to(b[:, None], (ROUTE_ROWS, LANES))

    qg = jnp.tile(a_q_norm[0], LANES // HEAD_DIM)[None, :]
    kg = jnp.tile(a_k_norm[0], LANES // HEAD_DIM)[None, :]
    q, k, v, u = _l0_in(xp, xs, mods, norm_mix[0][None, :], a_w_in[0].astype(BF16), qg, kg, bd, tabs)
    sink_b = jnp.broadcast_to(a_sink[0][:, None], (N_HEADS, LANES))
    o_ctx = _ctx_attn(q, k, v, sink_b)
    ck = cache_k[:, 0].reshape(N_LAT_B, PAST, KV_W)
    cv = cache_v[:, 0].reshape(N_LAT_B, PAST, KV_W)
    o_lat = _lat_attn(q, k, v, ck, cv, sink_b)
    x1 = _l0_out(o_ctx, o_lat, u, xp, xs, mods, pool_w[0].astype(BF16), pool_scale[0][None, :],
                 a_w_out[0].astype(BF16))
    wr, br = router_operands(0)
    x2 = _moe(x1, mods, 0, norm_ffn[0][None, :], wr, br, tri, moe_w1, moe_w3, moe_w2, False)

    zc, zs = _l1_in(x2, mods, norm_mix[1][None, :], f_w_in[0].astype(BF16), c256, s256)
    x3 = _l1_out(zc, zs, c256, s256, c1k, s1k, x2, mods, f_w_out[0].astype(BF16))
    wr, br = router_operands(1)
    yp, ys = _moe(x3, mods, 1, norm_ffn[1][None, :], wr, br, tri, moe_w1, moe_w3, moe_w2, True)

    new_k = k[:N_CTX].reshape(N_CTX_B, 1, T_CTX, KV_W // HEAD_DIM, HEAD_DIM)
    new_v = v[:N_CTX].reshape(N_CTX_B, 1, T_CTX, KV_W // HEAD_DIM, HEAD_DIM)
    return (yp.reshape(N_CTX_B, T_CTX, D), ys.reshape(N_LAT_B, T_LAT, D), new_k, new_v)
```

```python
import functools

import numpy as np
import jax
import jax.numpy as jnp
from jax import lax
from jax.experimental import pallas as pl
from jax.experimental.pallas import tpu as pltpu

F32 = jnp.float32
BF16 = jnp.bfloat16
I32 = jnp.int32

D = 1024
DEPTH = 2
N_CTX_B, T_CTX = 16, 256
N_LAT_B, T_LAT = 2, 1024
N_CTX = N_CTX_B * T_CTX
N_LAT = N_LAT_B * T_LAT
N_TOK = N_CTX + N_LAT
PAST = 512
GRID_W = 64
HEAD_DIM = 64
N_HEADS = 8
ATTN_W = 512
KV_W = 128
POOL_W = 512
POOL_WINDOWS = (2, 4, 8, 16)
MIX_IN = ATTN_W + 2 * KV_W + POOL_W
WINDOW = 128
N_GROUPS = 4
PER_GROUP = 4
N_EXPERTS = 16
D_EXPERT = 512
ROPE_THETA = 10000.0
EPS = 1e-6
NEG = -1e30

LANES = 128
SUBLANES = 8
TB = 256
NT = N_TOK // TB
NT_CTX = N_CTX // TB
TB_MIX = 1024
TM = 256
NT_EXP = 2 * N_TOK // TM + N_EXPERTS
P_ROWS = NT_EXP * TM
ROUTE_ROWS = 32
SLABS = D // LANES

VMEM_LIMIT = 56 * 1024 * 1024


def _cparams(n_axes=1, vmem=None):
    return pltpu.CompilerParams(dimension_semantics=("arbitrary",) * n_axes,
                                vmem_limit_bytes=vmem)


def _modulate(x, g, shift, scale):
    ms = jnp.mean(x * x, axis=-1, keepdims=True)
    return (x * lax.rsqrt(ms + EPS) * g) * (1.0 + scale) + shift


def _mod_row(tile, tiles_ctx, tiles_per_lat):
    return (tile >= tiles_ctx).astype(I32) + (tile >= tiles_ctx + tiles_per_lat).astype(I32)


def _rope_tables():
    t = np.arange(T_LAT)
    row = (t // GRID_W).astype(np.float64)
    col = (t % GRID_W).astype(np.float64)
    nf = HEAD_DIM // 4
    freqs = ROPE_THETA ** (-np.arange(nf, dtype=np.float64) / nf)
    d = np.arange(HEAD_DIM)
    pos = np.where(d[None, :] < HEAD_DIM // 2, row[:, None], col[:, None])
    ang = pos * freqs[d % nf][None, :]
    first = (d % (HEAD_DIM // 2)) < nf
    cos = np.cos(ang)
    sin_a = np.where(first[None, :], -np.sin(ang), 0.0)
    sin_b = np.where(first[None, :], 0.0, np.sin(ang))
    ident = (np.ones((TB, HEAD_DIM)), np.zeros((TB, HEAD_DIM)), np.zeros((TB, HEAD_DIM)))
    out = []
    for tab, idt in zip((cos, sin_a, sin_b), ident):
        full = np.concatenate([tab, idt], axis=0)
        out.append(jnp.asarray(np.tile(full, (1, LANES // HEAD_DIM)), F32))
    return out


def _dft_tables(t):
    m = np.outer(np.arange(t), np.arange(t)) % t
    ang = 2.0 * np.pi * m / t
    s = 1.0 / np.sqrt(t)
    return jnp.asarray(np.cos(ang) * s, F32).astype(BF16), jnp.asarray(np.sin(ang) * s, F32).astype(BF16)


def _adaln_body(cond_ref, w_ref, b_ref, o_ref):
    c = cond_ref[...]
    s = (c * jax.nn.sigmoid(c)).astype(BF16)
    o_ref[0] = jnp.dot(s, w_ref[0].astype(BF16), preferred_element_type=F32) + b_ref[0]


def _adaln(cond8, ada_w, ada_b):
    tn = 1536
    return pl.pallas_call(
        _adaln_body,
        grid=(DEPTH, 6 * D // tn),
        in_specs=[pl.BlockSpec((SUBLANES, D), lambda l, j: (0, 0)),
                  pl.BlockSpec((1, D, tn), lambda l, j: (l, 0, j)),
                  pl.BlockSpec((1, 1, tn), lambda l, j: (l, 0, j))],
        out_specs=pl.BlockSpec((1, SUBLANES, tn), lambda l, j: (l, 0, j)),
        out_shape=jax.ShapeDtypeStruct((DEPTH, SUBLANES, 6 * D), F32),
        compiler_params=_cparams(2),
        name="adaln",
    )(cond8, ada_w, ada_b.reshape(DEPTH, 1, 6 * D))


def _l0_in_body(xp_ref, xs_ref, mod_ref, g_ref, w_ref, qg_ref, kg_ref, bd_ref,
                cos_ref, sa_ref, sb_ref, q_ref, k_ref, v_ref, u_ref):
    i = pl.program_id(0)
    x = jnp.where(i < NT_CTX, xp_ref[...], xs_ref[...])
    m = mod_ref[0]
    h = _modulate(x, g_ref[...], m[:, 0:D], m[:, D:2 * D])
    z = jnp.dot(h.astype(BF16), w_ref[...], preferred_element_type=F32)
    cos, sa, sb, bd = cos_ref[...], sa_ref[...], sb_ref[...], bd_ref[...]

    def head_norm_rope(zz, gain):
        ss = jnp.dot((zz * zz).astype(BF16), bd, preferred_element_type=F32)
        y = zz * lax.rsqrt(ss * (1.0 / HEAD_DIM) + EPS) * gain
        return (y * cos + pltpu.roll(y, LANES - 16, axis=1) * sa
                + pltpu.roll(y, 16, axis=1) * sb)

    for s in range(ATTN_W // LANES):
        qs = head_norm_rope(z[:, s * LANES:(s + 1) * LANES], qg_ref[...])
        q_ref[:, s * LANES:(s + 1) * LANES] = (qs * (HEAD_DIM ** -0.5)).astype(BF16)
    k_ref[...] = head_norm_rope(z[:, ATTN_W:ATTN_W + KV_W], kg_ref[...])
    v_ref[...] = z[:, ATTN_W + KV_W:ATTN_W + 2 * KV_W]
    u_ref[...] = z[:, ATTN_W + 2 * KV_W:MIX_IN]


def _l0_in(xp, xs, mods, g, w_in, qg, kg, bd, tabs):
    tab_spec = pl.BlockSpec(
        (TB, LANES), lambda i: (jnp.where(i < NT_CTX, T_LAT // TB, (i - NT_CTX) % (T_LAT // TB)), 0))
    const = lambda shape: pl.BlockSpec(shape, lambda i: (0,) * len(shape))
    return pl.pallas_call(
        _l0_in_body,
        grid=(NT,),
        in_specs=[pl.BlockSpec((TB, D), lambda i: (jnp.minimum(i, NT_CTX - 1), 0)),
                  pl.BlockSpec((TB, D), lambda i: (jnp.maximum(i - NT_CTX, 0), 0)),
                  pl.BlockSpec((1, 1, 6 * D), lambda i: (_mod_row(i, NT_CTX, T_LAT // TB), 0, 0)),
                  const((1, D)), const((D, MIX_IN)), const((1, LANES)), const((1, LANES)),
                  const((LANES, LANES)), tab_spec, tab_spec, tab_spec],
        out_specs=[pl.BlockSpec((TB, ATTN_W), lambda i: (i, 0)),
                   pl.BlockSpec((TB, KV_W), lambda i: (i, 0)),
                   pl.BlockSpec((TB, KV_W), lambda i: (i, 0)),
                   pl.BlockSpec((TB, POOL_W), lambda i: (i, 0))],
        out_shape=[jax.ShapeDtypeStruct((N_TOK, ATTN_W), BF16),
                   jax.ShapeDtypeStruct((N_TOK, KV_W), F32),
                   jax.ShapeDtypeStruct((N_TOK, KV_W), F32),
                   jax.ShapeDtypeStruct((N_TOK, POOL_W), F32)],
        compiler_params=_cparams(1),
        name="l0_in",
    )(xp, xs, mods, g, w_in, qg, kg, bd, *tabs)


def _head_halves(x):
    z = jnp.zeros_like(x)
    return jnp.concatenate([x, z], axis=1), jnp.concatenate([z, x], axis=1)


_NT_DIMS = (((1,), (1,)), ((), ()))


def _ctx_attn_body(q_ref, k_ref, v_ref, sink_ref, o_ref):
    k = k_ref[...].astype(BF16)
    v = v_ref[...].astype(BF16)
    lo = lax.broadcasted_iota(I32, (T_CTX, LANES), 1) < HEAD_DIM
    for j in range(KV_W // HEAD_DIM):
        kj = k[:, j * HEAD_DIM:(j + 1) * HEAD_DIM]
        vj = v[:, j * HEAD_DIM:(j + 1) * HEAD_DIM]
        k_halves = _head_halves(kj)
        vd = jnp.concatenate([vj, vj], axis=1)
        for s2 in range(2):
            s = 2 * j + s2
            qs = q_ref[:, s * LANES:(s + 1) * LANES]
            outs = []
            for half in range(2):
                hh = 2 * s + half
                sc = lax.dot_general(qs, k_halves[half], _NT_DIMS, preferred_element_type=F32)
                sk = sink_ref[hh:hh + 1, 0:1]
                mx = jnp.maximum(jnp.max(sc, axis=-1, keepdims=True), sk)
                p = jnp.exp(sc - mx)
                den = jnp.sum(p, axis=-1, keepdims=True) + jnp.exp(sk - mx)
                outs.append(jnp.dot((p / den).astype(BF16), vd, preferred_element_type=F32))
            o_ref[:, s * LANES:(s + 1) * LANES] = jnp.where(lo, outs[0], outs[1]).astype(BF16)


def _ctx_attn(q, k, v, sink_b):
    return pl.pallas_call(
        _ctx_attn_body,
        grid=(N_CTX_B,),
        in_specs=[pl.BlockSpec((T_CTX, ATTN_W), lambda b: (b, 0)),
                  pl.BlockSpec((T_CTX, KV_W), lambda b: (b, 0)),
                  pl.BlockSpec((T_CTX, KV_W), lambda b: (b, 0)),
                  pl.BlockSpec((SUBLANES, LANES), lambda b: (0, 0))],
        out_specs=pl.BlockSpec((T_CTX, ATTN_W), lambda b: (b, 0)),
        out_shape=jax.ShapeDtypeStruct((N_CTX, ATTN_W), BF16),
        compiler_params=_cparams(1),
        name="ctx_attn",
    )(q, k, v, sink_b)


QB = 128
SPAN = QB + 2 * WINDOW


def _lat_attn_body(q_ref, k_ref, v_ref, ck_ref, cv_ref, sink_ref, o_ref):
    qb = pl.program_id(1)
    start = qb * QB
    kws, vws = [], []
    for c in (-1, 0, 1):
        cs = pl.multiple_of(jnp.clip(start + c * QB, 0, T_LAT - QB), QB)
        kws.append(k_ref[pl.ds(cs, QB), :])
        vws.append(v_ref[pl.ds(cs, QB), :])
    kw = jnp.concatenate(kws, axis=0).astype(BF16)
    vw = jnp.concatenate(vws, axis=0).astype(BF16)
    ck = ck_ref[0].astype(BF16)
    cv = cv_ref[0].astype(BF16)
    qpos = start + lax.broadcasted_iota(I32, (QB, SPAN), 0)
    kpos = start - WINDOW + lax.broadcasted_iota(I32, (QB, SPAN), 1)
    valid = (kpos >= 0) & (kpos < T_LAT) & (jnp.abs(qpos - kpos) <= WINDOW)
    lo = lax.broadcasted_iota(I32, (QB, LANES), 1) < HEAD_DIM
    for j in range(KV_W // HEAD_DIM):
        sl = slice(j * HEAD_DIM, (j + 1) * HEAD_DIM)
        kw_halves = _head_halves(kw[:, sl])
        ck_halves = _head_halves(ck[:, sl])
        vwd = jnp.concatenate([vw[:, sl], vw[:, sl]], axis=1)
        cvd = jnp.concatenate([cv[:, sl], cv[:, sl]], axis=1)
        for s2 in range(2):
            s = 2 * j + s2
            qs = q_ref[:, s * LANES:(s + 1) * LANES]
            outs = []
            for half in range(2):
                hh = 2 * s + half
                s_win = lax.dot_general(qs, kw_halves[half], _NT_DIMS, preferred_element_type=F32)
                s_win = jnp.where(valid, s_win, NEG)
                s_ctx = lax.dot_general(qs, ck_halves[half], _NT_DIMS, preferred_element_type=F32)
                sk = sink_ref[hh:hh + 1, 0:1]
                mx = jnp.maximum(jnp.maximum(jnp.max(s_win, axis=-1, keepdims=True),
                                             jnp.max(s_ctx, axis=-1, keepdims=True)), sk)
                pw = jnp.exp(s_win - mx)
                pc = jnp.exp(s_ctx - mx)
                den = (jnp.sum(pw, axis=-1, keepdims=True) + jnp.sum(pc, axis=-1, keepdims=True)
                       + jnp.exp(sk - mx))
                outs.append(jnp.dot((pw / den).astype(BF16), vwd, preferred_element_type=F32)
                            + jnp.dot((pc / den).astype(BF16), cvd, preferred_element_type=F32))
            o_ref[:, s * LANES:(s + 1) * LANES] = jnp.where(lo, outs[0], outs[1]).astype(BF16)


def _lat_attn(q, k, v, ck, cv, sink_b):
    lat0 = N_CTX // T_LAT
    return pl.pallas_call(
        _lat_attn_body,
        grid=(N_LAT_B, T_LAT // QB),
        in_specs=[pl.BlockSpec((QB, ATTN_W), lambda b, i: (N_CTX // QB + b * (T_LAT // QB) + i, 0)),
                  pl.BlockSpec((T_LAT, KV_W), lambda b, i: (lat0 + b, 0)),
                  pl.BlockSpec((T_LAT, KV_W), lambda b, i: (lat0 + b, 0)),
                  pl.BlockSpec((1, PAST, KV_W), lambda b, i: (b, 0, 0)),
                  pl.BlockSpec((1, PAST, KV_W), lambda b, i: (b, 0, 0)),
                  pl.BlockSpec((SUBLANES, LANES), lambda b, i: (0, 0))],
        out_specs=pl.BlockSpec((QB, ATTN_W), lambda b, i: (b * (T_LAT // QB) + i, 0)),
        out_shape=jax.ShapeDtypeStruct((N_LAT, ATTN_W), BF16),
        compiler_params=_cparams(2),
        name="lat_attn",
    )(q, k, v, ck, cv, sink_b)


def _l0_out_body(oc_ref, ol_ref, u_ref, xp_ref, xs_ref, mod_ref, pw_ref, ps_ref, wo_ref, x1_ref):
    i = pl.program_id(0)
    is_ctx = i < N_CTX // TB_MIX
    o = jnp.where(is_ctx, oc_ref[...], ol_ref[...])
    x = jnp.where(is_ctx, xp_ref[...], xs_ref[...])
    tseq = jnp.where(is_ctx, T_CTX, T_LAT)
    pos = lax.broadcasted_iota(I32, (TB_MIX, LANES), 0) & (tseq - 1)
    ys = []
    for g, win in enumerate(POOL_WINDOWS):
        hw = win // 2
        ug = u_ref[:, g * LANES:(g + 1) * LANES]
        acc = ug
        for jj in range(-hw, hw):
            if jj == 0:
                continue
            sh = pltpu.roll(ug, (-jj) % TB_MIX, axis=0)
            ok = (pos + jj >= 0) if jj < 0 else (pos + jj < tseq)
            acc = acc + jnp.where(ok, sh, 0.0)
        cnt = (jnp.minimum(pos + hw, tseq) - jnp.maximum(pos - hw, 0)).astype(F32)
        pooled = acc / cnt - ug
        ys.append(jnp.dot(pooled.astype(BF16), pw_ref[g], preferred_element_type=F32))
    y = jnp.concatenate(ys, axis=1) * ps_ref[...]
    out = (jnp.dot(o, wo_ref[0:ATTN_W, :], preferred_element_type=F32)
           + jnp.dot(y.astype(BF16), wo_ref[ATTN_W:ATTN_W + POOL_W, :], preferred_element_type=F32))
    x1_ref[...] = x + mod_ref[0][:, 2 * D:3 * D] * out


def _l0_out(o_ctx, o_lat, u, xp, xs, mods, pool_w, pool_scale, w_out):
    ntc = N_CTX // TB_MIX
    const = lambda shape: pl.BlockSpec(shape, lambda i: (0,) * len(shape))
    ctx_map = lambda i: (jnp.minimum(i, ntc - 1), 0)
    lat_map = lambda i: (jnp.maximum(i - ntc, 0), 0)
    return pl.pallas_call(
        _l0_out_body,
        grid=(N_TOK // TB_MIX,),
        in_specs=[pl.BlockSpec((TB_MIX, ATTN_W), ctx_map),
                  pl.BlockSpec((TB_MIX, ATTN_W), lat_map),
                  pl.BlockSpec((TB_MIX, POOL_W), lambda i: (i, 0)),
                  pl.BlockSpec((TB_MIX, D), ctx_map),
                  pl.BlockSpec((TB_MIX, D), lat_map),
                  pl.BlockSpec((1, 1, 6 * D), lambda i: (_mod_row(i, ntc, 1), 0, 0)),
                  const((len(POOL_WINDOWS), LANES, LANES)), const((1, POOL_W)), const((D, D))],
        out_specs=pl.BlockSpec((TB_MIX, D), lambda i: (i, 0)),
        out_shape=jax.ShapeDtypeStruct((N_TOK, D), F32),
        compiler_params=_cparams(1, VMEM_LIMIT),
        name="l0_out",
    )(o_ctx, o_lat, u, xp, xs, mods, pool_w, pool_scale, w_out)


def _first_max(vals):
    best, idx = vals[0], jnp.zeros(vals[0].shape, I32)
    for r in range(1, len(vals)):
        better = vals[r] > best
        idx = jnp.where(better, r, idx)
        best = jnp.where(better, vals[r], best)
    return best, idx


def _softmax_rows(rows):
    mx = functools.reduce(jnp.maximum, rows)
    ex = [jnp.exp(r - mx) for r in rows]
    tot = functools.reduce(lambda a, b: a + b, ex)
    return [e / tot for e in ex]


def _route_body(x_ref, mod_ref, g_ref, wr_ref, br_ref, tri_ref,
                h_ref, info_ref, tok_ref, cnt_ref, base_ref):
    i = pl.program_id(0)

    @pl.when(i == 0)
    def _():
        base_ref[...] = jnp.zeros_like(base_ref)

    m = mod_ref[0]
    h = _modulate(x_ref[...], g_ref[...], m[:, 3 * D:4 * D], m[:, 4 * D:5 * D])
    h_ref[...] = h

    hh = h.astype(BF16)
    hl = (h - hh.astype(F32)).astype(BF16)
    wh, wl = wr_ref[0], wr_ref[1]
    lg = (lax.dot_general(wh, hh, _NT_DIMS, preferred_element_type=F32)
          + lax.dot_general(wl, hh, _NT_DIMS, preferred_element_type=F32)
          + lax.dot_general(wh, hl, _NT_DIMS, preferred_element_type=F32)) + br_ref[:, 0:1]

    pg = _softmax_rows([lg[N_EXPERTS + r:N_EXPERTS + r + 1] for r in range(N_GROUPS)])
    pg_top, gi = _first_max(pg)
    le = []
    for j in range(PER_GROUP):
        sel = lg[(N_GROUPS - 1) * PER_GROUP + j:(N_GROUPS - 1) * PER_GROUP + j + 1]
        for g in range(N_GROUPS - 2, -1, -1):
            sel = jnp.where(gi == g, lg[g * PER_GROUP + j:g * PER_GROUP + j + 1], sel)
        le.append(sel)
    pe = _softmax_rows(le)
    p1, i1 = _first_max(pe)
    p2, i2 = _first_max([jnp.where(i1 == j, -1.0, pe[j]) for j in range(PER_GROUP)])
    den = p1 + p2
    w1 = pg_top * p1 / den
    w2 = pg_top * p2 / den
    id1 = gi * PER_GROUP + i1
    id2 = gi * PER_GROUP + i2

    erow = lax.broadcasted_iota(I32, (N_EXPERTS, TB), 0)
    hit1 = erow == id1
    hit2 = erow == id2
    onehot = jnp.where(hit1, 1.0, jnp.where(hit2, 1.0, 0.0))
    before = jnp.dot(onehot.astype(BF16), tri_ref[...], preferred_element_type=F32)
    before = before + base_ref[:, 0:1]
    r1 = jnp.sum(jnp.where(hit1, before, 0.0), axis=0, keepdims=True)
    r2 = jnp.sum(jnp.where(hit2, before, 0.0), axis=0, keepdims=True)
    base_ref[...] = base_ref[...] + jnp.sum(onehot, axis=1, keepdims=True)
    cnt_ref[...] = base_ref[...]

    zero = jnp.zeros_like(w1)
    info = jnp.concatenate([id1.astype(F32), id2.astype(F32), r1, r2, w1, w2, zero, zero], axis=0)
    info_ref[...] = info
    tok_ref[...] = jnp.concatenate([info, jnp.zeros((LANES - SUBLANES, TB), F32)], axis=0).T


def _route(x, mods, layer, g, wr, br, tri):
    const = lambda shape: pl.BlockSpec(shape, lambda i: (0,) * len(shape))
    return pl.pallas_call(
        _route_body,
        grid=(NT,),
        in_specs=[pl.BlockSpec((TB, D), lambda i: (i, 0)),
                  pl.BlockSpec((1, 1, 6 * D),
                               lambda i: (layer * SUBLANES + _mod_row(i, NT_CTX, T_LAT // TB), 0, 0)),
                  const((1, D)), const((2, ROUTE_ROWS, D)), const((ROUTE_ROWS, LANES)),
                  const((TB, TB))],
        out_specs=[pl.BlockSpec((TB, D), lambda i: (i, 0)),
                   pl.BlockSpec((SUBLANES, TB), lambda i: (0, i)),
                   pl.BlockSpec((TB, LANES), lambda i: (i, 0)),
                   pl.BlockSpec((N_EXPERTS, LANES), lambda i: (0, 0))],
        out_shape=[jax.ShapeDtypeStruct((N_TOK, D), F32),
                   jax.ShapeDtypeStruct((SUBLANES, N_TOK), F32),
                   jax.ShapeDtypeStruct((N_TOK, LANES), F32),
                   jax.ShapeDtypeStruct((N_EXPERTS, LANES), F32)],
        scratch_shapes=[pltpu.VMEM((N_EXPERTS, LANES), F32)],
        compiler_params=_cparams(1),
        name=f"route{layer}",
    )(x, mods, g, wr, br, tri)


def _plan(info, counts):
    ids = info[0:2].astype(I32)
    ranks = info[2:4].astype(I32)
    cnt = counts[:, 0].astype(I32)
    tiles = (cnt + TM - 1) // TM
    tend = jnp.cumsum(tiles)
    tstart = tend - tiles
    n_used = tend[-1]
    off = jnp.sum(jnp.where(ids[:, :, None] == jnp.arange(N_EXPERTS, dtype=I32), tstart * TM, 0), axis=-1)
    pos = (off + ranks).reshape(2 * N_TOK)
    n = jnp.arange(NT_EXP, dtype=I32)
    last = jnp.maximum(n_used - 1, 0)
    tile_blk = jnp.minimum(n, last)
    tile_e = jnp.minimum(jnp.sum((tile_blk[:, None] >= tend[None, :]).astype(I32), axis=1),
                         N_EXPERTS - 1)
    partial = jnp.any((n[:, None] == tend[None, :] - 1) & (tiles[None, :] > 0), axis=1)
    zflag = (partial | (n >= n_used)).astype(I32)
    return pos, tile_e, tile_blk, n_used.reshape(1), zflag


def _dispatch_body(pos_ref, zf_ref, h_ref, xs_hbm, zbuf, sem_z, sem):
    i = pl.program_id(0)

    @pl.when(i == 0)
    def _():
        zbuf[...] = jnp.zeros_like(zbuf)
        for n in range(NT_EXP):
            @pl.when(zf_ref[n] > 0)
            def _():
                pltpu.make_async_copy(zbuf, xs_hbm.at[pl.ds(n * TM, TM)], sem_z).start()
        for n in range(NT_EXP):
            @pl.when(zf_ref[n] > 0)
            def _():
                pltpu.make_async_copy(zbuf, xs_hbm.at[pl.ds(n * TM, TM)], sem_z).wait()

    base = i * TB

    def issue(r, carry):
        for k in range(2):
            pltpu.make_async_copy(h_ref.at[pl.ds(r, 1)],
                                  xs_hbm.at[pl.ds(pos_ref[k * N_TOK + base + r], 1)], sem).start()
        return carry

    lax.fori_loop(0, TB, issue, 0, unroll=8)
    for k in range(2):
        pltpu.make_async_copy(h_ref, xs_hbm.at[pl.ds(0, TB)], sem).wait()


def _dispatch(pos, zflag, h, layer):
    return pl.pallas_call(
        _dispatch_body,
        grid_spec=pltpu.PrefetchScalarGridSpec(
            num_scalar_prefetch=2, grid=(NT,),
            in_specs=[pl.BlockSpec((TB, D), lambda i, p, z: (i, 0))],
            out_specs=pl.BlockSpec(memory_space=pl.ANY),
            scratch_shapes=[pltpu.VMEM((TM, D), F32),
                            pltpu.SemaphoreType.DMA, pltpu.SemaphoreType.DMA]),
        out_shape=jax.ShapeDtypeStruct((P_ROWS, D), F32),
        compiler_params=_cparams(1),
        name=f"dispatch{layer}",
    )(pos, zflag, h)


def _experts_body(te_ref, tb_ref, nu_ref, x_ref, w1_ref, w3_ref, w2_ref, y_ref, wb1, wb3, wb2):
    n = pl.program_id(0)

    @pl.when((n == 0) | (te_ref[n] != te_ref[jnp.maximum(n - 1, 0)]))
    def _():
        wb1[...] = w1_ref[0, 0].astype(BF16)
        wb3[...] = w3_ref[0, 0].astype(BF16)
        wb2[...] = w2_ref[0, 0].astype(BF16)

    @pl.when(n < nu_ref[0])
    def _():
        x = x_ref[...].astype(BF16)
        h1 = jnp.dot(x, wb1[...], preferred_element_type=F32)
        h3 = jnp.dot(x, wb3[...], preferred_element_type=F32)
        a = (h1 * jax.nn.sigmoid(h1)) * h3
        y_ref[...] = jnp.dot(a.astype(BF16), wb2[...], preferred_element_type=F32)

    @pl.when(n >= nu_ref[0])
    def _():
        y_ref[...] = jnp.zeros_like(y_ref)


def _experts(tile_e, tile_blk, n_used, xs, w1, w3, w2, layer):
    wmap = lambda n, te, tb, nu: (layer, te[n], 0, 0)
    xmap = lambda n, te, tb, nu: (tb[n], 0)
    ymap = lambda n, te, tb, nu: (n, 0)
    return pl.pallas_call(
        _experts_body,
        grid_spec=pltpu.PrefetchScalarGridSpec(
            num_scalar_prefetch=3, grid=(NT_EXP,),
            in_specs=[pl.BlockSpec((TM, D), xmap),
                      pl.BlockSpec((1, 1, D, D_EXPERT), wmap),
                      pl.BlockSpec((1, 1, D, D_EXPERT), wmap),
                      pl.BlockSpec((1, 1, D_EXPERT, D), wmap)],
            out_specs=pl.BlockSpec((TM, D), ymap),
            scratch_shapes=[pltpu.VMEM((D, D_EXPERT), BF16), pltpu.VMEM((D, D_EXPERT), BF16),
                            pltpu.VMEM((D_EXPERT, D), BF16)]),
        out_shape=jax.ShapeDtypeStruct((P_ROWS, D), F32),
        compiler_params=_cparams(1, VMEM_LIMIT),
        name=f"experts{layer}",
    )(tile_e, tile_blk, n_used, xs, w1, w3, w2)


def _combine_body(split_out, pos_ref, y_hbm, x_ref, tok_ref, mod_ref, *rest):
    if split_out:
        oc_ref, ol_ref, buf, sem = rest
    else:
        o_ref, buf, sem = rest
    i = pl.program_id(0)
    base = i * TB

    def issue(r, carry):
        for k in range(2):
            pltpu.make_async_copy(y_hbm.at[pl.ds(pos_ref[k * N_TOK + base + r], 1)],
                                  buf.at[k, pl.ds(r, 1)], sem).start()
        return carry

    lax.fori_loop(0, TB, issue, 0, unroll=8)
    for k in range(2):
        pltpu.make_async_copy(y_hbm.at[pl.ds(0, TB)], buf.at[k], sem).wait()

    yy = tok_ref[:, 4:5] * buf[0] + tok_ref[:, 5:6] * buf[1]
    res = x_ref[...] + mod_ref[0][:, 5 * D:6 * D] * yy
    if split_out:
        @pl.when(i < NT_CTX)
        def _():
            oc_ref[...] = res

        @pl.when(i >= NT_CTX)
        def _():
            ol_ref[...] = res
    else:
        o_ref[...] = res


def _combine(pos, y, x, tok, mods, layer, split_out):
    if split_out:
        out_specs = [pl.BlockSpec((TB, D), lambda i, p: (jnp.minimum(i, NT_CTX - 1), 0)),
                     pl.BlockSpec((TB, D), lambda i, p: (jnp.maximum(i - NT_CTX, 0), 0))]
        out_shape = [jax.ShapeDtypeStruct((N_CTX, D), F32), jax.ShapeDtypeStruct((N_LAT, D), F32)]
    else:
        out_specs = pl.BlockSpec((TB, D), lambda i, p: (i, 0))
        out_shape = jax.ShapeDtypeStruct((N_TOK, D), F32)
    return pl.pallas_call(
        functools.partial(_combine_body, split_out),
        grid_spec=pltpu.PrefetchScalarGridSpec(
            num_scalar_prefetch=1, grid=(NT,),
            in_specs=[pl.BlockSpec(memory_space=pl.ANY),
                      pl.BlockSpec((TB, D), lambda i, p: (i, 0)),
                      pl.BlockSpec((TB, LANES), lambda i, p: (i, 0)),
                      pl.BlockSpec((1, 1, 6 * D),
                                   lambda i, p: (layer * SUBLANES + _mod_row(i, NT_CTX, T_LAT // TB), 0, 0))],
            out_specs=out_specs,
            scratch_shapes=[pltpu.VMEM((2, TB, D), F32), pltpu.SemaphoreType.DMA]),
        out_shape=out_shape,
        compiler_params=_cparams(1),
        name=f"combine{layer}",
    )(pos, y, x, tok, mods)


def _moe(x, mods, layer, g, wr, br, tri, w1, w3, w2, split_out):
    h, info, tok, counts = _route(x, mods, layer, g, wr, br, tri)
    pos, tile_e, tile_blk, n_used, zflag = _plan(info, counts)
    xs = _dispatch(pos, zflag, h, layer)
    y = _experts(tile_e, tile_blk, n_used, xs, w1, w3, w2, layer)
    return _combine(pos, y, x, tok, mods, layer, split_out)


FG = 256


def _l1_in_body(x_ref, mod_ref, g_ref, w_ref, c_ref, s_ref, zc_ref, zs_ref):
    m = mod_ref[0]
    h = _modulate(x_ref[...], g_ref[...], m[:, 0:D], m[:, D:2 * D])
    z = jnp.dot(h.astype(BF16), w_ref[...], preferred_element_type=F32).astype(BF16)
    for g in range(D // FG):
        zg = z[:, g * FG:(g + 1) * FG]
        zc_ref[:, g * FG:(g + 1) * FG] = jnp.dot(zg, c_ref[...], preferred_element_type=F32).astype(BF16)
        zs_ref[:, g * FG:(g + 1) * FG] = jnp.dot(zg, s_ref[...], preferred_element_type=F32).astype(BF16)


def _l1_in(x, mods, g, w, c256, s256):
    const = lambda shape: pl.BlockSpec(shape, lambda i: (0,) * len(shape))
    return pl.pallas_call(
        _l1_in_body,
        grid=(NT,),
        in_specs=[pl.BlockSpec((TB, D), lambda i: (i, 0)),
                  pl.BlockSpec((1, 1, 6 * D),
                               lambda i: (SUBLANES + _mod_row(i, NT_CTX, T_LAT // TB), 0, 0)),
                  const((1, D)), const((D, D)), const((FG, FG)), const((FG, FG))],
        out_specs=[pl.BlockSpec((TB, D), lambda i: (i, 0)), pl.BlockSpec((TB, D), lambda i: (i, 0))],
        out_shape=[jax.ShapeDtypeStruct((N_TOK, D), BF16), jax.ShapeDtypeStruct((N_TOK, D), BF16)],
        compiler_params=_cparams(1),
        name="l1_in",
    )(x, mods, g, w, c256, s256)


def _l1_out_body(zc_t_ref, zs_t_ref, zc_q_ref, zs_q_ref, c256_ref, s256_ref, c1k_ref, s1k_ref,
                 x_ref, mod_ref, wo_ref, o_ref, f_ref):
    i = pl.program_id(0)

    @pl.when(i < NT_CTX)
    def _():
        f = (jnp.dot(c256_ref[...], zc_t_ref[...], preferred_element_type=F32)
             - jnp.dot(s256_ref[...], zs_t_ref[...], preferred_element_type=F32))
        f_ref[...] = f.astype(BF16)

    @pl.when(i >= NT_CTX)
    def _():
        f = (jnp.dot(c1k_ref[...], zc_q_ref[...], preferred_element_type=F32)
             - jnp.dot(s1k_ref[...], zs_q_ref[...], preferred_element_type=F32))
        f_ref[...] = f.astype(BF16)

    out = jnp.dot(f_ref[...], wo_ref[...], preferred_element_type=F32)
    o_ref[...] = x_ref[...] + mod_ref[0][:, 2 * D:3 * D] * out


def _l1_out(zc, zs, c256, s256, c1k, s1k, x, mods, w_out):
    const = lambda shape: pl.BlockSpec(shape, lambda i: (0,) * len(shape))
    tile_map = lambda i: (jnp.minimum(i, NT_CTX - 1), 0)
    seq_map = lambda i: (N_CTX // T_LAT + jnp.maximum(i - NT_CTX, 0) // (T_LAT // TB), 0)
    row_map = lambda i: (jnp.maximum(i - NT_CTX, 0) % (T_LAT // TB), 0)
    return pl.pallas_call(
        _l1_out_body,
        grid=(NT,),
        in_specs=[pl.BlockSpec((TB, D), tile_map), pl.BlockSpec((TB, D), tile_map),
                  pl.BlockSpec((T_LAT, D), seq_map), pl.BlockSpec((T_LAT, D), seq_map),
                  const((T_CTX, T_CTX)), const((T_CTX, T_CTX)),
                  pl.BlockSpec((TB, T_LAT), row_map), pl.BlockSpec((TB, T_LAT), row_map),
                  pl.BlockSpec((TB, D), lambda i: (i, 0)),
                  pl.BlockSpec((1, 1, 6 * D),
                               lambda i: (SUBLANES + _mod_row(i, NT_CTX, T_LAT // TB), 0, 0)),
                  const((D, D))],
        out_specs=pl.BlockSpec((TB, D), lambda i: (i, 0)),
        out_shape=jax.ShapeDtypeStruct((N_TOK, D), F32),
        scratch_shapes=[pltpu.VMEM((TB, D), BF16)],
        compiler_params=_cparams(1),
        name="l1_out",
    )(zc, zs, zc, zs, c256, s256, c1k, s1k, x, mods, w_out)


def _split_hi_lo(w):
    hi = w.astype(BF16)
    lo = (w - hi.astype(F32)).astype(BF16)
    return jnp.stack([hi, lo])


def kernel(x_prompt, x_sample, cache_k, cache_v, c, c_ctx, ada_w, ada_b, norm_mix, norm_ffn, a_w_in, a_q_norm, a_k_norm, a_sink, pool_w, pool_scale, a_w_out, f_w_in, f_w_out, router_g_w, router_g_b, router_e_w, router_e_b, moe_w1, moe_w3, moe_w2):
    xp = x_prompt.reshape(N_CTX, D)
    xs = x_sample.reshape(N_LAT, D)

    cond8 = jnp.zeros((SUBLANES, D), F32).at[0].set(c_ctx).at[1:1 + N_LAT_B].set(c)
    mods = _adaln(cond8, ada_w, ada_b).reshape(DEPTH * SUBLANES, 1, 6 * D)

    tabs = _rope_tables()
    lane = np.arange(LANES)
    bd = jnp.asarray((lane[:, None] // HEAD_DIM) == (lane[None, :] // HEAD_DIM), BF16)
    tri = jnp.asarray(np.arange(TB)[:, None] < np.arange(TB)[None, :], BF16)
    c256, s256 = _dft_tables(T_CTX)
    c1k, s1k = _dft_tables(T_LAT)

    def router_operands(l):
        w = jnp.concatenate([router_e_w[l], router_g_w[l]], axis=1).T
        w = jnp.pad(w, ((0, ROUTE_ROWS - w.shape[0]), (0, 0)))
        b = jnp.concatenate([router_e_b[l], router_g_b[l]])
        b = jnp.pad(b, (0, ROUTE_ROWS - b.shape[0]))
        return _split_hi_lo(w), jnp.broadcast_to(b[:, None], (ROUTE_ROWS, LANES))

    qg = jnp.tile(a_q_norm[0], LANES // HEAD_DIM)[None, :]
    kg = jnp.tile(a_k_norm[0], LANES // HEAD_DIM)[None, :]
    q, k, v, u = _l0_in(xp, xs, mods, norm_mix[0][None, :], a_w_in[0].astype(BF16), qg, kg, bd, tabs)
    sink_b = jnp.broadcast_to(a_sink[0][:, None], (N_HEADS, LANES))
    o_ctx = _ctx_attn(q, k, v, sink_b)
    ck = cache_k[:, 0].reshape(N_LAT_B, PAST, KV_W)
    cv = cache_v[:, 0].reshape(N_LAT_B, PAST, KV_W)
    o_lat = _lat_attn(q, k, v, ck, cv, sink_b)
    x1 = _l0_out(o_ctx, o_lat, u, xp, xs, mods, pool_w[0].astype(BF16), pool_scale[0][None, :],
                 a_w_out[0].astype(BF16))
    wr, br = router_operands(0)
    x2 = _moe(x1, mods, 0, norm_ffn[0][None, :], wr, br, tri, moe_w1, moe_w3, moe_w2, False)

    zc, zs = _l1_in(x2, mods, norm_mix[1][None, :], f_w_in[0].astype(BF16), c256, s256)
    x3 = _l1_out(zc, zs, c256, s256, c1k, s1k, x2, mods, f_w_out[0].astype(BF16))
    wr, br = router_operands(1)
    yp, ys = _moe(x3, mods, 1, norm_ffn[1][None, :], wr, br, tri, moe_w1, moe_w3, moe_w2, True)

    new_k = k[:N_CTX].reshape(N_CTX_B, 1, T_CTX, KV_W // HEAD_DIM, HEAD_DIM)
    new_v = v[:N_CTX].reshape(N_CTX_B, 1, T_CTX, KV_W // HEAD_DIM, HEAD_DIM)
    return (yp.reshape(N_CTX_B, T_CTX, D), ys.reshape(N_LAT_B, T_LAT, D), new_k, new_v)
```

```python
import functools

import numpy as np
import jax
import jax.numpy as jnp
from jax import lax
from jax.experimental import pallas as pl
from jax.experimental.pallas import tpu as pltpu

F32 = jnp.float32
BF16 = jnp.bfloat16
I32 = jnp.int32

D = 1024
DEPTH = 2
N_CTX_B, T_CTX = 16, 256
N_LAT_B, T_LAT = 2, 1024
N_CTX = N_CTX_B * T_CTX
N_LAT = N_LAT_B * T_LAT
N_TOK = N_CTX + N_LAT
PAST = 512
GRID_W = 64
HEAD_DIM = 64
N_HEADS = 8
ATTN_W = 512
KV_W = 128
POOL_W = 512
POOL_WINDOWS = (2, 4, 8, 16)
MIX_IN = ATTN_W + 2 * KV_W + POOL_W
WINDOW = 128
N_GROUPS = 4
PER_GROUP = 4
N_EXPERTS = 16
D_EXPERT = 512
ROPE_THETA = 10000.0
EPS = 1e-6
NEG = -1e30

LANES = 128
SUBLANES = 8
TB = 256
NT = N_TOK // TB
NT_CTX = N_CTX // TB
TB_MIX = 1024
TM = 256
NT_EXP = 2 * N_TOK // TM + N_EXPERTS
P_ROWS = NT_EXP * TM
ROUTE_ROWS = 32
SLABS = D // LANES

VMEM_LIMIT = 56 * 1024 * 1024


def _cparams(n_axes=1, vmem=None):
    return pltpu.CompilerParams(dimension_semantics=("arbitrary",) * n_axes,
                                vmem_limit_bytes=vmem)


def _modulate(x, g, shift, scale):
    ms = jnp.mean(x * x, axis=-1, keepdims=True)
    return (x * lax.rsqrt(ms + EPS) * g) * (1.0 + scale) + shift


def _mod_row(tile, tiles_ctx, tiles_per_lat):
    return (tile >= tiles_ctx).astype(I32) + (tile >= tiles_ctx + tiles_per_lat).astype(I32)


def _rope_tables():
    t = np.arange(T_LAT)
    row = (t // GRID_W).astype(np.float64)
    col = (t % GRID_W).astype(np.float64)
    nf = HEAD_DIM // 4
    freqs = ROPE_THETA ** (-np.arange(nf, dtype=np.float64) / nf)
    d = np.arange(HEAD_DIM)
    pos = np.where(d[None, :] < HEAD_DIM // 2, row[:, None], col[:, None])
    ang = pos * freqs[d % nf][None, :]
    first = (d % (HEAD_DIM // 2)) < nf
    cos = np.cos(ang)
    sin_a = np.where(first[None, :], -np.sin(ang), 0.0)
    sin_b = np.where(first[None, :], 0.0, np.sin(ang))
    ident = (np.ones((TB, HEAD_DIM)), np.zeros((TB, HEAD_DIM)), np.zeros((TB, HEAD_DIM)))
    out = []
    for tab, idt in zip((cos, sin_a, sin_b), ident):
        full = np.concatenate([tab, idt], axis=0)
        out.append(jnp.asarray(np.tile(full, (1, LANES // HEAD_DIM)), F32))
    return out


def _dft_tables(t):
    m = np.outer(np.arange(t), np.arange(t)) % t
    ang = 2.0 * np.pi * m / t
    s = 1.0 / np.sqrt(t)
    return jnp.asarray(np.cos(ang) * s, F32).astype(BF16), jnp.asarray(np.sin(ang) * s, F32).astype(BF16)


def _adaln_body(cond_ref, w_ref, b_ref, o_ref):
    c = cond_ref[...]
    s = (c * jax.nn.sigmoid(c)).astype(BF16)
    o_ref[0] = jnp.dot(s, w_ref[0].astype(BF16), preferred_element_type=F32) + b_ref[0]


def _adaln(cond8, ada_w, ada_b):
    tn = 1536
    return pl.pallas_call(
        _adaln_body,
        grid=(DEPTH, 6 * D // tn),
        in_specs=[pl.BlockSpec((SUBLANES, D), lambda l, j: (0, 0)),
                  pl.BlockSpec((1, D, tn), lambda l, j: (l, 0, j)),
                  pl.BlockSpec((1, 1, tn), lambda l, j: (l, 0, j))],
        out_specs=pl.BlockSpec((1, SUBLANES, tn), lambda l, j: (l, 0, j)),
        out_shape=jax.ShapeDtypeStruct((DEPTH, SUBLANES, 6 * D), F32),
        compiler_params=_cparams(2),
        name="adaln",
    )(cond8, ada_w, ada_b.reshape(DEPTH, 1, 6 * D))


def _l0_in_body(xp_ref, xs_ref, mod_ref, g_ref, w_ref, qg_ref, kg_ref, bd_ref,
                cos_ref, sa_ref, sb_ref, q_ref, k_ref, v_ref, u_ref):
    i = pl.program_id(0)
    x = jnp.where(i < NT_CTX, xp_ref[...], xs_ref[...])
    m = mod_ref[0]
    h = _modulate(x, g_ref[...], m[:, 0:D], m[:, D:2 * D])
    z = jnp.dot(h.astype(BF16), w_ref[...], preferred_element_type=F32)
    cos, sa, sb, bd = cos_ref[...], sa_ref[...], sb_ref[...], bd_ref[...]

    def head_norm_rope(zz, gain):
        ss = jnp.dot((zz * zz).astype(BF16), bd, preferred_element_type=F32)
        y = zz * lax.rsqrt(ss * (1.0 / HEAD_DIM) + EPS) * gain
        return (y * cos + pltpu.roll(y, LANES - 16, axis=1) * sa
                + pltpu.roll(y, 16, axis=1) * sb)

    for s in range(ATTN_W // LANES):
        qs = head_norm_rope(z[:, s * LANES:(s + 1) * LANES], qg_ref[...])
        q_ref[:, s * LANES:(s + 1) * LANES] = (qs * (HEAD_DIM ** -0.5)).astype(BF16)
    k_ref[...] = head_norm_rope(z[:, ATTN_W:ATTN_W + KV_W], kg_ref[...])
    v_ref[...] = z[:, ATTN_W + KV_W:ATTN_W + 2 * KV_W]
    u_ref[...] = z[:, ATTN_W + 2 * KV_W:MIX_IN]


def _l0_in(xp, xs, mods, g, w_in, qg, kg, bd, tabs):
    tab_spec = pl.BlockSpec(
        (TB, LANES), lambda i: (jnp.where(i < NT_CTX, T_LAT // TB, (i - NT_CTX) % (T_LAT // TB)), 0))
    const = lambda shape: pl.BlockSpec(shape, lambda i: (0,) * len(shape))
    return pl.pallas_call(
        _l0_in_body,
        grid=(NT,),
        in_specs=[pl.BlockSpec((TB, D), lambda i: (jnp.minimum(i, NT_CTX - 1), 0)),
                  pl.BlockSpec((TB, D), lambda i: (jnp.maximum(i - NT_CTX, 0), 0)),
                  pl.BlockSpec((1, 1, 6 * D), lambda i: (_mod_row(i, NT_CTX, T_LAT // TB), 0, 0)),
                  const((1, D)), const((D, MIX_IN)), const((1, LANES)), const((1, LANES)),
                  const((LANES, LANES)), tab_spec, tab_spec, tab_spec],
        out_specs=[pl.BlockSpec((TB, ATTN_W), lambda i: (i, 0)),
                   pl.BlockSpec((TB, KV_W), lambda i: (i, 0)),
                   pl.BlockSpec((TB, KV_W), lambda i: (i, 0)),
                   pl.BlockSpec((TB, POOL_W), lambda i: (i, 0))],
        out_shape=[jax.ShapeDtypeStruct((N_TOK, ATTN_W), BF16),
                   jax.ShapeDtypeStruct((N_TOK, KV_W), F32),
                   jax.ShapeDtypeStruct((N_TOK, KV_W), F32),
                   jax.ShapeDtypeStruct((N_TOK, POOL_W), F32)],
        compiler_params=_cparams(1),
        name="l0_in",
    )(xp, xs, mods, g, w_in, qg, kg, bd, *tabs)


def _head_halves(x):
    z = jnp.zeros_like(x)
    return jnp.concatenate([x, z], axis=1), jnp.concatenate([z, x], axis=1)


_NT_DIMS = (((1,), (1,)), ((), ()))


def _sink_softmax(scores, sk):
    mx = sk
    for sc in scores:
        mx = jnp.maximum(mx, jnp.max(sc, axis=-1, keepdims=True))
    ps = [jnp.exp(sc - mx) for sc in scores]
    den = jnp.exp(sk - mx)
    for p in ps:
        den = den + jnp.sum(p, axis=-1, keepdims=True)
    inv = 1.0 / den
    return [(p * inv).astype(BF16) for p in ps]


def _sink_col(sink_ref, heads, rows):
    return jnp.concatenate([jnp.broadcast_to(sink_ref[h:h + 1, 0:1], (rows, 1)) for h in heads], axis=0)


def _ctx_attn_body(q_ref, k_ref, v_ref, sink_ref, o_ref):
    k = k_ref[...].astype(BF16)
    v = v_ref[...].astype(BF16)
    lo = lax.broadcasted_iota(I32, (T_CTX, LANES), 1) < HEAD_DIM
    for j in range(KV_W // HEAD_DIM):
        kj = k[:, j * HEAD_DIM:(j + 1) * HEAD_DIM]
        vj = v[:, j * HEAD_DIM:(j + 1) * HEAD_DIM]
        k_halves = _head_halves(kj)
        vd = jnp.concatenate([vj, vj], axis=1)
        q2 = jnp.concatenate([q_ref[:, (2 * j) * LANES:(2 * j + 1) * LANES],
                              q_ref[:, (2 * j + 1) * LANES:(2 * j + 2) * LANES]], axis=0)
        outs = []
        for half in range(2):
            sc = lax.dot_general(q2, k_halves[half], _NT_DIMS, preferred_element_type=F32)
            sk = _sink_col(sink_ref, (4 * j + half, 4 * j + 2 + half), T_CTX)
            (p,) = _sink_softmax([sc], sk)
            outs.append(jnp.dot(p, vd, preferred_element_type=F32))
        for s2 in range(2):
            rows = slice(s2 * T_CTX, (s2 + 1) * T_CTX)
            o_ref[:, (2 * j + s2) * LANES:(2 * j + s2 + 1) * LANES] = (
                jnp.where(lo, outs[0][rows], outs[1][rows]).astype(BF16))


def _ctx_attn(q, k, v, sink_b):
    return pl.pallas_call(
        _ctx_attn_body,
        grid=(N_CTX_B,),
        in_specs=[pl.BlockSpec((T_CTX, ATTN_W), lambda b: (b, 0)),
                  pl.BlockSpec((T_CTX, KV_W), lambda b: (b, 0)),
                  pl.BlockSpec((T_CTX, KV_W), lambda b: (b, 0)),
                  pl.BlockSpec((SUBLANES, LANES), lambda b: (0, 0))],
        out_specs=pl.BlockSpec((T_CTX, ATTN_W), lambda b: (b, 0)),
        out_shape=jax.ShapeDtypeStruct((N_CTX, ATTN_W), BF16),
        compiler_params=_cparams(1),
        name="ctx_attn",
    )(q, k, v, sink_b)


QB = 128
SPAN = QB + 2 * WINDOW


def _lat_attn_body(q_ref, k_ref, v_ref, ck_ref, cv_ref, sink_ref, o_ref):
    qb = pl.program_id(1)
    start = qb * QB
    kws, vws = [], []
    for c in (-1, 0, 1):
        cs = pl.multiple_of(jnp.clip(start + c * QB, 0, T_LAT - QB), QB)
        kws.append(k_ref[pl.ds(cs, QB), :])
        vws.append(v_ref[pl.ds(cs, QB), :])
    kw = jnp.concatenate(kws, axis=0).astype(BF16)
    vw = jnp.concatenate(vws, axis=0).astype(BF16)
    ck = ck_ref[0].astype(BF16)
    cv = cv_ref[0].astype(BF16)
    qpos = start + (lax.broadcasted_iota(I32, (2 * QB, SPAN), 0) & (QB - 1))
    kpos = start - WINDOW + lax.broadcasted_iota(I32, (2 * QB, SPAN), 1)
    valid = (kpos >= 0) & (kpos < T_LAT) & (jnp.abs(qpos - kpos) <= WINDOW)
    lo = lax.broadcasted_iota(I32, (QB, LANES), 1) < HEAD_DIM
    for j in range(KV_W // HEAD_DIM):
        sl = slice(j * HEAD_DIM, (j + 1) * HEAD_DIM)
        kw_halves = _head_halves(kw[:, sl])
        ck_halves = _head_halves(ck[:, sl])
        vwd = jnp.concatenate([vw[:, sl], vw[:, sl]], axis=1)
        cvd = jnp.concatenate([cv[:, sl], cv[:, sl]], axis=1)
        q2 = jnp.concatenate([q_ref[:, (2 * j) * LANES:(2 * j + 1) * LANES],
                              q_ref[:, (2 * j + 1) * LANES:(2 * j + 2) * LANES]], axis=0)
        outs = []
        for half in range(2):
            s_win = lax.dot_general(q2, kw_halves[half], _NT_DIMS, preferred_element_type=F32)
            s_win = jnp.where(valid, s_win, NEG)
            s_ctx = lax.dot_general(q2, ck_halves[half], _NT_DIMS, preferred_element_type=F32)
            sk = _sink_col(sink_ref, (4 * j + half, 4 * j + 2 + half), QB)
            pw, pc = _sink_softmax([s_win, s_ctx], sk)
            outs.append(jnp.dot(pw, vwd, preferred_element_type=F32)
                        + jnp.dot(pc, cvd, preferred_element_type=F32))
        for s2 in range(2):
            rows = slice(s2 * QB, (s2 + 1) * QB)
            o_ref[:, (2 * j + s2) * LANES:(2 * j + s2 + 1) * LANES] = (
                jnp.where(lo, outs[0][rows], outs[1][rows]).astype(BF16))


def _lat_attn(q, k, v, ck, cv, sink_b):
    lat0 = N_CTX // T_LAT
    return pl.pallas_call(
        _lat_attn_body,
        grid=(N_LAT_B, T_LAT // QB),
        in_specs=[pl.BlockSpec((QB, ATTN_W), lambda b, i: (N_CTX // QB + b * (T_LAT // QB) + i, 0)),
                  pl.BlockSpec((T_LAT, KV_W), lambda b, i: (lat0 + b, 0)),
                  pl.BlockSpec((T_LAT, KV_W), lambda b, i: (lat0 + b, 0)),
                  pl.BlockSpec((1, PAST, KV_W), lambda b, i: (b, 0, 0)),
                  pl.BlockSpec((1, PAST, KV_W), lambda b, i: (b, 0, 0)),
                  pl.BlockSpec((SUBLANES, LANES), lambda b, i: (0, 0))],
        out_specs=pl.BlockSpec((QB, ATTN_W), lambda b, i: (b * (T_LAT // QB) + i, 0)),
        out_shape=jax.ShapeDtypeStruct((N_LAT, ATTN_W), BF16),
        compiler_params=_cparams(2),
        name="lat_attn",
    )(q, k, v, ck, cv, sink_b)


def _l0_out_body(oc_ref, ol_ref, u_ref, xp_ref, xs_ref, mod_ref, pw_ref, ps_ref, wo_ref, x1_ref):
    i = pl.program_id(0)
    is_ctx = i < N_CTX // TB_MIX
    o = jnp.where(is_ctx, oc_ref[...], ol_ref[...])
    x = jnp.where(is_ctx, xp_ref[...], xs_ref[...])
    tseq = jnp.where(is_ctx, T_CTX, T_LAT)
    pos = lax.broadcasted_iota(I32, (TB_MIX, LANES), 0) & (tseq - 1)
    ys = []
    for g, win in enumerate(POOL_WINDOWS):
        hw = win // 2
        ug = u_ref[:, g * LANES:(g + 1) * LANES]
        acc = ug
        for jj in range(-hw, hw):
            if jj == 0:
                continue
            sh = pltpu.roll(ug, (-jj) % TB_MIX, axis=0)
            ok = (pos + jj >= 0) if jj < 0 else (pos + jj < tseq)
            acc = acc + jnp.where(ok, sh, 0.0)
        cnt = (jnp.minimum(pos + hw, tseq) - jnp.maximum(pos - hw, 0)).astype(F32)
        pooled = acc / cnt - ug
        ys.append(jnp.dot(pooled.astype(BF16), pw_ref[g], preferred_element_type=F32))
    y = jnp.concatenate(ys, axis=1) * ps_ref[...]
    out = (jnp.dot(o, wo_ref[0:ATTN_W, :], preferred_element_type=F32)
           + jnp.dot(y.astype(BF16), wo_ref[ATTN_W:ATTN_W + POOL_W, :], preferred_element_type=F32))
    x1_ref[...] = x + mod_ref[0][:, 2 * D:3 * D] * out


def _l0_out(o_ctx, o_lat, u, xp, xs, mods, pool_w, pool_scale, w_out):
    ntc = N_CTX // TB_MIX
    const = lambda shape: pl.BlockSpec(shape, lambda i: (0,) * len(shape))
    ctx_map = lambda i: (jnp.minimum(i, ntc - 1), 0)
    lat_map = lambda i: (jnp.maximum(i - ntc, 0), 0)
    return pl.pallas_call(
        _l0_out_body,
        grid=(N_TOK // TB_MIX,),
        in_specs=[pl.BlockSpec((TB_MIX, ATTN_W), ctx_map),
                  pl.BlockSpec((TB_MIX, ATTN_W), lat_map),
                  pl.BlockSpec((TB_MIX, POOL_W), lambda i: (i, 0)),
                  pl.BlockSpec((TB_MIX, D), ctx_map),
                  pl.BlockSpec((TB_MIX, D), lat_map),
                  pl.BlockSpec((1, 1, 6 * D), lambda i: (_mod_row(i, ntc, 1), 0, 0)),
                  const((len(POOL_WINDOWS), LANES, LANES)), const((1, POOL_W)), const((D, D))],
        out_specs=pl.BlockSpec((TB_MIX, D), lambda i: (i, 0)),
        out_shape=jax.ShapeDtypeStruct((N_TOK, D), F32),
        compiler_params=_cparams(1, VMEM_LIMIT),
        name="l0_out",
    )(o_ctx, o_lat, u, xp, xs, mods, pool_w, pool_scale, w_out)


def _first_max(vals):
    best, idx = vals[0], jnp.zeros(vals[0].shape, I32)
    for r in range(1, len(vals)):
        better = vals[r] > best
        idx = jnp.where(better, r, idx)
        best = jnp.where(better, vals[r], best)
    return best, idx


def _softmax_rows(rows):
    mx = functools.reduce(jnp.maximum, rows)
    ex = [jnp.exp(r - mx) for r in rows]
    tot = functools.reduce(lambda a, b: a + b, ex)
    return [e / tot for e in ex]


def _route_body(x_ref, mod_ref, g_ref, wr_ref, br_ref, tri_ref,
                h_ref, info_ref, tok_ref, cnt_ref, base_ref):
    i = pl.program_id(0)

    @pl.when(i == 0)
    def _():
        base_ref[...] = jnp.zeros_like(base_ref)

    m = mod_ref[0]
    h = _modulate(x_ref[...], g_ref[...], m[:, 3 * D:4 * D], m[:, 4 * D:5 * D])
    h_ref[...] = h

    hh = h.astype(BF16)
    hl = (h - hh.astype(F32)).astype(BF16)
    wh, wl = wr_ref[0], wr_ref[1]
    lg = (lax.dot_general(wh, hh, _NT_DIMS, preferred_element_type=F32)
          + lax.dot_general(wl, hh, _NT_DIMS, preferred_element_type=F32)
          + lax.dot_general(wh, hl, _NT_DIMS, preferred_element_type=F32)) + br_ref[:, 0:1]

    pg = _softmax_rows([lg[N_EXPERTS + r:N_EXPERTS + r + 1] for r in range(N_GROUPS)])
    pg_top, gi = _first_max(pg)
    le = []
    for j in range(PER_GROUP):
        sel = lg[(N_GROUPS - 1) * PER_GROUP + j:(N_GROUPS - 1) * PER_GROUP + j + 1]
        for g in range(N_GROUPS - 2, -1, -1):
            sel = jnp.where(gi == g, lg[g * PER_GROUP + j:g * PER_GROUP + j + 1], sel)
        le.append(sel)
    pe = _softmax_rows(le)
    p1, i1 = _first_max(pe)
    p2, i2 = _first_max([jnp.where(i1 == j, -1.0, pe[j]) for j in range(PER_GROUP)])
    den = p1 + p2
    w1 = pg_top * p1 / den
    w2 = pg_top * p2 / den
    id1 = gi * PER_GROUP + i1
    id2 = gi * PER_GROUP + i2

    erow = lax.broadcasted_iota(I32, (N_EXPERTS, TB), 0)
    hit1 = erow == id1
    hit2 = erow == id2
    onehot = jnp.where(hit1, 1.0, jnp.where(hit2, 1.0, 0.0))
    before = jnp.dot(onehot.astype(BF16), tri_ref[...], preferred_element_type=F32)
    before = before + base_ref[:, 0:1]
    r1 = jnp.sum(jnp.where(hit1, before, 0.0), axis=0, keepdims=True)
    r2 = jnp.sum(jnp.where(hit2, before, 0.0), axis=0, keepdims=True)
    base_ref[...] = base_ref[...] + jnp.sum(onehot, axis=1, keepdims=True)
    cnt_ref[...] = base_ref[...]

    zero = jnp.zeros_like(w1)
    info = jnp.concatenate([id1.astype(F32), id2.astype(F32), r1, r2, w1, w2, zero, zero], axis=0)
    info_ref[...] = info
    tok_ref[...] = jnp.concatenate([info, jnp.zeros((LANES - SUBLANES, TB), F32)], axis=0).T


def _route(x, mods, layer, g, wr, br, tri):
    const = lambda shape: pl.BlockSpec(shape, lambda i: (0,) * len(shape))
    return pl.pallas_call(
        _route_body,
        grid=(NT,),
        in_specs=[pl.BlockSpec((TB, D), lambda i: (i, 0)),
                  pl.BlockSpec((1, 1, 6 * D),
                               lambda i: (layer * SUBLANES + _mod_row(i, NT_CTX, T_LAT // TB), 0, 0)),
                  const((1, D)), const((2, ROUTE_ROWS, D)), const((ROUTE_ROWS, LANES)),
                  const((TB, TB))],
        out_specs=[pl.BlockSpec((TB, D), lambda i: (i, 0)),
                   pl.BlockSpec((SUBLANES, TB), lambda i: (0, i)),
                   pl.BlockSpec((TB, LANES), lambda i: (i, 0)),
                   pl.BlockSpec((N_EXPERTS, LANES), lambda i: (0, 0))],
        out_shape=[jax.ShapeDtypeStruct((N_TOK, D), F32),
                   jax.ShapeDtypeStruct((SUBLANES, N_TOK), F32),
                   jax.ShapeDtypeStruct((N_TOK, LANES), F32),
                   jax.ShapeDtypeStruct((N_EXPERTS, LANES), F32)],
        scratch_shapes=[pltpu.VMEM((N_EXPERTS, LANES), F32)],
        compiler_params=_cparams(1),
        name=f"route{layer}",
    )(x, mods, g, wr, br, tri)


def _plan(info, counts):
    ids = info[0:2].astype(I32)
    ranks = info[2:4].astype(I32)
    cnt = counts[:, 0].astype(I32)
    tiles = (cnt + TM - 1) // TM
    tend = jnp.cumsum(tiles)
    tstart = tend - tiles
    n_used = tend[-1]
    off = jnp.sum(jnp.where(ids[:, :, None] == jnp.arange(N_EXPERTS, dtype=I32), tstart * TM, 0), axis=-1)
    pos = (off + ranks).reshape(2 * N_TOK)
    n = jnp.arange(NT_EXP, dtype=I32)
    last = jnp.maximum(n_used - 1, 0)
    tile_blk = jnp.minimum(n, last)
    tile_e = jnp.minimum(jnp.sum((tile_blk[:, None] >= tend[None, :]).astype(I32), axis=1),
                         N_EXPERTS - 1)
    partial = jnp.any((n[:, None] == tend[None, :] - 1) & (tiles[None, :] > 0), axis=1)
    zflag = (partial | (n >= n_used)).astype(I32)
    return pos, tile_e, tile_blk, n_used.reshape(1), zflag


def _dispatch_body(pos_ref, zf_ref, h_ref, xs_hbm, zbuf, sem_z, sem):
    i = pl.program_id(0)

    @pl.when(i == 0)
    def _():
        zbuf[...] = jnp.zeros_like(zbuf)
        for n in range(NT_EXP):
            @pl.when(zf_ref[n] > 0)
            def _():
                pltpu.make_async_copy(zbuf, xs_hbm.at[pl.ds(n * TM, TM)], sem_z).start()
        for n in range(NT_EXP):
            @pl.when(zf_ref[n] > 0)
            def _():
                pltpu.make_async_copy(zbuf, xs_hbm.at[pl.ds(n * TM, TM)], sem_z).wait()

    base = i * TB

    def issue(r, carry):
        for k in range(2):
            pltpu.make_async_copy(h_ref.at[pl.ds(r, 1)],
                                  xs_hbm.at[pl.ds(pos_ref[k * N_TOK + base + r], 1)], sem).start()
        return carry

    lax.fori_loop(0, TB, issue, 0, unroll=8)
    for k in range(2):
        pltpu.make_async_copy(h_ref, xs_hbm.at[pl.ds(0, TB)], sem).wait()


def _dispatch(pos, zflag, h, layer):
    return pl.pallas_call(
        _dispatch_body,
        grid_spec=pltpu.PrefetchScalarGridSpec(
            num_scalar_prefetch=2, grid=(NT,),
            in_specs=[pl.BlockSpec((TB, D), lambda i, p, z: (i, 0))],
            out_specs=pl.BlockSpec(memory_space=pl.ANY),
            scratch_shapes=[pltpu.VMEM((TM, D), F32),
                            pltpu.SemaphoreType.DMA, pltpu.SemaphoreType.DMA]),
        out_shape=jax.ShapeDtypeStruct((P_ROWS, D), F32),
        compiler_params=_cparams(1),
        name=f"dispatch{layer}",
    )(pos, zflag, h)


def _experts_body(te_ref, tb_ref, nu_ref, x_ref, w1_ref, w3_ref, w2_ref, y_ref, wb1, wb3, wb2):
    n = pl.program_id(0)

    @pl.when((n == 0) | (te_ref[n] != te_ref[jnp.maximum(n - 1, 0)]))
    def _():
        wb1[...] = w1_ref[0, 0].astype(BF16)
        wb3[...] = w3_ref[0, 0].astype(BF16)
        wb2[...] = w2_ref[0, 0].astype(BF16)

    @pl.when(n < nu_ref[0])
    def _():
        x = x_ref[...].astype(BF16)
        h1 = jnp.dot(x, wb1[...], preferred_element_type=F32)
        h3 = jnp.dot(x, wb3[...], preferred_element_type=F32)
        a = (h1 * jax.nn.sigmoid(h1)) * h3
        y_ref[...] = jnp.dot(a.astype(BF16), wb2[...], preferred_element_type=F32)

    @pl.when(n >= nu_ref[0])
    def _():
        y_ref[...] = jnp.zeros_like(y_ref)


def _experts(tile_e, tile_blk, n_used, xs, w1, w3, w2, layer):
    wmap = lambda n, te, tb, nu: (layer, te[n], 0, 0)
    xmap = lambda n, te, tb, nu: (tb[n], 0)
    ymap = lambda n, te, tb, nu: (n, 0)
    return pl.pallas_call(
        _experts_body,
        grid_spec=pltpu.PrefetchScalarGridSpec(
            num_scalar_prefetch=3, grid=(NT_EXP,),
            in_specs=[pl.BlockSpec((TM, D), xmap),
                      pl.BlockSpec((1, 1, D, D_EXPERT), wmap),
                      pl.BlockSpec((1, 1, D, D_EXPERT), wmap),
                      pl.BlockSpec((1, 1, D_EXPERT, D), wmap)],
            out_specs=pl.BlockSpec((TM, D), ymap),
            scratch_shapes=[pltpu.VMEM((D, D_EXPERT), BF16), pltpu.VMEM((D, D_EXPERT), BF16),
                            pltpu.VMEM((D_EXPERT, D), BF16)]),
        out_shape=jax.ShapeDtypeStruct((P_ROWS, D), F32),
        compiler_params=_cparams(1, VMEM_LIMIT),
        name=f"experts{layer}",
    )(tile_e, tile_blk, n_used, xs, w1, w3, w2)


def _combine_body(split_out, pos_ref, y_hbm, x_ref, tok_ref, mod_ref, *rest):
    if split_out:
        oc_ref, ol_ref, buf, sem = rest
    else:
        o_ref, buf, sem = rest
    i = pl.program_id(0)
    base = i * TB

    def issue(r, carry):
        for k in range(2):
            pltpu.make_async_copy(y_hbm.at[pl.ds(pos_ref[k * N_TOK + base + r], 1)],
                                  buf.at[k, pl.ds(r, 1)], sem).start()
        return carry

    lax.fori_loop(0, TB, issue, 0, unroll=8)
    for k in range(2):
        pltpu.make_async_copy(y_hbm.at[pl.ds(0, TB)], buf.at[k], sem).wait()

    yy = tok_ref[:, 4:5] * buf[0] + tok_ref[:, 5:6] * buf[1]
    res = x_ref[...] + mod_ref[0][:, 5 * D:6 * D] * yy
    if split_out:
        @pl.when(i < NT_CTX)
        def _():
            oc_ref[...] = res

        @pl.when(i >= NT_CTX)
        def _():
            ol_ref[...] = res
    else:
        o_ref[...] = res


def _combine(pos, y, x, tok, mods, layer, split_out):
    if split_out:
        out_specs = [pl.BlockSpec((TB, D), lambda i, p: (jnp.minimum(i, NT_CTX - 1), 0)),
                     pl.BlockSpec((TB, D), lambda i, p: (jnp.maximum(i - NT_CTX, 0), 0))]
        out_shape = [jax.ShapeDtypeStruct((N_CTX, D), F32), jax.ShapeDtypeStruct((N_LAT, D), F32)]
    else:
        out_specs = pl.BlockSpec((TB, D), lambda i, p: (i, 0))
        out_shape = jax.ShapeDtypeStruct((N_TOK, D), F32)
    return pl.pallas_call(
        functools.partial(_combine_body, split_out),
        grid_spec=pltpu.PrefetchScalarGridSpec(
            num_scalar_prefetch=1, grid=(NT,),
            in_specs=[pl.BlockSpec(memory_space=pl.ANY),
                      pl.BlockSpec((TB, D), lambda i, p: (i, 0)),
                      pl.BlockSpec((TB, LANES), lambda i, p: (i, 0)),
                      pl.BlockSpec((1, 1, 6 * D),
                                   lambda i, p: (layer * SUBLANES + _mod_row(i, NT_CTX, T_LAT // TB), 0, 0))],
            out_specs=out_specs,
            scratch_shapes=[pltpu.VMEM((2, TB, D), F32), pltpu.SemaphoreType.DMA]),
        out_shape=out_shape,
        compiler_params=_cparams(1),
        name=f"combine{layer}",
    )(pos, y, x, tok, mods)


def _moe(x, mods, layer, g, wr, br, tri, w1, w3, w2, split_out):
    h, info, tok, counts = _route(x, mods, layer, g, wr, br, tri)
    pos, tile_e, tile_blk, n_used, zflag = _plan(info, counts)
    xs = _dispatch(pos, zflag, h, layer)
    y = _experts(tile_e, tile_blk, n_used, xs, w1, w3, w2, layer)
    return _combine(pos, y, x, tok, mods, layer, split_out)


FG = 256


def _l1_in_body(x_ref, mod_ref, g_ref, w_ref, c_ref, s_ref, zc_ref, zs_ref):
    m = mod_ref[0]
    h = _modulate(x_ref[...], g_ref[...], m[:, 0:D], m[:, D:2 * D])
    z = jnp.dot(h.astype(BF16), w_ref[...], preferred_element_type=F32).astype(BF16)
    for g in range(D // FG):
        zg = z[:, g * FG:(g + 1) * FG]
        zc_ref[:, g * FG:(g + 1) * FG] = jnp.dot(zg, c_ref[...], preferred_element_type=F32).astype(BF16)
        zs_ref[:, g * FG:(g + 1) * FG] = jnp.dot(zg, s_ref[...], preferred_element_type=F32).astype(BF16)


def _l1_in(x, mods, g, w, c256, s256):
    const = lambda shape: pl.BlockSpec(shape, lambda i: (0,) * len(shape))
    return pl.pallas_call(
        _l1_in_body,
        grid=(NT,),
        in_specs=[pl.BlockSpec((TB, D), lambda i: (i, 0)),
                  pl.BlockSpec((1, 1, 6 * D),
                               lambda i: (SUBLANES + _mod_row(i, NT_CTX, T_LAT // TB), 0, 0)),
                  const((1, D)), const((D, D)), const((FG, FG)), const((FG, FG))],
        out_specs=[pl.BlockSpec((TB, D), lambda i: (i, 0)), pl.BlockSpec((TB, D), lambda i: (i, 0))],
        out_shape=[jax.ShapeDtypeStruct((N_TOK, D), BF16), jax.ShapeDtypeStruct((N_TOK, D), BF16)],
        compiler_params=_cparams(1),
        name="l1_in",
    )(x, mods, g, w, c256, s256)


def _l1_out_body(zc_t_ref, zs_t_ref, zc_q_ref, zs_q_ref, c256_ref, s256_ref, c1k_ref, s1k_ref,
                 x_ref, mod_ref, wo_ref, o_ref, f_ref):
    i = pl.program_id(0)

    @pl.when(i < NT_CTX)
    def _():
        f = (jnp.dot(c256_ref[...], zc_t_ref[...], preferred_element_type=F32)
             - jnp.dot(s256_ref[...], zs_t_ref[...], preferred_element_type=F32))
        f_ref[...] = f.astype(BF16)

    @pl.when(i >= NT_CTX)
    def _():
        f = (jnp.dot(c1k_ref[...], zc_q_ref[...], preferred_element_type=F32)
             - jnp.dot(s1k_ref[...], zs_q_ref[...], preferred_element_type=F32))
        f_ref[...] = f.astype(BF16)

    out = jnp.dot(f_ref[...], wo_ref[...], preferred_element_type=F32)
    o_ref[...] = x_ref[...] + mod_ref[0][:, 2 * D:3 * D] * out


def _l1_out(zc, zs, c256, s256, c1k, s1k, x, mods, w_out):
    const = lambda shape: pl.BlockSpec(shape, lambda i: (0,) * len(shape))
    tile_map = lambda i: (jnp.minimum(i, NT_CTX - 1), 0)
    seq_map = lambda i: (N_CTX // T_LAT + jnp.maximum(i - NT_CTX, 0) // (T_LAT // TB), 0)
    row_map = lambda i: (jnp.maximum(i - NT_CTX, 0) % (T_LAT // TB), 0)
    return pl.pallas_call(
        _l1_out_body,
        grid=(NT,),
        in_specs=[pl.BlockSpec((TB, D), tile_map), pl.BlockSpec((TB, D), tile_map),
                  pl.BlockSpec((T_LAT, D), seq_map), pl.BlockSpec((T_LAT, D), seq_map),
                  const((T_CTX, T_CTX)), const((T_CTX, T_CTX)),
                  pl.BlockSpec((TB, T_LAT), row_map), pl.BlockSpec((TB, T_LAT), row_map),
                  pl.BlockSpec((TB, D), lambda i: (i, 0)),
                  pl.BlockSpec((1, 1, 6 * D),
                               lambda i: (SUBLANES + _mod_row(i, NT_CTX, T_LAT // TB), 0, 0)),
                  const((D, D))],
        out_specs=pl.BlockSpec((TB, D), lambda i: (i, 0)),
        out_shape=jax.ShapeDtypeStruct((N_TOK, D), F32),
        scratch_shapes=[pltpu.VMEM((TB, D), BF16)],
        compiler_params=_cparams(1),
        name="l1_out",
    )(zc, zs, zc, zs, c256, s256, c1k, s1k, x, mods, w_out)


def _split_hi_lo(w):
    hi = w.astype(BF16)
    lo = (w - hi.astype(F32)).astype(BF16)
    return jnp.stack([hi, lo])


def kernel(x_prompt, x_sample, cache_k, cache_v, c, c_ctx, ada_w, ada_b, norm_mix, norm_ffn, a_w_in, a_q_norm, a_k_norm, a_sink, pool_w, pool_scale, a_w_out, f_w_in, f_w_out, router_g_w, router_g_b, router_e_w, router_e_b, moe_w1, moe_w3, moe_w2):
    xp = x_prompt.reshape(N_CTX, D)
    xs = x_sample.reshape(N_LAT, D)

    cond8 = jnp.zeros((SUBLANES, D), F32).at[0].set(c_ctx).at[1:1 + N_LAT_B].set(c)
    mods = _adaln(cond8, ada_w, ada_b).reshape(DEPTH * SUBLANES, 1, 6 * D)

    tabs = _rope_tables()
    lane = np.arange(LANES)
    bd = jnp.asarray((lane[:, None] // HEAD_DIM) == (lane[None, :] // HEAD_DIM), BF16)
    tri = jnp.asarray(np.arange(TB)[:, None] < np.arange(TB)[None, :], BF16)
    c256, s256 = _dft_tables(T_CTX)
    c1k, s1k = _dft_tables(T_LAT)

    def router_operands(l):
        w = jnp.concatenate([router_e_w[l], router_g_w[l]], axis=1).T
        w = jnp.pad(w, ((0, ROUTE_ROWS - w.shape[0]), (0, 0)))
        b = jnp.concatenate([router_e_b[l], router_g_b[l]])
        b = jnp.pad(b, (0, ROUTE_ROWS - b.shape[0]))
        return _split_hi_lo(w), jnp.broadcast_to(b[:, None], (ROUTE_ROWS, LANES))

    qg = jnp.tile(a_q_norm[0], LANES // HEAD_DIM)[None, :]
    kg = jnp.tile(a_k_norm[0], LANES // HEAD_DIM)[None, :]
    q, k, v, u = _l0_in(xp, xs, mods, norm_mix[0][None, :], a_w_in[0].astype(BF16), qg, kg, bd, tabs)
    sink_b = jnp.broadcast_to(a_sink[0][:, None], (N_HEADS, LANES))
    o_ctx = _ctx_attn(q, k, v, sink_b)
    ck = cache_k[:, 0].reshape(N_LAT_B, PAST, KV_W)
    cv = cache_v[:, 0].reshape(N_LAT_B, PAST, KV_W)
    o_lat = _lat_attn(q, k, v, ck, cv, sink_b)
    x1 = _l0_out(o_ctx, o_lat, u, xp, xs, mods, pool_w[0].astype(BF16), pool_scale[0][None, :],
                 a_w_out[0].astype(BF16))
    wr, br = router_operands(0)
    x2 = _moe(x1, mods, 0, norm_ffn[0][None, :], wr, br, tri, moe_w1, moe_w3, moe_w2, False)

    zc, zs = _l1_in(x2, mods, norm_mix[1][None, :], f_w_in[0].astype(BF16), c256, s256)
    x3 = _l1_out(zc, zs, c256, s256, c1k, s1k, x2, mods, f_w_out[0].astype(BF16))
    wr, br = router_operands(1)
    yp, ys = _moe(x3, mods, 1, norm_ffn[1][None, :], wr, br, tri, moe_w1, moe_w3, moe_w2, True)

    new_k = k[:N_CTX].reshape(N_CTX_B, 1, T_CTX, KV_W // HEAD_DIM, HEAD_DIM)
    new_v = v[:N_CTX].reshape(N_CTX_B, 1, T_CTX, KV_W // HEAD_DIM, HEAD_DIM)
    return (yp.reshape(N_CTX_B, T_CTX, D), ys.reshape(N_LAT_B, T_LAT, D), new_k, new_v)
```

```python
import functools

import numpy as np
import jax
import jax.numpy as jnp
from jax import lax
from jax.experimental import pallas as pl
from jax.experimental.pallas import tpu as pltpu

F32 = jnp.float32
BF16 = jnp.bfloat16
I32 = jnp.int32

D = 1024
DEPTH = 2
N_CTX_B, T_CTX = 16, 256
N_LAT_B, T_LAT = 2, 1024
N_CTX = N_CTX_B * T_CTX
N_LAT = N_LAT_B * T_LAT
N_TOK = N_CTX + N_LAT
PAST = 512
GRID_W = 64
HEAD_DIM = 64
N_HEADS = 8
ATTN_W = 512
KV_W = 128
POOL_W = 512
POOL_WINDOWS = (2, 4, 8, 16)
MIX_IN = ATTN_W + 2 * KV_W + POOL_W
WINDOW = 128
N_GROUPS = 4
PER_GROUP = 4
N_EXPERTS = 16
D_EXPERT = 512
ROPE_THETA = 10000.0
EPS = 1e-6
NEG = -1e30

LANES = 128
SUBLANES = 8
TB = 256
NT = N_TOK // TB
NT_CTX = N_CTX // TB
TB_MIX = 1024
TM = 256
PAIRS = 6
N_CLASS = N_GROUPS * PAIRS
CLASS_ROWS = 32
NT_FFN = N_TOK // TM + N_CLASS
P_FFN = NT_FFN * TM
CH = 32
XEXT = D + LANES
ROUTE_ROWS = 32

VMEM_LIMIT = 56 * 1024 * 1024


def _cparams(n_axes=1, vmem=None):
    return pltpu.CompilerParams(dimension_semantics=("arbitrary",) * n_axes,
                                vmem_limit_bytes=vmem)


def _modulate(x, g, shift, scale):
    ms = jnp.mean(x * x, axis=-1, keepdims=True)
    return (x * lax.rsqrt(ms + EPS) * g) * (1.0 + scale) + shift


def _mod_row(tile, tiles_ctx, tiles_per_lat):
    return (tile >= tiles_ctx).astype(I32) + (tile >= tiles_ctx + tiles_per_lat).astype(I32)


def _rope_tables():
    t = np.arange(T_LAT)
    row = (t // GRID_W).astype(np.float64)
    col = (t % GRID_W).astype(np.float64)
    nf = HEAD_DIM // 4
    freqs = ROPE_THETA ** (-np.arange(nf, dtype=np.float64) / nf)
    d = np.arange(HEAD_DIM)
    pos = np.where(d[None, :] < HEAD_DIM // 2, row[:, None], col[:, None])
    ang = pos * freqs[d % nf][None, :]
    first = (d % (HEAD_DIM // 2)) < nf
    cos = np.cos(ang)
    sin_a = np.where(first[None, :], -np.sin(ang), 0.0)
    sin_b = np.where(first[None, :], 0.0, np.sin(ang))
    ident = (np.ones((TB, HEAD_DIM)), np.zeros((TB, HEAD_DIM)), np.zeros((TB, HEAD_DIM)))
    out = []
    for tab, idt in zip((cos, sin_a, sin_b), ident):
        full = np.concatenate([tab, idt], axis=0)
        out.append(jnp.asarray(np.tile(full, (1, LANES // HEAD_DIM)), F32))
    return out


def _dft_tables(t):
    m = np.outer(np.arange(t), np.arange(t)) % t
    ang = 2.0 * np.pi * m / t
    s = 1.0 / np.sqrt(t)
    return jnp.asarray(np.cos(ang) * s, F32).astype(BF16), jnp.asarray(np.sin(ang) * s, F32).astype(BF16)


def _adaln_body(cond_ref, w_ref, b_ref, o_ref):
    c = cond_ref[...]
    s = (c * jax.nn.sigmoid(c)).astype(BF16)
    o_ref[0] = jnp.dot(s, w_ref[0].astype(BF16), preferred_element_type=F32) + b_ref[0]


def _adaln(cond8, ada_w, ada_b):
    tn = 1536
    return pl.pallas_call(
        _adaln_body,
        grid=(DEPTH, 6 * D // tn),
        in_specs=[pl.BlockSpec((SUBLANES, D), lambda l, j: (0, 0)),
                  pl.BlockSpec((1, D, tn), lambda l, j: (l, 0, j)),
                  pl.BlockSpec((1, 1, tn), lambda l, j: (l, 0, j))],
        out_specs=pl.BlockSpec((1, SUBLANES, tn), lambda l, j: (l, 0, j)),
        out_shape=jax.ShapeDtypeStruct((DEPTH, SUBLANES, 6 * D), F32),
        compiler_params=_cparams(2),
        name="adaln",
    )(cond8, ada_w, ada_b.reshape(DEPTH, 1, 6 * D))


def _l0_in_body(xp_ref, xs_ref, mod_ref, g_ref, w_ref, qg_ref, kg_ref, bd_ref,
                cos_ref, sa_ref, sb_ref, q_ref, k_ref, v_ref, u_ref):
    i = pl.program_id(0)
    x = jnp.where(i < NT_CTX, xp_ref[...], xs_ref[...])
    m = mod_ref[0]
    h = _modulate(x, g_ref[...], m[:, 0:D], m[:, D:2 * D])
    z = jnp.dot(h.astype(BF16), w_ref[...], preferred_element_type=F32)
    cos, sa, sb, bd = cos_ref[...], sa_ref[...], sb_ref[...], bd_ref[...]

    def head_norm_rope(zz, gain):
        ss = jnp.dot((zz * zz).astype(BF16), bd, preferred_element_type=F32)
        y = zz * lax.rsqrt(ss * (1.0 / HEAD_DIM) + EPS) * gain
        return (y * cos + pltpu.roll(y, LANES - 16, axis=1) * sa
                + pltpu.roll(y, 16, axis=1) * sb)

    for s in range(ATTN_W // LANES):
        qs = head_norm_rope(z[:, s * LANES:(s + 1) * LANES], qg_ref[...])
        q_ref[:, s * LANES:(s + 1) * LANES] = (qs * (HEAD_DIM ** -0.5)).astype(BF16)
    k_ref[...] = head_norm_rope(z[:, ATTN_W:ATTN_W + KV_W], kg_ref[...])
    v_ref[...] = z[:, ATTN_W + KV_W:ATTN_W + 2 * KV_W]
    u_ref[...] = z[:, ATTN_W + 2 * KV_W:MIX_IN]


def _l0_in(xp, xs, mods, g, w_in, qg, kg, bd, tabs):
    tab_spec = pl.BlockSpec(
        (TB, LANES), lambda i: (jnp.where(i < NT_CTX, T_LAT // TB, (i - NT_CTX) % (T_LAT // TB)), 0))
    const = lambda shape: pl.BlockSpec(shape, lambda i: (0,) * len(shape))
    return pl.pallas_call(
        _l0_in_body,
        grid=(NT,),
        in_specs=[pl.BlockSpec((TB, D), lambda i: (jnp.minimum(i, NT_CTX - 1), 0)),
                  pl.BlockSpec((TB, D), lambda i: (jnp.maximum(i - NT_CTX, 0), 0)),
                  pl.BlockSpec((1, 1, 6 * D), lambda i: (_mod_row(i, NT_CTX, T_LAT // TB), 0, 0)),
                  const((1, D)), const((D, MIX_IN)), const((1, LANES)), const((1, LANES)),
                  const((LANES, LANES)), tab_spec, tab_spec, tab_spec],
        out_specs=[pl.BlockSpec((TB, ATTN_W), lambda i: (i, 0)),
                   pl.BlockSpec((TB, KV_W), lambda i: (i, 0)),
                   pl.BlockSpec((TB, KV_W), lambda i: (i, 0)),
                   pl.BlockSpec((TB, POOL_W), lambda i: (i, 0))],
        out_shape=[jax.ShapeDtypeStruct((N_TOK, ATTN_W), BF16),
                   jax.ShapeDtypeStruct((N_TOK, KV_W), F32),
                   jax.ShapeDtypeStruct((N_TOK, KV_W), F32),
                   jax.ShapeDtypeStruct((N_TOK, POOL_W), F32)],
        compiler_params=_cparams(1),
        name="l0_in",
    )(xp, xs, mods, g, w_in, qg, kg, bd, *tabs)


def _head_halves(x):
    z = jnp.zeros_like(x)
    return jnp.concatenate([x, z], axis=1), jnp.concatenate([z, x], axis=1)


_NT_DIMS = (((1,), (1,)), ((), ()))


def _sink_softmax(scores, sk):
    mx = sk
    for sc in scores:
        mx = jnp.maximum(mx, jnp.max(sc, axis=-1, keepdims=True))
    ps = [jnp.exp(sc - mx) for sc in scores]
    den = jnp.exp(sk - mx)
    for p in ps:
        den = den + jnp.sum(p, axis=-1, keepdims=True)
    inv = 1.0 / den
    return [(p * inv).astype(BF16) for p in ps]


def _sink_col(sink_ref, heads, rows):
    return jnp.concatenate([jnp.broadcast_to(sink_ref[h:h + 1, 0:1], (rows, 1)) for h in heads], axis=0)


def _ctx_attn_body(q_ref, k_ref, v_ref, sink_ref, o_ref):
    k = k_ref[...].astype(BF16)
    v = v_ref[...].astype(BF16)
    lo = lax.broadcasted_iota(I32, (T_CTX, LANES), 1) < HEAD_DIM
    for j in range(KV_W // HEAD_DIM):
        kj = k[:, j * HEAD_DIM:(j + 1) * HEAD_DIM]
        vj = v[:, j * HEAD_DIM:(j + 1) * HEAD_DIM]
        k_halves = _head_halves(kj)
        vd = jnp.concatenate([vj, vj], axis=1)
        q2 = jnp.concatenate([q_ref[:, (2 * j) * LANES:(2 * j + 1) * LANES],
                              q_ref[:, (2 * j + 1) * LANES:(2 * j + 2) * LANES]], axis=0)
        outs = []
        for half in range(2):
            sc = lax.dot_general(q2, k_halves[half], _NT_DIMS, preferred_element_type=F32)
            sk = _sink_col(sink_ref, (4 * j + half, 4 * j + 2 + half), T_CTX)
            (p,) = _sink_softmax([sc], sk)
            outs.append(jnp.dot(p, vd, preferred_element_type=F32))
        for s2 in range(2):
            rows = slice(s2 * T_CTX, (s2 + 1) * T_CTX)
            o_ref[:, (2 * j + s2) * LANES:(2 * j + s2 + 1) * LANES] = (
                jnp.where(lo, outs[0][rows], outs[1][rows]).astype(BF16))


def _ctx_attn(q, k, v, sink_b):
    return pl.pallas_call(
        _ctx_attn_body,
        grid=(N_CTX_B,),
        in_specs=[pl.BlockSpec((T_CTX, ATTN_W), lambda b: (b, 0)),
                  pl.BlockSpec((T_CTX, KV_W), lambda b: (b, 0)),
                  pl.BlockSpec((T_CTX, KV_W), lambda b: (b, 0)),
                  pl.BlockSpec((SUBLANES, LANES), lambda b: (0, 0))],
        out_specs=pl.BlockSpec((T_CTX, ATTN_W), lambda b: (b, 0)),
        out_shape=jax.ShapeDtypeStruct((N_CTX, ATTN_W), BF16),
        compiler_params=_cparams(1),
        name="ctx_attn",
    )(q, k, v, sink_b)


QB = 128
SPAN = QB + 2 * WINDOW


def _lat_attn_body(q_ref, k_ref, v_ref, ck_ref, cv_ref, sink_ref, o_ref):
    qb = pl.program_id(1)
    start = qb * QB
    kws, vws = [], []
    for c in (-1, 0, 1):
        cs = pl.multiple_of(jnp.clip(start + c * QB, 0, T_LAT - QB), QB)
        kws.append(k_ref[pl.ds(cs, QB), :])
        vws.append(v_ref[pl.ds(cs, QB), :])
    kw = jnp.concatenate(kws, axis=0).astype(BF16)
    vw = jnp.concatenate(vws, axis=0).astype(BF16)
    ck = ck_ref[0].astype(BF16)
    cv = cv_ref[0].astype(BF16)
    qpos = start + (lax.broadcasted_iota(I32, (2 * QB, SPAN), 0) & (QB - 1))
    kpos = start - WINDOW + lax.broadcasted_iota(I32, (2 * QB, SPAN), 1)
    valid = (kpos >= 0) & (kpos < T_LAT) & (jnp.abs(qpos - kpos) <= WINDOW)
    lo = lax.broadcasted_iota(I32, (QB, LANES), 1) < HEAD_DIM
    for j in range(KV_W // HEAD_DIM):
        sl = slice(j * HEAD_DIM, (j + 1) * HEAD_DIM)
        kw_halves = _head_halves(kw[:, sl])
        ck_halves = _head_halves(ck[:, sl])
        vwd = jnp.concatenate([vw[:, sl], vw[:, sl]], axis=1)
        cvd = jnp.concatenate([cv[:, sl], cv[:, sl]], axis=1)
        q2 = jnp.concatenate([q_ref[:, (2 * j) * LANES:(2 * j + 1) * LANES],
                              q_ref[:, (2 * j + 1) * LANES:(2 * j + 2) * LANES]], axis=0)
        outs = []
        for half in range(2):
            s_win = lax.dot_general(q2, kw_halves[half], _NT_DIMS, preferred_element_type=F32)
            s_win = jnp.where(valid, s_win, NEG)
            s_ctx = lax.dot_general(q2, ck_halves[half], _NT_DIMS, preferred_element_type=F32)
            sk = _sink_col(sink_ref, (4 * j + half, 4 * j + 2 + half), QB)
            pw, pc = _sink_softmax([s_win, s_ctx], sk)
            outs.append(jnp.dot(pw, vwd, preferred_element_type=F32)
                        + jnp.dot(pc, cvd, preferred_element_type=F32))
        for s2 in range(2):
            rows = slice(s2 * QB, (s2 + 1) * QB)
            o_ref[:, (2 * j + s2) * LANES:(2 * j + s2 + 1) * LANES] = (
                jnp.where(lo, outs[0][rows], outs[1][rows]).astype(BF16))


def _lat_attn(q, k, v, ck, cv, sink_b):
    lat0 = N_CTX // T_LAT
    return pl.pallas_call(
        _lat_attn_body,
        grid=(N_LAT_B, T_LAT // QB),
        in_specs=[pl.BlockSpec((QB, ATTN_W), lambda b, i: (N_CTX // QB + b * (T_LAT // QB) + i, 0)),
                  pl.BlockSpec((T_LAT, KV_W), lambda b, i: (lat0 + b, 0)),
                  pl.BlockSpec((T_LAT, KV_W), lambda b, i: (lat0 + b, 0)),
                  pl.BlockSpec((1, PAST, KV_W), lambda b, i: (b, 0, 0)),
                  pl.BlockSpec((1, PAST, KV_W), lambda b, i: (b, 0, 0)),
                  pl.BlockSpec((SUBLANES, LANES), lambda b, i: (0, 0))],
        out_specs=pl.BlockSpec((QB, ATTN_W), lambda b, i: (b * (T_LAT // QB) + i, 0)),
        out_shape=jax.ShapeDtypeStruct((N_LAT, ATTN_W), BF16),
        compiler_params=_cparams(2),
        name="lat_attn",
    )(q, k, v, ck, cv, sink_b)


def _l0_out_body(oc_ref, ol_ref, u_ref, xp_ref, xs_ref, mod_ref, pw_ref, ps_ref, wo_ref, x1_ref):
    i = pl.program_id(0)
    is_ctx = i < N_CTX // TB_MIX
    o = jnp.where(is_ctx, oc_ref[...], ol_ref[...])
    x = jnp.where(is_ctx, xp_ref[...], xs_ref[...])
    tseq = jnp.where(is_ctx, T_CTX, T_LAT)
    pos = lax.broadcasted_iota(I32, (TB_MIX, LANES), 0) & (tseq - 1)
    ys = []
    for g, win in enumerate(POOL_WINDOWS):
        hw = win // 2
        ug = u_ref[:, g * LANES:(g + 1) * LANES]
        acc = ug
        for jj in range(-hw, hw):
            if jj == 0:
                continue
            sh = pltpu.roll(ug, (-jj) % TB_MIX, axis=0)
            ok = (pos + jj >= 0) if jj < 0 else (pos + jj < tseq)
            acc = acc + jnp.where(ok, sh, 0.0)
        cnt = (jnp.minimum(pos + hw, tseq) - jnp.maximum(pos - hw, 0)).astype(F32)
        pooled = acc / cnt - ug
        ys.append(jnp.dot(pooled.astype(BF16), pw_ref[g], preferred_element_type=F32))
    y = jnp.concatenate(ys, axis=1) * ps_ref[...]
    out = (jnp.dot(o, wo_ref[0:ATTN_W, :], preferred_element_type=F32)
           + jnp.dot(y.astype(BF16), wo_ref[ATTN_W:ATTN_W + POOL_W, :], preferred_element_type=F32))
    x1_ref[...] = x + mod_ref[0][:, 2 * D:3 * D] * out


def _l0_out(o_ctx, o_lat, u, xp, xs, mods, pool_w, pool_scale, w_out):
    ntc = N_CTX // TB_MIX
    const = lambda shape: pl.BlockSpec(shape, lambda i: (0,) * len(shape))
    ctx_map = lambda i: (jnp.minimum(i, ntc - 1), 0)
    lat_map = lambda i: (jnp.maximum(i - ntc, 0), 0)
    return pl.pallas_call(
        _l0_out_body,
        grid=(N_TOK // TB_MIX,),
        in_specs=[pl.BlockSpec((TB_MIX, ATTN_W), ctx_map),
                  pl.BlockSpec((TB_MIX, ATTN_W), lat_map),
                  pl.BlockSpec((TB_MIX, POOL_W), lambda i: (i, 0)),
                  pl.BlockSpec((TB_MIX, D), ctx_map),
                  pl.BlockSpec((TB_MIX, D), lat_map),
                  pl.BlockSpec((1, 1, 6 * D), lambda i: (_mod_row(i, ntc, 1), 0, 0)),
                  const((len(POOL_WINDOWS), LANES, LANES)), const((1, POOL_W)), const((D, D))],
        out_specs=pl.BlockSpec((TB_MIX, D), lambda i: (i, 0)),
        out_shape=jax.ShapeDtypeStruct((N_TOK, D), F32),
        compiler_params=_cparams(1, VMEM_LIMIT),
        name="l0_out",
    )(o_ctx, o_lat, u, xp, xs, mods, pool_w, pool_scale, w_out)


def _first_max(vals):
    best, idx = vals[0], jnp.zeros(vals[0].shape, I32)
    for r in range(1, len(vals)):
        better = vals[r] > best
        idx = jnp.where(better, r, idx)
        best = jnp.where(better, vals[r], best)
    return best, idx


def _softmax_rows(rows):
    mx = functools.reduce(jnp.maximum, rows)
    ex = [jnp.exp(r - mx) for r in rows]
    tot = functools.reduce(lambda a, b: a + b, ex)
    return [e / tot for e in ex]


def _route_body(x_ref, mod_ref, g_ref, wr_ref, br_ref, tri_ref,
                xext_ref, info_ref, cnt_ref, base_ref):
    i = pl.program_id(0)

    @pl.when(i == 0)
    def _():
        base_ref[...] = jnp.zeros_like(base_ref)

    x = x_ref[...]
    m = mod_ref[0]
    h = _modulate(x, g_ref[...], m[:, 3 * D:4 * D], m[:, 4 * D:5 * D])

    hh = h.astype(BF16)
    hl = (h - hh.astype(F32)).astype(BF16)
    wh, wl = wr_ref[0], wr_ref[1]
    lg = (lax.dot_general(wh, hh, _NT_DIMS, preferred_element_type=F32)
          + lax.dot_general(wl, hh, _NT_DIMS, preferred_element_type=F32)
          + lax.dot_general(wh, hl, _NT_DIMS, preferred_element_type=F32)) + br_ref[:, 0:1]

    pg = _softmax_rows([lg[N_EXPERTS + r:N_EXPERTS + r + 1] for r in range(N_GROUPS)])
    pg_top, gi = _first_max(pg)
    le = []
    for j in range(PER_GROUP):
        sel = lg[(N_GROUPS - 1) * PER_GROUP + j:(N_GROUPS - 1) * PER_GROUP + j + 1]
        for g in range(N_GROUPS - 2, -1, -1):
            sel = jnp.where(gi == g, lg[g * PER_GROUP + j:g * PER_GROUP + j + 1], sel)
        le.append(sel)
    pe = _softmax_rows(le)
    p1, i1 = _first_max(pe)
    p2, i2 = _first_max([jnp.where(i1 == j, -1.0, pe[j]) for j in range(PER_GROUP)])
    den = p1 + p2
    w1 = pg_top * p1 / den
    w2 = pg_top * p2 / den

    lo = jnp.minimum(i1, i2)
    hi = jnp.maximum(i1, i2)
    cls = gi * PAIRS + jnp.where(lo == 0, 0, jnp.where(lo == 1, 3, 5)) + hi - lo - 1
    w_lo = jnp.where(i1 == lo, w1, w2)
    w_hi = jnp.where(i1 == lo, w2, w1)

    crow = lax.broadcasted_iota(I32, (CLASS_ROWS, TB), 0)
    hit = crow == cls
    onehot = jnp.where(hit, 1.0, 0.0)
    before = jnp.dot(onehot.astype(BF16), tri_ref[...], preferred_element_type=F32)
    before = before + base_ref[:, 0:1]
    rank = jnp.sum(jnp.where(hit, before, 0.0), axis=0, keepdims=True)
    base_ref[...] = base_ref[...] + jnp.sum(onehot, axis=1, keepdims=True)
    cnt_ref[...] = base_ref[...]

    mod_id = jnp.zeros_like(w1) + _mod_row(i, NT_CTX, T_LAT // TB).astype(F32)
    zero = jnp.zeros_like(w1)
    info_ref[...] = jnp.concatenate([cls.astype(F32), rank, zero, zero, zero, zero, zero, zero], axis=0)
    side = jnp.concatenate([w_lo, w_hi, mod_id, jnp.zeros((LANES - 3, TB), F32)], axis=0).T
    xext_ref[:, 0:D] = x
    xext_ref[:, D:XEXT] = side


def _route(x, mods, layer, g, wr, br, tri):
    const = lambda shape: pl.BlockSpec(shape, lambda i: (0,) * len(shape))
    return pl.pallas_call(
        _route_body,
        grid=(NT,),
        in_specs=[pl.BlockSpec((TB, D), lambda i: (i, 0)),
                  pl.BlockSpec((1, 1, 6 * D),
                               lambda i: (layer * SUBLANES + _mod_row(i, NT_CTX, T_LAT // TB), 0, 0)),
                  const((1, D)), const((2, ROUTE_ROWS, D)), const((ROUTE_ROWS, LANES)),
                  const((TB, TB))],
        out_specs=[pl.BlockSpec((TB, XEXT), lambda i: (i, 0)),
                   pl.BlockSpec((SUBLANES, TB), lambda i: (0, i)),
                   pl.BlockSpec((CLASS_ROWS, LANES), lambda i: (0, 0))],
        out_shape=[jax.ShapeDtypeStruct((N_TOK, XEXT), F32),
                   jax.ShapeDtypeStruct((SUBLANES, N_TOK), F32),
                   jax.ShapeDtypeStruct((CLASS_ROWS, LANES), F32)],
        scratch_shapes=[pltpu.VMEM((CLASS_ROWS, LANES), F32)],
        compiler_params=_cparams(1),
        name=f"route{layer}",
    )(x, mods, g, wr, br, tri)


def _plan(info, counts):
    cls = info[0].astype(I32)
    rank = info[1].astype(I32)
    cnt = counts[:N_CLASS, 0].astype(I32)
    tiles = (cnt + TM - 1) // TM
    tend = jnp.cumsum(tiles)
    tstart = tend - tiles
    n_used = tend[-1]
    cidx = jnp.arange(N_CLASS, dtype=I32)
    pos = jnp.sum(jnp.where(cls[:, None] == cidx, tstart * TM, 0), axis=-1) + rank
    n = jnp.arange(NT_FFN, dtype=I32)
    tile = jnp.minimum(n, jnp.maximum(n_used - 1, 0))
    tcls = jnp.minimum(jnp.sum((tile[:, None] >= tend[None, :]).astype(I32), axis=1), N_CLASS - 1)
    of_cls = tcls[:, None] == cidx
    rows = jnp.sum(jnp.where(of_cls, cnt, 0), axis=1) - (tile - jnp.sum(jnp.where(of_cls, tstart, 0), axis=1)) * TM
    rows = jnp.where(n < n_used, jnp.clip(rows, 0, TM), 0)
    chunks = (rows + CH - 1) // CH
    pair = tcls % PAIRS
    lo = (pair >= 3).astype(I32) + (pair >= 5).astype(I32)
    hi = jnp.where(pair < 3, pair + 1, jnp.where(pair < 5, pair - 1, 3))
    e_lo = (tcls // PAIRS) * PER_GROUP + lo
    e_hi = (tcls // PAIRS) * PER_GROUP + hi
    return pos, e_lo, e_hi, n_used.reshape(1), chunks


def _inverse_body(pos_ref, src_ref, dst_ref):
    def init(r, carry):
        src_ref[r] = 0
        dst_ref[r] = N_TOK + (r & (2 * TM - 1))
        return carry

    lax.fori_loop(0, P_FFN, init, 0, unroll=8)

    def put(t, carry):
        p = pos_ref[t]
        src_ref[p] = t
        dst_ref[p] = t
        return carry

    lax.fori_loop(0, N_TOK, put, 0, unroll=8)


def _inverse(pos, layer):
    smem = pl.BlockSpec(memory_space=pltpu.SMEM)
    return pl.pallas_call(
        _inverse_body,
        grid_spec=pltpu.PrefetchScalarGridSpec(
            num_scalar_prefetch=1, grid=(1,), in_specs=[], out_specs=[smem, smem]),
        out_shape=[jax.ShapeDtypeStruct((P_FFN,), I32), jax.ShapeDtypeStruct((P_FFN,), I32)],
        compiler_params=_cparams(1),
        name=f"inverse{layer}",
    )(pos)


def _ffn_body(src_ref, dst_ref, elo_ref, ehi_ref, nu_ref, ch_ref,
              xext_hbm, mod_ref, g_ref, w1a_ref, w1b_ref, w3a_ref, w3b_ref, w2a_ref, w2b_ref,
              out_hbm, xbuf, ybuf, wb1a, wb1b, wb3a, wb3b, wb2a, wb2b, gsem, ssem):
    n = pl.program_id(0)
    n_used = nu_ref[0]
    slot = n % 2

    def gather_copy(tile, s, r):
        return pltpu.make_async_copy(xext_hbm.at[pl.ds(src_ref[tile * TM + r], 1)],
                                     xbuf.at[s, pl.ds(r, 1)], gsem.at[s])

    def scatter_copy(tile, s, r):
        return pltpu.make_async_copy(ybuf.at[s, pl.ds(r, 1)],
                                     out_hbm.at[pl.ds(dst_ref[tile * TM + r], 1)], ssem.at[s])

    def start_rows(copy, tile, s):
        def chunk(c, carry):
            for r in range(CH):
                copy(tile, s, c * CH + r).start()
            return carry
        lax.fori_loop(0, ch_ref[tile], chunk, 0)

    def wait_rows(src, dst, sem, tile):
        def chunk(c, carry):
            pltpu.make_async_copy(src, dst, sem).wait()
            return carry
        lax.fori_loop(0, ch_ref[tile], chunk, 0)

    def wait_gather(tile, s):
        wait_rows(xext_hbm.at[pl.ds(0, CH)], xbuf.at[s, pl.ds(0, CH)], gsem.at[s], tile)

    def wait_scatter(tile, s):
        wait_rows(ybuf.at[s, pl.ds(0, CH)], out_hbm.at[pl.ds(0, CH)], ssem.at[s], tile)

    @pl.when(n == 0)
    def _():
        xbuf[...] = jnp.zeros_like(xbuf)
        ybuf[...] = jnp.zeros_like(ybuf)
        for s in range(2):
            dump = pltpu.make_async_copy(ybuf.at[s], out_hbm.at[pl.ds(N_TOK + s * TM, TM)], ssem.at[s])
            dump.start()
            dump.wait()
        start_rows(gather_copy, 0, 0)

    @pl.when(n < n_used)
    def _():
        @pl.when(n + 1 < n_used)
        def _():
            start_rows(gather_copy, n + 1, 1 - slot)

        wait_gather(n, slot)

        @pl.when(n >= 2)
        def _():
            wait_scatter(n - 2, slot)

        prev = jnp.maximum(n - 1, 0)

        @pl.when((n == 0) | (elo_ref[n] != elo_ref[prev]))
        def _():
            wb1a[...] = w1a_ref[0, 0].astype(BF16)
            wb3a[...] = w3a_ref[0, 0].astype(BF16)
            wb2a[...] = w2a_ref[0, 0].astype(BF16)

        @pl.when((n == 0) | (ehi_ref[n] != ehi_ref[prev]))
        def _():
            wb1b[...] = w1b_ref[0, 0].astype(BF16)
            wb3b[...] = w3b_ref[0, 0].astype(BF16)
            wb2b[...] = w2b_ref[0, 0].astype(BF16)

        xe = xbuf[slot]
        x = xe[:, 0:D]
        w_lo = xe[:, D:D + 1]
        w_hi = xe[:, D + 1:D + 2]
        mod_id = xe[:, D + 2:D + 3]

        def pick(lo, hi):
            return jnp.where(mod_id < 0.5, mod_ref[0][:, lo:hi],
                             jnp.where(mod_id < 1.5, mod_ref[1][:, lo:hi], mod_ref[2][:, lo:hi]))

        h = _modulate(x, g_ref[...], pick(3 * D, 4 * D), pick(4 * D, 5 * D)).astype(BF16)

        def act(wb1, wb3, w):
            h1 = jnp.dot(h, wb1[...], preferred_element_type=F32)
            h3 = jnp.dot(h, wb3[...], preferred_element_type=F32)
            return ((h1 * jax.nn.sigmoid(h1)) * h3 * w).astype(BF16)

        y = (jnp.dot(act(wb1a, wb3a, w_lo), wb2a[...], preferred_element_type=F32)
             + jnp.dot(act(wb1b, wb3b, w_hi), wb2b[...], preferred_element_type=F32))
        ybuf[slot] = x + pick(5 * D, 6 * D) * y
        start_rows(scatter_copy, n, slot)

        @pl.when(n == n_used - 1)
        def _():
            @pl.when(n >= 1)
            def _():
                wait_scatter(n - 1, 1 - slot)
            wait_scatter(n, slot)


def _ffn(src, dst, e_lo, e_hi, n_used, chunks, xext, mods, layer, g, w1, w3, w2):
    lo_map = lambda n, s, d, el, eh, nu, ch: (layer, el[n], 0, 0)
    hi_map = lambda n, s, d, el, eh, nu, ch: (layer, eh[n], 0, 0)
    up = lambda imap: pl.BlockSpec((1, 1, D, D_EXPERT), imap)
    down = lambda imap: pl.BlockSpec((1, 1, D_EXPERT, D), imap)
    return pl.pallas_call(
        _ffn_body,
        grid_spec=pltpu.PrefetchScalarGridSpec(
            num_scalar_prefetch=6, grid=(NT_FFN,),
            in_specs=[pl.BlockSpec(memory_space=pl.ANY),
                      pl.BlockSpec((SUBLANES, 1, 6 * D), lambda n, *_: (layer, 0, 0)),
                      pl.BlockSpec((1, D), lambda n, *_: (0, 0)),
                      up(lo_map), up(hi_map), up(lo_map), up(hi_map), down(lo_map), down(hi_map)],
            out_specs=pl.BlockSpec(memory_space=pl.ANY),
            scratch_shapes=[pltpu.VMEM((2, TM, XEXT), F32), pltpu.VMEM((2, TM, D), F32),
                            pltpu.VMEM((D, D_EXPERT), BF16), pltpu.VMEM((D, D_EXPERT), BF16),
                            pltpu.VMEM((D, D_EXPERT), BF16), pltpu.VMEM((D, D_EXPERT), BF16),
                            pltpu.VMEM((D_EXPERT, D), BF16), pltpu.VMEM((D_EXPERT, D), BF16),
                            pltpu.SemaphoreType.DMA((2,)), pltpu.SemaphoreType.DMA((2,))]),
        out_shape=jax.ShapeDtypeStruct((N_TOK + 2 * TM, D), F32),
        compiler_params=_cparams(1, VMEM_LIMIT),
        name=f"ffn{layer}",
    )(src, dst, e_lo, e_hi, n_used, chunks, xext, mods, g, w1, w1, w3, w3, w2, w2)


def _moe(x, mods, layer, g, wr, br, tri, w1, w3, w2):
    xext, info, counts = _route(x, mods, layer, g, wr, br, tri)
    pos, e_lo, e_hi, n_used, chunks = _plan(info, counts)
    src, dst = _inverse(pos, layer)
    return _ffn(src, dst, e_lo, e_hi, n_used, chunks, xext, mods, layer, g, w1, w3, w2)


FG = 256


def _l1_in_body(x_ref, mod_ref, g_ref, w_ref, c_ref, s_ref, zc_ref, zs_ref):
    m = mod_ref[0]
    h = _modulate(x_ref[...], g_ref[...], m[:, 0:D], m[:, D:2 * D])
    z = jnp.dot(h.astype(BF16), w_ref[...], preferred_element_type=F32).astype(BF16)
    for g in range(D // FG):
        zg = z[:, g * FG:(g + 1) * FG]
        zc_ref[:, g * FG:(g + 1) * FG] = jnp.dot(zg, c_ref[...], preferred_element_type=F32).astype(BF16)
        zs_ref[:, g * FG:(g + 1) * FG] = jnp.dot(zg, s_ref[...], preferred_element_type=F32).astype(BF16)


def _l1_in(x, mods, g, w, c256, s256):
    const = lambda shape: pl.BlockSpec(shape, lambda i: (0,) * len(shape))
    return pl.pallas_call(
        _l1_in_body,
        grid=(NT,),
        in_specs=[pl.BlockSpec((TB, D), lambda i: (i, 0)),
                  pl.BlockSpec((1, 1, 6 * D),
                               lambda i: (SUBLANES + _mod_row(i, NT_CTX, T_LAT // TB), 0, 0)),
                  const((1, D)), const((D, D)), const((FG, FG)), const((FG, FG))],
        out_specs=[pl.BlockSpec((TB, D), lambda i: (i, 0)), pl.BlockSpec((TB, D), lambda i: (i, 0))],
        out_shape=[jax.ShapeDtypeStruct((N_TOK, D), BF16), jax.ShapeDtypeStruct((N_TOK, D), BF16)],
        compiler_params=_cparams(1),
        name="l1_in",
    )(x, mods, g, w, c256, s256)


def _l1_out_body(zc_t_ref, zs_t_ref, zc_q_ref, zs_q_ref, c256_ref, s256_ref, c1k_ref, s1k_ref,
                 x_ref, mod_ref, wo_ref, o_ref, f_ref):
    i = pl.program_id(0)

    @pl.when(i < NT_CTX)
    def _():
        f = (jnp.dot(c256_ref[...], zc_t_ref[...], preferred_element_type=F32)
             - jnp.dot(s256_ref[...], zs_t_ref[...], preferred_element_type=F32))
        f_ref[...] = f.astype(BF16)

    @pl.when(i >= NT_CTX)
    def _():
        f = (jnp.dot(c1k_ref[...], zc_q_ref[...], preferred_element_type=F32)
             - jnp.dot(s1k_ref[...], zs_q_ref[...], preferred_element_type=F32))
        f_ref[...] = f.astype(BF16)

    out = jnp.dot(f_ref[...], wo_ref[...], preferred_element_type=F32)
    o_ref[...] = x_ref[...] + mod_ref[0][:, 2 * D:3 * D] * out


def _l1_out(zc, zs, c256, s256, c1k, s1k, x, mods, w_out):
    const = lambda shape: pl.BlockSpec(shape, lambda i: (0,) * len(shape))
    tile_map = lambda i: (jnp.minimum(i, NT_CTX - 1), 0)
    seq_map = lambda i: (N_CTX // T_LAT + jnp.maximum(i - NT_CTX, 0) // (T_LAT // TB), 0)
    row_map = lambda i: (jnp.maximum(i - NT_CTX, 0) % (T_LAT // TB), 0)
    return pl.pallas_call(
        _l1_out_body,
        grid=(NT,),
        in_specs=[pl.BlockSpec((TB, D), tile_map), pl.BlockSpec((TB, D), tile_map),
                  pl.BlockSpec((T_LAT, D), seq_map), pl.BlockSpec((T_LAT, D), seq_map),
                  const((T_CTX, T_CTX)), const((T_CTX, T_CTX)),
                  pl.BlockSpec((TB, T_LAT), row_map), pl.BlockSpec((TB, T_LAT), row_map),
                  pl.BlockSpec((TB, D), lambda i: (i, 0)),
                  pl.BlockSpec((1, 1, 6 * D),
                               lambda i: (SUBLANES + _mod_row(i, NT_CTX, T_LAT // TB), 0, 0)),
                  const((D, D))],
        out_specs=pl.BlockSpec((TB, D), lambda i: (i, 0)),
        out_shape=jax.ShapeDtypeStruct((N_TOK, D), F32),
        scratch_shapes=[pltpu.VMEM((TB, D), BF16)],
        compiler_params=_cparams(1),
        name="l1_out",
    )(zc, zs, zc, zs, c256, s256, c1k, s1k, x, mods, w_out)


def _split_hi_lo(w):
    hi = w.astype(BF16)
    lo = (w - hi.astype(F32)).astype(BF16)
    return jnp.stack([hi, lo])


def kernel(x_prompt, x_sample, cache_k, cache_v, c, c_ctx, ada_w, ada_b, norm_mix, norm_ffn, a_w_in, a_q_norm, a_k_norm, a_sink, pool_w, pool_scale, a_w_out, f_w_in, f_w_out, router_g_w, router_g_b, router_e_w, router_e_b, moe_w1, moe_w3, moe_w2):
    xp = x_prompt.reshape(N_CTX, D)
    xs = x_sample.reshape(N_LAT, D)

    cond8 = jnp.zeros((SUBLANES, D), F32).at[0].set(c_ctx).at[1:1 + N_LAT_B].set(c)
    mods = _adaln(cond8, ada_w, ada_b).reshape(DEPTH * SUBLANES, 1, 6 * D)

    tabs = _rope_tables()
    lane = np.arange(LANES)
    bd = jnp.asarray((lane[:, None] // HEAD_DIM) == (lane[None, :] // HEAD_DIM), BF16)
    tri = jnp.asarray(np.arange(TB)[:, None] < np.arange(TB)[None, :], BF16)
    c256, s256 = _dft_tables(T_CTX)
    c1k, s1k = _dft_tables(T_LAT)

    def router_operands(l):
        w = jnp.concatenate([router_e_w[l], router_g_w[l]], axis=1).T
        w = jnp.pad(w, ((0, ROUTE_ROWS - w.shape[0]), (0, 0)))
        b = jnp.concatenate([router_e_b[l], router_g_b[l]])
        b = jnp.pad(b, (0, ROUTE_ROWS - b.shape[0]))
        return _split_hi_lo(w), jnp.broadcast_to(b[:, None], (ROUTE_ROWS, LANES))

    qg = jnp.tile(a_q_norm[0], LANES // HEAD_DIM)[None, :]
    kg = jnp.tile(a_k_norm[0], LANES // HEAD_DIM)[None, :]
    q, k, v, u = _l0_in(xp, xs, mods, norm_mix[0][None, :], a_w_in[0].astype(BF16), qg, kg, bd, tabs)
    sink_b = jnp.broadcast_to(a_sink[0][:, None], (N_HEADS, LANES))
    o_ctx = _ctx_attn(q, k, v, sink_b)
    ck = cache_k[:, 0].reshape(N_LAT_B, PAST, KV_W)
    cv = cache_v[:, 0].reshape(N_LAT_B, PAST, KV_W)
    o_lat = _lat_attn(q, k, v, ck, cv, sink_b)
    x1 = _l0_out(o_ctx, o_lat, u, xp, xs, mods, pool_w[0].astype(BF16), pool_scale[0][None, :],
                 a_w_out[0].astype(BF16))
    wr, br = router_operands(0)
    x2 = _moe(x1, mods, 0, norm_ffn[0][None, :], wr, br, tri, moe_w1, moe_w3, moe_w2)

    zc, zs = _l1_in(x2, mods, norm_mix[1][None, :], f_w_in[0].astype(BF16), c256, s256)
    x3 = _l1_out(zc, zs, c256, s256, c1k, s1k, x2, mods, f_w_out[0].astype(BF16))
    wr, br = router_operands(1)
    x4 = _moe(x3, mods, 1, norm_ffn[1][None, :], wr, br, tri, moe_w1, moe_w3, moe_w2)

    new_k = k[:N_CTX].reshape(N_CTX_B, 1, T_CTX, KV_W // HEAD_DIM, HEAD_DIM)
    new_v = v[:N_CTX].reshape(N_CTX_B, 1, T_CTX, KV_W // HEAD_DIM, HEAD_DIM)
    return (x4[:N_CTX].reshape(N_CTX_B, T_CTX, D), x4[N_CTX:N_TOK].reshape(N_LAT_B, T_LAT, D),
            new_k, new_v)
```

```python
import functools

import numpy as np
import jax
import jax.numpy as jnp
from jax import lax
from jax.experimental import pallas as pl
from jax.experimental.pallas import tpu as pltpu

F32 = jnp.float32
BF16 = jnp.bfloat16
I32 = jnp.int32

D = 1024
DEPTH = 2
N_CTX_B, T_CTX = 16, 256
N_LAT_B, T_LAT = 2, 1024
N_CTX = N_CTX_B * T_CTX
N_LAT = N_LAT_B * T_LAT
N_TOK = N_CTX + N_LAT
PAST = 512
GRID_W = 64
HEAD_DIM = 64
N_HEADS = 8
ATTN_W = 512
KV_W = 128
POOL_W = 512
POOL_WINDOWS = (2, 4, 8, 16)
MIX_IN = ATTN_W + 2 * KV_W + POOL_W
WINDOW = 128
N_GROUPS = 4
PER_GROUP = 4
N_EXPERTS = 16
D_EXPERT = 512
ROPE_THETA = 10000.0
EPS = 1e-6
NEG = -1e30

LANES = 128
SUBLANES = 8
TB = 256
NT = N_TOK // TB
NT_CTX = N_CTX // TB
TB_MIX = 1024
TM = 256
PAIRS = 6
N_CLASS = N_GROUPS * PAIRS
CLASS_ROWS = 32
NT_FFN = N_TOK // TM + N_CLASS
P_FFN = NT_FFN * TM
CH = 32
XEXT = D + LANES
ROUTE_ROWS = 32

VMEM_LIMIT = 56 * 1024 * 1024


def _cparams(n_axes=1, vmem=None):
    return pltpu.CompilerParams(dimension_semantics=("arbitrary",) * n_axes,
                                vmem_limit_bytes=vmem)


def _modulate(x, g, shift, scale):
    ms = jnp.mean(x * x, axis=-1, keepdims=True)
    return (x * lax.rsqrt(ms + EPS) * g) * (1.0 + scale) + shift


def _mod_row(tile, tiles_ctx, tiles_per_lat):
    return (tile >= tiles_ctx).astype(I32) + (tile >= tiles_ctx + tiles_per_lat).astype(I32)


def _rope_tables():
    t = np.arange(T_LAT)
    row = (t // GRID_W).astype(np.float64)
    col = (t % GRID_W).astype(np.float64)
    nf = HEAD_DIM // 4
    freqs = ROPE_THETA ** (-np.arange(nf, dtype=np.float64) / nf)
    d = np.arange(HEAD_DIM)
    pos = np.where(d[None, :] < HEAD_DIM // 2, row[:, None], col[:, None])
    ang = pos * freqs[d % nf][None, :]
    first = (d % (HEAD_DIM // 2)) < nf
    cos = np.cos(ang)
    sin_a = np.where(first[None, :], -np.sin(ang), 0.0)
    sin_b = np.where(first[None, :], 0.0, np.sin(ang))
    ident = (np.ones((TB, HEAD_DIM)), np.zeros((TB, HEAD_DIM)), np.zeros((TB, HEAD_DIM)))
    out = []
    for tab, idt in zip((cos, sin_a, sin_b), ident):
        full = np.concatenate([tab, idt], axis=0)
        out.append(jnp.asarray(np.tile(full, (1, LANES // HEAD_DIM)), F32))
    return out


def _dft_tables(t):
    m = np.outer(np.arange(t), np.arange(t)) % t
    ang = 2.0 * np.pi * m / t
    s = 1.0 / np.sqrt(t)
    return jnp.asarray(np.cos(ang) * s, F32).astype(BF16), jnp.asarray(np.sin(ang) * s, F32).astype(BF16)


def _adaln_body(cond_ref, w_ref, b_ref, o_ref):
    c = cond_ref[...]
    s = (c * jax.nn.sigmoid(c)).astype(BF16)
    o_ref[0] = jnp.dot(s, w_ref[0].astype(BF16), preferred_element_type=F32) + b_ref[0]


def _adaln(cond8, ada_w, ada_b):
    tn = 1536
    return pl.pallas_call(
        _adaln_body,
        grid=(DEPTH, 6 * D // tn),
        in_specs=[pl.BlockSpec((SUBLANES, D), lambda l, j: (0, 0)),
                  pl.BlockSpec((1, D, tn), lambda l, j: (l, 0, j)),
                  pl.BlockSpec((1, 1, tn), lambda l, j: (l, 0, j))],
        out_specs=pl.BlockSpec((1, SUBLANES, tn), lambda l, j: (l, 0, j)),
        out_shape=jax.ShapeDtypeStruct((DEPTH, SUBLANES, 6 * D), F32),
        compiler_params=_cparams(2),
        name="adaln",
    )(cond8, ada_w, ada_b.reshape(DEPTH, 1, 6 * D))


def _l0_in_body(xp_ref, xs_ref, mod_ref, g_ref, w_ref, qg_ref, kg_ref, bd_ref,
                cos_ref, sa_ref, sb_ref, q_ref, k_ref, v_ref, u_ref):
    i = pl.program_id(0)
    x = jnp.where(i < NT_CTX, xp_ref[...], xs_ref[...])
    m = mod_ref[0]
    h = _modulate(x, g_ref[...], m[:, 0:D], m[:, D:2 * D])
    z = jnp.dot(h.astype(BF16), w_ref[...], preferred_element_type=F32)
    cos, sa, sb, bd = cos_ref[...], sa_ref[...], sb_ref[...], bd_ref[...]

    def head_norm_rope(zz, gain):
        ss = jnp.dot((zz * zz).astype(BF16), bd, preferred_element_type=F32)
        y = zz * lax.rsqrt(ss * (1.0 / HEAD_DIM) + EPS) * gain
        return (y * cos + pltpu.roll(y, LANES - 16, axis=1) * sa
                + pltpu.roll(y, 16, axis=1) * sb)

    for s in range(ATTN_W // LANES):
        qs = head_norm_rope(z[:, s * LANES:(s + 1) * LANES], qg_ref[...])
        q_ref[:, s * LANES:(s + 1) * LANES] = (qs * (HEAD_DIM ** -0.5)).astype(BF16)
    k_ref[...] = head_norm_rope(z[:, ATTN_W:ATTN_W + KV_W], kg_ref[...])
    v_ref[...] = z[:, ATTN_W + KV_W:ATTN_W + 2 * KV_W]
    u_ref[...] = z[:, ATTN_W + 2 * KV_W:MIX_IN]


def _l0_in(xp, xs, mods, g, w_in, qg, kg, bd, tabs):
    tab_spec = pl.BlockSpec(
        (TB, LANES), lambda i: (jnp.where(i < NT_CTX, T_LAT // TB, (i - NT_CTX) % (T_LAT // TB)), 0))
    const = lambda shape: pl.BlockSpec(shape, lambda i: (0,) * len(shape))
    return pl.pallas_call(
        _l0_in_body,
        grid=(NT,),
        in_specs=[pl.BlockSpec((TB, D), lambda i: (jnp.minimum(i, NT_CTX - 1), 0)),
                  pl.BlockSpec((TB, D), lambda i: (jnp.maximum(i - NT_CTX, 0), 0)),
                  pl.BlockSpec((1, 1, 6 * D), lambda i: (_mod_row(i, NT_CTX, T_LAT // TB), 0, 0)),
                  const((1, D)), const((D, MIX_IN)), const((1, LANES)), const((1, LANES)),
                  const((LANES, LANES)), tab_spec, tab_spec, tab_spec],
        out_specs=[pl.BlockSpec((TB, ATTN_W), lambda i: (i, 0)),
                   pl.BlockSpec((TB, KV_W), lambda i: (i, 0)),
                   pl.BlockSpec((TB, KV_W), lambda i: (i, 0)),
                   pl.BlockSpec((TB, POOL_W), lambda i: (i, 0))],
        out_shape=[jax.ShapeDtypeStruct((N_TOK, ATTN_W), BF16),
                   jax.ShapeDtypeStruct((N_TOK, KV_W), F32),
                   jax.ShapeDtypeStruct((N_TOK, KV_W), F32),
                   jax.ShapeDtypeStruct((N_TOK, POOL_W), F32)],
        compiler_params=_cparams(1),
        name="l0_in",
    )(xp, xs, mods, g, w_in, qg, kg, bd, *tabs)


def _head_halves(x):
    z = jnp.zeros_like(x)
    return jnp.concatenate([x, z], axis=1), jnp.concatenate([z, x], axis=1)


_NT_DIMS = (((1,), (1,)), ((), ()))


def _sink_softmax(scores, sk):
    mx = sk
    for sc in scores:
        mx = jnp.maximum(mx, jnp.max(sc, axis=-1, keepdims=True))
    ps = [jnp.exp(sc - mx) for sc in scores]
    den = jnp.exp(sk - mx)
    for p in ps:
        den = den + jnp.sum(p, axis=-1, keepdims=True)
    inv = 1.0 / den
    return [(p * inv).astype(BF16) for p in ps]


def _sink_col(sink_ref, heads, rows):
    return jnp.concatenate([jnp.broadcast_to(sink_ref[h:h + 1, 0:1], (rows, 1)) for h in heads], axis=0)


def _ctx_attn_body(q_ref, k_ref, v_ref, sink_ref, o_ref):
    k = k_ref[...].astype(BF16)
    v = v_ref[...].astype(BF16)
    lo = lax.broadcasted_iota(I32, (T_CTX, LANES), 1) < HEAD_DIM
    for j in range(KV_W // HEAD_DIM):
        kj = k[:, j * HEAD_DIM:(j + 1) * HEAD_DIM]
        vj = v[:, j * HEAD_DIM:(j + 1) * HEAD_DIM]
        k_halves = _head_halves(kj)
        vd = jnp.concatenate([vj, vj], axis=1)
        q2 = jnp.concatenate([q_ref[:, (2 * j) * LANES:(2 * j + 1) * LANES],
                              q_ref[:, (2 * j + 1) * LANES:(2 * j + 2) * LANES]], axis=0)
        outs = []
        for half in range(2):
            sc = lax.dot_general(q2, k_halves[half], _NT_DIMS, preferred_element_type=F32)
            sk = _sink_col(sink_ref, (4 * j + half, 4 * j + 2 + half), T_CTX)
            (p,) = _sink_softmax([sc], sk)
            outs.append(jnp.dot(p, vd, preferred_element_type=F32))
        for s2 in range(2):
            rows = slice(s2 * T_CTX, (s2 + 1) * T_CTX)
            o_ref[:, (2 * j + s2) * LANES:(2 * j + s2 + 1) * LANES] = (
                jnp.where(lo, outs[0][rows], outs[1][rows]).astype(BF16))


def _ctx_attn(q, k, v, sink_b):
    return pl.pallas_call(
        _ctx_attn_body,
        grid=(N_CTX_B,),
        in_specs=[pl.BlockSpec((T_CTX, ATTN_W), lambda b: (b, 0)),
                  pl.BlockSpec((T_CTX, KV_W), lambda b: (b, 0)),
                  pl.BlockSpec((T_CTX, KV_W), lambda b: (b, 0)),
                  pl.BlockSpec((SUBLANES, LANES), lambda b: (0, 0))],
        out_specs=pl.BlockSpec((T_CTX, ATTN_W), lambda b: (b, 0)),
        out_shape=jax.ShapeDtypeStruct((N_CTX, ATTN_W), BF16),
        compiler_params=_cparams(1),
        name="ctx_attn",
    )(q, k, v, sink_b)


QB = 128
SPAN = QB + 2 * WINDOW


def _lat_attn_body(q_ref, k_ref, v_ref, ck_ref, cv_ref, sink_ref, o_ref):
    qb = pl.program_id(1)
    start = qb * QB
    kws, vws = [], []
    for c in (-1, 0, 1):
        cs = pl.multiple_of(jnp.clip(start + c * QB, 0, T_LAT - QB), QB)
        kws.append(k_ref[pl.ds(cs, QB), :])
        vws.append(v_ref[pl.ds(cs, QB), :])
    kw = jnp.concatenate(kws, axis=0).astype(BF16)
    vw = jnp.concatenate(vws, axis=0).astype(BF16)
    ck = ck_ref[0].astype(BF16)
    cv = cv_ref[0].astype(BF16)
    qpos = start + (lax.broadcasted_iota(I32, (2 * QB, SPAN), 0) & (QB - 1))
    kpos = start - WINDOW + lax.broadcasted_iota(I32, (2 * QB, SPAN), 1)
    valid = (kpos >= 0) & (kpos < T_LAT) & (jnp.abs(qpos - kpos) <= WINDOW)
    lo = lax.broadcasted_iota(I32, (QB, LANES), 1) < HEAD_DIM
    for j in range(KV_W // HEAD_DIM):
        sl = slice(j * HEAD_DIM, (j + 1) * HEAD_DIM)
        kw_halves = _head_halves(kw[:, sl])
        ck_halves = _head_halves(ck[:, sl])
        vwd = jnp.concatenate([vw[:, sl], vw[:, sl]], axis=1)
        cvd = jnp.concatenate([cv[:, sl], cv[:, sl]], axis=1)
        q2 = jnp.concatenate([q_ref[:, (2 * j) * LANES:(2 * j + 1) * LANES],
                              q_ref[:, (2 * j + 1) * LANES:(2 * j + 2) * LANES]], axis=0)
        outs = []
        for half in range(2):
            s_win = lax.dot_general(q2, kw_halves[half], _NT_DIMS, preferred_element_type=F32)
            s_win = jnp.where(valid, s_win, NEG)
            s_ctx = lax.dot_general(q2, ck_halves[half], _NT_DIMS, preferred_element_type=F32)
            sk = _sink_col(sink_ref, (4 * j + half, 4 * j + 2 + half), QB)
            pw, pc = _sink_softmax([s_win, s_ctx], sk)
            outs.append(jnp.dot(pw, vwd, preferred_element_type=F32)
                        + jnp.dot(pc, cvd, preferred_element_type=F32))
        for s2 in range(2):
            rows = slice(s2 * QB, (s2 + 1) * QB)
            o_ref[:, (2 * j + s2) * LANES:(2 * j + s2 + 1) * LANES] = (
                jnp.where(lo, outs[0][rows], outs[1][rows]).astype(BF16))


def _lat_attn(q, k, v, ck, cv, sink_b):
    lat0 = N_CTX // T_LAT
    return pl.pallas_call(
        _lat_attn_body,
        grid=(N_LAT_B, T_LAT // QB),
        in_specs=[pl.BlockSpec((QB, ATTN_W), lambda b, i: (N_CTX // QB + b * (T_LAT // QB) + i, 0)),
                  pl.BlockSpec((T_LAT, KV_W), lambda b, i: (lat0 + b, 0)),
                  pl.BlockSpec((T_LAT, KV_W), lambda b, i: (lat0 + b, 0)),
                  pl.BlockSpec((1, PAST, KV_W), lambda b, i: (b, 0, 0)),
                  pl.BlockSpec((1, PAST, KV_W), lambda b, i: (b, 0, 0)),
                  pl.BlockSpec((SUBLANES, LANES), lambda b, i: (0, 0))],
        out_specs=pl.BlockSpec((QB, ATTN_W), lambda b, i: (b * (T_LAT // QB) + i, 0)),
        out_shape=jax.ShapeDtypeStruct((N_LAT, ATTN_W), BF16),
        compiler_params=_cparams(2),
        name="lat_attn",
    )(q, k, v, ck, cv, sink_b)


def _l0_out_body(oc_ref, ol_ref, u_ref, xp_ref, xs_ref, mod_ref, pw_ref, ps_ref, wo_ref, x1_ref):
    i = pl.program_id(0)
    is_ctx = i < N_CTX // TB_MIX
    o = jnp.where(is_ctx, oc_ref[...], ol_ref[...])
    x = jnp.where(is_ctx, xp_ref[...], xs_ref[...])
    tseq = jnp.where(is_ctx, T_CTX, T_LAT)
    pos = lax.broadcasted_iota(I32, (TB_MIX, LANES), 0) & (tseq - 1)
    ys = []
    for g, win in enumerate(POOL_WINDOWS):
        hw = win // 2
        ug = u_ref[:, g * LANES:(g + 1) * LANES]
        acc = ug
        for jj in range(-hw, hw):
            if jj == 0:
                continue
            sh = pltpu.roll(ug, (-jj) % TB_MIX, axis=0)
            ok = (pos + jj >= 0) if jj < 0 else (pos + jj < tseq)
            acc = acc + jnp.where(ok, sh, 0.0)
        cnt = (jnp.minimum(pos + hw, tseq) - jnp.maximum(pos - hw, 0)).astype(F32)
        pooled = acc / cnt - ug
        ys.append(jnp.dot(pooled.astype(BF16), pw_ref[g], preferred_element_type=F32))
    y = jnp.concatenate(ys, axis=1) * ps_ref[...]
    out = (jnp.dot(o, wo_ref[0:ATTN_W, :], preferred_element_type=F32)
           + jnp.dot(y.astype(BF16), wo_ref[ATTN_W:ATTN_W + POOL_W, :], preferred_element_type=F32))
    x1_ref[...] = x + mod_ref[0][:, 2 * D:3 * D] * out


def _l0_out(o_ctx, o_lat, u, xp, xs, mods, pool_w, pool_scale, w_out):
    ntc = N_CTX // TB_MIX
    const = lambda shape: pl.BlockSpec(shape, lambda i: (0,) * len(shape))
    ctx_map = lambda i: (jnp.minimum(i, ntc - 1), 0)
    lat_map = lambda i: (jnp.maximum(i - ntc, 0), 0)
    return pl.pallas_call(
        _l0_out_body,
        grid=(N_TOK // TB_MIX,),
        in_specs=[pl.BlockSpec((TB_MIX, ATTN_W), ctx_map),
                  pl.BlockSpec((TB_MIX, ATTN_W), lat_map),
                  pl.BlockSpec((TB_MIX, POOL_W), lambda i: (i, 0)),
                  pl.BlockSpec((TB_MIX, D), ctx_map),
                  pl.BlockSpec((TB_MIX, D), lat_map),
                  pl.BlockSpec((1, 1, 6 * D), lambda i: (_mod_row(i, ntc, 1), 0, 0)),
                  const((len(POOL_WINDOWS), LANES, LANES)), const((1, POOL_W)), const((D, D))],
        out_specs=pl.BlockSpec((TB_MIX, D), lambda i: (i, 0)),
        out_shape=jax.ShapeDtypeStruct((N_TOK, D), F32),
        compiler_params=_cparams(1, VMEM_LIMIT),
        name="l0_out",
    )(o_ctx, o_lat, u, xp, xs, mods, pool_w, pool_scale, w_out)


def _first_max(vals):
    best, idx = vals[0], jnp.zeros(vals[0].shape, I32)
    for r in range(1, len(vals)):
        better = vals[r] > best
        idx = jnp.where(better, r, idx)
        best = jnp.where(better, vals[r], best)
    return best, idx


def _softmax_rows(rows):
    mx = functools.reduce(jnp.maximum, rows)
    ex = [jnp.exp(r - mx) for r in rows]
    tot = functools.reduce(lambda a, b: a + b, ex)
    return [e / tot for e in ex]


def _route_body(x_ref, mod_ref, g_ref, wr_ref, br_ref, tri_ref,
                xext_ref, info_ref, cnt_ref, base_ref):
    i = pl.program_id(0)

    @pl.when(i == 0)
    def _():
        base_ref[...] = jnp.zeros_like(base_ref)

    x = x_ref[...]
    m = mod_ref[0]
    h = _modulate(x, g_ref[...], m[:, 3 * D:4 * D], m[:, 4 * D:5 * D])

    hh = h.astype(BF16)
    hl = (h - hh.astype(F32)).astype(BF16)
    wh, wl = wr_ref[0], wr_ref[1]
    lg = (lax.dot_general(wh, hh, _NT_DIMS, preferred_element_type=F32)
          + lax.dot_general(wl, hh, _NT_DIMS, preferred_element_type=F32)
          + lax.dot_general(wh, hl, _NT_DIMS, preferred_element_type=F32)) + br_ref[:, 0:1]

    pg = _softmax_rows([lg[N_EXPERTS + r:N_EXPERTS + r + 1] for r in range(N_GROUPS)])
    pg_top, gi = _first_max(pg)
    le = []
    for j in range(PER_GROUP):
        sel = lg[(N_GROUPS - 1) * PER_GROUP + j:(N_GROUPS - 1) * PER_GROUP + j + 1]
        for g in range(N_GROUPS - 2, -1, -1):
            sel = jnp.where(gi == g, lg[g * PER_GROUP + j:g * PER_GROUP + j + 1], sel)
        le.append(sel)
    pe = _softmax_rows(le)
    p1, i1 = _first_max(pe)
    p2, i2 = _first_max([jnp.where(i1 == j, -1.0, pe[j]) for j in range(PER_GROUP)])
    den = p1 + p2
    w1 = pg_top * p1 / den
    w2 = pg_top * p2 / den

    lo = jnp.minimum(i1, i2)
    hi = jnp.maximum(i1, i2)
    cls = gi * PAIRS + jnp.where(lo == 0, 0, jnp.where(lo == 1, 3, 5)) + hi - lo - 1
    w_lo = jnp.where(i1 == lo, w1, w2)
    w_hi = jnp.where(i1 == lo, w2, w1)

    crow = lax.broadcasted_iota(I32, (CLASS_ROWS, TB), 0)
    hit = crow == cls
    onehot = jnp.where(hit, 1.0, 0.0)
    before = jnp.dot(onehot.astype(BF16), tri_ref[...], preferred_element_type=F32)
    before = before + base_ref[:, 0:1]
    rank = jnp.sum(jnp.where(hit, before, 0.0), axis=0, keepdims=True)
    base_ref[...] = base_ref[...] + jnp.sum(onehot, axis=1, keepdims=True)
    cnt_ref[...] = base_ref[...]

    mod_id = jnp.zeros_like(w1) + _mod_row(i, NT_CTX, T_LAT // TB).astype(F32)
    zero = jnp.zeros_like(w1)
    info_ref[...] = jnp.concatenate([cls.astype(F32), rank, zero, zero, zero, zero, zero, zero], axis=0)
    side = jnp.concatenate([w_lo, w_hi, mod_id, jnp.zeros((LANES - 3, TB), F32)], axis=0).T
    xext_ref[:, 0:D] = x
    xext_ref[:, D:XEXT] = side


def _route(x, mods, layer, g, wr, br, tri):
    const = lambda shape: pl.BlockSpec(shape, lambda i: (0,) * len(shape))
    return pl.pallas_call(
        _route_body,
        grid=(NT,),
        in_specs=[pl.BlockSpec((TB, D), lambda i: (i, 0)),
                  pl.BlockSpec((1, 1, 6 * D),
                               lambda i: (layer * SUBLANES + _mod_row(i, NT_CTX, T_LAT // TB), 0, 0)),
                  const((1, D)), const((2, ROUTE_ROWS, D)), const((ROUTE_ROWS, LANES)),
                  const((TB, TB))],
        out_specs=[pl.BlockSpec((TB, XEXT), lambda i: (i, 0)),
                   pl.BlockSpec((SUBLANES, TB), lambda i: (0, i)),
                   pl.BlockSpec((CLASS_ROWS, LANES), lambda i: (0, 0))],
        out_shape=[jax.ShapeDtypeStruct((N_TOK, XEXT), F32),
                   jax.ShapeDtypeStruct((SUBLANES, N_TOK), F32),
                   jax.ShapeDtypeStruct((CLASS_ROWS, LANES), F32)],
        scratch_shapes=[pltpu.VMEM((CLASS_ROWS, LANES), F32)],
        compiler_params=_cparams(1),
        name=f"route{layer}",
    )(x, mods, g, wr, br, tri)


def _plan(info, counts):
    cls = info[0].astype(I32)
    rank = info[1].astype(I32)
    cnt = counts[:N_CLASS, 0].astype(I32)
    tiles = (cnt + TM - 1) // TM
    tend = jnp.cumsum(tiles)
    tstart = tend - tiles
    n_used = tend[-1]
    cidx = jnp.arange(N_CLASS, dtype=I32)
    pos = jnp.sum(jnp.where(cls[:, None] == cidx, tstart * TM, 0), axis=-1) + rank
    n = jnp.arange(NT_FFN, dtype=I32)
    tile = jnp.minimum(n, jnp.maximum(n_used - 1, 0))
    tcls = jnp.minimum(jnp.sum((tile[:, None] >= tend[None, :]).astype(I32), axis=1), N_CLASS - 1)
    of_cls = tcls[:, None] == cidx
    rows = jnp.sum(jnp.where(of_cls, cnt, 0), axis=1) - (tile - jnp.sum(jnp.where(of_cls, tstart, 0), axis=1)) * TM
    rows = jnp.where(n < n_used, jnp.clip(rows, 0, TM), 0)
    chunks = (rows + CH - 1) // CH
    pair = tcls % PAIRS
    lo = (pair >= 3).astype(I32) + (pair >= 5).astype(I32)
    hi = jnp.where(pair < 3, pair + 1, jnp.where(pair < 5, pair - 1, 3))
    e_lo = (tcls // PAIRS) * PER_GROUP + lo
    e_hi = (tcls // PAIRS) * PER_GROUP + hi
    return pos, e_lo, e_hi, n_used.reshape(1), chunks


OCT = TM // SUBLANES


def _ffn_body(pos_ref, elo_ref, ehi_ref, nu_ref, ch_ref,
              xext_hbm, mod_ref, g_ref, w1a_ref, w1b_ref, w3a_ref, w3b_ref, w2a_ref, w2b_ref,
              out_hbm, src_ref, dst_ref, xbuf, ybuf, wb1a, wb1b, wb3a, wb3b, wb2a, wb2b, gsem, ssem):
    n = pl.program_id(0)
    n_used = nu_ref[0]

    def gather_copy(tile, s, c, j):
        return pltpu.make_async_copy(
            xext_hbm.at[pl.ds(src_ref[tile * TM + c * CH + j], 1)],
            xbuf.at[s, c * (CH // SUBLANES) + j // SUBLANES, pl.ds(j % SUBLANES, 1)], gsem.at[s])

    def scatter_copy(tile, s, c, j):
        return pltpu.make_async_copy(
            ybuf.at[s, c * (CH // SUBLANES) + j // SUBLANES, pl.ds(j % SUBLANES, 1)],
            out_hbm.at[pl.ds(dst_ref[tile * TM + c * CH + j], 1)], ssem.at[s])

    def start_rows(copy, tile, s):
        def chunk(c, carry):
            for j in range(CH):
                copy(tile, s, c, j).start()
            return carry
        lax.fori_loop(0, ch_ref[tile], chunk, 0)

    def wait_rows(src, dst, sem, tile):
        def chunk(c, carry):
            pltpu.make_async_copy(src, dst, sem).wait()
            return carry
        lax.fori_loop(0, ch_ref[tile], chunk, 0)

    def wait_gather(tile, s):
        rows = xbuf.at[s, pl.ds(0, CH // SUBLANES)]
        wait_rows(rows, rows, gsem.at[s], tile)

    def wait_scatter(tile, s):
        rows = ybuf.at[s, pl.ds(0, CH // SUBLANES)]
        wait_rows(rows, rows, ssem.at[s], tile)

    @pl.when(n == 0)
    def _():
        def pad_rows(tile, carry):
            @pl.when(ch_ref[tile] > 0)
            def _():
                first = tile * TM + (ch_ref[tile] - 1) * CH
                for j in range(CH):
                    src_ref[first + j] = 0
                    dst_ref[first + j] = N_TOK + ((first + j) & (2 * TM - 1))
            return carry

        lax.fori_loop(0, NT_FFN, pad_rows, 0)

        def put(t, carry):
            p = pos_ref[t]
            src_ref[p] = t
            dst_ref[p] = t
            return carry

        lax.fori_loop(0, N_TOK, put, 0, unroll=8)

        xbuf[...] = jnp.zeros_like(xbuf)
        ybuf[...] = jnp.zeros_like(ybuf)
        for s in range(2):
            dumps = [pltpu.make_async_copy(
                ybuf.at[s, q], out_hbm.at[pl.ds(N_TOK + s * TM + q * SUBLANES, SUBLANES)], ssem.at[s])
                for q in range(OCT)]
            for dump in dumps:
                dump.start()
            for dump in dumps:
                dump.wait()
        start_rows(gather_copy, 0, 0)

    def step(slot):
        @pl.when(n + 1 < n_used)
        def _():
            start_rows(gather_copy, n + 1, 1 - slot)

        wait_gather(n, slot)

        @pl.when(n >= 2)
        def _():
            wait_scatter(n - 2, slot)

        prev = jnp.maximum(n - 1, 0)

        @pl.when((n == 0) | (elo_ref[n] != elo_ref[prev]))
        def _():
            wb1a[...] = w1a_ref[0, 0].astype(BF16)
            wb3a[...] = w3a_ref[0, 0].astype(BF16)
            wb2a[...] = w2a_ref[0, 0].astype(BF16)

        @pl.when((n == 0) | (ehi_ref[n] != ehi_ref[prev]))
        def _():
            wb1b[...] = w1b_ref[0, 0].astype(BF16)
            wb3b[...] = w3b_ref[0, 0].astype(BF16)
            wb2b[...] = w2b_ref[0, 0].astype(BF16)

        xe = xbuf[slot].reshape(TM, XEXT)
        x = xe[:, 0:D]
        w_lo = xe[:, D:D + 1]
        w_hi = xe[:, D + 1:D + 2]
        mod_id = xe[:, D + 2:D + 3]

        def pick(lo, hi):
            return jnp.where(mod_id < 0.5, mod_ref[0][:, lo:hi],
                             jnp.where(mod_id < 1.5, mod_ref[1][:, lo:hi], mod_ref[2][:, lo:hi]))

        h = _modulate(x, g_ref[...], pick(3 * D, 4 * D), pick(4 * D, 5 * D)).astype(BF16)

        def act(wb1, wb3, w):
            h1 = jnp.dot(h, wb1[...], preferred_element_type=F32)
            h3 = jnp.dot(h, wb3[...], preferred_element_type=F32)
            return ((h1 * jax.nn.sigmoid(h1)) * h3 * w).astype(BF16)

        y = (jnp.dot(act(wb1a, wb3a, w_lo), wb2a[...], preferred_element_type=F32)
             + jnp.dot(act(wb1b, wb3b, w_hi), wb2b[...], preferred_element_type=F32))
        ybuf[slot] = (x + pick(5 * D, 6 * D) * y).reshape(OCT, SUBLANES, D)
        start_rows(scatter_copy, n, slot)

        @pl.when(n == n_used - 1)
        def _():
            @pl.when(n >= 1)
            def _():
                wait_scatter(n - 1, 1 - slot)
            wait_scatter(n, slot)

    for s in range(2):
        @pl.when((n < n_used) & (n % 2 == s))
        def _():
            step(s)


def _ffn(pos, e_lo, e_hi, n_used, chunks, xext, mods, layer, g, w1, w3, w2):
    lo_map = lambda n, p, el, eh, nu, ch: (layer, el[n], 0, 0)
    hi_map = lambda n, p, el, eh, nu, ch: (layer, eh[n], 0, 0)
    up = lambda imap: pl.BlockSpec((1, 1, D, D_EXPERT), imap)
    down = lambda imap: pl.BlockSpec((1, 1, D_EXPERT, D), imap)
    return pl.pallas_call(
        _ffn_body,
        grid_spec=pltpu.PrefetchScalarGridSpec(
            num_scalar_prefetch=5, grid=(NT_FFN,),
            in_specs=[pl.BlockSpec(memory_space=pl.ANY),
                      pl.BlockSpec((SUBLANES, 1, 6 * D), lambda n, *_: (layer, 0, 0)),
                      pl.BlockSpec((1, D), lambda n, *_: (0, 0)),
                      up(lo_map), up(hi_map), up(lo_map), up(hi_map), down(lo_map), down(hi_map)],
            out_specs=pl.BlockSpec(memory_space=pl.ANY),
            scratch_shapes=[pltpu.SMEM((P_FFN,), I32), pltpu.SMEM((P_FFN,), I32),
                            pltpu.VMEM((2, OCT, SUBLANES, XEXT), F32),
                            pltpu.VMEM((2, OCT, SUBLANES, D), F32),
                            pltpu.VMEM((D, D_EXPERT), BF16), pltpu.VMEM((D, D_EXPERT), BF16),
                            pltpu.VMEM((D, D_EXPERT), BF16), pltpu.VMEM((D, D_EXPERT), BF16),
                            pltpu.VMEM((D_EXPERT, D), BF16), pltpu.VMEM((D_EXPERT, D), BF16),
                            pltpu.SemaphoreType.DMA((2,)), pltpu.SemaphoreType.DMA((2,))]),
        out_shape=jax.ShapeDtypeStruct((N_TOK + 2 * TM, D), F32),
        compiler_params=_cparams(1, VMEM_LIMIT),
        name=f"ffn{layer}",
    )(pos, e_lo, e_hi, n_used, chunks, xext, mods, g, w1, w1, w3, w3, w2, w2)


def _moe(x, mods, layer, g, wr, br, tri, w1, w3, w2):
    xext, info, counts = _route(x, mods, layer, g, wr, br, tri)
    pos, e_lo, e_hi, n_used, chunks = _plan(info, counts)
    return _ffn(pos, e_lo, e_hi, n_used, chunks, xext, mods, layer, g, w1, w3, w2)


FG = 256


def _l1_in_body(x_ref, mod_ref, g_ref, w_ref, c_ref, s_ref, zc_ref, zs_ref):
    m = mod_ref[0]
    h = _modulate(x_ref[...], g_ref[...], m[:, 0:D], m[:, D:2 * D])
    z = jnp.dot(h.astype(BF16), w_ref[...], preferred_element_type=F32).astype(BF16)
    for g in range(D // FG):
        zg = z[:, g * FG:(g + 1) * FG]
        zc_ref[:, g * FG:(g + 1) * FG] = jnp.dot(zg, c_ref[...], preferred_element_type=F32).astype(BF16)
        zs_ref[:, g * FG:(g + 1) * FG] = jnp.dot(zg, s_ref[...], preferred_element_type=F32).astype(BF16)


def _l1_in(x, mods, g, w, c256, s256):
    const = lambda shape: pl.BlockSpec(shape, lambda i: (0,) * len(shape))
    return pl.pallas_call(
        _l1_in_body,
        grid=(NT,),
        in_specs=[pl.BlockSpec((TB, D), lambda i: (i, 0)),
                  pl.BlockSpec((1, 1, 6 * D),
                               lambda i: (SUBLANES + _mod_row(i, NT_CTX, T_LAT // TB), 0, 0)),
                  const((1, D)), const((D, D)), const((FG, FG)), const((FG, FG))],
        out_specs=[pl.BlockSpec((TB, D), lambda i: (i, 0)), pl.BlockSpec((TB, D), lambda i: (i, 0))],
        out_shape=[jax.ShapeDtypeStruct((N_TOK, D), BF16), jax.ShapeDtypeStruct((N_TOK, D), BF16)],
        compiler_params=_cparams(1),
        name="l1_in",
    )(x, mods, g, w, c256, s256)


def _l1_out_body(zc_t_ref, zs_t_ref, zc_q_ref, zs_q_ref, c256_ref, s256_ref, c1k_ref, s1k_ref,
                 x_ref, mod_ref, wo_ref, o_ref, f_ref):
    i = pl.program_id(0)

    @pl.when(i < NT_CTX)
    def _():
        f = (jnp.dot(c256_ref[...], zc_t_ref[...], preferred_element_type=F32)
             - jnp.dot(s256_ref[...], zs_t_ref[...], preferred_element_type=F32))
        f_ref[...] = f.astype(BF16)

    @pl.when(i >= NT_CTX)
    def _():
        f = (jnp.dot(c1k_ref[...], zc_q_ref[...], preferred_element_type=F32)
             - jnp.dot(s1k_ref[...], zs_q_ref[...], preferred_element_type=F32))
        f_ref[...] = f.astype(BF16)

    out = jnp.dot(f_ref[...], wo_ref[...], preferred_element_type=F32)
    o_ref[...] = x_ref[...] + mod_ref[0][:, 2 * D:3 * D] * out


def _l1_out(zc, zs, c256, s256, c1k, s1k, x, mods, w_out):
    const = lambda shape: pl.BlockSpec(shape, lambda i: (0,) * len(shape))
    tile_map = lambda i: (jnp.minimum(i, NT_CTX - 1), 0)
    seq_map = lambda i: (N_CTX // T_LAT + jnp.maximum(i - NT_CTX, 0) // (T_LAT // TB), 0)
    row_map = lambda i: (jnp.maximum(i - NT_CTX, 0) % (T_LAT // TB), 0)
    return pl.pallas_call(
        _l1_out_body,
        grid=(NT,),
        in_specs=[pl.BlockSpec((TB, D), tile_map), pl.BlockSpec((TB, D), tile_map),
                  pl.BlockSpec((T_LAT, D), seq_map), pl.BlockSpec((T_LAT, D), seq_map),
                  const((T_CTX, T_CTX)), const((T_CTX, T_CTX)),
                  pl.BlockSpec((TB, T_LAT), row_map), pl.BlockSpec((TB, T_LAT), row_map),
                  pl.BlockSpec((TB, D), lambda i: (i, 0)),
                  pl.BlockSpec((1, 1, 6 * D),
                               lambda i: (SUBLANES + _mod_row(i, NT_CTX, T_LAT // TB), 0, 0)),
                  const((D, D))],
        out_specs=pl.BlockSpec((TB, D), lambda i: (i, 0)),
        out_shape=jax.ShapeDtypeStruct((N_TOK, D), F32),
        scratch_shapes=[pltpu.VMEM((TB, D), BF16)],
        compiler_params=_cparams(1),
        name="l1_out",
    )(zc, zs, zc, zs, c256, s256, c1k, s1k, x, mods, w_out)


def _split_hi_lo(w):
    hi = w.astype(BF16)
    lo = (w - hi.astype(F32)).astype(BF16)
    return jnp.stack([hi, lo])


def kernel(x_prompt, x_sample, cache_k, cache_v, c, c_ctx, ada_w, ada_b, norm_mix, norm_ffn, a_w_in, a_q_norm, a_k_norm, a_sink, pool_w, pool_scale, a_w_out, f_w_in, f_w_out, router_g_w, router_g_b, router_e_w, router_e_b, moe_w1, moe_w3, moe_w2):
    xp = x_prompt.reshape(N_CTX, D)
    xs = x_sample.reshape(N_LAT, D)

    cond8 = jnp.zeros((SUBLANES, D), F32).at[0].set(c_ctx).at[1:1 + N_LAT_B].set(c)
    mods = _adaln(cond8, ada_w, ada_b).reshape(DEPTH * SUBLANES, 1, 6 * D)

    tabs = _rope_tables()
    lane = np.arange(LANES)
    bd = jnp.asarray((lane[:, None] // HEAD_DIM) == (lane[None, :] // HEAD_DIM), BF16)
    tri = jnp.asarray(np.arange(TB)[:, None] < np.arange(TB)[None, :], BF16)
    c256, s256 = _dft_tables(T_CTX)
    c1k, s1k = _dft_tables(T_LAT)

    def router_operands(l):
        w = jnp.concatenate([router_e_w[l], router_g_w[l]], axis=1).T
        w = jnp.pad(w, ((0, ROUTE_ROWS - w.shape[0]), (0, 0)))
        b = jnp.concatenate([router_e_b[l], router_g_b[l]])
        b = jnp.pad(b, (0, ROUTE_ROWS - b.shape[0]))
        return _split_hi_lo(w), jnp.broadcast_to(b[:, None], (ROUTE_ROWS, LANES))

    qg = jnp.tile(a_q_norm[0], LANES // HEAD_DIM)[None, :]
    kg = jnp.tile(a_k_norm[0], LANES // HEAD_DIM)[None, :]
    q, k, v, u = _l0_in(xp, xs, mods, norm_mix[0][None, :], a_w_in[0].astype(BF16), qg, kg, bd, tabs)
    sink_b = jnp.broadcast_to(a_sink[0][:, None], (N_HEADS, LANES))
    o_ctx = _ctx_attn(q, k, v, sink_b)
    ck = cache_k[:, 0].reshape(N_LAT_B, PAST, KV_W)
    cv = cache_v[:, 0].reshape(N_LAT_B, PAST, KV_W)
    o_lat = _lat_attn(q, k, v, ck, cv, sink_b)
    x1 = _l0_out(o_ctx, o_lat, u, xp, xs, mods, pool_w[0].astype(BF16), pool_scale[0][None, :],
                 a_w_out[0].astype(BF16))
    wr, br = router_operands(0)
    x2 = _moe(x1, mods, 0, norm_ffn[0][None, :], wr, br, tri, moe_w1, moe_w3, moe_w2)

    zc, zs = _l1_in(x2, mods, norm_mix[1][None, :], f_w_in[0].astype(BF16), c256, s256)
    x3 = _l1_out(zc, zs, c256, s256, c1k, s1k, x2, mods, f_w_out[0].astype(BF16))
    wr, br = router_operands(1)
    x4 = _moe(x3, mods, 1, norm_ffn[1][None, :], wr, br, tri, moe_w1, moe_w3, moe_w2)

    new_k = k[:N_CTX].reshape(N_CTX_B, 1, T_CTX, KV_W // HEAD_DIM, HEAD_DIM)
    new_v = v[:N_CTX].reshape(N_CTX_B, 1, T_CTX, KV_W // HEAD_DIM, HEAD_DIM)
    return (x4[:N_CTX].reshape(N_CTX_B, T_CTX, D), x4[N_CTX:N_TOK].reshape(N_LAT_B, T_LAT, D),
            new_k, new_v)
```

```python
import functools

import numpy as np
import jax
import jax.numpy as jnp
from jax import lax
from jax.experimental import pallas as pl
from jax.experimental.pallas import tpu as pltpu

F32 = jnp.float32
BF16 = jnp.bfloat16
I32 = jnp.int32

D = 1024
DEPTH = 2
N_CTX_B, T_CTX = 16, 256
N_LAT_B, T_LAT = 2, 1024
N_CTX = N_CTX_B * T_CTX
N_LAT = N_LAT_B * T_LAT
N_TOK = N_CTX + N_LAT
PAST = 512
GRID_W = 64
HEAD_DIM = 64
N_HEADS = 8
ATTN_W = 512
KV_W = 128
POOL_W = 512
POOL_WINDOWS = (2, 4, 8, 16)
MIX_IN = ATTN_W + 2 * KV_W + POOL_W
WINDOW = 128
N_GROUPS = 4
PER_GROUP = 4
N_EXPERTS = 16
D_EXPERT = 512
ROPE_THETA = 10000.0
EPS = 1e-6
NEG = -1e30

LANES = 128
SUBLANES = 8
TB = 256
NT = N_TOK // TB
NT_CTX = N_CTX // TB
TB_MIX = 1024
TM = 256
PAIRS = 6
N_CLASS = N_GROUPS * PAIRS
CLASS_ROWS = 32
NT_FFN = N_TOK // TM + N_CLASS
P_FFN = NT_FFN * TM
XEXT = D + LANES
ROUTE_ROWS = 32

VMEM_LIMIT = 56 * 1024 * 1024


def _cparams(n_axes=1, vmem=None):
    return pltpu.CompilerParams(dimension_semantics=("arbitrary",) * n_axes,
                                vmem_limit_bytes=vmem)


def _modulate(x, g, shift, scale):
    ms = jnp.mean(x * x, axis=-1, keepdims=True)
    return (x * lax.rsqrt(ms + EPS) * g) * (1.0 + scale) + shift


def _mod_row(tile, tiles_ctx, tiles_per_lat):
    return (tile >= tiles_ctx).astype(I32) + (tile >= tiles_ctx + tiles_per_lat).astype(I32)


def _rope_tables():
    t = np.arange(T_LAT)
    row = (t // GRID_W).astype(np.float64)
    col = (t % GRID_W).astype(np.float64)
    nf = HEAD_DIM // 4
    freqs = ROPE_THETA ** (-np.arange(nf, dtype=np.float64) / nf)
    d = np.arange(HEAD_DIM)
    pos = np.where(d[None, :] < HEAD_DIM // 2, row[:, None], col[:, None])
    ang = pos * freqs[d % nf][None, :]
    first = (d % (HEAD_DIM // 2)) < nf
    cos = np.cos(ang)
    sin_a = np.where(first[None, :], -np.sin(ang), 0.0)
    sin_b = np.where(first[None, :], 0.0, np.sin(ang))
    ident = (np.ones((TB, HEAD_DIM)), np.zeros((TB, HEAD_DIM)), np.zeros((TB, HEAD_DIM)))
    out = []
    for tab, idt in zip((cos, sin_a, sin_b), ident):
        full = np.concatenate([tab, idt], axis=0)
        out.append(jnp.asarray(np.tile(full, (1, LANES // HEAD_DIM)), F32))
    return out


def _dft_tables(t):
    m = np.outer(np.arange(t), np.arange(t)) % t
    ang = 2.0 * np.pi * m / t
    s = 1.0 / np.sqrt(t)
    return jnp.asarray(np.cos(ang) * s, F32).astype(BF16), jnp.asarray(np.sin(ang) * s, F32).astype(BF16)


def _adaln_body(cond_ref, w_ref, b_ref, o_ref):
    c = cond_ref[...]
    s = (c * jax.nn.sigmoid(c)).astype(BF16)
    o_ref[0] = jnp.dot(s, w_ref[0].astype(BF16), preferred_element_type=F32) + b_ref[0]


def _adaln(cond8, ada_w, ada_b):
    tn = 1536
    return pl.pallas_call(
        _adaln_body,
        grid=(DEPTH, 6 * D // tn),
        in_specs=[pl.BlockSpec((SUBLANES, D), lambda l, j: (0, 0)),
                  pl.BlockSpec((1, D, tn), lambda l, j: (l, 0, j)),
                  pl.BlockSpec((1, 1, tn), lambda l, j: (l, 0, j))],
        out_specs=pl.BlockSpec((1, SUBLANES, tn), lambda l, j: (l, 0, j)),
        out_shape=jax.ShapeDtypeStruct((DEPTH, SUBLANES, 6 * D), F32),
        compiler_params=_cparams(2),
        name="adaln",
    )(cond8, ada_w, ada_b.reshape(DEPTH, 1, 6 * D))


def _l0_in_body(xp_ref, xs_ref, mod_ref, g_ref, w_ref, qg_ref, kg_ref, bd_ref,
                cos_ref, sa_ref, sb_ref, q_ref, k_ref, v_ref, u_ref):
    i = pl.program_id(0)
    x = jnp.where(i < NT_CTX, xp_ref[...], xs_ref[...])
    m = mod_ref[0]
    h = _modulate(x, g_ref[...], m[:, 0:D], m[:, D:2 * D])
    z = jnp.dot(h.astype(BF16), w_ref[...], preferred_element_type=F32)
    cos, sa, sb, bd = cos_ref[...], sa_ref[...], sb_ref[...], bd_ref[...]

    def head_norm_rope(zz, gain):
        ss = jnp.dot((zz * zz).astype(BF16), bd, preferred_element_type=F32)
        y = zz * lax.rsqrt(ss * (1.0 / HEAD_DIM) + EPS) * gain
        return (y * cos + pltpu.roll(y, LANES - 16, axis=1) * sa
                + pltpu.roll(y, 16, axis=1) * sb)

    for s in range(ATTN_W // LANES):
        qs = head_norm_rope(z[:, s * LANES:(s + 1) * LANES], qg_ref[...])
        q_ref[:, s * LANES:(s + 1) * LANES] = (qs * (HEAD_DIM ** -0.5)).astype(BF16)
    k_ref[...] = head_norm_rope(z[:, ATTN_W:ATTN_W + KV_W], kg_ref[...])
    v_ref[...] = z[:, ATTN_W + KV_W:ATTN_W + 2 * KV_W]
    u_ref[...] = z[:, ATTN_W + 2 * KV_W:MIX_IN]


def _l0_in(xp, xs, mods, g, w_in, qg, kg, bd, tabs):
    tab_spec = pl.BlockSpec(
        (TB, LANES), lambda i: (jnp.where(i < NT_CTX, T_LAT // TB, (i - NT_CTX) % (T_LAT // TB)), 0))
    const = lambda shape: pl.BlockSpec(shape, lambda i: (0,) * len(shape))
    return pl.pallas_call(
        _l0_in_body,
        grid=(NT,),
        in_specs=[pl.BlockSpec((TB, D), lambda i: (jnp.minimum(i, NT_CTX - 1), 0)),
                  pl.BlockSpec((TB, D), lambda i: (jnp.maximum(i - NT_CTX, 0), 0)),
                  pl.BlockSpec((1, 1, 6 * D), lambda i: (_mod_row(i, NT_CTX, T_LAT // TB), 0, 0)),
                  const((1, D)), const((D, MIX_IN)), const((1, LANES)), const((1, LANES)),
                  const((LANES, LANES)), tab_spec, tab_spec, tab_spec],
        out_specs=[pl.BlockSpec((TB, ATTN_W), lambda i: (i, 0)),
                   pl.BlockSpec((TB, KV_W), lambda i: (i, 0)),
                   pl.BlockSpec((TB, KV_W), lambda i: (i, 0)),
                   pl.BlockSpec((TB, POOL_W), lambda i: (i, 0))],
        out_shape=[jax.ShapeDtypeStruct((N_TOK, ATTN_W), BF16),
                   jax.ShapeDtypeStruct((N_TOK, KV_W), F32),
                   jax.ShapeDtypeStruct((N_TOK, KV_W), F32),
                   jax.ShapeDtypeStruct((N_TOK, POOL_W), F32)],
        compiler_params=_cparams(1),
        name="l0_in",
    )(xp, xs, mods, g, w_in, qg, kg, bd, *tabs)


def _head_halves(x):
    z = jnp.zeros_like(x)
    return jnp.concatenate([x, z], axis=1), jnp.concatenate([z, x], axis=1)


_NT_DIMS = (((1,), (1,)), ((), ()))


def _sink_softmax(scores, sk):
    mx = sk
    for sc in scores:
        mx = jnp.maximum(mx, jnp.max(sc, axis=-1, keepdims=True))
    ps = [jnp.exp(sc - mx) for sc in scores]
    den = jnp.exp(sk - mx)
    for p in ps:
        den = den + jnp.sum(p, axis=-1, keepdims=True)
    inv = 1.0 / den
    return [(p * inv).astype(BF16) for p in ps]


def _sink_col(sink_ref, heads, rows):
    return jnp.concatenate([jnp.broadcast_to(sink_ref[h:h + 1, 0:1], (rows, 1)) for h in heads], axis=0)


def _ctx_attn_body(q_ref, k_ref, v_ref, sink_ref, o_ref):
    k = k_ref[...].astype(BF16)
    v = v_ref[...].astype(BF16)
    lo = lax.broadcasted_iota(I32, (T_CTX, LANES), 1) < HEAD_DIM
    for j in range(KV_W // HEAD_DIM):
        kj = k[:, j * HEAD_DIM:(j + 1) * HEAD_DIM]
        vj = v[:, j * HEAD_DIM:(j + 1) * HEAD_DIM]
        k_halves = _head_halves(kj)
        vd = jnp.concatenate([vj, vj], axis=1)
        q2 = jnp.concatenate([q_ref[:, (2 * j) * LANES:(2 * j + 1) * LANES],
                              q_ref[:, (2 * j + 1) * LANES:(2 * j + 2) * LANES]], axis=0)
        outs = []
        for half in range(2):
            sc = lax.dot_general(q2, k_halves[half], _NT_DIMS, preferred_element_type=F32)
            sk = _sink_col(sink_ref, (4 * j + half, 4 * j + 2 + half), T_CTX)
            (p,) = _sink_softmax([sc], sk)
            outs.append(jnp.dot(p, vd, preferred_element_type=F32))
        for s2 in range(2):
            rows = slice(s2 * T_CTX, (s2 + 1) * T_CTX)
            o_ref[:, (2 * j + s2) * LANES:(2 * j + s2 + 1) * LANES] = (
                jnp.where(lo, outs[0][rows], outs[1][rows]).astype(BF16))


def _ctx_attn(q, k, v, sink_b):
    return pl.pallas_call(
        _ctx_attn_body,
        grid=(N_CTX_B,),
        in_specs=[pl.BlockSpec((T_CTX, ATTN_W), lambda b: (b, 0)),
                  pl.BlockSpec((T_CTX, KV_W), lambda b: (b, 0)),
                  pl.BlockSpec((T_CTX, KV_W), lambda b: (b, 0)),
                  pl.BlockSpec((SUBLANES, LANES), lambda b: (0, 0))],
        out_specs=pl.BlockSpec((T_CTX, ATTN_W), lambda b: (b, 0)),
        out_shape=jax.ShapeDtypeStruct((N_CTX, ATTN_W), BF16),
        compiler_params=_cparams(1),
        name="ctx_attn",
    )(q, k, v, sink_b)


QB = 128
SPAN = QB + 2 * WINDOW


def _lat_attn_body(q_ref, k_ref, v_ref, ck_ref, cv_ref, sink_ref, o_ref):
    qb = pl.program_id(1)
    start = qb * QB
    kws, vws = [], []
    for c in (-1, 0, 1):
        cs = pl.multiple_of(jnp.clip(start + c * QB, 0, T_LAT - QB), QB)
        kws.append(k_ref[pl.ds(cs, QB), :])
        vws.append(v_ref[pl.ds(cs, QB), :])
    kw = jnp.concatenate(kws, axis=0).astype(BF16)
    vw = jnp.concatenate(vws, axis=0).astype(BF16)
    ck = ck_ref[0].astype(BF16)
    cv = cv_ref[0].astype(BF16)
    qpos = start + (lax.broadcasted_iota(I32, (2 * QB, SPAN), 0) & (QB - 1))
    kpos = start - WINDOW + lax.broadcasted_iota(I32, (2 * QB, SPAN), 1)
    valid = (kpos >= 0) & (kpos < T_LAT) & (jnp.abs(qpos - kpos) <= WINDOW)
    lo = lax.broadcasted_iota(I32, (QB, LANES), 1) < HEAD_DIM
    for j in range(KV_W // HEAD_DIM):
        sl = slice(j * HEAD_DIM, (j + 1) * HEAD_DIM)
        kw_halves = _head_halves(kw[:, sl])
        ck_halves = _head_halves(ck[:, sl])
        vwd = jnp.concatenate([vw[:, sl], vw[:, sl]], axis=1)
        cvd = jnp.concatenate([cv[:, sl], cv[:, sl]], axis=1)
        q2 = jnp.concatenate([q_ref[:, (2 * j) * LANES:(2 * j + 1) * LANES],
                              q_ref[:, (2 * j + 1) * LANES:(2 * j + 2) * LANES]], axis=0)
        outs = []
        for half in range(2):
            s_win = lax.dot_general(q2, kw_halves[half], _NT_DIMS, preferred_element_type=F32)
            s_win = jnp.where(valid, s_win, NEG)
            s_ctx = lax.dot_general(q2, ck_halves[half], _NT_DIMS, preferred_element_type=F32)
            sk = _sink_col(sink_ref, (4 * j + half, 4 * j + 2 + half), QB)
            pw, pc = _sink_softmax([s_win, s_ctx], sk)
            outs.append(jnp.dot(pw, vwd, preferred_element_type=F32)
                        + jnp.dot(pc, cvd, preferred_element_type=F32))
        for s2 in range(2):
            rows = slice(s2 * QB, (s2 + 1) * QB)
            o_ref[:, (2 * j + s2) * LANES:(2 * j + s2 + 1) * LANES] = (
                jnp.where(lo, outs[0][rows], outs[1][rows]).astype(BF16))


def _lat_attn(q, k, v, ck, cv, sink_b):
    lat0 = N_CTX // T_LAT
    return pl.pallas_call(
        _lat_attn_body,
        grid=(N_LAT_B, T_LAT // QB),
        in_specs=[pl.BlockSpec((QB, ATTN_W), lambda b, i: (N_CTX // QB + b * (T_LAT // QB) + i, 0)),
                  pl.BlockSpec((T_LAT, KV_W), lambda b, i: (lat0 + b, 0)),
                  pl.BlockSpec((T_LAT, KV_W), lambda b, i: (lat0 + b, 0)),
                  pl.BlockSpec((1, PAST, KV_W), lambda b, i: (b, 0, 0)),
                  pl.BlockSpec((1, PAST, KV_W), lambda b, i: (b, 0, 0)),
                  pl.BlockSpec((SUBLANES, LANES), lambda b, i: (0, 0))],
        out_specs=pl.BlockSpec((QB, ATTN_W), lambda b, i: (b * (T_LAT // QB) + i, 0)),
        out_shape=jax.ShapeDtypeStruct((N_LAT, ATTN_W), BF16),
        compiler_params=_cparams(2),
        name="lat_attn",
    )(q, k, v, ck, cv, sink_b)


def _l0_out_body(oc_ref, ol_ref, u_ref, xp_ref, xs_ref, mod_ref, pw_ref, ps_ref, wo_ref, x1_ref):
    i = pl.program_id(0)
    is_ctx = i < N_CTX // TB_MIX
    o = jnp.where(is_ctx, oc_ref[...], ol_ref[...])
    x = jnp.where(is_ctx, xp_ref[...], xs_ref[...])
    tseq = jnp.where(is_ctx, T_CTX, T_LAT)
    pos = lax.broadcasted_iota(I32, (TB_MIX, LANES), 0) & (tseq - 1)
    ys = []
    for g, win in enumerate(POOL_WINDOWS):
        hw = win // 2
        ug = u_ref[:, g * LANES:(g + 1) * LANES]
        acc = ug
        for jj in range(-hw, hw):
            if jj == 0:
                continue
            sh = pltpu.roll(ug, (-jj) % TB_MIX, axis=0)
            ok = (pos + jj >= 0) if jj < 0 else (pos + jj < tseq)
            acc = acc + jnp.where(ok, sh, 0.0)
        cnt = (jnp.minimum(pos + hw, tseq) - jnp.maximum(pos - hw, 0)).astype(F32)
        pooled = acc / cnt - ug
        ys.append(jnp.dot(pooled.astype(BF16), pw_ref[g], preferred_element_type=F32))
    y = jnp.concatenate(ys, axis=1) * ps_ref[...]
    out = (jnp.dot(o, wo_ref[0:ATTN_W, :], preferred_element_type=F32)
           + jnp.dot(y.astype(BF16), wo_ref[ATTN_W:ATTN_W + POOL_W, :], preferred_element_type=F32))
    x1_ref[...] = x + mod_ref[0][:, 2 * D:3 * D] * out


def _l0_out(o_ctx, o_lat, u, xp, xs, mods, pool_w, pool_scale, w_out):
    ntc = N_CTX // TB_MIX
    const = lambda shape: pl.BlockSpec(shape, lambda i: (0,) * len(shape))
    ctx_map = lambda i: (jnp.minimum(i, ntc - 1), 0)
    lat_map = lambda i: (jnp.maximum(i - ntc, 0), 0)
    return pl.pallas_call(
        _l0_out_body,
        grid=(N_TOK // TB_MIX,),
        in_specs=[pl.BlockSpec((TB_MIX, ATTN_W), ctx_map),
                  pl.BlockSpec((TB_MIX, ATTN_W), lat_map),
                  pl.BlockSpec((TB_MIX, POOL_W), lambda i: (i, 0)),
                  pl.BlockSpec((TB_MIX, D), ctx_map),
                  pl.BlockSpec((TB_MIX, D), lat_map),
                  pl.BlockSpec((1, 1, 6 * D), lambda i: (_mod_row(i, ntc, 1), 0, 0)),
                  const((len(POOL_WINDOWS), LANES, LANES)), const((1, POOL_W)), const((D, D))],
        out_specs=pl.BlockSpec((TB_MIX, D), lambda i: (i, 0)),
        out_shape=jax.ShapeDtypeStruct((N_TOK, D), F32),
        compiler_params=_cparams(1, VMEM_LIMIT),
        name="l0_out",
    )(o_ctx, o_lat, u, xp, xs, mods, pool_w, pool_scale, w_out)


def _first_max(vals):
    best, idx = vals[0], jnp.zeros(vals[0].shape, I32)
    for r in range(1, len(vals)):
        better = vals[r] > best
        idx = jnp.where(better, r, idx)
        best = jnp.where(better, vals[r], best)
    return best, idx


def _softmax_rows(rows):
    mx = functools.reduce(jnp.maximum, rows)
    ex = [jnp.exp(r - mx) for r in rows]
    tot = functools.reduce(lambda a, b: a + b, ex)
    return [e / tot for e in ex]


def _route_body(x_ref, mod_ref, g_ref, wr_ref, br_ref, tri_ref,
                xext_ref, info_ref, cnt_ref, base_ref):
    i = pl.program_id(0)

    @pl.when(i == 0)
    def _():
        base_ref[...] = jnp.zeros_like(base_ref)

    x = x_ref[...]
    m = mod_ref[0]
    h = _modulate(x, g_ref[...], m[:, 3 * D:4 * D], m[:, 4 * D:5 * D])

    hh = h.astype(BF16)
    hl = (h - hh.astype(F32)).astype(BF16)
    wh, wl = wr_ref[0], wr_ref[1]
    lg = (lax.dot_general(wh, hh, _NT_DIMS, preferred_element_type=F32)
          + lax.dot_general(wl, hh, _NT_DIMS, preferred_element_type=F32)
          + lax.dot_general(wh, hl, _NT_DIMS, preferred_element_type=F32)) + br_ref[:, 0:1]

    pg = _softmax_rows([lg[N_EXPERTS + r:N_EXPERTS + r + 1] for r in range(N_GROUPS)])
    pg_top, gi = _first_max(pg)
    le = []
    for j in range(PER_GROUP):
        sel = lg[(N_GROUPS - 1) * PER_GROUP + j:(N_GROUPS - 1) * PER_GROUP + j + 1]
        for g in range(N_GROUPS - 2, -1, -1):
            sel = jnp.where(gi == g, lg[g * PER_GROUP + j:g * PER_GROUP + j + 1], sel)
        le.append(sel)
    pe = _softmax_rows(le)
    p1, i1 = _first_max(pe)
    p2, i2 = _first_max([jnp.where(i1 == j, -1.0, pe[j]) for j in range(PER_GROUP)])
    den = p1 + p2
    w1 = pg_top * p1 / den
    w2 = pg_top * p2 / den

    lo = jnp.minimum(i1, i2)
    hi = jnp.maximum(i1, i2)
    cls = gi * PAIRS + jnp.where(lo == 0, 0, jnp.where(lo == 1, 3, 5)) + hi - lo - 1
    w_lo = jnp.where(i1 == lo, w1, w2)
    w_hi = jnp.where(i1 == lo, w2, w1)

    crow = lax.broadcasted_iota(I32, (CLASS_ROWS, TB), 0)
    hit = crow == cls
    onehot = jnp.where(hit, 1.0, 0.0)
    before = jnp.dot(onehot.astype(BF16), tri_ref[...], preferred_element_type=F32)
    before = before + base_ref[:, 0:1]
    rank = jnp.sum(jnp.where(hit, before, 0.0), axis=0, keepdims=True)
    base_ref[...] = base_ref[...] + jnp.sum(onehot, axis=1, keepdims=True)
    cnt_ref[...] = base_ref[...]

    mod_id = jnp.zeros_like(w1) + _mod_row(i, NT_CTX, T_LAT // TB).astype(F32)
    zero = jnp.zeros_like(w1)
    info_ref[...] = jnp.concatenate([cls.astype(F32), rank, zero, zero, zero, zero, zero, zero], axis=0)
    side = jnp.concatenate([w_lo, w_hi, mod_id, jnp.zeros((LANES - 3, TB), F32)], axis=0).T
    xext_ref[:, 0:D] = x
    xext_ref[:, D:XEXT] = side


def _route(x, mods, layer, g, wr, br, tri):
    const = lambda shape: pl.BlockSpec(shape, lambda i: (0,) * len(shape))
    return pl.pallas_call(
        _route_body,
        grid=(NT,),
        in_specs=[pl.BlockSpec((TB, D), lambda i: (i, 0)),
                  pl.BlockSpec((1, 1, 6 * D),
                               lambda i: (layer * SUBLANES + _mod_row(i, NT_CTX, T_LAT // TB), 0, 0)),
                  const((1, D)), const((2, ROUTE_ROWS, D)), const((ROUTE_ROWS, LANES)),
                  const((TB, TB))],
        out_specs=[pl.BlockSpec((TB, XEXT), lambda i: (i, 0)),
                   pl.BlockSpec((SUBLANES, TB), lambda i: (0, i)),
                   pl.BlockSpec((CLASS_ROWS, LANES), lambda i: (0, 0))],
        out_shape=[jax.ShapeDtypeStruct((N_TOK, XEXT), F32),
                   jax.ShapeDtypeStruct((SUBLANES, N_TOK), F32),
                   jax.ShapeDtypeStruct((CLASS_ROWS, LANES), F32)],
        scratch_shapes=[pltpu.VMEM((CLASS_ROWS, LANES), F32)],
        compiler_params=_cparams(1),
        name=f"route{layer}",
    )(x, mods, g, wr, br, tri)


def _plan(info, counts):
    cls = info[0].astype(I32)
    rank = info[1].astype(I32)
    cnt = counts[:N_CLASS, 0].astype(I32)
    tiles = (cnt + TM - 1) // TM
    tend = jnp.cumsum(tiles)
    tstart = tend - tiles
    n_used = tend[-1]
    cidx = jnp.arange(N_CLASS, dtype=I32)
    pos = jnp.sum(jnp.where(cls[:, None] == cidx, tstart * TM, 0), axis=-1) + rank
    n = jnp.arange(NT_FFN + 1, dtype=I32)
    tile = jnp.minimum(n, jnp.maximum(n_used - 1, 0))
    tcls = jnp.minimum(jnp.sum((tile[:, None] >= tend[None, :]).astype(I32), axis=1), N_CLASS - 1)
    of_cls = tcls[:, None] == cidx
    rows = jnp.sum(jnp.where(of_cls, cnt, 0), axis=1) - (tile - jnp.sum(jnp.where(of_cls, tstart, 0), axis=1)) * TM
    rows = jnp.where(n < n_used, jnp.clip(rows, 0, TM), 0)
    pair = tcls % PAIRS
    lo = (pair >= 3).astype(I32) + (pair >= 5).astype(I32)
    hi = jnp.where(pair < 3, pair + 1, jnp.where(pair < 5, pair - 1, 3))
    e_lo = (tcls // PAIRS) * PER_GROUP + lo
    e_hi = (tcls // PAIRS) * PER_GROUP + hi
    return pos + TM, e_lo, e_hi, n_used.reshape(1), rows


OCT = TM // SUBLANES
N_STAGE = 4
STAGE_ROWS = TM // N_STAGE
P_MAP = (NT_FFN + 2) * TM


def _ffn_body(pos_ref, elo_ref, ehi_ref, nu_ref, rows_ref,
              xext_hbm, mod_ref, g_ref, w1a_ref, w1b_ref, w3a_ref, w3b_ref, w2a_ref, w2b_ref,
              out_hbm, src_ref, dst_ref, xbuf, ybuf, wb1a, wb1b, wb3a, wb3b, wb2a, wb2b, gsem, ssem):
    n = pl.program_id(0)
    n_used = nu_ref[0]

    def gather_start(tile, s, r):
        pltpu.make_async_copy(xext_hbm.at[pl.ds(src_ref[(tile + 1) * TM + r], 1)],
                              xbuf.at[s, r // SUBLANES, pl.ds(r % SUBLANES, 1)], gsem.at[s]).start()

    def scatter_start(tile, s, r):
        pltpu.make_async_copy(ybuf.at[s, r // SUBLANES, pl.ds(r % SUBLANES, 1)],
                              out_hbm.at[pl.ds(dst_ref[(tile + 1) * TM + r], 1)], ssem.at[s]).start()

    def wait_tile(buf, sem, s):
        pltpu.make_async_copy(buf.at[s], buf.at[s], sem.at[s]).wait()

    @pl.when(n == 0)
    def _():
        def pad_rows(tile, carry):
            def one(r, c):
                src_ref[(tile + 1) * TM + r] = 0
                dst_ref[(tile + 1) * TM + r] = N_TOK + (tile & 1) * TM + r
                return c
            lax.fori_loop(rows_ref[tile], TM, one, 0)
            return carry

        lax.fori_loop(0, n_used + 1, pad_rows, 0)

        def spare(r, c):
            dst_ref[r] = N_TOK + 2 * TM + r
            return c

        lax.fori_loop(0, TM, spare, 0, unroll=8)

        def put(t, carry):
            p = pos_ref[t]
            src_ref[p] = t
            dst_ref[p] = t
            return carry

        lax.fori_loop(0, N_TOK, put, 0, unroll=8)

        xbuf[...] = jnp.zeros_like(xbuf)
        ybuf[...] = jnp.zeros_like(ybuf)
        for s in range(2):
            dumps = [pltpu.make_async_copy(
                ybuf.at[s, q], out_hbm.at[pl.ds(N_TOK + s * TM + q * SUBLANES, SUBLANES)], ssem.at[s])
                for q in range(OCT)]
            for dump in dumps:
                dump.start()
            for dump in dumps:
                dump.wait()
        for r in range(TM):
            gather_start(0, 0, r)

    def step(slot):
        wait_tile(xbuf, gsem, slot)

        @pl.when(n >= 1)
        def _():
            wait_tile(ybuf, ssem, slot)

        prev = jnp.maximum(n - 1, 0)

        @pl.when((n == 0) | (elo_ref[n] != elo_ref[prev]))
        def _():
            wb1a[...] = w1a_ref[0, 0].astype(BF16)
            wb3a[...] = w3a_ref[0, 0].astype(BF16)
            wb2a[...] = w2a_ref[0, 0].astype(BF16)

        @pl.when((n == 0) | (ehi_ref[n] != ehi_ref[prev]))
        def _():
            wb1b[...] = w1b_ref[0, 0].astype(BF16)
            wb3b[...] = w3b_ref[0, 0].astype(BF16)
            wb2b[...] = w2b_ref[0, 0].astype(BF16)

        def issue(stage):
            for r in range(stage * STAGE_ROWS, (stage + 1) * STAGE_ROWS):
                gather_start(n + 1, 1 - slot, r)
                scatter_start(n - 1, 1 - slot, r)

        xe = xbuf[slot].reshape(TM, XEXT)
        x = xe[:, 0:D]
        w_lo = xe[:, D:D + 1]
        w_hi = xe[:, D + 1:D + 2]
        mod_id = xe[:, D + 2:D + 3]

        def pick(lo, hi):
            return jnp.where(mod_id < 0.5, mod_ref[0][:, lo:hi],
                             jnp.where(mod_id < 1.5, mod_ref[1][:, lo:hi], mod_ref[2][:, lo:hi]))

        h = _modulate(x, g_ref[...], pick(3 * D, 4 * D), pick(4 * D, 5 * D)).astype(BF16)

        def act(wb1, wb3, w):
            h1 = jnp.dot(h, wb1[...], preferred_element_type=F32)
            h3 = jnp.dot(h, wb3[...], preferred_element_type=F32)
            return ((h1 * jax.nn.sigmoid(h1)) * h3 * w).astype(BF16)

        issue(0)
        a_lo = act(wb1a, wb3a, w_lo)
        issue(1)
        a_hi = act(wb1b, wb3b, w_hi)
        issue(2)
        y = (jnp.dot(a_lo, wb2a[...], preferred_element_type=F32)
             + jnp.dot(a_hi, wb2b[...], preferred_element_type=F32))
        ybuf[slot] = (x + pick(5 * D, 6 * D) * y).reshape(OCT, SUBLANES, D)
        issue(3)

        @pl.when(n == n_used - 1)
        def _():
            for r in range(TM):
                scatter_start(n, slot, r)
            wait_tile(ybuf, ssem, 1 - slot)
            wait_tile(ybuf, ssem, slot)
            wait_tile(xbuf, gsem, 1 - slot)

    for s in range(2):
        @pl.when((n < n_used) & (n % 2 == s))
        def _():
            step(s)


def _ffn(pos, e_lo, e_hi, n_used, rows, xext, mods, layer, g, w1, w3, w2):
    lo_map = lambda n, p, el, eh, nu, ch: (layer, el[n], 0, 0)
    hi_map = lambda n, p, el, eh, nu, ch: (layer, eh[n], 0, 0)
    up = lambda imap: pl.BlockSpec((1, 1, D, D_EXPERT), imap)
    down = lambda imap: pl.BlockSpec((1, 1, D_EXPERT, D), imap)
    return pl.pallas_call(
        _ffn_body,
        grid_spec=pltpu.PrefetchScalarGridSpec(
            num_scalar_prefetch=5, grid=(NT_FFN,),
            in_specs=[pl.BlockSpec(memory_space=pl.ANY),
                      pl.BlockSpec((SUBLANES, 1, 6 * D), lambda n, *_: (layer, 0, 0)),
                      pl.BlockSpec((1, D), lambda n, *_: (0, 0)),
                      up(lo_map), up(hi_map), up(lo_map), up(hi_map), down(lo_map), down(hi_map)],
            out_specs=pl.BlockSpec(memory_space=pl.ANY),
            scratch_shapes=[pltpu.SMEM((P_MAP,), I32), pltpu.SMEM((P_MAP,), I32),
                            pltpu.VMEM((2, OCT, SUBLANES, XEXT), F32),
                            pltpu.VMEM((2, OCT, SUBLANES, D), F32),
                            pltpu.VMEM((D, D_EXPERT), BF16), pltpu.VMEM((D, D_EXPERT), BF16),
                            pltpu.VMEM((D, D_EXPERT), BF16), pltpu.VMEM((D, D_EXPERT), BF16),
                            pltpu.VMEM((D_EXPERT, D), BF16), pltpu.VMEM((D_EXPERT, D), BF16),
                            pltpu.SemaphoreType.DMA((2,)), pltpu.SemaphoreType.DMA((2,))]),
        out_shape=jax.ShapeDtypeStruct((N_TOK + 3 * TM, D), F32),
        compiler_params=_cparams(1, VMEM_LIMIT),
        name=f"ffn{layer}",
    )(pos, e_lo, e_hi, n_used, rows, xext, mods, g, w1, w1, w3, w3, w2, w2)


def _moe(x, mods, layer, g, wr, br, tri, w1, w3, w2):
    xext, info, counts = _route(x, mods, layer, g, wr, br, tri)
    pos, e_lo, e_hi, n_used, rows = _plan(info, counts)
    return _ffn(pos, e_lo, e_hi, n_used, rows, xext, mods, layer, g, w1, w3, w2)


FG = 256


def _l1_in_body(x_ref, mod_ref, g_ref, w_ref, c_ref, s_ref, zc_ref, zs_ref):
    m = mod_ref[0]
    h = _modulate(x_ref[...], g_ref[...], m[:, 0:D], m[:, D:2 * D])
    z = jnp.dot(h.astype(BF16), w_ref[...], preferred_element_type=F32).astype(BF16)
    for g in range(D // FG):
        zg = z[:, g * FG:(g + 1) * FG]
        zc_ref[:, g * FG:(g + 1) * FG] = jnp.dot(zg, c_ref[...], preferred_element_type=F32).astype(BF16)
        zs_ref[:, g * FG:(g + 1) * FG] = jnp.dot(zg, s_ref[...], preferred_element_type=F32).astype(BF16)


def _l1_in(x, mods, g, w, c256, s256):
    const = lambda shape: pl.BlockSpec(shape, lambda i: (0,) * len(shape))
    return pl.pallas_call(
        _l1_in_body,
        grid=(NT,),
        in_specs=[pl.BlockSpec((TB, D), lambda i: (i, 0)),
                  pl.BlockSpec((1, 1, 6 * D),
                               lambda i: (SUBLANES + _mod_row(i, NT_CTX, T_LAT // TB), 0, 0)),
                  const((1, D)), const((D, D)), const((FG, FG)), const((FG, FG))],
        out_specs=[pl.BlockSpec((TB, D), lambda i: (i, 0)), pl.BlockSpec((TB, D), lambda i: (i, 0))],
        out_shape=[jax.ShapeDtypeStruct((N_TOK, D), BF16), jax.ShapeDtypeStruct((N_TOK, D), BF16)],
        compiler_params=_cparams(1),
        name="l1_in",
    )(x, mods, g, w, c256, s256)


def _l1_out_body(zc_t_ref, zs_t_ref, zc_q_ref, zs_q_ref, c256_ref, s256_ref, c1k_ref, s1k_ref,
                 x_ref, mod_ref, wo_ref, o_ref, f_ref):
    i = pl.program_id(0)

    @pl.when(i < NT_CTX)
    def _():
        f = (jnp.dot(c256_ref[...], zc_t_ref[...], preferred_element_type=F32)
             - jnp.dot(s256_ref[...], zs_t_ref[...], preferred_element_type=F32))
        f_ref[...] = f.astype(BF16)

    @pl.when(i >= NT_CTX)
    def _():
        f = (jnp.dot(c1k_ref[...], zc_q_ref[...], preferred_element_type=F32)
             - jnp.dot(s1k_ref[...], zs_q_ref[...], preferred_element_type=F32))
        f_ref[...] = f.astype(BF16)

    out = jnp.dot(f_ref[...], wo_ref[...], preferred_element_type=F32)
    o_ref[...] = x_ref[...] + mod_ref[0][:, 2 * D:3 * D] * out


def _l1_out(zc, zs, c256, s256, c1k, s1k, x, mods, w_out):
    const = lambda shape: pl.BlockSpec(shape, lambda i: (0,) * len(shape))
    tile_map = lambda i: (jnp.minimum(i, NT_CTX - 1), 0)
    seq_map = lambda i: (N_CTX // T_LAT + jnp.maximum(i - NT_CTX, 0) // (T_LAT // TB), 0)
    row_map = lambda i: (jnp.maximum(i - NT_CTX, 0) % (T_LAT // TB), 0)
    return pl.pallas_call(
        _l1_out_body,
        grid=(NT,),
        in_specs=[pl.BlockSpec((TB, D), tile_map), pl.BlockSpec((TB, D), tile_map),
                  pl.BlockSpec((T_LAT, D), seq_map), pl.BlockSpec((T_LAT, D), seq_map),
                  const((T_CTX, T_CTX)), const((T_CTX, T_CTX)),
                  pl.BlockSpec((TB, T_LAT), row_map), pl.BlockSpec((TB, T_LAT), row_map),
                  pl.BlockSpec((TB, D), lambda i: (i, 0)),
                  pl.BlockSpec((1, 1, 6 * D),
                               lambda i: (SUBLANES + _mod_row(i, NT_CTX, T_LAT // TB), 0, 0)),
                  const((D, D))],
        out_specs=pl.BlockSpec((TB, D), lambda i: (i, 0)),
        out_shape=jax.ShapeDtypeStruct((N_TOK, D), F32),
        scratch_shapes=[pltpu.VMEM((TB, D), BF16)],
        compiler_params=_cparams(1),
        name="l1_out",
    )(zc, zs, zc, zs, c256, s256, c1k, s1k, x, mods, w_out)


def _split_hi_lo(w):
    hi = w.astype(BF16)
    lo = (w - hi.astype(F32)).astype(BF16)
    return jnp.stack([hi, lo])


def kernel(x_prompt, x_sample, cache_k, cache_v, c, c_ctx, ada_w, ada_b, norm_mix, norm_ffn, a_w_in, a_q_norm, a_k_norm, a_sink, pool_w, pool_scale, a_w_out, f_w_in, f_w_out, router_g_w, router_g_b, router_e_w, router_e_b, moe_w1, moe_w3, moe_w2):
    xp = x_prompt.reshape(N_CTX, D)
    xs = x_sample.reshape(N_LAT, D)

    cond8 = jnp.zeros((SUBLANES, D), F32).at[0].set(c_ctx).at[1:1 + N_LAT_B].set(c)
    mods = _adaln(cond8, ada_w, ada_b).reshape(DEPTH * SUBLANES, 1, 6 * D)

    tabs = _rope_tables()
    lane = np.arange(LANES)
    bd = jnp.asarray((lane[:, None] // HEAD_DIM) == (lane[None, :] // HEAD_DIM), BF16)
    tri = jnp.asarray(np.arange(TB)[:, None] < np.arange(TB)[None, :], BF16)
    c256, s256 = _dft_tables(T_CTX)
    c1k, s1k = _dft_tables(T_LAT)

    def router_operands(l):
        w = jnp.concatenate([router_e_w[l], router_g_w[l]], axis=1).T
        w = jnp.pad(w, ((0, ROUTE_ROWS - w.shape[0]), (0, 0)))
        b = jnp.concatenate([router_e_b[l], router_g_b[l]])
        b = jnp.pad(b, (0, ROUTE_ROWS - b.shape[0]))
        return _split_hi_lo(w), jnp.broadcast_to(b[:, None], (ROUTE_ROWS, LANES))

    qg = jnp.tile(a_q_norm[0], LANES // HEAD_DIM)[None, :]
    kg = jnp.tile(a_k_norm[0], LANES // HEAD_DIM)[None, :]
    q, k, v, u = _l0_in(xp, xs, mods, norm_mix[0][None, :], a_w_in[0].astype(BF16), qg, kg, bd, tabs)
    sink_b = jnp.broadcast_to(a_sink[0][:, None], (N_HEADS, LANES))
    o_ctx = _ctx_attn(q, k, v, sink_b)
    ck = cache_k[:, 0].reshape(N_LAT_B, PAST, KV_W)
    cv = cache_v[:, 0].reshape(N_LAT_B, PAST, KV_W)
    o_lat = _lat_attn(q, k, v, ck, cv, sink_b)
    x1 = _l0_out(o_ctx, o_lat, u, xp, xs, mods, pool_w[0].astype(BF16), pool_scale[0][None, :],
                 a_w_out[0].astype(BF16))
    wr, br = router_operands(0)
    x2 = _moe(x1, mods, 0, norm_ffn[0][None, :], wr, br, tri, moe_w1, moe_w3, moe_w2)

    zc, zs = _l1_in(x2, mods, norm_mix[1][None, :], f_w_in[0].astype(BF16), c256, s256)
    x3 = _l1_out(zc, zs, c256, s256, c1k, s1k, x2, mods, f_w_out[0].astype(BF16))
    wr, br = router_operands(1)
    x4 = _moe(x3, mods, 1, norm_ffn[1][None, :], wr, br, tri, moe_w1, moe_w3, moe_w2)

    new_k = k[:N_CTX].reshape(N_CTX_B, 1, T_CTX, KV_W // HEAD_DIM, HEAD_DIM)
    new_v = v[:N_CTX].reshape(N_CTX_B, 1, T_CTX, KV_W // HEAD_DIM, HEAD_DIM)
    return (x4[:N_CTX].reshape(N_CTX_B, T_CTX, D), x4[N_CTX:N_TOK].reshape(N_LAT_B, T_LAT, D),
            new_k, new_v)
```

```python
import functools

import numpy as np
import jax
import jax.numpy as jnp
from jax import lax
from jax.experimental import pallas as pl
from jax.experimental.pallas import tpu as pltpu

F32 = jnp.float32
BF16 = jnp.bfloat16
I32 = jnp.int32

D = 1024
DEPTH = 2
N_CTX_B, T_CTX = 16, 256
N_LAT_B, T_LAT = 2, 1024
N_CTX = N_CTX_B * T_CTX
N_LAT = N_LAT_B * T_LAT
N_TOK = N_CTX + N_LAT
PAST = 512
GRID_W = 64
HEAD_DIM = 64
N_HEADS = 8
ATTN_W = 512
KV_W = 128
POOL_W = 512
POOL_WINDOWS = (2, 4, 8, 16)
MIX_IN = ATTN_W + 2 * KV_W + POOL_W
WINDOW = 128
N_GROUPS = 4
PER_GROUP = 4
N_EXPERTS = 16
D_EXPERT = 512
ROPE_THETA = 10000.0
EPS = 1e-6
NEG = -1e30

LANES = 128
SUBLANES = 8
TB = 512
NT = N_TOK // TB
NT_CTX = N_CTX // TB
TB_MIX = 1024
TM = 256
PAIRS = 6
N_CLASS = N_GROUPS * PAIRS
CLASS_ROWS = 32
NT_FFN = N_TOK // TM + N_CLASS
P_FFN = NT_FFN * TM
XEXT = D + LANES
ROUTE_ROWS = 32

VMEM_LIMIT = 56 * 1024 * 1024


def _cparams(n_axes=1, vmem=None):
    return pltpu.CompilerParams(dimension_semantics=("arbitrary",) * n_axes,
                                vmem_limit_bytes=vmem)


def _modulate(x, g, shift, scale):
    ms = jnp.mean(x * x, axis=-1, keepdims=True)
    return (x * lax.rsqrt(ms + EPS) * g) * (1.0 + scale) + shift


def _mod_row(tile, tiles_ctx, tiles_per_lat):
    return (tile >= tiles_ctx).astype(I32) + (tile >= tiles_ctx + tiles_per_lat).astype(I32)


def _rope_tables():
    t = np.arange(T_LAT)
    row = (t // GRID_W).astype(np.float64)
    col = (t % GRID_W).astype(np.float64)
    nf = HEAD_DIM // 4
    freqs = ROPE_THETA ** (-np.arange(nf, dtype=np.float64) / nf)
    d = np.arange(HEAD_DIM)
    pos = np.where(d[None, :] < HEAD_DIM // 2, row[:, None], col[:, None])
    ang = pos * freqs[d % nf][None, :]
    first = (d % (HEAD_DIM // 2)) < nf
    cos = np.cos(ang)
    sin_a = np.where(first[None, :], -np.sin(ang), 0.0)
    sin_b = np.where(first[None, :], 0.0, np.sin(ang))
    ident = (np.ones((TB, HEAD_DIM)), np.zeros((TB, HEAD_DIM)), np.zeros((TB, HEAD_DIM)))
    out = []
    for tab, idt in zip((cos, sin_a, sin_b), ident):
        full = np.concatenate([tab, idt], axis=0)
        out.append(jnp.asarray(np.tile(full, (1, LANES // HEAD_DIM)), F32))
    return out


def _dft_tables(t):
    m = np.outer(np.arange(t), np.arange(t)) % t
    ang = 2.0 * np.pi * m / t
    s = 1.0 / np.sqrt(t)
    return jnp.asarray(np.cos(ang) * s, F32).astype(BF16), jnp.asarray(np.sin(ang) * s, F32).astype(BF16)


def _adaln_body(cond_ref, w_ref, b_ref, o_ref):
    c = cond_ref[...]
    s = (c * jax.nn.sigmoid(c)).astype(BF16)
    o_ref[0] = jnp.dot(s, w_ref[0].astype(BF16), preferred_element_type=F32) + b_ref[0]


def _adaln(cond8, ada_w, ada_b):
    tn = 1536
    return pl.pallas_call(
        _adaln_body,
        grid=(DEPTH, 6 * D // tn),
        in_specs=[pl.BlockSpec((SUBLANES, D), lambda l, j: (0, 0)),
                  pl.BlockSpec((1, D, tn), lambda l, j: (l, 0, j)),
                  pl.BlockSpec((1, 1, tn), lambda l, j: (l, 0, j))],
        out_specs=pl.BlockSpec((1, SUBLANES, tn), lambda l, j: (l, 0, j)),
        out_shape=jax.ShapeDtypeStruct((DEPTH, SUBLANES, 6 * D), F32),
        compiler_params=_cparams(2),
        name="adaln",
    )(cond8, ada_w, ada_b.reshape(DEPTH, 1, 6 * D))


def _l0_in_body(xp_ref, xs_ref, mod_ref, g_ref, w_ref, qg_ref, kg_ref, bd_ref,
                cos_ref, sa_ref, sb_ref, q_ref, k_ref, v_ref, u_ref):
    i = pl.program_id(0)
    x = jnp.where(i < NT_CTX, xp_ref[...], xs_ref[...])
    m = mod_ref[0]
    h = _modulate(x, g_ref[...], m[:, 0:D], m[:, D:2 * D])
    z = jnp.dot(h.astype(BF16), w_ref[...], preferred_element_type=F32)
    cos, sa, sb, bd = cos_ref[...], sa_ref[...], sb_ref[...], bd_ref[...]

    def head_norm_rope(zz, gain):
        ss = jnp.dot((zz * zz).astype(BF16), bd, preferred_element_type=F32)
        y = zz * lax.rsqrt(ss * (1.0 / HEAD_DIM) + EPS) * gain
        return (y * cos + pltpu.roll(y, LANES - 16, axis=1) * sa
                + pltpu.roll(y, 16, axis=1) * sb)

    for s in range(ATTN_W // LANES):
        qs = head_norm_rope(z[:, s * LANES:(s + 1) * LANES], qg_ref[...])
        q_ref[:, s * LANES:(s + 1) * LANES] = (qs * (HEAD_DIM ** -0.5)).astype(BF16)
    k_ref[...] = head_norm_rope(z[:, ATTN_W:ATTN_W + KV_W], kg_ref[...])
    v_ref[...] = z[:, ATTN_W + KV_W:ATTN_W + 2 * KV_W]
    u_ref[...] = z[:, ATTN_W + 2 * KV_W:MIX_IN]


def _l0_in(xp, xs, mods, g, w_in, qg, kg, bd, tabs):
    tab_spec = pl.BlockSpec(
        (TB, LANES), lambda i: (jnp.where(i < NT_CTX, T_LAT // TB, (i - NT_CTX) % (T_LAT // TB)), 0))
    const = lambda shape: pl.BlockSpec(shape, lambda i: (0,) * len(shape))
    return pl.pallas_call(
        _l0_in_body,
        grid=(NT,),
        in_specs=[pl.BlockSpec((TB, D), lambda i: (jnp.minimum(i, NT_CTX - 1), 0)),
                  pl.BlockSpec((TB, D), lambda i: (jnp.maximum(i - NT_CTX, 0), 0)),
                  pl.BlockSpec((1, 1, 6 * D), lambda i: (_mod_row(i, NT_CTX, T_LAT // TB), 0, 0)),
                  const((1, D)), const((D, MIX_IN)), const((1, LANES)), const((1, LANES)),
                  const((LANES, LANES)), tab_spec, tab_spec, tab_spec],
        out_specs=[pl.BlockSpec((TB, ATTN_W), lambda i: (i, 0)),
                   pl.BlockSpec((TB, KV_W), lambda i: (i, 0)),
                   pl.BlockSpec((TB, KV_W), lambda i: (i, 0)),
                   pl.BlockSpec((TB, POOL_W), lambda i: (i, 0))],
        out_shape=[jax.ShapeDtypeStruct((N_TOK, ATTN_W), BF16),
                   jax.ShapeDtypeStruct((N_TOK, KV_W), F32),
                   jax.ShapeDtypeStruct((N_TOK, KV_W), F32),
                   jax.ShapeDtypeStruct((N_TOK, POOL_W), F32)],
        compiler_params=_cparams(1),
        name="l0_in",
    )(xp, xs, mods, g, w_in, qg, kg, bd, *tabs)


def _head_halves(x):
    z = jnp.zeros_like(x)
    return jnp.concatenate([x, z], axis=1), jnp.concatenate([z, x], axis=1)


_NT_DIMS = (((1,), (1,)), ((), ()))


def _ones_halves(x):
    one = jnp.ones_like(x)
    return jnp.concatenate([x, one], axis=1), jnp.concatenate([one, x], axis=1)


def _sink_attend(scores, values, sk, half):
    mx = sk
    for sc in scores:
        mx = jnp.maximum(mx, jnp.max(sc, axis=-1, keepdims=True))
    acc = None
    for sc, val in zip(scores, values):
        part = jnp.dot(jnp.exp(sc - mx).astype(BF16), val, preferred_element_type=F32)
        acc = part if acc is None else acc + part
    ones_lane = HEAD_DIM * (1 - half)
    den = acc[:, ones_lane:ones_lane + 1] + jnp.exp(sk - mx)
    return acc * (1.0 / den)


def _sink_col(sink_ref, heads, rows):
    return jnp.concatenate([jnp.broadcast_to(sink_ref[h:h + 1, 0:1], (rows, 1)) for h in heads], axis=0)


def _ctx_attn_body(q_ref, k_ref, v_ref, sink_ref, o_ref):
    k = k_ref[...].astype(BF16)
    v = v_ref[...].astype(BF16)
    lo = lax.broadcasted_iota(I32, (T_CTX, LANES), 1) < HEAD_DIM
    for j in range(KV_W // HEAD_DIM):
        kj = k[:, j * HEAD_DIM:(j + 1) * HEAD_DIM]
        vj = v[:, j * HEAD_DIM:(j + 1) * HEAD_DIM]
        k_halves = _head_halves(kj)
        vd = jnp.concatenate([vj, vj], axis=1)
        q2 = jnp.concatenate([q_ref[:, (2 * j) * LANES:(2 * j + 1) * LANES],
                              q_ref[:, (2 * j + 1) * LANES:(2 * j + 2) * LANES]], axis=0)
        outs = []
        for half in range(2):
            sc = lax.dot_general(q2, k_halves[half], _NT_DIMS, preferred_element_type=F32)
            sk = _sink_col(sink_ref, (4 * j + half, 4 * j + 2 + half), T_CTX)
            mx = jnp.maximum(sk, jnp.max(sc, axis=-1, keepdims=True))
            p = jnp.exp(sc - mx)
            inv = 1.0 / (jnp.exp(sk - mx) + jnp.sum(p, axis=-1, keepdims=True))
            outs.append(jnp.dot((p * inv).astype(BF16), vd, preferred_element_type=F32))
        for s2 in range(2):
            rows = slice(s2 * T_CTX, (s2 + 1) * T_CTX)
            o_ref[:, (2 * j + s2) * LANES:(2 * j + s2 + 1) * LANES] = (
                jnp.where(lo, outs[0][rows], outs[1][rows]).astype(BF16))


def _ctx_attn(q, k, v, sink_b):
    return pl.pallas_call(
        _ctx_attn_body,
        grid=(N_CTX_B,),
        in_specs=[pl.BlockSpec((T_CTX, ATTN_W), lambda b: (b, 0)),
                  pl.BlockSpec((T_CTX, KV_W), lambda b: (b, 0)),
                  pl.BlockSpec((T_CTX, KV_W), lambda b: (b, 0)),
                  pl.BlockSpec((SUBLANES, LANES), lambda b: (0, 0))],
        out_specs=pl.BlockSpec((T_CTX, ATTN_W), lambda b: (b, 0)),
        out_shape=jax.ShapeDtypeStruct((N_CTX, ATTN_W), BF16),
        compiler_params=_cparams(1),
        name="ctx_attn",
    )(q, k, v, sink_b)


QB = 128
SPAN = QB + 2 * WINDOW


def _lat_attn_body(q_ref, k_ref, v_ref, ck_ref, cv_ref, sink_ref, o_ref):
    qb = pl.program_id(1)
    start = qb * QB
    kws, vws = [], []
    for c in (-1, 0, 1):
        cs = pl.multiple_of(jnp.clip(start + c * QB, 0, T_LAT - QB), QB)
        kws.append(k_ref[pl.ds(cs, QB), :])
        vws.append(v_ref[pl.ds(cs, QB), :])
    kw = jnp.concatenate(kws, axis=0).astype(BF16)
    vw = jnp.concatenate(vws, axis=0).astype(BF16)
    ck = ck_ref[0].astype(BF16)
    cv = cv_ref[0].astype(BF16)
    qpos = start + (lax.broadcasted_iota(I32, (2 * QB, SPAN), 0) & (QB - 1))
    kpos = start - WINDOW + lax.broadcasted_iota(I32, (2 * QB, SPAN), 1)
    valid = (kpos >= 0) & (kpos < T_LAT) & (jnp.abs(qpos - kpos) <= WINDOW)
    lo = lax.broadcasted_iota(I32, (QB, LANES), 1) < HEAD_DIM
    for j in range(KV_W // HEAD_DIM):
        sl = slice(j * HEAD_DIM, (j + 1) * HEAD_DIM)
        kw_halves = _head_halves(kw[:, sl])
        ck_halves = _head_halves(ck[:, sl])
        vw_halves = _ones_halves(vw[:, sl])
        cv_halves = _ones_halves(cv[:, sl])
        q2 = jnp.concatenate([q_ref[:, (2 * j) * LANES:(2 * j + 1) * LANES],
                              q_ref[:, (2 * j + 1) * LANES:(2 * j + 2) * LANES]], axis=0)
        outs = []
        for half in range(2):
            s_win = lax.dot_general(q2, kw_halves[half], _NT_DIMS, preferred_element_type=F32)
            s_win = jnp.where(valid, s_win, NEG)
            s_ctx = lax.dot_general(q2, ck_halves[half], _NT_DIMS, preferred_element_type=F32)
            sk = _sink_col(sink_ref, (4 * j + half, 4 * j + 2 + half), QB)
            outs.append(_sink_attend([s_win, s_ctx], [vw_halves[half], cv_halves[half]], sk, half))
        for s2 in range(2):
            rows = slice(s2 * QB, (s2 + 1) * QB)
            o_ref[:, (2 * j + s2) * LANES:(2 * j + s2 + 1) * LANES] = (
                jnp.where(lo, outs[0][rows], outs[1][rows]).astype(BF16))


def _lat_attn(q, k, v, ck, cv, sink_b):
    lat0 = N_CTX // T_LAT
    return pl.pallas_call(
        _lat_attn_body,
        grid=(N_LAT_B, T_LAT // QB),
        in_specs=[pl.BlockSpec((QB, ATTN_W), lambda b, i: (N_CTX // QB + b * (T_LAT // QB) + i, 0)),
                  pl.BlockSpec((T_LAT, KV_W), lambda b, i: (lat0 + b, 0)),
                  pl.BlockSpec((T_LAT, KV_W), lambda b, i: (lat0 + b, 0)),
                  pl.BlockSpec((1, PAST, KV_W), lambda b, i: (b, 0, 0)),
                  pl.BlockSpec((1, PAST, KV_W), lambda b, i: (b, 0, 0)),
                  pl.BlockSpec((SUBLANES, LANES), lambda b, i: (0, 0))],
        out_specs=pl.BlockSpec((QB, ATTN_W), lambda b, i: (b * (T_LAT // QB) + i, 0)),
        out_shape=jax.ShapeDtypeStruct((N_LAT, ATTN_W), BF16),
        compiler_params=_cparams(2),
        name="lat_attn",
    )(q, k, v, ck, cv, sink_b)


def _l0_out_body(oc_ref, ol_ref, u_ref, xp_ref, xs_ref, mod_ref, pw_ref, ps_ref, wo_ref, x1_ref):
    i = pl.program_id(0)
    is_ctx = i < N_CTX // TB_MIX
    o = jnp.where(is_ctx, oc_ref[...], ol_ref[...])
    x = jnp.where(is_ctx, xp_ref[...], xs_ref[...])
    tseq = jnp.where(is_ctx, T_CTX, T_LAT)
    pos = lax.broadcasted_iota(I32, (TB_MIX, LANES), 0) & (tseq - 1)
    ys = []
    for g, win in enumerate(POOL_WINDOWS):
        hw = win // 2
        ug = u_ref[:, g * LANES:(g + 1) * LANES]
        acc = ug
        for jj in range(-hw, hw):
            if jj == 0:
                continue
            sh = pltpu.roll(ug, (-jj) % TB_MIX, axis=0)
            ok = (pos + jj >= 0) if jj < 0 else (pos + jj < tseq)
            acc = acc + jnp.where(ok, sh, 0.0)
        cnt = (jnp.minimum(pos + hw, tseq) - jnp.maximum(pos - hw, 0)).astype(F32)
        pooled = acc / cnt - ug
        ys.append(jnp.dot(pooled.astype(BF16), pw_ref[g], preferred_element_type=F32))
    y = jnp.concatenate(ys, axis=1) * ps_ref[...]
    out = (jnp.dot(o, wo_ref[0:ATTN_W, :], preferred_element_type=F32)
           + jnp.dot(y.astype(BF16), wo_ref[ATTN_W:ATTN_W + POOL_W, :], preferred_element_type=F32))
    x1_ref[...] = x + mod_ref[0][:, 2 * D:3 * D] * out


def _l0_out(o_ctx, o_lat, u, xp, xs, mods, pool_w, pool_scale, w_out):
    ntc = N_CTX // TB_MIX
    const = lambda shape: pl.BlockSpec(shape, lambda i: (0,) * len(shape))
    ctx_map = lambda i: (jnp.minimum(i, ntc - 1), 0)
    lat_map = lambda i: (jnp.maximum(i - ntc, 0), 0)
    return pl.pallas_call(
        _l0_out_body,
        grid=(N_TOK // TB_MIX,),
        in_specs=[pl.BlockSpec((TB_MIX, ATTN_W), ctx_map),
                  pl.BlockSpec((TB_MIX, ATTN_W), lat_map),
                  pl.BlockSpec((TB_MIX, POOL_W), lambda i: (i, 0)),
                  pl.BlockSpec((TB_MIX, D), ctx_map),
                  pl.BlockSpec((TB_MIX, D), lat_map),
                  pl.BlockSpec((1, 1, 6 * D), lambda i: (_mod_row(i, ntc, 1), 0, 0)),
                  const((len(POOL_WINDOWS), LANES, LANES)), const((1, POOL_W)), const((D, D))],
        out_specs=pl.BlockSpec((TB_MIX, D), lambda i: (i, 0)),
        out_shape=jax.ShapeDtypeStruct((N_TOK, D), F32),
        compiler_params=_cparams(1, VMEM_LIMIT),
        name="l0_out",
    )(o_ctx, o_lat, u, xp, xs, mods, pool_w, pool_scale, w_out)


def _first_max(vals):
    best, idx = vals[0], jnp.zeros(vals[0].shape, I32)
    for r in range(1, len(vals)):
        better = vals[r] > best
        idx = jnp.where(better, r, idx)
        best = jnp.where(better, vals[r], best)
    return best, idx


def _softmax_rows(rows):
    mx = functools.reduce(jnp.maximum, rows)
    ex = [jnp.exp(r - mx) for r in rows]
    tot = functools.reduce(lambda a, b: a + b, ex)
    return [e / tot for e in ex]


def _route_body(x_ref, mod_ref, g_ref, wr_ref, br_ref, tri_ref,
                xext_ref, info_ref, cnt_ref, base_ref):
    i = pl.program_id(0)

    @pl.when(i == 0)
    def _():
        base_ref[...] = jnp.zeros_like(base_ref)

    x = x_ref[...]
    m = mod_ref[0]
    h = _modulate(x, g_ref[...], m[:, 3 * D:4 * D], m[:, 4 * D:5 * D])

    hh = h.astype(BF16)
    hl = (h - hh.astype(F32)).astype(BF16)
    wh, wl = wr_ref[0], wr_ref[1]
    lg = (lax.dot_general(wh, hh, _NT_DIMS, preferred_element_type=F32)
          + lax.dot_general(wl, hh, _NT_DIMS, preferred_element_type=F32)
          + lax.dot_general(wh, hl, _NT_DIMS, preferred_element_type=F32)) + br_ref[:, 0:1]

    pg = _softmax_rows([lg[N_EXPERTS + r:N_EXPERTS + r + 1] for r in range(N_GROUPS)])
    pg_top, gi = _first_max(pg)
    le = []
    for j in range(PER_GROUP):
        sel = lg[(N_GROUPS - 1) * PER_GROUP + j:(N_GROUPS - 1) * PER_GROUP + j + 1]
        for g in range(N_GROUPS - 2, -1, -1):
            sel = jnp.where(gi == g, lg[g * PER_GROUP + j:g * PER_GROUP + j + 1], sel)
        le.append(sel)
    pe = _softmax_rows(le)
    p1, i1 = _first_max(pe)
    p2, i2 = _first_max([jnp.where(i1 == j, -1.0, pe[j]) for j in range(PER_GROUP)])
    den = p1 + p2
    w1 = pg_top * p1 / den
    w2 = pg_top * p2 / den

    lo = jnp.minimum(i1, i2)
    hi = jnp.maximum(i1, i2)
    cls = gi * PAIRS + jnp.where(lo == 0, 0, jnp.where(lo == 1, 3, 5)) + hi - lo - 1
    w_lo = jnp.where(i1 == lo, w1, w2)
    w_hi = jnp.where(i1 == lo, w2, w1)

    crow = lax.broadcasted_iota(I32, (CLASS_ROWS, TB), 0)
    hit = crow == cls
    onehot = jnp.where(hit, 1.0, 0.0)
    before = jnp.dot(onehot.astype(BF16), tri_ref[...], preferred_element_type=F32)
    before = before + base_ref[:, 0:1]
    rank = jnp.sum(jnp.where(hit, before, 0.0), axis=0, keepdims=True)
    base_ref[...] = base_ref[...] + jnp.sum(onehot, axis=1, keepdims=True)
    cnt_ref[...] = base_ref[...]

    mod_id = jnp.zeros_like(w1) + _mod_row(i, NT_CTX, T_LAT // TB).astype(F32)
    zero = jnp.zeros_like(w1)
    info_ref[...] = jnp.concatenate([cls.astype(F32), rank, zero, zero, zero, zero, zero, zero], axis=0)
    side = jnp.concatenate([w_lo, w_hi, mod_id, jnp.zeros((LANES - 3, TB), F32)], axis=0).T
    xext_ref[:, 0:D] = x
    xext_ref[:, D:XEXT] = side


def _route(x, mods, layer, g, wr, br, tri):
    const = lambda shape: pl.BlockSpec(shape, lambda i: (0,) * len(shape))
    return pl.pallas_call(
        _route_body,
        grid=(NT,),
        in_specs=[pl.BlockSpec((TB, D), lambda i: (i, 0)),
                  pl.BlockSpec((1, 1, 6 * D),
                               lambda i: (layer * SUBLANES + _mod_row(i, NT_CTX, T_LAT // TB), 0, 0)),
                  const((1, D)), const((2, ROUTE_ROWS, D)), const((ROUTE_ROWS, LANES)),
                  const((TB, TB))],
        out_specs=[pl.BlockSpec((TB, XEXT), lambda i: (i, 0)),
                   pl.BlockSpec((SUBLANES, TB), lambda i: (0, i)),
                   pl.BlockSpec((CLASS_ROWS, LANES), lambda i: (0, 0))],
        out_shape=[jax.ShapeDtypeStruct((N_TOK, XEXT), F32),
                   jax.ShapeDtypeStruct((SUBLANES, N_TOK), F32),
                   jax.ShapeDtypeStruct((CLASS_ROWS, LANES), F32)],
        scratch_shapes=[pltpu.VMEM((CLASS_ROWS, LANES), F32)],
        compiler_params=_cparams(1),
        name=f"route{layer}",
    )(x, mods, g, wr, br, tri)


def _plan(info, counts):
    cls = info[0].astype(I32)
    rank = info[1].astype(I32)
    cnt = counts[:N_CLASS, 0].astype(I32)
    tiles = (cnt + TM - 1) // TM
    tend = jnp.cumsum(tiles)
    tstart = tend - tiles
    n_used = tend[-1]
    cidx = jnp.arange(N_CLASS, dtype=I32)
    pos = jnp.sum(jnp.where(cls[:, None] == cidx, tstart * TM, 0), axis=-1) + rank
    n = jnp.arange(NT_FFN, dtype=I32)
    tile = jnp.minimum(n, jnp.maximum(n_used - 1, 0))
    tcls = jnp.minimum(jnp.sum((tile[:, None] >= tend[None, :]).astype(I32), axis=1), N_CLASS - 1)
    of_cls = tcls[:, None] == cidx
    rows = jnp.sum(jnp.where(of_cls, cnt, 0), axis=1) - (tile - jnp.sum(jnp.where(of_cls, tstart, 0), axis=1)) * TM
    rows = jnp.where(n < n_used, jnp.clip(rows, 0, TM), 0)
    chunks = (rows + CH - 1) // CH
    pair = tcls % PAIRS
    lo = (pair >= 3).astype(I32) + (pair >= 5).astype(I32)
    hi = jnp.where(pair < 3, pair + 1, jnp.where(pair < 5, pair - 1, 3))
    e_lo = (tcls // PAIRS) * PER_GROUP + lo
    e_hi = (tcls // PAIRS) * PER_GROUP + hi
    return pos, e_lo, e_hi, n_used.reshape(1), chunks


OCT = TM // SUBLANES
CH = 32


def _ffn_body(pos_ref, elo_ref, ehi_ref, nu_ref, ch_ref,
              xext_hbm, mod_ref, g_ref, w1a_ref, w1b_ref, w3a_ref, w3b_ref, w2a_ref, w2b_ref,
              out_hbm, src_ref, dst_ref, xbuf, ybuf, wb1a, wb1b, wb3a, wb3b, wb2a, wb2b, gsem, ssem):
    n = pl.program_id(0)
    n_used = nu_ref[0]

    def gather_copy(tile, s, c, j):
        return pltpu.make_async_copy(
            xext_hbm.at[pl.ds(src_ref[tile * TM + c * CH + j], 1)],
            xbuf.at[s, c * (CH // SUBLANES) + j // SUBLANES, pl.ds(j % SUBLANES, 1)], gsem.at[s])

    def scatter_copy(tile, s, c, j):
        return pltpu.make_async_copy(
            ybuf.at[s, c * (CH // SUBLANES) + j // SUBLANES, pl.ds(j % SUBLANES, 1)],
            out_hbm.at[pl.ds(dst_ref[tile * TM + c * CH + j], 1)], ssem.at[s])

    def start_rows(copy, tile, s):
        def chunk(c, carry):
            for j in range(CH):
                copy(tile, s, c, j).start()
            return carry
        lax.fori_loop(0, ch_ref[tile], chunk, 0)

    def wait_rows(src, dst, sem, tile):
        def chunk(c, carry):
            pltpu.make_async_copy(src, dst, sem).wait()
            return carry
        lax.fori_loop(0, ch_ref[tile], chunk, 0)

    def wait_gather(tile, s):
        rows = xbuf.at[s, pl.ds(0, CH // SUBLANES)]
        wait_rows(rows, rows, gsem.at[s], tile)

    def wait_scatter(tile, s):
        rows = ybuf.at[s, pl.ds(0, CH // SUBLANES)]
        wait_rows(rows, rows, ssem.at[s], tile)

    @pl.when(n == 0)
    def _():
        def pad_rows(tile, carry):
            @pl.when(ch_ref[tile] > 0)
            def _():
                first = tile * TM + (ch_ref[tile] - 1) * CH
                for j in range(CH):
                    src_ref[first + j] = 0
                    dst_ref[first + j] = N_TOK + ((first + j) & (2 * TM - 1))
            return carry

        lax.fori_loop(0, NT_FFN, pad_rows, 0)

        def put(t, carry):
            p = pos_ref[t]
            src_ref[p] = t
            dst_ref[p] = t
            return carry

        lax.fori_loop(0, N_TOK, put, 0, unroll=8)

        xbuf[...] = jnp.zeros_like(xbuf)
        ybuf[...] = jnp.zeros_like(ybuf)
        for s in range(2):
            dumps = [pltpu.make_async_copy(
                ybuf.at[s, q], out_hbm.at[pl.ds(N_TOK + s * TM + q * SUBLANES, SUBLANES)], ssem.at[s])
                for q in range(OCT)]
            for dump in dumps:
                dump.start()
            for dump in dumps:
                dump.wait()
        start_rows(gather_copy, 0, 0)

    def step(slot):
        @pl.when(n + 1 < n_used)
        def _():
            start_rows(gather_copy, n + 1, 1 - slot)

        wait_gather(n, slot)

        @pl.when(n >= 2)
        def _():
            wait_scatter(n - 2, slot)

        prev = jnp.maximum(n - 1, 0)

        @pl.when((n == 0) | (elo_ref[n] != elo_ref[prev]))
        def _():
            wb1a[...] = w1a_ref[0, 0].astype(BF16)
            wb3a[...] = w3a_ref[0, 0].astype(BF16)
            wb2a[...] = w2a_ref[0, 0].astype(BF16)

        @pl.when((n == 0) | (ehi_ref[n] != ehi_ref[prev]))
        def _():
            wb1b[...] = w1b_ref[0, 0].astype(BF16)
            wb3b[...] = w3b_ref[0, 0].astype(BF16)
            wb2b[...] = w2b_ref[0, 0].astype(BF16)

        xe = xbuf[slot].reshape(TM, XEXT)
        x = xe[:, 0:D]
        w_lo = xe[:, D:D + 1]
        w_hi = xe[:, D + 1:D + 2]
        mod_id = xe[:, D + 2:D + 3]

        def pick(lo, hi):
            return jnp.where(mod_id < 0.5, mod_ref[0][:, lo:hi],
                             jnp.where(mod_id < 1.5, mod_ref[1][:, lo:hi], mod_ref[2][:, lo:hi]))

        h = _modulate(x, g_ref[...], pick(3 * D, 4 * D), pick(4 * D, 5 * D)).astype(BF16)

        def act(wb1, wb3, w):
            h1 = jnp.dot(h, wb1[...], preferred_element_type=F32)
            h3 = jnp.dot(h, wb3[...], preferred_element_type=F32)
            return ((h1 * jax.nn.sigmoid(h1)) * h3 * w).astype(BF16)

        y = (jnp.dot(act(wb1a, wb3a, w_lo), wb2a[...], preferred_element_type=F32)
             + jnp.dot(act(wb1b, wb3b, w_hi), wb2b[...], preferred_element_type=F32))
        ybuf[slot] = (x + pick(5 * D, 6 * D) * y).reshape(OCT, SUBLANES, D)
        start_rows(scatter_copy, n, slot)

        @pl.when(n == n_used - 1)
        def _():
            @pl.when(n >= 1)
            def _():
                wait_scatter(n - 1, 1 - slot)
            wait_scatter(n, slot)

    for s in range(2):
        @pl.when((n < n_used) & (n % 2 == s))
        def _():
            step(s)


def _ffn(pos, e_lo, e_hi, n_used, chunks, xext, mods, layer, g, w1, w3, w2):
    lo_map = lambda n, p, el, eh, nu, ch: (layer, el[n], 0, 0)
    hi_map = lambda n, p, el, eh, nu, ch: (layer, eh[n], 0, 0)
    up = lambda imap: pl.BlockSpec((1, 1, D, D_EXPERT), imap)
    down = lambda imap: pl.BlockSpec((1, 1, D_EXPERT, D), imap)
    return pl.pallas_call(
        _ffn_body,
        grid_spec=pltpu.PrefetchScalarGridSpec(
            num_scalar_prefetch=5, grid=(NT_FFN,),
            in_specs=[pl.BlockSpec(memory_space=pl.ANY),
                      pl.BlockSpec((SUBLANES, 1, 6 * D), lambda n, *_: (layer, 0, 0)),
                      pl.BlockSpec((1, D), lambda n, *_: (0, 0)),
                      up(lo_map), up(hi_map), up(lo_map), up(hi_map), down(lo_map), down(hi_map)],
            out_specs=pl.BlockSpec(memory_space=pl.ANY),
            scratch_shapes=[pltpu.SMEM((P_FFN,), I32), pltpu.SMEM((P_FFN,), I32),
                            pltpu.VMEM((2, OCT, SUBLANES, XEXT), F32),
                            pltpu.VMEM((2, OCT, SUBLANES, D), F32),
                            pltpu.VMEM((D, D_EXPERT), BF16), pltpu.VMEM((D, D_EXPERT), BF16),
                            pltpu.VMEM((D, D_EXPERT), BF16), pltpu.VMEM((D, D_EXPERT), BF16),
                            pltpu.VMEM((D_EXPERT, D), BF16), pltpu.VMEM((D_EXPERT, D), BF16),
                            pltpu.SemaphoreType.DMA((2,)), pltpu.SemaphoreType.DMA((2,))]),
        out_shape=jax.ShapeDtypeStruct((N_TOK + 2 * TM, D), F32),
        compiler_params=_cparams(1, VMEM_LIMIT),
        name=f"ffn{layer}",
    )(pos, e_lo, e_hi, n_used, chunks, xext, mods, g, w1, w1, w3, w3, w2, w2)


def _moe(x, mods, layer, g, wr, br, tri, w1, w3, w2):
    xext, info, counts = _route(x, mods, layer, g, wr, br, tri)
    pos, e_lo, e_hi, n_used, chunks = _plan(info, counts)
    return _ffn(pos, e_lo, e_hi, n_used, chunks, xext, mods, layer, g, w1, w3, w2)


FG = 256


def _l1_in_body(x_ref, mod_ref, g_ref, w_ref, c_ref, s_ref, zc_ref, zs_ref):
    m = mod_ref[0]
    h = _modulate(x_ref[...], g_ref[...], m[:, 0:D], m[:, D:2 * D])
    z = jnp.dot(h.astype(BF16), w_ref[...], preferred_element_type=F32).astype(BF16)
    for g in range(D // FG):
        zg = z[:, g * FG:(g + 1) * FG]
        zc_ref[:, g * FG:(g + 1) * FG] = jnp.dot(zg, c_ref[...], preferred_element_type=F32).astype(BF16)
        zs_ref[:, g * FG:(g + 1) * FG] = jnp.dot(zg, s_ref[...], preferred_element_type=F32).astype(BF16)


def _l1_in(x, mods, g, w, c256, s256):
    const = lambda shape: pl.BlockSpec(shape, lambda i: (0,) * len(shape))
    return pl.pallas_call(
        _l1_in_body,
        grid=(NT,),
        in_specs=[pl.BlockSpec((TB, D), lambda i: (i, 0)),
                  pl.BlockSpec((1, 1, 6 * D),
                               lambda i: (SUBLANES + _mod_row(i, NT_CTX, T_LAT // TB), 0, 0)),
                  const((1, D)), const((D, D)), const((FG, FG)), const((FG, FG))],
        out_specs=[pl.BlockSpec((TB, D), lambda i: (i, 0)), pl.BlockSpec((TB, D), lambda i: (i, 0))],
        out_shape=[jax.ShapeDtypeStruct((N_TOK, D), BF16), jax.ShapeDtypeStruct((N_TOK, D), BF16)],
        compiler_params=_cparams(1),
        name="l1_in",
    )(x, mods, g, w, c256, s256)


def _l1_out_body(zc_t_ref, zs_t_ref, zc_q_ref, zs_q_ref, c256_ref, s256_ref, c1k_ref, s1k_ref,
                 x_ref, mod_ref, wo_ref, o_ref, f_ref):
    i = pl.program_id(0)

    @pl.when(i < NT_CTX)
    def _():
        for q in range(TB // T_CTX):
            rows = slice(q * T_CTX, (q + 1) * T_CTX)
            f = (jnp.dot(c256_ref[...], zc_t_ref[rows, :], preferred_element_type=F32)
                 - jnp.dot(s256_ref[...], zs_t_ref[rows, :], preferred_element_type=F32))
            f_ref[rows, :] = f.astype(BF16)

    @pl.when(i >= NT_CTX)
    def _():
        f = (jnp.dot(c1k_ref[...], zc_q_ref[...], preferred_element_type=F32)
             - jnp.dot(s1k_ref[...], zs_q_ref[...], preferred_element_type=F32))
        f_ref[...] = f.astype(BF16)

    out = jnp.dot(f_ref[...], wo_ref[...], preferred_element_type=F32)
    o_ref[...] = x_ref[...] + mod_ref[0][:, 2 * D:3 * D] * out


def _l1_out(zc, zs, c256, s256, c1k, s1k, x, mods, w_out):
    const = lambda shape: pl.BlockSpec(shape, lambda i: (0,) * len(shape))
    tile_map = lambda i: (jnp.minimum(i, NT_CTX - 1), 0)
    seq_map = lambda i: (N_CTX // T_LAT + jnp.maximum(i - NT_CTX, 0) // (T_LAT // TB), 0)
    row_map = lambda i: (jnp.maximum(i - NT_CTX, 0) % (T_LAT // TB), 0)
    return pl.pallas_call(
        _l1_out_body,
        grid=(NT,),
        in_specs=[pl.BlockSpec((TB, D), tile_map), pl.BlockSpec((TB, D), tile_map),
                  pl.BlockSpec((T_LAT, D), seq_map), pl.BlockSpec((T_LAT, D), seq_map),
                  const((T_CTX, T_CTX)), const((T_CTX, T_CTX)),
                  pl.BlockSpec((TB, T_LAT), row_map), pl.BlockSpec((TB, T_LAT), row_map),
                  pl.BlockSpec((TB, D), lambda i: (i, 0)),
                  pl.BlockSpec((1, 1, 6 * D),
                               lambda i: (SUBLANES + _mod_row(i, NT_CTX, T_LAT // TB), 0, 0)),
                  const((D, D))],
        out_specs=pl.BlockSpec((TB, D), lambda i: (i, 0)),
        out_shape=jax.ShapeDtypeStruct((N_TOK, D), F32),
        scratch_shapes=[pltpu.VMEM((TB, D), BF16)],
        compiler_params=_cparams(1),
        name="l1_out",
    )(zc, zs, zc, zs, c256, s256, c1k, s1k, x, mods, w_out)


def _split_hi_lo(w):
    hi = w.astype(BF16)
    lo = (w - hi.astype(F32)).astype(BF16)
    return jnp.stack([hi, lo])


def kernel(x_prompt, x_sample, cache_k, cache_v, c, c_ctx, ada_w, ada_b, norm_mix, norm_ffn, a_w_in, a_q_norm, a_k_norm, a_sink, pool_w, pool_scale, a_w_out, f_w_in, f_w_out, router_g_w, router_g_b, router_e_w, router_e_b, moe_w1, moe_w3, moe_w2):
    xp = x_prompt.reshape(N_CTX, D)
    xs = x_sample.reshape(N_LAT, D)

    cond8 = jnp.zeros((SUBLANES, D), F32).at[0].set(c_ctx).at[1:1 + N_LAT_B].set(c)
    mods = _adaln(cond8, ada_w, ada_b).reshape(DEPTH * SUBLANES, 1, 6 * D)

    tabs = _rope_tables()
    lane = np.arange(LANES)
    bd = jnp.asarray((lane[:, None] // HEAD_DIM) == (lane[None, :] // HEAD_DIM), BF16)
    tri = jnp.asarray(np.arange(TB)[:, None] < np.arange(TB)[None, :], BF16)
    c256, s256 = _dft_tables(T_CTX)
    c1k, s1k = _dft_tables(T_LAT)

    def router_operands(l):
        w = jnp.concatenate([router_e_w[l], router_g_w[l]], axis=1).T
        w = jnp.pad(w, ((0, ROUTE_ROWS - w.shape[0]), (0, 0)))
        b = jnp.concatenate([router_e_b[l], router_g_b[l]])
        b = jnp.pad(b, (0, ROUTE_ROWS - b.shape[0]))
        return _split_hi_lo(w), jnp.broadcast_to(b[:, None], (ROUTE_ROWS, LANES))

    qg = jnp.tile(a_q_norm[0], LANES // HEAD_DIM)[None, :]
    kg = jnp.tile(a_k_norm[0], LANES // HEAD_DIM)[None, :]
    q, k, v, u = _l0_in(xp, xs, mods, norm_mix[0][None, :], a_w_in[0].astype(BF16), qg, kg, bd, tabs)
    sink_b = jnp.broadcast_to(a_sink[0][:, None], (N_HEADS, LANES))
    o_ctx = _ctx_attn(q, k, v, sink_b)
    ck = cache_k[:, 0].reshape(N_LAT_B, PAST, KV_W)
    cv = cache_v[:, 0].reshape(N_LAT_B, PAST, KV_W)
    o_lat = _lat_attn(q, k, v, ck, cv, sink_b)
    x1 = _l0_out(o_ctx, o_lat, u, xp, xs, mods, pool_w[0].astype(BF16), pool_scale[0][None, :],
                 a_w_out[0].astype(BF16))
    wr, br = router_operands(0)
    x2 = _moe(x1, mods, 0, norm_ffn[0][None, :], wr, br, tri, moe_w1, moe_w3, moe_w2)

    zc, zs = _l1_in(x2, mods, norm_mix[1][None, :], f_w_in[0].astype(BF16), c256, s256)
    x3 = _l1_out(zc, zs, c256, s256, c1k, s1k, x2, mods, f_w_out[0].astype(BF16))
    wr, br = router_operands(1)
    x4 = _moe(x3, mods, 1, norm_ffn[1][None, :], wr, br, tri, moe_w1, moe_w3, moe_w2)

    new_k = k[:N_CTX].reshape(N_CTX_B, 1, T_CTX, KV_W // HEAD_DIM, HEAD_DIM)
    new_v = v[:N_CTX].reshape(N_CTX_B, 1, T_CTX, KV_W // HEAD_DIM, HEAD_DIM)
    return (x4[:N_CTX].reshape(N_CTX_B, T_CTX, D), x4[N_CTX:N_TOK].reshape(N_LAT_B, T_LAT, D),
            new_k, new_v)
```

```python
import functools

import numpy as np
import jax
import jax.numpy as jnp
from jax import lax
from jax.experimental import pallas as pl
from jax.experimental.pallas import tpu as pltpu

F32 = jnp.float32
BF16 = jnp.bfloat16
I32 = jnp.int32

D = 1024
DEPTH = 2
N_CTX_B, T_CTX = 16, 256
N_LAT_B, T_LAT = 2, 1024
N_CTX = N_CTX_B * T_CTX
N_LAT = N_LAT_B * T_LAT
N_TOK = N_CTX + N_LAT
PAST = 512
GRID_W = 64
HEAD_DIM = 64
N_HEADS = 8
ATTN_W = 512
KV_W = 128
POOL_W = 512
POOL_WINDOWS = (2, 4, 8, 16)
MIX_IN = ATTN_W + 2 * KV_W + POOL_W
WINDOW = 128
N_GROUPS = 4
PER_GROUP = 4
N_EXPERTS = 16
D_EXPERT = 512
ROPE_THETA = 10000.0
EPS = 1e-6
NEG = -1e30

LANES = 128
SUBLANES = 8
TB = 512
NT = N_TOK // TB
NT_CTX = N_CTX // TB
TB_MIX = 1024
TM = 256
PAIRS = 6
N_CLASS = N_GROUPS * PAIRS
CLASS_ROWS = 32
NT_FFN = N_TOK // TM + N_CLASS
P_FFN = NT_FFN * TM
XEXT = D + LANES
ROUTE_ROWS = 32

VMEM_LIMIT = 56 * 1024 * 1024


def _cparams(n_axes=1, vmem=None):
    return pltpu.CompilerParams(dimension_semantics=("arbitrary",) * n_axes,
                                vmem_limit_bytes=vmem)


def _modulate(x, g, shift, scale):
    ms = jnp.mean(x * x, axis=-1, keepdims=True)
    return (x * lax.rsqrt(ms + EPS) * g) * (1.0 + scale) + shift


def _mod_row(tile, tiles_ctx, tiles_per_lat):
    return (tile >= tiles_ctx).astype(I32) + (tile >= tiles_ctx + tiles_per_lat).astype(I32)


def _rope_tables():
    t = np.arange(T_LAT)
    row = (t // GRID_W).astype(np.float64)
    col = (t % GRID_W).astype(np.float64)
    nf = HEAD_DIM // 4
    freqs = ROPE_THETA ** (-np.arange(nf, dtype=np.float64) / nf)
    d = np.arange(HEAD_DIM)
    pos = np.where(d[None, :] < HEAD_DIM // 2, row[:, None], col[:, None])
    ang = pos * freqs[d % nf][None, :]
    first = (d % (HEAD_DIM // 2)) < nf
    cos = np.cos(ang)
    sin_a = np.where(first[None, :], -np.sin(ang), 0.0)
    sin_b = np.where(first[None, :], 0.0, np.sin(ang))
    ident = (np.ones((TB, HEAD_DIM)), np.zeros((TB, HEAD_DIM)), np.zeros((TB, HEAD_DIM)))
    out = []
    for tab, idt in zip((cos, sin_a, sin_b), ident):
        full = np.concatenate([tab, idt], axis=0)
        out.append(jnp.asarray(np.tile(full, (1, LANES // HEAD_DIM)), F32))
    return out


def _dft_tables(t):
    m = np.outer(np.arange(t), np.arange(t)) % t
    ang = 2.0 * np.pi * m / t
    s = 1.0 / np.sqrt(t)
    return jnp.asarray(np.cos(ang) * s, F32).astype(BF16), jnp.asarray(np.sin(ang) * s, F32).astype(BF16)


def _adaln_body(cond_ref, w_ref, b_ref, o_ref):
    c = cond_ref[...]
    s = (c * jax.nn.sigmoid(c)).astype(BF16)
    o_ref[0] = jnp.dot(s, w_ref[0].astype(BF16), preferred_element_type=F32) + b_ref[0]


def _adaln(cond8, ada_w, ada_b):
    tn = 1536
    return pl.pallas_call(
        _adaln_body,
        grid=(DEPTH, 6 * D // tn),
        in_specs=[pl.BlockSpec((SUBLANES, D), lambda l, j: (0, 0)),
                  pl.BlockSpec((1, D, tn), lambda l, j: (l, 0, j)),
                  pl.BlockSpec((1, 1, tn), lambda l, j: (l, 0, j))],
        out_specs=pl.BlockSpec((1, SUBLANES, tn), lambda l, j: (l, 0, j)),
        out_shape=jax.ShapeDtypeStruct((DEPTH, SUBLANES, 6 * D), F32),
        compiler_params=_cparams(2),
        name="adaln",
    )(cond8, ada_w, ada_b.reshape(DEPTH, 1, 6 * D))


def _l0_in_body(xp_ref, xs_ref, mod_ref, g_ref, w_ref, qg_ref, kg_ref, bd_ref,
                cos_ref, sa_ref, sb_ref, q_ref, k_ref, v_ref, u_ref):
    i = pl.program_id(0)
    x = jnp.where(i < NT_CTX, xp_ref[...], xs_ref[...])
    m = mod_ref[0]
    h = _modulate(x, g_ref[...], m[:, 0:D], m[:, D:2 * D])
    z = jnp.dot(h.astype(BF16), w_ref[...], preferred_element_type=F32)
    cos, sa, sb, bd = cos_ref[...], sa_ref[...], sb_ref[...], bd_ref[...]

    def head_norm_rope(zz, gain):
        ss = jnp.dot((zz * zz).astype(BF16), bd, preferred_element_type=F32)
        y = zz * lax.rsqrt(ss * (1.0 / HEAD_DIM) + EPS) * gain
        return (y * cos + pltpu.roll(y, LANES - 16, axis=1) * sa
                + pltpu.roll(y, 16, axis=1) * sb)

    for s in range(ATTN_W // LANES):
        qs = head_norm_rope(z[:, s * LANES:(s + 1) * LANES], qg_ref[...])
        q_ref[:, s * LANES:(s + 1) * LANES] = (qs * (HEAD_DIM ** -0.5)).astype(BF16)
    k_ref[...] = head_norm_rope(z[:, ATTN_W:ATTN_W + KV_W], kg_ref[...])
    v_ref[...] = z[:, ATTN_W + KV_W:ATTN_W + 2 * KV_W]
    u_ref[...] = z[:, ATTN_W + 2 * KV_W:MIX_IN]


def _l0_in(xp, xs, mods, g, w_in, qg, kg, bd, tabs):
    tab_spec = pl.BlockSpec(
        (TB, LANES), lambda i: (jnp.where(i < NT_CTX, T_LAT // TB, (i - NT_CTX) % (T_LAT // TB)), 0))
    const = lambda shape: pl.BlockSpec(shape, lambda i: (0,) * len(shape))
    return pl.pallas_call(
        _l0_in_body,
        grid=(NT,),
        in_specs=[pl.BlockSpec((TB, D), lambda i: (jnp.minimum(i, NT_CTX - 1), 0)),
                  pl.BlockSpec((TB, D), lambda i: (jnp.maximum(i - NT_CTX, 0), 0)),
                  pl.BlockSpec((1, 1, 6 * D), lambda i: (_mod_row(i, NT_CTX, T_LAT // TB), 0, 0)),
                  const((1, D)), const((D, MIX_IN)), const((1, LANES)), const((1, LANES)),
                  const((LANES, LANES)), tab_spec, tab_spec, tab_spec],
        out_specs=[pl.BlockSpec((TB, ATTN_W), lambda i: (i, 0)),
                   pl.BlockSpec((TB, KV_W), lambda i: (i, 0)),
                   pl.BlockSpec((TB, KV_W), lambda i: (i, 0)),
                   pl.BlockSpec((TB, POOL_W), lambda i: (i, 0))],
        out_shape=[jax.ShapeDtypeStruct((N_TOK, ATTN_W), BF16),
                   jax.ShapeDtypeStruct((N_TOK, KV_W), F32),
                   jax.ShapeDtypeStruct((N_TOK, KV_W), F32),
                   jax.ShapeDtypeStruct((N_TOK, POOL_W), F32)],
        compiler_params=_cparams(1),
        name="l0_in",
    )(xp, xs, mods, g, w_in, qg, kg, bd, *tabs)


def _head_halves(x):
    z = jnp.zeros_like(x)
    return jnp.concatenate([x, z], axis=1), jnp.concatenate([z, x], axis=1)


_NT_DIMS = (((1,), (1,)), ((), ()))


def _ones_halves(x):
    one = jnp.ones_like(x)
    return jnp.concatenate([x, one], axis=1), jnp.concatenate([one, x], axis=1)


def _sink_attend(scores, values, sk, half):
    mx = sk
    for sc in scores:
        mx = jnp.maximum(mx, jnp.max(sc, axis=-1, keepdims=True))
    acc = None
    for sc, val in zip(scores, values):
        part = jnp.dot(jnp.exp(sc - mx).astype(BF16), val, preferred_element_type=F32)
        acc = part if acc is None else acc + part
    ones_lane = HEAD_DIM * (1 - half)
    den = acc[:, ones_lane:ones_lane + 1] + jnp.exp(sk - mx)
    return acc * (1.0 / den)


def _sink_col(sink_ref, heads, rows):
    return jnp.concatenate([jnp.broadcast_to(sink_ref[h:h + 1, 0:1], (rows, 1)) for h in heads], axis=0)


def _ctx_attn_body(q_ref, k_ref, v_ref, sink_ref, o_ref):
    k = k_ref[...].astype(BF16)
    v = v_ref[...].astype(BF16)
    lo = lax.broadcasted_iota(I32, (T_CTX, LANES), 1) < HEAD_DIM
    for j in range(KV_W // HEAD_DIM):
        kj = k[:, j * HEAD_DIM:(j + 1) * HEAD_DIM]
        vj = v[:, j * HEAD_DIM:(j + 1) * HEAD_DIM]
        k_halves = _head_halves(kj)
        vd = jnp.concatenate([vj, vj], axis=1)
        q2 = jnp.concatenate([q_ref[:, (2 * j) * LANES:(2 * j + 1) * LANES],
                              q_ref[:, (2 * j + 1) * LANES:(2 * j + 2) * LANES]], axis=0)
        outs = []
        for half in range(2):
            sc = lax.dot_general(q2, k_halves[half], _NT_DIMS, preferred_element_type=F32)
            sk = _sink_col(sink_ref, (4 * j + half, 4 * j + 2 + half), T_CTX)
            mx = jnp.maximum(sk, jnp.max(sc, axis=-1, keepdims=True))
            p = jnp.exp(sc - mx)
            inv = 1.0 / (jnp.exp(sk - mx) + jnp.sum(p, axis=-1, keepdims=True))
            outs.append(jnp.dot((p * inv).astype(BF16), vd, preferred_element_type=F32))
        for s2 in range(2):
            rows = slice(s2 * T_CTX, (s2 + 1) * T_CTX)
            o_ref[:, (2 * j + s2) * LANES:(2 * j + s2 + 1) * LANES] = (
                jnp.where(lo, outs[0][rows], outs[1][rows]).astype(BF16))


def _ctx_attn(q, k, v, sink_b):
    return pl.pallas_call(
        _ctx_attn_body,
        grid=(N_CTX_B,),
        in_specs=[pl.BlockSpec((T_CTX, ATTN_W), lambda b: (b, 0)),
                  pl.BlockSpec((T_CTX, KV_W), lambda b: (b, 0)),
                  pl.BlockSpec((T_CTX, KV_W), lambda b: (b, 0)),
                  pl.BlockSpec((SUBLANES, LANES), lambda b: (0, 0))],
        out_specs=pl.BlockSpec((T_CTX, ATTN_W), lambda b: (b, 0)),
        out_shape=jax.ShapeDtypeStruct((N_CTX, ATTN_W), BF16),
        compiler_params=_cparams(1),
        name="ctx_attn",
    )(q, k, v, sink_b)


QB = 128
SPAN = QB + 2 * WINDOW


def _lat_attn_body(q_ref, k_ref, v_ref, ck_ref, cv_ref, sink_ref, o_ref):
    qb = pl.program_id(1)
    start = qb * QB
    kws, vws = [], []
    for c in (-1, 0, 1):
        cs = pl.multiple_of(jnp.clip(start + c * QB, 0, T_LAT - QB), QB)
        kws.append(k_ref[pl.ds(cs, QB), :])
        vws.append(v_ref[pl.ds(cs, QB), :])
    kw = jnp.concatenate(kws, axis=0).astype(BF16)
    vw = jnp.concatenate(vws, axis=0).astype(BF16)
    ck = ck_ref[0].astype(BF16)
    cv = cv_ref[0].astype(BF16)
    qpos = start + (lax.broadcasted_iota(I32, (2 * QB, SPAN), 0) & (QB - 1))
    kpos = start - WINDOW + lax.broadcasted_iota(I32, (2 * QB, SPAN), 1)
    valid = (kpos >= 0) & (kpos < T_LAT) & (jnp.abs(qpos - kpos) <= WINDOW)
    lo = lax.broadcasted_iota(I32, (QB, LANES), 1) < HEAD_DIM
    for j in range(KV_W // HEAD_DIM):
        sl = slice(j * HEAD_DIM, (j + 1) * HEAD_DIM)
        kw_halves = _head_halves(kw[:, sl])
        ck_halves = _head_halves(ck[:, sl])
        vw_halves = _ones_halves(vw[:, sl])
        cv_halves = _ones_halves(cv[:, sl])
        q2 = jnp.concatenate([q_ref[:, (2 * j) * LANES:(2 * j + 1) * LANES],
                              q_ref[:, (2 * j + 1) * LANES:(2 * j + 2) * LANES]], axis=0)
        outs = []
        for half in range(2):
            s_win = lax.dot_general(q2, kw_halves[half], _NT_DIMS, preferred_element_type=F32)
            s_win = jnp.where(valid, s_win, NEG)
            s_ctx = lax.dot_general(q2, ck_halves[half], _NT_DIMS, preferred_element_type=F32)
            sk = _sink_col(sink_ref, (4 * j + half, 4 * j + 2 + half), QB)
            outs.append(_sink_attend([s_win, s_ctx], [vw_halves[half], cv_halves[half]], sk, half))
        for s2 in range(2):
            rows = slice(s2 * QB, (s2 + 1) * QB)
            o_ref[:, (2 * j + s2) * LANES:(2 * j + s2 + 1) * LANES] = (
                jnp.where(lo, outs[0][rows], outs[1][rows]).astype(BF16))


def _lat_attn(q, k, v, ck, cv, sink_b):
    lat0 = N_CTX // T_LAT
    return pl.pallas_call(
        _lat_attn_body,
        grid=(N_LAT_B, T_LAT // QB),
        in_specs=[pl.BlockSpec((QB, ATTN_W), lambda b, i: (N_CTX // QB + b * (T_LAT // QB) + i, 0)),
                  pl.BlockSpec((T_LAT, KV_W), lambda b, i: (lat0 + b, 0)),
                  pl.BlockSpec((T_LAT, KV_W), lambda b, i: (lat0 + b, 0)),
                  pl.BlockSpec((1, PAST, KV_W), lambda b, i: (b, 0, 0)),
                  pl.BlockSpec((1, PAST, KV_W), lambda b, i: (b, 0, 0)),
                  pl.BlockSpec((SUBLANES, LANES), lambda b, i: (0, 0))],
        out_specs=pl.BlockSpec((QB, ATTN_W), lambda b, i: (b * (T_LAT // QB) + i, 0)),
        out_shape=jax.ShapeDtypeStruct((N_LAT, ATTN_W), BF16),
        compiler_params=_cparams(2),
        name="lat_attn",
    )(q, k, v, ck, cv, sink_b)


def _l0_out_body(oc_ref, ol_ref, u_ref, xp_ref, xs_ref, mod_ref, pw_ref, ps_ref, wo_ref, x1_ref):
    i = pl.program_id(0)
    is_ctx = i < N_CTX // TB_MIX
    o = jnp.where(is_ctx, oc_ref[...], ol_ref[...])
    x = jnp.where(is_ctx, xp_ref[...], xs_ref[...])
    tseq = jnp.where(is_ctx, T_CTX, T_LAT)
    pos = lax.broadcasted_iota(I32, (TB_MIX, LANES), 0) & (tseq - 1)
    ys = []
    for g, win in enumerate(POOL_WINDOWS):
        hw = win // 2
        ug = u_ref[:, g * LANES:(g + 1) * LANES]
        acc = ug
        for jj in range(-hw, hw):
            if jj == 0:
                continue
            sh = pltpu.roll(ug, (-jj) % TB_MIX, axis=0)
            ok = (pos + jj >= 0) if jj < 0 else (pos + jj < tseq)
            acc = acc + jnp.where(ok, sh, 0.0)
        cnt = (jnp.minimum(pos + hw, tseq) - jnp.maximum(pos - hw, 0)).astype(F32)
        pooled = acc / cnt - ug
        ys.append(jnp.dot(pooled.astype(BF16), pw_ref[g], preferred_element_type=F32))
    y = jnp.concatenate(ys, axis=1) * ps_ref[...]
    out = (jnp.dot(o, wo_ref[0:ATTN_W, :], preferred_element_type=F32)
           + jnp.dot(y.astype(BF16), wo_ref[ATTN_W:ATTN_W + POOL_W, :], preferred_element_type=F32))
    x1_ref[...] = x + mod_ref[0][:, 2 * D:3 * D] * out


def _l0_out(o_ctx, o_lat, u, xp, xs, mods, pool_w, pool_scale, w_out):
    ntc = N_CTX // TB_MIX
    const = lambda shape: pl.BlockSpec(shape, lambda i: (0,) * len(shape))
    ctx_map = lambda i: (jnp.minimum(i, ntc - 1), 0)
    lat_map = lambda i: (jnp.maximum(i - ntc, 0), 0)
    return pl.pallas_call(
        _l0_out_body,
        grid=(N_TOK // TB_MIX,),
        in_specs=[pl.BlockSpec((TB_MIX, ATTN_W), ctx_map),
                  pl.BlockSpec((TB_MIX, ATTN_W), lat_map),
                  pl.BlockSpec((TB_MIX, POOL_W), lambda i: (i, 0)),
                  pl.BlockSpec((TB_MIX, D), ctx_map),
                  pl.BlockSpec((TB_MIX, D), lat_map),
                  pl.BlockSpec((1, 1, 6 * D), lambda i: (_mod_row(i, ntc, 1), 0, 0)),
                  const((len(POOL_WINDOWS), LANES, LANES)), const((1, POOL_W)), const((D, D))],
        out_specs=pl.BlockSpec((TB_MIX, D), lambda i: (i, 0)),
        out_shape=jax.ShapeDtypeStruct((N_TOK, D), F32),
        compiler_params=_cparams(1, VMEM_LIMIT),
        name="l0_out",
    )(o_ctx, o_lat, u, xp, xs, mods, pool_w, pool_scale, w_out)


def _first_max(vals):
    best, idx = vals[0], jnp.zeros(vals[0].shape, I32)
    for r in range(1, len(vals)):
        better = vals[r] > best
        idx = jnp.where(better, r, idx)
        best = jnp.where(better, vals[r], best)
    return best, idx


def _softmax_rows(rows):
    mx = functools.reduce(jnp.maximum, rows)
    ex = [jnp.exp(r - mx) for r in rows]
    tot = functools.reduce(lambda a, b: a + b, ex)
    return [e / tot for e in ex]


def _route_body(x_ref, mod_ref, g_ref, wr_ref, br_ref, tri_ref,
                xext_ref, info_ref, cnt_ref, base_ref):
    i = pl.program_id(0)

    @pl.when(i == 0)
    def _():
        base_ref[...] = jnp.zeros_like(base_ref)

    x = x_ref[...]
    m = mod_ref[0]
    h = _modulate(x, g_ref[...], m[:, 3 * D:4 * D], m[:, 4 * D:5 * D])

    hh = h.astype(BF16)
    hl = (h - hh.astype(F32)).astype(BF16)
    wh, wl = wr_ref[0], wr_ref[1]
    lg = (lax.dot_general(wh, hh, _NT_DIMS, preferred_element_type=F32)
          + lax.dot_general(wl, hh, _NT_DIMS, preferred_element_type=F32)
          + lax.dot_general(wh, hl, _NT_DIMS, preferred_element_type=F32)) + br_ref[:, 0:1]

    pg = _softmax_rows([lg[N_EXPERTS + r:N_EXPERTS + r + 1] for r in range(N_GROUPS)])
    pg_top, gi = _first_max(pg)
    le = []
    for j in range(PER_GROUP):
        sel = lg[(N_GROUPS - 1) * PER_GROUP + j:(N_GROUPS - 1) * PER_GROUP + j + 1]
        for g in range(N_GROUPS - 2, -1, -1):
            sel = jnp.where(gi == g, lg[g * PER_GROUP + j:g * PER_GROUP + j + 1], sel)
        le.append(sel)
    pe = _softmax_rows(le)
    p1, i1 = _first_max(pe)
    p2, i2 = _first_max([jnp.where(i1 == j, -1.0, pe[j]) for j in range(PER_GROUP)])
    den = p1 + p2
    w1 = pg_top * p1 / den
    w2 = pg_top * p2 / den

    lo = jnp.minimum(i1, i2)
    hi = jnp.maximum(i1, i2)
    cls = gi * PAIRS + jnp.where(lo == 0, 0, jnp.where(lo == 1, 3, 5)) + hi - lo - 1
    w_lo = jnp.where(i1 == lo, w1, w2)
    w_hi = jnp.where(i1 == lo, w2, w1)

    crow = lax.broadcasted_iota(I32, (CLASS_ROWS, TB), 0)
    hit = crow == cls
    onehot = jnp.where(hit, 1.0, 0.0)
    before = jnp.dot(onehot.astype(BF16), tri_ref[...], preferred_element_type=F32)
    before = before + base_ref[:, 0:1]
    rank = jnp.sum(jnp.where(hit, before, 0.0), axis=0, keepdims=True)
    base_ref[...] = base_ref[...] + jnp.sum(onehot, axis=1, keepdims=True)
    cnt_ref[...] = base_ref[...]

    mod_id = jnp.zeros_like(w1) + _mod_row(i, NT_CTX, T_LAT // TB).astype(F32)
    zero = jnp.zeros_like(w1)
    info_ref[...] = jnp.concatenate([cls.astype(F32), rank, zero, zero, zero, zero, zero, zero], axis=0)
    side = jnp.concatenate([w_lo, w_hi, mod_id, jnp.zeros((LANES - 3, TB), F32)], axis=0).T
    xext_ref[:, 0:D] = x
    xext_ref[:, D:XEXT] = side


def _route(x, mods, layer, g, wr, br, tri):
    const = lambda shape: pl.BlockSpec(shape, lambda i: (0,) * len(shape))
    return pl.pallas_call(
        _route_body,
        grid=(NT,),
        in_specs=[pl.BlockSpec((TB, D), lambda i: (i, 0)),
                  pl.BlockSpec((1, 1, 6 * D),
                               lambda i: (layer * SUBLANES + _mod_row(i, NT_CTX, T_LAT // TB), 0, 0)),
                  const((1, D)), const((2, ROUTE_ROWS, D)), const((ROUTE_ROWS, LANES)),
                  const((TB, TB))],
        out_specs=[pl.BlockSpec((TB, XEXT), lambda i: (i, 0)),
                   pl.BlockSpec((SUBLANES, TB), lambda i: (0, i)),
                   pl.BlockSpec((CLASS_ROWS, LANES), lambda i: (0, 0))],
        out_shape=[jax.ShapeDtypeStruct((N_TOK, XEXT), F32),
                   jax.ShapeDtypeStruct((SUBLANES, N_TOK), F32),
                   jax.ShapeDtypeStruct((CLASS_ROWS, LANES), F32)],
        scratch_shapes=[pltpu.VMEM((CLASS_ROWS, LANES), F32)],
        compiler_params=_cparams(1),
        name=f"route{layer}",
    )(x, mods, g, wr, br, tri)


def _plan(info, counts):
    cls = info[0].astype(I32)
    rank = info[1].astype(I32)
    cnt = counts[:N_CLASS, 0].astype(I32)
    tiles = (cnt + TM - 1) // TM
    tend = jnp.cumsum(tiles)
    tstart = tend - tiles
    n_used = tend[-1]
    cidx = jnp.arange(N_CLASS, dtype=I32)
    pos = jnp.sum(jnp.where(cls[:, None] == cidx, tstart * TM, 0), axis=-1) + rank
    n = jnp.arange(NT_FFN, dtype=I32)
    tile = jnp.minimum(n, jnp.maximum(n_used - 1, 0))
    tcls = jnp.minimum(jnp.sum((tile[:, None] >= tend[None, :]).astype(I32), axis=1), N_CLASS - 1)
    of_cls = tcls[:, None] == cidx
    rows = jnp.sum(jnp.where(of_cls, cnt, 0), axis=1) - (tile - jnp.sum(jnp.where(of_cls, tstart, 0), axis=1)) * TM
    rows = jnp.where(n < n_used, jnp.clip(rows, 0, TM), 0)
    chunks = (rows + CH - 1) // CH
    pair = tcls % PAIRS
    lo = (pair >= 3).astype(I32) + (pair >= 5).astype(I32)
    hi = jnp.where(pair < 3, pair + 1, jnp.where(pair < 5, pair - 1, 3))
    e_lo = (tcls // PAIRS) * PER_GROUP + lo
    e_hi = (tcls // PAIRS) * PER_GROUP + hi
    return pos, e_lo, e_hi, n_used.reshape(1), chunks


OCT = TM // SUBLANES
CH = 32


def _ffn_body(pos_ref, elo_ref, ehi_ref, nu_ref, ch_ref,
              xext_hbm, mod_ref, g_ref, w1a_ref, w1b_ref, w3a_ref, w3b_ref, w2a_ref, w2b_ref,
              out_hbm, src_ref, dst_ref, xbuf, ybuf, wb1a, wb1b, wb3a, wb3b, wb2a, wb2b, gsem, ssem):
    n = pl.program_id(0)
    n_used = nu_ref[0]

    def gather_copy(tile, s, c, j):
        return pltpu.make_async_copy(
            xext_hbm.at[pl.ds(src_ref[tile * TM + c * CH + j], 1)],
            xbuf.at[s, c * (CH // SUBLANES) + j // SUBLANES, pl.ds(j % SUBLANES, 1)], gsem.at[s])

    def scatter_copy(tile, s, c, j):
        return pltpu.make_async_copy(
            ybuf.at[s, c * (CH // SUBLANES) + j // SUBLANES, pl.ds(j % SUBLANES, 1)],
            out_hbm.at[pl.ds(dst_ref[tile * TM + c * CH + j], 1)], ssem.at[s])

    def start_rows(copy, tile, s):
        def chunk(c, carry):
            for j in range(CH):
                copy(tile, s, c, j).start(priority=j % 2)
            return carry
        lax.fori_loop(0, ch_ref[tile], chunk, 0)

    def wait_rows(src, dst, sem, tile):
        def chunk(c, carry):
            pltpu.make_async_copy(src, dst, sem).wait()
            return carry
        lax.fori_loop(0, ch_ref[tile], chunk, 0)

    def wait_gather(tile, s):
        rows = xbuf.at[s, pl.ds(0, CH // SUBLANES)]
        wait_rows(rows, rows, gsem.at[s], tile)

    def wait_scatter(tile, s):
        rows = ybuf.at[s, pl.ds(0, CH // SUBLANES)]
        wait_rows(rows, rows, ssem.at[s], tile)

    @pl.when(n == 0)
    def _():
        def pad_rows(tile, carry):
            @pl.when(ch_ref[tile] > 0)
            def _():
                first = tile * TM + (ch_ref[tile] - 1) * CH
                for j in range(CH):
                    src_ref[first + j] = 0
                    dst_ref[first + j] = N_TOK + ((first + j) & (2 * TM - 1))
            return carry

        lax.fori_loop(0, NT_FFN, pad_rows, 0)

        def put(t, carry):
            p = pos_ref[t]
            src_ref[p] = t
            dst_ref[p] = t
            return carry

        lax.fori_loop(0, N_TOK, put, 0, unroll=8)

        xbuf[...] = jnp.zeros_like(xbuf)
        ybuf[...] = jnp.zeros_like(ybuf)
        for s in range(2):
            dumps = [pltpu.make_async_copy(
                ybuf.at[s, q], out_hbm.at[pl.ds(N_TOK + s * TM + q * SUBLANES, SUBLANES)], ssem.at[s])
                for q in range(OCT)]
            for dump in dumps:
                dump.start()
            for dump in dumps:
                dump.wait()
        start_rows(gather_copy, 0, 0)

    def step(slot):
        @pl.when(n + 1 < n_used)
        def _():
            start_rows(gather_copy, n + 1, 1 - slot)

        wait_gather(n, slot)

        @pl.when(n >= 2)
        def _():
            wait_scatter(n - 2, slot)

        prev = jnp.maximum(n - 1, 0)

        @pl.when((n == 0) | (elo_ref[n] != elo_ref[prev]))
        def _():
            wb1a[...] = w1a_ref[0, 0].astype(BF16)
            wb3a[...] = w3a_ref[0, 0].astype(BF16)
            wb2a[...] = w2a_ref[0, 0].astype(BF16)

        @pl.when((n == 0) | (ehi_ref[n] != ehi_ref[prev]))
        def _():
            wb1b[...] = w1b_ref[0, 0].astype(BF16)
            wb3b[...] = w3b_ref[0, 0].astype(BF16)
            wb2b[...] = w2b_ref[0, 0].astype(BF16)

        xe = xbuf[slot].reshape(TM, XEXT)
        x = xe[:, 0:D]
        w_lo = xe[:, D:D + 1]
        w_hi = xe[:, D + 1:D + 2]
        mod_id = xe[:, D + 2:D + 3]

        def pick(lo, hi):
            return jnp.where(mod_id < 0.5, mod_ref[0][:, lo:hi],
                             jnp.where(mod_id < 1.5, mod_ref[1][:, lo:hi], mod_ref[2][:, lo:hi]))

        h = _modulate(x, g_ref[...], pick(3 * D, 4 * D), pick(4 * D, 5 * D)).astype(BF16)

        def act(wb1, wb3, w):
            h1 = jnp.dot(h, wb1[...], preferred_element_type=F32)
            h3 = jnp.dot(h, wb3[...], preferred_element_type=F32)
            return ((h1 * jax.nn.sigmoid(h1)) * h3 * w).astype(BF16)

        y = (jnp.dot(act(wb1a, wb3a, w_lo), wb2a[...], preferred_element_type=F32)
             + jnp.dot(act(wb1b, wb3b, w_hi), wb2b[...], preferred_element_type=F32))
        ybuf[slot] = (x + pick(5 * D, 6 * D) * y).reshape(OCT, SUBLANES, D)
        start_rows(scatter_copy, n, slot)

        @pl.when(n == n_used - 1)
        def _():
            @pl.when(n >= 1)
            def _():
                wait_scatter(n - 1, 1 - slot)
            wait_scatter(n, slot)

    for s in range(2):
        @pl.when((n < n_used) & (n % 2 == s))
        def _():
            step(s)


def _ffn(pos, e_lo, e_hi, n_used, chunks, xext, mods, layer, g, w1, w3, w2):
    lo_map = lambda n, p, el, eh, nu, ch: (layer, el[n], 0, 0)
    hi_map = lambda n, p, el, eh, nu, ch: (layer, eh[n], 0, 0)
    up = lambda imap: pl.BlockSpec((1, 1, D, D_EXPERT), imap)
    down = lambda imap: pl.BlockSpec((1, 1, D_EXPERT, D), imap)
    return pl.pallas_call(
        _ffn_body,
        grid_spec=pltpu.PrefetchScalarGridSpec(
            num_scalar_prefetch=5, grid=(NT_FFN,),
            in_specs=[pl.BlockSpec(memory_space=pl.ANY),
                      pl.BlockSpec((SUBLANES, 1, 6 * D), lambda n, *_: (layer, 0, 0)),
                      pl.BlockSpec((1, D), lambda n, *_: (0, 0)),
                      up(lo_map), up(hi_map), up(lo_map), up(hi_map), down(lo_map), down(hi_map)],
            out_specs=pl.BlockSpec(memory_space=pl.ANY),
            scratch_shapes=[pltpu.SMEM((P_FFN,), I32), pltpu.SMEM((P_FFN,), I32),
                            pltpu.VMEM((2, OCT, SUBLANES, XEXT), F32),
                            pltpu.VMEM((2, OCT, SUBLANES, D), F32),
                            pltpu.VMEM((D, D_EXPERT), BF16), pltpu.VMEM((D, D_EXPERT), BF16),
                            pltpu.VMEM((D, D_EXPERT), BF16), pltpu.VMEM((D, D_EXPERT), BF16),
                            pltpu.VMEM((D_EXPERT, D), BF16), pltpu.VMEM((D_EXPERT, D), BF16),
                            pltpu.SemaphoreType.DMA((2,)), pltpu.SemaphoreType.DMA((2,))]),
        out_shape=jax.ShapeDtypeStruct((N_TOK + 2 * TM, D), F32),
        compiler_params=_cparams(1, VMEM_LIMIT),
        name=f"ffn{layer}",
    )(pos, e_lo, e_hi, n_used, chunks, xext, mods, g, w1, w1, w3, w3, w2, w2)


def _moe(x, mods, layer, g, wr, br, tri, w1, w3, w2):
    xext, info, counts = _route(x, mods, layer, g, wr, br, tri)
    pos, e_lo, e_hi, n_used, chunks = _plan(info, counts)
    return _ffn(pos, e_lo, e_hi, n_used, chunks, xext, mods, layer, g, w1, w3, w2)


FG = 256


def _l1_in_body(x_ref, mod_ref, g_ref, w_ref, c_ref, s_ref, zc_ref, zs_ref):
    m = mod_ref[0]
    h = _modulate(x_ref[...], g_ref[...], m[:, 0:D], m[:, D:2 * D])
    z = jnp.dot(h.astype(BF16), w_ref[...], preferred_element_type=F32).astype(BF16)
    for g in range(D // FG):
        zg = z[:, g * FG:(g + 1) * FG]
        zc_ref[:, g * FG:(g + 1) * FG] = jnp.dot(zg, c_ref[...], preferred_element_type=F32).astype(BF16)
        zs_ref[:, g * FG:(g + 1) * FG] = jnp.dot(zg, s_ref[...], preferred_element_type=F32).astype(BF16)


def _l1_in(x, mods, g, w, c256, s256):
    const = lambda shape: pl.BlockSpec(shape, lambda i: (0,) * len(shape))
    return pl.pallas_call(
        _l1_in_body,
        grid=(NT,),
        in_specs=[pl.BlockSpec((TB, D), lambda i: (i, 0)),
                  pl.BlockSpec((1, 1, 6 * D),
                               lambda i: (SUBLANES + _mod_row(i, NT_CTX, T_LAT // TB), 0, 0)),
                  const((1, D)), const((D, D)), const((FG, FG)), const((FG, FG))],
        out_specs=[pl.BlockSpec((TB, D), lambda i: (i, 0)), pl.BlockSpec((TB, D), lambda i: (i, 0))],
        out_shape=[jax.ShapeDtypeStruct((N_TOK, D), BF16), jax.ShapeDtypeStruct((N_TOK, D), BF16)],
        compiler_params=_cparams(1),
        name="l1_in",
    )(x, mods, g, w, c256, s256)


def _l1_out_body(zc_t_ref, zs_t_ref, zc_q_ref, zs_q_ref, c256_ref, s256_ref, c1k_ref, s1k_ref,
                 x_ref, mod_ref, wo_ref, o_ref, f_ref):
    i = pl.program_id(0)

    @pl.when(i < NT_CTX)
    def _():
        for q in range(TB // T_CTX):
            rows = slice(q * T_CTX, (q + 1) * T_CTX)
            f = (jnp.dot(c256_ref[...], zc_t_ref[rows, :], preferred_element_type=F32)
                 - jnp.dot(s256_ref[...], zs_t_ref[rows, :], preferred_element_type=F32))
            f_ref[rows, :] = f.astype(BF16)

    @pl.when(i >= NT_CTX)
    def _():
        f = (jnp.dot(c1k_ref[...], zc_q_ref[...], preferred_element_type=F32)
             - jnp.dot(s1k_ref[...], zs_q_ref[...], preferred_element_type=F32))
        f_ref[...] = f.astype(BF16)

    out = jnp.dot(f_ref[...], wo_ref[...], preferred_element_type=F32)
    o_ref[...] = x_ref[...] + mod_ref[0][:, 2 * D:3 * D] * out


def _l1_out(zc, zs, c256, s256, c1k, s1k, x, mods, w_out):
    const = lambda shape: pl.BlockSpec(shape, lambda i: (0,) * len(shape))
    tile_map = lambda i: (jnp.minimum(i, NT_CTX - 1), 0)
    seq_map = lambda i: (N_CTX // T_LAT + jnp.maximum(i - NT_CTX, 0) // (T_LAT // TB), 0)
    row_map = lambda i: (jnp.maximum(i - NT_CTX, 0) % (T_LAT // TB), 0)
    return pl.pallas_call(
        _l1_out_body,
        grid=(NT,),
        in_specs=[pl.BlockSpec((TB, D), tile_map), pl.BlockSpec((TB, D), tile_map),
                  pl.BlockSpec((T_LAT, D), seq_map), pl.BlockSpec((T_LAT, D), seq_map),
                  const((T_CTX, T_CTX)), const((T_CTX, T_CTX)),
                  pl.BlockSpec((TB, T_LAT), row_map), pl.BlockSpec((TB, T_LAT), row_map),
                  pl.BlockSpec((TB, D), lambda i: (i, 0)),
                  pl.BlockSpec((1, 1, 6 * D),
                               lambda i: (SUBLANES + _mod_row(i, NT_CTX, T_LAT // TB), 0, 0)),
                  const((D, D))],
        out_specs=pl.BlockSpec((TB, D), lambda i: (i, 0)),
        out_shape=jax.ShapeDtypeStruct((N_TOK, D), F32),
        scratch_shapes=[pltpu.VMEM((TB, D), BF16)],
        compiler_params=_cparams(1),
        name="l1_out",
    )(zc, zs, zc, zs, c256, s256, c1k, s1k, x, mods, w_out)


def _split_hi_lo(w):
    hi = w.astype(BF16)
    lo = (w - hi.astype(F32)).astype(BF16)
    return jnp.stack([hi, lo])


def kernel(x_prompt, x_sample, cache_k, cache_v, c, c_ctx, ada_w, ada_b, norm_mix, norm_ffn, a_w_in, a_q_norm, a_k_norm, a_sink, pool_w, pool_scale, a_w_out, f_w_in, f_w_out, router_g_w, router_g_b, router_e_w, router_e_b, moe_w1, moe_w3, moe_w2):
    xp = x_prompt.reshape(N_CTX, D)
    xs = x_sample.reshape(N_LAT, D)

    cond8 = jnp.zeros((SUBLANES, D), F32).at[0].set(c_ctx).at[1:1 + N_LAT_B].set(c)
    mods = _adaln(cond8, ada_w, ada_b).reshape(DEPTH * SUBLANES, 1, 6 * D)

    tabs = _rope_tables()
    lane = np.arange(LANES)
    bd = jnp.asarray((lane[:, None] // HEAD_DIM) == (lane[None, :] // HEAD_DIM), BF16)
    tri = jnp.asarray(np.arange(TB)[:, None] < np.arange(TB)[None, :], BF16)
    c256, s256 = _dft_tables(T_CTX)
    c1k, s1k = _dft_tables(T_LAT)

    def router_operands(l):
        w = jnp.concatenate([router_e_w[l], router_g_w[l]], axis=1).T
        w = jnp.pad(w, ((0, ROUTE_ROWS - w.shape[0]), (0, 0)))
        b = jnp.concatenate([router_e_b[l], router_g_b[l]])
        b = jnp.pad(b, (0, ROUTE_ROWS - b.shape[0]))
        return _split_hi_lo(w), jnp.broadcast_to(b[:, None], (ROUTE_ROWS, LANES))

    qg = jnp.tile(a_q_norm[0], LANES // HEAD_DIM)[None, :]
    kg = jnp.tile(a_k_norm[0], LANES // HEAD_DIM)[None, :]
    q, k, v, u = _l0_in(xp, xs, mods, norm_mix[0][None, :], a_w_in[0].astype(BF16), qg, kg, bd, tabs)
    sink_b = jnp.broadcast_to(a_sink[0][:, None], (N_HEADS, LANES))
    o_ctx = _ctx_attn(q, k, v, sink_b)
    ck = cache_k[:, 0].reshape(N_LAT_B, PAST, KV_W)
    cv = cache_v[:, 0].reshape(N_LAT_B, PAST, KV_W)
    o_lat = _lat_attn(q, k, v, ck, cv, sink_b)
    x1 = _l0_out(o_ctx, o_lat, u, xp, xs, mods, pool_w[0].astype(BF16), pool_scale[0][None, :],
                 a_w_out[0].astype(BF16))
    wr, br = router_operands(0)
    x2 = _moe(x1, mods, 0, norm_ffn[0][None, :], wr, br, tri, moe_w1, moe_w3, moe_w2)

    zc, zs = _l1_in(x2, mods, norm_mix[1][None, :], f_w_in[0].astype(BF16), c256, s256)
    x3 = _l1_out(zc, zs, c256, s256, c1k, s1k, x2, mods, f_w_out[0].astype(BF16))
    wr, br = router_operands(1)
    x4 = _moe(x3, mods, 1, norm_ffn[1][None, :], wr, br, tri, moe_w1, moe_w3, moe_w2)

    new_k = k[:N_CTX].reshape(N_CTX_B, 1, T_CTX, KV_W // HEAD_DIM, HEAD_DIM)
    new_v = v[:N_CTX].reshape(N_CTX_B, 1, T_CTX, KV_W // HEAD_DIM, HEAD_DIM)
    return (x4[:N_CTX].reshape(N_CTX_B, T_CTX, D), x4[N_CTX:N_TOK].reshape(N_LAT_B, T_LAT, D),
            new_k, new_v)
```

```python
import functools

import numpy as np
import jax
import jax.numpy as jnp
from jax import lax
from jax.experimental import pallas as pl
from jax.experimental.pallas import tpu as pltpu

F32 = jnp.float32
BF16 = jnp.bfloat16
I32 = jnp.int32

D = 1024
DEPTH = 2
N_CTX_B, T_CTX = 16, 256
N_LAT_B, T_LAT = 2, 1024
N_CTX = N_CTX_B * T_CTX
N_LAT = N_LAT_B * T_LAT
N_TOK = N_CTX + N_LAT
PAST = 512
GRID_W = 64
HEAD_DIM = 64
N_HEADS = 8
ATTN_W = 512
KV_W = 128
POOL_W = 512
POOL_WINDOWS = (2, 4, 8, 16)
MIX_IN = ATTN_W + 2 * KV_W + POOL_W
WINDOW = 128
N_GROUPS = 4
PER_GROUP = 4
N_EXPERTS = 16
D_EXPERT = 512
ROPE_THETA = 10000.0
EPS = 1e-6
NEG = -1e30

LANES = 128
SUBLANES = 8
TB = 512
NT = N_TOK // TB
NT_CTX = N_CTX // TB
TB_MIX = 1024
TM = 256
PAIRS = 6
N_CLASS = N_GROUPS * PAIRS
CLASS_ROWS = 32
NT_FFN = N_TOK // TM + N_CLASS
P_FFN = NT_FFN * TM
XEXT = D + LANES
ROUTE_ROWS = 32

VMEM_LIMIT = 56 * 1024 * 1024


def _cparams(n_axes=1, vmem=None):
    return pltpu.CompilerParams(dimension_semantics=("arbitrary",) * n_axes,
                                vmem_limit_bytes=vmem)


def _modulate(x, g, shift, scale):
    ms = jnp.mean(x * x, axis=-1, keepdims=True)
    return (x * lax.rsqrt(ms + EPS) * g) * (1.0 + scale) + shift


def _mod_row(tile, tiles_ctx, tiles_per_lat):
    return (tile >= tiles_ctx).astype(I32) + (tile >= tiles_ctx + tiles_per_lat).astype(I32)


def _rope_tables():
    t = np.arange(T_LAT)
    row = (t // GRID_W).astype(np.float64)
    col = (t % GRID_W).astype(np.float64)
    nf = HEAD_DIM // 4
    freqs = ROPE_THETA ** (-np.arange(nf, dtype=np.float64) / nf)
    d = np.arange(HEAD_DIM)
    pos = np.where(d[None, :] < HEAD_DIM // 2, row[:, None], col[:, None])
    ang = pos * freqs[d % nf][None, :]
    first = (d % (HEAD_DIM // 2)) < nf
    cos = np.cos(ang)
    sin_a = np.where(first[None, :], -np.sin(ang), 0.0)
    sin_b = np.where(first[None, :], 0.0, np.sin(ang))
    ident = (np.ones((TB, HEAD_DIM)), np.zeros((TB, HEAD_DIM)), np.zeros((TB, HEAD_DIM)))
    out = []
    for tab, idt in zip((cos, sin_a, sin_b), ident):
        full = np.concatenate([tab, idt], axis=0)
        out.append(jnp.asarray(np.tile(full, (1, LANES // HEAD_DIM)), F32))
    return out


def _dft_tables(t):
    m = np.outer(np.arange(t), np.arange(t)) % t
    ang = 2.0 * np.pi * m / t
    s = 1.0 / np.sqrt(t)
    return jnp.asarray(np.cos(ang) * s, F32).astype(BF16), jnp.asarray(np.sin(ang) * s, F32).astype(BF16)


def _adaln_body(cond_ref, w_ref, b_ref, o_ref):
    c = cond_ref[...]
    s = (c * jax.nn.sigmoid(c)).astype(BF16)
    o_ref[0] = jnp.dot(s, w_ref[0].astype(BF16), preferred_element_type=F32) + b_ref[0]


def _adaln(cond8, ada_w, ada_b):
    tn = 1536
    return pl.pallas_call(
        _adaln_body,
        grid=(DEPTH, 6 * D // tn),
        in_specs=[pl.BlockSpec((SUBLANES, D), lambda l, j: (0, 0)),
                  pl.BlockSpec((1, D, tn), lambda l, j: (l, 0, j)),
                  pl.BlockSpec((1, 1, tn), lambda l, j: (l, 0, j))],
        out_specs=pl.BlockSpec((1, SUBLANES, tn), lambda l, j: (l, 0, j)),
        out_shape=jax.ShapeDtypeStruct((DEPTH, SUBLANES, 6 * D), F32),
        compiler_params=_cparams(2),
        name="adaln",
    )(cond8, ada_w, ada_b.reshape(DEPTH, 1, 6 * D))


def _l0_in_body(xp_ref, xs_ref, mod_ref, g_ref, w_ref, qg_ref, kg_ref, bd_ref,
                cos_ref, sa_ref, sb_ref, q_ref, k_ref, v_ref, u_ref):
    i = pl.program_id(0)
    x = jnp.where(i < NT_CTX, xp_ref[...], xs_ref[...])
    m = mod_ref[0]
    h = _modulate(x, g_ref[...], m[:, 0:D], m[:, D:2 * D])
    z = jnp.dot(h.astype(BF16), w_ref[...], preferred_element_type=F32)
    cos, sa, sb, bd = cos_ref[...], sa_ref[...], sb_ref[...], bd_ref[...]

    def head_norm_rope(zz, gain):
        ss = jnp.dot((zz * zz).astype(BF16), bd, preferred_element_type=F32)
        y = zz * lax.rsqrt(ss * (1.0 / HEAD_DIM) + EPS) * gain
        return (y * cos + pltpu.roll(y, LANES - 16, axis=1) * sa
                + pltpu.roll(y, 16, axis=1) * sb)

    for s in range(ATTN_W // LANES):
        qs = head_norm_rope(z[:, s * LANES:(s + 1) * LANES], qg_ref[...])
        q_ref[:, s * LANES:(s + 1) * LANES] = (qs * (HEAD_DIM ** -0.5)).astype(BF16)
    k_ref[...] = head_norm_rope(z[:, ATTN_W:ATTN_W + KV_W], kg_ref[...])
    v_ref[...] = z[:, ATTN_W + KV_W:ATTN_W + 2 * KV_W]
    u_ref[...] = z[:, ATTN_W + 2 * KV_W:MIX_IN]


def _l0_in(xp, xs, mods, g, w_in, qg, kg, bd, tabs):
    tab_spec = pl.BlockSpec(
        (TB, LANES), lambda i: (jnp.where(i < NT_CTX, T_LAT // TB, (i - NT_CTX) % (T_LAT // TB)), 0))
    const = lambda shape: pl.BlockSpec(shape, lambda i: (0,) * len(shape))
    return pl.pallas_call(
        _l0_in_body,
        grid=(NT,),
        in_specs=[pl.BlockSpec((TB, D), lambda i: (jnp.minimum(i, NT_CTX - 1), 0)),
                  pl.BlockSpec((TB, D), lambda i: (jnp.maximum(i - NT_CTX, 0), 0)),
                  pl.BlockSpec((1, 1, 6 * D), lambda i: (_mod_row(i, NT_CTX, T_LAT // TB), 0, 0)),
                  const((1, D)), const((D, MIX_IN)), const((1, LANES)), const((1, LANES)),
                  const((LANES, LANES)), tab_spec, tab_spec, tab_spec],
        out_specs=[pl.BlockSpec((TB, ATTN_W), lambda i: (i, 0)),
                   pl.BlockSpec((TB, KV_W), lambda i: (i, 0)),
                   pl.BlockSpec((TB, KV_W), lambda i: (i, 0)),
                   pl.BlockSpec((TB, POOL_W), lambda i: (i, 0))],
        out_shape=[jax.ShapeDtypeStruct((N_TOK, ATTN_W), BF16),
                   jax.ShapeDtypeStruct((N_TOK, KV_W), F32),
                   jax.ShapeDtypeStruct((N_TOK, KV_W), F32),
                   jax.ShapeDtypeStruct((N_TOK, POOL_W), F32)],
        compiler_params=_cparams(1),
        name="l0_in",
    )(xp, xs, mods, g, w_in, qg, kg, bd, *tabs)


def _head_halves(x):
    z = jnp.zeros_like(x)
    return jnp.concatenate([x, z], axis=1), jnp.concatenate([z, x], axis=1)


_NT_DIMS = (((1,), (1,)), ((), ()))


def _ones_halves(x):
    one = jnp.ones_like(x)
    return jnp.concatenate([x, one], axis=1), jnp.concatenate([one, x], axis=1)


def _sink_attend(scores, values, sk, half):
    mx = sk
    for sc in scores:
        mx = jnp.maximum(mx, jnp.max(sc, axis=-1, keepdims=True))
    acc = None
    for sc, val in zip(scores, values):
        part = jnp.dot(jnp.exp(sc - mx).astype(BF16), val, preferred_element_type=F32)
        acc = part if acc is None else acc + part
    ones_lane = HEAD_DIM * (1 - half)
    den = acc[:, ones_lane:ones_lane + 1] + jnp.exp(sk - mx)
    return acc * (1.0 / den)


def _sink_col(sink_ref, heads, rows):
    return jnp.concatenate([jnp.broadcast_to(sink_ref[h:h + 1, 0:1], (rows, 1)) for h in heads], axis=0)


def _ctx_attn_body(q_ref, k_ref, v_ref, sink_ref, o_ref):
    k = k_ref[...].astype(BF16)
    v = v_ref[...].astype(BF16)
    lo = lax.broadcasted_iota(I32, (T_CTX, LANES), 1) < HEAD_DIM
    for j in range(KV_W // HEAD_DIM):
        kj = k[:, j * HEAD_DIM:(j + 1) * HEAD_DIM]
        vj = v[:, j * HEAD_DIM:(j + 1) * HEAD_DIM]
        k_halves = _head_halves(kj)
        vd = jnp.concatenate([vj, vj], axis=1)
        q2 = jnp.concatenate([q_ref[:, (2 * j) * LANES:(2 * j + 1) * LANES],
                              q_ref[:, (2 * j + 1) * LANES:(2 * j + 2) * LANES]], axis=0)
        outs = []
        for half in range(2):
            sc = lax.dot_general(q2, k_halves[half], _NT_DIMS, preferred_element_type=F32)
            sk = _sink_col(sink_ref, (4 * j + half, 4 * j + 2 + half), T_CTX)
            mx = jnp.maximum(sk, jnp.max(sc, axis=-1, keepdims=True))
            p = jnp.exp(sc - mx)
            inv = 1.0 / (jnp.exp(sk - mx) + jnp.sum(p, axis=-1, keepdims=True))
            outs.append(jnp.dot((p * inv).astype(BF16), vd, preferred_element_type=F32))
        for s2 in range(2):
            rows = slice(s2 * T_CTX, (s2 + 1) * T_CTX)
            o_ref[:, (2 * j + s2) * LANES:(2 * j + s2 + 1) * LANES] = (
                jnp.where(lo, outs[0][rows], outs[1][rows]).astype(BF16))


def _ctx_attn(q, k, v, sink_b):
    return pl.pallas_call(
        _ctx_attn_body,
        grid=(N_CTX_B,),
        in_specs=[pl.BlockSpec((T_CTX, ATTN_W), lambda b: (b, 0)),
                  pl.BlockSpec((T_CTX, KV_W), lambda b: (b, 0)),
                  pl.BlockSpec((T_CTX, KV_W), lambda b: (b, 0)),
                  pl.BlockSpec((SUBLANES, LANES), lambda b: (0, 0))],
        out_specs=pl.BlockSpec((T_CTX, ATTN_W), lambda b: (b, 0)),
        out_shape=jax.ShapeDtypeStruct((N_CTX, ATTN_W), BF16),
        compiler_params=_cparams(1),
        name="ctx_attn",
    )(q, k, v, sink_b)


QB = 128
SPAN = QB + 2 * WINDOW


def _lat_attn_body(q_ref, k_ref, v_ref, ck_ref, cv_ref, sink_ref, o_ref):
    qb = pl.program_id(1)
    start = qb * QB
    kws, vws = [], []
    for c in (-1, 0, 1):
        cs = pl.multiple_of(jnp.clip(start + c * QB, 0, T_LAT - QB), QB)
        kws.append(k_ref[pl.ds(cs, QB), :])
        vws.append(v_ref[pl.ds(cs, QB), :])
    kw = jnp.concatenate(kws, axis=0).astype(BF16)
    vw = jnp.concatenate(vws, axis=0).astype(BF16)
    ck = ck_ref[0].astype(BF16)
    cv = cv_ref[0].astype(BF16)
    qpos = start + (lax.broadcasted_iota(I32, (2 * QB, SPAN), 0) & (QB - 1))
    kpos = start - WINDOW + lax.broadcasted_iota(I32, (2 * QB, SPAN), 1)
    valid = (kpos >= 0) & (kpos < T_LAT) & (jnp.abs(qpos - kpos) <= WINDOW)
    lo = lax.broadcasted_iota(I32, (QB, LANES), 1) < HEAD_DIM
    for j in range(KV_W // HEAD_DIM):
        sl = slice(j * HEAD_DIM, (j + 1) * HEAD_DIM)
        kw_halves = _head_halves(kw[:, sl])
        ck_halves = _head_halves(ck[:, sl])
        vw_halves = _ones_halves(vw[:, sl])
        cv_halves = _ones_halves(cv[:, sl])
        q2 = jnp.concatenate([q_ref[:, (2 * j) * LANES:(2 * j + 1) * LANES],
                              q_ref[:, (2 * j + 1) * LANES:(2 * j + 2) * LANES]], axis=0)
        outs = []
        for half in range(2):
            s_win = lax.dot_general(q2, kw_halves[half], _NT_DIMS, preferred_element_type=F32)
            s_win = jnp.where(valid, s_win, NEG)
            s_ctx = lax.dot_general(q2, ck_halves[half], _NT_DIMS, preferred_element_type=F32)
            sk = _sink_col(sink_ref, (4 * j + half, 4 * j + 2 + half), QB)
            outs.append(_sink_attend([s_win, s_ctx], [vw_halves[half], cv_halves[half]], sk, half))
        for s2 in range(2):
            rows = slice(s2 * QB, (s2 + 1) * QB)
            o_ref[:, (2 * j + s2) * LANES:(2 * j + s2 + 1) * LANES] = (
                jnp.where(lo, outs[0][rows], outs[1][rows]).astype(BF16))


def _lat_attn(q, k, v, ck, cv, sink_b):
    lat0 = N_CTX // T_LAT
    return pl.pallas_call(
        _lat_attn_body,
        grid=(N_LAT_B, T_LAT // QB),
        in_specs=[pl.BlockSpec((QB, ATTN_W), lambda b, i: (N_CTX // QB + b * (T_LAT // QB) + i, 0)),
                  pl.BlockSpec((T_LAT, KV_W), lambda b, i: (lat0 + b, 0)),
                  pl.BlockSpec((T_LAT, KV_W), lambda b, i: (lat0 + b, 0)),
                  pl.BlockSpec((1, PAST, KV_W), lambda b, i: (b, 0, 0)),
                  pl.BlockSpec((1, PAST, KV_W), lambda b, i: (b, 0, 0)),
                  pl.BlockSpec((SUBLANES, LANES), lambda b, i: (0, 0))],
        out_specs=pl.BlockSpec((QB, ATTN_W), lambda b, i: (b * (T_LAT // QB) + i, 0)),
        out_shape=jax.ShapeDtypeStruct((N_LAT, ATTN_W), BF16),
        compiler_params=_cparams(2),
        name="lat_attn",
    )(q, k, v, ck, cv, sink_b)


def _l0_out_body(oc_ref, ol_ref, u_ref, xp_ref, xs_ref, mod_ref, pw_ref, ps_ref, wo_ref, x1_ref):
    i = pl.program_id(0)
    is_ctx = i < N_CTX // TB_MIX
    o = jnp.where(is_ctx, oc_ref[...], ol_ref[...])
    x = jnp.where(is_ctx, xp_ref[...], xs_ref[...])
    tseq = jnp.where(is_ctx, T_CTX, T_LAT)
    pos = lax.broadcasted_iota(I32, (TB_MIX, LANES), 0) & (tseq - 1)
    ys = []
    for g, win in enumerate(POOL_WINDOWS):
        hw = win // 2
        ug = u_ref[:, g * LANES:(g + 1) * LANES]
        acc = ug
        for jj in range(-hw, hw):
            if jj == 0:
                continue
            sh = pltpu.roll(ug, (-jj) % TB_MIX, axis=0)
            ok = (pos + jj >= 0) if jj < 0 else (pos + jj < tseq)
            acc = acc + jnp.where(ok, sh, 0.0)
        cnt = (jnp.minimum(pos + hw, tseq) - jnp.maximum(pos - hw, 0)).astype(F32)
        pooled = acc / cnt - ug
        ys.append(jnp.dot(pooled.astype(BF16), pw_ref[g], preferred_element_type=F32))
    y = jnp.concatenate(ys, axis=1) * ps_ref[...]
    out = (jnp.dot(o, wo_ref[0:ATTN_W, :], preferred_element_type=F32)
           + jnp.dot(y.astype(BF16), wo_ref[ATTN_W:ATTN_W + POOL_W, :], preferred_element_type=F32))
    x1_ref[...] = x + mod_ref[0][:, 2 * D:3 * D] * out


def _l0_out(o_ctx, o_lat, u, xp, xs, mods, pool_w, pool_scale, w_out):
    ntc = N_CTX // TB_MIX
    const = lambda shape: pl.BlockSpec(shape, lambda i: (0,) * len(shape))
    ctx_map = lambda i: (jnp.minimum(i, ntc - 1), 0)
    lat_map = lambda i: (jnp.maximum(i - ntc, 0), 0)
    return pl.pallas_call(
        _l0_out_body,
        grid=(N_TOK // TB_MIX,),
        in_specs=[pl.BlockSpec((TB_MIX, ATTN_W), ctx_map),
                  pl.BlockSpec((TB_MIX, ATTN_W), lat_map),
                  pl.BlockSpec((TB_MIX, POOL_W), lambda i: (i, 0)),
                  pl.BlockSpec((TB_MIX, D), ctx_map),
                  pl.BlockSpec((TB_MIX, D), lat_map),
                  pl.BlockSpec((1, 1, 6 * D), lambda i: (_mod_row(i, ntc, 1), 0, 0)),
                  const((len(POOL_WINDOWS), LANES, LANES)), const((1, POOL_W)), const((D, D))],
        out_specs=pl.BlockSpec((TB_MIX, D), lambda i: (i, 0)),
        out_shape=jax.ShapeDtypeStruct((N_TOK, D), F32),
        compiler_params=_cparams(1, VMEM_LIMIT),
        name="l0_out",
    )(o_ctx, o_lat, u, xp, xs, mods, pool_w, pool_scale, w_out)


def _first_max(vals):
    best, idx = vals[0], jnp.zeros(vals[0].shape, I32)
    for r in range(1, len(vals)):
        better = vals[r] > best
        idx = jnp.where(better, r, idx)
        best = jnp.where(better, vals[r], best)
    return best, idx


def _softmax_rows(rows):
    mx = functools.reduce(jnp.maximum, rows)
    ex = [jnp.exp(r - mx) for r in rows]
    tot = functools.reduce(lambda a, b: a + b, ex)
    return [e / tot for e in ex]


def _route_body(x_ref, mod_ref, g_ref, wr_ref, br_ref, tri_ref,
                xext_ref, info_ref, cnt_ref, base_ref):
    i = pl.program_id(0)

    @pl.when(i == 0)
    def _():
        base_ref[...] = jnp.zeros_like(base_ref)

    x = x_ref[...]
    m = mod_ref[0]
    h = _modulate(x, g_ref[...], m[:, 3 * D:4 * D], m[:, 4 * D:5 * D])

    hh = h.astype(BF16)
    hl = (h - hh.astype(F32)).astype(BF16)
    wh, wl = wr_ref[0], wr_ref[1]
    lg = (lax.dot_general(wh, hh, _NT_DIMS, preferred_element_type=F32)
          + lax.dot_general(wl, hh, _NT_DIMS, preferred_element_type=F32)
          + lax.dot_general(wh, hl, _NT_DIMS, preferred_element_type=F32)) + br_ref[:, 0:1]

    pg = _softmax_rows([lg[N_EXPERTS + r:N_EXPERTS + r + 1] for r in range(N_GROUPS)])
    pg_top, gi = _first_max(pg)
    le = []
    for j in range(PER_GROUP):
        sel = lg[(N_GROUPS - 1) * PER_GROUP + j:(N_GROUPS - 1) * PER_GROUP + j + 1]
        for g in range(N_GROUPS - 2, -1, -1):
            sel = jnp.where(gi == g, lg[g * PER_GROUP + j:g * PER_GROUP + j + 1], sel)
        le.append(sel)
    pe = _softmax_rows(le)
    p1, i1 = _first_max(pe)
    p2, i2 = _first_max([jnp.where(i1 == j, -1.0, pe[j]) for j in range(PER_GROUP)])
    den = p1 + p2
    w1 = pg_top * p1 / den
    w2 = pg_top * p2 / den

    lo = jnp.minimum(i1, i2)
    hi = jnp.maximum(i1, i2)
    cls = gi * PAIRS + jnp.where(lo == 0, 0, jnp.where(lo == 1, 3, 5)) + hi - lo - 1
    w_lo = jnp.where(i1 == lo, w1, w2)
    w_hi = jnp.where(i1 == lo, w2, w1)

    crow = lax.broadcasted_iota(I32, (CLASS_ROWS, TB), 0)
    hit = crow == cls
    onehot = jnp.where(hit, 1.0, 0.0)
    before = jnp.dot(onehot.astype(BF16), tri_ref[...], preferred_element_type=F32)
    before = before + base_ref[:, 0:1]
    rank = jnp.sum(jnp.where(hit, before, 0.0), axis=0, keepdims=True)
    base_ref[...] = base_ref[...] + jnp.sum(onehot, axis=1, keepdims=True)
    cnt_ref[...] = base_ref[...]

    mod_id = jnp.zeros_like(w1) + _mod_row(i, NT_CTX, T_LAT // TB).astype(F32)
    zero = jnp.zeros_like(w1)
    info_ref[...] = jnp.concatenate([cls.astype(F32), rank, zero, zero, zero, zero, zero, zero], axis=0)
    side = jnp.concatenate([w_lo, w_hi, mod_id, jnp.zeros((LANES - 3, TB), F32)], axis=0).T
    xext_ref[:, 0:D] = x
    xext_ref[:, D:XEXT] = side


def _route(x, mods, layer, g, wr, br, tri):
    const = lambda shape: pl.BlockSpec(shape, lambda i: (0,) * len(shape))
    return pl.pallas_call(
        _route_body,
        grid=(NT,),
        in_specs=[pl.BlockSpec((TB, D), lambda i: (i, 0)),
                  pl.BlockSpec((1, 1, 6 * D),
                               lambda i: (layer * SUBLANES + _mod_row(i, NT_CTX, T_LAT // TB), 0, 0)),
                  const((1, D)), const((2, ROUTE_ROWS, D)), const((ROUTE_ROWS, LANES)),
                  const((TB, TB))],
        out_specs=[pl.BlockSpec((TB, XEXT), lambda i: (i, 0)),
                   pl.BlockSpec((SUBLANES, TB), lambda i: (0, i)),
                   pl.BlockSpec((CLASS_ROWS, LANES), lambda i: (0, 0))],
        out_shape=[jax.ShapeDtypeStruct((N_TOK, XEXT), F32),
                   jax.ShapeDtypeStruct((SUBLANES, N_TOK), F32),
                   jax.ShapeDtypeStruct((CLASS_ROWS, LANES), F32)],
        scratch_shapes=[pltpu.VMEM((CLASS_ROWS, LANES), F32)],
        compiler_params=_cparams(1),
        name=f"route{layer}",
    )(x, mods, g, wr, br, tri)


def _plan(info, counts):
    cls = info[0].astype(I32)
    rank = info[1].astype(I32)
    cnt = counts[:N_CLASS, 0].astype(I32)
    tiles = (cnt + TM - 1) // TM
    tend = jnp.cumsum(tiles)
    tstart = tend - tiles
    n_used = tend[-1]
    cidx = jnp.arange(N_CLASS, dtype=I32)
    pos = jnp.sum(jnp.where(cls[:, None] == cidx, tstart * TM, 0), axis=-1) + rank
    n = jnp.arange(NT_FFN, dtype=I32)
    tile = jnp.minimum(n, jnp.maximum(n_used - 1, 0))
    tcls = jnp.minimum(jnp.sum((tile[:, None] >= tend[None, :]).astype(I32), axis=1), N_CLASS - 1)
    of_cls = tcls[:, None] == cidx
    rows = jnp.sum(jnp.where(of_cls, cnt, 0), axis=1) - (tile - jnp.sum(jnp.where(of_cls, tstart, 0), axis=1)) * TM
    rows = jnp.where(n < n_used, jnp.clip(rows, 0, TM), 0)
    chunks = (rows + CH - 1) // CH
    pair = tcls % PAIRS
    lo = (pair >= 3).astype(I32) + (pair >= 5).astype(I32)
    hi = jnp.where(pair < 3, pair + 1, jnp.where(pair < 5, pair - 1, 3))
    return pos, tcls // PAIRS, lo, hi, n_used.reshape(1), chunks


OCT = TM // SUBLANES
CH = 32


def _ffn_body(pos_ref, grp_ref, lo_ref, hi_ref, nu_ref, ch_ref,
              xext_hbm, mod_ref, g_ref, w1_ref, w3_ref, w2_ref,
              out_hbm, src_ref, dst_ref, xbuf, ybuf, wb1, wb3, wb2, gsem, ssem):
    n = pl.program_id(0)
    n_used = nu_ref[0]

    def gather_copy(tile, s, c, j):
        return pltpu.make_async_copy(
            xext_hbm.at[pl.ds(src_ref[tile * TM + c * CH + j], 1)],
            xbuf.at[s, c * (CH // SUBLANES) + j // SUBLANES, pl.ds(j % SUBLANES, 1)], gsem.at[s])

    def scatter_copy(tile, s, c, j):
        return pltpu.make_async_copy(
            ybuf.at[s, c * (CH // SUBLANES) + j // SUBLANES, pl.ds(j % SUBLANES, 1)],
            out_hbm.at[pl.ds(dst_ref[tile * TM + c * CH + j], 1)], ssem.at[s])

    def start_rows(copy, tile, s):
        def chunk(c, carry):
            for j in range(CH):
                copy(tile, s, c, j).start()
            return carry
        lax.fori_loop(0, ch_ref[tile], chunk, 0)

    def wait_rows(src, dst, sem, tile):
        def chunk(c, carry):
            pltpu.make_async_copy(src, dst, sem).wait()
            return carry
        lax.fori_loop(0, ch_ref[tile], chunk, 0)

    def wait_gather(tile, s):
        rows = xbuf.at[s, pl.ds(0, CH // SUBLANES)]
        wait_rows(rows, rows, gsem.at[s], tile)

    def wait_scatter(tile, s):
        rows = ybuf.at[s, pl.ds(0, CH // SUBLANES)]
        wait_rows(rows, rows, ssem.at[s], tile)

    @pl.when(n == 0)
    def _():
        def pad_rows(tile, carry):
            @pl.when(ch_ref[tile] > 0)
            def _():
                first = tile * TM + (ch_ref[tile] - 1) * CH
                for j in range(CH):
                    src_ref[first + j] = 0
                    dst_ref[first + j] = N_TOK + ((first + j) & (2 * TM - 1))
            return carry

        lax.fori_loop(0, NT_FFN, pad_rows, 0)

        def put(t, carry):
            p = pos_ref[t]
            src_ref[p] = t
            dst_ref[p] = t
            return carry

        lax.fori_loop(0, N_TOK, put, 0, unroll=8)

        xbuf[...] = jnp.zeros_like(xbuf)
        ybuf[...] = jnp.zeros_like(ybuf)
        for s in range(2):
            dumps = [pltpu.make_async_copy(
                ybuf.at[s, q], out_hbm.at[pl.ds(N_TOK + s * TM + q * SUBLANES, SUBLANES)], ssem.at[s])
                for q in range(OCT)]
            for dump in dumps:
                dump.start()
            for dump in dumps:
                dump.wait()
        start_rows(gather_copy, 0, 0)

    def step(slot):
        @pl.when(n + 1 < n_used)
        def _():
            start_rows(gather_copy, n + 1, 1 - slot)

        wait_gather(n, slot)

        @pl.when(n >= 2)
        def _():
            wait_scatter(n - 2, slot)

        prev = jnp.maximum(n - 1, 0)

        @pl.when((n == 0) | (grp_ref[n] != grp_ref[prev]))
        def _():
            for e in range(PER_GROUP):
                wb1[e] = w1_ref[0, e].astype(BF16)
                wb3[e] = w3_ref[0, e].astype(BF16)
                wb2[e] = w2_ref[0, e].astype(BF16)

        e_lo = lo_ref[n]
        e_hi = hi_ref[n]

        xe = xbuf[slot].reshape(TM, XEXT)
        x = xe[:, 0:D]
        w_lo = xe[:, D:D + 1]
        w_hi = xe[:, D + 1:D + 2]
        mod_id = xe[:, D + 2:D + 3]

        def pick(lo, hi):
            return jnp.where(mod_id < 0.5, mod_ref[0][:, lo:hi],
                             jnp.where(mod_id < 1.5, mod_ref[1][:, lo:hi], mod_ref[2][:, lo:hi]))

        h = _modulate(x, g_ref[...], pick(3 * D, 4 * D), pick(4 * D, 5 * D)).astype(BF16)

        def act(e, w):
            h1 = jnp.dot(h, wb1[e], preferred_element_type=F32)
            h3 = jnp.dot(h, wb3[e], preferred_element_type=F32)
            return ((h1 * jax.nn.sigmoid(h1)) * h3 * w).astype(BF16)

        y = (jnp.dot(act(e_lo, w_lo), wb2[e_lo], preferred_element_type=F32)
             + jnp.dot(act(e_hi, w_hi), wb2[e_hi], preferred_element_type=F32))
        ybuf[slot] = (x + pick(5 * D, 6 * D) * y).reshape(OCT, SUBLANES, D)
        start_rows(scatter_copy, n, slot)

        @pl.when(n == n_used - 1)
        def _():
            @pl.when(n >= 1)
            def _():
                wait_scatter(n - 1, 1 - slot)
            wait_scatter(n, slot)

    for s in range(2):
        @pl.when((n < n_used) & (n % 2 == s))
        def _():
            step(s)


def _ffn(pos, grp, lo, hi, n_used, chunks, xext, mods, layer, g, w1, w3, w2):
    gmap = lambda n, p, gr, lo, hi, nu, ch: (layer * N_GROUPS + gr[n], 0, 0, 0)
    up = pl.BlockSpec((1, PER_GROUP, D, D_EXPERT), gmap, pipeline_mode=pl.Buffered(1))
    down = pl.BlockSpec((1, PER_GROUP, D_EXPERT, D), gmap, pipeline_mode=pl.Buffered(1))
    return pl.pallas_call(
        _ffn_body,
        grid_spec=pltpu.PrefetchScalarGridSpec(
            num_scalar_prefetch=6, grid=(NT_FFN,),
            in_specs=[pl.BlockSpec(memory_space=pl.ANY),
                      pl.BlockSpec((SUBLANES, 1, 6 * D), lambda n, *_: (layer, 0, 0)),
                      pl.BlockSpec((1, D), lambda n, *_: (0, 0)),
                      up, up, down],
            out_specs=pl.BlockSpec(memory_space=pl.ANY),
            scratch_shapes=[pltpu.SMEM((P_FFN,), I32), pltpu.SMEM((P_FFN,), I32),
                            pltpu.VMEM((2, OCT, SUBLANES, XEXT), F32),
                            pltpu.VMEM((2, OCT, SUBLANES, D), F32),
                            pltpu.VMEM((PER_GROUP, D, D_EXPERT), BF16),
                            pltpu.VMEM((PER_GROUP, D, D_EXPERT), BF16),
                            pltpu.VMEM((PER_GROUP, D_EXPERT, D), BF16),
                            pltpu.SemaphoreType.DMA((2,)), pltpu.SemaphoreType.DMA((2,))]),
        out_shape=jax.ShapeDtypeStruct((N_TOK + 2 * TM, D), F32),
        compiler_params=_cparams(1, VMEM_LIMIT),
        name=f"ffn{layer}",
    )(pos, grp, lo, hi, n_used, chunks, xext, mods, g,
      w1.reshape(DEPTH * N_GROUPS, PER_GROUP, D, D_EXPERT),
      w3.reshape(DEPTH * N_GROUPS, PER_GROUP, D, D_EXPERT),
      w2.reshape(DEPTH * N_GROUPS, PER_GROUP, D_EXPERT, D))


def _moe(x, mods, layer, g, wr, br, tri, w1, w3, w2):
    xext, info, counts = _route(x, mods, layer, g, wr, br, tri)
    pos, grp, lo, hi, n_used, chunks = _plan(info, counts)
    return _ffn(pos, grp, lo, hi, n_used, chunks, xext, mods, layer, g, w1, w3, w2)


FG = 256


def _l1_in_body(x_ref, mod_ref, g_ref, w_ref, c_ref, s_ref, zc_ref, zs_ref):
    m = mod_ref[0]
    h = _modulate(x_ref[...], g_ref[...], m[:, 0:D], m[:, D:2 * D])
    z = jnp.dot(h.astype(BF16), w_ref[...], preferred_element_type=F32).astype(BF16)
    for g in range(D // FG):
        zg = z[:, g * FG:(g + 1) * FG]
        zc_ref[:, g * FG:(g + 1) * FG] = jnp.dot(zg, c_ref[...], preferred_element_type=F32).astype(BF16)
        zs_ref[:, g * FG:(g + 1) * FG] = jnp.dot(zg, s_ref[...], preferred_element_type=F32).astype(BF16)


def _l1_in(x, mods, g, w, c256, s256):
    const = lambda shape: pl.BlockSpec(shape, lambda i: (0,) * len(shape))
    return pl.pallas_call(
        _l1_in_body,
        grid=(NT,),
        in_specs=[pl.BlockSpec((TB, D), lambda i: (i, 0)),
                  pl.BlockSpec((1, 1, 6 * D),
                               lambda i: (SUBLANES + _mod_row(i, NT_CTX, T_LAT // TB), 0, 0)),
                  const((1, D)), const((D, D)), const((FG, FG)), const((FG, FG))],
        out_specs=[pl.BlockSpec((TB, D), lambda i: (i, 0)), pl.BlockSpec((TB, D), lambda i: (i, 0))],
        out_shape=[jax.ShapeDtypeStruct((N_TOK, D), BF16), jax.ShapeDtypeStruct((N_TOK, D), BF16)],
        compiler_params=_cparams(1),
        name="l1_in",
    )(x, mods, g, w, c256, s256)


def _l1_out_body(zc_t_ref, zs_t_ref, zc_q_ref, zs_q_ref, c256_ref, s256_ref, c1k_ref, s1k_ref,
                 x_ref, mod_ref, wo_ref, o_ref, f_ref):
    i = pl.program_id(0)

    @pl.when(i < NT_CTX)
    def _():
        for q in range(TB // T_CTX):
            rows = slice(q * T_CTX, (q + 1) * T_CTX)
            f = (jnp.dot(c256_ref[...], zc_t_ref[rows, :], preferred_element_type=F32)
                 - jnp.dot(s256_ref[...], zs_t_ref[rows, :], preferred_element_type=F32))
            f_ref[rows, :] = f.astype(BF16)

    @pl.when(i >= NT_CTX)
    def _():
        f = (jnp.dot(c1k_ref[...], zc_q_ref[...], preferred_element_type=F32)
             - jnp.dot(s1k_ref[...], zs_q_ref[...], preferred_element_type=F32))
        f_ref[...] = f.astype(BF16)

    out = jnp.dot(f_ref[...], wo_ref[...], preferred_element_type=F32)
    o_ref[...] = x_ref[...] + mod_ref[0][:, 2 * D:3 * D] * out


def _l1_out(zc, zs, c256, s256, c1k, s1k, x, mods, w_out):
    const = lambda shape: pl.BlockSpec(shape, lambda i: (0,) * len(shape))
    tile_map = lambda i: (jnp.minimum(i, NT_CTX - 1), 0)
    seq_map = lambda i: (N_CTX // T_LAT + jnp.maximum(i - NT_CTX, 0) // (T_LAT // TB), 0)
    row_map = lambda i: (jnp.maximum(i - NT_CTX, 0) % (T_LAT // TB), 0)
    return pl.pallas_call(
        _l1_out_body,
        grid=(NT,),
        in_specs=[pl.BlockSpec((TB, D), tile_map), pl.BlockSpec((TB, D), tile_map),
                  pl.BlockSpec((T_LAT, D), seq_map), pl.BlockSpec((T_LAT, D), seq_map),
                  const((T_CTX, T_CTX)), const((T_CTX, T_CTX)),
                  pl.BlockSpec((TB, T_LAT), row_map), pl.BlockSpec((TB, T_LAT), row_map),
                  pl.BlockSpec((TB, D), lambda i: (i, 0)),
                  pl.BlockSpec((1, 1, 6 * D),
                               lambda i: (SUBLANES + _mod_row(i, NT_CTX, T_LAT // TB), 0, 0)),
                  const((D, D))],
        out_specs=pl.BlockSpec((TB, D), lambda i: (i, 0)),
        out_shape=jax.ShapeDtypeStruct((N_TOK, D), F32),
        scratch_shapes=[pltpu.VMEM((TB, D), BF16)],
        compiler_params=_cparams(1),
        name="l1_out",
    )(zc, zs, zc, zs, c256, s256, c1k, s1k, x, mods, w_out)


def _split_hi_lo(w):
    hi = w.astype(BF16)
    lo = (w - hi.astype(F32)).astype(BF16)
    return jnp.stack([hi, lo])


def kernel(x_prompt, x_sample, cache_k, cache_v, c, c_ctx, ada_w, ada_b, norm_mix, norm_ffn, a_w_in, a_q_norm, a_k_norm, a_sink, pool_w, pool_scale, a_w_out, f_w_in, f_w_out, router_g_w, router_g_b, router_e_w, router_e_b, moe_w1, moe_w3, moe_w2):
    xp = x_prompt.reshape(N_CTX, D)
    xs = x_sample.reshape(N_LAT, D)

    cond8 = jnp.zeros((SUBLANES, D), F32).at[0].set(c_ctx).at[1:1 + N_LAT_B].set(c)
    mods = _adaln(cond8, ada_w, ada_b).reshape(DEPTH * SUBLANES, 1, 6 * D)

    tabs = _rope_tables()
    lane = np.arange(LANES)
    bd = jnp.asarray((lane[:, None] // HEAD_DIM) == (lane[None, :] // HEAD_DIM), BF16)
    tri = jnp.asarray(np.arange(TB)[:, None] < np.arange(TB)[None, :], BF16)
    c256, s256 = _dft_tables(T_CTX)
    c1k, s1k = _dft_tables(T_LAT)

    def router_operands(l):
        w = jnp.concatenate([router_e_w[l], router_g_w[l]], axis=1).T
        w = jnp.pad(w, ((0, ROUTE_ROWS - w.shape[0]), (0, 0)))
        b = jnp.concatenate([router_e_b[l], router_g_b[l]])
        b = jnp.pad(b, (0, ROUTE_ROWS - b.shape[0]))
        return _split_hi_lo(w), jnp.broadcast_to(b[:, None], (ROUTE_ROWS, LANES))

    qg = jnp.tile(a_q_norm[0], LANES // HEAD_DIM)[None, :]
    kg = jnp.tile(a_k_norm[0], LANES // HEAD_DIM)[None, :]
    q, k, v, u = _l0_in(xp, xs, mods, norm_mix[0][None, :], a_w_in[0].astype(BF16), qg, kg, bd, tabs)
    sink_b = jnp.broadcast_to(a_sink[0][:, None], (N_HEADS, LANES))
    o_ctx = _ctx_attn(q, k, v, sink_b)
    ck = cache_k[:, 0].reshape(N_LAT_B, PAST, KV_W)
    cv = cache_v[:, 0].reshape(N_LAT_B, PAST, KV_W)
    o_lat = _lat_attn(q, k, v, ck, cv, sink_b)
    x1 = _l0_out(o_ctx, o_lat, u, xp, xs, mods, pool_w[0].astype(BF16), pool_scale[0][None, :],
                 a_w_out[0].astype(BF16))
    wr, br = router_operands(0)
    x2 = _moe(x1, mods, 0, norm_ffn[0][None, :], wr, br, tri, moe_w1, moe_w3, moe_w2)

    zc, zs = _l1_in(x2, mods, norm_mix[1][None, :], f_w_in[0].astype(BF16), c256, s256)
    x3 = _l1_out(zc, zs, c256, s256, c1k, s1k, x2, mods, f_w_out[0].astype(BF16))
    wr, br = router_operands(1)
    x4 = _moe(x3, mods, 1, norm_ffn[1][None, :], wr, br, tri, moe_w1, moe_w3, moe_w2)

    new_k = k[:N_CTX].reshape(N_CTX_B, 1, T_CTX, KV_W // HEAD_DIM, HEAD_DIM)
    new_v = v[:N_CTX].reshape(N_CTX_B, 1, T_CTX, KV_W // HEAD_DIM, HEAD_DIM)
    return (x4[:N_CTX].reshape(N_CTX_B, T_CTX, D), x4[N_CTX:N_TOK].reshape(N_LAT_B, T_LAT, D),
            new_k, new_v)
```

```python
import functools

import numpy as np
import jax
import jax.numpy as jnp
from jax import lax
from jax.experimental import pallas as pl
from jax.experimental.pallas import tpu as pltpu

F32 = jnp.float32
BF16 = jnp.bfloat16
I32 = jnp.int32

D = 1024
DEPTH = 2
N_CTX_B, T_CTX = 16, 256
N_LAT_B, T_LAT = 2, 1024
N_CTX = N_CTX_B * T_CTX
N_LAT = N_LAT_B * T_LAT
N_TOK = N_CTX + N_LAT
PAST = 512
GRID_W = 64
HEAD_DIM = 64
N_HEADS = 8
ATTN_W = 512
KV_W = 128
POOL_W = 512
POOL_WINDOWS = (2, 4, 8, 16)
MIX_IN = ATTN_W + 2 * KV_W + POOL_W
WINDOW = 128
N_GROUPS = 4
PER_GROUP = 4
N_EXPERTS = 16
D_EXPERT = 512
ROPE_THETA = 10000.0
EPS = 1e-6
NEG = -1e30

LANES = 128
SUBLANES = 8
TB = 512
NT = N_TOK // TB
NT_CTX = N_CTX // TB
TB_MIX = 1024
TM = 256
PAIRS = 6
N_CLASS = N_GROUPS * PAIRS
CLASS_ROWS = 32
NT_FFN = N_TOK // TM + N_CLASS
P_FFN = NT_FFN * TM
XEXT = D + LANES
ROUTE_ROWS = 32

VMEM_LIMIT = 56 * 1024 * 1024


def _cparams(n_axes=1, vmem=None):
    return pltpu.CompilerParams(dimension_semantics=("arbitrary",) * n_axes,
                                vmem_limit_bytes=vmem)


def _modulate(x, g, shift, scale):
    ms = jnp.mean(x * x, axis=-1, keepdims=True)
    return (x * lax.rsqrt(ms + EPS) * g) * (1.0 + scale) + shift


def _mod_row(tile, tiles_ctx, tiles_per_lat):
    return (tile >= tiles_ctx).astype(I32) + (tile >= tiles_ctx + tiles_per_lat).astype(I32)


def _rope_tables():
    t = np.arange(T_LAT)
    row = (t // GRID_W).astype(np.float64)
    col = (t % GRID_W).astype(np.float64)
    nf = HEAD_DIM // 4
    freqs = ROPE_THETA ** (-np.arange(nf, dtype=np.float64) / nf)
    d = np.arange(HEAD_DIM)
    pos = np.where(d[None, :] < HEAD_DIM // 2, row[:, None], col[:, None])
    ang = pos * freqs[d % nf][None, :]
    first = (d % (HEAD_DIM // 2)) < nf
    cos = np.cos(ang)
    sin_a = np.where(first[None, :], -np.sin(ang), 0.0)
    sin_b = np.where(first[None, :], 0.0, np.sin(ang))
    ident = (np.ones((TB, HEAD_DIM)), np.zeros((TB, HEAD_DIM)), np.zeros((TB, HEAD_DIM)))
    out = []
    for tab, idt in zip((cos, sin_a, sin_b), ident):
        full = np.concatenate([tab, idt], axis=0)
        out.append(jnp.asarray(np.tile(full, (1, LANES // HEAD_DIM)), F32))
    return out


def _dft_tables(t):
    m = np.outer(np.arange(t), np.arange(t)) % t
    ang = 2.0 * np.pi * m / t
    s = 1.0 / np.sqrt(t)
    return jnp.asarray(np.cos(ang) * s, F32).astype(BF16), jnp.asarray(np.sin(ang) * s, F32).astype(BF16)


def _adaln_body(cond_ref, w_ref, b_ref, o_ref):
    c = cond_ref[...]
    s = (c * jax.nn.sigmoid(c)).astype(BF16)
    o_ref[0] = jnp.dot(s, w_ref[0].astype(BF16), preferred_element_type=F32) + b_ref[0]


def _adaln(cond8, ada_w, ada_b):
    tn = 1536
    return pl.pallas_call(
        _adaln_body,
        grid=(DEPTH, 6 * D // tn),
        in_specs=[pl.BlockSpec((SUBLANES, D), lambda l, j: (0, 0)),
                  pl.BlockSpec((1, D, tn), lambda l, j: (l, 0, j)),
                  pl.BlockSpec((1, 1, tn), lambda l, j: (l, 0, j))],
        out_specs=pl.BlockSpec((1, SUBLANES, tn), lambda l, j: (l, 0, j)),
        out_shape=jax.ShapeDtypeStruct((DEPTH, SUBLANES, 6 * D), F32),
        compiler_params=_cparams(2),
        name="adaln",
    )(cond8, ada_w, ada_b.reshape(DEPTH, 1, 6 * D))


def _l0_in_body(xp_ref, xs_ref, mod_ref, g_ref, w_ref, qg_ref, kg_ref, bd_ref,
                cos_ref, sa_ref, sb_ref, q_ref, k_ref, v_ref, u_ref):
    i = pl.program_id(0)
    x = jnp.where(i < NT_CTX, xp_ref[...], xs_ref[...])
    m = mod_ref[0]
    h = _modulate(x, g_ref[...], m[:, 0:D], m[:, D:2 * D])
    z = jnp.dot(h.astype(BF16), w_ref[...], preferred_element_type=F32)
    cos, sa, sb, bd = cos_ref[...], sa_ref[...], sb_ref[...], bd_ref[...]

    def head_norm_rope(zz, gain):
        ss = jnp.dot((zz * zz).astype(BF16), bd, preferred_element_type=F32)
        y = zz * lax.rsqrt(ss * (1.0 / HEAD_DIM) + EPS) * gain
        return (y * cos + pltpu.roll(y, LANES - 16, axis=1) * sa
                + pltpu.roll(y, 16, axis=1) * sb)

    for s in range(ATTN_W // LANES):
        qs = head_norm_rope(z[:, s * LANES:(s + 1) * LANES], qg_ref[...])
        q_ref[:, s * LANES:(s + 1) * LANES] = (qs * (HEAD_DIM ** -0.5)).astype(BF16)
    k_ref[...] = head_norm_rope(z[:, ATTN_W:ATTN_W + KV_W], kg_ref[...])
    v_ref[...] = z[:, ATTN_W + KV_W:ATTN_W + 2 * KV_W]
    u_ref[...] = z[:, ATTN_W + 2 * KV_W:MIX_IN]


def _l0_in(xp, xs, mods, g, w_in, qg, kg, bd, tabs):
    tab_spec = pl.BlockSpec(
        (TB, LANES), lambda i: (jnp.where(i < NT_CTX, T_LAT // TB, (i - NT_CTX) % (T_LAT // TB)), 0))
    const = lambda shape: pl.BlockSpec(shape, lambda i: (0,) * len(shape))
    return pl.pallas_call(
        _l0_in_body,
        grid=(NT,),
        in_specs=[pl.BlockSpec((TB, D), lambda i: (jnp.minimum(i, NT_CTX - 1), 0)),
                  pl.BlockSpec((TB, D), lambda i: (jnp.maximum(i - NT_CTX, 0), 0)),
                  pl.BlockSpec((1, 1, 6 * D), lambda i: (_mod_row(i, NT_CTX, T_LAT // TB), 0, 0)),
                  const((1, D)), const((D, MIX_IN)), const((1, LANES)), const((1, LANES)),
                  const((LANES, LANES)), tab_spec, tab_spec, tab_spec],
        out_specs=[pl.BlockSpec((TB, ATTN_W), lambda i: (i, 0)),
                   pl.BlockSpec((TB, KV_W), lambda i: (i, 0)),
                   pl.BlockSpec((TB, KV_W), lambda i: (i, 0)),
                   pl.BlockSpec((TB, POOL_W), lambda i: (i, 0))],
        out_shape=[jax.ShapeDtypeStruct((N_TOK, ATTN_W), BF16),
                   jax.ShapeDtypeStruct((N_TOK, KV_W), F32),
                   jax.ShapeDtypeStruct((N_TOK, KV_W), F32),
                   jax.ShapeDtypeStruct((N_TOK, POOL_W), F32)],
        compiler_params=_cparams(1),
        name="l0_in",
    )(xp, xs, mods, g, w_in, qg, kg, bd, *tabs)


def _head_halves(x):
    z = jnp.zeros_like(x)
    return jnp.concatenate([x, z], axis=1), jnp.concatenate([z, x], axis=1)


_NT_DIMS = (((1,), (1,)), ((), ()))


def _ones_halves(x):
    one = jnp.ones_like(x)
    return jnp.concatenate([x, one], axis=1), jnp.concatenate([one, x], axis=1)


def _sink_attend(scores, values, sk, half):
    mx = sk
    for sc in scores:
        mx = jnp.maximum(mx, jnp.max(sc, axis=-1, keepdims=True))
    acc = None
    for sc, val in zip(scores, values):
        part = jnp.dot(jnp.exp(sc - mx).astype(BF16), val, preferred_element_type=F32)
        acc = part if acc is None else acc + part
    ones_lane = HEAD_DIM * (1 - half)
    den = acc[:, ones_lane:ones_lane + 1] + jnp.exp(sk - mx)
    return acc * (1.0 / den)


def _sink_col(sink_ref, heads, rows):
    return jnp.concatenate([jnp.broadcast_to(sink_ref[h:h + 1, 0:1], (rows, 1)) for h in heads], axis=0)


def _ctx_attn_body(q_ref, k_ref, v_ref, sink_ref, o_ref):
    k = k_ref[...].astype(BF16)
    v = v_ref[...].astype(BF16)
    lo = lax.broadcasted_iota(I32, (T_CTX, LANES), 1) < HEAD_DIM
    for j in range(KV_W // HEAD_DIM):
        kj = k[:, j * HEAD_DIM:(j + 1) * HEAD_DIM]
        vj = v[:, j * HEAD_DIM:(j + 1) * HEAD_DIM]
        k_halves = _head_halves(kj)
        vd = jnp.concatenate([vj, vj], axis=1)
        q2 = jnp.concatenate([q_ref[:, (2 * j) * LANES:(2 * j + 1) * LANES],
                              q_ref[:, (2 * j + 1) * LANES:(2 * j + 2) * LANES]], axis=0)
        outs = []
        for half in range(2):
            sc = lax.dot_general(q2, k_halves[half], _NT_DIMS, preferred_element_type=F32)
            sk = _sink_col(sink_ref, (4 * j + half, 4 * j + 2 + half), T_CTX)
            mx = jnp.maximum(sk, jnp.max(sc, axis=-1, keepdims=True))
            p = jnp.exp(sc - mx)
            inv = 1.0 / (jnp.exp(sk - mx) + jnp.sum(p, axis=-1, keepdims=True))
            outs.append(jnp.dot((p * inv).astype(BF16), vd, preferred_element_type=F32))
        for s2 in range(2):
            rows = slice(s2 * T_CTX, (s2 + 1) * T_CTX)
            o_ref[:, (2 * j + s2) * LANES:(2 * j + s2 + 1) * LANES] = (
                jnp.where(lo, outs[0][rows], outs[1][rows]).astype(BF16))


def _ctx_attn(q, k, v, sink_b):
    return pl.pallas_call(
        _ctx_attn_body,
        grid=(N_CTX_B,),
        in_specs=[pl.BlockSpec((T_CTX, ATTN_W), lambda b: (b, 0)),
                  pl.BlockSpec((T_CTX, KV_W), lambda b: (b, 0)),
                  pl.BlockSpec((T_CTX, KV_W), lambda b: (b, 0)),
                  pl.BlockSpec((SUBLANES, LANES), lambda b: (0, 0))],
        out_specs=pl.BlockSpec((T_CTX, ATTN_W), lambda b: (b, 0)),
        out_shape=jax.ShapeDtypeStruct((N_CTX, ATTN_W), BF16),
        compiler_params=_cparams(1),
        name="ctx_attn",
    )(q, k, v, sink_b)


QB = 128
SPAN = QB + 2 * WINDOW


def _lat_attn_body(q_ref, k_ref, v_ref, ck_ref, cv_ref, sink_ref, o_ref):
    qb = pl.program_id(1)
    start = qb * QB
    kws, vws = [], []
    for c in (-1, 0, 1):
        cs = pl.multiple_of(jnp.clip(start + c * QB, 0, T_LAT - QB), QB)
        kws.append(k_ref[pl.ds(cs, QB), :])
        vws.append(v_ref[pl.ds(cs, QB), :])
    kw = jnp.concatenate(kws, axis=0).astype(BF16)
    vw = jnp.concatenate(vws, axis=0).astype(BF16)
    ck = ck_ref[0].astype(BF16)
    cv = cv_ref[0].astype(BF16)
    qpos = start + (lax.broadcasted_iota(I32, (2 * QB, SPAN), 0) & (QB - 1))
    kpos = start - WINDOW + lax.broadcasted_iota(I32, (2 * QB, SPAN), 1)
    valid = (kpos >= 0) & (kpos < T_LAT) & (jnp.abs(qpos - kpos) <= WINDOW)
    lo = lax.broadcasted_iota(I32, (QB, LANES), 1) < HEAD_DIM
    for j in range(KV_W // HEAD_DIM):
        sl = slice(j * HEAD_DIM, (j + 1) * HEAD_DIM)
        kw_halves = _head_halves(kw[:, sl])
        ck_halves = _head_halves(ck[:, sl])
        vw_halves = _ones_halves(vw[:, sl])
        cv_halves = _ones_halves(cv[:, sl])
        q2 = jnp.concatenate([q_ref[:, (2 * j) * LANES:(2 * j + 1) * LANES],
                              q_ref[:, (2 * j + 1) * LANES:(2 * j + 2) * LANES]], axis=0)
        outs = []
        for half in range(2):
            s_win = lax.dot_general(q2, kw_halves[half], _NT_DIMS, preferred_element_type=F32)
            s_win = jnp.where(valid, s_win, NEG)
            s_ctx = lax.dot_general(q2, ck_halves[half], _NT_DIMS, preferred_element_type=F32)
            sk = _sink_col(sink_ref, (4 * j + half, 4 * j + 2 + half), QB)
            outs.append(_sink_attend([s_win, s_ctx], [vw_halves[half], cv_halves[half]], sk, half))
        for s2 in range(2):
            rows = slice(s2 * QB, (s2 + 1) * QB)
            o_ref[:, (2 * j + s2) * LANES:(2 * j + s2 + 1) * LANES] = (
                jnp.where(lo, outs[0][rows], outs[1][rows]).astype(BF16))


def _lat_attn(q, k, v, ck, cv, sink_b):
    lat0 = N_CTX // T_LAT
    return pl.pallas_call(
        _lat_attn_body,
        grid=(N_LAT_B, T_LAT // QB),
        in_specs=[pl.BlockSpec((QB, ATTN_W), lambda b, i: (N_CTX // QB + b * (T_LAT // QB) + i, 0)),
                  pl.BlockSpec((T_LAT, KV_W), lambda b, i: (lat0 + b, 0)),
                  pl.BlockSpec((T_LAT, KV_W), lambda b, i: (lat0 + b, 0)),
                  pl.BlockSpec((1, PAST, KV_W), lambda b, i: (b, 0, 0)),
                  pl.BlockSpec((1, PAST, KV_W), lambda b, i: (b, 0, 0)),
                  pl.BlockSpec((SUBLANES, LANES), lambda b, i: (0, 0))],
        out_specs=pl.BlockSpec((QB, ATTN_W), lambda b, i: (b * (T_LAT // QB) + i, 0)),
        out_shape=jax.ShapeDtypeStruct((N_LAT, ATTN_W), BF16),
        compiler_params=_cparams(2),
        name="lat_attn",
    )(q, k, v, ck, cv, sink_b)


def _l0_out_body(oc_ref, ol_ref, u_ref, xp_ref, xs_ref, mod_ref, pw_ref, ps_ref, wo_ref, x1_ref):
    i = pl.program_id(0)
    is_ctx = i < N_CTX // TB_MIX
    o = jnp.where(is_ctx, oc_ref[...], ol_ref[...])
    x = jnp.where(is_ctx, xp_ref[...], xs_ref[...])
    tseq = jnp.where(is_ctx, T_CTX, T_LAT)
    pos = lax.broadcasted_iota(I32, (TB_MIX, LANES), 0) & (tseq - 1)
    ys = []
    for g, win in enumerate(POOL_WINDOWS):
        hw = win // 2
        ug = u_ref[:, g * LANES:(g + 1) * LANES]
        acc = ug
        for jj in range(-hw, hw):
            if jj == 0:
                continue
            sh = pltpu.roll(ug, (-jj) % TB_MIX, axis=0)
            ok = (pos + jj >= 0) if jj < 0 else (pos + jj < tseq)
            acc = acc + jnp.where(ok, sh, 0.0)
        cnt = (jnp.minimum(pos + hw, tseq) - jnp.maximum(pos - hw, 0)).astype(F32)
        pooled = acc / cnt - ug
        ys.append(jnp.dot(pooled.astype(BF16), pw_ref[g], preferred_element_type=F32))
    y = jnp.concatenate(ys, axis=1) * ps_ref[...]
    out = (jnp.dot(o, wo_ref[0:ATTN_W, :], preferred_element_type=F32)
           + jnp.dot(y.astype(BF16), wo_ref[ATTN_W:ATTN_W + POOL_W, :], preferred_element_type=F32))
    x1_ref[...] = x + mod_ref[0][:, 2 * D:3 * D] * out


def _l0_out(o_ctx, o_lat, u, xp, xs, mods, pool_w, pool_scale, w_out):
    ntc = N_CTX // TB_MIX
    const = lambda shape: pl.BlockSpec(shape, lambda i: (0,) * len(shape))
    ctx_map = lambda i: (jnp.minimum(i, ntc - 1), 0)
    lat_map = lambda i: (jnp.maximum(i - ntc, 0), 0)
    return pl.pallas_call(
        _l0_out_body,
        grid=(N_TOK // TB_MIX,),
        in_specs=[pl.BlockSpec((TB_MIX, ATTN_W), ctx_map),
                  pl.BlockSpec((TB_MIX, ATTN_W), lat_map),
                  pl.BlockSpec((TB_MIX, POOL_W), lambda i: (i, 0)),
                  pl.BlockSpec((TB_MIX, D), ctx_map),
                  pl.BlockSpec((TB_MIX, D), lat_map),
                  pl.BlockSpec((1, 1, 6 * D), lambda i: (_mod_row(i, ntc, 1), 0, 0)),
                  const((len(POOL_WINDOWS), LANES, LANES)), const((1, POOL_W)), const((D, D))],
        out_specs=pl.BlockSpec((TB_MIX, D), lambda i: (i, 0)),
        out_shape=jax.ShapeDtypeStruct((N_TOK, D), F32),
        compiler_params=_cparams(1, VMEM_LIMIT),
        name="l0_out",
    )(o_ctx, o_lat, u, xp, xs, mods, pool_w, pool_scale, w_out)


def _first_max(vals):
    best, idx = vals[0], jnp.zeros(vals[0].shape, I32)
    for r in range(1, len(vals)):
        better = vals[r] > best
        idx = jnp.where(better, r, idx)
        best = jnp.where(better, vals[r], best)
    return best, idx


def _softmax_rows(rows):
    mx = functools.reduce(jnp.maximum, rows)
    ex = [jnp.exp(r - mx) for r in rows]
    tot = functools.reduce(lambda a, b: a + b, ex)
    return [e / tot for e in ex]


def _route_body(x_ref, mod_ref, g_ref, wr_ref, br_ref, tri_ref,
                xext_ref, info_ref, cnt_ref, base_ref):
    i = pl.program_id(0)

    @pl.when(i == 0)
    def _():
        base_ref[...] = jnp.zeros_like(base_ref)

    x = x_ref[...]
    m = mod_ref[0]
    h = _modulate(x, g_ref[...], m[:, 3 * D:4 * D], m[:, 4 * D:5 * D])

    hh = h.astype(BF16)
    hl = (h - hh.astype(F32)).astype(BF16)
    wh, wl = wr_ref[0], wr_ref[1]
    lg = (lax.dot_general(wh, hh, _NT_DIMS, preferred_element_type=F32)
          + lax.dot_general(wl, hh, _NT_DIMS, preferred_element_type=F32)
          + lax.dot_general(wh, hl, _NT_DIMS, preferred_element_type=F32)) + br_ref[:, 0:1]

    pg = _softmax_rows([lg[N_EXPERTS + r:N_EXPERTS + r + 1] for r in range(N_GROUPS)])
    pg_top, gi = _first_max(pg)
    le = []
    for j in range(PER_GROUP):
        sel = lg[(N_GROUPS - 1) * PER_GROUP + j:(N_GROUPS - 1) * PER_GROUP + j + 1]
        for g in range(N_GROUPS - 2, -1, -1):
            sel = jnp.where(gi == g, lg[g * PER_GROUP + j:g * PER_GROUP + j + 1], sel)
        le.append(sel)
    pe = _softmax_rows(le)
    p1, i1 = _first_max(pe)
    p2, i2 = _first_max([jnp.where(i1 == j, -1.0, pe[j]) for j in range(PER_GROUP)])
    den = p1 + p2
    w1 = pg_top * p1 / den
    w2 = pg_top * p2 / den

    lo = jnp.minimum(i1, i2)
    hi = jnp.maximum(i1, i2)
    cls = gi * PAIRS + jnp.where(lo == 0, 0, jnp.where(lo == 1, 3, 5)) + hi - lo - 1
    w_lo = jnp.where(i1 == lo, w1, w2)
    w_hi = jnp.where(i1 == lo, w2, w1)

    crow = lax.broadcasted_iota(I32, (CLASS_ROWS, TB), 0)
    hit = crow == cls
    onehot = jnp.where(hit, 1.0, 0.0)
    before = jnp.dot(onehot.astype(BF16), tri_ref[...], preferred_element_type=F32)
    before = before + base_ref[:, 0:1]
    rank = jnp.sum(jnp.where(hit, before, 0.0), axis=0, keepdims=True)
    base_ref[...] = base_ref[...] + jnp.sum(onehot, axis=1, keepdims=True)
    cnt_ref[...] = base_ref[...]

    mod_id = jnp.zeros_like(w1) + _mod_row(i, NT_CTX, T_LAT // TB).astype(F32)
    zero = jnp.zeros_like(w1)
    info_ref[...] = jnp.concatenate([cls.astype(F32), rank, zero, zero, zero, zero, zero, zero], axis=0)
    side = jnp.concatenate([w_lo, w_hi, mod_id, jnp.zeros((LANES - 3, TB), F32)], axis=0).T
    xext_ref[:, 0:D] = x
    xext_ref[:, D:XEXT] = side


def _route(x, mods, layer, g, wr, br, tri):
    const = lambda shape: pl.BlockSpec(shape, lambda i: (0,) * len(shape))
    return pl.pallas_call(
        _route_body,
        grid=(NT,),
        in_specs=[pl.BlockSpec((TB, D), lambda i: (i, 0)),
                  pl.BlockSpec((1, 1, 6 * D),
                               lambda i: (layer * SUBLANES + _mod_row(i, NT_CTX, T_LAT // TB), 0, 0)),
                  const((1, D)), const((2, ROUTE_ROWS, D)), const((ROUTE_ROWS, LANES)),
                  const((TB, TB))],
        out_specs=[pl.BlockSpec((TB, XEXT), lambda i: (i, 0)),
                   pl.BlockSpec((SUBLANES, TB), lambda i: (0, i)),
                   pl.BlockSpec((CLASS_ROWS, LANES), lambda i: (0, 0))],
        out_shape=[jax.ShapeDtypeStruct((N_TOK, XEXT), F32),
                   jax.ShapeDtypeStruct((SUBLANES, N_TOK), F32),
                   jax.ShapeDtypeStruct((CLASS_ROWS, LANES), F32)],
        scratch_shapes=[pltpu.VMEM((CLASS_ROWS, LANES), F32)],
        compiler_params=_cparams(1),
        name=f"route{layer}",
    )(x, mods, g, wr, br, tri)


def _plan(info, counts):
    cls = info[0].astype(I32)
    rank = info[1].astype(I32)
    cnt = counts[:N_CLASS, 0].astype(I32)
    tiles = (cnt + TM - 1) // TM
    tend = jnp.cumsum(tiles)
    tstart = tend - tiles
    n_used = tend[-1]
    cidx = jnp.arange(N_CLASS, dtype=I32)
    pos = jnp.sum(jnp.where(cls[:, None] == cidx, tstart * TM, 0), axis=-1) + rank
    n = jnp.arange(NT_FFN, dtype=I32)
    tile = jnp.minimum(n, jnp.maximum(n_used - 1, 0))
    tcls = jnp.minimum(jnp.sum((tile[:, None] >= tend[None, :]).astype(I32), axis=1), N_CLASS - 1)
    of_cls = tcls[:, None] == cidx
    rows = jnp.sum(jnp.where(of_cls, cnt, 0), axis=1) - (tile - jnp.sum(jnp.where(of_cls, tstart, 0), axis=1)) * TM
    rows = jnp.where(n < n_used, jnp.clip(rows, 0, TM), 0)
    chunks = (rows + CH - 1) // CH
    pair = tcls % PAIRS
    lo = (pair >= 3).astype(I32) + (pair >= 5).astype(I32)
    hi = jnp.where(pair < 3, pair + 1, jnp.where(pair < 5, pair - 1, 3))
    e_lo = (tcls // PAIRS) * PER_GROUP + lo
    e_hi = (tcls // PAIRS) * PER_GROUP + hi
    eidx = jnp.arange(N_EXPERTS, dtype=I32)
    live = n < n_used
    uses = ((e_lo[:, None] == eidx) | (e_hi[:, None] == eidx)) & live[:, None]
    first = jnp.min(jnp.where(uses, n[:, None], NT_FFN), axis=0)
    new_lo = live & (jnp.sum(jnp.where(e_lo[:, None] == eidx, first, 0), axis=1) == n)
    new_hi = live & (jnp.sum(jnp.where(e_hi[:, None] == eidx, first, 0), axis=1) == n)

    def held(new, e):
        last = lax.cummax(jnp.where(new, n, 0))
        return jnp.sum(jnp.where(last[:, None] == n[None, :], e[None, :], 0), axis=1)

    slots = (held(new_lo, e_lo), held(new_hi, e_hi), new_lo.astype(I32), new_hi.astype(I32))
    return pos, lo, hi, slots, n_used.reshape(1), chunks


OCT = TM // SUBLANES
CH = 32


def _ffn_body(pos_ref, lo_ref, hi_ref, sa_ref, sb_ref, newa_ref, newb_ref, nu_ref, ch_ref,
              xext_hbm, mod_ref, g_ref, w1a_ref, w1b_ref, w3a_ref, w3b_ref, w2a_ref, w2b_ref,
              out_hbm, src_ref, dst_ref, xbuf, ybuf, wb1, wb3, wb2, gsem, ssem):
    n = pl.program_id(0)
    n_used = nu_ref[0]

    def gather_copy(tile, s, c, j):
        return pltpu.make_async_copy(
            xext_hbm.at[pl.ds(src_ref[tile * TM + c * CH + j], 1)],
            xbuf.at[s, c * (CH // SUBLANES) + j // SUBLANES, pl.ds(j % SUBLANES, 1)], gsem.at[s])

    def scatter_copy(tile, s, c, j):
        return pltpu.make_async_copy(
            ybuf.at[s, c * (CH // SUBLANES) + j // SUBLANES, pl.ds(j % SUBLANES, 1)],
            out_hbm.at[pl.ds(dst_ref[tile * TM + c * CH + j], 1)], ssem.at[s])

    def start_rows(copy, tile, s):
        def chunk(c, carry):
            for j in range(CH):
                copy(tile, s, c, j).start()
            return carry
        lax.fori_loop(0, ch_ref[tile], chunk, 0)

    def wait_rows(src, dst, sem, tile):
        def chunk(c, carry):
            pltpu.make_async_copy(src, dst, sem).wait()
            return carry
        lax.fori_loop(0, ch_ref[tile], chunk, 0)

    def wait_gather(tile, s):
        rows = xbuf.at[s, pl.ds(0, CH // SUBLANES)]
        wait_rows(rows, rows, gsem.at[s], tile)

    def wait_scatter(tile, s):
        rows = ybuf.at[s, pl.ds(0, CH // SUBLANES)]
        wait_rows(rows, rows, ssem.at[s], tile)

    @pl.when(n == 0)
    def _():
        def pad_rows(tile, carry):
            @pl.when(ch_ref[tile] > 0)
            def _():
                first = tile * TM + (ch_ref[tile] - 1) * CH
                for j in range(CH):
                    src_ref[first + j] = 0
                    dst_ref[first + j] = N_TOK + ((first + j) & (2 * TM - 1))
            return carry

        lax.fori_loop(0, NT_FFN, pad_rows, 0)

        def put(t, carry):
            p = pos_ref[t]
            src_ref[p] = t
            dst_ref[p] = t
            return carry

        lax.fori_loop(0, N_TOK, put, 0, unroll=8)

        xbuf[...] = jnp.zeros_like(xbuf)
        ybuf[...] = jnp.zeros_like(ybuf)
        for s in range(2):
            dumps = [pltpu.make_async_copy(
                ybuf.at[s, q], out_hbm.at[pl.ds(N_TOK + s * TM + q * SUBLANES, SUBLANES)], ssem.at[s])
                for q in range(OCT)]
            for dump in dumps:
                dump.start()
            for dump in dumps:
                dump.wait()
        start_rows(gather_copy, 0, 0)

    def step(slot):
        @pl.when(n + 1 < n_used)
        def _():
            start_rows(gather_copy, n + 1, 1 - slot)

        wait_gather(n, slot)

        @pl.when(n >= 2)
        def _():
            wait_scatter(n - 2, slot)

        prev = jnp.maximum(n - 1, 0)

        e_lo = lo_ref[n]
        e_hi = hi_ref[n]

        @pl.when(newa_ref[n] == 1)
        def _():
            wb1[e_lo] = w1a_ref[0, 0].astype(BF16)
            wb3[e_lo] = w3a_ref[0, 0].astype(BF16)
            wb2[e_lo] = w2a_ref[0, 0].astype(BF16)

        @pl.when(newb_ref[n] == 1)
        def _():
            wb1[e_hi] = w1b_ref[0, 0].astype(BF16)
            wb3[e_hi] = w3b_ref[0, 0].astype(BF16)
            wb2[e_hi] = w2b_ref[0, 0].astype(BF16)

        xe = xbuf[slot].reshape(TM, XEXT)
        x = xe[:, 0:D]
        w_lo = xe[:, D:D + 1]
        w_hi = xe[:, D + 1:D + 2]
        mod_id = xe[:, D + 2:D + 3]

        def pick(lo, hi):
            return jnp.where(mod_id < 0.5, mod_ref[0][:, lo:hi],
                             jnp.where(mod_id < 1.5, mod_ref[1][:, lo:hi], mod_ref[2][:, lo:hi]))

        h = _modulate(x, g_ref[...], pick(3 * D, 4 * D), pick(4 * D, 5 * D)).astype(BF16)

        def act(e, w):
            h1 = jnp.dot(h, wb1[e], preferred_element_type=F32)
            h3 = jnp.dot(h, wb3[e], preferred_element_type=F32)
            return ((h1 * jax.nn.sigmoid(h1)) * h3 * w).astype(BF16)

        y = (jnp.dot(act(e_lo, w_lo), wb2[e_lo], preferred_element_type=F32)
             + jnp.dot(act(e_hi, w_hi), wb2[e_hi], preferred_element_type=F32))
        ybuf[slot] = (x + pick(5 * D, 6 * D) * y).reshape(OCT, SUBLANES, D)
        start_rows(scatter_copy, n, slot)

        @pl.when(n == n_used - 1)
        def _():
            @pl.when(n >= 1)
            def _():
                wait_scatter(n - 1, 1 - slot)
            wait_scatter(n, slot)

    for s in range(2):
        @pl.when((n < n_used) & (n % 2 == s))
        def _():
            step(s)


def _ffn(pos, lo, hi, slots, n_used, chunks, xext, mods, layer, g, w1, w3, w2):
    a_map = lambda n, p, lo, hi, sa, sb, na, nb, nu, ch: (layer, sa[n], 0, 0)
    b_map = lambda n, p, lo, hi, sa, sb, na, nb, nu, ch: (layer, sb[n], 0, 0)
    up = lambda imap: pl.BlockSpec((1, 1, D, D_EXPERT), imap)
    down = lambda imap: pl.BlockSpec((1, 1, D_EXPERT, D), imap)
    return pl.pallas_call(
        _ffn_body,
        grid_spec=pltpu.PrefetchScalarGridSpec(
            num_scalar_prefetch=9, grid=(NT_FFN,),
            in_specs=[pl.BlockSpec(memory_space=pl.ANY),
                      pl.BlockSpec((SUBLANES, 1, 6 * D), lambda n, *_: (layer, 0, 0)),
                      pl.BlockSpec((1, D), lambda n, *_: (0, 0)),
                      up(a_map), up(b_map), up(a_map), up(b_map), down(a_map), down(b_map)],
            out_specs=pl.BlockSpec(memory_space=pl.ANY),
            scratch_shapes=[pltpu.SMEM((P_FFN,), I32), pltpu.SMEM((P_FFN,), I32),
                            pltpu.VMEM((2, OCT, SUBLANES, XEXT), F32),
                            pltpu.VMEM((2, OCT, SUBLANES, D), F32),
                            pltpu.VMEM((PER_GROUP, D, D_EXPERT), BF16),
                            pltpu.VMEM((PER_GROUP, D, D_EXPERT), BF16),
                            pltpu.VMEM((PER_GROUP, D_EXPERT, D), BF16),
                            pltpu.SemaphoreType.DMA((2,)), pltpu.SemaphoreType.DMA((2,))]),
        out_shape=jax.ShapeDtypeStruct((N_TOK + 2 * TM, D), F32),
        compiler_params=_cparams(1, VMEM_LIMIT),
        name=f"ffn{layer}",
    )(pos, lo, hi, *slots, n_used, chunks, xext, mods, g, w1, w1, w3, w3, w2, w2)


def _moe(x, mods, layer, g, wr, br, tri, w1, w3, w2):
    xext, info, counts = _route(x, mods, layer, g, wr, br, tri)
    pos, lo, hi, slots, n_used, chunks = _plan(info, counts)
    return _ffn(pos, lo, hi, slots, n_used, chunks, xext, mods, layer, g, w1, w3, w2)


FG = 256


def _l1_in_body(x_ref, mod_ref, g_ref, w_ref, c_ref, s_ref, zc_ref, zs_ref):
    m = mod_ref[0]
    h = _modulate(x_ref[...], g_ref[...], m[:, 0:D], m[:, D:2 * D])
    z = jnp.dot(h.astype(BF16), w_ref[...], preferred_element_type=F32).astype(BF16)
    for g in range(D // FG):
        zg = z[:, g * FG:(g + 1) * FG]
        zc_ref[:, g * FG:(g + 1) * FG] = jnp.dot(zg, c_ref[...], preferred_element_type=F32).astype(BF16)
        zs_ref[:, g * FG:(g + 1) * FG] = jnp.dot(zg, s_ref[...], preferred_element_type=F32).astype(BF16)


def _l1_in(x, mods, g, w, c256, s256):
    const = lambda shape: pl.BlockSpec(shape, lambda i: (0,) * len(shape))
    return pl.pallas_call(
        _l1_in_body,
        grid=(NT,),
        in_specs=[pl.BlockSpec((TB, D), lambda i: (i, 0)),
                  pl.BlockSpec((1, 1, 6 * D),
                               lambda i: (SUBLANES + _mod_row(i, NT_CTX, T_LAT // TB), 0, 0)),
                  const((1, D)), const((D, D)), const((FG, FG)), const((FG, FG))],
        out_specs=[pl.BlockSpec((TB, D), lambda i: (i, 0)), pl.BlockSpec((TB, D), lambda i: (i, 0))],
        out_shape=[jax.ShapeDtypeStruct((N_TOK, D), BF16), jax.ShapeDtypeStruct((N_TOK, D), BF16)],
        compiler_params=_cparams(1),
        name="l1_in",
    )(x, mods, g, w, c256, s256)


def _l1_out_body(zc_t_ref, zs_t_ref, zc_q_ref, zs_q_ref, c256_ref, s256_ref, c1k_ref, s1k_ref,
                 x_ref, mod_ref, wo_ref, o_ref, f_ref):
    i = pl.program_id(0)

    @pl.when(i < NT_CTX)
    def _():
        for q in range(TB // T_CTX):
            rows = slice(q * T_CTX, (q + 1) * T_CTX)
            f = (jnp.dot(c256_ref[...], zc_t_ref[rows, :], preferred_element_type=F32)
                 - jnp.dot(s256_ref[...], zs_t_ref[rows, :], preferred_element_type=F32))
            f_ref[rows, :] = f.astype(BF16)

    @pl.when(i >= NT_CTX)
    def _():
        f = (jnp.dot(c1k_ref[...], zc_q_ref[...], preferred_element_type=F32)
             - jnp.dot(s1k_ref[...], zs_q_ref[...], preferred_element_type=F32))
        f_ref[...] = f.astype(BF16)

    out = jnp.dot(f_ref[...], wo_ref[...], preferred_element_type=F32)
    o_ref[...] = x_ref[...] + mod_ref[0][:, 2 * D:3 * D] * out


def _l1_out(zc, zs, c256, s256, c1k, s1k, x, mods, w_out):
    const = lambda shape: pl.BlockSpec(shape, lambda i: (0,) * len(shape))
    tile_map = lambda i: (jnp.minimum(i, NT_CTX - 1), 0)
    seq_map = lambda i: (N_CTX // T_LAT + jnp.maximum(i - NT_CTX, 0) // (T_LAT // TB), 0)
    row_map = lambda i: (jnp.maximum(i - NT_CTX, 0) % (T_LAT // TB), 0)
    return pl.pallas_call(
        _l1_out_body,
        grid=(NT,),
        in_specs=[pl.BlockSpec((TB, D), tile_map), pl.BlockSpec((TB, D), tile_map),
                  pl.BlockSpec((T_LAT, D), seq_map), pl.BlockSpec((T_LAT, D), seq_map),
                  const((T_CTX, T_CTX)), const((T_CTX, T_CTX)),
                  pl.BlockSpec((TB, T_LAT), row_map), pl.BlockSpec((TB, T_LAT), row_map),
                  pl.BlockSpec((TB, D), lambda i: (i, 0)),
                  pl.BlockSpec((1, 1, 6 * D),
                               lambda i: (SUBLANES + _mod_row(i, NT_CTX, T_LAT // TB), 0, 0)),
                  const((D, D))],
        out_specs=pl.BlockSpec((TB, D), lambda i: (i, 0)),
        out_shape=jax.ShapeDtypeStruct((N_TOK, D), F32),
        scratch_shapes=[pltpu.VMEM((TB, D), BF16)],
        compiler_params=_cparams(1),
        name="l1_out",
    )(zc, zs, zc, zs, c256, s256, c1k, s1k, x, mods, w_out)


def _split_hi_lo(w):
    hi = w.astype(BF16)
    lo = (w - hi.astype(F32)).astype(BF16)
    return jnp.stack([hi, lo])


def kernel(x_prompt, x_sample, cache_k, cache_v, c, c_ctx, ada_w, ada_b, norm_mix, norm_ffn, a_w_in, a_q_norm, a_k_norm, a_sink, pool_w, pool_scale, a_w_out, f_w_in, f_w_out, router_g_w, router_g_b, router_e_w, router_e_b, moe_w1, moe_w3, moe_w2):
    xp = x_prompt.reshape(N_CTX, D)
    xs = x_sample.reshape(N_LAT, D)

    cond8 = jnp.zeros((SUBLANES, D), F32).at[0].set(c_ctx).at[1:1 + N_LAT_B].set(c)
    mods = _adaln(cond8, ada_w, ada_b).reshape(DEPTH * SUBLANES, 1, 6 * D)

    tabs = _rope_tables()
    lane = np.arange(LANES)
    bd = jnp.asarray((lane[:, None] // HEAD_DIM) == (lane[None, :] // HEAD_DIM), BF16)
    tri = jnp.asarray(np.arange(TB)[:, None] < np.arange(TB)[None, :], BF16)
    c256, s256 = _dft_tables(T_CTX)
    c1k, s1k = _dft_tables(T_LAT)

    def router_operands(l):
        w = jnp.concatenate([router_e_w[l], router_g_w[l]], axis=1).T
        w = jnp.pad(w, ((0, ROUTE_ROWS - w.shape[0]), (0, 0)))
        b = jnp.concatenate([router_e_b[l], router_g_b[l]])
        b = jnp.pad(b, (0, ROUTE_ROWS - b.shape[0]))
        return _split_hi_lo(w), jnp.broadcast_to(b[:, None], (ROUTE_ROWS, LANES))

    qg = jnp.tile(a_q_norm[0], LANES // HEAD_DIM)[None, :]
    kg = jnp.tile(a_k_norm[0], LANES // HEAD_DIM)[None, :]
    q, k, v, u = _l0_in(xp, xs, mods, norm_mix[0][None, :], a_w_in[0].astype(BF16), qg, kg, bd, tabs)
    sink_b = jnp.broadcast_to(a_sink[0][:, None], (N_HEADS, LANES))
    o_ctx = _ctx_attn(q, k, v, sink_b)
    ck = cache_k[:, 0].reshape(N_LAT_B, PAST, KV_W)
    cv = cache_v[:, 0].reshape(N_LAT_B, PAST, KV_W)
    o_lat = _lat_attn(q, k, v, ck, cv, sink_b)
    x1 = _l0_out(o_ctx, o_lat, u, xp, xs, mods, pool_w[0].astype(BF16), pool_scale[0][None, :],
                 a_w_out[0].astype(BF16))
    wr, br = router_operands(0)
    x2 = _moe(x1, mods, 0, norm_ffn[0][None, :], wr, br, tri, moe_w1, moe_w3, moe_w2)

    zc, zs = _l1_in(x2, mods, norm_mix[1][None, :], f_w_in[0].astype(BF16), c256, s256)
    x3 = _l1_out(zc, zs, c256, s256, c1k, s1k, x2, mods, f_w_out[0].astype(BF16))
    wr, br = router_operands(1)
    x4 = _moe(x3, mods, 1, norm_ffn[1][None, :], wr, br, tri, moe_w1, moe_w3, moe_w2)

    new_k = k[:N_CTX].reshape(N_CTX_B, 1, T_CTX, KV_W // HEAD_DIM, HEAD_DIM)
    new_v = v[:N_CTX].reshape(N_CTX_B, 1, T_CTX, KV_W // HEAD_DIM, HEAD_DIM)
    return (x4[:N_CTX].reshape(N_CTX_B, T_CTX, D), x4[N_CTX:N_TOK].reshape(N_LAT_B, T_LAT, D),
            new_k, new_v)
```

```python
import functools

import numpy as np
import jax
import jax.numpy as jnp
from jax import lax
from jax.experimental import pallas as pl
from jax.experimental.pallas import tpu as pltpu

F32 = jnp.float32
BF16 = jnp.bfloat16
I32 = jnp.int32

D = 1024
DEPTH = 2
N_CTX_B, T_CTX = 16, 256
N_LAT_B, T_LAT = 2, 1024
N_CTX = N_CTX_B * T_CTX
N_LAT = N_LAT_B * T_LAT
N_TOK = N_CTX + N_LAT
PAST = 512
GRID_W = 64
HEAD_DIM = 64
N_HEADS = 8
ATTN_W = 512
KV_W = 128
POOL_W = 512
POOL_WINDOWS = (2, 4, 8, 16)
MIX_IN = ATTN_W + 2 * KV_W + POOL_W
WINDOW = 128
N_GROUPS = 4
PER_GROUP = 4
N_EXPERTS = 16
D_EXPERT = 512
ROPE_THETA = 10000.0
EPS = 1e-6
NEG = -1e30

LANES = 128
SUBLANES = 8
TB = 512
NT = N_TOK // TB
NT_CTX = N_CTX // TB
TB_MIX = 1024
TM = 256
PAIRS = 6
N_CLASS = N_GROUPS * PAIRS
CLASS_ROWS = 32
NT_FFN = N_TOK // TM + N_CLASS
P_FFN = NT_FFN * TM
XEXT = D + LANES
ROUTE_ROWS = 32

VMEM_LIMIT = 56 * 1024 * 1024


def _cparams(n_axes=1, vmem=None):
    return pltpu.CompilerParams(dimension_semantics=("arbitrary",) * n_axes,
                                vmem_limit_bytes=vmem)


def _modulate(x, g, shift, scale):
    ms = jnp.mean(x * x, axis=-1, keepdims=True)
    return (x * lax.rsqrt(ms + EPS) * g) * (1.0 + scale) + shift


def _bf16_once(w_ref, wb_ref):
    @pl.when(pl.program_id(0) == 0)
    def _():
        wb_ref[...] = w_ref[...].astype(BF16)


def _mod_row(tile, tiles_ctx, tiles_per_lat):
    return (tile >= tiles_ctx).astype(I32) + (tile >= tiles_ctx + tiles_per_lat).astype(I32)


def _rope_tables():
    t = np.arange(T_LAT)
    row = (t // GRID_W).astype(np.float64)
    col = (t % GRID_W).astype(np.float64)
    nf = HEAD_DIM // 4
    freqs = ROPE_THETA ** (-np.arange(nf, dtype=np.float64) / nf)
    d = np.arange(HEAD_DIM)
    pos = np.where(d[None, :] < HEAD_DIM // 2, row[:, None], col[:, None])
    ang = pos * freqs[d % nf][None, :]
    first = (d % (HEAD_DIM // 2)) < nf
    cos = np.cos(ang)
    sin_a = np.where(first[None, :], -np.sin(ang), 0.0)
    sin_b = np.where(first[None, :], 0.0, np.sin(ang))
    ident = (np.ones((TB, HEAD_DIM)), np.zeros((TB, HEAD_DIM)), np.zeros((TB, HEAD_DIM)))
    out = []
    for tab, idt in zip((cos, sin_a, sin_b), ident):
        full = np.concatenate([tab, idt], axis=0)
        out.append(jnp.asarray(np.tile(full, (1, LANES // HEAD_DIM)), F32))
    return out


def _dft_tables(t):
    m = np.outer(np.arange(t), np.arange(t)) % t
    ang = 2.0 * np.pi * m / t
    s = 1.0 / np.sqrt(t)
    return jnp.asarray(np.cos(ang) * s, F32).astype(BF16), jnp.asarray(np.sin(ang) * s, F32).astype(BF16)


def _adaln_body(cond_ref, w_ref, b_ref, o_ref):
    c = cond_ref[...]
    s = (c * jax.nn.sigmoid(c)).astype(BF16)
    o_ref[0] = jnp.dot(s, w_ref[0].astype(BF16), preferred_element_type=F32) + b_ref[0]


def _adaln(cond8, ada_w, ada_b):
    tn = 1536
    return pl.pallas_call(
        _adaln_body,
        grid=(DEPTH, 6 * D // tn),
        in_specs=[pl.BlockSpec((SUBLANES, D), lambda l, j: (0, 0)),
                  pl.BlockSpec((1, D, tn), lambda l, j: (l, 0, j)),
                  pl.BlockSpec((1, 1, tn), lambda l, j: (l, 0, j))],
        out_specs=pl.BlockSpec((1, SUBLANES, tn), lambda l, j: (l, 0, j)),
        out_shape=jax.ShapeDtypeStruct((DEPTH, SUBLANES, 6 * D), F32),
        compiler_params=_cparams(2),
        name="adaln",
    )(cond8, ada_w, ada_b.reshape(DEPTH, 1, 6 * D))


def _l0_in_body(xp_ref, xs_ref, mod_ref, g_ref, w_ref, qg_ref, kg_ref, bd_ref,
                cos_ref, sa_ref, sb_ref, q_ref, k_ref, v_ref, u_ref, kc_ref, vc_ref, wb_ref):
    i = pl.program_id(0)
    _bf16_once(w_ref, wb_ref)
    x = jnp.where(i < NT_CTX, xp_ref[...], xs_ref[...])
    m = mod_ref[0]
    h = _modulate(x, g_ref[...], m[:, 0:D], m[:, D:2 * D])
    z = jnp.dot(h.astype(BF16), wb_ref[...], preferred_element_type=F32)
    cos, sa, sb, bd = cos_ref[...], sa_ref[...], sb_ref[...], bd_ref[...]

    def head_norm_rope(zz, gain):
        ss = jnp.dot((zz * zz).astype(BF16), bd, preferred_element_type=F32)
        y = zz * lax.rsqrt(ss * (1.0 / HEAD_DIM) + EPS) * gain
        return (y * cos + pltpu.roll(y, LANES - 16, axis=1) * sa
                + pltpu.roll(y, 16, axis=1) * sb)

    for s in range(ATTN_W // LANES):
        qs = head_norm_rope(z[:, s * LANES:(s + 1) * LANES], qg_ref[...])
        q_ref[:, s * LANES:(s + 1) * LANES] = (qs * (HEAD_DIM ** -0.5)).astype(BF16)
    k = head_norm_rope(z[:, ATTN_W:ATTN_W + KV_W], kg_ref[...])
    v = z[:, ATTN_W + KV_W:ATTN_W + 2 * KV_W]
    k_ref[...] = k
    v_ref[...] = v
    u_ref[...] = z[:, ATTN_W + 2 * KV_W:MIX_IN]

    @pl.when(i < NT_CTX)
    def _():
        kc_ref[...] = k
        vc_ref[...] = v


def _l0_in(xp, xs, mods, g, w_in, qg, kg, bd, tabs):
    tab_spec = pl.BlockSpec(
        (TB, LANES), lambda i: (jnp.where(i < NT_CTX, T_LAT // TB, (i - NT_CTX) % (T_LAT // TB)), 0))
    const = lambda shape: pl.BlockSpec(shape, lambda i: (0,) * len(shape))
    return pl.pallas_call(
        _l0_in_body,
        grid=(NT,),
        in_specs=[pl.BlockSpec((TB, D), lambda i: (jnp.minimum(i, NT_CTX - 1), 0)),
                  pl.BlockSpec((TB, D), lambda i: (jnp.maximum(i - NT_CTX, 0), 0)),
                  pl.BlockSpec((1, 1, 6 * D), lambda i: (_mod_row(i, NT_CTX, T_LAT // TB), 0, 0)),
                  const((1, D)), const((D, MIX_IN)), const((1, LANES)), const((1, LANES)),
                  const((LANES, LANES)), tab_spec, tab_spec, tab_spec],
        out_specs=[pl.BlockSpec((TB, ATTN_W), lambda i: (i, 0)),
                   pl.BlockSpec((TB, KV_W), lambda i: (i, 0)),
                   pl.BlockSpec((TB, KV_W), lambda i: (i, 0)),
                   pl.BlockSpec((TB, POOL_W), lambda i: (i, 0)),
                   pl.BlockSpec((TB, KV_W), lambda i: (jnp.minimum(i, NT_CTX - 1), 0)),
                   pl.BlockSpec((TB, KV_W), lambda i: (jnp.minimum(i, NT_CTX - 1), 0))],
        out_shape=[jax.ShapeDtypeStruct((N_TOK, ATTN_W), BF16),
                   jax.ShapeDtypeStruct((N_TOK, KV_W), F32),
                   jax.ShapeDtypeStruct((N_TOK, KV_W), F32),
                   jax.ShapeDtypeStruct((N_TOK, POOL_W), F32),
                   jax.ShapeDtypeStruct((N_CTX, KV_W), F32),
                   jax.ShapeDtypeStruct((N_CTX, KV_W), F32)],
        scratch_shapes=[pltpu.VMEM((D, MIX_IN), BF16)],
        compiler_params=_cparams(1),
        name="l0_in",
    )(xp, xs, mods, g, w_in, qg, kg, bd, *tabs)


def _head_halves(x):
    z = jnp.zeros_like(x)
    return jnp.concatenate([x, z], axis=1), jnp.concatenate([z, x], axis=1)


_NT_DIMS = (((1,), (1,)), ((), ()))


def _ones_halves(x):
    one = jnp.ones_like(x)
    return jnp.concatenate([x, one], axis=1), jnp.concatenate([one, x], axis=1)


def _sink_attend(scores, values, sk, half):
    mx = sk
    for sc in scores:
        mx = jnp.maximum(mx, jnp.max(sc, axis=-1, keepdims=True))
    acc = None
    for sc, val in zip(scores, values):
        part = jnp.dot(jnp.exp(sc - mx).astype(BF16), val, preferred_element_type=F32)
        acc = part if acc is None else acc + part
    ones_lane = HEAD_DIM * (1 - half)
    den = acc[:, ones_lane:ones_lane + 1] + jnp.exp(sk - mx)
    return acc * (1.0 / den)


def _sink_col(sink_ref, heads, rows):
    return jnp.concatenate([jnp.broadcast_to(sink_ref[h:h + 1, 0:1], (rows, 1)) for h in heads], axis=0)


def _ctx_attn_body(q_ref, k_ref, v_ref, sink_ref, o_ref):
    k = k_ref[...].astype(BF16)
    v = v_ref[...].astype(BF16)
    lo = lax.broadcasted_iota(I32, (T_CTX, LANES), 1) < HEAD_DIM
    for j in range(KV_W // HEAD_DIM):
        kj = k[:, j * HEAD_DIM:(j + 1) * HEAD_DIM]
        vj = v[:, j * HEAD_DIM:(j + 1) * HEAD_DIM]
        k_halves = _head_halves(kj)
        vd = jnp.concatenate([vj, vj], axis=1)
        q2 = jnp.concatenate([q_ref[:, (2 * j) * LANES:(2 * j + 1) * LANES],
                              q_ref[:, (2 * j + 1) * LANES:(2 * j + 2) * LANES]], axis=0)
        outs = []
        for half in range(2):
            sc = lax.dot_general(q2, k_halves[half], _NT_DIMS, preferred_element_type=F32)
            sk = _sink_col(sink_ref, (4 * j + half, 4 * j + 2 + half), T_CTX)
            mx = jnp.maximum(sk, jnp.max(sc, axis=-1, keepdims=True))
            p = jnp.exp(sc - mx)
            inv = 1.0 / (jnp.exp(sk - mx) + jnp.sum(p, axis=-1, keepdims=True))
            outs.append(jnp.dot((p * inv).astype(BF16), vd, preferred_element_type=F32))
        for s2 in range(2):
            rows = slice(s2 * T_CTX, (s2 + 1) * T_CTX)
            o_ref[:, (2 * j + s2) * LANES:(2 * j + s2 + 1) * LANES] = (
                jnp.where(lo, outs[0][rows], outs[1][rows]).astype(BF16))


def _ctx_attn(q, k, v, sink_b):
    return pl.pallas_call(
        _ctx_attn_body,
        grid=(N_CTX_B,),
        in_specs=[pl.BlockSpec((T_CTX, ATTN_W), lambda b: (b, 0)),
                  pl.BlockSpec((T_CTX, KV_W), lambda b: (b, 0)),
                  pl.BlockSpec((T_CTX, KV_W), lambda b: (b, 0)),
                  pl.BlockSpec((SUBLANES, LANES), lambda b: (0, 0))],
        out_specs=pl.BlockSpec((T_CTX, ATTN_W), lambda b: (b, 0)),
        out_shape=jax.ShapeDtypeStruct((N_CTX, ATTN_W), BF16),
        compiler_params=_cparams(1),
        name="ctx_attn",
    )(q, k, v, sink_b)


QB = 128
SPAN = QB + 2 * WINDOW


def _lat_attn_body(q_ref, k_ref, v_ref, ck_ref, cv_ref, sink_ref, o_ref):
    qb = pl.program_id(1)
    start = qb * QB
    kws, vws = [], []
    for c in (-1, 0, 1):
        cs = pl.multiple_of(jnp.clip(start + c * QB, 0, T_LAT - QB), QB)
        kws.append(k_ref[pl.ds(cs, QB), :])
        vws.append(v_ref[pl.ds(cs, QB), :])
    kw = jnp.concatenate(kws, axis=0).astype(BF16)
    vw = jnp.concatenate(vws, axis=0).astype(BF16)
    ck = ck_ref[0].astype(BF16)
    cv = cv_ref[0].astype(BF16)
    qpos = start + (lax.broadcasted_iota(I32, (2 * QB, SPAN), 0) & (QB - 1))
    kpos = start - WINDOW + lax.broadcasted_iota(I32, (2 * QB, SPAN), 1)
    valid = (kpos >= 0) & (kpos < T_LAT) & (jnp.abs(qpos - kpos) <= WINDOW)
    lo = lax.broadcasted_iota(I32, (QB, LANES), 1) < HEAD_DIM
    for j in range(KV_W // HEAD_DIM):
        sl = slice(j * HEAD_DIM, (j + 1) * HEAD_DIM)
        kw_halves = _head_halves(kw[:, sl])
        ck_halves = _head_halves(ck[:, sl])
        vw_halves = _ones_halves(vw[:, sl])
        cv_halves = _ones_halves(cv[:, sl])
        q2 = jnp.concatenate([q_ref[:, (2 * j) * LANES:(2 * j + 1) * LANES],
                              q_ref[:, (2 * j + 1) * LANES:(2 * j + 2) * LANES]], axis=0)
        outs = []
        for half in range(2):
            s_win = lax.dot_general(q2, kw_halves[half], _NT_DIMS, preferred_element_type=F32)
            s_win = jnp.where(valid, s_win, NEG)
            s_ctx = lax.dot_general(q2, ck_halves[half], _NT_DIMS, preferred_element_type=F32)
            sk = _sink_col(sink_ref, (4 * j + half, 4 * j + 2 + half), QB)
            outs.append(_sink_attend([s_win, s_ctx], [vw_halves[half], cv_halves[half]], sk, half))
        for s2 in range(2):
            rows = slice(s2 * QB, (s2 + 1) * QB)
            o_ref[:, (2 * j + s2) * LANES:(2 * j + s2 + 1) * LANES] = (
                jnp.where(lo, outs[0][rows], outs[1][rows]).astype(BF16))


def _lat_attn(q, k, v, ck, cv, sink_b):
    lat0 = N_CTX // T_LAT
    return pl.pallas_call(
        _lat_attn_body,
        grid=(N_LAT_B, T_LAT // QB),
        in_specs=[pl.BlockSpec((QB, ATTN_W), lambda b, i: (N_CTX // QB + b * (T_LAT // QB) + i, 0)),
                  pl.BlockSpec((T_LAT, KV_W), lambda b, i: (lat0 + b, 0)),
                  pl.BlockSpec((T_LAT, KV_W), lambda b, i: (lat0 + b, 0)),
                  pl.BlockSpec((1, PAST, KV_W), lambda b, i: (b, 0, 0)),
                  pl.BlockSpec((1, PAST, KV_W), lambda b, i: (b, 0, 0)),
                  pl.BlockSpec((SUBLANES, LANES), lambda b, i: (0, 0))],
        out_specs=pl.BlockSpec((QB, ATTN_W), lambda b, i: (b * (T_LAT // QB) + i, 0)),
        out_shape=jax.ShapeDtypeStruct((N_LAT, ATTN_W), BF16),
        compiler_params=_cparams(2),
        name="lat_attn",
    )(q, k, v, ck, cv, sink_b)


def _l0_out_body(oc_ref, ol_ref, u_ref, xp_ref, xs_ref, mod_ref, pw_ref, ps_ref, wo_ref, x1_ref, wb_ref):
    i = pl.program_id(0)
    _bf16_once(wo_ref, wb_ref)
    is_ctx = i < N_CTX // TB_MIX
    o = jnp.where(is_ctx, oc_ref[...], ol_ref[...])
    x = jnp.where(is_ctx, xp_ref[...], xs_ref[...])
    tseq = jnp.where(is_ctx, T_CTX, T_LAT)
    pos = lax.broadcasted_iota(I32, (TB_MIX, LANES), 0) & (tseq - 1)
    ys = []
    for g, win in enumerate(POOL_WINDOWS):
        hw = win // 2
        ug = u_ref[:, g * LANES:(g + 1) * LANES]
        acc = ug
        for jj in range(-hw, hw):
            if jj == 0:
                continue
            sh = pltpu.roll(ug, (-jj) % TB_MIX, axis=0)
            ok = (pos + jj >= 0) if jj < 0 else (pos + jj < tseq)
            acc = acc + jnp.where(ok, sh, 0.0)
        cnt = (jnp.minimum(pos + hw, tseq) - jnp.maximum(pos - hw, 0)).astype(F32)
        pooled = acc / cnt - ug
        ys.append(jnp.dot(pooled.astype(BF16), pw_ref[g].astype(BF16), preferred_element_type=F32))
    y = jnp.concatenate(ys, axis=1) * ps_ref[...]
    out = (jnp.dot(o, wb_ref[0:ATTN_W, :], preferred_element_type=F32)
           + jnp.dot(y.astype(BF16), wb_ref[ATTN_W:ATTN_W + POOL_W, :], preferred_element_type=F32))
    x1_ref[...] = x + mod_ref[0][:, 2 * D:3 * D] * out


def _l0_out(o_ctx, o_lat, u, xp, xs, mods, pool_w, pool_scale, w_out):
    ntc = N_CTX // TB_MIX
    const = lambda shape: pl.BlockSpec(shape, lambda i: (0,) * len(shape))
    ctx_map = lambda i: (jnp.minimum(i, ntc - 1), 0)
    lat_map = lambda i: (jnp.maximum(i - ntc, 0), 0)
    return pl.pallas_call(
        _l0_out_body,
        grid=(N_TOK // TB_MIX,),
        in_specs=[pl.BlockSpec((TB_MIX, ATTN_W), ctx_map),
                  pl.BlockSpec((TB_MIX, ATTN_W), lat_map),
                  pl.BlockSpec((TB_MIX, POOL_W), lambda i: (i, 0)),
                  pl.BlockSpec((TB_MIX, D), ctx_map),
                  pl.BlockSpec((TB_MIX, D), lat_map),
                  pl.BlockSpec((1, 1, 6 * D), lambda i: (_mod_row(i, ntc, 1), 0, 0)),
                  const((len(POOL_WINDOWS), LANES, LANES)), const((1, POOL_W)), const((D, D))],
        out_specs=pl.BlockSpec((TB_MIX, D), lambda i: (i, 0)),
        out_shape=jax.ShapeDtypeStruct((N_TOK, D), F32),
        scratch_shapes=[pltpu.VMEM((D, D), BF16)],
        compiler_params=_cparams(1, VMEM_LIMIT),
        name="l0_out",
    )(o_ctx, o_lat, u, xp, xs, mods, pool_w, pool_scale, w_out)


def _first_max(vals):
    best, idx = vals[0], jnp.zeros(vals[0].shape, I32)
    for r in range(1, len(vals)):
        better = vals[r] > best
        idx = jnp.where(better, r, idx)
        best = jnp.where(better, vals[r], best)
    return best, idx


def _softmax_rows(rows):
    mx = functools.reduce(jnp.maximum, rows)
    ex = [jnp.exp(r - mx) for r in rows]
    tot = functools.reduce(lambda a, b: a + b, ex)
    return [e / tot for e in ex]


def _route_body(x_ref, mod_ref, g_ref, wr_ref, br_ref, tri_ref,
                xext_ref, info_ref, cnt_ref, base_ref):
    i = pl.program_id(0)

    @pl.when(i == 0)
    def _():
        base_ref[...] = jnp.zeros_like(base_ref)

    x = x_ref[...]
    m = mod_ref[0]
    h = _modulate(x, g_ref[...], m[:, 3 * D:4 * D], m[:, 4 * D:5 * D])

    hh = h.astype(BF16)
    hl = (h - hh.astype(F32)).astype(BF16)
    wh, wl = wr_ref[0], wr_ref[1]
    lg = (lax.dot_general(wh, hh, _NT_DIMS, preferred_element_type=F32)
          + lax.dot_general(wl, hh, _NT_DIMS, preferred_element_type=F32)
          + lax.dot_general(wh, hl, _NT_DIMS, preferred_element_type=F32)) + br_ref[:, 0:1]

    pg = _softmax_rows([lg[N_EXPERTS + r:N_EXPERTS + r + 1] for r in range(N_GROUPS)])
    pg_top, gi = _first_max(pg)
    le = []
    for j in range(PER_GROUP):
        sel = lg[(N_GROUPS - 1) * PER_GROUP + j:(N_GROUPS - 1) * PER_GROUP + j + 1]
        for g in range(N_GROUPS - 2, -1, -1):
            sel = jnp.where(gi == g, lg[g * PER_GROUP + j:g * PER_GROUP + j + 1], sel)
        le.append(sel)
    pe = _softmax_rows(le)
    p1, i1 = _first_max(pe)
    p2, i2 = _first_max([jnp.where(i1 == j, -1.0, pe[j]) for j in range(PER_GROUP)])
    den = p1 + p2
    w1 = pg_top * p1 / den
    w2 = pg_top * p2 / den

    lo = jnp.minimum(i1, i2)
    hi = jnp.maximum(i1, i2)
    cls = gi * PAIRS + jnp.where(lo == 0, 0, jnp.where(lo == 1, 3, 5)) + hi - lo - 1
    w_lo = jnp.where(i1 == lo, w1, w2)
    w_hi = jnp.where(i1 == lo, w2, w1)

    crow = lax.broadcasted_iota(I32, (CLASS_ROWS, TB), 0)
    hit = crow == cls
    onehot = jnp.where(hit, 1.0, 0.0)
    before = jnp.dot(onehot.astype(BF16), tri_ref[...], preferred_element_type=F32)
    before = before + base_ref[:, 0:1]
    rank = jnp.sum(jnp.where(hit, before, 0.0), axis=0, keepdims=True)
    base_ref[...] = base_ref[...] + jnp.sum(onehot, axis=1, keepdims=True)
    cnt_ref[...] = base_ref[...]

    mod_id = jnp.zeros_like(w1) + _mod_row(i, NT_CTX, T_LAT // TB).astype(F32)
    zero = jnp.zeros_like(w1)
    info_ref[...] = jnp.concatenate([cls.astype(F32), rank, zero, zero, zero, zero, zero, zero], axis=0)
    side = jnp.concatenate([w_lo, w_hi, mod_id, jnp.zeros((LANES - 3, TB), F32)], axis=0).T
    xext_ref[:, 0:D] = x
    xext_ref[:, D:XEXT] = side


def _route(x, mods, layer, g, wr, br, tri):
    const = lambda shape: pl.BlockSpec(shape, lambda i: (0,) * len(shape))
    return pl.pallas_call(
        _route_body,
        grid=(NT,),
        in_specs=[pl.BlockSpec((TB, D), lambda i: (i, 0)),
                  pl.BlockSpec((1, 1, 6 * D),
                               lambda i: (layer * SUBLANES + _mod_row(i, NT_CTX, T_LAT // TB), 0, 0)),
                  const((1, D)), const((2, ROUTE_ROWS, D)), const((ROUTE_ROWS, LANES)),
                  const((TB, TB))],
        out_specs=[pl.BlockSpec((TB, XEXT), lambda i: (i, 0)),
                   pl.BlockSpec((SUBLANES, TB), lambda i: (0, i)),
                   pl.BlockSpec((CLASS_ROWS, LANES), lambda i: (0, 0))],
        out_shape=[jax.ShapeDtypeStruct((N_TOK, XEXT), F32),
                   jax.ShapeDtypeStruct((SUBLANES, N_TOK), F32),
                   jax.ShapeDtypeStruct((CLASS_ROWS, LANES), F32)],
        scratch_shapes=[pltpu.VMEM((CLASS_ROWS, LANES), F32)],
        compiler_params=_cparams(1),
        name=f"route{layer}",
    )(x, mods, g, wr, br, tri)


def _plan(info, counts):
    cls = info[0].astype(I32)
    rank = info[1].astype(I32)
    cnt = counts[:N_CLASS, 0].astype(I32)
    tiles = (cnt + TM - 1) // TM
    tend = jnp.cumsum(tiles)
    tstart = tend - tiles
    n_used = tend[-1]
    cidx = jnp.arange(N_CLASS, dtype=I32)
    pos = jnp.sum(jnp.where(cls[:, None] == cidx, tstart * TM, 0), axis=-1) + rank
    n = jnp.arange(NT_FFN, dtype=I32)
    tile = jnp.minimum(n, jnp.maximum(n_used - 1, 0))
    tcls = jnp.minimum(jnp.sum((tile[:, None] >= tend[None, :]).astype(I32), axis=1), N_CLASS - 1)
    of_cls = tcls[:, None] == cidx
    rows = jnp.sum(jnp.where(of_cls, cnt, 0), axis=1) - (tile - jnp.sum(jnp.where(of_cls, tstart, 0), axis=1)) * TM
    rows = jnp.where(n < n_used, jnp.clip(rows, 0, TM), 0)
    chunks = (rows + CH - 1) // CH
    pair = tcls % PAIRS
    lo = (pair >= 3).astype(I32) + (pair >= 5).astype(I32)
    hi = jnp.where(pair < 3, pair + 1, jnp.where(pair < 5, pair - 1, 3))
    e_lo = (tcls // PAIRS) * PER_GROUP + lo
    e_hi = (tcls // PAIRS) * PER_GROUP + hi
    eidx = jnp.arange(N_EXPERTS, dtype=I32)
    live = n < n_used
    uses = ((e_lo[:, None] == eidx) | (e_hi[:, None] == eidx)) & live[:, None]
    first = jnp.min(jnp.where(uses, n[:, None], NT_FFN), axis=0)
    new_lo = live & (jnp.sum(jnp.where(e_lo[:, None] == eidx, first, 0), axis=1) == n)
    new_hi = live & (jnp.sum(jnp.where(e_hi[:, None] == eidx, first, 0), axis=1) == n)

    def held(new, e):
        last = lax.cummax(jnp.where(new, n, 0))
        return jnp.sum(jnp.where(last[:, None] == n[None, :], e[None, :], 0), axis=1)

    slots = (held(new_lo, e_lo), held(new_hi, e_hi), new_lo.astype(I32), new_hi.astype(I32))
    return pos, lo, hi, slots, n_used.reshape(1), chunks


OCT = TM // SUBLANES
CH = 32


def _ffn_body(pos_ref, lo_ref, hi_ref, sa_ref, sb_ref, newa_ref, newb_ref, nu_ref, ch_ref,
              xext_hbm, mod_ref, g_ref, w1a_ref, w1b_ref, w3a_ref, w3b_ref, w2a_ref, w2b_ref,
              out_hbm, src_ref, dst_ref, xbuf, ybuf, wb1, wb3, wb2, gsem, ssem):
    n = pl.program_id(0)
    n_used = nu_ref[0]

    def gather_copy(tile, s, c, j):
        return pltpu.make_async_copy(
            xext_hbm.at[pl.ds(src_ref[tile * TM + c * CH + j], 1)],
            xbuf.at[s, c * (CH // SUBLANES) + j // SUBLANES, pl.ds(j % SUBLANES, 1)], gsem.at[s])

    def scatter_copy(tile, s, c, j):
        return pltpu.make_async_copy(
            ybuf.at[s, c * (CH // SUBLANES) + j // SUBLANES, pl.ds(j % SUBLANES, 1)],
            out_hbm.at[pl.ds(dst_ref[tile * TM + c * CH + j], 1)], ssem.at[s])

    def start_rows(copy, tile, s):
        def chunk(c, carry):
            for j in range(CH):
                copy(tile, s, c, j).start()
            return carry
        lax.fori_loop(0, ch_ref[tile], chunk, 0)

    def wait_rows(src, dst, sem, tile):
        def chunk(c, carry):
            pltpu.make_async_copy(src, dst, sem).wait()
            return carry
        lax.fori_loop(0, ch_ref[tile], chunk, 0)

    def wait_gather(tile, s):
        rows = xbuf.at[s, pl.ds(0, CH // SUBLANES)]
        wait_rows(rows, rows, gsem.at[s], tile)

    def wait_scatter(tile, s):
        rows = ybuf.at[s, pl.ds(0, CH // SUBLANES)]
        wait_rows(rows, rows, ssem.at[s], tile)

    @pl.when(n == 0)
    def _():
        def pad_rows(tile, carry):
            @pl.when(ch_ref[tile] > 0)
            def _():
                first = tile * TM + (ch_ref[tile] - 1) * CH
                for j in range(CH):
                    src_ref[first + j] = 0
                    dst_ref[first + j] = N_TOK + ((first + j) & (2 * TM - 1))
            return carry

        lax.fori_loop(0, NT_FFN, pad_rows, 0)

        def put(t, carry):
            p = pos_ref[t]
            src_ref[p] = t
            dst_ref[p] = t
            return carry

        lax.fori_loop(0, N_TOK, put, 0, unroll=8)

        xbuf[...] = jnp.zeros_like(xbuf)
        ybuf[...] = jnp.zeros_like(ybuf)
        for s in range(2):
            dumps = [pltpu.make_async_copy(
                ybuf.at[s, q], out_hbm.at[pl.ds(N_TOK + s * TM + q * SUBLANES, SUBLANES)], ssem.at[s])
                for q in range(OCT)]
            for dump in dumps:
                dump.start()
            for dump in dumps:
                dump.wait()
        start_rows(gather_copy, 0, 0)

    def step(slot):
        @pl.when(n + 1 < n_used)
        def _():
            start_rows(gather_copy, n + 1, 1 - slot)

        wait_gather(n, slot)

        @pl.when(n >= 2)
        def _():
            wait_scatter(n - 2, slot)

        prev = jnp.maximum(n - 1, 0)

        e_lo = lo_ref[n]
        e_hi = hi_ref[n]

        @pl.when(newa_ref[n] == 1)
        def _():
            wb1[e_lo] = w1a_ref[0, 0].astype(BF16)
            wb3[e_lo] = w3a_ref[0, 0].astype(BF16)
            wb2[e_lo] = w2a_ref[0, 0].astype(BF16)

        @pl.when(newb_ref[n] == 1)
        def _():
            wb1[e_hi] = w1b_ref[0, 0].astype(BF16)
            wb3[e_hi] = w3b_ref[0, 0].astype(BF16)
            wb2[e_hi] = w2b_ref[0, 0].astype(BF16)

        xe = xbuf[slot].reshape(TM, XEXT)
        x = xe[:, 0:D]
        w_lo = xe[:, D:D + 1]
        w_hi = xe[:, D + 1:D + 2]
        mod_id = xe[:, D + 2:D + 3]

        def pick(lo, hi):
            return jnp.where(mod_id < 0.5, mod_ref[0][:, lo:hi],
                             jnp.where(mod_id < 1.5, mod_ref[1][:, lo:hi], mod_ref[2][:, lo:hi]))

        h = _modulate(x, g_ref[...], pick(3 * D, 4 * D), pick(4 * D, 5 * D)).astype(BF16)

        def act(e, w):
            h1 = jnp.dot(h, wb1[e], preferred_element_type=F32)
            h3 = jnp.dot(h, wb3[e], preferred_element_type=F32)
            return ((h1 * jax.nn.sigmoid(h1)) * h3 * w).astype(BF16)

        y = (jnp.dot(act(e_lo, w_lo), wb2[e_lo], preferred_element_type=F32)
             + jnp.dot(act(e_hi, w_hi), wb2[e_hi], preferred_element_type=F32))
        ybuf[slot] = (x + pick(5 * D, 6 * D) * y).reshape(OCT, SUBLANES, D)
        start_rows(scatter_copy, n, slot)

        @pl.when(n == n_used - 1)
        def _():
            @pl.when(n >= 1)
            def _():
                wait_scatter(n - 1, 1 - slot)
            wait_scatter(n, slot)

    for s in range(2):
        @pl.when((n < n_used) & (n % 2 == s))
        def _():
            step(s)


def _ffn(pos, lo, hi, slots, n_used, chunks, xext, mods, layer, g, w1, w3, w2):
    a_map = lambda n, p, lo, hi, sa, sb, na, nb, nu, ch: (layer, sa[n], 0, 0)
    b_map = lambda n, p, lo, hi, sa, sb, na, nb, nu, ch: (layer, sb[n], 0, 0)
    up = lambda imap: pl.BlockSpec((1, 1, D, D_EXPERT), imap)
    down = lambda imap: pl.BlockSpec((1, 1, D_EXPERT, D), imap)
    return pl.pallas_call(
        _ffn_body,
        grid_spec=pltpu.PrefetchScalarGridSpec(
            num_scalar_prefetch=9, grid=(NT_FFN,),
            in_specs=[pl.BlockSpec(memory_space=pl.ANY),
                      pl.BlockSpec((SUBLANES, 1, 6 * D), lambda n, *_: (layer, 0, 0)),
                      pl.BlockSpec((1, D), lambda n, *_: (0, 0)),
                      up(a_map), up(b_map), up(a_map), up(b_map), down(a_map), down(b_map)],
            out_specs=pl.BlockSpec(memory_space=pl.ANY),
            scratch_shapes=[pltpu.SMEM((P_FFN,), I32), pltpu.SMEM((P_FFN,), I32),
                            pltpu.VMEM((2, OCT, SUBLANES, XEXT), F32),
                            pltpu.VMEM((2, OCT, SUBLANES, D), F32),
                            pltpu.VMEM((PER_GROUP, D, D_EXPERT), BF16),
                            pltpu.VMEM((PER_GROUP, D, D_EXPERT), BF16),
                            pltpu.VMEM((PER_GROUP, D_EXPERT, D), BF16),
                            pltpu.SemaphoreType.DMA((2,)), pltpu.SemaphoreType.DMA((2,))]),
        out_shape=jax.ShapeDtypeStruct((N_TOK + 2 * TM, D), F32),
        compiler_params=_cparams(1, VMEM_LIMIT),
        name=f"ffn{layer}",
    )(pos, lo, hi, *slots, n_used, chunks, xext, mods, g, w1, w1, w3, w3, w2, w2)


def _moe(x, mods, layer, g, wr, br, tri, w1, w3, w2):
    xext, info, counts = _route(x, mods, layer, g, wr, br, tri)
    pos, lo, hi, slots, n_used, chunks = _plan(info, counts)
    return _ffn(pos, lo, hi, slots, n_used, chunks, xext, mods, layer, g, w1, w3, w2)


FG = 256


def _l1_in_body(x_ref, mod_ref, g_ref, w_ref, c_ref, s_ref, zc_ref, zs_ref, wb_ref):
    _bf16_once(w_ref, wb_ref)
    m = mod_ref[0]
    h = _modulate(x_ref[...], g_ref[...], m[:, 0:D], m[:, D:2 * D])
    z = jnp.dot(h.astype(BF16), wb_ref[...], preferred_element_type=F32).astype(BF16)
    for g in range(D // FG):
        zg = z[:, g * FG:(g + 1) * FG]
        zc_ref[:, g * FG:(g + 1) * FG] = jnp.dot(zg, c_ref[...], preferred_element_type=F32).astype(BF16)
        zs_ref[:, g * FG:(g + 1) * FG] = jnp.dot(zg, s_ref[...], preferred_element_type=F32).astype(BF16)


def _l1_in(x, mods, g, w, c256, s256):
    const = lambda shape: pl.BlockSpec(shape, lambda i: (0,) * len(shape))
    return pl.pallas_call(
        _l1_in_body,
        grid=(NT,),
        in_specs=[pl.BlockSpec((TB, D), lambda i: (i, 0)),
                  pl.BlockSpec((1, 1, 6 * D),
                               lambda i: (SUBLANES + _mod_row(i, NT_CTX, T_LAT // TB), 0, 0)),
                  const((1, D)), const((D, D)), const((FG, FG)), const((FG, FG))],
        out_specs=[pl.BlockSpec((TB, D), lambda i: (i, 0)), pl.BlockSpec((TB, D), lambda i: (i, 0))],
        out_shape=[jax.ShapeDtypeStruct((N_TOK, D), BF16), jax.ShapeDtypeStruct((N_TOK, D), BF16)],
        scratch_shapes=[pltpu.VMEM((D, D), BF16)],
        compiler_params=_cparams(1),
        name="l1_in",
    )(x, mods, g, w, c256, s256)


def _l1_out_body(zc_t_ref, zs_t_ref, zc_q_ref, zs_q_ref, c256_ref, s256_ref, c1k_ref, s1k_ref,
                 x_ref, mod_ref, wo_ref, o_ref, f_ref, wb_ref):
    i = pl.program_id(0)
    _bf16_once(wo_ref, wb_ref)

    @pl.when(i < NT_CTX)
    def _():
        for q in range(TB // T_CTX):
            rows = slice(q * T_CTX, (q + 1) * T_CTX)
            f = (jnp.dot(c256_ref[...], zc_t_ref[rows, :], preferred_element_type=F32)
                 - jnp.dot(s256_ref[...], zs_t_ref[rows, :], preferred_element_type=F32))
            f_ref[rows, :] = f.astype(BF16)

    @pl.when(i >= NT_CTX)
    def _():
        f = (jnp.dot(c1k_ref[...], zc_q_ref[...], preferred_element_type=F32)
             - jnp.dot(s1k_ref[...], zs_q_ref[...], preferred_element_type=F32))
        f_ref[...] = f.astype(BF16)

    out = jnp.dot(f_ref[...], wb_ref[...], preferred_element_type=F32)
    o_ref[...] = x_ref[...] + mod_ref[0][:, 2 * D:3 * D] * out


def _l1_out(zc, zs, c256, s256, c1k, s1k, x, mods, w_out):
    const = lambda shape: pl.BlockSpec(shape, lambda i: (0,) * len(shape))
    tile_map = lambda i: (jnp.minimum(i, NT_CTX - 1), 0)
    seq_map = lambda i: (N_CTX // T_LAT + jnp.maximum(i - NT_CTX, 0) // (T_LAT // TB), 0)
    row_map = lambda i: (jnp.maximum(i - NT_CTX, 0) % (T_LAT // TB), 0)
    return pl.pallas_call(
        _l1_out_body,
        grid=(NT,),
        in_specs=[pl.BlockSpec((TB, D), tile_map), pl.BlockSpec((TB, D), tile_map),
                  pl.BlockSpec((T_LAT, D), seq_map), pl.BlockSpec((T_LAT, D), seq_map),
                  const((T_CTX, T_CTX)), const((T_CTX, T_CTX)),
                  pl.BlockSpec((TB, T_LAT), row_map), pl.BlockSpec((TB, T_LAT), row_map),
                  pl.BlockSpec((TB, D), lambda i: (i, 0)),
                  pl.BlockSpec((1, 1, 6 * D),
                               lambda i: (SUBLANES + _mod_row(i, NT_CTX, T_LAT // TB), 0, 0)),
                  const((D, D))],
        out_specs=pl.BlockSpec((TB, D), lambda i: (i, 0)),
        out_shape=jax.ShapeDtypeStruct((N_TOK, D), F32),
        scratch_shapes=[pltpu.VMEM((TB, D), BF16), pltpu.VMEM((D, D), BF16)],
        compiler_params=_cparams(1),
        name="l1_out",
    )(zc, zs, zc, zs, c256, s256, c1k, s1k, x, mods, w_out)


def _split_hi_lo(w):
    hi = w.astype(BF16)
    lo = (w - hi.astype(F32)).astype(BF16)
    return jnp.stack([hi, lo])


def kernel(x_prompt, x_sample, cache_k, cache_v, c, c_ctx, ada_w, ada_b, norm_mix, norm_ffn, a_w_in, a_q_norm, a_k_norm, a_sink, pool_w, pool_scale, a_w_out, f_w_in, f_w_out, router_g_w, router_g_b, router_e_w, router_e_b, moe_w1, moe_w3, moe_w2):
    xp = x_prompt.reshape(N_CTX, D)
    xs = x_sample.reshape(N_LAT, D)

    cond8 = jnp.zeros((SUBLANES, D), F32).at[0].set(c_ctx).at[1:1 + N_LAT_B].set(c)
    mods = _adaln(cond8, ada_w, ada_b).reshape(DEPTH * SUBLANES, 1, 6 * D)

    tabs = _rope_tables()
    lane = np.arange(LANES)
    bd = jnp.asarray((lane[:, None] // HEAD_DIM) == (lane[None, :] // HEAD_DIM), BF16)
    tri = jnp.asarray(np.arange(TB)[:, None] < np.arange(TB)[None, :], BF16)
    c256, s256 = _dft_tables(T_CTX)
    c1k, s1k = _dft_tables(T_LAT)

    def router_operands(l):
        w = jnp.concatenate([router_e_w[l], router_g_w[l]], axis=1).T
        w = jnp.pad(w, ((0, ROUTE_ROWS - w.shape[0]), (0, 0)))
        b = jnp.concatenate([router_e_b[l], router_g_b[l]])
        b = jnp.pad(b, (0, ROUTE_ROWS - b.shape[0]))
        return _split_hi_lo(w), jnp.broadcast_to(b[:, None], (ROUTE_ROWS, LANES))

    qg = jnp.tile(a_q_norm[0], LANES // HEAD_DIM)[None, :]
    kg = jnp.tile(a_k_norm[0], LANES // HEAD_DIM)[None, :]
    q, k, v, u, new_k, new_v = _l0_in(xp, xs, mods, norm_mix[0][None, :], a_w_in[0], qg, kg, bd, tabs)
    sink_b = jnp.broadcast_to(a_sink[0][:, None], (N_HEADS, LANES))
    o_ctx = _ctx_attn(q, k, v, sink_b)
    ck = cache_k[:, 0].reshape(N_LAT_B, PAST, KV_W)
    cv = cache_v[:, 0].reshape(N_LAT_B, PAST, KV_W)
    o_lat = _lat_attn(q, k, v, ck, cv, sink_b)
    x1 = _l0_out(o_ctx, o_lat, u, xp, xs, mods, pool_w[0], pool_scale[0][None, :], a_w_out[0])
    wr, br = router_operands(0)
    x2 = _moe(x1, mods, 0, norm_ffn[0][None, :], wr, br, tri, moe_w1, moe_w3, moe_w2)

    zc, zs = _l1_in(x2, mods, norm_mix[1][None, :], f_w_in[0], c256, s256)
    x3 = _l1_out(zc, zs, c256, s256, c1k, s1k, x2, mods, f_w_out[0])
    wr, br = router_operands(1)
    x4 = _moe(x3, mods, 1, norm_ffn[1][None, :], wr, br, tri, moe_w1, moe_w3, moe_w2)

    new_k = new_k.reshape(N_CTX_B, 1, T_CTX, KV_W // HEAD_DIM, HEAD_DIM)
    new_v = new_v.reshape(N_CTX_B, 1, T_CTX, KV_W // HEAD_DIM, HEAD_DIM)
    return (x4[:N_CTX].reshape(N_CTX_B, T_CTX, D), x4[N_CTX:N_TOK].reshape(N_LAT_B, T_LAT, D),
            new_k, new_v)
```

```python
import functools

import numpy as np
import jax
import jax.numpy as jnp
from jax import lax
from jax.experimental import pallas as pl
from jax.experimental.pallas import tpu as pltpu

F32 = jnp.float32
BF16 = jnp.bfloat16
I32 = jnp.int32

D = 1024
DEPTH = 2
N_CTX_B, T_CTX = 16, 256
N_LAT_B, T_LAT = 2, 1024
N_CTX = N_CTX_B * T_CTX
N_LAT = N_LAT_B * T_LAT
N_TOK = N_CTX + N_LAT
PAST = 512
GRID_W = 64
HEAD_DIM = 64
N_HEADS = 8
ATTN_W = 512
KV_W = 128
POOL_W = 512
POOL_WINDOWS = (2, 4, 8, 16)
MIX_IN = ATTN_W + 2 * KV_W + POOL_W
WINDOW = 128
N_GROUPS = 4
PER_GROUP = 4
N_EXPERTS = 16
D_EXPERT = 512
ROPE_THETA = 10000.0
EPS = 1e-6
NEG = -1e30

LANES = 128
SUBLANES = 8
TB = 512
NT = N_TOK // TB
NT_CTX = N_CTX // TB
TB_MIX = 1024
TM = 256
PAIRS = 6
N_CLASS = N_GROUPS * PAIRS
CLASS_ROWS = 32
NT_FFN = N_TOK // TM + N_CLASS
P_FFN = NT_FFN * TM
XEXT = D + LANES
ROUTE_ROWS = 32

VMEM_LIMIT = 56 * 1024 * 1024


def _cparams(n_axes=1, vmem=None):
    return pltpu.CompilerParams(dimension_semantics=("arbitrary",) * n_axes,
                                vmem_limit_bytes=vmem)


def _modulate(x, g, shift, scale):
    ms = jnp.mean(x * x, axis=-1, keepdims=True)
    return (x * lax.rsqrt(ms + EPS) * g) * (1.0 + scale) + shift


def _bf16_once(w_ref, wb_ref):
    @pl.when(pl.program_id(0) == 0)
    def _():
        wb_ref[...] = w_ref[...].astype(BF16)


def _mod_row(tile, tiles_ctx, tiles_per_lat):
    return (tile >= tiles_ctx).astype(I32) + (tile >= tiles_ctx + tiles_per_lat).astype(I32)


def _rope_tables():
    t = np.arange(T_LAT)
    row = (t // GRID_W).astype(np.float64)
    col = (t % GRID_W).astype(np.float64)
    nf = HEAD_DIM // 4
    freqs = ROPE_THETA ** (-np.arange(nf, dtype=np.float64) / nf)
    d = np.arange(HEAD_DIM)
    pos = np.where(d[None, :] < HEAD_DIM // 2, row[:, None], col[:, None])
    ang = pos * freqs[d % nf][None, :]
    first = (d % (HEAD_DIM // 2)) < nf
    cos = np.cos(ang)
    sin_a = np.where(first[None, :], -np.sin(ang), 0.0)
    sin_b = np.where(first[None, :], 0.0, np.sin(ang))
    ident = (np.ones((TB, HEAD_DIM)), np.zeros((TB, HEAD_DIM)), np.zeros((TB, HEAD_DIM)))
    out = []
    for tab, idt in zip((cos, sin_a, sin_b), ident):
        full = np.concatenate([tab, idt], axis=0)
        out.append(jnp.asarray(np.tile(full, (1, LANES // HEAD_DIM)), F32))
    return out


def _dft_tables(t):
    m = np.outer(np.arange(t), np.arange(t)) % t
    ang = 2.0 * np.pi * m / t
    s = 1.0 / np.sqrt(t)
    return jnp.asarray(np.cos(ang) * s, F32).astype(BF16), jnp.asarray(np.sin(ang) * s, F32).astype(BF16)


def _adaln_body(cond_ref, w_ref, b_ref, o_ref):
    c = cond_ref[...]
    s = (c * jax.nn.sigmoid(c)).astype(BF16)
    o_ref[0] = jnp.dot(s, w_ref[0].astype(BF16), preferred_element_type=F32) + b_ref[0]


def _adaln(cond8, ada_w, ada_b):
    tn = 1536
    return pl.pallas_call(
        _adaln_body,
        grid=(DEPTH, 6 * D // tn),
        in_specs=[pl.BlockSpec((SUBLANES, D), lambda l, j: (0, 0)),
                  pl.BlockSpec((1, D, tn), lambda l, j: (l, 0, j)),
                  pl.BlockSpec((1, 1, tn), lambda l, j: (l, 0, j))],
        out_specs=pl.BlockSpec((1, SUBLANES, tn), lambda l, j: (l, 0, j)),
        out_shape=jax.ShapeDtypeStruct((DEPTH, SUBLANES, 6 * D), F32),
        compiler_params=_cparams(2),
        name="adaln",
    )(cond8, ada_w, ada_b.reshape(DEPTH, 1, 6 * D))


def _l0_in_body(xp_ref, xs_ref, mod_ref, g_ref, w_ref, qg_ref, kg_ref, bd_ref,
                cos_ref, sa_ref, sb_ref, q_ref, k_ref, v_ref, u_ref, kc_ref, vc_ref, wb_ref):
    i = pl.program_id(0)
    _bf16_once(w_ref, wb_ref)
    x = jnp.where(i < NT_CTX, xp_ref[...], xs_ref[...])
    m = mod_ref[0]
    h = _modulate(x, g_ref[...], m[:, 0:D], m[:, D:2 * D])
    z = jnp.dot(h.astype(BF16), wb_ref[...], preferred_element_type=F32)
    cos, sa, sb, bd = cos_ref[...], sa_ref[...], sb_ref[...], bd_ref[...]

    def head_norm_rope(zz, gain):
        ss = jnp.dot((zz * zz).astype(BF16), bd, preferred_element_type=F32)
        y = zz * lax.rsqrt(ss * (1.0 / HEAD_DIM) + EPS) * gain
        return (y * cos + pltpu.roll(y, LANES - 16, axis=1) * sa
                + pltpu.roll(y, 16, axis=1) * sb)

    for s in range(ATTN_W // LANES):
        qs = head_norm_rope(z[:, s * LANES:(s + 1) * LANES], qg_ref[...])
        q_ref[:, s * LANES:(s + 1) * LANES] = (qs * (HEAD_DIM ** -0.5)).astype(BF16)
    k = head_norm_rope(z[:, ATTN_W:ATTN_W + KV_W], kg_ref[...])
    v = z[:, ATTN_W + KV_W:ATTN_W + 2 * KV_W]
    k_ref[...] = k
    v_ref[...] = v
    u_ref[...] = z[:, ATTN_W + 2 * KV_W:MIX_IN]

    @pl.when(i < NT_CTX)
    def _():
        for q in range(TB // T_CTX):
            kc_ref[q] = k[q * T_CTX:(q + 1) * T_CTX, :].T
            vc_ref[q] = v[q * T_CTX:(q + 1) * T_CTX, :].T


def _l0_in(xp, xs, mods, g, w_in, qg, kg, bd, tabs):
    tab_spec = pl.BlockSpec(
        (TB, LANES), lambda i: (jnp.where(i < NT_CTX, T_LAT // TB, (i - NT_CTX) % (T_LAT // TB)), 0))
    const = lambda shape: pl.BlockSpec(shape, lambda i: (0,) * len(shape))
    return pl.pallas_call(
        _l0_in_body,
        grid=(NT,),
        in_specs=[pl.BlockSpec((TB, D), lambda i: (jnp.minimum(i, NT_CTX - 1), 0)),
                  pl.BlockSpec((TB, D), lambda i: (jnp.maximum(i - NT_CTX, 0), 0)),
                  pl.BlockSpec((1, 1, 6 * D), lambda i: (_mod_row(i, NT_CTX, T_LAT // TB), 0, 0)),
                  const((1, D)), const((D, MIX_IN)), const((1, LANES)), const((1, LANES)),
                  const((LANES, LANES)), tab_spec, tab_spec, tab_spec],
        out_specs=[pl.BlockSpec((TB, ATTN_W), lambda i: (i, 0)),
                   pl.BlockSpec((TB, KV_W), lambda i: (i, 0)),
                   pl.BlockSpec((TB, KV_W), lambda i: (i, 0)),
                   pl.BlockSpec((TB, POOL_W), lambda i: (i, 0)),
                   pl.BlockSpec((TB // T_CTX, KV_W, T_CTX), lambda i: (jnp.minimum(i, NT_CTX - 1), 0, 0)),
                   pl.BlockSpec((TB // T_CTX, KV_W, T_CTX), lambda i: (jnp.minimum(i, NT_CTX - 1), 0, 0))],
        out_shape=[jax.ShapeDtypeStruct((N_TOK, ATTN_W), BF16),
                   jax.ShapeDtypeStruct((N_TOK, KV_W), F32),
                   jax.ShapeDtypeStruct((N_TOK, KV_W), F32),
                   jax.ShapeDtypeStruct((N_TOK, POOL_W), F32),
                   jax.ShapeDtypeStruct((N_CTX_B, KV_W, T_CTX), F32),
                   jax.ShapeDtypeStruct((N_CTX_B, KV_W, T_CTX), F32)],
        scratch_shapes=[pltpu.VMEM((D, MIX_IN), BF16)],
        compiler_params=_cparams(1),
        name="l0_in",
    )(xp, xs, mods, g, w_in, qg, kg, bd, *tabs)


def _head_halves(x):
    z = jnp.zeros_like(x)
    return jnp.concatenate([x, z], axis=1), jnp.concatenate([z, x], axis=1)


_NT_DIMS = (((1,), (1,)), ((), ()))


def _ones_halves(x):
    one = jnp.ones_like(x)
    return jnp.concatenate([x, one], axis=1), jnp.concatenate([one, x], axis=1)


def _sink_attend(scores, values, sk, half):
    mx = sk
    for sc in scores:
        mx = jnp.maximum(mx, jnp.max(sc, axis=-1, keepdims=True))
    acc = None
    for sc, val in zip(scores, values):
        part = jnp.dot(jnp.exp(sc - mx).astype(BF16), val, preferred_element_type=F32)
        acc = part if acc is None else acc + part
    ones_lane = HEAD_DIM * (1 - half)
    den = acc[:, ones_lane:ones_lane + 1] + jnp.exp(sk - mx)
    return acc * (1.0 / den)


def _sink_col(sink_ref, heads, rows):
    return jnp.concatenate([jnp.broadcast_to(sink_ref[h:h + 1, 0:1], (rows, 1)) for h in heads], axis=0)


def _ctx_attn_body(q_ref, k_ref, v_ref, sink_ref, o_ref):
    k = k_ref[...].astype(BF16)
    v = v_ref[...].astype(BF16)
    lo = lax.broadcasted_iota(I32, (T_CTX, LANES), 1) < HEAD_DIM
    for j in range(KV_W // HEAD_DIM):
        kj = k[:, j * HEAD_DIM:(j + 1) * HEAD_DIM]
        vj = v[:, j * HEAD_DIM:(j + 1) * HEAD_DIM]
        k_halves = _head_halves(kj)
        vd = jnp.concatenate([vj, vj], axis=1)
        q2 = jnp.concatenate([q_ref[:, (2 * j) * LANES:(2 * j + 1) * LANES],
                              q_ref[:, (2 * j + 1) * LANES:(2 * j + 2) * LANES]], axis=0)
        outs = []
        for half in range(2):
            sc = lax.dot_general(q2, k_halves[half], _NT_DIMS, preferred_element_type=F32)
            sk = _sink_col(sink_ref, (4 * j + half, 4 * j + 2 + half), T_CTX)
            mx = jnp.maximum(sk, jnp.max(sc, axis=-1, keepdims=True))
            p = jnp.exp(sc - mx)
            inv = 1.0 / (jnp.exp(sk - mx) + jnp.sum(p, axis=-1, keepdims=True))
            outs.append(jnp.dot((p * inv).astype(BF16), vd, preferred_element_type=F32))
        for s2 in range(2):
            rows = slice(s2 * T_CTX, (s2 + 1) * T_CTX)
            o_ref[:, (2 * j + s2) * LANES:(2 * j + s2 + 1) * LANES] = (
                jnp.where(lo, outs[0][rows], outs[1][rows]).astype(BF16))


def _ctx_attn(q, k, v, sink_b):
    return pl.pallas_call(
        _ctx_attn_body,
        grid=(N_CTX_B,),
        in_specs=[pl.BlockSpec((T_CTX, ATTN_W), lambda b: (b, 0)),
                  pl.BlockSpec((T_CTX, KV_W), lambda b: (b, 0)),
                  pl.BlockSpec((T_CTX, KV_W), lambda b: (b, 0)),
                  pl.BlockSpec((SUBLANES, LANES), lambda b: (0, 0))],
        out_specs=pl.BlockSpec((T_CTX, ATTN_W), lambda b: (b, 0)),
        out_shape=jax.ShapeDtypeStruct((N_CTX, ATTN_W), BF16),
        compiler_params=_cparams(1),
        name="ctx_attn",
    )(q, k, v, sink_b)


QB = 128
SPAN = QB + 2 * WINDOW


def _lat_attn_body(q_ref, k_ref, v_ref, ck_ref, cv_ref, sink_ref, o_ref):
    qb = pl.program_id(1)
    start = qb * QB
    kws, vws = [], []
    for c in (-1, 0, 1):
        cs = pl.multiple_of(jnp.clip(start + c * QB, 0, T_LAT - QB), QB)
        kws.append(k_ref[pl.ds(cs, QB), :])
        vws.append(v_ref[pl.ds(cs, QB), :])
    kw = jnp.concatenate(kws, axis=0).astype(BF16)
    vw = jnp.concatenate(vws, axis=0).astype(BF16)
    ck = ck_ref[0].astype(BF16)
    cv = cv_ref[0].astype(BF16)
    qpos = start + (lax.broadcasted_iota(I32, (2 * QB, SPAN), 0) & (QB - 1))
    kpos = start - WINDOW + lax.broadcasted_iota(I32, (2 * QB, SPAN), 1)
    valid = (kpos >= 0) & (kpos < T_LAT) & (jnp.abs(qpos - kpos) <= WINDOW)
    lo = lax.broadcasted_iota(I32, (QB, LANES), 1) < HEAD_DIM
    for j in range(KV_W // HEAD_DIM):
        sl = slice(j * HEAD_DIM, (j + 1) * HEAD_DIM)
        kw_halves = _head_halves(kw[:, sl])
        ck_halves = _head_halves(ck[:, sl])
        vw_halves = _ones_halves(vw[:, sl])
        cv_halves = _ones_halves(cv[:, sl])
        q2 = jnp.concatenate([q_ref[:, (2 * j) * LANES:(2 * j + 1) * LANES],
                              q_ref[:, (2 * j + 1) * LANES:(2 * j + 2) * LANES]], axis=0)
        outs = []
        for half in range(2):
            s_win = lax.dot_general(q2, kw_halves[half], _NT_DIMS, preferred_element_type=F32)
            s_win = jnp.where(valid, s_win, NEG)
            s_ctx = lax.dot_general(q2, ck_halves[half], _NT_DIMS, preferred_element_type=F32)
            sk = _sink_col(sink_ref, (4 * j + half, 4 * j + 2 + half), QB)
            outs.append(_sink_attend([s_win, s_ctx], [vw_halves[half], cv_halves[half]], sk, half))
        for s2 in range(2):
            rows = slice(s2 * QB, (s2 + 1) * QB)
            o_ref[:, (2 * j + s2) * LANES:(2 * j + s2 + 1) * LANES] = (
                jnp.where(lo, outs[0][rows], outs[1][rows]).astype(BF16))


def _lat_attn(q, k, v, ck, cv, sink_b):
    lat0 = N_CTX // T_LAT
    return pl.pallas_call(
        _lat_attn_body,
        grid=(N_LAT_B, T_LAT // QB),
        in_specs=[pl.BlockSpec((QB, ATTN_W), lambda b, i: (N_CTX // QB + b * (T_LAT // QB) + i, 0)),
                  pl.BlockSpec((T_LAT, KV_W), lambda b, i: (lat0 + b, 0)),
                  pl.BlockSpec((T_LAT, KV_W), lambda b, i: (lat0 + b, 0)),
                  pl.BlockSpec((1, PAST, KV_W), lambda b, i: (b, 0, 0)),
                  pl.BlockSpec((1, PAST, KV_W), lambda b, i: (b, 0, 0)),
                  pl.BlockSpec((SUBLANES, LANES), lambda b, i: (0, 0))],
        out_specs=pl.BlockSpec((QB, ATTN_W), lambda b, i: (b * (T_LAT // QB) + i, 0)),
        out_shape=jax.ShapeDtypeStruct((N_LAT, ATTN_W), BF16),
        compiler_params=_cparams(2),
        name="lat_attn",
    )(q, k, v, ck, cv, sink_b)


def _l0_out_body(oc_ref, ol_ref, u_ref, xp_ref, xs_ref, mod_ref, pw_ref, ps_ref, wo_ref, x1_ref, wb_ref):
    i = pl.program_id(0)
    _bf16_once(wo_ref, wb_ref)
    is_ctx = i < N_CTX // TB_MIX
    o = jnp.where(is_ctx, oc_ref[...], ol_ref[...])
    x = jnp.where(is_ctx, xp_ref[...], xs_ref[...])
    tseq = jnp.where(is_ctx, T_CTX, T_LAT)
    pos = lax.broadcasted_iota(I32, (TB_MIX, LANES), 0) & (tseq - 1)
    ys = []
    for g, win in enumerate(POOL_WINDOWS):
        hw = win // 2
        ug = u_ref[:, g * LANES:(g + 1) * LANES]
        acc = ug
        for jj in range(-hw, hw):
            if jj == 0:
                continue
            sh = pltpu.roll(ug, (-jj) % TB_MIX, axis=0)
            ok = (pos + jj >= 0) if jj < 0 else (pos + jj < tseq)
            acc = acc + jnp.where(ok, sh, 0.0)
        cnt = (jnp.minimum(pos + hw, tseq) - jnp.maximum(pos - hw, 0)).astype(F32)
        pooled = acc / cnt - ug
        ys.append(jnp.dot(pooled.astype(BF16), pw_ref[g].astype(BF16), preferred_element_type=F32))
    y = jnp.concatenate(ys, axis=1) * ps_ref[...]
    out = (jnp.dot(o, wb_ref[0:ATTN_W, :], preferred_element_type=F32)
           + jnp.dot(y.astype(BF16), wb_ref[ATTN_W:ATTN_W + POOL_W, :], preferred_element_type=F32))
    x1_ref[...] = x + mod_ref[0][:, 2 * D:3 * D] * out


def _l0_out(o_ctx, o_lat, u, xp, xs, mods, pool_w, pool_scale, w_out):
    ntc = N_CTX // TB_MIX
    const = lambda shape: pl.BlockSpec(shape, lambda i: (0,) * len(shape))
    ctx_map = lambda i: (jnp.minimum(i, ntc - 1), 0)
    lat_map = lambda i: (jnp.maximum(i - ntc, 0), 0)
    return pl.pallas_call(
        _l0_out_body,
        grid=(N_TOK // TB_MIX,),
        in_specs=[pl.BlockSpec((TB_MIX, ATTN_W), ctx_map),
                  pl.BlockSpec((TB_MIX, ATTN_W), lat_map),
                  pl.BlockSpec((TB_MIX, POOL_W), lambda i: (i, 0)),
                  pl.BlockSpec((TB_MIX, D), ctx_map),
                  pl.BlockSpec((TB_MIX, D), lat_map),
                  pl.BlockSpec((1, 1, 6 * D), lambda i: (_mod_row(i, ntc, 1), 0, 0)),
                  const((len(POOL_WINDOWS), LANES, LANES)), const((1, POOL_W)), const((D, D))],
        out_specs=pl.BlockSpec((TB_MIX, D), lambda i: (i, 0)),
        out_shape=jax.ShapeDtypeStruct((N_TOK, D), F32),
        scratch_shapes=[pltpu.VMEM((D, D), BF16)],
        compiler_params=_cparams(1, VMEM_LIMIT),
        name="l0_out",
    )(o_ctx, o_lat, u, xp, xs, mods, pool_w, pool_scale, w_out)


def _first_max(vals):
    best, idx = vals[0], jnp.zeros(vals[0].shape, I32)
    for r in range(1, len(vals)):
        better = vals[r] > best
        idx = jnp.where(better, r, idx)
        best = jnp.where(better, vals[r], best)
    return best, idx


def _softmax_rows(rows):
    mx = functools.reduce(jnp.maximum, rows)
    ex = [jnp.exp(r - mx) for r in rows]
    tot = functools.reduce(lambda a, b: a + b, ex)
    return [e / tot for e in ex]


def _route_body(x_ref, mod_ref, g_ref, wr_ref, br_ref, tri_ref,
                xext_ref, info_ref, cnt_ref, cctx_ref, base_ref):
    i = pl.program_id(0)

    @pl.when(i == 0)
    def _():
        base_ref[...] = jnp.zeros_like(base_ref)

    x = x_ref[...]
    m = mod_ref[0]
    h = _modulate(x, g_ref[...], m[:, 3 * D:4 * D], m[:, 4 * D:5 * D])

    hh = h.astype(BF16)
    hl = (h - hh.astype(F32)).astype(BF16)
    wh, wl = wr_ref[0], wr_ref[1]
    lg = (lax.dot_general(wh, hh, _NT_DIMS, preferred_element_type=F32)
          + lax.dot_general(wl, hh, _NT_DIMS, preferred_element_type=F32)
          + lax.dot_general(wh, hl, _NT_DIMS, preferred_element_type=F32)) + br_ref[:, 0:1]

    pg = _softmax_rows([lg[N_EXPERTS + r:N_EXPERTS + r + 1] for r in range(N_GROUPS)])
    pg_top, gi = _first_max(pg)
    le = []
    for j in range(PER_GROUP):
        sel = lg[(N_GROUPS - 1) * PER_GROUP + j:(N_GROUPS - 1) * PER_GROUP + j + 1]
        for g in range(N_GROUPS - 2, -1, -1):
            sel = jnp.where(gi == g, lg[g * PER_GROUP + j:g * PER_GROUP + j + 1], sel)
        le.append(sel)
    pe = _softmax_rows(le)
    p1, i1 = _first_max(pe)
    p2, i2 = _first_max([jnp.where(i1 == j, -1.0, pe[j]) for j in range(PER_GROUP)])
    den = p1 + p2
    w1 = pg_top * p1 / den
    w2 = pg_top * p2 / den

    lo = jnp.minimum(i1, i2)
    hi = jnp.maximum(i1, i2)
    cls = gi * PAIRS + jnp.where(lo == 0, 0, jnp.where(lo == 1, 3, 5)) + hi - lo - 1
    w_lo = jnp.where(i1 == lo, w1, w2)
    w_hi = jnp.where(i1 == lo, w2, w1)

    crow = lax.broadcasted_iota(I32, (CLASS_ROWS, TB), 0)
    hit = crow == cls
    onehot = jnp.where(hit, 1.0, 0.0)
    before = jnp.dot(onehot.astype(BF16), tri_ref[...], preferred_element_type=F32)
    before = before + base_ref[:, 0:1]
    rank = jnp.sum(jnp.where(hit, before, 0.0), axis=0, keepdims=True)
    base_ref[...] = base_ref[...] + jnp.sum(onehot, axis=1, keepdims=True)
    cnt_ref[...] = base_ref[...]

    @pl.when(i == NT_CTX - 1)
    def _():
        cctx_ref[...] = base_ref[...]

    mod_id =jnp.zeros_like(w1) + _mod_row(i, NT_CTX, T_LAT // TB).astype(F32)
    zero = jnp.zeros_like(w1)
    info_ref[...] = jnp.concatenate([cls.astype(F32), rank, zero, zero, zero, zero, zero, zero], axis=0)
    side = jnp.concatenate([w_lo, w_hi, mod_id, jnp.zeros((LANES - 3, TB), F32)], axis=0).T
    xext_ref[:, 0:D] = x
    xext_ref[:, D:XEXT] = side


def _route(x, mods, layer, g, wr, br, tri):
    const = lambda shape: pl.BlockSpec(shape, lambda i: (0,) * len(shape))
    return pl.pallas_call(
        _route_body,
        grid=(NT,),
        in_specs=[pl.BlockSpec((TB, D), lambda i: (i, 0)),
                  pl.BlockSpec((1, 1, 6 * D),
                               lambda i: (layer * SUBLANES + _mod_row(i, NT_CTX, T_LAT // TB), 0, 0)),
                  const((1, D)), const((2, ROUTE_ROWS, D)), const((ROUTE_ROWS, LANES)),
                  const((TB, TB))],
        out_specs=[pl.BlockSpec((TB, XEXT), lambda i: (i, 0)),
                   pl.BlockSpec((SUBLANES, TB), lambda i: (0, i)),
                   pl.BlockSpec((CLASS_ROWS, LANES), lambda i: (0, 0)),
                   pl.BlockSpec((CLASS_ROWS, LANES), lambda i: (0, 0))],
        out_shape=[jax.ShapeDtypeStruct((N_TOK, XEXT), F32),
                   jax.ShapeDtypeStruct((SUBLANES, N_TOK), F32),
                   jax.ShapeDtypeStruct((CLASS_ROWS, LANES), F32),
                   jax.ShapeDtypeStruct((CLASS_ROWS, LANES), F32)],
        scratch_shapes=[pltpu.VMEM((CLASS_ROWS, LANES), F32)],
        compiler_params=_cparams(1),
        name=f"route{layer}",
    )(x, mods, g, wr, br, tri)


def _plan(info, counts, counts_ctx):
    cls = info[0].astype(I32)
    rank = info[1].astype(I32)
    cnt = counts[:N_CLASS, 0].astype(I32)
    cnt_ctx = counts_ctx[:N_CLASS, 0].astype(I32)
    tiles = (cnt + TM - 1) // TM
    tend = jnp.cumsum(tiles)
    tstart = tend - tiles
    n_used = tend[-1]
    cidx = jnp.arange(N_CLASS, dtype=I32)
    pos = jnp.sum(jnp.where(cls[:, None] == cidx, tstart * TM, 0), axis=-1) + rank
    n = jnp.arange(NT_FFN, dtype=I32)
    tile = jnp.minimum(n, jnp.maximum(n_used - 1, 0))
    tcls = jnp.minimum(jnp.sum((tile[:, None] >= tend[None, :]).astype(I32), axis=1), N_CLASS - 1)
    of_cls = tcls[:, None] == cidx
    before = (tile - jnp.sum(jnp.where(of_cls, tstart, 0), axis=1)) * TM
    rows = jnp.where(n < n_used, jnp.clip(jnp.sum(jnp.where(of_cls, cnt, 0), axis=1) - before, 0, TM), 0)
    n_ctx = jnp.clip(jnp.sum(jnp.where(of_cls, cnt_ctx, 0), axis=1) - before, 0, rows)
    pair = tcls % PAIRS
    lo = (pair >= 3).astype(I32) + (pair >= 5).astype(I32)
    hi = jnp.where(pair < 3, pair + 1, jnp.where(pair < 5, pair - 1, 3))
    e_lo = (tcls // PAIRS) * PER_GROUP + lo
    e_hi = (tcls // PAIRS) * PER_GROUP + hi
    eidx = jnp.arange(N_EXPERTS, dtype=I32)
    live = n < n_used
    uses = ((e_lo[:, None] == eidx) | (e_hi[:, None] == eidx)) & live[:, None]
    first = jnp.min(jnp.where(uses, n[:, None], NT_FFN), axis=0)
    new_lo = live & (jnp.sum(jnp.where(e_lo[:, None] == eidx, first, 0), axis=1) == n)
    new_hi = live & (jnp.sum(jnp.where(e_hi[:, None] == eidx, first, 0), axis=1) == n)

    def held(new, e):
        last = lax.cummax(jnp.where(new, n, 0))
        return jnp.sum(jnp.where(last[:, None] == n[None, :], e[None, :], 0), axis=1)

    slots = (held(new_lo, e_lo), held(new_hi, e_hi), new_lo.astype(I32), new_hi.astype(I32))
    return pos, lo, hi, slots, n_used.reshape(1), rows, n_ctx


OCT = TM // SUBLANES
CH = 32


def _ffn_body(split, pos_ref, lo_ref, hi_ref, sa_ref, sb_ref, newa_ref, newb_ref, nu_ref, rows_ref, nctx_ref,
              xext_hbm, mod_ref, g_ref, w1a_ref, w1b_ref, w3a_ref, w3b_ref, w2a_ref, w2b_ref, *rest):
    outs = rest[:2] if split else rest[:1]
    src_ref, dst_ref, xbuf, ybuf, wb1, wb3, wb2, gsem, ssem = rest[len(outs):]
    n = pl.program_id(0)
    n_used = nu_ref[0]

    def n_chunks(tile):
        return lax.shift_right_logical(rows_ref[tile] + (CH - 1), CH.bit_length() - 1)

    def chunk_wait(buf, sem, s):
        rows = buf.at[s, pl.ds(0, CH // SUBLANES)]
        pltpu.make_async_copy(rows, rows, sem.at[s]).wait()

    def gather_copy(tile, s, c, j):
        return pltpu.make_async_copy(
            xext_hbm.at[pl.ds(src_ref[tile * TM + c * CH + j], 1)],
            xbuf.at[s, c * (CH // SUBLANES) + j // SUBLANES, pl.ds(j % SUBLANES, 1)], gsem.at[s])

    def start_gather(tile, s):
        def chunk(c, carry):
            for j in range(CH):
                gather_copy(tile, s, c, j).start()
            return carry
        lax.fori_loop(0, n_chunks(tile), chunk, 0)

    def wait_gather(tile, s):
        def chunk(c, carry):
            chunk_wait(xbuf, gsem, s)
            return carry
        lax.fori_loop(0, n_chunks(tile), chunk, 0)

    def scatter(tile, s, wait):
        rows = rows_ref[tile]
        n_ctx = nctx_ref[tile] if split else rows

        def row(out, c, j):
            copy = pltpu.make_async_copy(
                ybuf.at[s, c * (CH // SUBLANES) + j // SUBLANES, pl.ds(j % SUBLANES, 1)],
                out.at[pl.ds(dst_ref[tile * TM + c * CH + j], 1)], ssem.at[s])
            copy.wait() if wait else copy.start()

        def whole(out, c):
            if wait:
                chunk_wait(ybuf, ssem, s)
            else:
                for j in range(CH):
                    row(out, c, j)

        def chunk(c, carry):
            first = c * CH
            whole_ctx = first + CH <= n_ctx
            whole_lat = (first >= n_ctx) & (first + CH <= rows)

            @pl.when(whole_ctx)
            def _():
                whole(outs[0], c)

            @pl.when(whole_lat)
            def _():
                whole(outs[-1], c)

            @pl.when(jnp.logical_not(whole_ctx | whole_lat))
            def _():
                for j in range(CH):
                    @pl.when(first + j < n_ctx)
                    def _():
                        row(outs[0], c, j)

                    @pl.when((first + j >= n_ctx) & (first + j < rows))
                    def _():
                        row(outs[-1], c, j)
            return carry

        lax.fori_loop(0, n_chunks(tile), chunk, 0)

    @pl.when(n == 0)
    def _():
        def pad_rows(tile, carry):
            @pl.when(rows_ref[tile] > 0)
            def _():
                first = tile * TM + (n_chunks(tile) - 1) * CH
                for j in range(CH):
                    src_ref[first + j] = 0
            return carry

        lax.fori_loop(0, NT_FFN, pad_rows, 0)

        def put(t, carry):
            p = pos_ref[t]
            src_ref[p] = t
            dst_ref[p] = jnp.where(t < N_CTX, t, t - N_CTX) if split else t
            return carry

        lax.fori_loop(0, N_TOK, put, 0, unroll=8)

        xbuf[...] = jnp.zeros_like(xbuf)
        start_gather(0, 0)

    def step(slot):
        @pl.when(n + 1 < n_used)
        def _():
            start_gather(n + 1, 1 - slot)

        wait_gather(n, slot)

        @pl.when(n >= 2)
        def _():
            scatter(n - 2, slot, wait=True)

        e_lo = lo_ref[n]
        e_hi = hi_ref[n]

        @pl.when(newa_ref[n] == 1)
        def _():
            wb1[e_lo] = w1a_ref[0, 0].astype(BF16)
            wb3[e_lo] = w3a_ref[0, 0].astype(BF16)
            wb2[e_lo] = w2a_ref[0, 0].astype(BF16)

        @pl.when(newb_ref[n] == 1)
        def _():
            wb1[e_hi] = w1b_ref[0, 0].astype(BF16)
            wb3[e_hi] = w3b_ref[0, 0].astype(BF16)
            wb2[e_hi] = w2b_ref[0, 0].astype(BF16)

        xe = xbuf[slot].reshape(TM, XEXT)
        x = xe[:, 0:D]
        w_lo = xe[:, D:D + 1]
        w_hi = xe[:, D + 1:D + 2]
        mod_id = xe[:, D + 2:D + 3]

        def pick(lo, hi):
            return jnp.where(mod_id < 0.5, mod_ref[0][:, lo:hi],
                             jnp.where(mod_id < 1.5, mod_ref[1][:, lo:hi], mod_ref[2][:, lo:hi]))

        h = _modulate(x, g_ref[...], pick(3 * D, 4 * D), pick(4 * D, 5 * D)).astype(BF16)

        def act(e, w):
            h1 = jnp.dot(h, wb1[e], preferred_element_type=F32)
            h3 = jnp.dot(h, wb3[e], preferred_element_type=F32)
            return ((h1 * jax.nn.sigmoid(h1)) * h3 * w).astype(BF16)

        y = (jnp.dot(act(e_lo, w_lo), wb2[e_lo], preferred_element_type=F32)
             + jnp.dot(act(e_hi, w_hi), wb2[e_hi], preferred_element_type=F32))
        ybuf[slot] = (x + pick(5 * D, 6 * D) * y).reshape(OCT, SUBLANES, D)
        scatter(n, slot, wait=False)

        @pl.when(n == n_used - 1)
        def _():
            @pl.when(n >= 1)
            def _():
                scatter(n - 1, 1 - slot, wait=True)
            scatter(n, slot, wait=True)

    for s in range(2):
        @pl.when((n < n_used) & (n % 2 == s))
        def _():
            step(s)


def _ffn(pos, lo, hi, slots, n_used, rows, n_ctx, xext, mods, layer, g, w1, w3, w2, split):
    a_map = lambda n, p, lo, hi, sa, *_: (layer, sa[n], 0, 0)
    b_map = lambda n, p, lo, hi, sa, sb, *_: (layer, sb[n], 0, 0)
    up = lambda imap: pl.BlockSpec((1, 1, D, D_EXPERT), imap)
    down = lambda imap: pl.BlockSpec((1, 1, D_EXPERT, D), imap)
    out_rows = (N_CTX, N_LAT) if split else (N_TOK,)
    return pl.pallas_call(
        functools.partial(_ffn_body, split),
        grid_spec=pltpu.PrefetchScalarGridSpec(
            num_scalar_prefetch=10, grid=(NT_FFN,),
            in_specs=[pl.BlockSpec(memory_space=pl.ANY),
                      pl.BlockSpec((SUBLANES, 1, 6 * D), lambda n, *_: (layer, 0, 0)),
                      pl.BlockSpec((1, D), lambda n, *_: (0, 0)),
                      up(a_map), up(b_map), up(a_map), up(b_map), down(a_map), down(b_map)],
            out_specs=[pl.BlockSpec(memory_space=pl.ANY) for _ in out_rows],
            scratch_shapes=[pltpu.SMEM((P_FFN,), I32), pltpu.SMEM((P_FFN,), I32),
                            pltpu.VMEM((2, OCT, SUBLANES, XEXT), F32),
                            pltpu.VMEM((2, OCT, SUBLANES, D), F32),
                            pltpu.VMEM((PER_GROUP, D, D_EXPERT), BF16),
                            pltpu.VMEM((PER_GROUP, D, D_EXPERT), BF16),
                            pltpu.VMEM((PER_GROUP, D_EXPERT, D), BF16),
                            pltpu.SemaphoreType.DMA((2,)), pltpu.SemaphoreType.DMA((2,))]),
        out_shape=[jax.ShapeDtypeStruct((r, D), F32) for r in out_rows],
        compiler_params=_cparams(1, VMEM_LIMIT),
        name=f"ffn{layer}",
    )(pos, lo, hi, *slots, n_used, rows, n_ctx, xext, mods, g, w1, w1, w3, w3, w2, w2)


def _moe(x, mods, layer, g, wr, br, tri, w1, w3, w2, split):
    xext, info, counts, counts_ctx = _route(x, mods, layer, g, wr, br, tri)
    pos, lo, hi, slots, n_used, rows, n_ctx = _plan(info, counts, counts_ctx)
    return _ffn(pos, lo, hi, slots, n_used, rows, n_ctx, xext, mods, layer, g, w1, w3, w2, split)


FG = 256


def _l1_in_body(x_ref, mod_ref, g_ref, w_ref, c_ref, s_ref, zc_ref, zs_ref, wb_ref):
    _bf16_once(w_ref, wb_ref)
    m = mod_ref[0]
    h = _modulate(x_ref[...], g_ref[...], m[:, 0:D], m[:, D:2 * D])
    z = jnp.dot(h.astype(BF16), wb_ref[...], preferred_element_type=F32).astype(BF16)
    for g in range(D // FG):
        zg = z[:, g * FG:(g + 1) * FG]
        zc_ref[:, g * FG:(g + 1) * FG] = jnp.dot(zg, c_ref[...], preferred_element_type=F32).astype(BF16)
        zs_ref[:, g * FG:(g + 1) * FG] = jnp.dot(zg, s_ref[...], preferred_element_type=F32).astype(BF16)


def _l1_in(x, mods, g, w, c256, s256):
    const = lambda shape: pl.BlockSpec(shape, lambda i: (0,) * len(shape))
    return pl.pallas_call(
        _l1_in_body,
        grid=(NT,),
        in_specs=[pl.BlockSpec((TB, D), lambda i: (i, 0)),
                  pl.BlockSpec((1, 1, 6 * D),
                               lambda i: (SUBLANES + _mod_row(i, NT_CTX, T_LAT // TB), 0, 0)),
                  const((1, D)), const((D, D)), const((FG, FG)), const((FG, FG))],
        out_specs=[pl.BlockSpec((TB, D), lambda i: (i, 0)), pl.BlockSpec((TB, D), lambda i: (i, 0))],
        out_shape=[jax.ShapeDtypeStruct((N_TOK, D), BF16), jax.ShapeDtypeStruct((N_TOK, D), BF16)],
        scratch_shapes=[pltpu.VMEM((D, D), BF16)],
        compiler_params=_cparams(1),
        name="l1_in",
    )(x, mods, g, w, c256, s256)


def _l1_out_body(zc_t_ref, zs_t_ref, zc_q_ref, zs_q_ref, c256_ref, s256_ref, c1k_ref, s1k_ref,
                 x_ref, mod_ref, wo_ref, o_ref, f_ref, wb_ref):
    i = pl.program_id(0)
    _bf16_once(wo_ref, wb_ref)

    @pl.when(i < NT_CTX)
    def _():
        for q in range(TB // T_CTX):
            rows = slice(q * T_CTX, (q + 1) * T_CTX)
            f = (jnp.dot(c256_ref[...], zc_t_ref[rows, :], preferred_element_type=F32)
                 - jnp.dot(s256_ref[...], zs_t_ref[rows, :], preferred_element_type=F32))
            f_ref[rows, :] = f.astype(BF16)

    @pl.when(i >= NT_CTX)
    def _():
        f = (jnp.dot(c1k_ref[...], zc_q_ref[...], preferred_element_type=F32)
             - jnp.dot(s1k_ref[...], zs_q_ref[...], preferred_element_type=F32))
        f_ref[...] = f.astype(BF16)

    out = jnp.dot(f_ref[...], wb_ref[...], preferred_element_type=F32)
    o_ref[...] = x_ref[...] + mod_ref[0][:, 2 * D:3 * D] * out


def _l1_out(zc, zs, c256, s256, c1k, s1k, x, mods, w_out):
    const = lambda shape: pl.BlockSpec(shape, lambda i: (0,) * len(shape))
    tile_map = lambda i: (jnp.minimum(i, NT_CTX - 1), 0)
    seq_map = lambda i: (N_CTX // T_LAT + jnp.maximum(i - NT_CTX, 0) // (T_LAT // TB), 0)
    row_map = lambda i: (jnp.maximum(i - NT_CTX, 0) % (T_LAT // TB), 0)
    return pl.pallas_call(
        _l1_out_body,
        grid=(NT,),
        in_specs=[pl.BlockSpec((TB, D), tile_map), pl.BlockSpec((TB, D), tile_map),
                  pl.BlockSpec((T_LAT, D), seq_map), pl.BlockSpec((T_LAT, D), seq_map),
                  const((T_CTX, T_CTX)), const((T_CTX, T_CTX)),
                  pl.BlockSpec((TB, T_LAT), row_map), pl.BlockSpec((TB, T_LAT), row_map),
                  pl.BlockSpec((TB, D), lambda i: (i, 0)),
                  pl.BlockSpec((1, 1, 6 * D),
                               lambda i: (SUBLANES + _mod_row(i, NT_CTX, T_LAT // TB), 0, 0)),
                  const((D, D))],
        out_specs=pl.BlockSpec((TB, D), lambda i: (i, 0)),
        out_shape=jax.ShapeDtypeStruct((N_TOK, D), F32),
        scratch_shapes=[pltpu.VMEM((TB, D), BF16), pltpu.VMEM((D, D), BF16)],
        compiler_params=_cparams(1),
        name="l1_out",
    )(zc, zs, zc, zs, c256, s256, c1k, s1k, x, mods, w_out)


def _split_hi_lo(w):
    hi = w.astype(BF16)
    lo = (w - hi.astype(F32)).astype(BF16)
    return jnp.stack([hi, lo])


def kernel(x_prompt, x_sample, cache_k, cache_v, c, c_ctx, ada_w, ada_b, norm_mix, norm_ffn, a_w_in, a_q_norm, a_k_norm, a_sink, pool_w, pool_scale, a_w_out, f_w_in, f_w_out, router_g_w, router_g_b, router_e_w, router_e_b, moe_w1, moe_w3, moe_w2):
    xp = x_prompt.reshape(N_CTX, D)
    xs = x_sample.reshape(N_LAT, D)

    cond8 = jnp.zeros((SUBLANES, D), F32).at[0].set(c_ctx).at[1:1 + N_LAT_B].set(c)
    mods = _adaln(cond8, ada_w, ada_b).reshape(DEPTH * SUBLANES, 1, 6 * D)

    tabs = _rope_tables()
    lane = np.arange(LANES)
    bd = jnp.asarray((lane[:, None] // HEAD_DIM) == (lane[None, :] // HEAD_DIM), BF16)
    tri = jnp.asarray(np.arange(TB)[:, None] < np.arange(TB)[None, :], BF16)
    c256, s256 = _dft_tables(T_CTX)
    c1k, s1k = _dft_tables(T_LAT)

    def router_operands(l):
        w = jnp.concatenate([router_e_w[l], router_g_w[l]], axis=1).T
        w = jnp.pad(w, ((0, ROUTE_ROWS - w.shape[0]), (0, 0)))
        b = jnp.concatenate([router_e_b[l], router_g_b[l]])
        b = jnp.pad(b, (0, ROUTE_ROWS - b.shape[0]))
        return _split_hi_lo(w), jnp.broadcast_to(b[:, None], (ROUTE_ROWS, LANES))

    qg = jnp.tile(a_q_norm[0], LANES // HEAD_DIM)[None, :]
    kg = jnp.tile(a_k_norm[0], LANES // HEAD_DIM)[None, :]
    q, k, v, u, new_k, new_v = _l0_in(xp, xs, mods, norm_mix[0][None, :], a_w_in[0], qg, kg, bd, tabs)
    sink_b = jnp.broadcast_to(a_sink[0][:, None], (N_HEADS, LANES))
    o_ctx = _ctx_attn(q, k, v, sink_b)
    ck = cache_k[:, 0].reshape(N_LAT_B, PAST, KV_W)
    cv = cache_v[:, 0].reshape(N_LAT_B, PAST, KV_W)
    o_lat = _lat_attn(q, k, v, ck, cv, sink_b)
    x1 = _l0_out(o_ctx, o_lat, u, xp, xs, mods, pool_w[0], pool_scale[0][None, :], a_w_out[0])
    wr, br = router_operands(0)
    (x2,) = _moe(x1, mods, 0, norm_ffn[0][None, :], wr, br, tri, moe_w1, moe_w3, moe_w2, False)

    zc, zs = _l1_in(x2, mods, norm_mix[1][None, :], f_w_in[0], c256, s256)
    x3 = _l1_out(zc, zs, c256, s256, c1k, s1k, x2, mods, f_w_out[0])
    wr, br = router_operands(1)
    yp, ys = _moe(x3, mods, 1, norm_ffn[1][None, :], wr, br, tri, moe_w1, moe_w3, moe_w2, True)

    def cache_entry(t):
        t = t.reshape(N_CTX_B, 1, KV_W // HEAD_DIM, HEAD_DIM, T_CTX)
        return jnp.transpose(t, (0, 1, 4, 2, 3))

    new_k, new_v = cache_entry(new_k), cache_entry(new_v)
    return (yp.reshape(N_CTX_B, T_CTX, D), ys.reshape(N_LAT_B, T_LAT, D), new_k, new_v)
```

```python
import functools

import numpy as np
import jax
import jax.numpy as jnp
from jax import lax
from jax.experimental import pallas as pl
from jax.experimental.pallas import tpu as pltpu

F32 = jnp.float32
BF16 = jnp.bfloat16
I32 = jnp.int32

D = 1024
DEPTH = 2
N_CTX_B, T_CTX = 16, 256
N_LAT_B, T_LAT = 2, 1024
N_CTX = N_CTX_B * T_CTX
N_LAT = N_LAT_B * T_LAT
N_TOK = N_CTX + N_LAT
PAST = 512
GRID_W = 64
HEAD_DIM = 64
N_HEADS = 8
ATTN_W = 512
KV_W = 128
POOL_W = 512
POOL_WINDOWS = (2, 4, 8, 16)
MIX_IN = ATTN_W + 2 * KV_W + POOL_W
WINDOW = 128
N_GROUPS = 4
PER_GROUP = 4
N_EXPERTS = 16
D_EXPERT = 512
ROPE_THETA = 10000.0
EPS = 1e-6
NEG = -1e30

LANES = 128
SUBLANES = 8
TB = 512
NT = N_TOK // TB
NT_CTX = N_CTX // TB
TB_MIX = 1024
TM = 256
PAIRS = 6
N_CLASS = N_GROUPS * PAIRS
CLASS_ROWS = 32
CH = 32
NT_FFN = (N_TOK + N_CLASS * (CH - 1)) // TM + N_CLASS + 1
P_FFN = NT_FFN * TM
XEXT = D + LANES
ROUTE_ROWS = 32

VMEM_LIMIT = 56 * 1024 * 1024


def _cparams(n_axes=1, vmem=None):
    return pltpu.CompilerParams(dimension_semantics=("arbitrary",) * n_axes,
                                vmem_limit_bytes=vmem)


def _modulate(x, g, shift, scale):
    ms = jnp.mean(x * x, axis=-1, keepdims=True)
    return (x * lax.rsqrt(ms + EPS) * g) * (1.0 + scale) + shift


def _bf16_once(w_ref, wb_ref):
    @pl.when(pl.program_id(0) == 0)
    def _():
        wb_ref[...] = w_ref[...].astype(BF16)


def _mod_row(tile, tiles_ctx, tiles_per_lat):
    return (tile >= tiles_ctx).astype(I32) + (tile >= tiles_ctx + tiles_per_lat).astype(I32)


def _rope_tables():
    t = np.arange(T_LAT)
    row = (t // GRID_W).astype(np.float64)
    col = (t % GRID_W).astype(np.float64)
    nf = HEAD_DIM // 4
    freqs = ROPE_THETA ** (-np.arange(nf, dtype=np.float64) / nf)
    d = np.arange(HEAD_DIM)
    pos = np.where(d[None, :] < HEAD_DIM // 2, row[:, None], col[:, None])
    ang = pos * freqs[d % nf][None, :]
    first = (d % (HEAD_DIM // 2)) < nf
    cos = np.cos(ang)
    sin_a = np.where(first[None, :], -np.sin(ang), 0.0)
    sin_b = np.where(first[None, :], 0.0, np.sin(ang))
    ident = (np.ones((TB, HEAD_DIM)), np.zeros((TB, HEAD_DIM)), np.zeros((TB, HEAD_DIM)))
    out = []
    for tab, idt in zip((cos, sin_a, sin_b), ident):
        full = np.concatenate([tab, idt], axis=0)
        out.append(jnp.asarray(np.tile(full, (1, LANES // HEAD_DIM)), F32))
    return out


def _dft_tables(t):
    m = np.outer(np.arange(t), np.arange(t)) % t
    ang = 2.0 * np.pi * m / t
    s = 1.0 / np.sqrt(t)
    return jnp.asarray(np.cos(ang) * s, F32).astype(BF16), jnp.asarray(np.sin(ang) * s, F32).astype(BF16)


def _adaln_body(cond_ref, w_ref, b_ref, o_ref):
    c = cond_ref[...]
    s = (c * jax.nn.sigmoid(c)).astype(BF16)
    o_ref[0] = jnp.dot(s, w_ref[0].astype(BF16), preferred_element_type=F32) + b_ref[0]


def _adaln(cond8, ada_w, ada_b):
    tn = 1536
    return pl.pallas_call(
        _adaln_body,
        grid=(DEPTH, 6 * D // tn),
        in_specs=[pl.BlockSpec((SUBLANES, D), lambda l, j: (0, 0)),
                  pl.BlockSpec((1, D, tn), lambda l, j: (l, 0, j)),
                  pl.BlockSpec((1, 1, tn), lambda l, j: (l, 0, j))],
        out_specs=pl.BlockSpec((1, SUBLANES, tn), lambda l, j: (l, 0, j)),
        out_shape=jax.ShapeDtypeStruct((DEPTH, SUBLANES, 6 * D), F32),
        compiler_params=_cparams(2),
        name="adaln",
    )(cond8, ada_w, ada_b.reshape(DEPTH, 1, 6 * D))


def _l0_in_body(xp_ref, xs_ref, mod_ref, g_ref, w_ref, qg_ref, kg_ref, bd_ref,
                cos_ref, sa_ref, sb_ref, q_ref, k_ref, v_ref, u_ref, kc_ref, vc_ref, wb_ref):
    i = pl.program_id(0)
    _bf16_once(w_ref, wb_ref)
    x = jnp.where(i < NT_CTX, xp_ref[...], xs_ref[...])
    m = mod_ref[0]
    h = _modulate(x, g_ref[...], m[:, 0:D], m[:, D:2 * D])
    z = jnp.dot(h.astype(BF16), wb_ref[...], preferred_element_type=F32)
    cos, sa, sb, bd = cos_ref[...], sa_ref[...], sb_ref[...], bd_ref[...]

    def head_norm_rope(zz, gain):
        ss = jnp.dot((zz * zz).astype(BF16), bd, preferred_element_type=F32)
        y = zz * lax.rsqrt(ss * (1.0 / HEAD_DIM) + EPS) * gain
        return (y * cos + pltpu.roll(y, LANES - 16, axis=1) * sa
                + pltpu.roll(y, 16, axis=1) * sb)

    for s in range(ATTN_W // LANES):
        qs = head_norm_rope(z[:, s * LANES:(s + 1) * LANES], qg_ref[...])
        q_ref[:, s * LANES:(s + 1) * LANES] = (qs * (HEAD_DIM ** -0.5)).astype(BF16)
    k = head_norm_rope(z[:, ATTN_W:ATTN_W + KV_W], kg_ref[...])
    v = z[:, ATTN_W + KV_W:ATTN_W + 2 * KV_W]
    k_ref[...] = k
    v_ref[...] = v
    u_ref[...] = z[:, ATTN_W + 2 * KV_W:MIX_IN]

    @pl.when(i < NT_CTX)
    def _():
        for q in range(TB // T_CTX):
            kc_ref[q] = k[q * T_CTX:(q + 1) * T_CTX, :].T
            vc_ref[q] = v[q * T_CTX:(q + 1) * T_CTX, :].T


def _l0_in(xp, xs, mods, g, w_in, qg, kg, bd, tabs):
    tab_spec = pl.BlockSpec(
        (TB, LANES), lambda i: (jnp.where(i < NT_CTX, T_LAT // TB, (i - NT_CTX) % (T_LAT // TB)), 0))
    const = lambda shape: pl.BlockSpec(shape, lambda i: (0,) * len(shape))
    return pl.pallas_call(
        _l0_in_body,
        grid=(NT,),
        in_specs=[pl.BlockSpec((TB, D), lambda i: (jnp.minimum(i, NT_CTX - 1), 0)),
                  pl.BlockSpec((TB, D), lambda i: (jnp.maximum(i - NT_CTX, 0), 0)),
                  pl.BlockSpec((1, 1, 6 * D), lambda i: (_mod_row(i, NT_CTX, T_LAT // TB), 0, 0)),
                  const((1, D)), const((D, MIX_IN)), const((1, LANES)), const((1, LANES)),
                  const((LANES, LANES)), tab_spec, tab_spec, tab_spec],
        out_specs=[pl.BlockSpec((TB, ATTN_W), lambda i: (i, 0)),
                   pl.BlockSpec((TB, KV_W), lambda i: (i, 0)),
                   pl.BlockSpec((TB, KV_W), lambda i: (i, 0)),
                   pl.BlockSpec((TB, POOL_W), lambda i: (i, 0)),
                   pl.BlockSpec((TB // T_CTX, KV_W, T_CTX), lambda i: (jnp.minimum(i, NT_CTX - 1), 0, 0)),
                   pl.BlockSpec((TB // T_CTX, KV_W, T_CTX), lambda i: (jnp.minimum(i, NT_CTX - 1), 0, 0))],
        out_shape=[jax.ShapeDtypeStruct((N_TOK, ATTN_W), BF16),
                   jax.ShapeDtypeStruct((N_TOK, KV_W), F32),
                   jax.ShapeDtypeStruct((N_TOK, KV_W), F32),
                   jax.ShapeDtypeStruct((N_TOK, POOL_W), F32),
                   jax.ShapeDtypeStruct((N_CTX_B, KV_W, T_CTX), F32),
                   jax.ShapeDtypeStruct((N_CTX_B, KV_W, T_CTX), F32)],
        scratch_shapes=[pltpu.VMEM((D, MIX_IN), BF16)],
        compiler_params=_cparams(1),
        name="l0_in",
    )(xp, xs, mods, g, w_in, qg, kg, bd, *tabs)


def _head_halves(x):
    z = jnp.zeros_like(x)
    return jnp.concatenate([x, z], axis=1), jnp.concatenate([z, x], axis=1)


_NT_DIMS = (((1,), (1,)), ((), ()))


def _ones_halves(x):
    one = jnp.ones_like(x)
    return jnp.concatenate([x, one], axis=1), jnp.concatenate([one, x], axis=1)


def _sink_attend(scores, values, sk, half):
    mx = sk
    for sc in scores:
        mx = jnp.maximum(mx, jnp.max(sc, axis=-1, keepdims=True))
    acc = None
    for sc, val in zip(scores, values):
        part = jnp.dot(jnp.exp(sc - mx).astype(BF16), val, preferred_element_type=F32)
        acc = part if acc is None else acc + part
    ones_lane = HEAD_DIM * (1 - half)
    den = acc[:, ones_lane:ones_lane + 1] + jnp.exp(sk - mx)
    return acc * (1.0 / den)


def _sink_col(sink_ref, heads, rows):
    return jnp.concatenate([jnp.broadcast_to(sink_ref[h:h + 1, 0:1], (rows, 1)) for h in heads], axis=0)


def _ctx_attn_body(q_ref, k_ref, v_ref, sink_ref, o_ref):
    k = k_ref[...].astype(BF16)
    v = v_ref[...].astype(BF16)
    lo = lax.broadcasted_iota(I32, (T_CTX, LANES), 1) < HEAD_DIM
    for j in range(KV_W // HEAD_DIM):
        kj = k[:, j * HEAD_DIM:(j + 1) * HEAD_DIM]
        vj = v[:, j * HEAD_DIM:(j + 1) * HEAD_DIM]
        k_halves = _head_halves(kj)
        vd = jnp.concatenate([vj, vj], axis=1)
        q2 = jnp.concatenate([q_ref[:, (2 * j) * LANES:(2 * j + 1) * LANES],
                              q_ref[:, (2 * j + 1) * LANES:(2 * j + 2) * LANES]], axis=0)
        outs = []
        for half in range(2):
            sc = lax.dot_general(q2, k_halves[half], _NT_DIMS, preferred_element_type=F32)
            sk = _sink_col(sink_ref, (4 * j + half, 4 * j + 2 + half), T_CTX)
            mx = jnp.maximum(sk, jnp.max(sc, axis=-1, keepdims=True))
            p = jnp.exp(sc - mx)
            inv = 1.0 / (jnp.exp(sk - mx) + jnp.sum(p, axis=-1, keepdims=True))
            outs.append(jnp.dot((p * inv).astype(BF16), vd, preferred_element_type=F32))
        for s2 in range(2):
            rows = slice(s2 * T_CTX, (s2 + 1) * T_CTX)
            o_ref[:, (2 * j + s2) * LANES:(2 * j + s2 + 1) * LANES] = (
                jnp.where(lo, outs[0][rows], outs[1][rows]).astype(BF16))


def _ctx_attn(q, k, v, sink_b):
    return pl.pallas_call(
        _ctx_attn_body,
        grid=(N_CTX_B,),
        in_specs=[pl.BlockSpec((T_CTX, ATTN_W), lambda b: (b, 0)),
                  pl.BlockSpec((T_CTX, KV_W), lambda b: (b, 0)),
                  pl.BlockSpec((T_CTX, KV_W), lambda b: (b, 0)),
                  pl.BlockSpec((SUBLANES, LANES), lambda b: (0, 0))],
        out_specs=pl.BlockSpec((T_CTX, ATTN_W), lambda b: (b, 0)),
        out_shape=jax.ShapeDtypeStruct((N_CTX, ATTN_W), BF16),
        compiler_params=_cparams(1),
        name="ctx_attn",
    )(q, k, v, sink_b)


QB = 128
SPAN = QB + 2 * WINDOW


def _lat_attn_body(q_ref, k_ref, v_ref, ck_ref, cv_ref, sink_ref, o_ref):
    qb = pl.program_id(1)
    start = qb * QB
    kws, vws = [], []
    for c in (-1, 0, 1):
        cs = pl.multiple_of(jnp.clip(start + c * QB, 0, T_LAT - QB), QB)
        kws.append(k_ref[pl.ds(cs, QB), :])
        vws.append(v_ref[pl.ds(cs, QB), :])
    kw = jnp.concatenate(kws, axis=0).astype(BF16)
    vw = jnp.concatenate(vws, axis=0).astype(BF16)
    ck = ck_ref[0].astype(BF16)
    cv = cv_ref[0].astype(BF16)
    qpos = start + (lax.broadcasted_iota(I32, (2 * QB, SPAN), 0) & (QB - 1))
    kpos = start - WINDOW + lax.broadcasted_iota(I32, (2 * QB, SPAN), 1)
    valid = (kpos >= 0) & (kpos < T_LAT) & (jnp.abs(qpos - kpos) <= WINDOW)
    lo = lax.broadcasted_iota(I32, (QB, LANES), 1) < HEAD_DIM
    for j in range(KV_W // HEAD_DIM):
        sl = slice(j * HEAD_DIM, (j + 1) * HEAD_DIM)
        kw_halves = _head_halves(kw[:, sl])
        ck_halves = _head_halves(ck[:, sl])
        vw_halves = _ones_halves(vw[:, sl])
        cv_halves = _ones_halves(cv[:, sl])
        q2 = jnp.concatenate([q_ref[:, (2 * j) * LANES:(2 * j + 1) * LANES],
                              q_ref[:, (2 * j + 1) * LANES:(2 * j + 2) * LANES]], axis=0)
        outs = []
        for half in range(2):
            s_win = lax.dot_general(q2, kw_halves[half], _NT_DIMS, preferred_element_type=F32)
            s_win = jnp.where(valid, s_win, NEG)
            s_ctx = lax.dot_general(q2, ck_halves[half], _NT_DIMS, preferred_element_type=F32)
            sk = _sink_col(sink_ref, (4 * j + half, 4 * j + 2 + half), QB)
            outs.append(_sink_attend([s_win, s_ctx], [vw_halves[half], cv_halves[half]], sk, half))
        for s2 in range(2):
            rows = slice(s2 * QB, (s2 + 1) * QB)
            o_ref[:, (2 * j + s2) * LANES:(2 * j + s2 + 1) * LANES] = (
                jnp.where(lo, outs[0][rows], outs[1][rows]).astype(BF16))


def _lat_attn(q, k, v, ck, cv, sink_b):
    lat0 = N_CTX // T_LAT
    return pl.pallas_call(
        _lat_attn_body,
        grid=(N_LAT_B, T_LAT // QB),
        in_specs=[pl.BlockSpec((QB, ATTN_W), lambda b, i: (N_CTX // QB + b * (T_LAT // QB) + i, 0)),
                  pl.BlockSpec((T_LAT, KV_W), lambda b, i: (lat0 + b, 0)),
                  pl.BlockSpec((T_LAT, KV_W), lambda b, i: (lat0 + b, 0)),
                  pl.BlockSpec((1, PAST, KV_W), lambda b, i: (b, 0, 0)),
                  pl.BlockSpec((1, PAST, KV_W), lambda b, i: (b, 0, 0)),
                  pl.BlockSpec((SUBLANES, LANES), lambda b, i: (0, 0))],
        out_specs=pl.BlockSpec((QB, ATTN_W), lambda b, i: (b * (T_LAT // QB) + i, 0)),
        out_shape=jax.ShapeDtypeStruct((N_LAT, ATTN_W), BF16),
        compiler_params=_cparams(2),
        name="lat_attn",
    )(q, k, v, ck, cv, sink_b)


def _l0_out_body(oc_ref, ol_ref, u_ref, xp_ref, xs_ref, mod_ref, pw_ref, ps_ref, wo_ref, x1_ref, wb_ref):
    i = pl.program_id(0)
    _bf16_once(wo_ref, wb_ref)
    is_ctx = i < N_CTX // TB_MIX
    o = jnp.where(is_ctx, oc_ref[...], ol_ref[...])
    x = jnp.where(is_ctx, xp_ref[...], xs_ref[...])
    tseq = jnp.where(is_ctx, T_CTX, T_LAT)
    pos = lax.broadcasted_iota(I32, (TB_MIX, LANES), 0) & (tseq - 1)
    ys = []
    for g, win in enumerate(POOL_WINDOWS):
        hw = win // 2
        ug = u_ref[:, g * LANES:(g + 1) * LANES]
        acc = ug
        for jj in range(-hw, hw):
            if jj == 0:
                continue
            sh = pltpu.roll(ug, (-jj) % TB_MIX, axis=0)
            ok = (pos + jj >= 0) if jj < 0 else (pos + jj < tseq)
            acc = acc + jnp.where(ok, sh, 0.0)
        cnt = (jnp.minimum(pos + hw, tseq) - jnp.maximum(pos - hw, 0)).astype(F32)
        pooled = acc / cnt - ug
        ys.append(jnp.dot(pooled.astype(BF16), pw_ref[g].astype(BF16), preferred_element_type=F32))
    y = jnp.concatenate(ys, axis=1) * ps_ref[...]
    out = (jnp.dot(o, wb_ref[0:ATTN_W, :], preferred_element_type=F32)
           + jnp.dot(y.astype(BF16), wb_ref[ATTN_W:ATTN_W + POOL_W, :], preferred_element_type=F32))
    x1_ref[...] = x + mod_ref[0][:, 2 * D:3 * D] * out


def _l0_out(o_ctx, o_lat, u, xp, xs, mods, pool_w, pool_scale, w_out):
    ntc = N_CTX // TB_MIX
    const = lambda shape: pl.BlockSpec(shape, lambda i: (0,) * len(shape))
    ctx_map = lambda i: (jnp.minimum(i, ntc - 1), 0)
    lat_map = lambda i: (jnp.maximum(i - ntc, 0), 0)
    return pl.pallas_call(
        _l0_out_body,
        grid=(N_TOK // TB_MIX,),
        in_specs=[pl.BlockSpec((TB_MIX, ATTN_W), ctx_map),
                  pl.BlockSpec((TB_MIX, ATTN_W), lat_map),
                  pl.BlockSpec((TB_MIX, POOL_W), lambda i: (i, 0)),
                  pl.BlockSpec((TB_MIX, D), ctx_map),
                  pl.BlockSpec((TB_MIX, D), lat_map),
                  pl.BlockSpec((1, 1, 6 * D), lambda i: (_mod_row(i, ntc, 1), 0, 0)),
                  const((len(POOL_WINDOWS), LANES, LANES)), const((1, POOL_W)), const((D, D))],
        out_specs=pl.BlockSpec((TB_MIX, D), lambda i: (i, 0)),
        out_shape=jax.ShapeDtypeStruct((N_TOK, D), F32),
        scratch_shapes=[pltpu.VMEM((D, D), BF16)],
        compiler_params=_cparams(1, VMEM_LIMIT),
        name="l0_out",
    )(o_ctx, o_lat, u, xp, xs, mods, pool_w, pool_scale, w_out)


def _first_max(vals):
    best, idx = vals[0], jnp.zeros(vals[0].shape, I32)
    for r in range(1, len(vals)):
        better = vals[r] > best
        idx = jnp.where(better, r, idx)
        best = jnp.where(better, vals[r], best)
    return best, idx


def _softmax_rows(rows):
    mx = functools.reduce(jnp.maximum, rows)
    ex = [jnp.exp(r - mx) for r in rows]
    tot = functools.reduce(lambda a, b: a + b, ex)
    return [e / tot for e in ex]


def _route_body(x_ref, mod_ref, g_ref, wr_ref, br_ref, tri_ref,
                xext_ref, info_ref, cnt_ref, cctx_ref, base_ref):
    i = pl.program_id(0)

    @pl.when(i == 0)
    def _():
        base_ref[...] = jnp.zeros_like(base_ref)

    x = x_ref[...]
    m = mod_ref[0]
    h = _modulate(x, g_ref[...], m[:, 3 * D:4 * D], m[:, 4 * D:5 * D])

    hh = h.astype(BF16)
    hl = (h - hh.astype(F32)).astype(BF16)
    wh, wl = wr_ref[0], wr_ref[1]
    lg = (lax.dot_general(wh, hh, _NT_DIMS, preferred_element_type=F32)
          + lax.dot_general(wl, hh, _NT_DIMS, preferred_element_type=F32)
          + lax.dot_general(wh, hl, _NT_DIMS, preferred_element_type=F32)) + br_ref[:, 0:1]

    pg = _softmax_rows([lg[N_EXPERTS + r:N_EXPERTS + r + 1] for r in range(N_GROUPS)])
    pg_top, gi = _first_max(pg)
    le = []
    for j in range(PER_GROUP):
        sel = lg[(N_GROUPS - 1) * PER_GROUP + j:(N_GROUPS - 1) * PER_GROUP + j + 1]
        for g in range(N_GROUPS - 2, -1, -1):
            sel = jnp.where(gi == g, lg[g * PER_GROUP + j:g * PER_GROUP + j + 1], sel)
        le.append(sel)
    pe = _softmax_rows(le)
    p1, i1 = _first_max(pe)
    p2, i2 = _first_max([jnp.where(i1 == j, -1.0, pe[j]) for j in range(PER_GROUP)])
    den = p1 + p2
    w1 = pg_top * p1 / den
    w2 = pg_top * p2 / den

    lo = jnp.minimum(i1, i2)
    hi = jnp.maximum(i1, i2)
    cls = gi * PAIRS + jnp.where(lo == 0, 0, jnp.where(lo == 1, 3, 5)) + hi - lo - 1
    w_lo = jnp.where(i1 == lo, w1, w2)
    w_hi = jnp.where(i1 == lo, w2, w1)

    crow = lax.broadcasted_iota(I32, (CLASS_ROWS, TB), 0)
    hit = crow == cls
    onehot = jnp.where(hit, 1.0, 0.0)
    before = jnp.dot(onehot.astype(BF16), tri_ref[...], preferred_element_type=F32)
    before = before + base_ref[:, 0:1]
    rank = jnp.sum(jnp.where(hit, before, 0.0), axis=0, keepdims=True)
    base_ref[...] = base_ref[...] + jnp.sum(onehot, axis=1, keepdims=True)
    cnt_ref[...] = base_ref[...]

    @pl.when(i == NT_CTX - 1)
    def _():
        cctx_ref[...] = base_ref[...]

    mod_id =jnp.zeros_like(w1) + _mod_row(i, NT_CTX, T_LAT // TB).astype(F32)
    zero = jnp.zeros_like(w1)
    info_ref[...] = jnp.concatenate([cls.astype(F32), rank, zero, zero, zero, zero, zero, zero], axis=0)
    side = jnp.concatenate([w_lo, w_hi, mod_id, jnp.zeros((LANES - 3, TB), F32)], axis=0).T
    xext_ref[:, 0:D] = x
    xext_ref[:, D:XEXT] = side


def _route(x, mods, layer, g, wr, br, tri):
    const = lambda shape: pl.BlockSpec(shape, lambda i: (0,) * len(shape))
    return pl.pallas_call(
        _route_body,
        grid=(NT,),
        in_specs=[pl.BlockSpec((TB, D), lambda i: (i, 0)),
                  pl.BlockSpec((1, 1, 6 * D),
                               lambda i: (layer * SUBLANES + _mod_row(i, NT_CTX, T_LAT // TB), 0, 0)),
                  const((1, D)), const((2, ROUTE_ROWS, D)), const((ROUTE_ROWS, LANES)),
                  const((TB, TB))],
        out_specs=[pl.BlockSpec((TB, XEXT), lambda i: (i, 0)),
                   pl.BlockSpec((SUBLANES, TB), lambda i: (0, i)),
                   pl.BlockSpec((CLASS_ROWS, LANES), lambda i: (0, 0)),
                   pl.BlockSpec((CLASS_ROWS, LANES), lambda i: (0, 0))],
        out_shape=[jax.ShapeDtypeStruct((N_TOK, XEXT), F32),
                   jax.ShapeDtypeStruct((SUBLANES, N_TOK), F32),
                   jax.ShapeDtypeStruct((CLASS_ROWS, LANES), F32),
                   jax.ShapeDtypeStruct((CLASS_ROWS, LANES), F32)],
        scratch_shapes=[pltpu.VMEM((CLASS_ROWS, LANES), F32)],
        compiler_params=_cparams(1),
        name=f"route{layer}",
    )(x, mods, g, wr, br, tri)


def _plan(info, counts, counts_ctx):
    cls = info[0].astype(I32)
    rank = info[1].astype(I32)
    cnt = counts[:N_CLASS, 0].astype(I32)
    cnt_ctx = counts_ctx[:N_CLASS, 0].astype(I32)
    lat_off = (cnt_ctx + CH - 1) // CH * CH
    cnt_all = lat_off + (cnt - cnt_ctx)
    tiles = (cnt_all + TM - 1) // TM
    tend = jnp.cumsum(tiles)
    tstart = tend - tiles
    n_used = tend[-1]
    cidx = jnp.arange(N_CLASS, dtype=I32)
    is_lat = (jnp.arange(N_TOK, dtype=I32) >= N_CTX)[:, None]
    pos = jnp.sum(jnp.where(cls[:, None] == cidx, tstart * TM + jnp.where(is_lat, lat_off - cnt_ctx, 0), 0),
                  axis=-1) + rank
    n = jnp.arange(NT_FFN, dtype=I32)
    tile = jnp.minimum(n, jnp.maximum(n_used - 1, 0))
    tcls = jnp.minimum(jnp.sum((tile[:, None] >= tend[None, :]).astype(I32), axis=1), N_CLASS - 1)
    of_cls = tcls[:, None] == cidx
    before = (tile - jnp.sum(jnp.where(of_cls, tstart, 0), axis=1)) * TM
    in_tile = lambda v: jnp.where(n < n_used, jnp.clip(jnp.sum(jnp.where(of_cls, v, 0), axis=1) - before, 0, TM), 0)
    rows, n_ctx, lat_start = in_tile(cnt_all), in_tile(cnt_ctx), in_tile(lat_off)
    pair = tcls % PAIRS
    lo = (pair >= 3).astype(I32) + (pair >= 5).astype(I32)
    hi = jnp.where(pair < 3, pair + 1, jnp.where(pair < 5, pair - 1, 3))
    e_lo = (tcls // PAIRS) * PER_GROUP + lo
    e_hi = (tcls // PAIRS) * PER_GROUP + hi
    eidx = jnp.arange(N_EXPERTS, dtype=I32)
    live = n < n_used
    uses = ((e_lo[:, None] == eidx) | (e_hi[:, None] == eidx)) & live[:, None]
    first = jnp.min(jnp.where(uses, n[:, None], NT_FFN), axis=0)
    new_lo = live & (jnp.sum(jnp.where(e_lo[:, None] == eidx, first, 0), axis=1) == n)
    new_hi = live & (jnp.sum(jnp.where(e_hi[:, None] == eidx, first, 0), axis=1) == n)

    def held(new, e):
        last = lax.cummax(jnp.where(new, n, 0))
        return jnp.sum(jnp.where(last[:, None] == n[None, :], e[None, :], 0), axis=1)

    slots = (held(new_lo, e_lo), held(new_hi, e_hi), new_lo.astype(I32), new_hi.astype(I32))
    return pos, lo, hi, slots, n_used.reshape(1), (rows, n_ctx, lat_start)


OCT = TM // SUBLANES


def _ffn_body(split, pos_ref, lo_ref, hi_ref, sa_ref, sb_ref, newa_ref, newb_ref, nu_ref, rows_ref, nctx_ref, lat_ref,
              xext_hbm, mod_ref, g_ref, w1a_ref, w1b_ref, w3a_ref, w3b_ref, w2a_ref, w2b_ref, *rest):
    outs = rest[:2] if split else rest[:1]
    src_ref, dst_ref, xbuf, ybuf, wb1, wb3, wb2, gsem, ssem = rest[len(outs):]
    n = pl.program_id(0)
    n_used = nu_ref[0]

    def n_chunks(tile):
        return lax.shift_right_logical(rows_ref[tile] + (CH - 1), CH.bit_length() - 1)

    def chunk_wait(buf, sem, s):
        rows = buf.at[s, pl.ds(0, CH // SUBLANES)]
        pltpu.make_async_copy(rows, rows, sem.at[s]).wait()

    def gather_copy(tile, s, c, j):
        return pltpu.make_async_copy(
            xext_hbm.at[pl.ds(src_ref[tile * TM + c * CH + j], 1)],
            xbuf.at[s, c * (CH // SUBLANES) + j // SUBLANES, pl.ds(j % SUBLANES, 1)], gsem.at[s])

    def start_gather(tile, s):
        def chunk(c, carry):
            for j in range(CH):
                gather_copy(tile, s, c, j).start()
            return carry
        lax.fori_loop(0, n_chunks(tile), chunk, 0)

    def wait_gather(tile, s):
        def chunk(c, carry):
            chunk_wait(xbuf, gsem, s)
            return carry
        lax.fori_loop(0, n_chunks(tile), chunk, 0)

    def scatter(tile, s, wait):
        rows, n_ctx, lat_start = rows_ref[tile], nctx_ref[tile], lat_ref[tile]
        shift = CH.bit_length() - 1

        if wait:
            total = n_ctx + jnp.maximum(rows - lat_start, 0)
            one_row = ybuf.at[s, 0, pl.ds(0, 1)]

            def chunk_done(c, carry):
                chunk_wait(ybuf, ssem, s)
                return carry

            def row_done(r, carry):
                pltpu.make_async_copy(one_row, one_row, ssem.at[s]).wait()
                return carry

            lax.fori_loop(0, lax.shift_right_logical(total, shift), chunk_done, 0)
            lax.fori_loop(0, total & (CH - 1), row_done, 0)
            return

        def row(out, octet, sub, r):
            pltpu.make_async_copy(ybuf.at[s, octet, pl.ds(sub, 1)],
                                  out.at[pl.ds(dst_ref[tile * TM + r], 1)], ssem.at[s]).start()

        def chunk(c, carry):
            first = c * CH
            to_lat = first >= lat_start
            valid = jnp.clip(jnp.where(to_lat, rows, n_ctx) - first, 0, CH)
            for out, mine in ((outs[0], jnp.logical_not(to_lat)), (outs[-1], to_lat)):
                @pl.when(mine & (valid == CH))
                def _():
                    for j in range(CH):
                        row(out, c * (CH // SUBLANES) + j // SUBLANES, j % SUBLANES, first + j)

                def single(j, carry):
                    r = first + j
                    row(out, lax.shift_right_logical(r, 3), r & (SUBLANES - 1), r)
                    return carry

                lax.fori_loop(0, jnp.where(mine & (valid < CH), valid, 0), single, 0)
            return carry

        lax.fori_loop(0, n_chunks(tile), chunk, 0)

    @pl.when(n == 0)
    def _():
        def pad_rows(tile, carry):
            @pl.when(rows_ref[tile] > 0)
            def _():
                shift = CH.bit_length() - 1
                for c in (jnp.minimum(lax.shift_right_logical(nctx_ref[tile], shift), TM // CH - 1),
                          n_chunks(tile) - 1):
                    for j in range(CH):
                        src_ref[tile * TM + c * CH + j] = 0
            return carry

        lax.fori_loop(0, NT_FFN, pad_rows, 0)

        def put(t, carry):
            p = pos_ref[t]
            src_ref[p] = t
            dst_ref[p] = jnp.where(t < N_CTX, t, t - N_CTX) if split else t
            return carry

        lax.fori_loop(0, N_TOK, put, 0, unroll=8)

        xbuf[...] = jnp.zeros_like(xbuf)
        start_gather(0, 0)

    def step(slot):
        @pl.when(n + 1 < n_used)
        def _():
            start_gather(n + 1, 1 - slot)

        wait_gather(n, slot)

        @pl.when(n >= 2)
        def _():
            scatter(n - 2, slot, wait=True)

        e_lo = lo_ref[n]
        e_hi = hi_ref[n]

        @pl.when(newa_ref[n] == 1)
        def _():
            wb1[e_lo] = w1a_ref[0, 0].astype(BF16)
            wb3[e_lo] = w3a_ref[0, 0].astype(BF16)
            wb2[e_lo] = w2a_ref[0, 0].astype(BF16)

        @pl.when(newb_ref[n] == 1)
        def _():
            wb1[e_hi] = w1b_ref[0, 0].astype(BF16)
            wb3[e_hi] = w3b_ref[0, 0].astype(BF16)
            wb2[e_hi] = w2b_ref[0, 0].astype(BF16)

        xe = xbuf[slot].reshape(TM, XEXT)
        x = xe[:, 0:D]
        w_lo = xe[:, D:D + 1]
        w_hi = xe[:, D + 1:D + 2]
        mod_id = xe[:, D + 2:D + 3]

        def pick(lo, hi):
            return jnp.where(mod_id < 0.5, mod_ref[0][:, lo:hi],
                             jnp.where(mod_id < 1.5, mod_ref[1][:, lo:hi], mod_ref[2][:, lo:hi]))

        h = _modulate(x, g_ref[...], pick(3 * D, 4 * D), pick(4 * D, 5 * D)).astype(BF16)

        def act(e, w):
            h1 = jnp.dot(h, wb1[e], preferred_element_type=F32)
            h3 = jnp.dot(h, wb3[e], preferred_element_type=F32)
            return ((h1 * jax.nn.sigmoid(h1)) * h3 * w).astype(BF16)

        y = (jnp.dot(act(e_lo, w_lo), wb2[e_lo], preferred_element_type=F32)
             + jnp.dot(act(e_hi, w_hi), wb2[e_hi], preferred_element_type=F32))
        ybuf[slot] = (x + pick(5 * D, 6 * D) * y).reshape(OCT, SUBLANES, D)
        scatter(n, slot, wait=False)

        @pl.when(n == n_used - 1)
        def _():
            @pl.when(n >= 1)
            def _():
                scatter(n - 1, 1 - slot, wait=True)
            scatter(n, slot, wait=True)

    for s in range(2):
        @pl.when((n < n_used) & (n % 2 == s))
        def _():
            step(s)


def _ffn(pos, lo, hi, slots, n_used, extents, xext, mods, layer, g, w1, w3, w2, split):
    a_map = lambda n, p, lo, hi, sa, *_: (layer, sa[n], 0, 0)
    b_map = lambda n, p, lo, hi, sa, sb, *_: (layer, sb[n], 0, 0)
    up = lambda imap: pl.BlockSpec((1, 1, D, D_EXPERT), imap)
    down = lambda imap: pl.BlockSpec((1, 1, D_EXPERT, D), imap)
    out_rows = (N_CTX, N_LAT) if split else (N_TOK,)
    return pl.pallas_call(
        functools.partial(_ffn_body, split),
        grid_spec=pltpu.PrefetchScalarGridSpec(
            num_scalar_prefetch=11, grid=(NT_FFN,),
            in_specs=[pl.BlockSpec(memory_space=pl.ANY),
                      pl.BlockSpec((SUBLANES, 1, 6 * D), lambda n, *_: (layer, 0, 0)),
                      pl.BlockSpec((1, D), lambda n, *_: (0, 0)),
                      up(a_map), up(b_map), up(a_map), up(b_map), down(a_map), down(b_map)],
            out_specs=[pl.BlockSpec(memory_space=pl.ANY) for _ in out_rows],
            scratch_shapes=[pltpu.SMEM((P_FFN,), I32), pltpu.SMEM((P_FFN,), I32),
                            pltpu.VMEM((2, OCT, SUBLANES, XEXT), F32),
                            pltpu.VMEM((2, OCT, SUBLANES, D), F32),
                            pltpu.VMEM((PER_GROUP, D, D_EXPERT), BF16),
                            pltpu.VMEM((PER_GROUP, D, D_EXPERT), BF16),
                            pltpu.VMEM((PER_GROUP, D_EXPERT, D), BF16),
                            pltpu.SemaphoreType.DMA((2,)), pltpu.SemaphoreType.DMA((2,))]),
        out_shape=[jax.ShapeDtypeStruct((r, D), F32) for r in out_rows],
        compiler_params=_cparams(1, VMEM_LIMIT),
        name=f"ffn{layer}",
    )(pos, lo, hi, *slots, n_used, *extents, xext, mods, g, w1, w1, w3, w3, w2, w2)


def _moe(x, mods, layer, g, wr, br, tri, w1, w3, w2, split):
    xext, info, counts, counts_ctx = _route(x, mods, layer, g, wr, br, tri)
    pos, lo, hi, slots, n_used, extents = _plan(info, counts, counts_ctx)
    return _ffn(pos, lo, hi, slots, n_used, extents, xext, mods, layer, g, w1, w3, w2, split)


FG = 256


def _l1_in_body(x_ref, mod_ref, g_ref, w_ref, c_ref, s_ref, zc_ref, zs_ref, wb_ref):
    _bf16_once(w_ref, wb_ref)
    m = mod_ref[0]
    h = _modulate(x_ref[...], g_ref[...], m[:, 0:D], m[:, D:2 * D])
    z = jnp.dot(h.astype(BF16), wb_ref[...], preferred_element_type=F32).astype(BF16)
    for g in range(D // FG):
        zg = z[:, g * FG:(g + 1) * FG]
        zc_ref[:, g * FG:(g + 1) * FG] = jnp.dot(zg, c_ref[...], preferred_element_type=F32).astype(BF16)
        zs_ref[:, g * FG:(g + 1) * FG] = jnp.dot(zg, s_ref[...], preferred_element_type=F32).astype(BF16)


def _l1_in(x, mods, g, w, c256, s256):
    const = lambda shape: pl.BlockSpec(shape, lambda i: (0,) * len(shape))
    return pl.pallas_call(
        _l1_in_body,
        grid=(NT,),
        in_specs=[pl.BlockSpec((TB, D), lambda i: (i, 0)),
                  pl.BlockSpec((1, 1, 6 * D),
                               lambda i: (SUBLANES + _mod_row(i, NT_CTX, T_LAT // TB), 0, 0)),
                  const((1, D)), const((D, D)), const((FG, FG)), const((FG, FG))],
        out_specs=[pl.BlockSpec((TB, D), lambda i: (i, 0)), pl.BlockSpec((TB, D), lambda i: (i, 0))],
        out_shape=[jax.ShapeDtypeStruct((N_TOK, D), BF16), jax.ShapeDtypeStruct((N_TOK, D), BF16)],
        scratch_shapes=[pltpu.VMEM((D, D), BF16)],
        compiler_params=_cparams(1),
        name="l1_in",
    )(x, mods, g, w, c256, s256)


def _l1_out_body(zc_t_ref, zs_t_ref, zc_q_ref, zs_q_ref, c256_ref, s256_ref, c1k_ref, s1k_ref,
                 x_ref, mod_ref, wo_ref, o_ref, f_ref, wb_ref):
    i = pl.program_id(0)
    _bf16_once(wo_ref, wb_ref)

    @pl.when(i < NT_CTX)
    def _():
        for q in range(TB // T_CTX):
            rows = slice(q * T_CTX, (q + 1) * T_CTX)
            f = (jnp.dot(c256_ref[...], zc_t_ref[rows, :], preferred_element_type=F32)
                 - jnp.dot(s256_ref[...], zs_t_ref[rows, :], preferred_element_type=F32))
            f_ref[rows, :] = f.astype(BF16)

    @pl.when(i >= NT_CTX)
    def _():
        f = (jnp.dot(c1k_ref[...], zc_q_ref[...], preferred_element_type=F32)
             - jnp.dot(s1k_ref[...], zs_q_ref[...], preferred_element_type=F32))
        f_ref[...] = f.astype(BF16)

    out = jnp.dot(f_ref[...], wb_ref[...], preferred_element_type=F32)
    o_ref[...] = x_ref[...] + mod_ref[0][:, 2 * D:3 * D] * out


def _l1_out(zc, zs, c256, s256, c1k, s1k, x, mods, w_out):
    const = lambda shape: pl.BlockSpec(shape, lambda i: (0,) * len(shape))
    tile_map = lambda i: (jnp.minimum(i, NT_CTX - 1), 0)
    seq_map = lambda i: (N_CTX // T_LAT + jnp.maximum(i - NT_CTX, 0) // (T_LAT // TB), 0)
    row_map = lambda i: (jnp.maximum(i - NT_CTX, 0) % (T_LAT // TB), 0)
    return pl.pallas_call(
        _l1_out_body,
        grid=(NT,),
        in_specs=[pl.BlockSpec((TB, D), tile_map), pl.BlockSpec((TB, D), tile_map),
                  pl.BlockSpec((T_LAT, D), seq_map), pl.BlockSpec((T_LAT, D), seq_map),
                  const((T_CTX, T_CTX)), const((T_CTX, T_CTX)),
                  pl.BlockSpec((TB, T_LAT), row_map), pl.BlockSpec((TB, T_LAT), row_map),
                  pl.BlockSpec((TB, D), lambda i: (i, 0)),
                  pl.BlockSpec((1, 1, 6 * D),
                               lambda i: (SUBLANES + _mod_row(i, NT_CTX, T_LAT // TB), 0, 0)),
                  const((D, D))],
        out_specs=pl.BlockSpec((TB, D), lambda i: (i, 0)),
        out_shape=jax.ShapeDtypeStruct((N_TOK, D), F32),
        scratch_shapes=[pltpu.VMEM((TB, D), BF16), pltpu.VMEM((D, D), BF16)],
        compiler_params=_cparams(1),
        name="l1_out",
    )(zc, zs, zc, zs, c256, s256, c1k, s1k, x, mods, w_out)


def _split_hi_lo(w):
    hi = w.astype(BF16)
    lo = (w - hi.astype(F32)).astype(BF16)
    return jnp.stack([hi, lo])


def kernel(x_prompt, x_sample, cache_k, cache_v, c, c_ctx, ada_w, ada_b, norm_mix, norm_ffn, a_w_in, a_q_norm, a_k_norm, a_sink, pool_w, pool_scale, a_w_out, f_w_in, f_w_out, router_g_w, router_g_b, router_e_w, router_e_b, moe_w1, moe_w3, moe_w2):
    xp = x_prompt.reshape(N_CTX, D)
    xs = x_sample.reshape(N_LAT, D)

    cond8 = jnp.zeros((SUBLANES, D), F32).at[0].set(c_ctx).at[1:1 + N_LAT_B].set(c)
    mods = _adaln(cond8, ada_w, ada_b).reshape(DEPTH * SUBLANES, 1, 6 * D)

    tabs = _rope_tables()
    lane = np.arange(LANES)
    bd = jnp.asarray((lane[:, None] // HEAD_DIM) == (lane[None, :] // HEAD_DIM), BF16)
    tri = jnp.asarray(np.arange(TB)[:, None] < np.arange(TB)[None, :], BF16)
    c256, s256 = _dft_tables(T_CTX)
    c1k, s1k = _dft_tables(T_LAT)

    def router_operands(l):
        w = jnp.concatenate([router_e_w[l], router_g_w[l]], axis=1).T
        w = jnp.pad(w, ((0, ROUTE_ROWS - w.shape[0]), (0, 0)))
        b = jnp.concatenate([router_e_b[l], router_g_b[l]])
        b = jnp.pad(b, (0, ROUTE_ROWS - b.shape[0]))
        return _split_hi_lo(w), jnp.broadcast_to(b[:, None], (ROUTE_ROWS, LANES))

    qg = jnp.tile(a_q_norm[0], LANES // HEAD_DIM)[None, :]
    kg = jnp.tile(a_k_norm[0], LANES // HEAD_DIM)[None, :]
    q, k, v, u, new_k, new_v = _l0_in(xp, xs, mods, norm_mix[0][None, :], a_w_in[0], qg, kg, bd, tabs)
    sink_b = jnp.broadcast_to(a_sink[0][:, None], (N_HEADS, LANES))
    o_ctx = _ctx_attn(q, k, v, sink_b)
    ck = cache_k[:, 0].reshape(N_LAT_B, PAST, KV_W)
    cv = cache_v[:, 0].reshape(N_LAT_B, PAST, KV_W)
    o_lat = _lat_attn(q, k, v, ck, cv, sink_b)
    x1 = _l0_out(o_ctx, o_lat, u, xp, xs, mods, pool_w[0], pool_scale[0][None, :], a_w_out[0])
    wr, br = router_operands(0)
    (x2,) = _moe(x1, mods, 0, norm_ffn[0][None, :], wr, br, tri, moe_w1, moe_w3, moe_w2, False)

    zc, zs = _l1_in(x2, mods, norm_mix[1][None, :], f_w_in[0], c256, s256)
    x3 = _l1_out(zc, zs, c256, s256, c1k, s1k, x2, mods, f_w_out[0])
    wr, br = router_operands(1)
    yp, ys = _moe(x3, mods, 1, norm_ffn[1][None, :], wr, br, tri, moe_w1, moe_w3, moe_w2, True)

    def cache_entry(t):
        t = t.reshape(N_CTX_B, 1, KV_W // HEAD_DIM, HEAD_DIM, T_CTX)
        return jnp.transpose(t, (0, 1, 4, 2, 3))

    new_k, new_v = cache_entry(new_k), cache_entry(new_v)
    return (yp.reshape(N_CTX_B, T_CTX, D), ys.reshape(N_LAT_B, T_LAT, D), new_k, new_v)
```

```python
import functools

import numpy as np
import jax
import jax.numpy as jnp
from jax import lax
from jax.experimental import pallas as pl
from jax.experimental.pallas import tpu as pltpu

F32 = jnp.float32
BF16 = jnp.bfloat16
I32 = jnp.int32

D = 1024
DEPTH = 2
N_CTX_B, T_CTX = 16, 256
N_LAT_B, T_LAT = 2, 1024
N_CTX = N_CTX_B * T_CTX
N_LAT = N_LAT_B * T_LAT
N_TOK = N_CTX + N_LAT
PAST = 512
GRID_W = 64
HEAD_DIM = 64
N_HEADS = 8
ATTN_W = 512
KV_W = 128
POOL_W = 512
POOL_WINDOWS = (2, 4, 8, 16)
MIX_IN = ATTN_W + 2 * KV_W + POOL_W
WINDOW = 128
N_GROUPS = 4
PER_GROUP = 4
N_EXPERTS = 16
D_EXPERT = 512
ROPE_THETA = 10000.0
EPS = 1e-6
NEG = -1e30

LANES = 128
SUBLANES = 8
TB = 512
NT = N_TOK // TB
NT_CTX = N_CTX // TB
TB_MIX = 1024
TM = 256
PAIRS = 6
N_CLASS = N_GROUPS * PAIRS
CLASS_ROWS = 32
NT_FFN = N_TOK // TM + N_CLASS
P_FFN = NT_FFN * TM
CH = 32
XEXT = D + LANES
ROUTE_ROWS = 32

VMEM_LIMIT = 56 * 1024 * 1024


def _cparams(n_axes=1, vmem=None):
    return pltpu.CompilerParams(dimension_semantics=("arbitrary",) * n_axes,
                                vmem_limit_bytes=vmem)


def _modulate(x, g, shift, scale):
    ms = jnp.mean(x * x, axis=-1, keepdims=True)
    return (x * lax.rsqrt(ms + EPS) * g) * (1.0 + scale) + shift


def _bf16_once(w_ref, wb_ref):
    @pl.when(pl.program_id(0) == 0)
    def _():
        wb_ref[...] = w_ref[...].astype(BF16)


def _mod_row(tile, tiles_ctx, tiles_per_lat):
    return (tile >= tiles_ctx).astype(I32) + (tile >= tiles_ctx + tiles_per_lat).astype(I32)


def _rope_tables():
    t = np.arange(T_LAT)
    row = (t // GRID_W).astype(np.float64)
    col = (t % GRID_W).astype(np.float64)
    nf = HEAD_DIM // 4
    freqs = ROPE_THETA ** (-np.arange(nf, dtype=np.float64) / nf)
    d = np.arange(HEAD_DIM)
    pos = np.where(d[None, :] < HEAD_DIM // 2, row[:, None], col[:, None])
    ang = pos * freqs[d % nf][None, :]
    first = (d % (HEAD_DIM // 2)) < nf
    cos = np.cos(ang)
    sin_a = np.where(first[None, :], -np.sin(ang), 0.0)
    sin_b = np.where(first[None, :], 0.0, np.sin(ang))
    ident = (np.ones((TB, HEAD_DIM)), np.zeros((TB, HEAD_DIM)), np.zeros((TB, HEAD_DIM)))
    out = []
    for tab, idt in zip((cos, sin_a, sin_b), ident):
        full = np.concatenate([tab, idt], axis=0)
        out.append(jnp.asarray(np.tile(full, (1, LANES // HEAD_DIM)), F32))
    return out


def _dft_tables(t):
    m = np.outer(np.arange(t), np.arange(t)) % t
    ang = 2.0 * np.pi * m / t
    s = 1.0 / np.sqrt(t)
    return jnp.asarray(np.cos(ang) * s, F32).astype(BF16), jnp.asarray(np.sin(ang) * s, F32).astype(BF16)


def _adaln_body(cond_ref, w_ref, b_ref, o_ref):
    c = cond_ref[...]
    s = (c * jax.nn.sigmoid(c)).astype(BF16)
    o_ref[0] = jnp.dot(s, w_ref[0].astype(BF16), preferred_element_type=F32) + b_ref[0]


def _adaln(cond8, ada_w, ada_b):
    tn = 1536
    return pl.pallas_call(
        _adaln_body,
        grid=(DEPTH, 6 * D // tn),
        in_specs=[pl.BlockSpec((SUBLANES, D), lambda l, j: (0, 0)),
                  pl.BlockSpec((1, D, tn), lambda l, j: (l, 0, j)),
                  pl.BlockSpec((1, 1, tn), lambda l, j: (l, 0, j))],
        out_specs=pl.BlockSpec((1, SUBLANES, tn), lambda l, j: (l, 0, j)),
        out_shape=jax.ShapeDtypeStruct((DEPTH, SUBLANES, 6 * D), F32),
        compiler_params=_cparams(2),
        name="adaln",
    )(cond8, ada_w, ada_b.reshape(DEPTH, 1, 6 * D))


def _l0_in_body(xp_ref, xs_ref, mod_ref, g_ref, w_ref, qg_ref, kg_ref, bd_ref,
                cos_ref, sa_ref, sb_ref, q_ref, k_ref, v_ref, u_ref, kc_ref, vc_ref, wb_ref):
    i = pl.program_id(0)
    _bf16_once(w_ref, wb_ref)
    x = jnp.where(i < NT_CTX, xp_ref[...], xs_ref[...])
    m = mod_ref[0]
    h = _modulate(x, g_ref[...], m[:, 0:D], m[:, D:2 * D])
    z = jnp.dot(h.astype(BF16), wb_ref[...], preferred_element_type=F32)
    cos, sa, sb, bd = cos_ref[...], sa_ref[...], sb_ref[...], bd_ref[...]

    def head_norm_rope(zz, gain):
        ss = jnp.dot((zz * zz).astype(BF16), bd, preferred_element_type=F32)
        y = zz * lax.rsqrt(ss * (1.0 / HEAD_DIM) + EPS) * gain
        return (y * cos + pltpu.roll(y, LANES - 16, axis=1) * sa
                + pltpu.roll(y, 16, axis=1) * sb)

    for s in range(ATTN_W // LANES):
        qs = head_norm_rope(z[:, s * LANES:(s + 1) * LANES], qg_ref[...])
        q_ref[:, s * LANES:(s + 1) * LANES] = (qs * (HEAD_DIM ** -0.5)).astype(BF16)
    k = head_norm_rope(z[:, ATTN_W:ATTN_W + KV_W], kg_ref[...])
    v = z[:, ATTN_W + KV_W:ATTN_W + 2 * KV_W]
    k_ref[...] = k
    v_ref[...] = v
    u_ref[...] = z[:, ATTN_W + 2 * KV_W:MIX_IN]

    @pl.when(i < NT_CTX)
    def _():
        for q in range(TB // T_CTX):
            kc_ref[q] = k[q * T_CTX:(q + 1) * T_CTX, :].T
            vc_ref[q] = v[q * T_CTX:(q + 1) * T_CTX, :].T


def _l0_in(xp, xs, mods, g, w_in, qg, kg, bd, tabs):
    tab_spec = pl.BlockSpec(
        (TB, LANES), lambda i: (jnp.where(i < NT_CTX, T_LAT // TB, (i - NT_CTX) % (T_LAT // TB)), 0))
    const = lambda shape: pl.BlockSpec(shape, lambda i: (0,) * len(shape))
    return pl.pallas_call(
        _l0_in_body,
        grid=(NT,),
        in_specs=[pl.BlockSpec((TB, D), lambda i: (jnp.minimum(i, NT_CTX - 1), 0)),
                  pl.BlockSpec((TB, D), lambda i: (jnp.maximum(i - NT_CTX, 0), 0)),
                  pl.BlockSpec((1, 1, 6 * D), lambda i: (_mod_row(i, NT_CTX, T_LAT // TB), 0, 0)),
                  const((1, D)), const((D, MIX_IN)), const((1, LANES)), const((1, LANES)),
                  const((LANES, LANES)), tab_spec, tab_spec, tab_spec],
        out_specs=[pl.BlockSpec((TB, ATTN_W), lambda i: (i, 0)),
                   pl.BlockSpec((TB, KV_W), lambda i: (i, 0)),
                   pl.BlockSpec((TB, KV_W), lambda i: (i, 0)),
                   pl.BlockSpec((TB, POOL_W), lambda i: (i, 0)),
                   pl.BlockSpec((TB // T_CTX, KV_W, T_CTX), lambda i: (jnp.minimum(i, NT_CTX - 1), 0, 0)),
                   pl.BlockSpec((TB // T_CTX, KV_W, T_CTX), lambda i: (jnp.minimum(i, NT_CTX - 1), 0, 0))],
        out_shape=[jax.ShapeDtypeStruct((N_TOK, ATTN_W), BF16),
                   jax.ShapeDtypeStruct((N_TOK, KV_W), F32),
                   jax.ShapeDtypeStruct((N_TOK, KV_W), F32),
                   jax.ShapeDtypeStruct((N_TOK, POOL_W), F32),
                   jax.ShapeDtypeStruct((N_CTX_B, KV_W, T_CTX), F32),
                   jax.ShapeDtypeStruct((N_CTX_B, KV_W, T_CTX), F32)],
        scratch_shapes=[pltpu.VMEM((D, MIX_IN), BF16)],
        compiler_params=_cparams(1),
        name="l0_in",
    )(xp, xs, mods, g, w_in, qg, kg, bd, *tabs)


def _head_halves(x):
    z = jnp.zeros_like(x)
    return jnp.concatenate([x, z], axis=1), jnp.concatenate([z, x], axis=1)


_NT_DIMS = (((1,), (1,)), ((), ()))


def _ones_halves(x):
    one = jnp.ones_like(x)
    return jnp.concatenate([x, one], axis=1), jnp.concatenate([one, x], axis=1)


def _sink_attend(scores, values, sk, half):
    mx = sk
    for sc in scores:
        mx = jnp.maximum(mx, jnp.max(sc, axis=-1, keepdims=True))
    acc = None
    for sc, val in zip(scores, values):
        part = jnp.dot(jnp.exp(sc - mx).astype(BF16), val, preferred_element_type=F32)
        acc = part if acc is None else acc + part
    ones_lane = HEAD_DIM * (1 - half)
    den = acc[:, ones_lane:ones_lane + 1] + jnp.exp(sk - mx)
    return acc * (1.0 / den)


def _sink_col(sink_ref, heads, rows):
    return jnp.concatenate([jnp.broadcast_to(sink_ref[h:h + 1, 0:1], (rows, 1)) for h in heads], axis=0)


def _ctx_attn_body(q_ref, k_ref, v_ref, sink_ref, o_ref):
    k = k_ref[...].astype(BF16)
    v = v_ref[...].astype(BF16)
    lo = lax.broadcasted_iota(I32, (T_CTX, LANES), 1) < HEAD_DIM
    for j in range(KV_W // HEAD_DIM):
        kj = k[:, j * HEAD_DIM:(j + 1) * HEAD_DIM]
        vj = v[:, j * HEAD_DIM:(j + 1) * HEAD_DIM]
        k_halves = _head_halves(kj)
        vd = jnp.concatenate([vj, vj], axis=1)
        q2 = jnp.concatenate([q_ref[:, (2 * j) * LANES:(2 * j + 1) * LANES],
                              q_ref[:, (2 * j + 1) * LANES:(2 * j + 2) * LANES]], axis=0)
        outs = []
        for half in range(2):
            sc = lax.dot_general(q2, k_halves[half], _NT_DIMS, preferred_element_type=F32)
            sk = _sink_col(sink_ref, (4 * j + half, 4 * j + 2 + half), T_CTX)
            mx = jnp.maximum(sk, jnp.max(sc, axis=-1, keepdims=True))
            p = jnp.exp(sc - mx)
            inv = 1.0 / (jnp.exp(sk - mx) + jnp.sum(p, axis=-1, keepdims=True))
            outs.append(jnp.dot((p * inv).astype(BF16), vd, preferred_element_type=F32))
        for s2 in range(2):
            rows = slice(s2 * T_CTX, (s2 + 1) * T_CTX)
            o_ref[:, (2 * j + s2) * LANES:(2 * j + s2 + 1) * LANES] = (
                jnp.where(lo, outs[0][rows], outs[1][rows]).astype(BF16))


def _ctx_attn(q, k, v, sink_b):
    return pl.pallas_call(
        _ctx_attn_body,
        grid=(N_CTX_B,),
        in_specs=[pl.BlockSpec((T_CTX, ATTN_W), lambda b: (b, 0)),
                  pl.BlockSpec((T_CTX, KV_W), lambda b: (b, 0)),
                  pl.BlockSpec((T_CTX, KV_W), lambda b: (b, 0)),
                  pl.BlockSpec((SUBLANES, LANES), lambda b: (0, 0))],
        out_specs=pl.BlockSpec((T_CTX, ATTN_W), lambda b: (b, 0)),
        out_shape=jax.ShapeDtypeStruct((N_CTX, ATTN_W), BF16),
        compiler_params=_cparams(1),
        name="ctx_attn",
    )(q, k, v, sink_b)


QB = 128
SPAN = QB + 2 * WINDOW


def _lat_attn_body(q_ref, k_ref, v_ref, ck_ref, cv_ref, sink_ref, o_ref):
    qb = pl.program_id(1)
    start = qb * QB
    kws, vws = [], []
    for c in (-1, 0, 1):
        cs = pl.multiple_of(jnp.clip(start + c * QB, 0, T_LAT - QB), QB)
        kws.append(k_ref[pl.ds(cs, QB), :])
        vws.append(v_ref[pl.ds(cs, QB), :])
    kw = jnp.concatenate(kws, axis=0).astype(BF16)
    vw = jnp.concatenate(vws, axis=0).astype(BF16)
    ck = ck_ref[0].astype(BF16)
    cv = cv_ref[0].astype(BF16)
    qpos = start + (lax.broadcasted_iota(I32, (2 * QB, SPAN), 0) & (QB - 1))
    kpos = start - WINDOW + lax.broadcasted_iota(I32, (2 * QB, SPAN), 1)
    valid = (kpos >= 0) & (kpos < T_LAT) & (jnp.abs(qpos - kpos) <= WINDOW)
    lo = lax.broadcasted_iota(I32, (QB, LANES), 1) < HEAD_DIM
    for j in range(KV_W // HEAD_DIM):
        sl = slice(j * HEAD_DIM, (j + 1) * HEAD_DIM)
        kw_halves = _head_halves(kw[:, sl])
        ck_halves = _head_halves(ck[:, sl])
        vw_halves = _ones_halves(vw[:, sl])
        cv_halves = _ones_halves(cv[:, sl])
        q2 = jnp.concatenate([q_ref[:, (2 * j) * LANES:(2 * j + 1) * LANES],
                              q_ref[:, (2 * j + 1) * LANES:(2 * j + 2) * LANES]], axis=0)
        outs = []
        for half in range(2):
            s_win = lax.dot_general(q2, kw_halves[half], _NT_DIMS, preferred_element_type=F32)
            s_win = jnp.where(valid, s_win, NEG)
            s_ctx = lax.dot_general(q2, ck_halves[half], _NT_DIMS, preferred_element_type=F32)
            sk = _sink_col(sink_ref, (4 * j + half, 4 * j + 2 + half), QB)
            outs.append(_sink_attend([s_win, s_ctx], [vw_halves[half], cv_halves[half]], sk, half))
        for s2 in range(2):
            rows = slice(s2 * QB, (s2 + 1) * QB)
            o_ref[:, (2 * j + s2) * LANES:(2 * j + s2 + 1) * LANES] = (
                jnp.where(lo, outs[0][rows], outs[1][rows]).astype(BF16))


def _lat_attn(q, k, v, ck, cv, sink_b):
    lat0 = N_CTX // T_LAT
    return pl.pallas_call(
        _lat_attn_body,
        grid=(N_LAT_B, T_LAT // QB),
        in_specs=[pl.BlockSpec((QB, ATTN_W), lambda b, i: (N_CTX // QB + b * (T_LAT // QB) + i, 0)),
                  pl.BlockSpec((T_LAT, KV_W), lambda b, i: (lat0 + b, 0)),
                  pl.BlockSpec((T_LAT, KV_W), lambda b, i: (lat0 + b, 0)),
                  pl.BlockSpec((1, PAST, KV_W), lambda b, i: (b, 0, 0)),
                  pl.BlockSpec((1, PAST, KV_W), lambda b, i: (b, 0, 0)),
                  pl.BlockSpec((SUBLANES, LANES), lambda b, i: (0, 0))],
        out_specs=pl.BlockSpec((QB, ATTN_W), lambda b, i: (b * (T_LAT // QB) + i, 0)),
        out_shape=jax.ShapeDtypeStruct((N_LAT, ATTN_W), BF16),
        compiler_params=_cparams(2),
        name="lat_attn",
    )(q, k, v, ck, cv, sink_b)


def _l0_out_body(oc_ref, ol_ref, u_ref, xp_ref, xs_ref, mod_ref, pw_ref, ps_ref, wo_ref, x1_ref, wb_ref):
    i = pl.program_id(0)
    _bf16_once(wo_ref, wb_ref)
    is_ctx = i < N_CTX // TB_MIX
    o = jnp.where(is_ctx, oc_ref[...], ol_ref[...])
    x = jnp.where(is_ctx, xp_ref[...], xs_ref[...])
    tseq = jnp.where(is_ctx, T_CTX, T_LAT)
    pos = lax.broadcasted_iota(I32, (TB_MIX, LANES), 0) & (tseq - 1)
    ys = []
    for g, win in enumerate(POOL_WINDOWS):
        hw = win // 2
        ug = u_ref[:, g * LANES:(g + 1) * LANES]
        acc = ug
        for jj in range(-hw, hw):
            if jj == 0:
                continue
            sh = pltpu.roll(ug, (-jj) % TB_MIX, axis=0)
            ok = (pos + jj >= 0) if jj < 0 else (pos + jj < tseq)
            acc = acc + jnp.where(ok, sh, 0.0)
        cnt = (jnp.minimum(pos + hw, tseq) - jnp.maximum(pos - hw, 0)).astype(F32)
        pooled = acc / cnt - ug
        ys.append(jnp.dot(pooled.astype(BF16), pw_ref[g].astype(BF16), preferred_element_type=F32))
    y = jnp.concatenate(ys, axis=1) * ps_ref[...]
    out = (jnp.dot(o, wb_ref[0:ATTN_W, :], preferred_element_type=F32)
           + jnp.dot(y.astype(BF16), wb_ref[ATTN_W:ATTN_W + POOL_W, :], preferred_element_type=F32))
    x1_ref[...] = x + mod_ref[0][:, 2 * D:3 * D] * out


def _l0_out(o_ctx, o_lat, u, xp, xs, mods, pool_w, pool_scale, w_out):
    ntc = N_CTX // TB_MIX
    const = lambda shape: pl.BlockSpec(shape, lambda i: (0,) * len(shape))
    ctx_map = lambda i: (jnp.minimum(i, ntc - 1), 0)
    lat_map = lambda i: (jnp.maximum(i - ntc, 0), 0)
    return pl.pallas_call(
        _l0_out_body,
        grid=(N_TOK // TB_MIX,),
        in_specs=[pl.BlockSpec((TB_MIX, ATTN_W), ctx_map),
                  pl.BlockSpec((TB_MIX, ATTN_W), lat_map),
                  pl.BlockSpec((TB_MIX, POOL_W), lambda i: (i, 0)),
                  pl.BlockSpec((TB_MIX, D), ctx_map),
                  pl.BlockSpec((TB_MIX, D), lat_map),
                  pl.BlockSpec((1, 1, 6 * D), lambda i: (_mod_row(i, ntc, 1), 0, 0)),
                  const((len(POOL_WINDOWS), LANES, LANES)), const((1, POOL_W)), const((D, D))],
        out_specs=pl.BlockSpec((TB_MIX, D), lambda i: (i, 0)),
        out_shape=jax.ShapeDtypeStruct((N_TOK, D), F32),
        scratch_shapes=[pltpu.VMEM((D, D), BF16)],
        compiler_params=_cparams(1, VMEM_LIMIT),
        name="l0_out",
    )(o_ctx, o_lat, u, xp, xs, mods, pool_w, pool_scale, w_out)


def _first_max(vals):
    best, idx = vals[0], jnp.zeros(vals[0].shape, I32)
    for r in range(1, len(vals)):
        better = vals[r] > best
        idx = jnp.where(better, r, idx)
        best = jnp.where(better, vals[r], best)
    return best, idx


def _softmax_rows(rows):
    mx = functools.reduce(jnp.maximum, rows)
    ex = [jnp.exp(r - mx) for r in rows]
    tot = functools.reduce(lambda a, b: a + b, ex)
    return [e / tot for e in ex]


def _route_body(x_ref, mod_ref, g_ref, wr_ref, br_ref, tri_ref,
                xext_ref, info_ref, cnt_ref, base_ref):
    i = pl.program_id(0)

    @pl.when(i == 0)
    def _():
        base_ref[...] = jnp.zeros_like(base_ref)

    x = x_ref[...]
    m = mod_ref[0]
    h = _modulate(x, g_ref[...], m[:, 3 * D:4 * D], m[:, 4 * D:5 * D])

    hh = h.astype(BF16)
    hl = (h - hh.astype(F32)).astype(BF16)
    wh, wl = wr_ref[0], wr_ref[1]
    lg = (lax.dot_general(wh, hh, _NT_DIMS, preferred_element_type=F32)
          + lax.dot_general(wl, hh, _NT_DIMS, preferred_element_type=F32)
          + lax.dot_general(wh, hl, _NT_DIMS, preferred_element_type=F32)) + br_ref[:, 0:1]

    pg = _softmax_rows([lg[N_EXPERTS + r:N_EXPERTS + r + 1] for r in range(N_GROUPS)])
    pg_top, gi = _first_max(pg)
    le = []
    for j in range(PER_GROUP):
        sel = lg[(N_GROUPS - 1) * PER_GROUP + j:(N_GROUPS - 1) * PER_GROUP + j + 1]
        for g in range(N_GROUPS - 2, -1, -1):
            sel = jnp.where(gi == g, lg[g * PER_GROUP + j:g * PER_GROUP + j + 1], sel)
        le.append(sel)
    pe = _softmax_rows(le)
    p1, i1 = _first_max(pe)
    p2, i2 = _first_max([jnp.where(i1 == j, -1.0, pe[j]) for j in range(PER_GROUP)])
    den = p1 + p2
    w1 = pg_top * p1 / den
    w2 = pg_top * p2 / den

    lo = jnp.minimum(i1, i2)
    hi = jnp.maximum(i1, i2)
    cls = gi * PAIRS + jnp.where(lo == 0, 0, jnp.where(lo == 1, 3, 5)) + hi - lo - 1
    w_lo = jnp.where(i1 == lo, w1, w2)
    w_hi = jnp.where(i1 == lo, w2, w1)

    crow = lax.broadcasted_iota(I32, (CLASS_ROWS, TB), 0)
    hit = crow == cls
    onehot = jnp.where(hit, 1.0, 0.0)
    before = jnp.dot(onehot.astype(BF16), tri_ref[...], preferred_element_type=F32)
    before = before + base_ref[:, 0:1]
    rank = jnp.sum(jnp.where(hit, before, 0.0), axis=0, keepdims=True)
    base_ref[...] = base_ref[...] + jnp.sum(onehot, axis=1, keepdims=True)
    cnt_ref[...] = base_ref[...]

    mod_id = jnp.zeros_like(w1) + _mod_row(i, NT_CTX, T_LAT // TB).astype(F32)
    zero = jnp.zeros_like(w1)
    info_ref[...] = jnp.concatenate([cls.astype(F32), rank, zero, zero, zero, zero, zero, zero], axis=0)
    side = jnp.concatenate([w_lo, w_hi, mod_id, jnp.zeros((LANES - 3, TB), F32)], axis=0).T
    xext_ref[:, 0:D] = x
    xext_ref[:, D:XEXT] = side


def _route(x, mods, layer, g, wr, br, tri):
    const = lambda shape: pl.BlockSpec(shape, lambda i: (0,) * len(shape))
    return pl.pallas_call(
        _route_body,
        grid=(NT,),
        in_specs=[pl.BlockSpec((TB, D), lambda i: (i, 0)),
                  pl.BlockSpec((1, 1, 6 * D),
                               lambda i: (layer * SUBLANES + _mod_row(i, NT_CTX, T_LAT // TB), 0, 0)),
                  const((1, D)), const((2, ROUTE_ROWS, D)), const((ROUTE_ROWS, LANES)),
                  const((TB, TB))],
        out_specs=[pl.BlockSpec((TB, XEXT), lambda i: (i, 0)),
                   pl.BlockSpec((SUBLANES, TB), lambda i: (0, i)),
                   pl.BlockSpec((CLASS_ROWS, LANES), lambda i: (0, 0))],
        out_shape=[jax.ShapeDtypeStruct((N_TOK, XEXT), F32),
                   jax.ShapeDtypeStruct((SUBLANES, N_TOK), F32),
                   jax.ShapeDtypeStruct((CLASS_ROWS, LANES), F32)],
        scratch_shapes=[pltpu.VMEM((CLASS_ROWS, LANES), F32)],
        compiler_params=_cparams(1),
        name=f"route{layer}",
    )(x, mods, g, wr, br, tri)


def _plan(info, counts):
    cls = info[0].astype(I32)
    rank = info[1].astype(I32)
    cnt = counts[:N_CLASS, 0].astype(I32)
    tiles = (cnt + TM - 1) // TM
    tend = jnp.cumsum(tiles)
    tstart = tend - tiles
    n_used = tend[-1]
    cidx = jnp.arange(N_CLASS, dtype=I32)
    pos = jnp.sum(jnp.where(cls[:, None] == cidx, tstart * TM, 0), axis=-1) + rank
    n = jnp.arange(NT_FFN, dtype=I32)
    tile = jnp.minimum(n, jnp.maximum(n_used - 1, 0))
    tcls = jnp.minimum(jnp.sum((tile[:, None] >= tend[None, :]).astype(I32), axis=1), N_CLASS - 1)
    of_cls = tcls[:, None] == cidx
    rows = jnp.sum(jnp.where(of_cls, cnt, 0), axis=1) - (tile - jnp.sum(jnp.where(of_cls, tstart, 0), axis=1)) * TM
    rows = jnp.where(n < n_used, jnp.clip(rows, 0, TM), 0)
    chunks = (rows + CH - 1) // CH
    pair = tcls % PAIRS
    lo = (pair >= 3).astype(I32) + (pair >= 5).astype(I32)
    hi = jnp.where(pair < 3, pair + 1, jnp.where(pair < 5, pair - 1, 3))
    e_lo = (tcls // PAIRS) * PER_GROUP + lo
    e_hi = (tcls // PAIRS) * PER_GROUP + hi
    eidx = jnp.arange(N_EXPERTS, dtype=I32)
    live = n < n_used
    uses = ((e_lo[:, None] == eidx) | (e_hi[:, None] == eidx)) & live[:, None]
    first = jnp.min(jnp.where(uses, n[:, None], NT_FFN), axis=0)
    new_lo = live & (jnp.sum(jnp.where(e_lo[:, None] == eidx, first, 0), axis=1) == n)
    new_hi = live & (jnp.sum(jnp.where(e_hi[:, None] == eidx, first, 0), axis=1) == n)

    def held(new, e):
        last = lax.cummax(jnp.where(new, n, 0))
        return jnp.sum(jnp.where(last[:, None] == n[None, :], e[None, :], 0), axis=1)

    slots = (held(new_lo, e_lo), held(new_hi, e_hi), new_lo.astype(I32), new_hi.astype(I32))
    return pos, lo, hi, slots, n_used.reshape(1), chunks


OCT = TM // SUBLANES


def _ffn_body(pos_ref, lo_ref, hi_ref, sa_ref, sb_ref, newa_ref, newb_ref, nu_ref, ch_ref,
              xext_hbm, mod_ref, g_ref, w1a_ref, w1b_ref, w3a_ref, w3b_ref, w2a_ref, w2b_ref,
              out_hbm, src_ref, dst_ref, xbuf, ybuf, wb1, wb3, wb2, gsem, ssem):
    n = pl.program_id(0)
    n_used = nu_ref[0]

    def gather_copy(tile, s, c, j):
        return pltpu.make_async_copy(
            xext_hbm.at[pl.ds(src_ref[tile * TM + c * CH + j], 1)],
            xbuf.at[s, c * (CH // SUBLANES) + j // SUBLANES, pl.ds(j % SUBLANES, 1)], gsem.at[s])

    def scatter_copy(tile, s, c, j):
        return pltpu.make_async_copy(
            ybuf.at[s, c * (CH // SUBLANES) + j // SUBLANES, pl.ds(j % SUBLANES, 1)],
            out_hbm.at[pl.ds(dst_ref[tile * TM + c * CH + j], 1)], ssem.at[s])

    def start_rows(copy, tile, s):
        def chunk(c, carry):
            for j in range(CH):
                copy(tile, s, c, j).start()
            return carry
        lax.fori_loop(0, ch_ref[tile], chunk, 0)

    def wait_rows(src, dst, sem, tile):
        def chunk(c, carry):
            pltpu.make_async_copy(src, dst, sem).wait()
            return carry
        lax.fori_loop(0, ch_ref[tile], chunk, 0)

    def wait_gather(tile, s):
        rows = xbuf.at[s, pl.ds(0, CH // SUBLANES)]
        wait_rows(rows, rows, gsem.at[s], tile)

    def wait_scatter(tile, s):
        rows = ybuf.at[s, pl.ds(0, CH // SUBLANES)]
        wait_rows(rows, rows, ssem.at[s], tile)

    @pl.when(n == 0)
    def _():
        def pad_rows(tile, carry):
            @pl.when(ch_ref[tile] > 0)
            def _():
                first = tile * TM + (ch_ref[tile] - 1) * CH
                for j in range(CH):
                    src_ref[first + j] = 0
                    dst_ref[first + j] = N_TOK + ((first + j) & (2 * TM - 1))
            return carry

        lax.fori_loop(0, NT_FFN, pad_rows, 0)

        def put(t, carry):
            p = pos_ref[t]
            src_ref[p] = t
            dst_ref[p] = t
            return carry

        lax.fori_loop(0, N_TOK, put, 0, unroll=8)

        xbuf[...] = jnp.zeros_like(xbuf)
        ybuf[...] = jnp.zeros_like(ybuf)
        for s in range(2):
            dumps = [pltpu.make_async_copy(
                ybuf.at[s, q], out_hbm.at[pl.ds(N_TOK + s * TM + q * SUBLANES, SUBLANES)], ssem.at[s])
                for q in range(OCT)]
            for dump in dumps:
                dump.start()
            for dump in dumps:
                dump.wait()
        start_rows(gather_copy, 0, 0)

    def step(slot):
        @pl.when(n + 1 < n_used)
        def _():
            start_rows(gather_copy, n + 1, 1 - slot)

        wait_gather(n, slot)

        @pl.when(n >= 2)
        def _():
            wait_scatter(n - 2, slot)

        e_lo = lo_ref[n]
        e_hi = hi_ref[n]

        @pl.when(newa_ref[n] == 1)
        def _():
            wb1[e_lo] = w1a_ref[0, 0].astype(BF16)
            wb3[e_lo] = w3a_ref[0, 0].astype(BF16)
            wb2[e_lo] = w2a_ref[0, 0].astype(BF16)

        @pl.when(newb_ref[n] == 1)
        def _():
            wb1[e_hi] = w1b_ref[0, 0].astype(BF16)
            wb3[e_hi] = w3b_ref[0, 0].astype(BF16)
            wb2[e_hi] = w2b_ref[0, 0].astype(BF16)

        xe = xbuf[slot].reshape(TM, XEXT)
        x = xe[:, 0:D]
        w_lo = xe[:, D:D + 1]
        w_hi = xe[:, D + 1:D + 2]
        mod_id = xe[:, D + 2:D + 3]

        def pick(lo, hi):
            return jnp.where(mod_id < 0.5, mod_ref[0][:, lo:hi],
                             jnp.where(mod_id < 1.5, mod_ref[1][:, lo:hi], mod_ref[2][:, lo:hi]))

        h = _modulate(x, g_ref[...], pick(3 * D, 4 * D), pick(4 * D, 5 * D)).astype(BF16)

        def act(e, w):
            h1 = jnp.dot(h, wb1[e], preferred_element_type=F32)
            h3 = jnp.dot(h, wb3[e], preferred_element_type=F32)
            return ((h1 * jax.nn.sigmoid(h1)) * h3 * w).astype(BF16)

        y = (jnp.dot(act(e_lo, w_lo), wb2[e_lo], preferred_element_type=F32)
             + jnp.dot(act(e_hi, w_hi), wb2[e_hi], preferred_element_type=F32))
        ybuf[slot] = (x + pick(5 * D, 6 * D) * y).reshape(OCT, SUBLANES, D)
        start_rows(scatter_copy, n, slot)

        @pl.when(n == n_used - 1)
        def _():
            @pl.when(n >= 1)
            def _():
                wait_scatter(n - 1, 1 - slot)
            wait_scatter(n, slot)

    for s in range(2):
        @pl.when((n < n_used) & (n % 2 == s))
        def _():
            step(s)


def _ffn(pos, lo, hi, slots, n_used, chunks, xext, mods, layer, g, w1, w3, w2):
    a_map = lambda n, p, lo, hi, sa, sb, na, nb, nu, ch: (layer, sa[n], 0, 0)
    b_map = lambda n, p, lo, hi, sa, sb, na, nb, nu, ch: (layer, sb[n], 0, 0)
    up = lambda imap: pl.BlockSpec((1, 1, D, D_EXPERT), imap)
    down = lambda imap: pl.BlockSpec((1, 1, D_EXPERT, D), imap)
    return pl.pallas_call(
        _ffn_body,
        grid_spec=pltpu.PrefetchScalarGridSpec(
            num_scalar_prefetch=9, grid=(NT_FFN,),
            in_specs=[pl.BlockSpec(memory_space=pl.ANY),
                      pl.BlockSpec((SUBLANES, 1, 6 * D), lambda n, *_: (layer, 0, 0)),
                      pl.BlockSpec((1, D), lambda n, *_: (0, 0)),
                      up(a_map), up(b_map), up(a_map), up(b_map), down(a_map), down(b_map)],
            out_specs=pl.BlockSpec(memory_space=pl.ANY),
            scratch_shapes=[pltpu.SMEM((P_FFN,), I32), pltpu.SMEM((P_FFN,), I32),
                            pltpu.VMEM((2, OCT, SUBLANES, XEXT), F32),
                            pltpu.VMEM((2, OCT, SUBLANES, D), F32),
                            pltpu.VMEM((PER_GROUP, D, D_EXPERT), BF16),
                            pltpu.VMEM((PER_GROUP, D, D_EXPERT), BF16),
                            pltpu.VMEM((PER_GROUP, D_EXPERT, D), BF16),
                            pltpu.SemaphoreType.DMA((2,)), pltpu.SemaphoreType.DMA((2,))]),
        out_shape=jax.ShapeDtypeStruct((N_TOK + 2 * TM, D), F32),
        compiler_params=_cparams(1, VMEM_LIMIT),
        name=f"ffn{layer}",
    )(pos, lo, hi, *slots, n_used, chunks, xext, mods, g, w1, w1, w3, w3, w2, w2)


def _moe(x, mods, layer, g, wr, br, tri, w1, w3, w2):
    xext, info, counts = _route(x, mods, layer, g, wr, br, tri)
    pos, lo, hi, slots, n_used, chunks = _plan(info, counts)
    return _ffn(pos, lo, hi, slots, n_used, chunks, xext, mods, layer, g, w1, w3, w2)


FG = 256


def _l1_in_body(x_ref, mod_ref, g_ref, w_ref, c_ref, s_ref, zc_ref, zs_ref, wb_ref):
    _bf16_once(w_ref, wb_ref)
    m = mod_ref[0]
    h = _modulate(x_ref[...], g_ref[...], m[:, 0:D], m[:, D:2 * D])
    z = jnp.dot(h.astype(BF16), wb_ref[...], preferred_element_type=F32).astype(BF16)
    for g in range(D // FG):
        zg = z[:, g * FG:(g + 1) * FG]
        zc_ref[:, g * FG:(g + 1) * FG] = jnp.dot(zg, c_ref[...], preferred_element_type=F32).astype(BF16)
        zs_ref[:, g * FG:(g + 1) * FG] = jnp.dot(zg, s_ref[...], preferred_element_type=F32).astype(BF16)


def _l1_in(x, mods, g, w, c256, s256):
    const = lambda shape: pl.BlockSpec(shape, lambda i: (0,) * len(shape))
    return pl.pallas_call(
        _l1_in_body,
        grid=(NT,),
        in_specs=[pl.BlockSpec((TB, D), lambda i: (i, 0)),
                  pl.BlockSpec((1, 1, 6 * D),
                               lambda i: (SUBLANES + _mod_row(i, NT_CTX, T_LAT // TB), 0, 0)),
                  const((1, D)), const((D, D)), const((FG, FG)), const((FG, FG))],
        out_specs=[pl.BlockSpec((TB, D), lambda i: (i, 0)), pl.BlockSpec((TB, D), lambda i: (i, 0))],
        out_shape=[jax.ShapeDtypeStruct((N_TOK, D), BF16), jax.ShapeDtypeStruct((N_TOK, D), BF16)],
        scratch_shapes=[pltpu.VMEM((D, D), BF16)],
        compiler_params=_cparams(1),
        name="l1_in",
    )(x, mods, g, w, c256, s256)


def _l1_out_body(zc_t_ref, zs_t_ref, zc_q_ref, zs_q_ref, c256_ref, s256_ref, c1k_ref, s1k_ref,
                 x_ref, mod_ref, wo_ref, o_ref, f_ref, wb_ref):
    i = pl.program_id(0)
    _bf16_once(wo_ref, wb_ref)

    @pl.when(i < NT_CTX)
    def _():
        for q in range(TB // T_CTX):
            rows = slice(q * T_CTX, (q + 1) * T_CTX)
            f = (jnp.dot(c256_ref[...], zc_t_ref[rows, :], preferred_element_type=F32)
                 - jnp.dot(s256_ref[...], zs_t_ref[rows, :], preferred_element_type=F32))
            f_ref[rows, :] = f.astype(BF16)

    @pl.when(i >= NT_CTX)
    def _():
        f = (jnp.dot(c1k_ref[...], zc_q_ref[...], preferred_element_type=F32)
             - jnp.dot(s1k_ref[...], zs_q_ref[...], preferred_element_type=F32))
        f_ref[...] = f.astype(BF16)

    out = jnp.dot(f_ref[...], wb_ref[...], preferred_element_type=F32)
    o_ref[...] = x_ref[...] + mod_ref[0][:, 2 * D:3 * D] * out


def _l1_out(zc, zs, c256, s256, c1k, s1k, x, mods, w_out):
    const = lambda shape: pl.BlockSpec(shape, lambda i: (0,) * len(shape))
    tile_map = lambda i: (jnp.minimum(i, NT_CTX - 1), 0)
    seq_map = lambda i: (N_CTX // T_LAT + jnp.maximum(i - NT_CTX, 0) // (T_LAT // TB), 0)
    row_map = lambda i: (jnp.maximum(i - NT_CTX, 0) % (T_LAT // TB), 0)
    return pl.pallas_call(
        _l1_out_body,
        grid=(NT,),
        in_specs=[pl.BlockSpec((TB, D), tile_map), pl.BlockSpec((TB, D), tile_map),
                  pl.BlockSpec((T_LAT, D), seq_map), pl.BlockSpec((T_LAT, D), seq_map),
                  const((T_CTX, T_CTX)), const((T_CTX, T_CTX)),
                  pl.BlockSpec((TB, T_LAT), row_map), pl.BlockSpec((TB, T_LAT), row_map),
                  pl.BlockSpec((TB, D), lambda i: (i, 0)),
                  pl.BlockSpec((1, 1, 6 * D),
                               lambda i: (SUBLANES + _mod_row(i, NT_CTX, T_LAT // TB), 0, 0)),
                  const((D, D))],
        out_specs=pl.BlockSpec((TB, D), lambda i: (i, 0)),
        out_shape=jax.ShapeDtypeStruct((N_TOK, D), F32),
        scratch_shapes=[pltpu.VMEM((TB, D), BF16), pltpu.VMEM((D, D), BF16)],
        compiler_params=_cparams(1),
        name="l1_out",
    )(zc, zs, zc, zs, c256, s256, c1k, s1k, x, mods, w_out)


def _split_hi_lo(w):
    hi = w.astype(BF16)
    lo = (w - hi.astype(F32)).astype(BF16)
    return jnp.stack([hi, lo])


def kernel(x_prompt, x_sample, cache_k, cache_v, c, c_ctx, ada_w, ada_b, norm_mix, norm_ffn, a_w_in, a_q_norm, a_k_norm, a_sink, pool_w, pool_scale, a_w_out, f_w_in, f_w_out, router_g_w, router_g_b, router_e_w, router_e_b, moe_w1, moe_w3, moe_w2):
    xp = x_prompt.reshape(N_CTX, D)
    xs = x_sample.reshape(N_LAT, D)

    cond8 = jnp.zeros((SUBLANES, D), F32).at[0].set(c_ctx).at[1:1 + N_LAT_B].set(c)
    mods = _adaln(cond8, ada_w, ada_b).reshape(DEPTH * SUBLANES, 1, 6 * D)

    tabs = _rope_tables()
    lane = np.arange(LANES)
    bd = jnp.asarray((lane[:, None] // HEAD_DIM) == (lane[None, :] // HEAD_DIM), BF16)
    tri = jnp.asarray(np.arange(TB)[:, None] < np.arange(TB)[None, :], BF16)
    c256, s256 = _dft_tables(T_CTX)
    c1k, s1k = _dft_tables(T_LAT)

    def router_operands(l):
        w = jnp.concatenate([router_e_w[l], router_g_w[l]], axis=1).T
        w = jnp.pad(w, ((0, ROUTE_ROWS - w.shape[0]), (0, 0)))
        b = jnp.concatenate([router_e_b[l], router_g_b[l]])
        b = jnp.pad(b, (0, ROUTE_ROWS - b.shape[0]))
        return _split_hi_lo(w), jnp.broadcast_to(b[:, None], (ROUTE_ROWS, LANES))

    qg = jnp.tile(a_q_norm[0], LANES // HEAD_DIM)[None, :]
    kg = jnp.tile(a_k_norm[0], LANES // HEAD_DIM)[None, :]
    q, k, v, u, new_k, new_v = _l0_in(xp, xs, mods, norm_mix[0][None, :], a_w_in[0], qg, kg, bd, tabs)
    sink_b = jnp.broadcast_to(a_sink[0][:, None], (N_HEADS, LANES))
    o_ctx = _ctx_attn(q, k, v, sink_b)
    ck = cache_k[:, 0].reshape(N_LAT_B, PAST, KV_W)
    cv = cache_v[:, 0].reshape(N_LAT_B, PAST, KV_W)
    o_lat = _lat_attn(q, k, v, ck, cv, sink_b)
    x1 = _l0_out(o_ctx, o_lat, u, xp, xs, mods, pool_w[0], pool_scale[0][None, :], a_w_out[0])
    wr, br = router_operands(0)
    x2 = _moe(x1, mods, 0, norm_ffn[0][None, :], wr, br, tri, moe_w1, moe_w3, moe_w2)

    zc, zs = _l1_in(x2, mods, norm_mix[1][None, :], f_w_in[0], c256, s256)
    x3 = _l1_out(zc, zs, c256, s256, c1k, s1k, x2, mods, f_w_out[0])
    wr, br = router_operands(1)
    x4 = _moe(x3, mods, 1, norm_ffn[1][None, :], wr, br, tri, moe_w1, moe_w3, moe_w2)

    def cache_entry(t):
        t = t.reshape(N_CTX_B, 1, KV_W // HEAD_DIM, HEAD_DIM, T_CTX)
        return jnp.transpose(t, (0, 1, 4, 2, 3))

    new_k, new_v = cache_entry(new_k), cache_entry(new_v)
    return (x4[:N_CTX].reshape(N_CTX_B, T_CTX, D), x4[N_CTX:N_TOK].reshape(N_LAT_B, T_LAT, D),
            new_k, new_v)
```

```python
import functools

import numpy as np
import jax
import jax.numpy as jnp
from jax import lax
from jax.experimental import pallas as pl
from jax.experimental.pallas import tpu as pltpu

F32 = jnp.float32
BF16 = jnp.bfloat16
I32 = jnp.int32

D = 1024
DEPTH = 2
N_CTX_B, T_CTX = 16, 256
N_LAT_B, T_LAT = 2, 1024
N_CTX = N_CTX_B * T_CTX
N_LAT = N_LAT_B * T_LAT
N_TOK = N_CTX + N_LAT
PAST = 512
GRID_W = 64
HEAD_DIM = 64
N_HEADS = 8
ATTN_W = 512
KV_W = 128
POOL_W = 512
POOL_WINDOWS = (2, 4, 8, 16)
MIX_IN = ATTN_W + 2 * KV_W + POOL_W
WINDOW = 128
N_GROUPS = 4
PER_GROUP = 4
N_EXPERTS = 16
D_EXPERT = 512
ROPE_THETA = 10000.0
EPS = 1e-6
NEG = -1e30

LANES = 128
SUBLANES = 8
TB = 512
NT = N_TOK // TB
NT_CTX = N_CTX // TB
TB_MIX = 1024
TM = 256
PAIRS = 6
N_CLASS = N_GROUPS * PAIRS
CLASS_ROWS = 32
NT_FFN = N_TOK // TM + N_CLASS
P_FFN = NT_FFN * TM
CH = 32
XEXT = D + LANES
ROUTE_ROWS = 32

VMEM_LIMIT = 56 * 1024 * 1024


def _cparams(n_axes=1, vmem=None):
    return pltpu.CompilerParams(dimension_semantics=("arbitrary",) * n_axes,
                                vmem_limit_bytes=vmem)


def _modulate(x, g, shift, scale):
    ms = jnp.mean(x * x, axis=-1, keepdims=True)
    return (x * lax.rsqrt(ms + EPS) * g) * (1.0 + scale) + shift


def _bf16_once(w_ref, wb_ref):
    @pl.when(pl.program_id(0) == 0)
    def _():
        wb_ref[...] = w_ref[...].astype(BF16)


def _mod_row(tile, tiles_ctx, tiles_per_lat):
    return (tile >= tiles_ctx).astype(I32) + (tile >= tiles_ctx + tiles_per_lat).astype(I32)


def _rope_tables():
    t = np.arange(T_LAT)
    row = (t // GRID_W).astype(np.float64)
    col = (t % GRID_W).astype(np.float64)
    nf = HEAD_DIM // 4
    freqs = ROPE_THETA ** (-np.arange(nf, dtype=np.float64) / nf)
    d = np.arange(HEAD_DIM)
    pos = np.where(d[None, :] < HEAD_DIM // 2, row[:, None], col[:, None])
    ang = pos * freqs[d % nf][None, :]
    first = (d % (HEAD_DIM // 2)) < nf
    cos = np.cos(ang)
    sin_a = np.where(first[None, :], -np.sin(ang), 0.0)
    sin_b = np.where(first[None, :], 0.0, np.sin(ang))
    ident = (np.ones((TB, HEAD_DIM)), np.zeros((TB, HEAD_DIM)), np.zeros((TB, HEAD_DIM)))
    out = []
    for tab, idt in zip((cos, sin_a, sin_b), ident):
        full = np.concatenate([tab, idt], axis=0)
        out.append(jnp.asarray(np.tile(full, (1, LANES // HEAD_DIM)), F32))
    return out


def _dft_tables(t):
    m = np.outer(np.arange(t), np.arange(t)) % t
    ang = 2.0 * np.pi * m / t
    s = 1.0 / np.sqrt(t)
    return jnp.asarray(np.cos(ang) * s, F32).astype(BF16), jnp.asarray(np.sin(ang) * s, F32).astype(BF16)


def _adaln_body(cond_ref, w_ref, b_ref, o_ref):
    c = cond_ref[...]
    s = (c * jax.nn.sigmoid(c)).astype(BF16)
    o_ref[0] = jnp.dot(s, w_ref[0].astype(BF16), preferred_element_type=F32) + b_ref[0]


def _adaln(cond8, ada_w, ada_b):
    tn = 1536
    return pl.pallas_call(
        _adaln_body,
        grid=(DEPTH, 6 * D // tn),
        in_specs=[pl.BlockSpec((SUBLANES, D), lambda l, j: (0, 0)),
                  pl.BlockSpec((1, D, tn), lambda l, j: (l, 0, j)),
                  pl.BlockSpec((1, 1, tn), lambda l, j: (l, 0, j))],
        out_specs=pl.BlockSpec((1, SUBLANES, tn), lambda l, j: (l, 0, j)),
        out_shape=jax.ShapeDtypeStruct((DEPTH, SUBLANES, 6 * D), F32),
        compiler_params=_cparams(2),
        name="adaln",
    )(cond8, ada_w, ada_b.reshape(DEPTH, 1, 6 * D))


def _l0_in_body(xp_ref, xs_ref, mod_ref, g_ref, w_ref, qg_ref, kg_ref, bd_ref,
                cos_ref, sa_ref, sb_ref, q_ref, k_ref, v_ref, u_ref, kc_ref, vc_ref, wb_ref):
    i = pl.program_id(0)
    _bf16_once(w_ref, wb_ref)
    x = jnp.where(i < NT_CTX, xp_ref[...], xs_ref[...])
    m = mod_ref[0]
    h = _modulate(x, g_ref[...], m[:, 0:D], m[:, D:2 * D])
    z = jnp.dot(h.astype(BF16), wb_ref[...], preferred_element_type=F32)
    cos, sa, sb, bd = cos_ref[...], sa_ref[...], sb_ref[...], bd_ref[...]

    def head_norm_rope(zz, gain):
        ss = jnp.dot((zz * zz).astype(BF16), bd, preferred_element_type=F32)
        y = zz * lax.rsqrt(ss * (1.0 / HEAD_DIM) + EPS) * gain
        return (y * cos + pltpu.roll(y, LANES - 16, axis=1) * sa
                + pltpu.roll(y, 16, axis=1) * sb)

    for s in range(ATTN_W // LANES):
        qs = head_norm_rope(z[:, s * LANES:(s + 1) * LANES], qg_ref[...])
        q_ref[:, s * LANES:(s + 1) * LANES] = (qs * (HEAD_DIM ** -0.5)).astype(BF16)
    k = head_norm_rope(z[:, ATTN_W:ATTN_W + KV_W], kg_ref[...])
    v = z[:, ATTN_W + KV_W:ATTN_W + 2 * KV_W]
    k_ref[...] = k
    v_ref[...] = v
    u_ref[...] = z[:, ATTN_W + 2 * KV_W:MIX_IN]

    @pl.when(i < NT_CTX)
    def _():
        for q in range(TB // T_CTX):
            kc_ref[q] = k[q * T_CTX:(q + 1) * T_CTX, :].T
            vc_ref[q] = v[q * T_CTX:(q + 1) * T_CTX, :].T


def _l0_in(xp, xs, mods, g, w_in, qg, kg, bd, tabs):
    tab_spec = pl.BlockSpec(
        (TB, LANES), lambda i: (jnp.where(i < NT_CTX, T_LAT // TB, (i - NT_CTX) % (T_LAT // TB)), 0))
    const = lambda shape: pl.BlockSpec(shape, lambda i: (0,) * len(shape))
    return pl.pallas_call(
        _l0_in_body,
        grid=(NT,),
        in_specs=[pl.BlockSpec((TB, D), lambda i: (jnp.minimum(i, NT_CTX - 1), 0)),
                  pl.BlockSpec((TB, D), lambda i: (jnp.maximum(i - NT_CTX, 0), 0)),
                  pl.BlockSpec((1, 1, 6 * D), lambda i: (_mod_row(i, NT_CTX, T_LAT // TB), 0, 0)),
                  const((1, D)), const((D, MIX_IN)), const((1, LANES)), const((1, LANES)),
                  const((LANES, LANES)), tab_spec, tab_spec, tab_spec],
        out_specs=[pl.BlockSpec((TB, ATTN_W), lambda i: (i, 0)),
                   pl.BlockSpec((TB, KV_W), lambda i: (i, 0)),
                   pl.BlockSpec((TB, KV_W), lambda i: (i, 0)),
                   pl.BlockSpec((TB, POOL_W), lambda i: (i, 0)),
                   pl.BlockSpec((TB // T_CTX, KV_W, T_CTX), lambda i: (jnp.minimum(i, NT_CTX - 1), 0, 0)),
                   pl.BlockSpec((TB // T_CTX, KV_W, T_CTX), lambda i: (jnp.minimum(i, NT_CTX - 1), 0, 0))],
        out_shape=[jax.ShapeDtypeStruct((N_TOK, ATTN_W), BF16),
                   jax.ShapeDtypeStruct((N_TOK, KV_W), F32),
                   jax.ShapeDtypeStruct((N_TOK, KV_W), F32),
                   jax.ShapeDtypeStruct((N_TOK, POOL_W), F32),
                   jax.ShapeDtypeStruct((N_CTX_B, KV_W, T_CTX), F32),
                   jax.ShapeDtypeStruct((N_CTX_B, KV_W, T_CTX), F32)],
        scratch_shapes=[pltpu.VMEM((D, MIX_IN), BF16)],
        compiler_params=_cparams(1),
        name="l0_in",
    )(xp, xs, mods, g, w_in, qg, kg, bd, *tabs)


def _head_halves(x):
    z = jnp.zeros_like(x)
    return jnp.concatenate([x, z], axis=1), jnp.concatenate([z, x], axis=1)


_NT_DIMS = (((1,), (1,)), ((), ()))


def _ones_halves(x):
    one = jnp.ones_like(x)
    return jnp.concatenate([x, one], axis=1), jnp.concatenate([one, x], axis=1)


def _sink_attend(scores, values, sk, half):
    mx = sk
    for sc in scores:
        mx = jnp.maximum(mx, jnp.max(sc, axis=-1, keepdims=True))
    acc = None
    for sc, val in zip(scores, values):
        part = jnp.dot(jnp.exp(sc - mx).astype(BF16), val, preferred_element_type=F32)
        acc = part if acc is None else acc + part
    ones_lane = HEAD_DIM * (1 - half)
    den = acc[:, ones_lane:ones_lane + 1] + jnp.exp(sk - mx)
    return acc * (1.0 / den)


def _sink_col(sink_ref, heads, rows):
    return jnp.concatenate([jnp.broadcast_to(sink_ref[h:h + 1, 0:1], (rows, 1)) for h in heads], axis=0)


def _ctx_attn_body(q_ref, k_ref, v_ref, sink_ref, o_ref):
    k = k_ref[...].astype(BF16)
    v = v_ref[...].astype(BF16)
    lo = lax.broadcasted_iota(I32, (T_CTX, LANES), 1) < HEAD_DIM
    for j in range(KV_W // HEAD_DIM):
        kj = k[:, j * HEAD_DIM:(j + 1) * HEAD_DIM]
        vj = v[:, j * HEAD_DIM:(j + 1) * HEAD_DIM]
        k_halves = _head_halves(kj)
        vd = jnp.concatenate([vj, vj], axis=1)
        q2 = jnp.concatenate([q_ref[:, (2 * j) * LANES:(2 * j + 1) * LANES],
                              q_ref[:, (2 * j + 1) * LANES:(2 * j + 2) * LANES]], axis=0)
        outs = []
        for half in range(2):
            sc = lax.dot_general(q2, k_halves[half], _NT_DIMS, preferred_element_type=F32)
            sk = _sink_col(sink_ref, (4 * j + half, 4 * j + 2 + half), T_CTX)
            mx = jnp.maximum(sk, jnp.max(sc, axis=-1, keepdims=True))
            p = jnp.exp(sc - mx)
            inv = 1.0 / (jnp.exp(sk - mx) + jnp.sum(p, axis=-1, keepdims=True))
            outs.append(jnp.dot((p * inv).astype(BF16), vd, preferred_element_type=F32))
        for s2 in range(2):
            rows = slice(s2 * T_CTX, (s2 + 1) * T_CTX)
            o_ref[:, (2 * j + s2) * LANES:(2 * j + s2 + 1) * LANES] = (
                jnp.where(lo, outs[0][rows], outs[1][rows]).astype(BF16))


def _ctx_attn(q, k, v, sink_b):
    return pl.pallas_call(
        _ctx_attn_body,
        grid=(N_CTX_B,),
        in_specs=[pl.BlockSpec((T_CTX, ATTN_W), lambda b: (b, 0)),
                  pl.BlockSpec((T_CTX, KV_W), lambda b: (b, 0)),
                  pl.BlockSpec((T_CTX, KV_W), lambda b: (b, 0)),
                  pl.BlockSpec((SUBLANES, LANES), lambda b: (0, 0))],
        out_specs=pl.BlockSpec((T_CTX, ATTN_W), lambda b: (b, 0)),
        out_shape=jax.ShapeDtypeStruct((N_CTX, ATTN_W), BF16),
        compiler_params=_cparams(1),
        name="ctx_attn",
    )(q, k, v, sink_b)


QB = 128
SPAN = QB + 2 * WINDOW


def _lat_attn_body(q_ref, k_ref, v_ref, ck_ref, cv_ref, sink_ref, o_ref):
    qb = pl.program_id(1)
    start = qb * QB
    kws, vws = [], []
    for c in (-1, 0, 1):
        cs = pl.multiple_of(jnp.clip(start + c * QB, 0, T_LAT - QB), QB)
        kws.append(k_ref[pl.ds(cs, QB), :])
        vws.append(v_ref[pl.ds(cs, QB), :])
    kw = jnp.concatenate(kws, axis=0).astype(BF16)
    vw = jnp.concatenate(vws, axis=0).astype(BF16)
    ck = ck_ref[0].astype(BF16)
    cv = cv_ref[0].astype(BF16)
    qpos = start + (lax.broadcasted_iota(I32, (2 * QB, SPAN), 0) & (QB - 1))
    kpos = start - WINDOW + lax.broadcasted_iota(I32, (2 * QB, SPAN), 1)
    valid = (kpos >= 0) & (kpos < T_LAT) & (jnp.abs(qpos - kpos) <= WINDOW)
    lo = lax.broadcasted_iota(I32, (QB, LANES), 1) < HEAD_DIM
    for j in range(KV_W // HEAD_DIM):
        sl = slice(j * HEAD_DIM, (j + 1) * HEAD_DIM)
        kw_halves = _head_halves(kw[:, sl])
        ck_halves = _head_halves(ck[:, sl])
        vw_halves = _ones_halves(vw[:, sl])
        cv_halves = _ones_halves(cv[:, sl])
        q2 = jnp.concatenate([q_ref[:, (2 * j) * LANES:(2 * j + 1) * LANES],
                              q_ref[:, (2 * j + 1) * LANES:(2 * j + 2) * LANES]], axis=0)
        outs = []
        for half in range(2):
            s_win = lax.dot_general(q2, kw_halves[half], _NT_DIMS, preferred_element_type=F32)
            s_win = jnp.where(valid, s_win, NEG)
            s_ctx = lax.dot_general(q2, ck_halves[half], _NT_DIMS, preferred_element_type=F32)
            sk = _sink_col(sink_ref, (4 * j + half, 4 * j + 2 + half), QB)
            outs.append(_sink_attend([s_win, s_ctx], [vw_halves[half], cv_halves[half]], sk, half))
        for s2 in range(2):
            rows = slice(s2 * QB, (s2 + 1) * QB)
            o_ref[:, (2 * j + s2) * LANES:(2 * j + s2 + 1) * LANES] = (
                jnp.where(lo, outs[0][rows], outs[1][rows]).astype(BF16))


def _lat_attn(q, k, v, ck, cv, sink_b):
    lat0 = N_CTX // T_LAT
    return pl.pallas_call(
        _lat_attn_body,
        grid=(N_LAT_B, T_LAT // QB),
        in_specs=[pl.BlockSpec((QB, ATTN_W), lambda b, i: (N_CTX // QB + b * (T_LAT // QB) + i, 0)),
                  pl.BlockSpec((T_LAT, KV_W), lambda b, i: (lat0 + b, 0)),
                  pl.BlockSpec((T_LAT, KV_W), lambda b, i: (lat0 + b, 0)),
                  pl.BlockSpec((1, PAST, KV_W), lambda b, i: (b, 0, 0)),
                  pl.BlockSpec((1, PAST, KV_W), lambda b, i: (b, 0, 0)),
                  pl.BlockSpec((SUBLANES, LANES), lambda b, i: (0, 0))],
        out_specs=pl.BlockSpec((QB, ATTN_W), lambda b, i: (b * (T_LAT // QB) + i, 0)),
        out_shape=jax.ShapeDtypeStruct((N_LAT, ATTN_W), BF16),
        compiler_params=_cparams(2),
        name="lat_attn",
    )(q, k, v, ck, cv, sink_b)


def _l0_out_body(oc_ref, ol_ref, u_ref, xp_ref, xs_ref, mod_ref, pw_ref, ps_ref, wo_ref,
                 g_ref, wr_ref, br_ref, tri_ref, xext_ref, info_ref, cnt_ref, wb_ref, base_ref):
    i = pl.program_id(0)
    _bf16_once(wo_ref, wb_ref)
    is_ctx = i < N_CTX // TB_MIX
    o = jnp.where(is_ctx, oc_ref[...], ol_ref[...])
    x = jnp.where(is_ctx, xp_ref[...], xs_ref[...])
    tseq = jnp.where(is_ctx, T_CTX, T_LAT)
    pos = lax.broadcasted_iota(I32, (TB_MIX, LANES), 0) & (tseq - 1)
    ys = []
    for g, win in enumerate(POOL_WINDOWS):
        hw = win // 2
        ug = u_ref[:, g * LANES:(g + 1) * LANES]
        acc = ug
        for jj in range(-hw, hw):
            if jj == 0:
                continue
            sh = pltpu.roll(ug, (-jj) % TB_MIX, axis=0)
            ok = (pos + jj >= 0) if jj < 0 else (pos + jj < tseq)
            acc = acc + jnp.where(ok, sh, 0.0)
        cnt = (jnp.minimum(pos + hw, tseq) - jnp.maximum(pos - hw, 0)).astype(F32)
        pooled = acc / cnt - ug
        ys.append(jnp.dot(pooled.astype(BF16), pw_ref[g].astype(BF16), preferred_element_type=F32))
    y = jnp.concatenate(ys, axis=1) * ps_ref[...]
    out = (jnp.dot(o, wb_ref[0:ATTN_W, :], preferred_element_type=F32)
           + jnp.dot(y.astype(BF16), wb_ref[ATTN_W:ATTN_W + POOL_W, :], preferred_element_type=F32))
    x1 = x + mod_ref[0][:, 2 * D:3 * D] * out
    _route_tile(x1, mod_ref[0], _mod_row(i, N_CTX // TB_MIX, 1), g_ref, wr_ref, br_ref, tri_ref,
                xext_ref, info_ref, cnt_ref, base_ref)


def _l0_out(o_ctx, o_lat, u, xp, xs, mods, pool_w, pool_scale, w_out, route_ops):
    ntc = N_CTX // TB_MIX
    r_in, r_out, r_shape, r_scratch = _route_specs(TB_MIX, lambda i: i)
    const = lambda shape: pl.BlockSpec(shape, lambda i: (0,) * len(shape))
    ctx_map = lambda i: (jnp.minimum(i, ntc - 1), 0)
    lat_map = lambda i: (jnp.maximum(i - ntc, 0), 0)
    return pl.pallas_call(
        _l0_out_body,
        grid=(N_TOK // TB_MIX,),
        in_specs=[pl.BlockSpec((TB_MIX, ATTN_W), ctx_map),
                  pl.BlockSpec((TB_MIX, ATTN_W), lat_map),
                  pl.BlockSpec((TB_MIX, POOL_W), lambda i: (i, 0)),
                  pl.BlockSpec((TB_MIX, D), ctx_map),
                  pl.BlockSpec((TB_MIX, D), lat_map),
                  pl.BlockSpec((1, 1, 6 * D), lambda i: (_mod_row(i, ntc, 1), 0, 0)),
                  const((len(POOL_WINDOWS), LANES, LANES)), const((1, POOL_W)), const((D, D))] + r_in,
        out_specs=r_out,
        out_shape=r_shape,
        scratch_shapes=[pltpu.VMEM((D, D), BF16), r_scratch],
        compiler_params=_cparams(1, VMEM_LIMIT),
        name="l0_out",
    )(o_ctx, o_lat, u, xp, xs, mods, pool_w, pool_scale, w_out, *route_ops)


def _first_max(vals):
    best, idx = vals[0], jnp.zeros(vals[0].shape, I32)
    for r in range(1, len(vals)):
        better = vals[r] > best
        idx = jnp.where(better, r, idx)
        best = jnp.where(better, vals[r], best)
    return best, idx


def _softmax_rows(rows):
    mx = functools.reduce(jnp.maximum, rows)
    ex = [jnp.exp(r - mx) for r in rows]
    tot = functools.reduce(lambda a, b: a + b, ex)
    return [e / tot for e in ex]


def _route_tile(x, m, mod_id, g_ref, wr_ref, br_ref, tri_ref, xext_ref, info_ref, cnt_ref, base_ref):
    t_rows = x.shape[0]

    @pl.when(pl.program_id(0) == 0)
    def _():
        base_ref[...] = jnp.zeros_like(base_ref)

    h = _modulate(x, g_ref[...], m[:, 3 * D:4 * D], m[:, 4 * D:5 * D])

    hh = h.astype(BF16)
    hl = (h - hh.astype(F32)).astype(BF16)
    wh, wl = wr_ref[0], wr_ref[1]
    lg = (lax.dot_general(wh, hh, _NT_DIMS, preferred_element_type=F32)
          + lax.dot_general(wl, hh, _NT_DIMS, preferred_element_type=F32)
          + lax.dot_general(wh, hl, _NT_DIMS, preferred_element_type=F32)) + br_ref[:, 0:1]

    pg = _softmax_rows([lg[N_EXPERTS + r:N_EXPERTS + r + 1] for r in range(N_GROUPS)])
    pg_top, gi = _first_max(pg)
    le = []
    for j in range(PER_GROUP):
        sel = lg[(N_GROUPS - 1) * PER_GROUP + j:(N_GROUPS - 1) * PER_GROUP + j + 1]
        for g in range(N_GROUPS - 2, -1, -1):
            sel = jnp.where(gi == g, lg[g * PER_GROUP + j:g * PER_GROUP + j + 1], sel)
        le.append(sel)
    pe = _softmax_rows(le)
    p1, i1 = _first_max(pe)
    p2, i2 = _first_max([jnp.where(i1 == j, -1.0, pe[j]) for j in range(PER_GROUP)])
    den = p1 + p2
    w1 = pg_top * p1 / den
    w2 = pg_top * p2 / den

    lo = jnp.minimum(i1, i2)
    hi = jnp.maximum(i1, i2)
    cls = gi * PAIRS + jnp.where(lo == 0, 0, jnp.where(lo == 1, 3, 5)) + hi - lo - 1
    w_lo = jnp.where(i1 == lo, w1, w2)
    w_hi = jnp.where(i1 == lo, w2, w1)

    crow = lax.broadcasted_iota(I32, (CLASS_ROWS, t_rows), 0)
    hit = crow == cls
    onehot = jnp.where(hit, 1.0, 0.0)
    before = jnp.dot(onehot.astype(BF16), tri_ref[...], preferred_element_type=F32)
    before = before + base_ref[:, 0:1]
    rank = jnp.sum(jnp.where(hit, before, 0.0), axis=0, keepdims=True)
    base_ref[...] = base_ref[...] + jnp.sum(onehot, axis=1, keepdims=True)
    cnt_ref[...] = base_ref[...]

    mod_id = jnp.zeros_like(w1) + mod_id.astype(F32)
    zero = jnp.zeros_like(w1)
    info_ref[...] = jnp.concatenate([cls.astype(F32), rank, zero, zero, zero, zero, zero, zero], axis=0)
    side = jnp.concatenate([w_lo, w_hi, mod_id, jnp.zeros((LANES - 3, t_rows), F32)], axis=0).T
    xext_ref[:, 0:D] = x
    xext_ref[:, D:XEXT] = side


def _route_specs(tile, step_map):
    const = lambda shape: pl.BlockSpec(shape, lambda *i: (0,) * len(shape))
    in_specs = [const((1, D)), const((2, ROUTE_ROWS, D)), const((ROUTE_ROWS, LANES)), const((tile, tile))]
    out_specs = [pl.BlockSpec((tile, XEXT), lambda *i: (step_map(*i), 0)),
                 pl.BlockSpec((SUBLANES, tile), lambda *i: (0, step_map(*i))),
                 const((CLASS_ROWS, LANES))]
    out_shape = [jax.ShapeDtypeStruct((N_TOK, XEXT), F32),
                 jax.ShapeDtypeStruct((SUBLANES, N_TOK), F32),
                 jax.ShapeDtypeStruct((CLASS_ROWS, LANES), F32)]
    return in_specs, out_specs, out_shape, pltpu.VMEM((CLASS_ROWS, LANES), F32)


def _route_operands(g, wr, br, tile):
    tri = jnp.asarray(np.arange(tile)[:, None] < np.arange(tile)[None, :], BF16)
    return g, wr, br, tri


def _plan(info, counts):
    cls = info[0].astype(I32)
    rank = info[1].astype(I32)
    cnt = counts[:N_CLASS, 0].astype(I32)
    tiles = (cnt + TM - 1) // TM
    tend = jnp.cumsum(tiles)
    tstart = tend - tiles
    n_used = tend[-1]
    cidx = jnp.arange(N_CLASS, dtype=I32)
    pos = jnp.sum(jnp.where(cls[:, None] == cidx, tstart * TM, 0), axis=-1) + rank
    n = jnp.arange(NT_FFN, dtype=I32)
    tile = jnp.minimum(n, jnp.maximum(n_used - 1, 0))
    tcls = jnp.minimum(jnp.sum((tile[:, None] >= tend[None, :]).astype(I32), axis=1), N_CLASS - 1)
    of_cls = tcls[:, None] == cidx
    rows = jnp.sum(jnp.where(of_cls, cnt, 0), axis=1) - (tile - jnp.sum(jnp.where(of_cls, tstart, 0), axis=1)) * TM
    rows = jnp.where(n < n_used, jnp.clip(rows, 0, TM), 0)
    chunks = (rows + CH - 1) // CH
    pair = tcls % PAIRS
    lo = (pair >= 3).astype(I32) + (pair >= 5).astype(I32)
    hi = jnp.where(pair < 3, pair + 1, jnp.where(pair < 5, pair - 1, 3))
    e_lo = (tcls // PAIRS) * PER_GROUP + lo
    e_hi = (tcls // PAIRS) * PER_GROUP + hi
    eidx = jnp.arange(N_EXPERTS, dtype=I32)
    live = n < n_used
    uses = ((e_lo[:, None] == eidx) | (e_hi[:, None] == eidx)) & live[:, None]
    first = jnp.min(jnp.where(uses, n[:, None], NT_FFN), axis=0)
    new_lo = live & (jnp.sum(jnp.where(e_lo[:, None] == eidx, first, 0), axis=1) == n)
    new_hi = live & (jnp.sum(jnp.where(e_hi[:, None] == eidx, first, 0), axis=1) == n)

    def held(new, e):
        last = lax.cummax(jnp.where(new, n, 0))
        return jnp.sum(jnp.where(last[:, None] == n[None, :], e[None, :], 0), axis=1)

    slots = (held(new_lo, e_lo), held(new_hi, e_hi), new_lo.astype(I32), new_hi.astype(I32))
    return pos, lo, hi, slots, n_used.reshape(1), chunks


OCT = TM // SUBLANES


def _ffn_body(pos_ref, lo_ref, hi_ref, sa_ref, sb_ref, newa_ref, newb_ref, nu_ref, ch_ref,
              xext_hbm, mod_ref, g_ref, w1a_ref, w1b_ref, w3a_ref, w3b_ref, w2a_ref, w2b_ref,
              out_hbm, src_ref, dst_ref, xbuf, ybuf, wb1, wb3, wb2, gsem, ssem):
    n = pl.program_id(0)
    n_used = nu_ref[0]

    def gather_copy(tile, s, c, j):
        return pltpu.make_async_copy(
            xext_hbm.at[pl.ds(src_ref[tile * TM + c * CH + j], 1)],
            xbuf.at[s, c * (CH // SUBLANES) + j // SUBLANES, pl.ds(j % SUBLANES, 1)], gsem.at[s])

    def scatter_copy(tile, s, c, j):
        return pltpu.make_async_copy(
            ybuf.at[s, c * (CH // SUBLANES) + j // SUBLANES, pl.ds(j % SUBLANES, 1)],
            out_hbm.at[pl.ds(dst_ref[tile * TM + c * CH + j], 1)], ssem.at[s])

    def start_rows(copy, tile, s):
        def chunk(c, carry):
            for j in range(CH):
                copy(tile, s, c, j).start()
            return carry
        lax.fori_loop(0, ch_ref[tile], chunk, 0)

    def wait_rows(src, dst, sem, tile):
        def chunk(c, carry):
            pltpu.make_async_copy(src, dst, sem).wait()
            return carry
        lax.fori_loop(0, ch_ref[tile], chunk, 0)

    def wait_gather(tile, s):
        rows = xbuf.at[s, pl.ds(0, CH // SUBLANES)]
        wait_rows(rows, rows, gsem.at[s], tile)

    def wait_scatter(tile, s):
        rows = ybuf.at[s, pl.ds(0, CH // SUBLANES)]
        wait_rows(rows, rows, ssem.at[s], tile)

    @pl.when(n == 0)
    def _():
        def pad_rows(tile, carry):
            @pl.when(ch_ref[tile] > 0)
            def _():
                first = tile * TM + (ch_ref[tile] - 1) * CH
                for j in range(CH):
                    src_ref[first + j] = 0
                    dst_ref[first + j] = N_TOK + ((first + j) & (2 * TM - 1))
            return carry

        lax.fori_loop(0, NT_FFN, pad_rows, 0)

        def put(t, carry):
            p = pos_ref[t]
            src_ref[p] = t
            dst_ref[p] = t
            return carry

        lax.fori_loop(0, N_TOK, put, 0, unroll=8)

        xbuf[...] = jnp.zeros_like(xbuf)
        ybuf[...] = jnp.zeros_like(ybuf)
        for s in range(2):
            dumps = [pltpu.make_async_copy(
                ybuf.at[s, q], out_hbm.at[pl.ds(N_TOK + s * TM + q * SUBLANES, SUBLANES)], ssem.at[s])
                for q in range(OCT)]
            for dump in dumps:
                dump.start()
            for dump in dumps:
                dump.wait()
        start_rows(gather_copy, 0, 0)

    def step(slot):
        @pl.when(n + 1 < n_used)
        def _():
            start_rows(gather_copy, n + 1, 1 - slot)

        wait_gather(n, slot)

        @pl.when(n >= 2)
        def _():
            wait_scatter(n - 2, slot)

        e_lo = lo_ref[n]
        e_hi = hi_ref[n]

        @pl.when(newa_ref[n] == 1)
        def _():
            wb1[e_lo] = w1a_ref[0, 0].astype(BF16)
            wb3[e_lo] = w3a_ref[0, 0].astype(BF16)
            wb2[e_lo] = w2a_ref[0, 0].astype(BF16)

        @pl.when(newb_ref[n] == 1)
        def _():
            wb1[e_hi] = w1b_ref[0, 0].astype(BF16)
            wb3[e_hi] = w3b_ref[0, 0].astype(BF16)
            wb2[e_hi] = w2b_ref[0, 0].astype(BF16)

        xe = xbuf[slot].reshape(TM, XEXT)
        x = xe[:, 0:D]
        w_lo = xe[:, D:D + 1]
        w_hi = xe[:, D + 1:D + 2]
        mod_id = xe[:, D + 2:D + 3]

        def pick(lo, hi):
            return jnp.where(mod_id < 0.5, mod_ref[0][:, lo:hi],
                             jnp.where(mod_id < 1.5, mod_ref[1][:, lo:hi], mod_ref[2][:, lo:hi]))

        h = _modulate(x, g_ref[...], pick(3 * D, 4 * D), pick(4 * D, 5 * D)).astype(BF16)

        def act(e, w):
            h1 = jnp.dot(h, wb1[e], preferred_element_type=F32)
            h3 = jnp.dot(h, wb3[e], preferred_element_type=F32)
            return ((h1 * jax.nn.sigmoid(h1)) * h3 * w).astype(BF16)

        y = (jnp.dot(act(e_lo, w_lo), wb2[e_lo], preferred_element_type=F32)
             + jnp.dot(act(e_hi, w_hi), wb2[e_hi], preferred_element_type=F32))
        ybuf[slot] = (x + pick(5 * D, 6 * D) * y).reshape(OCT, SUBLANES, D)
        start_rows(scatter_copy, n, slot)

        @pl.when(n == n_used - 1)
        def _():
            @pl.when(n >= 1)
            def _():
                wait_scatter(n - 1, 1 - slot)
            wait_scatter(n, slot)

    for s in range(2):
        @pl.when((n < n_used) & (n % 2 == s))
        def _():
            step(s)


def _ffn(pos, lo, hi, slots, n_used, chunks, xext, mods, layer, g, w1, w3, w2):
    a_map = lambda n, p, lo, hi, sa, sb, na, nb, nu, ch: (layer, sa[n], 0, 0)
    b_map = lambda n, p, lo, hi, sa, sb, na, nb, nu, ch: (layer, sb[n], 0, 0)
    up = lambda imap: pl.BlockSpec((1, 1, D, D_EXPERT), imap)
    down = lambda imap: pl.BlockSpec((1, 1, D_EXPERT, D), imap)
    return pl.pallas_call(
        _ffn_body,
        grid_spec=pltpu.PrefetchScalarGridSpec(
            num_scalar_prefetch=9, grid=(NT_FFN,),
            in_specs=[pl.BlockSpec(memory_space=pl.ANY),
                      pl.BlockSpec((SUBLANES, 1, 6 * D), lambda n, *_: (layer, 0, 0)),
                      pl.BlockSpec((1, D), lambda n, *_: (0, 0)),
                      up(a_map), up(b_map), up(a_map), up(b_map), down(a_map), down(b_map)],
            out_specs=pl.BlockSpec(memory_space=pl.ANY),
            scratch_shapes=[pltpu.SMEM((P_FFN,), I32), pltpu.SMEM((P_FFN,), I32),
                            pltpu.VMEM((2, OCT, SUBLANES, XEXT), F32),
                            pltpu.VMEM((2, OCT, SUBLANES, D), F32),
                            pltpu.VMEM((PER_GROUP, D, D_EXPERT), BF16),
                            pltpu.VMEM((PER_GROUP, D, D_EXPERT), BF16),
                            pltpu.VMEM((PER_GROUP, D_EXPERT, D), BF16),
                            pltpu.SemaphoreType.DMA((2,)), pltpu.SemaphoreType.DMA((2,))]),
        out_shape=jax.ShapeDtypeStruct((N_TOK + 2 * TM, D), F32),
        compiler_params=_cparams(1, VMEM_LIMIT),
        name=f"ffn{layer}",
    )(pos, lo, hi, *slots, n_used, chunks, xext, mods, g, w1, w1, w3, w3, w2, w2)


def _moe(routed, mods, layer, g, w1, w3, w2):
    xext, info, counts = routed
    pos, lo, hi, slots, n_used, chunks = _plan(info, counts)
    return _ffn(pos, lo, hi, slots, n_used, chunks, xext, mods, layer, g, w1, w3, w2)


FG = 256


def _l1_in_body(x_ref, mod_ref, g_ref, w_ref, c_ref, s_ref, zc_ref, zs_ref, wb_ref):
    _bf16_once(w_ref, wb_ref)
    m = mod_ref[0]
    h = _modulate(x_ref[...], g_ref[...], m[:, 0:D], m[:, D:2 * D])
    z = jnp.dot(h.astype(BF16), wb_ref[...], preferred_element_type=F32).astype(BF16)
    for g in range(D // FG):
        zg = z[:, g * FG:(g + 1) * FG]
        zc_ref[:, g * FG:(g + 1) * FG] = jnp.dot(zg, c_ref[...], preferred_element_type=F32).astype(BF16)
        zs_ref[:, g * FG:(g + 1) * FG] = jnp.dot(zg, s_ref[...], preferred_element_type=F32).astype(BF16)


def _l1_in(x, mods, g, w, c256, s256):
    const = lambda shape: pl.BlockSpec(shape, lambda i: (0,) * len(shape))
    return pl.pallas_call(
        _l1_in_body,
        grid=(NT,),
        in_specs=[pl.BlockSpec((TB, D), lambda i: (i, 0)),
                  pl.BlockSpec((1, 1, 6 * D),
                               lambda i: (SUBLANES + _mod_row(i, NT_CTX, T_LAT // TB), 0, 0)),
                  const((1, D)), const((D, D)), const((FG, FG)), const((FG, FG))],
        out_specs=[pl.BlockSpec((TB, D), lambda i: (i, 0)), pl.BlockSpec((TB, D), lambda i: (i, 0))],
        out_shape=[jax.ShapeDtypeStruct((N_TOK, D), BF16), jax.ShapeDtypeStruct((N_TOK, D), BF16)],
        scratch_shapes=[pltpu.VMEM((D, D), BF16)],
        compiler_params=_cparams(1),
        name="l1_in",
    )(x, mods, g, w, c256, s256)


def _l1_out_body(zc_t_ref, zs_t_ref, zc_q_ref, zs_q_ref, c256_ref, s256_ref, c1k_ref, s1k_ref,
                 x_ref, mod_ref, wo_ref, g_ref, wr_ref, br_ref, tri_ref,
                 xext_ref, info_ref, cnt_ref, f_ref, wb_ref, base_ref):
    i = pl.program_id(0)
    _bf16_once(wo_ref, wb_ref)

    @pl.when(i < NT_CTX)
    def _():
        for q in range(TB // T_CTX):
            rows = slice(q * T_CTX, (q + 1) * T_CTX)
            f = (jnp.dot(c256_ref[...], zc_t_ref[rows, :], preferred_element_type=F32)
                 - jnp.dot(s256_ref[...], zs_t_ref[rows, :], preferred_element_type=F32))
            f_ref[rows, :] = f.astype(BF16)

    @pl.when(i >= NT_CTX)
    def _():
        f = (jnp.dot(c1k_ref[...], zc_q_ref[...], preferred_element_type=F32)
             - jnp.dot(s1k_ref[...], zs_q_ref[...], preferred_element_type=F32))
        f_ref[...] = f.astype(BF16)

    out = jnp.dot(f_ref[...], wb_ref[...], preferred_element_type=F32)
    x3 = x_ref[...] + mod_ref[0][:, 2 * D:3 * D] * out
    _route_tile(x3, mod_ref[0], _mod_row(i, NT_CTX, T_LAT // TB), g_ref, wr_ref, br_ref, tri_ref,
                xext_ref, info_ref, cnt_ref, base_ref)


def _l1_out(zc, zs, c256, s256, c1k, s1k, x, mods, w_out, route_ops):
    const = lambda shape: pl.BlockSpec(shape, lambda i: (0,) * len(shape))
    r_in, r_out, r_shape, r_scratch = _route_specs(TB, lambda i: i)
    tile_map = lambda i: (jnp.minimum(i, NT_CTX - 1), 0)
    seq_map = lambda i: (N_CTX // T_LAT + jnp.maximum(i - NT_CTX, 0) // (T_LAT // TB), 0)
    row_map = lambda i: (jnp.maximum(i - NT_CTX, 0) % (T_LAT // TB), 0)
    return pl.pallas_call(
        _l1_out_body,
        grid=(NT,),
        in_specs=[pl.BlockSpec((TB, D), tile_map), pl.BlockSpec((TB, D), tile_map),
                  pl.BlockSpec((T_LAT, D), seq_map), pl.BlockSpec((T_LAT, D), seq_map),
                  const((T_CTX, T_CTX)), const((T_CTX, T_CTX)),
                  pl.BlockSpec((TB, T_LAT), row_map), pl.BlockSpec((TB, T_LAT), row_map),
                  pl.BlockSpec((TB, D), lambda i: (i, 0)),
                  pl.BlockSpec((1, 1, 6 * D),
                               lambda i: (SUBLANES + _mod_row(i, NT_CTX, T_LAT // TB), 0, 0)),
                  const((D, D))] + r_in,
        out_specs=r_out,
        out_shape=r_shape,
        scratch_shapes=[pltpu.VMEM((TB, D), BF16), pltpu.VMEM((D, D), BF16), r_scratch],
        compiler_params=_cparams(1, VMEM_LIMIT),
        name="l1_out",
    )(zc, zs, zc, zs, c256, s256, c1k, s1k, x, mods, w_out, *route_ops)


def _split_hi_lo(w):
    hi = w.astype(BF16)
    lo = (w - hi.astype(F32)).astype(BF16)
    return jnp.stack([hi, lo])


def kernel(x_prompt, x_sample, cache_k, cache_v, c, c_ctx, ada_w, ada_b, norm_mix, norm_ffn, a_w_in, a_q_norm, a_k_norm, a_sink, pool_w, pool_scale, a_w_out, f_w_in, f_w_out, router_g_w, router_g_b, router_e_w, router_e_b, moe_w1, moe_w3, moe_w2):
    xp = x_prompt.reshape(N_CTX, D)
    xs = x_sample.reshape(N_LAT, D)

    cond8 = jnp.zeros((SUBLANES, D), F32).at[0].set(c_ctx).at[1:1 + N_LAT_B].set(c)
    mods = _adaln(cond8, ada_w, ada_b).reshape(DEPTH * SUBLANES, 1, 6 * D)

    tabs = _rope_tables()
    lane = np.arange(LANES)
    bd = jnp.asarray((lane[:, None] // HEAD_DIM) == (lane[None, :] // HEAD_DIM), BF16)
    c256, s256 = _dft_tables(T_CTX)
    c1k, s1k = _dft_tables(T_LAT)

    def router_operands(l):
        w = jnp.concatenate([router_e_w[l], router_g_w[l]], axis=1).T
        w = jnp.pad(w, ((0, ROUTE_ROWS - w.shape[0]), (0, 0)))
        b = jnp.concatenate([router_e_b[l], router_g_b[l]])
        b = jnp.pad(b, (0, ROUTE_ROWS - b.shape[0]))
        return _split_hi_lo(w), jnp.broadcast_to(b[:, None], (ROUTE_ROWS, LANES))

    qg = jnp.tile(a_q_norm[0], LANES // HEAD_DIM)[None, :]
    kg = jnp.tile(a_k_norm[0], LANES // HEAD_DIM)[None, :]
    q, k, v, u, new_k, new_v = _l0_in(xp, xs, mods, norm_mix[0][None, :], a_w_in[0], qg, kg, bd, tabs)
    sink_b = jnp.broadcast_to(a_sink[0][:, None], (N_HEADS, LANES))
    o_ctx = _ctx_attn(q, k, v, sink_b)
    ck = cache_k[:, 0].reshape(N_LAT_B, PAST, KV_W)
    cv = cache_v[:, 0].reshape(N_LAT_B, PAST, KV_W)
    o_lat = _lat_attn(q, k, v, ck, cv, sink_b)
    routed = _l0_out(o_ctx, o_lat, u, xp, xs, mods, pool_w[0], pool_scale[0][None, :], a_w_out[0],
                     _route_operands(norm_ffn[0][None, :], *router_operands(0), TB_MIX))
    x2 = _moe(routed, mods, 0, norm_ffn[0][None, :], moe_w1, moe_w3, moe_w2)

    zc, zs = _l1_in(x2, mods, norm_mix[1][None, :], f_w_in[0], c256, s256)
    routed = _l1_out(zc, zs, c256, s256, c1k, s1k, x2, mods, f_w_out[0],
                     _route_operands(norm_ffn[1][None, :], *router_operands(1), TB))
    x4 = _moe(routed, mods, 1, norm_ffn[1][None, :], moe_w1, moe_w3, moe_w2)

    def cache_entry(t):
        t = t.reshape(N_CTX_B, 1, KV_W // HEAD_DIM, HEAD_DIM, T_CTX)
        return jnp.transpose(t, (0, 1, 4, 2, 3))

    new_k, new_v = cache_entry(new_k), cache_entry(new_v)
    return (x4[:N_CTX].reshape(N_CTX_B, T_CTX, D), x4[N_CTX:N_TOK].reshape(N_LAT_B, T_LAT, D),
            new_k, new_v)
```

```python
import functools

import numpy as np
import jax
import jax.numpy as jnp
from jax import lax
from jax.experimental import pallas as pl
from jax.experimental.pallas import tpu as pltpu

F32 = jnp.float32
BF16 = jnp.bfloat16
I32 = jnp.int32

D = 1024
DEPTH = 2
N_CTX_B, T_CTX = 16, 256
N_LAT_B, T_LAT = 2, 1024
N_CTX = N_CTX_B * T_CTX
N_LAT = N_LAT_B * T_LAT
N_TOK = N_CTX + N_LAT
PAST = 512
GRID_W = 64
HEAD_DIM = 64
N_HEADS = 8
ATTN_W = 512
KV_W = 128
POOL_W = 512
POOL_WINDOWS = (2, 4, 8, 16)
MIX_IN = ATTN_W + 2 * KV_W + POOL_W
WINDOW = 128
N_GROUPS = 4
PER_GROUP = 4
N_EXPERTS = 16
D_EXPERT = 512
ROPE_THETA = 10000.0
EPS = 1e-6
NEG = -1e30

LANES = 128
SUBLANES = 8
TB = 512
NT = N_TOK // TB
NT_CTX = N_CTX // TB
TB_MIX = 1024
TM = 256
PAIRS = 6
N_CLASS = N_GROUPS * PAIRS
CLASS_ROWS = 32
NT_FFN = N_TOK // TM + N_CLASS
P_FFN = NT_FFN * TM
CH = 32
XEXT = D + LANES
ROUTE_ROWS = 32

VMEM_LIMIT = 56 * 1024 * 1024


def _cparams(n_axes=1, vmem=None):
    return pltpu.CompilerParams(dimension_semantics=("arbitrary",) * n_axes,
                                vmem_limit_bytes=vmem)


def _modulate(x, g, shift, scale):
    ms = jnp.mean(x * x, axis=-1, keepdims=True)
    return (x * lax.rsqrt(ms + EPS) * g) * (1.0 + scale) + shift


def _bf16_once(w_ref, wb_ref):
    @pl.when(pl.program_id(0) == 0)
    def _():
        wb_ref[...] = w_ref[...].astype(BF16)


def _mod_row(tile, tiles_ctx, tiles_per_lat):
    return (tile >= tiles_ctx).astype(I32) + (tile >= tiles_ctx + tiles_per_lat).astype(I32)


def _rope_tables():
    t = np.arange(T_LAT)
    row = (t // GRID_W).astype(np.float64)
    col = (t % GRID_W).astype(np.float64)
    nf = HEAD_DIM // 4
    freqs = ROPE_THETA ** (-np.arange(nf, dtype=np.float64) / nf)
    d = np.arange(HEAD_DIM)
    pos = np.where(d[None, :] < HEAD_DIM // 2, row[:, None], col[:, None])
    ang = pos * freqs[d % nf][None, :]
    first = (d % (HEAD_DIM // 2)) < nf
    cos = np.cos(ang)
    sin_a = np.where(first[None, :], -np.sin(ang), 0.0)
    sin_b = np.where(first[None, :], 0.0, np.sin(ang))
    ident = (np.ones((TB, HEAD_DIM)), np.zeros((TB, HEAD_DIM)), np.zeros((TB, HEAD_DIM)))
    out = []
    for tab, idt in zip((cos, sin_a, sin_b), ident):
        full = np.concatenate([tab, idt], axis=0)
        out.append(jnp.asarray(np.tile(full, (1, LANES // HEAD_DIM)), F32))
    return out


def _dft_tables(t):
    m = np.outer(np.arange(t), np.arange(t)) % t
    ang = 2.0 * np.pi * m / t
    s = 1.0 / np.sqrt(t)
    return jnp.asarray(np.cos(ang) * s, F32).astype(BF16), jnp.asarray(np.sin(ang) * s, F32).astype(BF16)


def _adaln_body(cond_ref, w_ref, b_ref, o_ref):
    c = cond_ref[...]
    s = (c * jax.nn.sigmoid(c)).astype(BF16)
    o_ref[0] = jnp.dot(s, w_ref[0].astype(BF16), preferred_element_type=F32) + b_ref[0]


def _adaln(cond8, ada_w, ada_b):
    tn = 1536
    return pl.pallas_call(
        _adaln_body,
        grid=(DEPTH, 6 * D // tn),
        in_specs=[pl.BlockSpec((SUBLANES, D), lambda l, j: (0, 0)),
                  pl.BlockSpec((1, D, tn), lambda l, j: (l, 0, j)),
                  pl.BlockSpec((1, 1, tn), lambda l, j: (l, 0, j))],
        out_specs=pl.BlockSpec((1, SUBLANES, tn), lambda l, j: (l, 0, j)),
        out_shape=jax.ShapeDtypeStruct((DEPTH, SUBLANES, 6 * D), F32),
        compiler_params=_cparams(2),
        name="adaln",
    )(cond8, ada_w, ada_b.reshape(DEPTH, 1, 6 * D))


def _l0_in_body(xp_ref, xs_ref, mod_ref, g_ref, w_ref, qg_ref, kg_ref, bd_ref,
                cos_ref, sa_ref, sb_ref, q_ref, k_ref, v_ref, u_ref, kc_ref, vc_ref, wb_ref):
    i = pl.program_id(0)
    _bf16_once(w_ref, wb_ref)
    x = jnp.where(i < NT_CTX, xp_ref[...], xs_ref[...])
    m = mod_ref[0]
    h = _modulate(x, g_ref[...], m[:, 0:D], m[:, D:2 * D])
    z = jnp.dot(h.astype(BF16), wb_ref[...], preferred_element_type=F32)
    cos, sa, sb, bd = cos_ref[...], sa_ref[...], sb_ref[...], bd_ref[...]

    def head_norm_rope(zz, gain):
        ss = jnp.dot((zz * zz).astype(BF16), bd, preferred_element_type=F32)
        y = zz * lax.rsqrt(ss * (1.0 / HEAD_DIM) + EPS) * gain
        return (y * cos + pltpu.roll(y, LANES - 16, axis=1) * sa
                + pltpu.roll(y, 16, axis=1) * sb)

    for s in range(ATTN_W // LANES):
        qs = head_norm_rope(z[:, s * LANES:(s + 1) * LANES], qg_ref[...])
        q_ref[:, s * LANES:(s + 1) * LANES] = (qs * (HEAD_DIM ** -0.5)).astype(BF16)
    k = head_norm_rope(z[:, ATTN_W:ATTN_W + KV_W], kg_ref[...])
    v = z[:, ATTN_W + KV_W:ATTN_W + 2 * KV_W]
    k_ref[...] = k
    v_ref[...] = v
    u_ref[...] = z[:, ATTN_W + 2 * KV_W:MIX_IN]

    @pl.when(i < NT_CTX)
    def _():
        for q in range(TB // T_CTX):
            kc_ref[q] = k[q * T_CTX:(q + 1) * T_CTX, :].T
            vc_ref[q] = v[q * T_CTX:(q + 1) * T_CTX, :].T


def _l0_in(xp, xs, mods, g, w_in, qg, kg, bd, tabs):
    tab_spec = pl.BlockSpec(
        (TB, LANES), lambda i: (jnp.where(i < NT_CTX, T_LAT // TB, (i - NT_CTX) % (T_LAT // TB)), 0))
    const = lambda shape: pl.BlockSpec(shape, lambda i: (0,) * len(shape))
    return pl.pallas_call(
        _l0_in_body,
        grid=(NT,),
        in_specs=[pl.BlockSpec((TB, D), lambda i: (jnp.minimum(i, NT_CTX - 1), 0)),
                  pl.BlockSpec((TB, D), lambda i: (jnp.maximum(i - NT_CTX, 0), 0)),
                  pl.BlockSpec((1, 1, 6 * D), lambda i: (_mod_row(i, NT_CTX, T_LAT // TB), 0, 0)),
                  const((1, D)), const((D, MIX_IN)), const((1, LANES)), const((1, LANES)),
                  const((LANES, LANES)), tab_spec, tab_spec, tab_spec],
        out_specs=[pl.BlockSpec((TB, ATTN_W), lambda i: (i, 0)),
                   pl.BlockSpec((TB, KV_W), lambda i: (i, 0)),
                   pl.BlockSpec((TB, KV_W), lambda i: (i, 0)),
                   pl.BlockSpec((TB, POOL_W), lambda i: (i, 0)),
                   pl.BlockSpec((TB // T_CTX, KV_W, T_CTX), lambda i: (jnp.minimum(i, NT_CTX - 1), 0, 0)),
                   pl.BlockSpec((TB // T_CTX, KV_W, T_CTX), lambda i: (jnp.minimum(i, NT_CTX - 1), 0, 0))],
        out_shape=[jax.ShapeDtypeStruct((N_TOK, ATTN_W), BF16),
                   jax.ShapeDtypeStruct((N_TOK, KV_W), F32),
                   jax.ShapeDtypeStruct((N_TOK, KV_W), F32),
                   jax.ShapeDtypeStruct((N_TOK, POOL_W), F32),
                   jax.ShapeDtypeStruct((N_CTX_B, KV_W, T_CTX), F32),
                   jax.ShapeDtypeStruct((N_CTX_B, KV_W, T_CTX), F32)],
        scratch_shapes=[pltpu.VMEM((D, MIX_IN), BF16)],
        compiler_params=_cparams(1),
        name="l0_in",
    )(xp, xs, mods, g, w_in, qg, kg, bd, *tabs)


def _head_halves(x):
    z = jnp.zeros_like(x)
    return jnp.concatenate([x, z], axis=1), jnp.concatenate([z, x], axis=1)


_NT_DIMS = (((1,), (1,)), ((), ()))


def _ones_halves(x):
    one = jnp.ones_like(x)
    return jnp.concatenate([x, one], axis=1), jnp.concatenate([one, x], axis=1)


def _sink_attend(scores, values, sk, half):
    mx = sk
    for sc in scores:
        mx = jnp.maximum(mx, jnp.max(sc, axis=-1, keepdims=True))
    acc = None
    for sc, val in zip(scores, values):
        part = jnp.dot(jnp.exp(sc - mx).astype(BF16), val, preferred_element_type=F32)
        acc = part if acc is None else acc + part
    ones_lane = HEAD_DIM * (1 - half)
    den = acc[:, ones_lane:ones_lane + 1] + jnp.exp(sk - mx)
    return acc * (1.0 / den)


def _sink_col(sink_ref, heads, rows):
    return jnp.concatenate([jnp.broadcast_to(sink_ref[h:h + 1, 0:1], (rows, 1)) for h in heads], axis=0)


def _ctx_attn_body(q_ref, k_ref, v_ref, sink_ref, o_ref):
    k = k_ref[...].astype(BF16)
    v = v_ref[...].astype(BF16)
    lo = lax.broadcasted_iota(I32, (T_CTX, LANES), 1) < HEAD_DIM
    for j in range(KV_W // HEAD_DIM):
        kj = k[:, j * HEAD_DIM:(j + 1) * HEAD_DIM]
        vj = v[:, j * HEAD_DIM:(j + 1) * HEAD_DIM]
        k_halves = _head_halves(kj)
        vd = jnp.concatenate([vj, vj], axis=1)
        q2 = jnp.concatenate([q_ref[:, (2 * j) * LANES:(2 * j + 1) * LANES],
                              q_ref[:, (2 * j + 1) * LANES:(2 * j + 2) * LANES]], axis=0)
        outs = []
        for half in range(2):
            sc = lax.dot_general(q2, k_halves[half], _NT_DIMS, preferred_element_type=F32)
            sk = _sink_col(sink_ref, (4 * j + half, 4 * j + 2 + half), T_CTX)
            mx = jnp.maximum(sk, jnp.max(sc, axis=-1, keepdims=True))
            p = jnp.exp(sc - mx)
            inv = 1.0 / (jnp.exp(sk - mx) + jnp.sum(p, axis=-1, keepdims=True))
            outs.append(jnp.dot((p * inv).astype(BF16), vd, preferred_element_type=F32))
        for s2 in range(2):
            rows = slice(s2 * T_CTX, (s2 + 1) * T_CTX)
            o_ref[:, (2 * j + s2) * LANES:(2 * j + s2 + 1) * LANES] = (
                jnp.where(lo, outs[0][rows], outs[1][rows]).astype(BF16))


def _ctx_attn(q, k, v, sink_b):
    return pl.pallas_call(
        _ctx_attn_body,
        grid=(N_CTX_B,),
        in_specs=[pl.BlockSpec((T_CTX, ATTN_W), lambda b: (b, 0)),
                  pl.BlockSpec((T_CTX, KV_W), lambda b: (b, 0)),
                  pl.BlockSpec((T_CTX, KV_W), lambda b: (b, 0)),
                  pl.BlockSpec((SUBLANES, LANES), lambda b: (0, 0))],
        out_specs=pl.BlockSpec((T_CTX, ATTN_W), lambda b: (b, 0)),
        out_shape=jax.ShapeDtypeStruct((N_CTX, ATTN_W), BF16),
        compiler_params=_cparams(1),
        name="ctx_attn",
    )(q, k, v, sink_b)


QB = 128
SPAN = QB + 2 * WINDOW


def _lat_attn_body(q_ref, k_ref, v_ref, ck_ref, cv_ref, sink_ref, o_ref):
    qb = pl.program_id(1)
    start = qb * QB
    kws, vws = [], []
    for c in (-1, 0, 1):
        cs = pl.multiple_of(jnp.clip(start + c * QB, 0, T_LAT - QB), QB)
        kws.append(k_ref[pl.ds(cs, QB), :])
        vws.append(v_ref[pl.ds(cs, QB), :])
    kw = jnp.concatenate(kws, axis=0).astype(BF16)
    vw = jnp.concatenate(vws, axis=0).astype(BF16)
    ck = ck_ref[0].astype(BF16)
    cv = cv_ref[0].astype(BF16)
    qpos = start + (lax.broadcasted_iota(I32, (2 * QB, SPAN), 0) & (QB - 1))
    kpos = start - WINDOW + lax.broadcasted_iota(I32, (2 * QB, SPAN), 1)
    valid = (kpos >= 0) & (kpos < T_LAT) & (jnp.abs(qpos - kpos) <= WINDOW)
    lo = lax.broadcasted_iota(I32, (QB, LANES), 1) < HEAD_DIM
    for j in range(KV_W // HEAD_DIM):
        sl = slice(j * HEAD_DIM, (j + 1) * HEAD_DIM)
        kw_halves = _head_halves(kw[:, sl])
        ck_halves = _head_halves(ck[:, sl])
        vw_halves = _ones_halves(vw[:, sl])
        cv_halves = _ones_halves(cv[:, sl])
        q2 = jnp.concatenate([q_ref[:, (2 * j) * LANES:(2 * j + 1) * LANES],
                              q_ref[:, (2 * j + 1) * LANES:(2 * j + 2) * LANES]], axis=0)
        outs = []
        for half in range(2):
            s_win = lax.dot_general(q2, kw_halves[half], _NT_DIMS, preferred_element_type=F32)
            s_win = jnp.where(valid, s_win, NEG)
            s_ctx = lax.dot_general(q2, ck_halves[half], _NT_DIMS, preferred_element_type=F32)
            sk = _sink_col(sink_ref, (4 * j + half, 4 * j + 2 + half), QB)
            outs.append(_sink_attend([s_win, s_ctx], [vw_halves[half], cv_halves[half]], sk, half))
        for s2 in range(2):
            rows = slice(s2 * QB, (s2 + 1) * QB)
            o_ref[:, (2 * j + s2) * LANES:(2 * j + s2 + 1) * LANES] = (
                jnp.where(lo, outs[0][rows], outs[1][rows]).astype(BF16))


def _lat_attn(q, k, v, ck, cv, sink_b):
    lat0 = N_CTX // T_LAT
    return pl.pallas_call(
        _lat_attn_body,
        grid=(N_LAT_B, T_LAT // QB),
        in_specs=[pl.BlockSpec((QB, ATTN_W), lambda b, i: (N_CTX // QB + b * (T_LAT // QB) + i, 0)),
                  pl.BlockSpec((T_LAT, KV_W), lambda b, i: (lat0 + b, 0)),
                  pl.BlockSpec((T_LAT, KV_W), lambda b, i: (lat0 + b, 0)),
                  pl.BlockSpec((1, PAST, KV_W), lambda b, i: (b, 0, 0)),
                  pl.BlockSpec((1, PAST, KV_W), lambda b, i: (b, 0, 0)),
                  pl.BlockSpec((SUBLANES, LANES), lambda b, i: (0, 0))],
        out_specs=pl.BlockSpec((QB, ATTN_W), lambda b, i: (b * (T_LAT // QB) + i, 0)),
        out_shape=jax.ShapeDtypeStruct((N_LAT, ATTN_W), BF16),
        compiler_params=_cparams(2),
        name="lat_attn",
    )(q, k, v, ck, cv, sink_b)


def _l0_out_body(oc_ref, ol_ref, u_ref, xp_ref, xs_ref, mod_ref, pw_ref, ps_ref, wo_ref,
                 g_ref, wr_ref, br_ref, tri_ref, xext_ref, info_ref, cnt_ref, wb_ref, base_ref):
    i = pl.program_id(0)
    _bf16_once(wo_ref, wb_ref)
    is_ctx = i < N_CTX // TB_MIX
    o = jnp.where(is_ctx, oc_ref[...], ol_ref[...])
    x = jnp.where(is_ctx, xp_ref[...], xs_ref[...])
    tseq = jnp.where(is_ctx, T_CTX, T_LAT)
    pos = lax.broadcasted_iota(I32, (TB_MIX, LANES), 0) & (tseq - 1)
    ys = []
    for g, win in enumerate(POOL_WINDOWS):
        hw = win // 2
        ug = u_ref[:, g * LANES:(g + 1) * LANES]
        acc = ug
        for jj in range(-hw, hw):
            if jj == 0:
                continue
            sh = pltpu.roll(ug, (-jj) % TB_MIX, axis=0)
            ok = (pos + jj >= 0) if jj < 0 else (pos + jj < tseq)
            acc = acc + jnp.where(ok, sh, 0.0)
        cnt = (jnp.minimum(pos + hw, tseq) - jnp.maximum(pos - hw, 0)).astype(F32)
        pooled = acc / cnt - ug
        ys.append(jnp.dot(pooled.astype(BF16), pw_ref[g].astype(BF16), preferred_element_type=F32))
    y = jnp.concatenate(ys, axis=1) * ps_ref[...]
    out = (jnp.dot(o, wb_ref[0:ATTN_W, :], preferred_element_type=F32)
           + jnp.dot(y.astype(BF16), wb_ref[ATTN_W:ATTN_W + POOL_W, :], preferred_element_type=F32))
    x1 = x + mod_ref[0][:, 2 * D:3 * D] * out
    _route_tile(x1, mod_ref[0], _mod_row(i, N_CTX // TB_MIX, 1), g_ref, wr_ref, br_ref, tri_ref,
                xext_ref, info_ref, cnt_ref, base_ref)


def _l0_out(o_ctx, o_lat, u, xp, xs, mods, pool_w, pool_scale, w_out, route_ops):
    ntc = N_CTX // TB_MIX
    r_in, r_out, r_shape, r_scratch = _route_specs(TB_MIX, lambda i: i)
    const = lambda shape: pl.BlockSpec(shape, lambda i: (0,) * len(shape))
    ctx_map = lambda i: (jnp.minimum(i, ntc - 1), 0)
    lat_map = lambda i: (jnp.maximum(i - ntc, 0), 0)
    return pl.pallas_call(
        _l0_out_body,
        grid=(N_TOK // TB_MIX,),
        in_specs=[pl.BlockSpec((TB_MIX, ATTN_W), ctx_map),
                  pl.BlockSpec((TB_MIX, ATTN_W), lat_map),
                  pl.BlockSpec((TB_MIX, POOL_W), lambda i: (i, 0)),
                  pl.BlockSpec((TB_MIX, D), ctx_map),
                  pl.BlockSpec((TB_MIX, D), lat_map),
                  pl.BlockSpec((1, 1, 6 * D), lambda i: (_mod_row(i, ntc, 1), 0, 0)),
                  const((len(POOL_WINDOWS), LANES, LANES)), const((1, POOL_W)), const((D, D))] + r_in,
        out_specs=r_out,
        out_shape=r_shape,
        scratch_shapes=[pltpu.VMEM((D, D), BF16), r_scratch],
        compiler_params=_cparams(1, VMEM_LIMIT),
        name="l0_out",
    )(o_ctx, o_lat, u, xp, xs, mods, pool_w, pool_scale, w_out, *route_ops)


def _first_max(vals):
    best, idx = vals[0], jnp.zeros(vals[0].shape, I32)
    for r in range(1, len(vals)):
        better = vals[r] > best
        idx = jnp.where(better, r, idx)
        best = jnp.where(better, vals[r], best)
    return best, idx


def _softmax_rows(rows):
    mx = functools.reduce(jnp.maximum, rows)
    ex = [jnp.exp(r - mx) for r in rows]
    tot = functools.reduce(lambda a, b: a + b, ex)
    return [e / tot for e in ex]


def _route_tile(x, m, mod_id, g_ref, wr_ref, br_ref, tri_ref, xext_ref, info_ref, cnt_ref, base_ref):
    t_rows = x.shape[0]

    @pl.when(pl.program_id(0) == 0)
    def _():
        base_ref[...] = jnp.zeros_like(base_ref)

    h = _modulate(x, g_ref[...], m[:, 3 * D:4 * D], m[:, 4 * D:5 * D])

    hh = h.astype(BF16)
    hl = (h - hh.astype(F32)).astype(BF16)
    wh, wl = wr_ref[0], wr_ref[1]
    lg = (lax.dot_general(wh, hh, _NT_DIMS, preferred_element_type=F32)
          + lax.dot_general(wl, hh, _NT_DIMS, preferred_element_type=F32)
          + lax.dot_general(wh, hl, _NT_DIMS, preferred_element_type=F32)) + br_ref[:, 0:1]

    pg = _softmax_rows([lg[N_EXPERTS + r:N_EXPERTS + r + 1] for r in range(N_GROUPS)])
    pg_top, gi = _first_max(pg)
    le = []
    for j in range(PER_GROUP):
        sel = lg[(N_GROUPS - 1) * PER_GROUP + j:(N_GROUPS - 1) * PER_GROUP + j + 1]
        for g in range(N_GROUPS - 2, -1, -1):
            sel = jnp.where(gi == g, lg[g * PER_GROUP + j:g * PER_GROUP + j + 1], sel)
        le.append(sel)
    pe = _softmax_rows(le)
    p1, i1 = _first_max(pe)
    p2, i2 = _first_max([jnp.where(i1 == j, -1.0, pe[j]) for j in range(PER_GROUP)])
    den = p1 + p2
    w1 = pg_top * p1 / den
    w2 = pg_top * p2 / den

    lo = jnp.minimum(i1, i2)
    hi = jnp.maximum(i1, i2)
    cls = gi * PAIRS + jnp.where(lo == 0, 0, jnp.where(lo == 1, 3, 5)) + hi - lo - 1
    w_lo = jnp.where(i1 == lo, w1, w2)
    w_hi = jnp.where(i1 == lo, w2, w1)

    crow = lax.broadcasted_iota(I32, (CLASS_ROWS, t_rows), 0)
    hit = crow == cls
    onehot = jnp.where(hit, 1.0, 0.0)
    before = jnp.dot(onehot.astype(BF16), tri_ref[...], preferred_element_type=F32)
    before = before + base_ref[:, 0:1]
    rank = jnp.sum(jnp.where(hit, before, 0.0), axis=0, keepdims=True)
    base_ref[...] = base_ref[...] + jnp.sum(onehot, axis=1, keepdims=True)
    cnt_ref[...] = base_ref[...]

    mod_id = jnp.zeros_like(w1) + mod_id.astype(F32)
    zero = jnp.zeros_like(w1)
    info_ref[...] = jnp.concatenate([cls.astype(F32), rank, zero, zero, zero, zero, zero, zero], axis=0)
    side = jnp.concatenate([w_lo, w_hi, mod_id, jnp.zeros((LANES - 3, t_rows), F32)], axis=0).T
    xext_ref[:, 0:D] = x
    xext_ref[:, D:XEXT] = side


def _route_specs(tile, step_map):
    const = lambda shape: pl.BlockSpec(shape, lambda *i: (0,) * len(shape))
    in_specs = [const((1, D)), const((2, ROUTE_ROWS, D)), const((ROUTE_ROWS, LANES)), const((tile, tile))]
    out_specs = [pl.BlockSpec((tile, XEXT), lambda *i: (step_map(*i), 0)),
                 pl.BlockSpec((SUBLANES, tile), lambda *i: (0, step_map(*i))),
                 const((CLASS_ROWS, LANES))]
    out_shape = [jax.ShapeDtypeStruct((N_TOK, XEXT), F32),
                 jax.ShapeDtypeStruct((SUBLANES, N_TOK), F32),
                 jax.ShapeDtypeStruct((CLASS_ROWS, LANES), F32)]
    return in_specs, out_specs, out_shape, pltpu.VMEM((CLASS_ROWS, LANES), F32)


def _route_operands(g, wr, br, tile):
    tri = jnp.asarray(np.arange(tile)[:, None] < np.arange(tile)[None, :], BF16)
    return g, wr, br, tri


def _plan_body(cnt_ref, toff_ref, lo_ref, hi_ref, sa_ref, sb_ref, newa_ref, newb_ref, nu_ref, ch_ref,
               seen_ref):
    tm_shift = TM.bit_length() - 1

    def per_class(c, first_tile):
        cnt = cnt_ref[c]
        tiles = lax.shift_right_logical(cnt + (TM - 1), tm_shift)
        toff_ref[c] = first_tile * TM
        group = lax.div(c, PAIRS)
        pair = c - group * PAIRS
        lo = (pair >= 3).astype(I32) + (pair >= 5).astype(I32)
        hi = jnp.where(pair < 3, pair + 1, jnp.where(pair < 5, pair - 1, 3))

        def per_tile(k, carry):
            n = first_tile + k
            lo_ref[n] = lo
            hi_ref[n] = hi
            sa_ref[n] = group * PER_GROUP + lo
            sb_ref[n] = group * PER_GROUP + hi
            rows = jnp.minimum(cnt - k * TM, TM)
            ch_ref[n] = lax.shift_right_logical(rows + (CH - 1), CH.bit_length() - 1)
            return carry

        lax.fori_loop(0, tiles, per_tile, 0)
        return first_tile + tiles

    n_used = lax.fori_loop(0, N_CLASS, per_class, 0)
    nu_ref[0] = n_used

    for e in range(N_EXPERTS):
        seen_ref[e] = 0

    def per_used(n, slots):
        e_lo, e_hi = sa_ref[n], sb_ref[n]
        new_a = seen_ref[e_lo] == 0
        seen_ref[e_lo] = 1
        new_b = seen_ref[e_hi] == 0
        seen_ref[e_hi] = 1
        newa_ref[n] = new_a.astype(I32)
        newb_ref[n] = new_b.astype(I32)
        slot_a = jnp.where(new_a, e_lo, slots[0])
        slot_b = jnp.where(new_b, e_hi, slots[1])
        sa_ref[n] = slot_a
        sb_ref[n] = slot_b
        return slot_a, slot_b

    slot_a, slot_b = lax.fori_loop(0, n_used, per_used, (jnp.int32(0), jnp.int32(0)))

    def unused(n, carry):
        lo_ref[n] = 0
        hi_ref[n] = 0
        sa_ref[n] = slot_a
        sb_ref[n] = slot_b
        newa_ref[n] = 0
        newb_ref[n] = 0
        ch_ref[n] = 0
        return carry

    lax.fori_loop(n_used, NT_FFN, unused, 0)


def _plan(counts, layer):
    smem = pl.BlockSpec(memory_space=pltpu.SMEM)
    per_tile = jax.ShapeDtypeStruct((NT_FFN,), I32)
    return pl.pallas_call(
        _plan_body,
        grid_spec=pltpu.PrefetchScalarGridSpec(
            num_scalar_prefetch=1, grid=(1,), in_specs=[], out_specs=[smem] * 9,
            scratch_shapes=[pltpu.SMEM((N_EXPERTS,), I32)]),
        out_shape=[jax.ShapeDtypeStruct((CLASS_ROWS,), I32)] + [per_tile] * 6
                  + [jax.ShapeDtypeStruct((1,), I32), per_tile],
        compiler_params=_cparams(1),
        name=f"plan{layer}",
    )(counts[:, 0].astype(I32))


OCT = TM // SUBLANES


def _ffn_body(cls_ref, rank_ref, toff_ref, lo_ref, hi_ref, sa_ref, sb_ref, newa_ref, newb_ref, nu_ref, ch_ref,
              xext_hbm, mod_ref, g_ref, w1a_ref, w1b_ref, w3a_ref, w3b_ref, w2a_ref, w2b_ref,
              out_hbm, src_ref, dst_ref, xbuf, ybuf, wb1, wb3, wb2, gsem, ssem):
    n = pl.program_id(0)
    n_used = nu_ref[0]

    def gather_copy(tile, s, c, j):
        return pltpu.make_async_copy(
            xext_hbm.at[pl.ds(src_ref[tile * TM + c * CH + j], 1)],
            xbuf.at[s, c * (CH // SUBLANES) + j // SUBLANES, pl.ds(j % SUBLANES, 1)], gsem.at[s])

    def scatter_copy(tile, s, c, j):
        return pltpu.make_async_copy(
            ybuf.at[s, c * (CH // SUBLANES) + j // SUBLANES, pl.ds(j % SUBLANES, 1)],
            out_hbm.at[pl.ds(dst_ref[tile * TM + c * CH + j], 1)], ssem.at[s])

    def start_rows(copy, tile, s):
        def chunk(c, carry):
            for j in range(CH):
                copy(tile, s, c, j).start()
            return carry
        lax.fori_loop(0, ch_ref[tile], chunk, 0)

    def wait_rows(src, dst, sem, tile):
        def chunk(c, carry):
            pltpu.make_async_copy(src, dst, sem).wait()
            return carry
        lax.fori_loop(0, ch_ref[tile], chunk, 0)

    def wait_gather(tile, s):
        rows = xbuf.at[s, pl.ds(0, CH // SUBLANES)]
        wait_rows(rows, rows, gsem.at[s], tile)

    def wait_scatter(tile, s):
        rows = ybuf.at[s, pl.ds(0, CH // SUBLANES)]
        wait_rows(rows, rows, ssem.at[s], tile)

    @pl.when(n == 0)
    def _():
        def pad_rows(tile, carry):
            @pl.when(ch_ref[tile] > 0)
            def _():
                first = tile * TM + (ch_ref[tile] - 1) * CH
                for j in range(CH):
                    src_ref[first + j] = 0
                    dst_ref[first + j] = N_TOK + ((first + j) & (2 * TM - 1))
            return carry

        lax.fori_loop(0, NT_FFN, pad_rows, 0)

        def put(t, carry):
            p = toff_ref[cls_ref[t]] + rank_ref[t]
            src_ref[p] = t
            dst_ref[p] = t
            return carry

        lax.fori_loop(0, N_TOK, put, 0, unroll=8)

        xbuf[...] = jnp.zeros_like(xbuf)
        ybuf[...] = jnp.zeros_like(ybuf)
        for s in range(2):
            dumps = [pltpu.make_async_copy(
                ybuf.at[s, q], out_hbm.at[pl.ds(N_TOK + s * TM + q * SUBLANES, SUBLANES)], ssem.at[s])
                for q in range(OCT)]
            for dump in dumps:
                dump.start()
            for dump in dumps:
                dump.wait()
        start_rows(gather_copy, 0, 0)

    def step(slot):
        @pl.when(n + 1 < n_used)
        def _():
            start_rows(gather_copy, n + 1, 1 - slot)

        wait_gather(n, slot)

        @pl.when(n >= 2)
        def _():
            wait_scatter(n - 2, slot)

        e_lo = lo_ref[n]
        e_hi = hi_ref[n]

        @pl.when(newa_ref[n] == 1)
        def _():
            wb1[e_lo] = w1a_ref[0, 0].astype(BF16)
            wb3[e_lo] = w3a_ref[0, 0].astype(BF16)
            wb2[e_lo] = w2a_ref[0, 0].astype(BF16)

        @pl.when(newb_ref[n] == 1)
        def _():
            wb1[e_hi] = w1b_ref[0, 0].astype(BF16)
            wb3[e_hi] = w3b_ref[0, 0].astype(BF16)
            wb2[e_hi] = w2b_ref[0, 0].astype(BF16)

        xe = xbuf[slot].reshape(TM, XEXT)
        x = xe[:, 0:D]
        w_lo = xe[:, D:D + 1]
        w_hi = xe[:, D + 1:D + 2]
        mod_id = xe[:, D + 2:D + 3]

        def pick(lo, hi):
            return jnp.where(mod_id < 0.5, mod_ref[0][:, lo:hi],
                             jnp.where(mod_id < 1.5, mod_ref[1][:, lo:hi], mod_ref[2][:, lo:hi]))

        h = _modulate(x, g_ref[...], pick(3 * D, 4 * D), pick(4 * D, 5 * D)).astype(BF16)

        def act(e, w):
            h1 = jnp.dot(h, wb1[e], preferred_element_type=F32)
            h3 = jnp.dot(h, wb3[e], preferred_element_type=F32)
            return ((h1 * jax.nn.sigmoid(h1)) * h3 * w).astype(BF16)

        y = (jnp.dot(act(e_lo, w_lo), wb2[e_lo], preferred_element_type=F32)
             + jnp.dot(act(e_hi, w_hi), wb2[e_hi], preferred_element_type=F32))
        ybuf[slot] = (x + pick(5 * D, 6 * D) * y).reshape(OCT, SUBLANES, D)
        start_rows(scatter_copy, n, slot)

        @pl.when(n == n_used - 1)
        def _():
            @pl.when(n >= 1)
            def _():
                wait_scatter(n - 1, 1 - slot)
            wait_scatter(n, slot)

    for s in range(2):
        @pl.when((n < n_used) & (n % 2 == s))
        def _():
            step(s)


def _ffn(cls, rank, plan, xext, mods, layer, g, w1, w3, w2):
    a_map = lambda n, cl, rk, to, lo, hi, sa, *_: (layer, sa[n], 0, 0)
    b_map = lambda n, cl, rk, to, lo, hi, sa, sb, *_: (layer, sb[n], 0, 0)
    up = lambda imap: pl.BlockSpec((1, 1, D, D_EXPERT), imap)
    down = lambda imap: pl.BlockSpec((1, 1, D_EXPERT, D), imap)
    return pl.pallas_call(
        _ffn_body,
        grid_spec=pltpu.PrefetchScalarGridSpec(
            num_scalar_prefetch=11, grid=(NT_FFN,),
            in_specs=[pl.BlockSpec(memory_space=pl.ANY),
                      pl.BlockSpec((SUBLANES, 1, 6 * D), lambda n, *_: (layer, 0, 0)),
                      pl.BlockSpec((1, D), lambda n, *_: (0, 0)),
                      up(a_map), up(b_map), up(a_map), up(b_map), down(a_map), down(b_map)],
            out_specs=pl.BlockSpec(memory_space=pl.ANY),
            scratch_shapes=[pltpu.SMEM((P_FFN,), I32), pltpu.SMEM((P_FFN,), I32),
                            pltpu.VMEM((2, OCT, SUBLANES, XEXT), F32),
                            pltpu.VMEM((2, OCT, SUBLANES, D), F32),
                            pltpu.VMEM((PER_GROUP, D, D_EXPERT), BF16),
                            pltpu.VMEM((PER_GROUP, D, D_EXPERT), BF16),
                            pltpu.VMEM((PER_GROUP, D_EXPERT, D), BF16),
                            pltpu.SemaphoreType.DMA((2,)), pltpu.SemaphoreType.DMA((2,))]),
        out_shape=jax.ShapeDtypeStruct((N_TOK + 2 * TM, D), F32),
        compiler_params=_cparams(1, VMEM_LIMIT),
        name=f"ffn{layer}",
    )(cls, rank, *plan, xext, mods, g, w1, w1, w3, w3, w2, w2)


def _moe(routed, mods, layer, g, w1, w3, w2):
    xext, info, counts = routed
    return _ffn(info[0].astype(I32), info[1].astype(I32), _plan(counts, layer),
                xext, mods, layer, g, w1, w3, w2)


FG = 256


def _l1_in_body(x_ref, mod_ref, g_ref, w_ref, c_ref, s_ref, zc_ref, zs_ref, wb_ref):
    _bf16_once(w_ref, wb_ref)
    m = mod_ref[0]
    h = _modulate(x_ref[...], g_ref[...], m[:, 0:D], m[:, D:2 * D])
    z = jnp.dot(h.astype(BF16), wb_ref[...], preferred_element_type=F32).astype(BF16)
    for g in range(D // FG):
        zg = z[:, g * FG:(g + 1) * FG]
        zc_ref[:, g * FG:(g + 1) * FG] = jnp.dot(zg, c_ref[...], preferred_element_type=F32).astype(BF16)
        zs_ref[:, g * FG:(g + 1) * FG] = jnp.dot(zg, s_ref[...], preferred_element_type=F32).astype(BF16)


def _l1_in(x, mods, g, w, c256, s256):
    const = lambda shape: pl.BlockSpec(shape, lambda i: (0,) * len(shape))
    return pl.pallas_call(
        _l1_in_body,
        grid=(NT,),
        in_specs=[pl.BlockSpec((TB, D), lambda i: (i, 0)),
                  pl.BlockSpec((1, 1, 6 * D),
                               lambda i: (SUBLANES + _mod_row(i, NT_CTX, T_LAT // TB), 0, 0)),
                  const((1, D)), const((D, D)), const((FG, FG)), const((FG, FG))],
        out_specs=[pl.BlockSpec((TB, D), lambda i: (i, 0)), pl.BlockSpec((TB, D), lambda i: (i, 0))],
        out_shape=[jax.ShapeDtypeStruct((N_TOK, D), BF16), jax.ShapeDtypeStruct((N_TOK, D), BF16)],
        scratch_shapes=[pltpu.VMEM((D, D), BF16)],
        compiler_params=_cparams(1),
        name="l1_in",
    )(x, mods, g, w, c256, s256)


def _l1_out_body(zc_t_ref, zs_t_ref, zc_q_ref, zs_q_ref, c256_ref, s256_ref, c1k_ref, s1k_ref,
                 x_ref, mod_ref, wo_ref, g_ref, wr_ref, br_ref, tri_ref,
                 xext_ref, info_ref, cnt_ref, f_ref, wb_ref, base_ref):
    i = pl.program_id(0)
    _bf16_once(wo_ref, wb_ref)

    @pl.when(i < NT_CTX)
    def _():
        for q in range(TB // T_CTX):
            rows = slice(q * T_CTX, (q + 1) * T_CTX)
            f = (jnp.dot(c256_ref[...], zc_t_ref[rows, :], preferred_element_type=F32)
                 - jnp.dot(s256_ref[...], zs_t_ref[rows, :], preferred_element_type=F32))
            f_ref[rows, :] = f.astype(BF16)

    @pl.when(i >= NT_CTX)
    def _():
        f = (jnp.dot(c1k_ref[...], zc_q_ref[...], preferred_element_type=F32)
             - jnp.dot(s1k_ref[...], zs_q_ref[...], preferred_element_type=F32))
        f_ref[...] = f.astype(BF16)

    out = jnp.dot(f_ref[...], wb_ref[...], preferred_element_type=F32)
    x3 = x_ref[...] + mod_ref[0][:, 2 * D:3 * D] * out
    _route_tile(x3, mod_ref[0], _mod_row(i, NT_CTX, T_LAT // TB), g_ref, wr_ref, br_ref, tri_ref,
                xext_ref, info_ref, cnt_ref, base_ref)


def _l1_out(zc, zs, c256, s256, c1k, s1k, x, mods, w_out, route_ops):
    const = lambda shape: pl.BlockSpec(shape, lambda i: (0,) * len(shape))
    r_in, r_out, r_shape, r_scratch = _route_specs(TB, lambda i: i)
    tile_map = lambda i: (jnp.minimum(i, NT_CTX - 1), 0)
    seq_map = lambda i: (N_CTX // T_LAT + jnp.maximum(i - NT_CTX, 0) // (T_LAT // TB), 0)
    row_map = lambda i: (jnp.maximum(i - NT_CTX, 0) % (T_LAT // TB), 0)
    return pl.pallas_call(
        _l1_out_body,
        grid=(NT,),
        in_specs=[pl.BlockSpec((TB, D), tile_map), pl.BlockSpec((TB, D), tile_map),
                  pl.BlockSpec((T_LAT, D), seq_map), pl.BlockSpec((T_LAT, D), seq_map),
                  const((T_CTX, T_CTX)), const((T_CTX, T_CTX)),
                  pl.BlockSpec((TB, T_LAT), row_map), pl.BlockSpec((TB, T_LAT), row_map),
                  pl.BlockSpec((TB, D), lambda i: (i, 0)),
                  pl.BlockSpec((1, 1, 6 * D),
                               lambda i: (SUBLANES + _mod_row(i, NT_CTX, T_LAT // TB), 0, 0)),
                  const((D, D))] + r_in,
        out_specs=r_out,
        out_shape=r_shape,
        scratch_shapes=[pltpu.VMEM((TB, D), BF16), pltpu.VMEM((D, D), BF16), r_scratch],
        compiler_params=_cparams(1, VMEM_LIMIT),
        name="l1_out",
    )(zc, zs, zc, zs, c256, s256, c1k, s1k, x, mods, w_out, *route_ops)


def _split_hi_lo(w):
    hi = w.astype(BF16)
    lo = (w - hi.astype(F32)).astype(BF16)
    return jnp.stack([hi, lo])


def kernel(x_prompt, x_sample, cache_k, cache_v, c, c_ctx, ada_w, ada_b, norm_mix, norm_ffn, a_w_in, a_q_norm, a_k_norm, a_sink, pool_w, pool_scale, a_w_out, f_w_in, f_w_out, router_g_w, router_g_b, router_e_w, router_e_b, moe_w1, moe_w3, moe_w2):
    xp = x_prompt.reshape(N_CTX, D)
    xs = x_sample.reshape(N_LAT, D)

    cond8 = jnp.zeros((SUBLANES, D), F32).at[0].set(c_ctx).at[1:1 + N_LAT_B].set(c)
    mods = _adaln(cond8, ada_w, ada_b).reshape(DEPTH * SUBLANES, 1, 6 * D)

    tabs = _rope_tables()
    lane = np.arange(LANES)
    bd = jnp.asarray((lane[:, None] // HEAD_DIM) == (lane[None, :] // HEAD_DIM), BF16)
    c256, s256 = _dft_tables(T_CTX)
    c1k, s1k = _dft_tables(T_LAT)

    def router_operands(l):
        w = jnp.concatenate([router_e_w[l], router_g_w[l]], axis=1).T
        w = jnp.pad(w, ((0, ROUTE_ROWS - w.shape[0]), (0, 0)))
        b = jnp.concatenate([router_e_b[l], router_g_b[l]])
        b = jnp.pad(b, (0, ROUTE_ROWS - b.shape[0]))
        return _split_hi_lo(w), jnp.broadcast_to(b[:, None], (ROUTE_ROWS, LANES))

    qg = jnp.tile(a_q_norm[0], LANES // HEAD_DIM)[None, :]
    kg = jnp.tile(a_k_norm[0], LANES // HEAD_DIM)[None, :]
    q, k, v, u, new_k, new_v = _l0_in(xp, xs, mods, norm_mix[0][None, :], a_w_in[0], qg, kg, bd, tabs)
    sink_b = jnp.broadcast_to(a_sink[0][:, None], (N_HEADS, LANES))
    o_ctx = _ctx_attn(q, k, v, sink_b)
    ck = cache_k[:, 0].reshape(N_LAT_B, PAST, KV_W)
    cv = cache_v[:, 0].reshape(N_LAT_B, PAST, KV_W)
    o_lat = _lat_attn(q, k, v, ck, cv, sink_b)
    routed = _l0_out(o_ctx, o_lat, u, xp, xs, mods, pool_w[0], pool_scale[0][None, :], a_w_out[0],
                     _route_operands(norm_ffn[0][None, :], *router_operands(0), TB_MIX))
    x2 = _moe(routed, mods, 0, norm_ffn[0][None, :], moe_w1, moe_w3, moe_w2)

    zc, zs = _l1_in(x2, mods, norm_mix[1][None, :], f_w_in[0], c256, s256)
    routed = _l1_out(zc, zs, c256, s256, c1k, s1k, x2, mods, f_w_out[0],
                     _route_operands(norm_ffn[1][None, :], *router_operands(1), TB))
    x4 = _moe(routed, mods, 1, norm_ffn[1][None, :], moe_w1, moe_w3, moe_w2)

    def cache_entry(t):
        t = t.reshape(N_CTX_B, 1, KV_W // HEAD_DIM, HEAD_DIM, T_CTX)
        return jnp.transpose(t, (0, 1, 4, 2, 3))

    new_k, new_v = cache_entry(new_k), cache_entry(new_v)
    return (x4[:N_CTX].reshape(N_CTX_B, T_CTX, D), x4[N_CTX:N_TOK].reshape(N_LAT_B, T_LAT, D),
            new_k, new_v)
```

```python
import functools

import numpy as np
import jax
import jax.numpy as jnp
from jax import lax
from jax.experimental import pallas as pl
from jax.experimental.pallas import tpu as pltpu

F32 = jnp.float32
BF16 = jnp.bfloat16
I32 = jnp.int32

D = 1024
DEPTH = 2
N_CTX_B, T_CTX = 16, 256
N_LAT_B, T_LAT = 2, 1024
N_CTX = N_CTX_B * T_CTX
N_LAT = N_LAT_B * T_LAT
N_TOK = N_CTX + N_LAT
PAST = 512
GRID_W = 64
HEAD_DIM = 64
N_HEADS = 8
ATTN_W = 512
KV_W = 128
POOL_W = 512
POOL_WINDOWS = (2, 4, 8, 16)
MIX_IN = ATTN_W + 2 * KV_W + POOL_W
WINDOW = 128
N_GROUPS = 4
PER_GROUP = 4
N_EXPERTS = 16
D_EXPERT = 512
ROPE_THETA = 10000.0
EPS = 1e-6
NEG = -1e30

LANES = 128
SUBLANES = 8
TB = 512
NT = N_TOK // TB
NT_CTX = N_CTX // TB
TB_MIX = 1024
TM = 256
PAIRS = 6
N_CLASS = N_GROUPS * PAIRS
CLASS_ROWS = 32
NT_FFN = N_TOK // TM + N_CLASS
P_FFN = NT_FFN * TM
CH = 32
XEXT = D + LANES
ROUTE_ROWS = 32

VMEM_LIMIT = 56 * 1024 * 1024


def _cparams(n_axes=1, vmem=None):
    return pltpu.CompilerParams(dimension_semantics=("arbitrary",) * n_axes,
                                vmem_limit_bytes=vmem)


def _modulate(x, g, shift, scale):
    ms = jnp.mean(x * x, axis=-1, keepdims=True)
    return (x * lax.rsqrt(ms + EPS) * g) * (1.0 + scale) + shift


def _bf16_once(w_ref, wb_ref):
    @pl.when(pl.program_id(0) == 0)
    def _():
        wb_ref[...] = w_ref[...].astype(BF16)


def _mod_row(tile, tiles_ctx, tiles_per_lat):
    return (tile >= tiles_ctx).astype(I32) + (tile >= tiles_ctx + tiles_per_lat).astype(I32)


def _rope_tables():
    t = np.arange(T_LAT)
    row = (t // GRID_W).astype(np.float64)
    col = (t % GRID_W).astype(np.float64)
    nf = HEAD_DIM // 4
    freqs = ROPE_THETA ** (-np.arange(nf, dtype=np.float64) / nf)
    d = np.arange(HEAD_DIM)
    pos = np.where(d[None, :] < HEAD_DIM // 2, row[:, None], col[:, None])
    ang = pos * freqs[d % nf][None, :]
    first = (d % (HEAD_DIM // 2)) < nf
    cos = np.cos(ang)
    sin_a = np.where(first[None, :], -np.sin(ang), 0.0)
    sin_b = np.where(first[None, :], 0.0, np.sin(ang))
    ident = (np.ones((TB, HEAD_DIM)), np.zeros((TB, HEAD_DIM)), np.zeros((TB, HEAD_DIM)))
    out = []
    for tab, idt in zip((cos, sin_a, sin_b), ident):
        full = np.concatenate([tab, idt], axis=0)
        out.append(jnp.asarray(np.tile(full, (1, LANES // HEAD_DIM)), F32))
    return out


def _dft_tables(t):
    m = np.outer(np.arange(t), np.arange(t)) % t
    ang = 2.0 * np.pi * m / t
    s = 1.0 / np.sqrt(t)
    return jnp.asarray(np.cos(ang) * s, F32).astype(BF16), jnp.asarray(np.sin(ang) * s, F32).astype(BF16)


def _adaln_body(cond_ref, w_ref, b_ref, o_ref):
    c = cond_ref[...]
    s = (c * jax.nn.sigmoid(c)).astype(BF16)
    o_ref[0] = jnp.dot(s, w_ref[0].astype(BF16), preferred_element_type=F32) + b_ref[0]


def _adaln(cond8, ada_w, ada_b):
    tn = 1536
    return pl.pallas_call(
        _adaln_body,
        grid=(DEPTH, 6 * D // tn),
        in_specs=[pl.BlockSpec((SUBLANES, D), lambda l, j: (0, 0)),
                  pl.BlockSpec((1, D, tn), lambda l, j: (l, 0, j)),
                  pl.BlockSpec((1, 1, tn), lambda l, j: (l, 0, j))],
        out_specs=pl.BlockSpec((1, SUBLANES, tn), lambda l, j: (l, 0, j)),
        out_shape=jax.ShapeDtypeStruct((DEPTH, SUBLANES, 6 * D), F32),
        compiler_params=_cparams(2),
        name="adaln",
    )(cond8, ada_w, ada_b.reshape(DEPTH, 1, 6 * D))


def _l0_in_body(xp_ref, xs_ref, mod_ref, g_ref, w_ref, qg_ref, kg_ref, bd_ref,
                cos_ref, sa_ref, sb_ref, q_ref, k_ref, v_ref, u_ref, kc_ref, vc_ref, wb_ref):
    i = pl.program_id(0)
    _bf16_once(w_ref, wb_ref)
    x = jnp.where(i < NT_CTX, xp_ref[...], xs_ref[...])
    m = mod_ref[0]
    h = _modulate(x, g_ref[...], m[:, 0:D], m[:, D:2 * D])
    z = jnp.dot(h.astype(BF16), wb_ref[...], preferred_element_type=F32)
    cos, sa, sb, bd = cos_ref[...], sa_ref[...], sb_ref[...], bd_ref[...]

    def head_norm_rope(zz, gain):
        ss = jnp.dot((zz * zz).astype(BF16), bd, preferred_element_type=F32)
        y = zz * lax.rsqrt(ss * (1.0 / HEAD_DIM) + EPS) * gain
        return (y * cos + pltpu.roll(y, LANES - 16, axis=1) * sa
                + pltpu.roll(y, 16, axis=1) * sb)

    for s in range(ATTN_W // LANES):
        qs = head_norm_rope(z[:, s * LANES:(s + 1) * LANES], qg_ref[...])
        q_ref[:, s * LANES:(s + 1) * LANES] = (qs * (HEAD_DIM ** -0.5)).astype(BF16)
    k = head_norm_rope(z[:, ATTN_W:ATTN_W + KV_W], kg_ref[...])
    v = z[:, ATTN_W + KV_W:ATTN_W + 2 * KV_W]
    k_ref[...] = k
    v_ref[...] = v
    u_ref[...] = z[:, ATTN_W + 2 * KV_W:MIX_IN]

    @pl.when(i < NT_CTX)
    def _():
        for q in range(TB // T_CTX):
            kc_ref[q] = k[q * T_CTX:(q + 1) * T_CTX, :].T
            vc_ref[q] = v[q * T_CTX:(q + 1) * T_CTX, :].T


def _l0_in(xp, xs, mods, g, w_in, qg, kg, bd, tabs):
    tab_spec = pl.BlockSpec(
        (TB, LANES), lambda i: (jnp.where(i < NT_CTX, T_LAT // TB, (i - NT_CTX) % (T_LAT // TB)), 0))
    const = lambda shape: pl.BlockSpec(shape, lambda i: (0,) * len(shape))
    return pl.pallas_call(
        _l0_in_body,
        grid=(NT,),
        in_specs=[pl.BlockSpec((TB, D), lambda i: (jnp.minimum(i, NT_CTX - 1), 0)),
                  pl.BlockSpec((TB, D), lambda i: (jnp.maximum(i - NT_CTX, 0), 0)),
                  pl.BlockSpec((1, 1, 6 * D), lambda i: (_mod_row(i, NT_CTX, T_LAT // TB), 0, 0)),
                  const((1, D)), const((D, MIX_IN)), const((1, LANES)), const((1, LANES)),
                  const((LANES, LANES)), tab_spec, tab_spec, tab_spec],
        out_specs=[pl.BlockSpec((TB, ATTN_W), lambda i: (i, 0)),
                   pl.BlockSpec((TB, KV_W), lambda i: (i, 0)),
                   pl.BlockSpec((TB, KV_W), lambda i: (i, 0)),
                   pl.BlockSpec((TB, POOL_W), lambda i: (i, 0)),
                   pl.BlockSpec((TB // T_CTX, KV_W, T_CTX), lambda i: (jnp.minimum(i, NT_CTX - 1), 0, 0)),
                   pl.BlockSpec((TB // T_CTX, KV_W, T_CTX), lambda i: (jnp.minimum(i, NT_CTX - 1), 0, 0))],
        out_shape=[jax.ShapeDtypeStruct((N_TOK, ATTN_W), BF16),
                   jax.ShapeDtypeStruct((N_TOK, KV_W), F32),
                   jax.ShapeDtypeStruct((N_TOK, KV_W), F32),
                   jax.ShapeDtypeStruct((N_TOK, POOL_W), F32),
                   jax.ShapeDtypeStruct((N_CTX_B, KV_W, T_CTX), F32),
                   jax.ShapeDtypeStruct((N_CTX_B, KV_W, T_CTX), F32)],
        scratch_shapes=[pltpu.VMEM((D, MIX_IN), BF16)],
        compiler_params=_cparams(1),
        name="l0_in",
    )(xp, xs, mods, g, w_in, qg, kg, bd, *tabs)


def _head_halves(x):
    z = jnp.zeros_like(x)
    return jnp.concatenate([x, z], axis=1), jnp.concatenate([z, x], axis=1)


_NT_DIMS = (((1,), (1,)), ((), ()))


def _ones_halves(x):
    one = jnp.ones_like(x)
    return jnp.concatenate([x, one], axis=1), jnp.concatenate([one, x], axis=1)


def _sink_attend(scores, values, sk, half):
    mx = sk
    for sc in scores:
        mx = jnp.maximum(mx, jnp.max(sc, axis=-1, keepdims=True))
    acc = None
    for sc, val in zip(scores, values):
        part = jnp.dot(jnp.exp(sc - mx).astype(BF16), val, preferred_element_type=F32)
        acc = part if acc is None else acc + part
    ones_lane = HEAD_DIM * (1 - half)
    den = acc[:, ones_lane:ones_lane + 1] + jnp.exp(sk - mx)
    return acc * (1.0 / den)


def _sink_col(sink_ref, heads, rows):
    return jnp.concatenate([jnp.broadcast_to(sink_ref[h:h + 1, 0:1], (rows, 1)) for h in heads], axis=0)


CTX_SEQS = 2


def _ctx_attn_body(q_ref, k_ref, v_ref, sink_ref, o_ref):
    lo = lax.broadcasted_iota(I32, (T_CTX, LANES), 1) < HEAD_DIM
    for b, j in ((b, j) for b in range(CTX_SEQS) for j in range(KV_W // HEAD_DIM)):
        seq = slice(b * T_CTX, (b + 1) * T_CTX)
        kj = k_ref[seq, j * HEAD_DIM:(j + 1) * HEAD_DIM].astype(BF16)
        vj = v_ref[seq, j * HEAD_DIM:(j + 1) * HEAD_DIM].astype(BF16)
        k_halves = _head_halves(kj)
        vd = jnp.concatenate([vj, vj], axis=1)
        q2 = jnp.concatenate([q_ref[seq, (2 * j) * LANES:(2 * j + 1) * LANES],
                              q_ref[seq, (2 * j + 1) * LANES:(2 * j + 2) * LANES]], axis=0)
        outs = []
        for half in range(2):
            sc = lax.dot_general(q2, k_halves[half], _NT_DIMS, preferred_element_type=F32)
            sk = _sink_col(sink_ref, (4 * j + half, 4 * j + 2 + half), T_CTX)
            mx = jnp.maximum(sk, jnp.max(sc, axis=-1, keepdims=True))
            p = jnp.exp(sc - mx)
            inv = 1.0 / (jnp.exp(sk - mx) + jnp.sum(p, axis=-1, keepdims=True))
            outs.append(jnp.dot((p * inv).astype(BF16), vd, preferred_element_type=F32))
        for s2 in range(2):
            rows = slice(s2 * T_CTX, (s2 + 1) * T_CTX)
            o_ref[seq, (2 * j + s2) * LANES:(2 * j + s2 + 1) * LANES] = (
                jnp.where(lo, outs[0][rows], outs[1][rows]).astype(BF16))


def _ctx_attn(q, k, v, sink_b):
    rows = CTX_SEQS * T_CTX
    return pl.pallas_call(
        _ctx_attn_body,
        grid=(N_CTX_B // CTX_SEQS,),
        in_specs=[pl.BlockSpec((rows, ATTN_W), lambda b: (b, 0)),
                  pl.BlockSpec((rows, KV_W), lambda b: (b, 0)),
                  pl.BlockSpec((rows, KV_W), lambda b: (b, 0)),
                  pl.BlockSpec((SUBLANES, LANES), lambda b: (0, 0))],
        out_specs=pl.BlockSpec((rows, ATTN_W), lambda b: (b, 0)),
        out_shape=jax.ShapeDtypeStruct((N_CTX, ATTN_W), BF16),
        compiler_params=_cparams(1),
        name="ctx_attn",
    )(q, k, v, sink_b)


QB = 128
SPAN = QB + 2 * WINDOW


def _lat_attn_body(q_ref, k_ref, v_ref, ck_ref, cv_ref, sink_ref, o_ref):
    qb = pl.program_id(1)
    start = qb * QB
    kws, vws = [], []
    for c in (-1, 0, 1):
        cs = pl.multiple_of(jnp.clip(start + c * QB, 0, T_LAT - QB), QB)
        kws.append(k_ref[pl.ds(cs, QB), :])
        vws.append(v_ref[pl.ds(cs, QB), :])
    kw = jnp.concatenate(kws, axis=0).astype(BF16)
    vw = jnp.concatenate(vws, axis=0).astype(BF16)
    ck = ck_ref[0].astype(BF16)
    cv = cv_ref[0].astype(BF16)
    qpos = start + (lax.broadcasted_iota(I32, (2 * QB, SPAN), 0) & (QB - 1))
    kpos = start - WINDOW + lax.broadcasted_iota(I32, (2 * QB, SPAN), 1)
    valid = (kpos >= 0) & (kpos < T_LAT) & (jnp.abs(qpos - kpos) <= WINDOW)
    lo = lax.broadcasted_iota(I32, (QB, LANES), 1) < HEAD_DIM
    for j in range(KV_W // HEAD_DIM):
        sl = slice(j * HEAD_DIM, (j + 1) * HEAD_DIM)
        kw_halves = _head_halves(kw[:, sl])
        ck_halves = _head_halves(ck[:, sl])
        vw_halves = _ones_halves(vw[:, sl])
        cv_halves = _ones_halves(cv[:, sl])
        q2 = jnp.concatenate([q_ref[:, (2 * j) * LANES:(2 * j + 1) * LANES],
                              q_ref[:, (2 * j + 1) * LANES:(2 * j + 2) * LANES]], axis=0)
        outs = []
        for half in range(2):
            s_win = lax.dot_general(q2, kw_halves[half], _NT_DIMS, preferred_element_type=F32)
            s_win = jnp.where(valid, s_win, NEG)
            s_ctx = lax.dot_general(q2, ck_halves[half], _NT_DIMS, preferred_element_type=F32)
            sk = _sink_col(sink_ref, (4 * j + half, 4 * j + 2 + half), QB)
            outs.append(_sink_attend([s_win, s_ctx], [vw_halves[half], cv_halves[half]], sk, half))
        for s2 in range(2):
            rows = slice(s2 * QB, (s2 + 1) * QB)
            o_ref[:, (2 * j + s2) * LANES:(2 * j + s2 + 1) * LANES] = (
                jnp.where(lo, outs[0][rows], outs[1][rows]).astype(BF16))


def _lat_attn(q, k, v, ck, cv, sink_b):
    lat0 = N_CTX // T_LAT
    return pl.pallas_call(
        _lat_attn_body,
        grid=(N_LAT_B, T_LAT // QB),
        in_specs=[pl.BlockSpec((QB, ATTN_W), lambda b, i: (N_CTX // QB + b * (T_LAT // QB) + i, 0)),
                  pl.BlockSpec((T_LAT, KV_W), lambda b, i: (lat0 + b, 0)),
                  pl.BlockSpec((T_LAT, KV_W), lambda b, i: (lat0 + b, 0)),
                  pl.BlockSpec((1, PAST, KV_W), lambda b, i: (b, 0, 0)),
                  pl.BlockSpec((1, PAST, KV_W), lambda b, i: (b, 0, 0)),
                  pl.BlockSpec((SUBLANES, LANES), lambda b, i: (0, 0))],
        out_specs=pl.BlockSpec((QB, ATTN_W), lambda b, i: (b * (T_LAT // QB) + i, 0)),
        out_shape=jax.ShapeDtypeStruct((N_LAT, ATTN_W), BF16),
        compiler_params=_cparams(2),
        name="lat_attn",
    )(q, k, v, ck, cv, sink_b)


def _l0_out_body(oc_ref, ol_ref, u_ref, xp_ref, xs_ref, mod_ref, pw_ref, ps_ref, wo_ref,
                 g_ref, wr_ref, br_ref, tri_ref, xext_ref, info_ref, cnt_ref, wb_ref, base_ref):
    i = pl.program_id(0)
    _bf16_once(wo_ref, wb_ref)
    is_ctx = i < N_CTX // TB_MIX
    o = jnp.where(is_ctx, oc_ref[...], ol_ref[...])
    x = jnp.where(is_ctx, xp_ref[...], xs_ref[...])
    tseq = jnp.where(is_ctx, T_CTX, T_LAT)
    pos = lax.broadcasted_iota(I32, (TB_MIX, LANES), 0) & (tseq - 1)
    ys = []
    for g, win in enumerate(POOL_WINDOWS):
        hw = win // 2
        ug = u_ref[:, g * LANES:(g + 1) * LANES]
        acc = ug
        for jj in range(-hw, hw):
            if jj == 0:
                continue
            sh = pltpu.roll(ug, (-jj) % TB_MIX, axis=0)
            ok = (pos + jj >= 0) if jj < 0 else (pos + jj < tseq)
            acc = acc + jnp.where(ok, sh, 0.0)
        cnt = (jnp.minimum(pos + hw, tseq) - jnp.maximum(pos - hw, 0)).astype(F32)
        pooled = acc / cnt - ug
        ys.append(jnp.dot(pooled.astype(BF16), pw_ref[g].astype(BF16), preferred_element_type=F32))
    y = jnp.concatenate(ys, axis=1) * ps_ref[...]
    out = (jnp.dot(o, wb_ref[0:ATTN_W, :], preferred_element_type=F32)
           + jnp.dot(y.astype(BF16), wb_ref[ATTN_W:ATTN_W + POOL_W, :], preferred_element_type=F32))
    x1 = x + mod_ref[0][:, 2 * D:3 * D] * out
    _route_tile(x1, mod_ref[0], _mod_row(i, N_CTX // TB_MIX, 1), g_ref, wr_ref, br_ref, tri_ref,
                xext_ref, info_ref, cnt_ref, base_ref)


def _l0_out(o_ctx, o_lat, u, xp, xs, mods, pool_w, pool_scale, w_out, route_ops):
    ntc = N_CTX // TB_MIX
    r_in, r_out, r_shape, r_scratch = _route_specs(TB_MIX, lambda i: i)
    const = lambda shape: pl.BlockSpec(shape, lambda i: (0,) * len(shape))
    ctx_map = lambda i: (jnp.minimum(i, ntc - 1), 0)
    lat_map = lambda i: (jnp.maximum(i - ntc, 0), 0)
    return pl.pallas_call(
        _l0_out_body,
        grid=(N_TOK // TB_MIX,),
        in_specs=[pl.BlockSpec((TB_MIX, ATTN_W), ctx_map),
                  pl.BlockSpec((TB_MIX, ATTN_W), lat_map),
                  pl.BlockSpec((TB_MIX, POOL_W), lambda i: (i, 0)),
                  pl.BlockSpec((TB_MIX, D), ctx_map),
                  pl.BlockSpec((TB_MIX, D), lat_map),
                  pl.BlockSpec((1, 1, 6 * D), lambda i: (_mod_row(i, ntc, 1), 0, 0)),
                  const((len(POOL_WINDOWS), LANES, LANES)), const((1, POOL_W)), const((D, D))] + r_in,
        out_specs=r_out,
        out_shape=r_shape,
        scratch_shapes=[pltpu.VMEM((D, D), BF16), r_scratch],
        compiler_params=_cparams(1, VMEM_LIMIT),
        name="l0_out",
    )(o_ctx, o_lat, u, xp, xs, mods, pool_w, pool_scale, w_out, *route_ops)


def _first_max(vals):
    best, idx = vals[0], jnp.zeros(vals[0].shape, I32)
    for r in range(1, len(vals)):
        better = vals[r] > best
        idx = jnp.where(better, r, idx)
        best = jnp.where(better, vals[r], best)
    return best, idx


def _softmax_rows(rows):
    mx = functools.reduce(jnp.maximum, rows)
    ex = [jnp.exp(r - mx) for r in rows]
    tot = functools.reduce(lambda a, b: a + b, ex)
    return [e / tot for e in ex]


def _route_tile(x, m, mod_id, g_ref, wr_ref, br_ref, tri_ref, xext_ref, info_ref, cnt_ref, base_ref):
    t_rows = x.shape[0]

    @pl.when(pl.program_id(0) == 0)
    def _():
        base_ref[...] = jnp.zeros_like(base_ref)

    h = _modulate(x, g_ref[...], m[:, 3 * D:4 * D], m[:, 4 * D:5 * D])

    hh = h.astype(BF16)
    hl = (h - hh.astype(F32)).astype(BF16)
    wh, wl = wr_ref[0], wr_ref[1]
    lg = (lax.dot_general(wh, hh, _NT_DIMS, preferred_element_type=F32)
          + lax.dot_general(wl, hh, _NT_DIMS, preferred_element_type=F32)
          + lax.dot_general(wh, hl, _NT_DIMS, preferred_element_type=F32)) + br_ref[:, 0:1]

    pg = _softmax_rows([lg[N_EXPERTS + r:N_EXPERTS + r + 1] for r in range(N_GROUPS)])
    pg_top, gi = _first_max(pg)
    le = []
    for j in range(PER_GROUP):
        sel = lg[(N_GROUPS - 1) * PER_GROUP + j:(N_GROUPS - 1) * PER_GROUP + j + 1]
        for g in range(N_GROUPS - 2, -1, -1):
            sel = jnp.where(gi == g, lg[g * PER_GROUP + j:g * PER_GROUP + j + 1], sel)
        le.append(sel)
    pe = _softmax_rows(le)
    p1, i1 = _first_max(pe)
    p2, i2 = _first_max([jnp.where(i1 == j, -1.0, pe[j]) for j in range(PER_GROUP)])
    den = p1 + p2
    w1 = pg_top * p1 / den
    w2 = pg_top * p2 / den

    lo = jnp.minimum(i1, i2)
    hi = jnp.maximum(i1, i2)
    cls = gi * PAIRS + jnp.where(lo == 0, 0, jnp.where(lo == 1, 3, 5)) + hi - lo - 1
    w_lo = jnp.where(i1 == lo, w1, w2)
    w_hi = jnp.where(i1 == lo, w2, w1)

    crow = lax.broadcasted_iota(I32, (CLASS_ROWS, t_rows), 0)
    hit = crow == cls
    onehot = jnp.where(hit, 1.0, 0.0)
    before = jnp.dot(onehot.astype(BF16), tri_ref[...], preferred_element_type=F32)
    before = before + base_ref[:, 0:1]
    rank = jnp.sum(jnp.where(hit, before, 0.0), axis=0, keepdims=True)
    base_ref[...] = base_ref[...] + jnp.sum(onehot, axis=1, keepdims=True)
    cnt_ref[...] = base_ref[...]

    mod_id = jnp.zeros_like(w1) + mod_id.astype(F32)
    zero = jnp.zeros_like(w1)
    info_ref[...] = jnp.concatenate([cls.astype(F32), rank, zero, zero, zero, zero, zero, zero], axis=0)
    side = jnp.concatenate([w_lo, w_hi, mod_id, jnp.zeros((LANES - 3, t_rows), F32)], axis=0).T
    xext_ref[:, 0:D] = x
    xext_ref[:, D:XEXT] = side


def _route_specs(tile, step_map):
    const = lambda shape: pl.BlockSpec(shape, lambda *i: (0,) * len(shape))
    in_specs = [const((1, D)), const((2, ROUTE_ROWS, D)), const((ROUTE_ROWS, LANES)), const((tile, tile))]
    out_specs = [pl.BlockSpec((tile, XEXT), lambda *i: (step_map(*i), 0)),
                 pl.BlockSpec((SUBLANES, tile), lambda *i: (0, step_map(*i))),
                 const((CLASS_ROWS, LANES))]
    out_shape = [jax.ShapeDtypeStruct((N_TOK, XEXT), F32),
                 jax.ShapeDtypeStruct((SUBLANES, N_TOK), F32),
                 jax.ShapeDtypeStruct((CLASS_ROWS, LANES), F32)]
    return in_specs, out_specs, out_shape, pltpu.VMEM((CLASS_ROWS, LANES), F32)


def _route_operands(g, wr, br, tile):
    tri = jnp.asarray(np.arange(tile)[:, None] < np.arange(tile)[None, :], BF16)
    return g, wr, br, tri


def _plan(info, counts):
    cls = info[0].astype(I32)
    rank = info[1].astype(I32)
    cnt = counts[:N_CLASS, 0].astype(I32)
    tiles = (cnt + TM - 1) // TM
    tend = jnp.cumsum(tiles)
    tstart = tend - tiles
    n_used = tend[-1]
    cidx = jnp.arange(N_CLASS, dtype=I32)
    pos = jnp.sum(jnp.where(cls[:, None] == cidx, tstart * TM, 0), axis=-1) + rank
    n = jnp.arange(NT_FFN, dtype=I32)
    tile = jnp.minimum(n, jnp.maximum(n_used - 1, 0))
    tcls = jnp.minimum(jnp.sum((tile[:, None] >= tend[None, :]).astype(I32), axis=1), N_CLASS - 1)
    of_cls = tcls[:, None] == cidx
    rows = jnp.sum(jnp.where(of_cls, cnt, 0), axis=1) - (tile - jnp.sum(jnp.where(of_cls, tstart, 0), axis=1)) * TM
    rows = jnp.where(n < n_used, jnp.clip(rows, 0, TM), 0)
    chunks = (rows + CH - 1) // CH
    pair = tcls % PAIRS
    lo = (pair >= 3).astype(I32) + (pair >= 5).astype(I32)
    hi = jnp.where(pair < 3, pair + 1, jnp.where(pair < 5, pair - 1, 3))
    e_lo = (tcls // PAIRS) * PER_GROUP + lo
    e_hi = (tcls // PAIRS) * PER_GROUP + hi
    eidx = jnp.arange(N_EXPERTS, dtype=I32)
    live = n < n_used
    uses = ((e_lo[:, None] == eidx) | (e_hi[:, None] == eidx)) & live[:, None]
    first = jnp.min(jnp.where(uses, n[:, None], NT_FFN), axis=0)
    new_lo = live & (jnp.sum(jnp.where(e_lo[:, None] == eidx, first, 0), axis=1) == n)
    new_hi = live & (jnp.sum(jnp.where(e_hi[:, None] == eidx, first, 0), axis=1) == n)

    def held(new, e):
        last = lax.cummax(jnp.where(new, n, 0))
        return jnp.sum(jnp.where(last[:, None] == n[None, :], e[None, :], 0), axis=1)

    slots = (held(new_lo, e_lo), held(new_hi, e_hi), new_lo.astype(I32), new_hi.astype(I32))
    return pos, lo, hi, slots, n_used.reshape(1), chunks


OCT = TM // SUBLANES


def _ffn_body(pos_ref, lo_ref, hi_ref, sa_ref, sb_ref, newa_ref, newb_ref, nu_ref, ch_ref,
              xext_hbm, mod_ref, g_ref, w1a_ref, w1b_ref, w3a_ref, w3b_ref, w2a_ref, w2b_ref,
              out_hbm, src_ref, dst_ref, xbuf, ybuf, wb1, wb3, wb2, gsem, ssem):
    n = pl.program_id(0)
    n_used = nu_ref[0]

    def gather_copy(tile, s, c, j):
        return pltpu.make_async_copy(
            xext_hbm.at[pl.ds(src_ref[tile * TM + c * CH + j], 1)],
            xbuf.at[s, c * (CH // SUBLANES) + j // SUBLANES, pl.ds(j % SUBLANES, 1)], gsem.at[s])

    def scatter_copy(tile, s, c, j):
        return pltpu.make_async_copy(
            ybuf.at[s, c * (CH // SUBLANES) + j // SUBLANES, pl.ds(j % SUBLANES, 1)],
            out_hbm.at[pl.ds(dst_ref[tile * TM + c * CH + j], 1)], ssem.at[s])

    def start_rows(copy, tile, s):
        def chunk(c, carry):
            for j in range(CH):
                copy(tile, s, c, j).start()
            return carry
        lax.fori_loop(0, ch_ref[tile], chunk, 0)

    def wait_rows(src, dst, sem, tile):
        def chunk(c, carry):
            pltpu.make_async_copy(src, dst, sem).wait()
            return carry
        lax.fori_loop(0, ch_ref[tile], chunk, 0)

    def wait_gather(tile, s):
        rows = xbuf.at[s, pl.ds(0, CH // SUBLANES)]
        wait_rows(rows, rows, gsem.at[s], tile)

    def wait_scatter(tile, s):
        rows = ybuf.at[s, pl.ds(0, CH // SUBLANES)]
        wait_rows(rows, rows, ssem.at[s], tile)

    @pl.when(n == 0)
    def _():
        def pad_rows(tile, carry):
            @pl.when(ch_ref[tile] > 0)
            def _():
                first = tile * TM + (ch_ref[tile] - 1) * CH
                for j in range(CH):
                    src_ref[first + j] = 0
                    dst_ref[first + j] = N_TOK + ((first + j) & (2 * TM - 1))
            return carry

        lax.fori_loop(0, NT_FFN, pad_rows, 0)

        def put(t, carry):
            p = pos_ref[t]
            src_ref[p] = t
            dst_ref[p] = t
            return carry

        lax.fori_loop(0, N_TOK, put, 0, unroll=8)

        xbuf[...] = jnp.zeros_like(xbuf)
        ybuf[...] = jnp.zeros_like(ybuf)
        for s in range(2):
            dumps = [pltpu.make_async_copy(
                ybuf.at[s, q], out_hbm.at[pl.ds(N_TOK + s * TM + q * SUBLANES, SUBLANES)], ssem.at[s])
                for q in range(OCT)]
            for dump in dumps:
                dump.start()
            for dump in dumps:
                dump.wait()
        start_rows(gather_copy, 0, 0)

    def step(slot):
        @pl.when(n + 1 < n_used)
        def _():
            start_rows(gather_copy, n + 1, 1 - slot)

        wait_gather(n, slot)

        @pl.when(n >= 2)
        def _():
            wait_scatter(n - 2, slot)

        e_lo = lo_ref[n]
        e_hi = hi_ref[n]

        @pl.when(newa_ref[n] == 1)
        def _():
            wb1[e_lo] = w1a_ref[0, 0].astype(BF16)
            wb3[e_lo] = w3a_ref[0, 0].astype(BF16)
            wb2[e_lo] = w2a_ref[0, 0].astype(BF16)

        @pl.when(newb_ref[n] == 1)
        def _():
            wb1[e_hi] = w1b_ref[0, 0].astype(BF16)
            wb3[e_hi] = w3b_ref[0, 0].astype(BF16)
            wb2[e_hi] = w2b_ref[0, 0].astype(BF16)

        xe = xbuf[slot].reshape(TM, XEXT)
        x = xe[:, 0:D]
        w_lo = xe[:, D:D + 1]
        w_hi = xe[:, D + 1:D + 2]
        mod_id = xe[:, D + 2:D + 3]

        def pick(lo, hi):
            return jnp.where(mod_id < 0.5, mod_ref[0][:, lo:hi],
                             jnp.where(mod_id < 1.5, mod_ref[1][:, lo:hi], mod_ref[2][:, lo:hi]))

        h = _modulate(x, g_ref[...], pick(3 * D, 4 * D), pick(4 * D, 5 * D)).astype(BF16)

        def act(e, w):
            h1 = jnp.dot(h, wb1[e], preferred_element_type=F32)
            h3 = jnp.dot(h, wb3[e], preferred_element_type=F32)
            return ((h1 * jax.nn.sigmoid(h1)) * h3 * w).astype(BF16)

        y = (jnp.dot(act(e_lo, w_lo), wb2[e_lo], preferred_element_type=F32)
             + jnp.dot(act(e_hi, w_hi), wb2[e_hi], preferred_element_type=F32))
        ybuf[slot] = (x + pick(5 * D, 6 * D) * y).reshape(OCT, SUBLANES, D)
        start_rows(scatter_copy, n, slot)

        @pl.when(n == n_used - 1)
        def _():
            @pl.when(n >= 1)
            def _():
                wait_scatter(n - 1, 1 - slot)
            wait_scatter(n, slot)

    for s in range(2):
        @pl.when((n < n_used) & (n % 2 == s))
        def _():
            step(s)


def _ffn(pos, lo, hi, slots, n_used, chunks, xext, mods, layer, g, w1, w3, w2):
    a_map = lambda n, p, lo, hi, sa, sb, na, nb, nu, ch: (layer, sa[n], 0, 0)
    b_map = lambda n, p, lo, hi, sa, sb, na, nb, nu, ch: (layer, sb[n], 0, 0)
    up = lambda imap: pl.BlockSpec((1, 1, D, D_EXPERT), imap)
    down = lambda imap: pl.BlockSpec((1, 1, D_EXPERT, D), imap)
    return pl.pallas_call(
        _ffn_body,
        grid_spec=pltpu.PrefetchScalarGridSpec(
            num_scalar_prefetch=9, grid=(NT_FFN,),
            in_specs=[pl.BlockSpec(memory_space=pl.ANY),
                      pl.BlockSpec((SUBLANES, 1, 6 * D), lambda n, *_: (layer, 0, 0)),
                      pl.BlockSpec((1, D), lambda n, *_: (0, 0)),
                      up(a_map), up(b_map), up(a_map), up(b_map), down(a_map), down(b_map)],
            out_specs=pl.BlockSpec(memory_space=pl.ANY),
            scratch_shapes=[pltpu.SMEM((P_FFN,), I32), pltpu.SMEM((P_FFN,), I32),
                            pltpu.VMEM((2, OCT, SUBLANES, XEXT), F32),
                            pltpu.VMEM((2, OCT, SUBLANES, D), F32),
                            pltpu.VMEM((PER_GROUP, D, D_EXPERT), BF16),
                            pltpu.VMEM((PER_GROUP, D, D_EXPERT), BF16),
                            pltpu.VMEM((PER_GROUP, D_EXPERT, D), BF16),
                            pltpu.SemaphoreType.DMA((2,)), pltpu.SemaphoreType.DMA((2,))]),
        out_shape=jax.ShapeDtypeStruct((N_TOK + 2 * TM, D), F32),
        compiler_params=_cparams(1, VMEM_LIMIT),
        name=f"ffn{layer}",
    )(pos, lo, hi, *slots, n_used, chunks, xext, mods, g, w1, w1, w3, w3, w2, w2)


def _moe(routed, mods, layer, g, w1, w3, w2):
    xext, info, counts = routed
    pos, lo, hi, slots, n_used, chunks = _plan(info, counts)
    return _ffn(pos, lo, hi, slots, n_used, chunks, xext, mods, layer, g, w1, w3, w2)


FG = 256


def _l1_in_body(x_ref, mod_ref, g_ref, w_ref, c_ref, s_ref, zc_ref, zs_ref, wb_ref):
    _bf16_once(w_ref, wb_ref)
    m = mod_ref[0]
    h = _modulate(x_ref[...], g_ref[...], m[:, 0:D], m[:, D:2 * D])
    z = jnp.dot(h.astype(BF16), wb_ref[...], preferred_element_type=F32).astype(BF16)
    for g in range(D // FG):
        zg = z[:, g * FG:(g + 1) * FG]
        zc_ref[:, g * FG:(g + 1) * FG] = jnp.dot(zg, c_ref[...], preferred_element_type=F32).astype(BF16)
        zs_ref[:, g * FG:(g + 1) * FG] = jnp.dot(zg, s_ref[...], preferred_element_type=F32).astype(BF16)


def _l1_in(x, mods, g, w, c256, s256):
    const = lambda shape: pl.BlockSpec(shape, lambda i: (0,) * len(shape))
    return pl.pallas_call(
        _l1_in_body,
        grid=(NT,),
        in_specs=[pl.BlockSpec((TB, D), lambda i: (i, 0)),
                  pl.BlockSpec((1, 1, 6 * D),
                               lambda i: (SUBLANES + _mod_row(i, NT_CTX, T_LAT // TB), 0, 0)),
                  const((1, D)), const((D, D)), const((FG, FG)), const((FG, FG))],
        out_specs=[pl.BlockSpec((TB, D), lambda i: (i, 0)), pl.BlockSpec((TB, D), lambda i: (i, 0))],
        out_shape=[jax.ShapeDtypeStruct((N_TOK, D), BF16), jax.ShapeDtypeStruct((N_TOK, D), BF16)],
        scratch_shapes=[pltpu.VMEM((D, D), BF16)],
        compiler_params=_cparams(1),
        name="l1_in",
    )(x, mods, g, w, c256, s256)


def _l1_out_body(zc_t_ref, zs_t_ref, zc_q_ref, zs_q_ref, c256_ref, s256_ref, c1k_ref, s1k_ref,
                 x_ref, mod_ref, wo_ref, g_ref, wr_ref, br_ref, tri_ref,
                 xext_ref, info_ref, cnt_ref, f_ref, wb_ref, base_ref):
    i = pl.program_id(0)
    _bf16_once(wo_ref, wb_ref)

    @pl.when(i < NT_CTX)
    def _():
        for q in range(TB // T_CTX):
            rows = slice(q * T_CTX, (q + 1) * T_CTX)
            f = (jnp.dot(c256_ref[...], zc_t_ref[rows, :], preferred_element_type=F32)
                 - jnp.dot(s256_ref[...], zs_t_ref[rows, :], preferred_element_type=F32))
            f_ref[rows, :] = f.astype(BF16)

    @pl.when(i >= NT_CTX)
    def _():
        f = (jnp.dot(c1k_ref[...], zc_q_ref[...], preferred_element_type=F32)
             - jnp.dot(s1k_ref[...], zs_q_ref[...], preferred_element_type=F32))
        f_ref[...] = f.astype(BF16)

    out = jnp.dot(f_ref[...], wb_ref[...], preferred_element_type=F32)
    x3 = x_ref[...] + mod_ref[0][:, 2 * D:3 * D] * out
    _route_tile(x3, mod_ref[0], _mod_row(i, NT_CTX, T_LAT // TB), g_ref, wr_ref, br_ref, tri_ref,
                xext_ref, info_ref, cnt_ref, base_ref)


def _l1_out(zc, zs, c256, s256, c1k, s1k, x, mods, w_out, route_ops):
    const = lambda shape: pl.BlockSpec(shape, lambda i: (0,) * len(shape))
    r_in, r_out, r_shape, r_scratch = _route_specs(TB, lambda i: i)
    tile_map = lambda i: (jnp.minimum(i, NT_CTX - 1), 0)
    seq_map = lambda i: (N_CTX // T_LAT + jnp.maximum(i - NT_CTX, 0) // (T_LAT // TB), 0)
    row_map = lambda i: (jnp.maximum(i - NT_CTX, 0) % (T_LAT // TB), 0)
    return pl.pallas_call(
        _l1_out_body,
        grid=(NT,),
        in_specs=[pl.BlockSpec((TB, D), tile_map), pl.BlockSpec((TB, D), tile_map),
                  pl.BlockSpec((T_LAT, D), seq_map), pl.BlockSpec((T_LAT, D), seq_map),
                  const((T_CTX, T_CTX)), const((T_CTX, T_CTX)),
                  pl.BlockSpec((TB, T_LAT), row_map), pl.BlockSpec((TB, T_LAT), row_map),
                  pl.BlockSpec((TB, D), lambda i: (i, 0)),
                  pl.BlockSpec((1, 1, 6 * D),
                               lambda i: (SUBLANES + _mod_row(i, NT_CTX, T_LAT // TB), 0, 0)),
                  const((D, D))] + r_in,
        out_specs=r_out,
        out_shape=r_shape,
        scratch_shapes=[pltpu.VMEM((TB, D), BF16), pltpu.VMEM((D, D), BF16), r_scratch],
        compiler_params=_cparams(1, VMEM_LIMIT),
        name="l1_out",
    )(zc, zs, zc, zs, c256, s256, c1k, s1k, x, mods, w_out, *route_ops)


def kernel(x_prompt, x_sample, cache_k, cache_v, c, c_ctx, ada_w, ada_b, norm_mix, norm_ffn, a_w_in, a_q_norm, a_k_norm, a_sink, pool_w, pool_scale, a_w_out, f_w_in, f_w_out, router_g_w, router_g_b, router_e_w, router_e_b, moe_w1, moe_w3, moe_w2):
    xp = x_prompt.reshape(N_CTX, D)
    xs = x_sample.reshape(N_LAT, D)

    cond8 = jnp.concatenate([c_ctx[None, :], c, jnp.zeros((SUBLANES - 1 - N_LAT_B, D), F32)], axis=0)
    mods = _adaln(cond8, ada_w, ada_b).reshape(DEPTH * SUBLANES, 1, 6 * D)

    tabs = _rope_tables()
    lane = np.arange(LANES)
    bd = jnp.asarray((lane[:, None] // HEAD_DIM) == (lane[None, :] // HEAD_DIM), BF16)
    c256, s256 = _dft_tables(T_CTX)
    c1k, s1k = _dft_tables(T_LAT)

    rw = jnp.swapaxes(jnp.concatenate([router_e_w, router_g_w], axis=2), 1, 2)
    rw = jnp.pad(rw, ((0, 0), (0, ROUTE_ROWS - rw.shape[1]), (0, 0)))
    rw_hi = rw.astype(BF16)
    rw_split = jnp.stack([rw_hi, (rw - rw_hi.astype(F32)).astype(BF16)], axis=1)
    rb = jnp.pad(jnp.concatenate([router_e_b, router_g_b], axis=1), ((0, 0), (0, ROUTE_ROWS - N_EXPERTS - N_GROUPS)))
    rb = jnp.broadcast_to(rb[:, :, None], (DEPTH, ROUTE_ROWS, LANES))

    def router_operands(l):
        return rw_split[l], rb[l]

    qg = jnp.tile(a_q_norm[0], LANES // HEAD_DIM)[None, :]
    kg = jnp.tile(a_k_norm[0], LANES // HEAD_DIM)[None, :]
    q, k, v, u, new_k, new_v = _l0_in(xp, xs, mods, norm_mix[0][None, :], a_w_in[0], qg, kg, bd, tabs)
    sink_b = jnp.broadcast_to(a_sink[0][:, None], (N_HEADS, LANES))
    o_ctx = _ctx_attn(q, k, v, sink_b)
    ck = cache_k[:, 0].reshape(N_LAT_B, PAST, KV_W)
    cv = cache_v[:, 0].reshape(N_LAT_B, PAST, KV_W)
    o_lat = _lat_attn(q, k, v, ck, cv, sink_b)
    routed = _l0_out(o_ctx, o_lat, u, xp, xs, mods, pool_w[0], pool_scale[0][None, :], a_w_out[0],
                     _route_operands(norm_ffn[0][None, :], *router_operands(0), TB_MIX))
    x2 = _moe(routed, mods, 0, norm_ffn[0][None, :], moe_w1, moe_w3, moe_w2)

    zc, zs = _l1_in(x2, mods, norm_mix[1][None, :], f_w_in[0], c256, s256)
    routed = _l1_out(zc, zs, c256, s256, c1k, s1k, x2, mods, f_w_out[0],
                     _route_operands(norm_ffn[1][None, :], *router_operands(1), TB))
    x4 = _moe(routed, mods, 1, norm_ffn[1][None, :], moe_w1, moe_w3, moe_w2)

    def cache_entry(t):
        t = t.reshape(N_CTX_B, 1, KV_W // HEAD_DIM, HEAD_DIM, T_CTX)
        return jnp.transpose(t, (0, 1, 4, 2, 3))

    new_k, new_v = cache_entry(new_k), cache_entry(new_v)
    return (x4[:N_CTX].reshape(N_CTX_B, T_CTX, D), x4[N_CTX:N_TOK].reshape(N_LAT_B, T_LAT, D),
            new_k, new_v)
```

```python
import functools

import numpy as np
import jax
import jax.numpy as jnp
from jax import lax
from jax.experimental import pallas as pl
from jax.experimental.pallas import tpu as pltpu

F32 = jnp.float32
BF16 = jnp.bfloat16
I32 = jnp.int32

D = 1024
DEPTH = 2
N_CTX_B, T_CTX = 16, 256
N_LAT_B, T_LAT = 2, 1024
N_CTX = N_CTX_B * T_CTX
N_LAT = N_LAT_B * T_LAT
N_TOK = N_CTX + N_LAT
PAST = 512
GRID_W = 64
HEAD_DIM = 64
N_HEADS = 8
ATTN_W = 512
KV_W = 128
POOL_W = 512
POOL_WINDOWS = (2, 4, 8, 16)
MIX_IN = ATTN_W + 2 * KV_W + POOL_W
WINDOW = 128
N_GROUPS = 4
PER_GROUP = 4
N_EXPERTS = 16
D_EXPERT = 512
ROPE_THETA = 10000.0
EPS = 1e-6
NEG = -1e30

LANES = 128
SUBLANES = 8
TB = 512
NT = N_TOK // TB
NT_CTX = N_CTX // TB
TB_MIX = 1024
TM = 256
PAIRS = 6
N_CLASS = N_GROUPS * PAIRS
CLASS_ROWS = 32
NT_FFN = N_TOK // TM + N_CLASS
P_FFN = NT_FFN * TM
CH = 32
XEXT = D + LANES
ROUTE_ROWS = 32

VMEM_LIMIT = 56 * 1024 * 1024


def _cparams(n_axes=1, vmem=None):
    return pltpu.CompilerParams(dimension_semantics=("arbitrary",) * n_axes,
                                vmem_limit_bytes=vmem)


def _modulate(x, g, shift, scale):
    ms = jnp.mean(x * x, axis=-1, keepdims=True)
    return (x * lax.rsqrt(ms + EPS) * g) * (1.0 + scale) + shift


def _bf16_once(w_ref, wb_ref):
    @pl.when(pl.program_id(0) == 0)
    def _():
        wb_ref[...] = w_ref[...].astype(BF16)


def _mod_row(tile, tiles_ctx, tiles_per_lat):
    return (tile >= tiles_ctx).astype(I32) + (tile >= tiles_ctx + tiles_per_lat).astype(I32)


def _rope_tables():
    t = np.arange(T_LAT)
    row = (t // GRID_W).astype(np.float64)
    col = (t % GRID_W).astype(np.float64)
    nf = HEAD_DIM // 4
    freqs = ROPE_THETA ** (-np.arange(nf, dtype=np.float64) / nf)
    d = np.arange(HEAD_DIM)
    pos = np.where(d[None, :] < HEAD_DIM // 2, row[:, None], col[:, None])
    ang = pos * freqs[d % nf][None, :]
    first = (d % (HEAD_DIM // 2)) < nf
    cos = np.cos(ang)
    sin_a = np.where(first[None, :], -np.sin(ang), 0.0)
    sin_b = np.where(first[None, :], 0.0, np.sin(ang))
    ident = (np.ones((TB, HEAD_DIM)), np.zeros((TB, HEAD_DIM)), np.zeros((TB, HEAD_DIM)))
    out = []
    for tab, idt in zip((cos, sin_a, sin_b), ident):
        full = np.concatenate([tab, idt], axis=0)
        out.append(jnp.asarray(np.tile(full, (1, LANES // HEAD_DIM)), F32))
    return out


def _dft_tables(t):
    m = np.outer(np.arange(t), np.arange(t)) % t
    ang = 2.0 * np.pi * m / t
    s = 1.0 / np.sqrt(t)
    return jnp.asarray(np.cos(ang) * s, F32).astype(BF16), jnp.asarray(np.sin(ang) * s, F32).astype(BF16)


def _adaln_body(cond_ref, w_ref, b_ref, o_ref):
    c = cond_ref[...]
    s = (c * jax.nn.sigmoid(c)).astype(BF16)
    o_ref[0] = jnp.dot(s, w_ref[0].astype(BF16), preferred_element_type=F32) + b_ref[0]


def _adaln(cond8, ada_w, ada_b):
    tn = 1536
    return pl.pallas_call(
        _adaln_body,
        grid=(DEPTH, 6 * D // tn),
        in_specs=[pl.BlockSpec((SUBLANES, D), lambda l, j: (0, 0)),
                  pl.BlockSpec((1, D, tn), lambda l, j: (l, 0, j)),
                  pl.BlockSpec((1, 1, tn), lambda l, j: (l, 0, j))],
        out_specs=pl.BlockSpec((1, SUBLANES, tn), lambda l, j: (l, 0, j)),
        out_shape=jax.ShapeDtypeStruct((DEPTH, SUBLANES, 6 * D), F32),
        compiler_params=_cparams(2),
        name="adaln",
    )(cond8, ada_w, ada_b.reshape(DEPTH, 1, 6 * D))


def _l0_in_body(xp_ref, xs_ref, mod_ref, g_ref, w_ref, qg_ref, kg_ref, bd_ref,
                cos_ref, sa_ref, sb_ref, q_ref, k_ref, v_ref, u_ref, kc_ref, vc_ref, wb_ref):
    i = pl.program_id(0)
    _bf16_once(w_ref, wb_ref)
    x = jnp.where(i < NT_CTX, xp_ref[...], xs_ref[...])
    m = mod_ref[0]
    h = _modulate(x, g_ref[...], m[:, 0:D], m[:, D:2 * D])
    z = jnp.dot(h.astype(BF16), wb_ref[...], preferred_element_type=F32)
    cos, sa, sb, bd = cos_ref[...], sa_ref[...], sb_ref[...], bd_ref[...]

    def head_norm_rope(zz, gain):
        ss = jnp.dot((zz * zz).astype(BF16), bd, preferred_element_type=F32)
        y = zz * lax.rsqrt(ss * (1.0 / HEAD_DIM) + EPS) * gain
        return (y * cos + pltpu.roll(y, LANES - 16, axis=1) * sa
                + pltpu.roll(y, 16, axis=1) * sb)

    for s in range(ATTN_W // LANES):
        qs = head_norm_rope(z[:, s * LANES:(s + 1) * LANES], qg_ref[...])
        q_ref[:, s * LANES:(s + 1) * LANES] = (qs * (HEAD_DIM ** -0.5)).astype(BF16)
    k = head_norm_rope(z[:, ATTN_W:ATTN_W + KV_W], kg_ref[...])
    v = z[:, ATTN_W + KV_W:ATTN_W + 2 * KV_W]
    k_ref[...] = k
    v_ref[...] = v
    u_ref[...] = z[:, ATTN_W + 2 * KV_W:MIX_IN]

    @pl.when(i < NT_CTX)
    def _():
        for q in range(TB // T_CTX):
            kc_ref[q] = k[q * T_CTX:(q + 1) * T_CTX, :].T
            vc_ref[q] = v[q * T_CTX:(q + 1) * T_CTX, :].T


def _l0_in(xp, xs, mods, g, w_in, qg, kg, bd, tabs):
    tab_spec = pl.BlockSpec(
        (TB, LANES), lambda i: (jnp.where(i < NT_CTX, T_LAT // TB, (i - NT_CTX) % (T_LAT // TB)), 0))
    const = lambda shape: pl.BlockSpec(shape, lambda i: (0,) * len(shape))
    return pl.pallas_call(
        _l0_in_body,
        grid=(NT,),
        in_specs=[pl.BlockSpec((TB, D), lambda i: (jnp.minimum(i, NT_CTX - 1), 0)),
                  pl.BlockSpec((TB, D), lambda i: (jnp.maximum(i - NT_CTX, 0), 0)),
                  pl.BlockSpec((1, 1, 6 * D), lambda i: (_mod_row(i, NT_CTX, T_LAT // TB), 0, 0)),
                  const((1, D)), const((D, MIX_IN)), const((1, LANES)), const((1, LANES)),
                  const((LANES, LANES)), tab_spec, tab_spec, tab_spec],
        out_specs=[pl.BlockSpec((TB, ATTN_W), lambda i: (i, 0)),
                   pl.BlockSpec((TB, KV_W), lambda i: (i, 0)),
                   pl.BlockSpec((TB, KV_W), lambda i: (i, 0)),
                   pl.BlockSpec((TB, POOL_W), lambda i: (i, 0)),
                   pl.BlockSpec((TB // T_CTX, KV_W, T_CTX), lambda i: (jnp.minimum(i, NT_CTX - 1), 0, 0)),
                   pl.BlockSpec((TB // T_CTX, KV_W, T_CTX), lambda i: (jnp.minimum(i, NT_CTX - 1), 0, 0))],
        out_shape=[jax.ShapeDtypeStruct((N_TOK, ATTN_W), BF16),
                   jax.ShapeDtypeStruct((N_TOK, KV_W), F32),
                   jax.ShapeDtypeStruct((N_TOK, KV_W), F32),
                   jax.ShapeDtypeStruct((N_TOK, POOL_W), F32),
                   jax.ShapeDtypeStruct((N_CTX_B, KV_W, T_CTX), F32),
                   jax.ShapeDtypeStruct((N_CTX_B, KV_W, T_CTX), F32)],
        scratch_shapes=[pltpu.VMEM((D, MIX_IN), BF16)],
        compiler_params=_cparams(1),
        name="l0_in",
    )(xp, xs, mods, g, w_in, qg, kg, bd, *tabs)


def _head_halves(x):
    z = jnp.zeros_like(x)
    return jnp.concatenate([x, z], axis=1), jnp.concatenate([z, x], axis=1)


_NT_DIMS = (((1,), (1,)), ((), ()))


def _ones_halves(x):
    one = jnp.ones_like(x)
    return jnp.concatenate([x, one], axis=1), jnp.concatenate([one, x], axis=1)


def _sink_attend(scores, values, sk, half):
    mx = sk
    for sc in scores:
        mx = jnp.maximum(mx, jnp.max(sc, axis=-1, keepdims=True))
    acc = None
    for sc, val in zip(scores, values):
        part = jnp.dot(jnp.exp(sc - mx).astype(BF16), val, preferred_element_type=F32)
        acc = part if acc is None else acc + part
    ones_lane = HEAD_DIM * (1 - half)
    den = acc[:, ones_lane:ones_lane + 1] + jnp.exp(sk - mx)
    return acc * (1.0 / den)


def _sink_col(sink_ref, heads, rows):
    return jnp.concatenate([jnp.broadcast_to(sink_ref[h:h + 1, 0:1], (rows, 1)) for h in heads], axis=0)


CTX_SEQS = 2


def _ctx_attn_body(q_ref, k_ref, v_ref, sink_ref, o_ref):
    lo = lax.broadcasted_iota(I32, (T_CTX, LANES), 1) < HEAD_DIM
    for b, j in ((b, j) for b in range(CTX_SEQS) for j in range(KV_W // HEAD_DIM)):
        seq = slice(b * T_CTX, (b + 1) * T_CTX)
        kj = k_ref[seq, j * HEAD_DIM:(j + 1) * HEAD_DIM].astype(BF16)
        vj = v_ref[seq, j * HEAD_DIM:(j + 1) * HEAD_DIM].astype(BF16)
        k_halves = _head_halves(kj)
        vd = jnp.concatenate([vj, vj], axis=1)
        q2 = jnp.concatenate([q_ref[seq, (2 * j) * LANES:(2 * j + 1) * LANES],
                              q_ref[seq, (2 * j + 1) * LANES:(2 * j + 2) * LANES]], axis=0)
        outs = []
        for half in range(2):
            sc = lax.dot_general(q2, k_halves[half], _NT_DIMS, preferred_element_type=F32)
            sk = _sink_col(sink_ref, (4 * j + half, 4 * j + 2 + half), T_CTX)
            mx = jnp.maximum(sk, jnp.max(sc, axis=-1, keepdims=True))
            p = jnp.exp(sc - mx)
            inv = 1.0 / (jnp.exp(sk - mx) + jnp.sum(p, axis=-1, keepdims=True))
            outs.append(jnp.dot((p * inv).astype(BF16), vd, preferred_element_type=F32))
        for s2 in range(2):
            rows = slice(s2 * T_CTX, (s2 + 1) * T_CTX)
            o_ref[seq, (2 * j + s2) * LANES:(2 * j + s2 + 1) * LANES] = (
                jnp.where(lo, outs[0][rows], outs[1][rows]).astype(BF16))


def _ctx_attn(q, k, v, sink_b):
    rows = CTX_SEQS * T_CTX
    return pl.pallas_call(
        _ctx_attn_body,
        grid=(N_CTX_B // CTX_SEQS,),
        in_specs=[pl.BlockSpec((rows, ATTN_W), lambda b: (b, 0)),
                  pl.BlockSpec((rows, KV_W), lambda b: (b, 0)),
                  pl.BlockSpec((rows, KV_W), lambda b: (b, 0)),
                  pl.BlockSpec((SUBLANES, LANES), lambda b: (0, 0))],
        out_specs=pl.BlockSpec((rows, ATTN_W), lambda b: (b, 0)),
        out_shape=jax.ShapeDtypeStruct((N_CTX, ATTN_W), BF16),
        compiler_params=_cparams(1),
        name="ctx_attn",
    )(q, k, v, sink_b)


QB = 128
SPAN = QB + 2 * WINDOW


def _lat_attn_body(q_ref, k_ref, v_ref, ck_ref, cv_ref, sink_ref, o_ref):
    qb = pl.program_id(1)
    start = qb * QB
    kws, vws = [], []
    for c in (-1, 0, 1):
        cs = pl.multiple_of(jnp.clip(start + c * QB, 0, T_LAT - QB), QB)
        kws.append(k_ref[pl.ds(cs, QB), :])
        vws.append(v_ref[pl.ds(cs, QB), :])
    kw = jnp.concatenate(kws, axis=0).astype(BF16)
    vw = jnp.concatenate(vws, axis=0).astype(BF16)
    ck = ck_ref[0].astype(BF16)
    cv = cv_ref[0].astype(BF16)
    qpos = start + (lax.broadcasted_iota(I32, (2 * QB, SPAN), 0) & (QB - 1))
    kpos = start - WINDOW + lax.broadcasted_iota(I32, (2 * QB, SPAN), 1)
    valid = (kpos >= 0) & (kpos < T_LAT) & (jnp.abs(qpos - kpos) <= WINDOW)
    lo = lax.broadcasted_iota(I32, (QB, LANES), 1) < HEAD_DIM
    for j in range(KV_W // HEAD_DIM):
        sl = slice(j * HEAD_DIM, (j + 1) * HEAD_DIM)
        kw_halves = _head_halves(kw[:, sl])
        ck_halves = _head_halves(ck[:, sl])
        vw_halves = _ones_halves(vw[:, sl])
        cv_halves = _ones_halves(cv[:, sl])
        q2 = jnp.concatenate([q_ref[:, (2 * j) * LANES:(2 * j + 1) * LANES],
                              q_ref[:, (2 * j + 1) * LANES:(2 * j + 2) * LANES]], axis=0)
        outs = []
        for half in range(2):
            s_win = lax.dot_general(q2, kw_halves[half], _NT_DIMS, preferred_element_type=F32)
            s_win = jnp.where(valid, s_win, NEG)
            s_ctx = lax.dot_general(q2, ck_halves[half], _NT_DIMS, preferred_element_type=F32)
            sk = _sink_col(sink_ref, (4 * j + half, 4 * j + 2 + half), QB)
            outs.append(_sink_attend([s_win, s_ctx], [vw_halves[half], cv_halves[half]], sk, half))
        for s2 in range(2):
            rows = slice(s2 * QB, (s2 + 1) * QB)
            o_ref[:, (2 * j + s2) * LANES:(2 * j + s2 + 1) * LANES] = (
                jnp.where(lo, outs[0][rows], outs[1][rows]).astype(BF16))


def _lat_attn(q, k, v, ck, cv, sink_b):
    lat0 = N_CTX // T_LAT
    return pl.pallas_call(
        _lat_attn_body,
        grid=(N_LAT_B, T_LAT // QB),
        in_specs=[pl.BlockSpec((QB, ATTN_W), lambda b, i: (N_CTX // QB + b * (T_LAT // QB) + i, 0)),
                  pl.BlockSpec((T_LAT, KV_W), lambda b, i: (lat0 + b, 0)),
                  pl.BlockSpec((T_LAT, KV_W), lambda b, i: (lat0 + b, 0)),
                  pl.BlockSpec((1, PAST, KV_W), lambda b, i: (b, 0, 0)),
                  pl.BlockSpec((1, PAST, KV_W), lambda b, i: (b, 0, 0)),
                  pl.BlockSpec((SUBLANES, LANES), lambda b, i: (0, 0))],
        out_specs=pl.BlockSpec((QB, ATTN_W), lambda b, i: (b * (T_LAT // QB) + i, 0)),
        out_shape=jax.ShapeDtypeStruct((N_LAT, ATTN_W), BF16),
        compiler_params=_cparams(2),
        name="lat_attn",
    )(q, k, v, ck, cv, sink_b)


def _l0_out_body(oc_ref, ol_ref, u_ref, xp_ref, xs_ref, mod_ref, pw_ref, ps_ref, wo_ref,
                 g_ref, wr_ref, br_ref, tri_ref, xext_ref, info_ref, cnt_ref, wb_ref, base_ref):
    i = pl.program_id(0)
    _bf16_once(wo_ref, wb_ref)
    is_ctx = i < N_CTX // TB_MIX
    o = jnp.where(is_ctx, oc_ref[...], ol_ref[...])
    x = jnp.where(is_ctx, xp_ref[...], xs_ref[...])
    tseq = jnp.where(is_ctx, T_CTX, T_LAT)
    pos = lax.broadcasted_iota(I32, (TB_MIX, LANES), 0) & (tseq - 1)
    ys = []
    for g, win in enumerate(POOL_WINDOWS):
        hw = win // 2
        ug = u_ref[:, g * LANES:(g + 1) * LANES]
        acc = ug
        for jj in range(-hw, hw):
            if jj == 0:
                continue
            sh = pltpu.roll(ug, (-jj) % TB_MIX, axis=0)
            ok = (pos + jj >= 0) if jj < 0 else (pos + jj < tseq)
            acc = acc + jnp.where(ok, sh, 0.0)
        cnt = (jnp.minimum(pos + hw, tseq) - jnp.maximum(pos - hw, 0)).astype(F32)
        pooled = acc / cnt - ug
        ys.append(jnp.dot(pooled.astype(BF16), pw_ref[g].astype(BF16), preferred_element_type=F32))
    y = jnp.concatenate(ys, axis=1) * ps_ref[...]
    out = (jnp.dot(o, wb_ref[0:ATTN_W, :], preferred_element_type=F32)
           + jnp.dot(y.astype(BF16), wb_ref[ATTN_W:ATTN_W + POOL_W, :], preferred_element_type=F32))
    x1 = x + mod_ref[0][:, 2 * D:3 * D] * out
    _route_tile(x1, mod_ref[0], _mod_row(i, N_CTX // TB_MIX, 1), g_ref, wr_ref, br_ref, tri_ref,
                xext_ref, info_ref, cnt_ref, base_ref)


def _l0_out(o_ctx, o_lat, u, xp, xs, mods, pool_w, pool_scale, w_out, route_ops):
    ntc = N_CTX // TB_MIX
    r_in, r_out, r_shape, r_scratch = _route_specs(TB_MIX, lambda i: i)
    const = lambda shape: pl.BlockSpec(shape, lambda i: (0,) * len(shape))
    ctx_map = lambda i: (jnp.minimum(i, ntc - 1), 0)
    lat_map = lambda i: (jnp.maximum(i - ntc, 0), 0)
    return pl.pallas_call(
        _l0_out_body,
        grid=(N_TOK // TB_MIX,),
        in_specs=[pl.BlockSpec((TB_MIX, ATTN_W), ctx_map),
                  pl.BlockSpec((TB_MIX, ATTN_W), lat_map),
                  pl.BlockSpec((TB_MIX, POOL_W), lambda i: (i, 0)),
                  pl.BlockSpec((TB_MIX, D), ctx_map),
                  pl.BlockSpec((TB_MIX, D), lat_map),
                  pl.BlockSpec((1, 1, 6 * D), lambda i: (_mod_row(i, ntc, 1), 0, 0)),
                  const((len(POOL_WINDOWS), LANES, LANES)), const((1, POOL_W)), const((D, D))] + r_in,
        out_specs=r_out,
        out_shape=r_shape,
        scratch_shapes=[pltpu.VMEM((D, D), BF16), r_scratch],
        compiler_params=_cparams(1, VMEM_LIMIT),
        name="l0_out",
    )(o_ctx, o_lat, u, xp, xs, mods, pool_w, pool_scale, w_out, *route_ops)


def _first_max(vals):
    best, idx = vals[0], jnp.zeros(vals[0].shape, I32)
    for r in range(1, len(vals)):
        better = vals[r] > best
        idx = jnp.where(better, r, idx)
        best = jnp.where(better, vals[r], best)
    return best, idx


def _softmax_rows(rows):
    mx = functools.reduce(jnp.maximum, rows)
    ex = [jnp.exp(r - mx) for r in rows]
    tot = functools.reduce(lambda a, b: a + b, ex)
    return [e / tot for e in ex]


def _route_tile(x, m, mod_id, g_ref, wr_ref, br_ref, tri_ref, xext_ref, info_ref, cnt_ref, base_ref):
    t_rows = x.shape[0]

    @pl.when(pl.program_id(0) == 0)
    def _():
        base_ref[...] = jnp.zeros_like(base_ref)

    h = _modulate(x, g_ref[...], m[:, 3 * D:4 * D], m[:, 4 * D:5 * D])

    hh = h.astype(BF16)
    hl = (h - hh.astype(F32)).astype(BF16)
    wh, wl = wr_ref[0], wr_ref[1]
    lg = (lax.dot_general(wh, hh, _NT_DIMS, preferred_element_type=F32)
          + lax.dot_general(wl, hh, _NT_DIMS, preferred_element_type=F32)
          + lax.dot_general(wh, hl, _NT_DIMS, preferred_element_type=F32)) + br_ref[:, 0:1]

    pg = _softmax_rows([lg[N_EXPERTS + r:N_EXPERTS + r + 1] for r in range(N_GROUPS)])
    pg_top, gi = _first_max(pg)
    le = []
    for j in range(PER_GROUP):
        sel = lg[(N_GROUPS - 1) * PER_GROUP + j:(N_GROUPS - 1) * PER_GROUP + j + 1]
        for g in range(N_GROUPS - 2, -1, -1):
            sel = jnp.where(gi == g, lg[g * PER_GROUP + j:g * PER_GROUP + j + 1], sel)
        le.append(sel)
    pe = _softmax_rows(le)
    p1, i1 = _first_max(pe)
    p2, i2 = _first_max([jnp.where(i1 == j, -1.0, pe[j]) for j in range(PER_GROUP)])
    den = p1 + p2
    w1 = pg_top * p1 / den
    w2 = pg_top * p2 / den

    lo = jnp.minimum(i1, i2)
    hi = jnp.maximum(i1, i2)
    cls = gi * PAIRS + jnp.where(lo == 0, 0, jnp.where(lo == 1, 3, 5)) + hi - lo - 1
    w_lo = jnp.where(i1 == lo, w1, w2)
    w_hi = jnp.where(i1 == lo, w2, w1)

    crow = lax.broadcasted_iota(I32, (CLASS_ROWS, t_rows), 0)
    hit = crow == cls
    onehot = jnp.where(hit, 1.0, 0.0)
    before = jnp.dot(onehot.astype(BF16), tri_ref[...], preferred_element_type=F32)
    before = before + base_ref[:, 0:1]
    rank = jnp.sum(jnp.where(hit, before, 0.0), axis=0, keepdims=True)
    base_ref[...] = base_ref[...] + jnp.sum(onehot, axis=1, keepdims=True)
    cnt_ref[...] = base_ref[...].astype(I32)

    mod_id = jnp.zeros_like(w1) + mod_id.astype(F32)
    zero = jnp.zeros_like(w1)
    info_ref[...] = jnp.concatenate([cls.astype(F32), rank, zero, zero, zero, zero, zero, zero], axis=0)
    side = jnp.concatenate([w_lo, w_hi, mod_id, jnp.zeros((LANES - 3, t_rows), F32)], axis=0).T
    xext_ref[:, 0:D] = x
    xext_ref[:, D:XEXT] = side


def _route_specs(tile, step_map):
    const = lambda shape: pl.BlockSpec(shape, lambda *i: (0,) * len(shape))
    in_specs = [const((1, D)), const((2, ROUTE_ROWS, D)), const((ROUTE_ROWS, LANES)), const((tile, tile))]
    out_specs = [pl.BlockSpec((tile, XEXT), lambda *i: (step_map(*i), 0)),
                 pl.BlockSpec((SUBLANES, tile), lambda *i: (0, step_map(*i))),
                 const((CLASS_ROWS, LANES))]
    out_shape = [jax.ShapeDtypeStruct((N_TOK, XEXT), F32),
                 jax.ShapeDtypeStruct((SUBLANES, N_TOK), F32),
                 jax.ShapeDtypeStruct((CLASS_ROWS, LANES), I32)]
    return in_specs, out_specs, out_shape, pltpu.VMEM((CLASS_ROWS, LANES), F32)


def _route_operands(g, wr, br, tile):
    tri = jnp.asarray(np.arange(tile)[:, None] < np.arange(tile)[None, :], BF16)
    return g, wr, br, tri


def _plan_body(cnt_ref, info_ref, pos_ref, lo_ref, hi_ref, sa_ref, sb_ref, newa_ref, newb_ref, nu_ref, ch_ref,
               toff_ref, seen_ref):
    tm_shift = TM.bit_length() - 1

    def per_class(c, first_tile):
        cnt = cnt_ref[c, 0]
        tiles = lax.shift_right_logical(cnt + (TM - 1), tm_shift)
        toff_ref[c] = first_tile * TM
        group = lax.div(c, PAIRS)
        pair = c - group * PAIRS
        lo = (pair >= 3).astype(I32) + (pair >= 5).astype(I32)
        hi = jnp.where(pair < 3, pair + 1, jnp.where(pair < 5, pair - 1, 3))

        def per_tile(k, carry):
            n = first_tile + k
            lo_ref[n] = lo
            hi_ref[n] = hi
            sa_ref[n] = group * PER_GROUP + lo
            sb_ref[n] = group * PER_GROUP + hi
            rows = jnp.minimum(cnt - k * TM, TM)
            ch_ref[n] = lax.shift_right_logical(rows + (CH - 1), CH.bit_length() - 1)
            return carry

        lax.fori_loop(0, tiles, per_tile, 0)
        return first_tile + tiles

    n_used = lax.fori_loop(0, N_CLASS, per_class, 0)
    nu_ref[0] = n_used

    for e in range(N_EXPERTS):
        seen_ref[e] = 0

    def per_used(n, slots):
        e_lo, e_hi = sa_ref[n], sb_ref[n]
        new_a = seen_ref[e_lo] == 0
        seen_ref[e_lo] = 1
        new_b = seen_ref[e_hi] == 0
        seen_ref[e_hi] = 1
        newa_ref[n] = new_a.astype(I32)
        newb_ref[n] = new_b.astype(I32)
        slot_a = jnp.where(new_a, e_lo, slots[0])
        slot_b = jnp.where(new_b, e_hi, slots[1])
        sa_ref[n] = slot_a
        sb_ref[n] = slot_b
        return slot_a, slot_b

    slot_a, slot_b = lax.fori_loop(0, n_used, per_used, (jnp.int32(0), jnp.int32(0)))

    def unused(n, carry):
        lo_ref[n] = 0
        hi_ref[n] = 0
        sa_ref[n] = slot_a
        sb_ref[n] = slot_b
        newa_ref[n] = 0
        newb_ref[n] = 0
        ch_ref[n] = 0
        return carry

    lax.fori_loop(n_used, NT_FFN, unused, 0)

    cls = info_ref[0:1, :]
    first_row = jnp.zeros(cls.shape, I32)
    for c in range(N_CLASS):
        first_row = jnp.where(cls == float(c), toff_ref[c], first_row)
    pos = first_row + info_ref[1:2, :].astype(I32)
    pos_ref[...] = jnp.broadcast_to(pos, pos_ref.shape)


def _plan(info, counts, layer):
    smem = pl.BlockSpec(memory_space=pltpu.SMEM)
    per_tile = jax.ShapeDtypeStruct((NT_FFN,), I32)
    pos, lo, hi, sa, sb, newa, newb, n_used, chunks = pl.pallas_call(
        _plan_body,
        grid=(1,),
        in_specs=[smem, pl.BlockSpec((SUBLANES, N_TOK), lambda i: (0, 0))],
        out_specs=[pl.BlockSpec((SUBLANES, N_TOK), lambda i: (0, 0))] + [smem] * 8,
        out_shape=[jax.ShapeDtypeStruct((SUBLANES, N_TOK), I32)] + [per_tile] * 6
                  + [jax.ShapeDtypeStruct((1,), I32), per_tile],
        scratch_shapes=[pltpu.SMEM((CLASS_ROWS,), I32), pltpu.SMEM((N_EXPERTS,), I32)],
        compiler_params=_cparams(1),
        name=f"plan{layer}",
    )(counts, info)
    return pos[0], lo, hi, (sa, sb, newa, newb), n_used, chunks


OCT = TM // SUBLANES


def _ffn_body(pos_ref, lo_ref, hi_ref, sa_ref, sb_ref, newa_ref, newb_ref, nu_ref, ch_ref,
              xext_hbm, mod_ref, g_ref, w1a_ref, w1b_ref, w3a_ref, w3b_ref, w2a_ref, w2b_ref,
              out_hbm, src_ref, dst_ref, xbuf, ybuf, wb1, wb3, wb2, gsem, ssem):
    n = pl.program_id(0)
    n_used = nu_ref[0]

    def gather_copy(tile, s, c, j):
        return pltpu.make_async_copy(
            xext_hbm.at[pl.ds(src_ref[tile * TM + c * CH + j], 1)],
            xbuf.at[s, c * (CH // SUBLANES) + j // SUBLANES, pl.ds(j % SUBLANES, 1)], gsem.at[s])

    def scatter_copy(tile, s, c, j):
        return pltpu.make_async_copy(
            ybuf.at[s, c * (CH // SUBLANES) + j // SUBLANES, pl.ds(j % SUBLANES, 1)],
            out_hbm.at[pl.ds(dst_ref[tile * TM + c * CH + j], 1)], ssem.at[s])

    def start_rows(copy, tile, s):
        def chunk(c, carry):
            for j in range(CH):
                copy(tile, s, c, j).start()
            return carry
        lax.fori_loop(0, ch_ref[tile], chunk, 0)

    def wait_rows(src, dst, sem, tile):
        def chunk(c, carry):
            pltpu.make_async_copy(src, dst, sem).wait()
            return carry
        lax.fori_loop(0, ch_ref[tile], chunk, 0)

    def wait_gather(tile, s):
        rows = xbuf.at[s, pl.ds(0, CH // SUBLANES)]
        wait_rows(rows, rows, gsem.at[s], tile)

    def wait_scatter(tile, s):
        rows = ybuf.at[s, pl.ds(0, CH // SUBLANES)]
        wait_rows(rows, rows, ssem.at[s], tile)

    @pl.when(n == 0)
    def _():
        def pad_rows(tile, carry):
            @pl.when(ch_ref[tile] > 0)
            def _():
                first = tile * TM + (ch_ref[tile] - 1) * CH
                for j in range(CH):
                    src_ref[first + j] = 0
                    dst_ref[first + j] = N_TOK + ((first + j) & (2 * TM - 1))
            return carry

        lax.fori_loop(0, NT_FFN, pad_rows, 0)

        def put(t, carry):
            p = pos_ref[t]
            src_ref[p] = t
            dst_ref[p] = t
            return carry

        lax.fori_loop(0, N_TOK, put, 0, unroll=8)

        xbuf[...] = jnp.zeros_like(xbuf)
        ybuf[...] = jnp.zeros_like(ybuf)
        for s in range(2):
            dumps = [pltpu.make_async_copy(
                ybuf.at[s, q], out_hbm.at[pl.ds(N_TOK + s * TM + q * SUBLANES, SUBLANES)], ssem.at[s])
                for q in range(OCT)]
            for dump in dumps:
                dump.start()
            for dump in dumps:
                dump.wait()
        start_rows(gather_copy, 0, 0)

    def step(slot):
        @pl.when(n + 1 < n_used)
        def _():
            start_rows(gather_copy, n + 1, 1 - slot)

        wait_gather(n, slot)

        @pl.when(n >= 2)
        def _():
            wait_scatter(n - 2, slot)

        e_lo = lo_ref[n]
        e_hi = hi_ref[n]

        @pl.when(newa_ref[n] == 1)
        def _():
            wb1[e_lo] = w1a_ref[0, 0].astype(BF16)
            wb3[e_lo] = w3a_ref[0, 0].astype(BF16)
            wb2[e_lo] = w2a_ref[0, 0].astype(BF16)

        @pl.when(newb_ref[n] == 1)
        def _():
            wb1[e_hi] = w1b_ref[0, 0].astype(BF16)
            wb3[e_hi] = w3b_ref[0, 0].astype(BF16)
            wb2[e_hi] = w2b_ref[0, 0].astype(BF16)

        xe = xbuf[slot].reshape(TM, XEXT)
        x = xe[:, 0:D]
        w_lo = xe[:, D:D + 1]
        w_hi = xe[:, D + 1:D + 2]
        mod_id = xe[:, D + 2:D + 3]

        def pick(lo, hi):
            return jnp.where(mod_id < 0.5, mod_ref[0][:, lo:hi],
                             jnp.where(mod_id < 1.5, mod_ref[1][:, lo:hi], mod_ref[2][:, lo:hi]))

        h = _modulate(x, g_ref[...], pick(3 * D, 4 * D), pick(4 * D, 5 * D)).astype(BF16)

        def act(e, w):
            h1 = jnp.dot(h, wb1[e], preferred_element_type=F32)
            h3 = jnp.dot(h, wb3[e], preferred_element_type=F32)
            return ((h1 * jax.nn.sigmoid(h1)) * h3 * w).astype(BF16)

        y = (jnp.dot(act(e_lo, w_lo), wb2[e_lo], preferred_element_type=F32)
             + jnp.dot(act(e_hi, w_hi), wb2[e_hi], preferred_element_type=F32))
        ybuf[slot] = (x + pick(5 * D, 6 * D) * y).reshape(OCT, SUBLANES, D)
        start_rows(scatter_copy, n, slot)

        @pl.when(n == n_used - 1)
        def _():
            @pl.when(n >= 1)
            def _():
                wait_scatter(n - 1, 1 - slot)
            wait_scatter(n, slot)

    for s in range(2):
        @pl.when((n < n_used) & (n % 2 == s))
        def _():
            step(s)


def _ffn(pos, lo, hi, slots, n_used, chunks, xext, mods, layer, g, w1, w3, w2):
    a_map = lambda n, p, lo, hi, sa, sb, na, nb, nu, ch: (layer, sa[n], 0, 0)
    b_map = lambda n, p, lo, hi, sa, sb, na, nb, nu, ch: (layer, sb[n], 0, 0)
    up = lambda imap: pl.BlockSpec((1, 1, D, D_EXPERT), imap)
    down = lambda imap: pl.BlockSpec((1, 1, D_EXPERT, D), imap)
    return pl.pallas_call(
        _ffn_body,
        grid_spec=pltpu.PrefetchScalarGridSpec(
            num_scalar_prefetch=9, grid=(NT_FFN,),
            in_specs=[pl.BlockSpec(memory_space=pl.ANY),
                      pl.BlockSpec((SUBLANES, 1, 6 * D), lambda n, *_: (layer, 0, 0)),
                      pl.BlockSpec((1, D), lambda n, *_: (0, 0)),
                      up(a_map), up(b_map), up(a_map), up(b_map), down(a_map), down(b_map)],
            out_specs=pl.BlockSpec(memory_space=pl.ANY),
            scratch_shapes=[pltpu.SMEM((P_FFN,), I32), pltpu.SMEM((P_FFN,), I32),
                            pltpu.VMEM((2, OCT, SUBLANES, XEXT), F32),
                            pltpu.VMEM((2, OCT, SUBLANES, D), F32),
                            pltpu.VMEM((PER_GROUP, D, D_EXPERT), BF16),
                            pltpu.VMEM((PER_GROUP, D, D_EXPERT), BF16),
                            pltpu.VMEM((PER_GROUP, D_EXPERT, D), BF16),
                            pltpu.SemaphoreType.DMA((2,)), pltpu.SemaphoreType.DMA((2,))]),
        out_shape=jax.ShapeDtypeStruct((N_TOK + 2 * TM, D), F32),
        compiler_params=_cparams(1, VMEM_LIMIT),
        name=f"ffn{layer}",
    )(pos, lo, hi, *slots, n_used, chunks, xext, mods, g, w1, w1, w3, w3, w2, w2)


def _moe(routed, mods, layer, g, w1, w3, w2):
    xext, info, counts = routed
    pos, lo, hi, slots, n_used, chunks = _plan(info, counts, layer)
    return _ffn(pos, lo, hi, slots, n_used, chunks, xext, mods, layer, g, w1, w3, w2)


FG = 256


def _l1_in_body(x_ref, mod_ref, g_ref, w_ref, c_ref, s_ref, zc_ref, zs_ref, wb_ref):
    _bf16_once(w_ref, wb_ref)
    m = mod_ref[0]
    h = _modulate(x_ref[...], g_ref[...], m[:, 0:D], m[:, D:2 * D])
    z = jnp.dot(h.astype(BF16), wb_ref[...], preferred_element_type=F32).astype(BF16)
    for g in range(D // FG):
        zg = z[:, g * FG:(g + 1) * FG]
        zc_ref[:, g * FG:(g + 1) * FG] = jnp.dot(zg, c_ref[...], preferred_element_type=F32).astype(BF16)
        zs_ref[:, g * FG:(g + 1) * FG] = jnp.dot(zg, s_ref[...], preferred_element_type=F32).astype(BF16)


def _l1_in(x, mods, g, w, c256, s256):
    const = lambda shape: pl.BlockSpec(shape, lambda i: (0,) * len(shape))
    return pl.pallas_call(
        _l1_in_body,
        grid=(NT,),
        in_specs=[pl.BlockSpec((TB, D), lambda i: (i, 0)),
                  pl.BlockSpec((1, 1, 6 * D),
                               lambda i: (SUBLANES + _mod_row(i, NT_CTX, T_LAT // TB), 0, 0)),
                  const((1, D)), const((D, D)), const((FG, FG)), const((FG, FG))],
        out_specs=[pl.BlockSpec((TB, D), lambda i: (i, 0)), pl.BlockSpec((TB, D), lambda i: (i, 0))],
        out_shape=[jax.ShapeDtypeStruct((N_TOK, D), BF16), jax.ShapeDtypeStruct((N_TOK, D), BF16)],
        scratch_shapes=[pltpu.VMEM((D, D), BF16)],
        compiler_params=_cparams(1),
        name="l1_in",
    )(x, mods, g, w, c256, s256)


def _l1_out_body(zc_t_ref, zs_t_ref, zc_q_ref, zs_q_ref, c256_ref, s256_ref, c1k_ref, s1k_ref,
                 x_ref, mod_ref, wo_ref, g_ref, wr_ref, br_ref, tri_ref,
                 xext_ref, info_ref, cnt_ref, f_ref, wb_ref, base_ref):
    i = pl.program_id(0)
    _bf16_once(wo_ref, wb_ref)

    @pl.when(i < NT_CTX)
    def _():
        for q in range(TB // T_CTX):
            rows = slice(q * T_CTX, (q + 1) * T_CTX)
            f = (jnp.dot(c256_ref[...], zc_t_ref[rows, :], preferred_element_type=F32)
                 - jnp.dot(s256_ref[...], zs_t_ref[rows, :], preferred_element_type=F32))
            f_ref[rows, :] = f.astype(BF16)

    @pl.when(i >= NT_CTX)
    def _():
        f = (jnp.dot(c1k_ref[...], zc_q_ref[...], preferred_element_type=F32)
             - jnp.dot(s1k_ref[...], zs_q_ref[...], preferred_element_type=F32))
        f_ref[...] = f.astype(BF16)

    out = jnp.dot(f_ref[...], wb_ref[...], preferred_element_type=F32)
    x3 = x_ref[...] + mod_ref[0][:, 2 * D:3 * D] * out
    _route_tile(x3, mod_ref[0], _mod_row(i, NT_CTX, T_LAT // TB), g_ref, wr_ref, br_ref, tri_ref,
                xext_ref, info_ref, cnt_ref, base_ref)


def _l1_out(zc, zs, c256, s256, c1k, s1k, x, mods, w_out, route_ops):
    const = lambda shape: pl.BlockSpec(shape, lambda i: (0,) * len(shape))
    r_in, r_out, r_shape, r_scratch = _route_specs(TB, lambda i: i)
    tile_map = lambda i: (jnp.minimum(i, NT_CTX - 1), 0)
    seq_map = lambda i: (N_CTX // T_LAT + jnp.maximum(i - NT_CTX, 0) // (T_LAT // TB), 0)
    row_map = lambda i: (jnp.maximum(i - NT_CTX, 0) % (T_LAT // TB), 0)
    return pl.pallas_call(
        _l1_out_body,
        grid=(NT,),
        in_specs=[pl.BlockSpec((TB, D), tile_map), pl.BlockSpec((TB, D), tile_map),
                  pl.BlockSpec((T_LAT, D), seq_map), pl.BlockSpec((T_LAT, D), seq_map),
                  const((T_CTX, T_CTX)), const((T_CTX, T_CTX)),
                  pl.BlockSpec((TB, T_LAT), row_map), pl.BlockSpec((TB, T_LAT), row_map),
                  pl.BlockSpec((TB, D), lambda i: (i, 0)),
                  pl.BlockSpec((1, 1, 6 * D),
                               lambda i: (SUBLANES + _mod_row(i, NT_CTX, T_LAT // TB), 0, 0)),
                  const((D, D))] + r_in,
        out_specs=r_out,
        out_shape=r_shape,
        scratch_shapes=[pltpu.VMEM((TB, D), BF16), pltpu.VMEM((D, D), BF16), r_scratch],
        compiler_params=_cparams(1, VMEM_LIMIT),
        name="l1_out",
    )(zc, zs, zc, zs, c256, s256, c1k, s1k, x, mods, w_out, *route_ops)


def kernel(x_prompt, x_sample, cache_k, cache_v, c, c_ctx, ada_w, ada_b, norm_mix, norm_ffn, a_w_in, a_q_norm, a_k_norm, a_sink, pool_w, pool_scale, a_w_out, f_w_in, f_w_out, router_g_w, router_g_b, router_e_w, router_e_b, moe_w1, moe_w3, moe_w2):
    xp = x_prompt.reshape(N_CTX, D)
    xs = x_sample.reshape(N_LAT, D)

    cond8 = jnp.concatenate([c_ctx[None, :], c, jnp.zeros((SUBLANES - 1 - N_LAT_B, D), F32)], axis=0)
    mods = _adaln(cond8, ada_w, ada_b).reshape(DEPTH * SUBLANES, 1, 6 * D)

    tabs = _rope_tables()
    lane = np.arange(LANES)
    bd = jnp.asarray((lane[:, None] // HEAD_DIM) == (lane[None, :] // HEAD_DIM), BF16)
    c256, s256 = _dft_tables(T_CTX)
    c1k, s1k = _dft_tables(T_LAT)

    rw = jnp.swapaxes(jnp.concatenate([router_e_w, router_g_w], axis=2), 1, 2)
    rw = jnp.pad(rw, ((0, 0), (0, ROUTE_ROWS - rw.shape[1]), (0, 0)))
    rw_hi = rw.astype(BF16)
    rw_split = jnp.stack([rw_hi, (rw - rw_hi.astype(F32)).astype(BF16)], axis=1)
    rb = jnp.pad(jnp.concatenate([router_e_b, router_g_b], axis=1), ((0, 0), (0, ROUTE_ROWS - N_EXPERTS - N_GROUPS)))
    rb = jnp.broadcast_to(rb[:, :, None], (DEPTH, ROUTE_ROWS, LANES))

    def router_operands(l):
        return rw_split[l], rb[l]

    qg = jnp.tile(a_q_norm[0], LANES // HEAD_DIM)[None, :]
    kg = jnp.tile(a_k_norm[0], LANES // HEAD_DIM)[None, :]
    q, k, v, u, new_k, new_v = _l0_in(xp, xs, mods, norm_mix[0][None, :], a_w_in[0], qg, kg, bd, tabs)
    sink_b = jnp.broadcast_to(a_sink[0][:, None], (N_HEADS, LANES))
    o_ctx = _ctx_attn(q, k, v, sink_b)
    ck = cache_k[:, 0].reshape(N_LAT_B, PAST, KV_W)
    cv = cache_v[:, 0].reshape(N_LAT_B, PAST, KV_W)
    o_lat = _lat_attn(q, k, v, ck, cv, sink_b)
    routed = _l0_out(o_ctx, o_lat, u, xp, xs, mods, pool_w[0], pool_scale[0][None, :], a_w_out[0],
                     _route_operands(norm_ffn[0][None, :], *router_operands(0), TB_MIX))
    x2 = _moe(routed, mods, 0, norm_ffn[0][None, :], moe_w1, moe_w3, moe_w2)

    zc, zs = _l1_in(x2, mods, norm_mix[1][None, :], f_w_in[0], c256, s256)
    routed = _l1_out(zc, zs, c256, s256, c1k, s1k, x2, mods, f_w_out[0],
                     _route_operands(norm_ffn[1][None, :], *router_operands(1), TB))
    x4 = _moe(routed, mods, 1, norm_ffn[1][None, :], moe_w1, moe_w3, moe_w2)

    def cache_entry(t):
        t = t.reshape(N_CTX_B, 1, KV_W // HEAD_DIM, HEAD_DIM, T_CTX)
        return jnp.transpose(t, (0, 1, 4, 2, 3))

    new_k, new_v = cache_entry(new_k), cache_entry(new_v)
    return (x4[:N_CTX].reshape(N_CTX_B, T_CTX, D), x4[N_CTX:N_TOK].reshape(N_LAT_B, T_LAT, D),
            new_k, new_v)
```

```python
import functools

import numpy as np
import jax
import jax.numpy as jnp
from jax import lax
from jax.experimental import pallas as pl
from jax.experimental.pallas import tpu as pltpu

F32 = jnp.float32
BF16 = jnp.bfloat16
I32 = jnp.int32

D = 1024
DEPTH = 2
N_CTX_B, T_CTX = 16, 256
N_LAT_B, T_LAT = 2, 1024
N_CTX = N_CTX_B * T_CTX
N_LAT = N_LAT_B * T_LAT
N_TOK = N_CTX + N_LAT
PAST = 512
GRID_W = 64
HEAD_DIM = 64
N_HEADS = 8
ATTN_W = 512
KV_W = 128
POOL_W = 512
POOL_WINDOWS = (2, 4, 8, 16)
MIX_IN = ATTN_W + 2 * KV_W + POOL_W
WINDOW = 128
N_GROUPS = 4
PER_GROUP = 4
N_EXPERTS = 16
D_EXPERT = 512
ROPE_THETA = 10000.0
EPS = 1e-6
NEG = -1e30

LANES = 128
SUBLANES = 8
TB = 512
NT = N_TOK // TB
NT_CTX = N_CTX // TB
TB_MIX = 1024
TM = 256
PAIRS = 6
N_CLASS = N_GROUPS * PAIRS
CLASS_ROWS = 32
NT_FFN = N_TOK // TM + N_CLASS
P_FFN = NT_FFN * TM
CH = 32
XEXT = D + LANES
ROUTE_ROWS = 32

VMEM_LIMIT = 56 * 1024 * 1024


def _cparams(n_axes=1, vmem=None):
    return pltpu.CompilerParams(dimension_semantics=("arbitrary",) * n_axes,
                                vmem_limit_bytes=vmem)


def _modulate(x, g, shift, scale):
    ms = jnp.mean(x * x, axis=-1, keepdims=True)
    return (x * lax.rsqrt(ms + EPS) * g) * (1.0 + scale) + shift


def _bf16_once(w_ref, wb_ref):
    @pl.when(pl.program_id(0) == 0)
    def _():
        wb_ref[...] = w_ref[...].astype(BF16)


def _mod_row(tile, tiles_ctx, tiles_per_lat):
    return (tile >= tiles_ctx).astype(I32) + (tile >= tiles_ctx + tiles_per_lat).astype(I32)


def _rope_tables():
    t = np.arange(T_LAT)
    row = (t // GRID_W).astype(np.float64)
    col = (t % GRID_W).astype(np.float64)
    nf = HEAD_DIM // 4
    freqs = ROPE_THETA ** (-np.arange(nf, dtype=np.float64) / nf)
    d = np.arange(HEAD_DIM)
    pos = np.where(d[None, :] < HEAD_DIM // 2, row[:, None], col[:, None])
    ang = pos * freqs[d % nf][None, :]
    first = (d % (HEAD_DIM // 2)) < nf
    cos = np.cos(ang)
    sin_a = np.where(first[None, :], -np.sin(ang), 0.0)
    sin_b = np.where(first[None, :], 0.0, np.sin(ang))
    ident = (np.ones((TB, HEAD_DIM)), np.zeros((TB, HEAD_DIM)), np.zeros((TB, HEAD_DIM)))
    out = []
    for tab, idt in zip((cos, sin_a, sin_b), ident):
        full = np.concatenate([tab, idt], axis=0)
        out.append(jnp.asarray(np.tile(full, (1, LANES // HEAD_DIM)), F32))
    return out


def _dft_tables(t):
    m = np.outer(np.arange(t), np.arange(t)) % t
    ang = 2.0 * np.pi * m / t
    s = 1.0 / np.sqrt(t)
    return jnp.asarray(np.cos(ang) * s, F32).astype(BF16), jnp.asarray(np.sin(ang) * s, F32).astype(BF16)


def _adaln_body(cctx_ref, c_ref, w_ref, b_ref, o_ref):
    layer = pl.program_id(0)
    c = jnp.concatenate([cctx_ref[...], c_ref[...], jnp.zeros((SUBLANES - 1 - N_LAT_B, D), F32)], axis=0)
    s = (c * jax.nn.sigmoid(c)).astype(BF16)
    m = jnp.dot(s, w_ref[0].astype(BF16), preferred_element_type=F32) + b_ref[pl.ds(layer, 1), :]
    for r in range(SUBLANES):
        o_ref[r] = m[r:r + 1]


def _adaln(c_ctx, c, ada_w, ada_b):
    tn = 1536
    return pl.pallas_call(
        _adaln_body,
        grid=(DEPTH, 6 * D // tn),
        in_specs=[pl.BlockSpec((1, D), lambda l, j: (0, 0)),
                  pl.BlockSpec((N_LAT_B, D), lambda l, j: (0, 0)),
                  pl.BlockSpec((1, D, tn), lambda l, j: (l, 0, j)),
                  pl.BlockSpec((DEPTH, tn), lambda l, j: (0, j))],
        out_specs=pl.BlockSpec((SUBLANES, 1, tn), lambda l, j: (l, 0, j)),
        out_shape=jax.ShapeDtypeStruct((DEPTH * SUBLANES, 1, 6 * D), F32),
        compiler_params=_cparams(2),
        name="adaln",
    )(c_ctx.reshape(1, D), c, ada_w, ada_b)


def _l0_in_body(xp_ref, xs_ref, mod_ref, g_ref, w_ref, qg_ref, kg_ref, bd_ref,
                cos_ref, sa_ref, sb_ref, q_ref, k_ref, v_ref, u_ref, kc_ref, vc_ref, wb_ref):
    i = pl.program_id(0)
    _bf16_once(w_ref, wb_ref)
    x = jnp.where(i < NT_CTX, xp_ref[...], xs_ref[...])
    m = mod_ref[0]
    h = _modulate(x, g_ref[0:1, :], m[:, 0:D], m[:, D:2 * D])
    z = jnp.dot(h.astype(BF16), wb_ref[...], preferred_element_type=F32)
    cos, sa, sb, bd = cos_ref[...], sa_ref[...], sb_ref[...], bd_ref[...]
    qg = jnp.concatenate([qg_ref[...]] * (LANES // HEAD_DIM), axis=1)
    kg = jnp.concatenate([kg_ref[...]] * (LANES // HEAD_DIM), axis=1)

    def head_norm_rope(zz, gain):
        ss = jnp.dot((zz * zz).astype(BF16), bd, preferred_element_type=F32)
        y = zz * lax.rsqrt(ss * (1.0 / HEAD_DIM) + EPS) * gain
        return (y * cos + pltpu.roll(y, LANES - 16, axis=1) * sa
                + pltpu.roll(y, 16, axis=1) * sb)

    for s in range(ATTN_W // LANES):
        qs = head_norm_rope(z[:, s * LANES:(s + 1) * LANES], qg)
        q_ref[:, s * LANES:(s + 1) * LANES] = (qs * (HEAD_DIM ** -0.5)).astype(BF16)
    k = head_norm_rope(z[:, ATTN_W:ATTN_W + KV_W], kg)
    v = z[:, ATTN_W + KV_W:ATTN_W + 2 * KV_W]
    k_ref[...] = k
    v_ref[...] = v
    u_ref[...] = z[:, ATTN_W + 2 * KV_W:MIX_IN]

    @pl.when(i < NT_CTX)
    def _():
        for q in range(TB // T_CTX):
            kc_ref[q] = k[q * T_CTX:(q + 1) * T_CTX, :].T
            vc_ref[q] = v[q * T_CTX:(q + 1) * T_CTX, :].T


def _l0_in(xp, xs, mods, g, w_in, qg, kg, bd, tabs):
    tab_spec = pl.BlockSpec(
        (TB, LANES), lambda i: (jnp.where(i < NT_CTX, T_LAT // TB, (i - NT_CTX) % (T_LAT // TB)), 0))
    const = lambda shape: pl.BlockSpec(shape, lambda i: (0,) * len(shape))
    return pl.pallas_call(
        _l0_in_body,
        grid=(NT,),
        in_specs=[pl.BlockSpec((TB, D), lambda i: (jnp.minimum(i, NT_CTX - 1), 0)),
                  pl.BlockSpec((TB, D), lambda i: (jnp.maximum(i - NT_CTX, 0), 0)),
                  pl.BlockSpec((1, 1, 6 * D), lambda i: (_mod_row(i, NT_CTX, T_LAT // TB), 0, 0)),
                  const((DEPTH, D)), const((D, MIX_IN)), const((1, HEAD_DIM)), const((1, HEAD_DIM)),
                  const((LANES, LANES)), tab_spec, tab_spec, tab_spec],
        out_specs=[pl.BlockSpec((TB, ATTN_W), lambda i: (i, 0)),
                   pl.BlockSpec((TB, KV_W), lambda i: (i, 0)),
                   pl.BlockSpec((TB, KV_W), lambda i: (i, 0)),
                   pl.BlockSpec((TB, POOL_W), lambda i: (i, 0)),
                   pl.BlockSpec((TB // T_CTX, KV_W, T_CTX), lambda i: (jnp.minimum(i, NT_CTX - 1), 0, 0)),
                   pl.BlockSpec((TB // T_CTX, KV_W, T_CTX), lambda i: (jnp.minimum(i, NT_CTX - 1), 0, 0))],
        out_shape=[jax.ShapeDtypeStruct((N_TOK, ATTN_W), BF16),
                   jax.ShapeDtypeStruct((N_TOK, KV_W), F32),
                   jax.ShapeDtypeStruct((N_TOK, KV_W), F32),
                   jax.ShapeDtypeStruct((N_TOK, POOL_W), F32),
                   jax.ShapeDtypeStruct((N_CTX_B, KV_W, T_CTX), F32),
                   jax.ShapeDtypeStruct((N_CTX_B, KV_W, T_CTX), F32)],
        scratch_shapes=[pltpu.VMEM((D, MIX_IN), BF16)],
        compiler_params=_cparams(1),
        name="l0_in",
    )(xp, xs, mods, g, w_in, qg, kg, bd, *tabs)


def _head_halves(x):
    z = jnp.zeros_like(x)
    return jnp.concatenate([x, z], axis=1), jnp.concatenate([z, x], axis=1)


_NT_DIMS = (((1,), (1,)), ((), ()))


def _ones_halves(x):
    one = jnp.ones_like(x)
    return jnp.concatenate([x, one], axis=1), jnp.concatenate([one, x], axis=1)


def _sink_attend(scores, values, sk, half):
    mx = sk
    for sc in scores:
        mx = jnp.maximum(mx, jnp.max(sc, axis=-1, keepdims=True))
    acc = None
    for sc, val in zip(scores, values):
        part = jnp.dot(jnp.exp(sc - mx).astype(BF16), val, preferred_element_type=F32)
        acc = part if acc is None else acc + part
    ones_lane = HEAD_DIM * (1 - half)
    den = acc[:, ones_lane:ones_lane + 1] + jnp.exp(sk - mx)
    return acc * (1.0 / den)


def _sink_col(sink_ref, heads, rows):
    return jnp.concatenate([jnp.full((rows, 1), sink_ref[0, h], F32) for h in heads], axis=0)


CTX_SEQS = 2


def _ctx_attn_body(q_ref, k_ref, v_ref, sink_ref, o_ref):
    lo = lax.broadcasted_iota(I32, (T_CTX, LANES), 1) < HEAD_DIM
    for b, j in ((b, j) for b in range(CTX_SEQS) for j in range(KV_W // HEAD_DIM)):
        seq = slice(b * T_CTX, (b + 1) * T_CTX)
        kj = k_ref[seq, j * HEAD_DIM:(j + 1) * HEAD_DIM].astype(BF16)
        vj = v_ref[seq, j * HEAD_DIM:(j + 1) * HEAD_DIM].astype(BF16)
        k_halves = _head_halves(kj)
        vd = jnp.concatenate([vj, vj], axis=1)
        q2 = jnp.concatenate([q_ref[seq, (2 * j) * LANES:(2 * j + 1) * LANES],
                              q_ref[seq, (2 * j + 1) * LANES:(2 * j + 2) * LANES]], axis=0)
        outs = []
        for half in range(2):
            sc = lax.dot_general(q2, k_halves[half], _NT_DIMS, preferred_element_type=F32)
            sk = _sink_col(sink_ref, (4 * j + half, 4 * j + 2 + half), T_CTX)
            mx = jnp.maximum(sk, jnp.max(sc, axis=-1, keepdims=True))
            p = jnp.exp(sc - mx)
            inv = 1.0 / (jnp.exp(sk - mx) + jnp.sum(p, axis=-1, keepdims=True))
            outs.append(jnp.dot((p * inv).astype(BF16), vd, preferred_element_type=F32))
        for s2 in range(2):
            rows = slice(s2 * T_CTX, (s2 + 1) * T_CTX)
            o_ref[seq, (2 * j + s2) * LANES:(2 * j + s2 + 1) * LANES] = (
                jnp.where(lo, outs[0][rows], outs[1][rows]).astype(BF16))


def _ctx_attn(q, k, v, sink):
    rows = CTX_SEQS * T_CTX
    return pl.pallas_call(
        _ctx_attn_body,
        grid=(N_CTX_B // CTX_SEQS,),
        in_specs=[pl.BlockSpec((rows, ATTN_W), lambda b: (b, 0)),
                  pl.BlockSpec((rows, KV_W), lambda b: (b, 0)),
                  pl.BlockSpec((rows, KV_W), lambda b: (b, 0)),
                  pl.BlockSpec(memory_space=pltpu.SMEM)],
        out_specs=pl.BlockSpec((rows, ATTN_W), lambda b: (b, 0)),
        out_shape=jax.ShapeDtypeStruct((N_CTX, ATTN_W), BF16),
        compiler_params=_cparams(1),
        name="ctx_attn",
    )(q, k, v, sink)


QB = 128
SPAN = QB + 2 * WINDOW


def _lat_attn_body(q_ref, k_ref, v_ref, ck_ref, cv_ref, sink_ref, o_ref):
    qb = pl.program_id(1)
    start = qb * QB
    kws, vws = [], []
    for c in (-1, 0, 1):
        cs = pl.multiple_of(jnp.clip(start + c * QB, 0, T_LAT - QB), QB)
        kws.append(k_ref[pl.ds(cs, QB), :])
        vws.append(v_ref[pl.ds(cs, QB), :])
    kw = jnp.concatenate(kws, axis=0).astype(BF16)
    vw = jnp.concatenate(vws, axis=0).astype(BF16)
    ck = ck_ref[0].astype(BF16)
    cv = cv_ref[0].astype(BF16)
    qpos = start + (lax.broadcasted_iota(I32, (2 * QB, SPAN), 0) & (QB - 1))
    kpos = start - WINDOW + lax.broadcasted_iota(I32, (2 * QB, SPAN), 1)
    valid = (kpos >= 0) & (kpos < T_LAT) & (jnp.abs(qpos - kpos) <= WINDOW)
    lo = lax.broadcasted_iota(I32, (QB, LANES), 1) < HEAD_DIM
    for j in range(KV_W // HEAD_DIM):
        sl = slice(j * HEAD_DIM, (j + 1) * HEAD_DIM)
        kw_halves = _head_halves(kw[:, sl])
        ck_halves = _head_halves(ck[:, sl])
        vw_halves = _ones_halves(vw[:, sl])
        cv_halves = _ones_halves(cv[:, sl])
        q2 = jnp.concatenate([q_ref[:, (2 * j) * LANES:(2 * j + 1) * LANES],
                              q_ref[:, (2 * j + 1) * LANES:(2 * j + 2) * LANES]], axis=0)
        outs = []
        for half in range(2):
            s_win = lax.dot_general(q2, kw_halves[half], _NT_DIMS, preferred_element_type=F32)
            s_win = jnp.where(valid, s_win, NEG)
            s_ctx = lax.dot_general(q2, ck_halves[half], _NT_DIMS, preferred_element_type=F32)
            sk = _sink_col(sink_ref, (4 * j + half, 4 * j + 2 + half), QB)
            outs.append(_sink_attend([s_win, s_ctx], [vw_halves[half], cv_halves[half]], sk, half))
        for s2 in range(2):
            rows = slice(s2 * QB, (s2 + 1) * QB)
            o_ref[:, (2 * j + s2) * LANES:(2 * j + s2 + 1) * LANES] = (
                jnp.where(lo, outs[0][rows], outs[1][rows]).astype(BF16))


def _lat_attn(q, k, v, ck, cv, sink):
    lat0 = N_CTX // T_LAT
    return pl.pallas_call(
        _lat_attn_body,
        grid=(N_LAT_B, T_LAT // QB),
        in_specs=[pl.BlockSpec((QB, ATTN_W), lambda b, i: (N_CTX // QB + b * (T_LAT // QB) + i, 0)),
                  pl.BlockSpec((T_LAT, KV_W), lambda b, i: (lat0 + b, 0)),
                  pl.BlockSpec((T_LAT, KV_W), lambda b, i: (lat0 + b, 0)),
                  pl.BlockSpec((1, PAST, KV_W), lambda b, i: (b, 0, 0)),
                  pl.BlockSpec((1, PAST, KV_W), lambda b, i: (b, 0, 0)),
                  pl.BlockSpec(memory_space=pltpu.SMEM)],
        out_specs=pl.BlockSpec((QB, ATTN_W), lambda b, i: (b * (T_LAT // QB) + i, 0)),
        out_shape=jax.ShapeDtypeStruct((N_LAT, ATTN_W), BF16),
        compiler_params=_cparams(2),
        name="lat_attn",
    )(q, k, v, ck, cv, sink)


def _l0_out_body(oc_ref, ol_ref, u_ref, xp_ref, xs_ref, mod_ref, pw_ref, ps_ref, wo_ref,
                 g_ref, wr_ref, eb_ref, gb_ref, tri_ref, xext_ref, info_ref, cnt_ref, wb_ref, base_ref):
    i = pl.program_id(0)
    _bf16_once(wo_ref, wb_ref)
    is_ctx = i < N_CTX // TB_MIX
    o = jnp.where(is_ctx, oc_ref[...], ol_ref[...])
    x = jnp.where(is_ctx, xp_ref[...], xs_ref[...])
    tseq = jnp.where(is_ctx, T_CTX, T_LAT)
    pos = lax.broadcasted_iota(I32, (TB_MIX, LANES), 0) & (tseq - 1)
    ys = []
    for g, win in enumerate(POOL_WINDOWS):
        hw = win // 2
        ug = u_ref[:, g * LANES:(g + 1) * LANES]
        acc = ug
        for jj in range(-hw, hw):
            if jj == 0:
                continue
            sh = pltpu.roll(ug, (-jj) % TB_MIX, axis=0)
            ok = (pos + jj >= 0) if jj < 0 else (pos + jj < tseq)
            acc = acc + jnp.where(ok, sh, 0.0)
        cnt = (jnp.minimum(pos + hw, tseq) - jnp.maximum(pos - hw, 0)).astype(F32)
        pooled = acc / cnt - ug
        ys.append(jnp.dot(pooled.astype(BF16), pw_ref[g].astype(BF16), preferred_element_type=F32))
    y = jnp.concatenate(ys, axis=1) * ps_ref[...]
    out = (jnp.dot(o, wb_ref[0:ATTN_W, :], preferred_element_type=F32)
           + jnp.dot(y.astype(BF16), wb_ref[ATTN_W:ATTN_W + POOL_W, :], preferred_element_type=F32))
    x1 = x + mod_ref[0][:, 2 * D:3 * D] * out
    _route_tile(x1, mod_ref[0], _mod_row(i, N_CTX // TB_MIX, 1), 0, g_ref, wr_ref, eb_ref, gb_ref, tri_ref,
                xext_ref, info_ref, cnt_ref, base_ref)


def _l0_out(o_ctx, o_lat, u, xp, xs, mods, pool_w, pool_scale, w_out, route_ops):
    ntc = N_CTX // TB_MIX
    r_in, r_out, r_shape, r_scratch = _route_specs(TB_MIX, lambda i: i)
    const = lambda shape: pl.BlockSpec(shape, lambda i: (0,) * len(shape))
    ctx_map = lambda i: (jnp.minimum(i, ntc - 1), 0)
    lat_map = lambda i: (jnp.maximum(i - ntc, 0), 0)
    return pl.pallas_call(
        _l0_out_body,
        grid=(N_TOK // TB_MIX,),
        in_specs=[pl.BlockSpec((TB_MIX, ATTN_W), ctx_map),
                  pl.BlockSpec((TB_MIX, ATTN_W), lat_map),
                  pl.BlockSpec((TB_MIX, POOL_W), lambda i: (i, 0)),
                  pl.BlockSpec((TB_MIX, D), ctx_map),
                  pl.BlockSpec((TB_MIX, D), lat_map),
                  pl.BlockSpec((1, 1, 6 * D), lambda i: (_mod_row(i, ntc, 1), 0, 0)),
                  const((len(POOL_WINDOWS), LANES, LANES)), const((1, POOL_W)), const((D, D))] + r_in,
        out_specs=r_out,
        out_shape=r_shape,
        scratch_shapes=[pltpu.VMEM((D, D), BF16), r_scratch],
        compiler_params=_cparams(1, VMEM_LIMIT),
        name="l0_out",
    )(o_ctx, o_lat, u, xp, xs, mods, pool_w, pool_scale, w_out, *route_ops)


def _first_max(vals):
    best, idx = vals[0], jnp.zeros(vals[0].shape, I32)
    for r in range(1, len(vals)):
        better = vals[r] > best
        idx = jnp.where(better, r, idx)
        best = jnp.where(better, vals[r], best)
    return best, idx


def _softmax_rows(rows):
    mx = functools.reduce(jnp.maximum, rows)
    ex = [jnp.exp(r - mx) for r in rows]
    tot = functools.reduce(lambda a, b: a + b, ex)
    return [e / tot for e in ex]


def _route_tile(x, m, mod_id, layer, g_ref, wr_ref, eb_ref, gb_ref, tri_ref, xext_ref, info_ref, cnt_ref,
                base_ref):
    t_rows = x.shape[0]

    @pl.when(pl.program_id(0) == 0)
    def _():
        base_ref[...] = jnp.zeros_like(base_ref)

    h = _modulate(x, g_ref[layer:layer + 1, :], m[:, 3 * D:4 * D], m[:, 4 * D:5 * D])

    hh = h.astype(BF16)
    hl = (h - hh.astype(F32)).astype(BF16)
    wh, wl = wr_ref[0], wr_ref[1]
    lg = (lax.dot_general(wh, hh, _NT_DIMS, preferred_element_type=F32)
          + lax.dot_general(wl, hh, _NT_DIMS, preferred_element_type=F32)
          + lax.dot_general(wh, hl, _NT_DIMS, preferred_element_type=F32))
    lg_e = [lg[e:e + 1] + eb_ref[layer, e] for e in range(N_EXPERTS)]
    lg_g = [lg[N_EXPERTS + g:N_EXPERTS + g + 1] + gb_ref[layer, g] for g in range(N_GROUPS)]

    pg = _softmax_rows(lg_g)
    pg_top, gi = _first_max(pg)
    le = []
    for j in range(PER_GROUP):
        sel = lg_e[(N_GROUPS - 1) * PER_GROUP + j]
        for g in range(N_GROUPS - 2, -1, -1):
            sel = jnp.where(gi == g, lg_e[g * PER_GROUP + j], sel)
        le.append(sel)
    pe = _softmax_rows(le)
    p1, i1 = _first_max(pe)
    p2, i2 = _first_max([jnp.where(i1 == j, -1.0, pe[j]) for j in range(PER_GROUP)])
    den = p1 + p2
    w1 = pg_top * p1 / den
    w2 = pg_top * p2 / den

    lo = jnp.minimum(i1, i2)
    hi = jnp.maximum(i1, i2)
    cls = gi * PAIRS + jnp.where(lo == 0, 0, jnp.where(lo == 1, 3, 5)) + hi - lo - 1
    w_lo = jnp.where(i1 == lo, w1, w2)
    w_hi = jnp.where(i1 == lo, w2, w1)

    crow = lax.broadcasted_iota(I32, (CLASS_ROWS, t_rows), 0)
    hit = crow == cls
    onehot = jnp.where(hit, 1.0, 0.0)
    before = jnp.dot(onehot.astype(BF16), tri_ref[...], preferred_element_type=F32)
    before = before + base_ref[:, 0:1]
    rank = jnp.sum(jnp.where(hit, before, 0.0), axis=0, keepdims=True)
    base_ref[...] = base_ref[...] + jnp.sum(onehot, axis=1, keepdims=True)
    cnt_ref[...] = base_ref[...].astype(I32)

    mod_id = jnp.zeros_like(w1) + mod_id.astype(F32)
    zero = jnp.zeros_like(w1)
    info_ref[...] = jnp.concatenate([cls.astype(F32), rank, zero, zero, zero, zero, zero, zero], axis=0)
    side = jnp.concatenate([w_lo, w_hi, mod_id, jnp.zeros((LANES - 3, t_rows), F32)], axis=0).T
    xext_ref[:, 0:D] = x
    xext_ref[:, D:XEXT] = side


def _route_specs(tile, step_map):
    const = lambda shape: pl.BlockSpec(shape, lambda *i: (0,) * len(shape))
    smem = pl.BlockSpec(memory_space=pltpu.SMEM)
    in_specs = [const((DEPTH, D)), const((2, ROUTE_ROWS, D)), smem, smem, const((tile, tile))]
    out_specs = [pl.BlockSpec((tile, XEXT), lambda *i: (step_map(*i), 0)),
                 pl.BlockSpec((SUBLANES, tile), lambda *i: (0, step_map(*i))),
                 const((CLASS_ROWS, LANES))]
    out_shape = [jax.ShapeDtypeStruct((N_TOK, XEXT), F32),
                 jax.ShapeDtypeStruct((SUBLANES, N_TOK), F32),
                 jax.ShapeDtypeStruct((CLASS_ROWS, LANES), I32)]
    return in_specs, out_specs, out_shape, pltpu.VMEM((CLASS_ROWS, LANES), F32)


def _route_operands(g, wr, e_b, g_b, tile):
    tri = jnp.asarray(np.arange(tile)[:, None] < np.arange(tile)[None, :], BF16)
    return g, wr, e_b, g_b, tri


def _plan_body(cnt_ref, info_ref, pos_ref, lo_ref, hi_ref, sa_ref, sb_ref, newa_ref, newb_ref, nu_ref, ch_ref,
               toff_ref, seen_ref):
    tm_shift = TM.bit_length() - 1

    def per_class(c, first_tile):
        cnt = cnt_ref[c, 0]
        tiles = lax.shift_right_logical(cnt + (TM - 1), tm_shift)
        toff_ref[c] = first_tile * TM
        group = lax.div(c, PAIRS)
        pair = c - group * PAIRS
        lo = (pair >= 3).astype(I32) + (pair >= 5).astype(I32)
        hi = jnp.where(pair < 3, pair + 1, jnp.where(pair < 5, pair - 1, 3))

        def per_tile(k, carry):
            n = first_tile + k
            lo_ref[n] = lo
            hi_ref[n] = hi
            sa_ref[n] = group * PER_GROUP + lo
            sb_ref[n] = group * PER_GROUP + hi
            rows = jnp.minimum(cnt - k * TM, TM)
            ch_ref[n] = lax.shift_right_logical(rows + (CH - 1), CH.bit_length() - 1)
            return carry

        lax.fori_loop(0, tiles, per_tile, 0)
        return first_tile + tiles

    n_used = lax.fori_loop(0, N_CLASS, per_class, 0)
    nu_ref[0] = n_used

    for e in range(N_EXPERTS):
        seen_ref[e] = 0

    def per_used(n, slots):
        e_lo, e_hi = sa_ref[n], sb_ref[n]
        new_a = seen_ref[e_lo] == 0
        seen_ref[e_lo] = 1
        new_b = seen_ref[e_hi] == 0
        seen_ref[e_hi] = 1
        newa_ref[n] = new_a.astype(I32)
        newb_ref[n] = new_b.astype(I32)
        slot_a = jnp.where(new_a, e_lo, slots[0])
        slot_b = jnp.where(new_b, e_hi, slots[1])
        sa_ref[n] = slot_a
        sb_ref[n] = slot_b
        return slot_a, slot_b

    slot_a, slot_b = lax.fori_loop(0, n_used, per_used, (jnp.int32(0), jnp.int32(0)))

    def unused(n, carry):
        lo_ref[n] = 0
        hi_ref[n] = 0
        sa_ref[n] = slot_a
        sb_ref[n] = slot_b
        newa_ref[n] = 0
        newb_ref[n] = 0
        ch_ref[n] = 0
        return carry

    lax.fori_loop(n_used, NT_FFN, unused, 0)

    cls = info_ref[0:1, :]
    first_row = jnp.zeros(cls.shape, I32)
    for c in range(N_CLASS):
        first_row = jnp.where(cls == float(c), toff_ref[c], first_row)
    pos = first_row + info_ref[1:2, :].astype(I32)
    pos_ref[...] = jnp.broadcast_to(pos, pos_ref.shape)


def _plan(info, counts, layer):
    smem = pl.BlockSpec(memory_space=pltpu.SMEM)
    per_tile = jax.ShapeDtypeStruct((NT_FFN,), I32)
    pos, lo, hi, sa, sb, newa, newb, n_used, chunks = pl.pallas_call(
        _plan_body,
        grid=(1,),
        in_specs=[smem, pl.BlockSpec((SUBLANES, N_TOK), lambda i: (0, 0))],
        out_specs=[pl.BlockSpec((SUBLANES, N_TOK), lambda i: (0, 0))] + [smem] * 8,
        out_shape=[jax.ShapeDtypeStruct((SUBLANES, N_TOK), I32)] + [per_tile] * 6
                  + [jax.ShapeDtypeStruct((1,), I32), per_tile],
        scratch_shapes=[pltpu.SMEM((CLASS_ROWS,), I32), pltpu.SMEM((N_EXPERTS,), I32)],
        compiler_params=_cparams(1),
        name=f"plan{layer}",
    )(counts, info)
    return pos[0], lo, hi, (sa, sb, newa, newb), n_used, chunks


OCT = TM // SUBLANES


def _ffn_body(layer, pos_ref, lo_ref, hi_ref, sa_ref, sb_ref, newa_ref, newb_ref, nu_ref, ch_ref,
              xext_hbm, mod_ref, g_ref, w1a_ref, w1b_ref, w3a_ref, w3b_ref, w2a_ref, w2b_ref,
              out_hbm, src_ref, dst_ref, xbuf, ybuf, wb1, wb3, wb2, gsem, ssem):
    n = pl.program_id(0)
    n_used = nu_ref[0]

    def gather_copy(tile, s, c, j):
        return pltpu.make_async_copy(
            xext_hbm.at[pl.ds(src_ref[tile * TM + c * CH + j], 1)],
            xbuf.at[s, c * (CH // SUBLANES) + j // SUBLANES, pl.ds(j % SUBLANES, 1)], gsem.at[s])

    def scatter_copy(tile, s, c, j):
        return pltpu.make_async_copy(
            ybuf.at[s, c * (CH // SUBLANES) + j // SUBLANES, pl.ds(j % SUBLANES, 1)],
            out_hbm.at[pl.ds(dst_ref[tile * TM + c * CH + j], 1)], ssem.at[s])

    def start_rows(copy, tile, s):
        def chunk(c, carry):
            for j in range(CH):
                copy(tile, s, c, j).start()
            return carry
        lax.fori_loop(0, ch_ref[tile], chunk, 0)

    def wait_rows(src, dst, sem, tile):
        def chunk(c, carry):
            pltpu.make_async_copy(src, dst, sem).wait()
            return carry
        lax.fori_loop(0, ch_ref[tile], chunk, 0)

    def wait_gather(tile, s):
        rows = xbuf.at[s, pl.ds(0, CH // SUBLANES)]
        wait_rows(rows, rows, gsem.at[s], tile)

    def wait_scatter(tile, s):
        rows = ybuf.at[s, pl.ds(0, CH // SUBLANES)]
        wait_rows(rows, rows, ssem.at[s], tile)

    @pl.when(n == 0)
    def _():
        def pad_rows(tile, carry):
            @pl.when(ch_ref[tile] > 0)
            def _():
                first = tile * TM + (ch_ref[tile] - 1) * CH
                for j in range(CH):
                    src_ref[first + j] = 0
                    dst_ref[first + j] = N_TOK + ((first + j) & (2 * TM - 1))
            return carry

        lax.fori_loop(0, NT_FFN, pad_rows, 0)

        def put(t, carry):
            p = pos_ref[t]
            src_ref[p] = t
            dst_ref[p] = t
            return carry

        lax.fori_loop(0, N_TOK, put, 0, unroll=8)

        xbuf[...] = jnp.zeros_like(xbuf)
        ybuf[...] = jnp.zeros_like(ybuf)
        for s in range(2):
            dumps = [pltpu.make_async_copy(
                ybuf.at[s, q], out_hbm.at[pl.ds(N_TOK + s * TM + q * SUBLANES, SUBLANES)], ssem.at[s])
                for q in range(OCT)]
            for dump in dumps:
                dump.start()
            for dump in dumps:
                dump.wait()
        start_rows(gather_copy, 0, 0)

    def step(slot):
        @pl.when(n + 1 < n_used)
        def _():
            start_rows(gather_copy, n + 1, 1 - slot)

        wait_gather(n, slot)

        @pl.when(n >= 2)
        def _():
            wait_scatter(n - 2, slot)

        e_lo = lo_ref[n]
        e_hi = hi_ref[n]

        @pl.when(newa_ref[n] == 1)
        def _():
            wb1[e_lo] = w1a_ref[0, 0].astype(BF16)
            wb3[e_lo] = w3a_ref[0, 0].astype(BF16)
            wb2[e_lo] = w2a_ref[0, 0].astype(BF16)

        @pl.when(newb_ref[n] == 1)
        def _():
            wb1[e_hi] = w1b_ref[0, 0].astype(BF16)
            wb3[e_hi] = w3b_ref[0, 0].astype(BF16)
            wb2[e_hi] = w2b_ref[0, 0].astype(BF16)

        xe = xbuf[slot].reshape(TM, XEXT)
        x = xe[:, 0:D]
        w_lo = xe[:, D:D + 1]
        w_hi = xe[:, D + 1:D + 2]
        mod_id = xe[:, D + 2:D + 3]

        def pick(lo, hi):
            return jnp.where(mod_id < 0.5, mod_ref[0][:, lo:hi],
                             jnp.where(mod_id < 1.5, mod_ref[1][:, lo:hi], mod_ref[2][:, lo:hi]))

        h = _modulate(x, g_ref[layer:layer + 1, :], pick(3 * D, 4 * D), pick(4 * D, 5 * D)).astype(BF16)

        def act(e, w):
            h1 = jnp.dot(h, wb1[e], preferred_element_type=F32)
            h3 = jnp.dot(h, wb3[e], preferred_element_type=F32)
            return ((h1 * jax.nn.sigmoid(h1)) * h3 * w).astype(BF16)

        y = (jnp.dot(act(e_lo, w_lo), wb2[e_lo], preferred_element_type=F32)
             + jnp.dot(act(e_hi, w_hi), wb2[e_hi], preferred_element_type=F32))
        ybuf[slot] = (x + pick(5 * D, 6 * D) * y).reshape(OCT, SUBLANES, D)
        start_rows(scatter_copy, n, slot)

        @pl.when(n == n_used - 1)
        def _():
            @pl.when(n >= 1)
            def _():
                wait_scatter(n - 1, 1 - slot)
            wait_scatter(n, slot)

    for s in range(2):
        @pl.when((n < n_used) & (n % 2 == s))
        def _():
            step(s)


def _ffn(pos, lo, hi, slots, n_used, chunks, xext, mods, layer, g, w1, w3, w2):
    a_map = lambda n, p, lo, hi, sa, sb, na, nb, nu, ch: (layer, sa[n], 0, 0)
    b_map = lambda n, p, lo, hi, sa, sb, na, nb, nu, ch: (layer, sb[n], 0, 0)
    up = lambda imap: pl.BlockSpec((1, 1, D, D_EXPERT), imap)
    down = lambda imap: pl.BlockSpec((1, 1, D_EXPERT, D), imap)
    return pl.pallas_call(
        functools.partial(_ffn_body, layer),
        grid_spec=pltpu.PrefetchScalarGridSpec(
            num_scalar_prefetch=9, grid=(NT_FFN,),
            in_specs=[pl.BlockSpec(memory_space=pl.ANY),
                      pl.BlockSpec((SUBLANES, 1, 6 * D), lambda n, *_: (layer, 0, 0)),
                      pl.BlockSpec((DEPTH, D), lambda n, *_: (0, 0)),
                      up(a_map), up(b_map), up(a_map), up(b_map), down(a_map), down(b_map)],
            out_specs=pl.BlockSpec(memory_space=pl.ANY),
            scratch_shapes=[pltpu.SMEM((P_FFN,), I32), pltpu.SMEM((P_FFN,), I32),
                            pltpu.VMEM((2, OCT, SUBLANES, XEXT), F32),
                            pltpu.VMEM((2, OCT, SUBLANES, D), F32),
                            pltpu.VMEM((PER_GROUP, D, D_EXPERT), BF16),
                            pltpu.VMEM((PER_GROUP, D, D_EXPERT), BF16),
                            pltpu.VMEM((PER_GROUP, D_EXPERT, D), BF16),
                            pltpu.SemaphoreType.DMA((2,)), pltpu.SemaphoreType.DMA((2,))]),
        out_shape=jax.ShapeDtypeStruct((N_TOK + 2 * TM, D), F32),
        compiler_params=_cparams(1, VMEM_LIMIT),
        name=f"ffn{layer}",
    )(pos, lo, hi, *slots, n_used, chunks, xext, mods, g, w1, w1, w3, w3, w2, w2)


def _moe(routed, mods, layer, g, w1, w3, w2):
    xext, info, counts = routed
    pos, lo, hi, slots, n_used, chunks = _plan(info, counts, layer)
    return _ffn(pos, lo, hi, slots, n_used, chunks, xext, mods, layer, g, w1, w3, w2)


FG = 256


def _l1_in_body(x_ref, mod_ref, g_ref, w_ref, c_ref, s_ref, zc_ref, zs_ref, wb_ref):
    _bf16_once(w_ref, wb_ref)
    m = mod_ref[0]
    h = _modulate(x_ref[...], g_ref[1:2, :], m[:, 0:D], m[:, D:2 * D])
    z = jnp.dot(h.astype(BF16), wb_ref[...], preferred_element_type=F32).astype(BF16)
    for g in range(D // FG):
        zg = z[:, g * FG:(g + 1) * FG]
        zc_ref[:, g * FG:(g + 1) * FG] = jnp.dot(zg, c_ref[...], preferred_element_type=F32).astype(BF16)
        zs_ref[:, g * FG:(g + 1) * FG] = jnp.dot(zg, s_ref[...], preferred_element_type=F32).astype(BF16)


def _l1_in(x, mods, g, w, c256, s256):
    const = lambda shape: pl.BlockSpec(shape, lambda i: (0,) * len(shape))
    return pl.pallas_call(
        _l1_in_body,
        grid=(NT,),
        in_specs=[pl.BlockSpec((TB, D), lambda i: (i, 0)),
                  pl.BlockSpec((1, 1, 6 * D),
                               lambda i: (SUBLANES + _mod_row(i, NT_CTX, T_LAT // TB), 0, 0)),
                  const((DEPTH, D)), const((D, D)), const((FG, FG)), const((FG, FG))],
        out_specs=[pl.BlockSpec((TB, D), lambda i: (i, 0)), pl.BlockSpec((TB, D), lambda i: (i, 0))],
        out_shape=[jax.ShapeDtypeStruct((N_TOK, D), BF16), jax.ShapeDtypeStruct((N_TOK, D), BF16)],
        scratch_shapes=[pltpu.VMEM((D, D), BF16)],
        compiler_params=_cparams(1),
        name="l1_in",
    )(x, mods, g, w, c256, s256)


def _l1_out_body(zc_t_ref, zs_t_ref, zc_q_ref, zs_q_ref, c256_ref, s256_ref, c1k_ref, s1k_ref,
                 x_ref, mod_ref, wo_ref, g_ref, wr_ref, eb_ref, gb_ref, tri_ref,
                 xext_ref, info_ref, cnt_ref, f_ref, wb_ref, base_ref):
    i = pl.program_id(0)
    _bf16_once(wo_ref, wb_ref)

    @pl.when(i < NT_CTX)
    def _():
        for q in range(TB // T_CTX):
            rows = slice(q * T_CTX, (q + 1) * T_CTX)
            f = (jnp.dot(c256_ref[...], zc_t_ref[rows, :], preferred_element_type=F32)
                 - jnp.dot(s256_ref[...], zs_t_ref[rows, :], preferred_element_type=F32))
            f_ref[rows, :] = f.astype(BF16)

    @pl.when(i >= NT_CTX)
    def _():
        f = (jnp.dot(c1k_ref[...], zc_q_ref[...], preferred_element_type=F32)
             - jnp.dot(s1k_ref[...], zs_q_ref[...], preferred_element_type=F32))
        f_ref[...] = f.astype(BF16)

    out = jnp.dot(f_ref[...], wb_ref[...], preferred_element_type=F32)
    x3 = x_ref[...] + mod_ref[0][:, 2 * D:3 * D] * out
    _route_tile(x3, mod_ref[0], _mod_row(i, NT_CTX, T_LAT // TB), 1, g_ref, wr_ref, eb_ref, gb_ref, tri_ref,
                xext_ref, info_ref, cnt_ref, base_ref)


def _l1_out(zc, zs, c256, s256, c1k, s1k, x, mods, w_out, route_ops):
    const = lambda shape: pl.BlockSpec(shape, lambda i: (0,) * len(shape))
    r_in, r_out, r_shape, r_scratch = _route_specs(TB, lambda i: i)
    tile_map = lambda i: (jnp.minimum(i, NT_CTX - 1), 0)
    seq_map = lambda i: (N_CTX // T_LAT + jnp.maximum(i - NT_CTX, 0) // (T_LAT // TB), 0)
    row_map = lambda i: (jnp.maximum(i - NT_CTX, 0) % (T_LAT // TB), 0)
    return pl.pallas_call(
        _l1_out_body,
        grid=(NT,),
        in_specs=[pl.BlockSpec((TB, D), tile_map), pl.BlockSpec((TB, D), tile_map),
                  pl.BlockSpec((T_LAT, D), seq_map), pl.BlockSpec((T_LAT, D), seq_map),
                  const((T_CTX, T_CTX)), const((T_CTX, T_CTX)),
                  pl.BlockSpec((TB, T_LAT), row_map), pl.BlockSpec((TB, T_LAT), row_map),
                  pl.BlockSpec((TB, D), lambda i: (i, 0)),
                  pl.BlockSpec((1, 1, 6 * D),
                               lambda i: (SUBLANES + _mod_row(i, NT_CTX, T_LAT // TB), 0, 0)),
                  const((D, D))] + r_in,
        out_specs=r_out,
        out_shape=r_shape,
        scratch_shapes=[pltpu.VMEM((TB, D), BF16), pltpu.VMEM((D, D), BF16), r_scratch],
        compiler_params=_cparams(1, VMEM_LIMIT),
        name="l1_out",
    )(zc, zs, zc, zs, c256, s256, c1k, s1k, x, mods, w_out, *route_ops)


def kernel(x_prompt, x_sample, cache_k, cache_v, c, c_ctx, ada_w, ada_b, norm_mix, norm_ffn, a_w_in, a_q_norm, a_k_norm, a_sink, pool_w, pool_scale, a_w_out, f_w_in, f_w_out, router_g_w, router_g_b, router_e_w, router_e_b, moe_w1, moe_w3, moe_w2):
    xp = x_prompt.reshape(N_CTX, D)
    xs = x_sample.reshape(N_LAT, D)

    mods = _adaln(c_ctx, c, ada_w, ada_b)

    tabs = _rope_tables()
    lane = np.arange(LANES)
    bd = jnp.asarray((lane[:, None] // HEAD_DIM) == (lane[None, :] // HEAD_DIM), BF16)
    c256, s256 = _dft_tables(T_CTX)
    c1k, s1k = _dft_tables(T_LAT)

    rw = jnp.swapaxes(jnp.concatenate([router_e_w, router_g_w], axis=2), 1, 2)
    rw = jnp.pad(rw, ((0, 0), (0, ROUTE_ROWS - rw.shape[1]), (0, 0)))
    rw_hi = rw.astype(BF16)
    rw_split = jnp.stack([rw_hi, (rw - rw_hi.astype(F32)).astype(BF16)], axis=1)

    def route_operands(l, tile):
        return _route_operands(norm_ffn, rw_split[l], router_e_b, router_g_b, tile)

    q, k, v, u, new_k, new_v = _l0_in(xp, xs, mods, norm_mix, a_w_in[0], a_q_norm, a_k_norm, bd, tabs)
    o_ctx = _ctx_attn(q, k, v, a_sink)
    ck = cache_k[:, 0].reshape(N_LAT_B, PAST, KV_W)
    cv = cache_v[:, 0].reshape(N_LAT_B, PAST, KV_W)
    o_lat = _lat_attn(q, k, v, ck, cv, a_sink)
    routed = _l0_out(o_ctx, o_lat, u, xp, xs, mods, pool_w[0], pool_scale[0][None, :], a_w_out[0],
                     route_operands(0, TB_MIX))
    x2 = _moe(routed, mods, 0, norm_ffn, moe_w1, moe_w3, moe_w2)

    zc, zs = _l1_in(x2, mods, norm_mix, f_w_in[0], c256, s256)
    routed = _l1_out(zc, zs, c256, s256, c1k, s1k, x2, mods, f_w_out[0], route_operands(1, TB))
    x4 = _moe(routed, mods, 1, norm_ffn, moe_w1, moe_w3, moe_w2)

    def cache_entry(t):
        t = t.reshape(N_CTX_B, 1, KV_W // HEAD_DIM, HEAD_DIM, T_CTX)
        return jnp.transpose(t, (0, 1, 4, 2, 3))

    new_k, new_v = cache_entry(new_k), cache_entry(new_v)
    return (x4[:N_CTX].reshape(N_CTX_B, T_CTX, D), x4[N_CTX:N_TOK].reshape(N_LAT_B, T_LAT, D),
            new_k, new_v)
```

```python
import functools

import numpy as np
import jax
import jax.numpy as jnp
from jax import lax
from jax.experimental import pallas as pl
from jax.experimental.pallas import tpu as pltpu

F32 = jnp.float32
BF16 = jnp.bfloat16
I32 = jnp.int32

D = 1024
DEPTH = 2
N_CTX_B, T_CTX = 16, 256
N_LAT_B, T_LAT = 2, 1024
N_CTX = N_CTX_B * T_CTX
N_LAT = N_LAT_B * T_LAT
N_TOK = N_CTX + N_LAT
PAST = 512
GRID_W = 64
HEAD_DIM = 64
N_HEADS = 8
ATTN_W = 512
KV_W = 128
POOL_W = 512
POOL_WINDOWS = (2, 4, 8, 16)
MIX_IN = ATTN_W + 2 * KV_W + POOL_W
WINDOW = 128
N_GROUPS = 4
PER_GROUP = 4
N_EXPERTS = 16
D_EXPERT = 512
ROPE_THETA = 10000.0
EPS = 1e-6
NEG = -1e30

LANES = 128
SUBLANES = 8
TB = 512
NT = N_TOK // TB
NT_CTX = N_CTX // TB
TB_MIX = 1024
TM = 256
PAIRS = 6
N_CLASS = N_GROUPS * PAIRS
CLASS_ROWS = 32
NT_FFN = N_TOK // TM + N_CLASS
P_FFN = NT_FFN * TM
CH = 32
XEXT = D + LANES
ROUTE_ROWS = 32

VMEM_LIMIT = 56 * 1024 * 1024


def _cparams(n_axes=1, vmem=None):
    return pltpu.CompilerParams(dimension_semantics=("arbitrary",) * n_axes,
                                vmem_limit_bytes=vmem)


def _modulate(x, g, shift, scale):
    ms = jnp.mean(x * x, axis=-1, keepdims=True)
    return (x * lax.rsqrt(ms + EPS) * g) * (1.0 + scale) + shift


def _bf16_once(w_ref, wb_ref):
    @pl.when(pl.program_id(0) == 0)
    def _():
        wb_ref[...] = w_ref[...].astype(BF16)


def _mod_row(tile, tiles_ctx, tiles_per_lat):
    return (tile >= tiles_ctx).astype(I32) + (tile >= tiles_ctx + tiles_per_lat).astype(I32)


def _rope_tables():
    t = np.arange(T_LAT)
    row = (t // GRID_W).astype(np.float64)
    col = (t % GRID_W).astype(np.float64)
    nf = HEAD_DIM // 4
    freqs = ROPE_THETA ** (-np.arange(nf, dtype=np.float64) / nf)
    d = np.arange(HEAD_DIM)
    pos = np.where(d[None, :] < HEAD_DIM // 2, row[:, None], col[:, None])
    ang = pos * freqs[d % nf][None, :]
    first = (d % (HEAD_DIM // 2)) < nf
    cos = np.cos(ang)
    sin_a = np.where(first[None, :], -np.sin(ang), 0.0)
    sin_b = np.where(first[None, :], 0.0, np.sin(ang))
    ident = (np.ones((TB, HEAD_DIM)), np.zeros((TB, HEAD_DIM)), np.zeros((TB, HEAD_DIM)))
    out = []
    for tab, idt in zip((cos, sin_a, sin_b), ident):
        full = np.concatenate([tab, idt], axis=0)
        out.append(jnp.asarray(np.tile(full, (1, LANES // HEAD_DIM)), F32))
    return out


def _dft_tables(t):
    m = np.outer(np.arange(t), np.arange(t)) % t
    ang = 2.0 * np.pi * m / t
    s = 1.0 / np.sqrt(t)
    return jnp.asarray(np.cos(ang) * s, F32).astype(BF16), jnp.asarray(np.sin(ang) * s, F32).astype(BF16)


def _adaln_body(cctx_ref, c_ref, w_ref, b_ref, o_ref):
    layer = pl.program_id(0)
    c = jnp.concatenate([cctx_ref[...], c_ref[...], jnp.zeros((SUBLANES - 1 - N_LAT_B, D), F32)], axis=0)
    s = (c * jax.nn.sigmoid(c)).astype(BF16)
    m = jnp.dot(s, w_ref[0].astype(BF16), preferred_element_type=F32) + b_ref[pl.ds(layer, 1), :]
    for r in range(SUBLANES):
        o_ref[r] = m[r:r + 1]


def _adaln(c_ctx, c, ada_w, ada_b):
    tn = 1536
    return pl.pallas_call(
        _adaln_body,
        grid=(DEPTH, 6 * D // tn),
        in_specs=[pl.BlockSpec((1, D), lambda l, j: (0, 0)),
                  pl.BlockSpec((N_LAT_B, D), lambda l, j: (0, 0)),
                  pl.BlockSpec((1, D, tn), lambda l, j: (l, 0, j)),
                  pl.BlockSpec((DEPTH, tn), lambda l, j: (0, j))],
        out_specs=pl.BlockSpec((SUBLANES, 1, tn), lambda l, j: (l, 0, j)),
        out_shape=jax.ShapeDtypeStruct((DEPTH * SUBLANES, 1, 6 * D), F32),
        compiler_params=_cparams(2),
        name="adaln",
    )(c_ctx.reshape(1, D), c, ada_w, ada_b)


def _l0_in_body(xp_ref, xs_ref, mod_ref, g_ref, w_ref, qg_ref, kg_ref, bd_ref,
                cos_ref, sa_ref, sb_ref, q_ref, k_ref, v_ref, u_ref, kc_ref, vc_ref, wb_ref):
    i = pl.program_id(0)
    _bf16_once(w_ref, wb_ref)
    x = jnp.where(i < NT_CTX, xp_ref[...], xs_ref[...])
    m = mod_ref[0]
    h = _modulate(x, g_ref[0:1, :], m[:, 0:D], m[:, D:2 * D])
    z = jnp.dot(h.astype(BF16), wb_ref[...], preferred_element_type=F32)
    cos, sa, sb, bd = cos_ref[...], sa_ref[...], sb_ref[...], bd_ref[...]
    qg = jnp.concatenate([qg_ref[...]] * (LANES // HEAD_DIM), axis=1)
    kg = jnp.concatenate([kg_ref[...]] * (LANES // HEAD_DIM), axis=1)

    def head_norm_rope(zz, gain):
        ss = jnp.dot((zz * zz).astype(BF16), bd, preferred_element_type=F32)
        y = zz * lax.rsqrt(ss * (1.0 / HEAD_DIM) + EPS) * gain
        return (y * cos + pltpu.roll(y, LANES - 16, axis=1) * sa
                + pltpu.roll(y, 16, axis=1) * sb)

    for s in range(ATTN_W // LANES):
        qs = head_norm_rope(z[:, s * LANES:(s + 1) * LANES], qg)
        q_ref[:, s * LANES:(s + 1) * LANES] = (qs * (HEAD_DIM ** -0.5)).astype(BF16)
    k = head_norm_rope(z[:, ATTN_W:ATTN_W + KV_W], kg)
    v = z[:, ATTN_W + KV_W:ATTN_W + 2 * KV_W]
    k_ref[...] = k
    v_ref[...] = v
    u_ref[...] = z[:, ATTN_W + 2 * KV_W:MIX_IN]

    @pl.when(i < NT_CTX)
    def _():
        for q in range(TB // T_CTX):
            kc_ref[q] = k[q * T_CTX:(q + 1) * T_CTX, :].T
            vc_ref[q] = v[q * T_CTX:(q + 1) * T_CTX, :].T


def _l0_in(xp, xs, mods, g, w_in, qg, kg, bd, tabs):
    tab_spec = pl.BlockSpec(
        (TB, LANES), lambda i: (jnp.where(i < NT_CTX, T_LAT // TB, (i - NT_CTX) % (T_LAT // TB)), 0))
    const = lambda shape: pl.BlockSpec(shape, lambda i: (0,) * len(shape))
    return pl.pallas_call(
        _l0_in_body,
        grid=(NT,),
        in_specs=[pl.BlockSpec((TB, D), lambda i: (jnp.minimum(i, NT_CTX - 1), 0)),
                  pl.BlockSpec((TB, D), lambda i: (jnp.maximum(i - NT_CTX, 0), 0)),
                  pl.BlockSpec((1, 1, 6 * D), lambda i: (_mod_row(i, NT_CTX, T_LAT // TB), 0, 0)),
                  const((DEPTH, D)), const((D, MIX_IN)), const((1, HEAD_DIM)), const((1, HEAD_DIM)),
                  const((LANES, LANES)), tab_spec, tab_spec, tab_spec],
        out_specs=[pl.BlockSpec((TB, ATTN_W), lambda i: (i, 0)),
                   pl.BlockSpec((TB, KV_W), lambda i: (i, 0)),
                   pl.BlockSpec((TB, KV_W), lambda i: (i, 0)),
                   pl.BlockSpec((TB, POOL_W), lambda i: (i, 0)),
                   pl.BlockSpec((TB // T_CTX, KV_W, T_CTX), lambda i: (jnp.minimum(i, NT_CTX - 1), 0, 0)),
                   pl.BlockSpec((TB // T_CTX, KV_W, T_CTX), lambda i: (jnp.minimum(i, NT_CTX - 1), 0, 0))],
        out_shape=[jax.ShapeDtypeStruct((N_TOK, ATTN_W), BF16),
                   jax.ShapeDtypeStruct((N_TOK, KV_W), F32),
                   jax.ShapeDtypeStruct((N_TOK, KV_W), F32),
                   jax.ShapeDtypeStruct((N_TOK, POOL_W), F32),
                   jax.ShapeDtypeStruct((N_CTX_B, KV_W, T_CTX), F32),
                   jax.ShapeDtypeStruct((N_CTX_B, KV_W, T_CTX), F32)],
        scratch_shapes=[pltpu.VMEM((D, MIX_IN), BF16)],
        compiler_params=_cparams(1),
        name="l0_in",
    )(xp, xs, mods, g, w_in, qg, kg, bd, *tabs)


def _head_halves(x):
    z = jnp.zeros_like(x)
    return jnp.concatenate([x, z], axis=1), jnp.concatenate([z, x], axis=1)


_NT_DIMS = (((1,), (1,)), ((), ()))


def _ones_halves(x):
    one = jnp.ones_like(x)
    return jnp.concatenate([x, one], axis=1), jnp.concatenate([one, x], axis=1)


def _sink_attend(scores, values, sk, half):
    mx = sk
    for sc in scores:
        mx = jnp.maximum(mx, jnp.max(sc, axis=-1, keepdims=True))
    acc = None
    for sc, val in zip(scores, values):
        part = jnp.dot(jnp.exp(sc - mx).astype(BF16), val, preferred_element_type=F32)
        acc = part if acc is None else acc + part
    ones_lane = HEAD_DIM * (1 - half)
    den = acc[:, ones_lane:ones_lane + 1] + jnp.exp(sk - mx)
    return acc * (1.0 / den)


def _sink_col(sink_ref, heads, rows):
    return jnp.concatenate([jnp.full((rows, 1), sink_ref[0, h], F32) for h in heads], axis=0)


CTX_SEQS = 4


def _ctx_attn_body(q_ref, k_ref, v_ref, sink_ref, o_ref):
    lo = lax.broadcasted_iota(I32, (T_CTX, LANES), 1) < HEAD_DIM
    for b, j in ((b, j) for b in range(CTX_SEQS) for j in range(KV_W // HEAD_DIM)):
        seq = slice(b * T_CTX, (b + 1) * T_CTX)
        kj = k_ref[seq, j * HEAD_DIM:(j + 1) * HEAD_DIM].astype(BF16)
        vj = v_ref[seq, j * HEAD_DIM:(j + 1) * HEAD_DIM].astype(BF16)
        k_halves = _head_halves(kj)
        vd = jnp.concatenate([vj, vj], axis=1)
        q2 = jnp.concatenate([q_ref[seq, (2 * j) * LANES:(2 * j + 1) * LANES],
                              q_ref[seq, (2 * j + 1) * LANES:(2 * j + 2) * LANES]], axis=0)
        outs = []
        for half in range(2):
            sc = lax.dot_general(q2, k_halves[half], _NT_DIMS, preferred_element_type=F32)
            sk = _sink_col(sink_ref, (4 * j + half, 4 * j + 2 + half), T_CTX)
            mx = jnp.maximum(sk, jnp.max(sc, axis=-1, keepdims=True))
            p = jnp.exp(sc - mx)
            inv = 1.0 / (jnp.exp(sk - mx) + jnp.sum(p, axis=-1, keepdims=True))
            outs.append(jnp.dot((p * inv).astype(BF16), vd, preferred_element_type=F32))
        for s2 in range(2):
            rows = slice(s2 * T_CTX, (s2 + 1) * T_CTX)
            o_ref[seq, (2 * j + s2) * LANES:(2 * j + s2 + 1) * LANES] = (
                jnp.where(lo, outs[0][rows], outs[1][rows]).astype(BF16))


def _ctx_attn(q, k, v, sink):
    rows = CTX_SEQS * T_CTX
    return pl.pallas_call(
        _ctx_attn_body,
        grid=(N_CTX_B // CTX_SEQS,),
        in_specs=[pl.BlockSpec((rows, ATTN_W), lambda b: (b, 0)),
                  pl.BlockSpec((rows, KV_W), lambda b: (b, 0)),
                  pl.BlockSpec((rows, KV_W), lambda b: (b, 0)),
                  pl.BlockSpec(memory_space=pltpu.SMEM)],
        out_specs=pl.BlockSpec((rows, ATTN_W), lambda b: (b, 0)),
        out_shape=jax.ShapeDtypeStruct((N_CTX, ATTN_W), BF16),
        compiler_params=_cparams(1),
        name="ctx_attn",
    )(q, k, v, sink)


QB = 128
SPAN = QB + 2 * WINDOW


LAT_QBLOCKS = 2


def _lat_attn_body(q_ref, k_ref, v_ref, ck_ref, cv_ref, sink_ref, o_ref):
    ck = ck_ref[0].astype(BF16)
    cv = cv_ref[0].astype(BF16)
    lo = lax.broadcasted_iota(I32, (QB, LANES), 1) < HEAD_DIM
    kv_heads = [slice(j * HEAD_DIM, (j + 1) * HEAD_DIM) for j in range(KV_W // HEAD_DIM)]
    ck_halves = [_head_halves(ck[:, sl]) for sl in kv_heads]
    cv_halves = [_ones_halves(cv[:, sl]) for sl in kv_heads]
    for r in range(LAT_QBLOCKS):
        start = (pl.program_id(1) * LAT_QBLOCKS + r) * QB
        qrows = slice(r * QB, (r + 1) * QB)
        kws, vws = [], []
        for c in (-1, 0, 1):
            cs = pl.multiple_of(jnp.clip(start + c * QB, 0, T_LAT - QB), QB)
            kws.append(k_ref[pl.ds(cs, QB), :])
            vws.append(v_ref[pl.ds(cs, QB), :])
        kw = jnp.concatenate(kws, axis=0).astype(BF16)
        vw = jnp.concatenate(vws, axis=0).astype(BF16)
        qpos = start + (lax.broadcasted_iota(I32, (2 * QB, SPAN), 0) & (QB - 1))
        kpos = start - WINDOW + lax.broadcasted_iota(I32, (2 * QB, SPAN), 1)
        valid = (kpos >= 0) & (kpos < T_LAT) & (jnp.abs(qpos - kpos) <= WINDOW)
        for j, sl in enumerate(kv_heads):
            kw_halves = _head_halves(kw[:, sl])
            vw_halves = _ones_halves(vw[:, sl])
            q2 = jnp.concatenate([q_ref[qrows, (2 * j) * LANES:(2 * j + 1) * LANES],
                                  q_ref[qrows, (2 * j + 1) * LANES:(2 * j + 2) * LANES]], axis=0)
            outs = []
            for half in range(2):
                s_win = lax.dot_general(q2, kw_halves[half], _NT_DIMS, preferred_element_type=F32)
                s_win = jnp.where(valid, s_win, NEG)
                s_ctx = lax.dot_general(q2, ck_halves[j][half], _NT_DIMS, preferred_element_type=F32)
                sk = _sink_col(sink_ref, (4 * j + half, 4 * j + 2 + half), QB)
                outs.append(_sink_attend([s_win, s_ctx], [vw_halves[half], cv_halves[j][half]], sk, half))
            for s2 in range(2):
                rows = slice(s2 * QB, (s2 + 1) * QB)
                o_ref[qrows, (2 * j + s2) * LANES:(2 * j + s2 + 1) * LANES] = (
                    jnp.where(lo, outs[0][rows], outs[1][rows]).astype(BF16))


def _lat_attn(q, k, v, ck, cv, sink):
    lat0 = N_CTX // T_LAT
    qrows = LAT_QBLOCKS * QB
    return pl.pallas_call(
        _lat_attn_body,
        grid=(N_LAT_B, T_LAT // qrows),
        in_specs=[pl.BlockSpec((qrows, ATTN_W), lambda b, i: (N_CTX // qrows + b * (T_LAT // qrows) + i, 0)),
                  pl.BlockSpec((T_LAT, KV_W), lambda b, i: (lat0 + b, 0)),
                  pl.BlockSpec((T_LAT, KV_W), lambda b, i: (lat0 + b, 0)),
                  pl.BlockSpec((1, PAST, KV_W), lambda b, i: (b, 0, 0)),
                  pl.BlockSpec((1, PAST, KV_W), lambda b, i: (b, 0, 0)),
                  pl.BlockSpec(memory_space=pltpu.SMEM)],
        out_specs=pl.BlockSpec((qrows, ATTN_W), lambda b, i: (b * (T_LAT // qrows) + i, 0)),
        out_shape=jax.ShapeDtypeStruct((N_LAT, ATTN_W), BF16),
        compiler_params=_cparams(2),
        name="lat_attn",
    )(q, k, v, ck, cv, sink)


def _l0_out_body(oc_ref, ol_ref, u_ref, xp_ref, xs_ref, mod_ref, pw_ref, ps_ref, wo_ref,
                 g_ref, wr_ref, eb_ref, gb_ref, tri_ref, xext_ref, info_ref, cnt_ref, wb_ref, base_ref):
    i = pl.program_id(0)
    _bf16_once(wo_ref, wb_ref)
    is_ctx = i < N_CTX // TB_MIX
    o = jnp.where(is_ctx, oc_ref[...], ol_ref[...])
    x = jnp.where(is_ctx, xp_ref[...], xs_ref[...])
    tseq = jnp.where(is_ctx, T_CTX, T_LAT)
    pos = lax.broadcasted_iota(I32, (TB_MIX, LANES), 0) & (tseq - 1)
    ys = []
    for g, win in enumerate(POOL_WINDOWS):
        hw = win // 2
        ug = u_ref[:, g * LANES:(g + 1) * LANES]
        acc = ug
        for jj in range(-hw, hw):
            if jj == 0:
                continue
            sh = pltpu.roll(ug, (-jj) % TB_MIX, axis=0)
            ok = (pos + jj >= 0) if jj < 0 else (pos + jj < tseq)
            acc = acc + jnp.where(ok, sh, 0.0)
        cnt = (jnp.minimum(pos + hw, tseq) - jnp.maximum(pos - hw, 0)).astype(F32)
        pooled = acc / cnt - ug
        ys.append(jnp.dot(pooled.astype(BF16), pw_ref[g].astype(BF16), preferred_element_type=F32))
    y = jnp.concatenate(ys, axis=1) * ps_ref[...]
    out = (jnp.dot(o, wb_ref[0:ATTN_W, :], preferred_element_type=F32)
           + jnp.dot(y.astype(BF16), wb_ref[ATTN_W:ATTN_W + POOL_W, :], preferred_element_type=F32))
    x1 = x + mod_ref[0][:, 2 * D:3 * D] * out
    _route_tile(x1, mod_ref[0], _mod_row(i, N_CTX // TB_MIX, 1), 0, g_ref, wr_ref, eb_ref, gb_ref, tri_ref,
                xext_ref, info_ref, cnt_ref, base_ref)


def _l0_out(o_ctx, o_lat, u, xp, xs, mods, pool_w, pool_scale, w_out, route_ops):
    ntc = N_CTX // TB_MIX
    r_in, r_out, r_shape, r_scratch = _route_specs(TB_MIX, lambda i: i)
    const = lambda shape: pl.BlockSpec(shape, lambda i: (0,) * len(shape))
    ctx_map = lambda i: (jnp.minimum(i, ntc - 1), 0)
    lat_map = lambda i: (jnp.maximum(i - ntc, 0), 0)
    return pl.pallas_call(
        _l0_out_body,
        grid=(N_TOK // TB_MIX,),
        in_specs=[pl.BlockSpec((TB_MIX, ATTN_W), ctx_map),
                  pl.BlockSpec((TB_MIX, ATTN_W), lat_map),
                  pl.BlockSpec((TB_MIX, POOL_W), lambda i: (i, 0)),
                  pl.BlockSpec((TB_MIX, D), ctx_map),
                  pl.BlockSpec((TB_MIX, D), lat_map),
                  pl.BlockSpec((1, 1, 6 * D), lambda i: (_mod_row(i, ntc, 1), 0, 0)),
                  const((len(POOL_WINDOWS), LANES, LANES)), const((1, POOL_W)), const((D, D))] + r_in,
        out_specs=r_out,
        out_shape=r_shape,
        scratch_shapes=[pltpu.VMEM((D, D), BF16), r_scratch],
        compiler_params=_cparams(1, VMEM_LIMIT),
        name="l0_out",
    )(o_ctx, o_lat, u, xp, xs, mods, pool_w, pool_scale, w_out, *route_ops)


def _first_max(vals):
    best, idx = vals[0], jnp.zeros(vals[0].shape, I32)
    for r in range(1, len(vals)):
        better = vals[r] > best
        idx = jnp.where(better, r, idx)
        best = jnp.where(better, vals[r], best)
    return best, idx


def _softmax_rows(rows):
    mx = functools.reduce(jnp.maximum, rows)
    ex = [jnp.exp(r - mx) for r in rows]
    tot = functools.reduce(lambda a, b: a + b, ex)
    return [e / tot for e in ex]


def _route_tile(x, m, mod_id, layer, g_ref, wr_ref, eb_ref, gb_ref, tri_ref, xext_ref, info_ref, cnt_ref,
                base_ref):
    t_rows = x.shape[0]

    @pl.when(pl.program_id(0) == 0)
    def _():
        base_ref[...] = jnp.zeros_like(base_ref)

    h = _modulate(x, g_ref[layer:layer + 1, :], m[:, 3 * D:4 * D], m[:, 4 * D:5 * D])

    hh = h.astype(BF16)
    hl = (h - hh.astype(F32)).astype(BF16)
    wh, wl = wr_ref[0], wr_ref[1]
    lg = (lax.dot_general(wh, hh, _NT_DIMS, preferred_element_type=F32)
          + lax.dot_general(wl, hh, _NT_DIMS, preferred_element_type=F32)
          + lax.dot_general(wh, hl, _NT_DIMS, preferred_element_type=F32))
    lg_e = [lg[e:e + 1] + eb_ref[layer, e] for e in range(N_EXPERTS)]
    lg_g = [lg[N_EXPERTS + g:N_EXPERTS + g + 1] + gb_ref[layer, g] for g in range(N_GROUPS)]

    pg = _softmax_rows(lg_g)
    pg_top, gi = _first_max(pg)
    le = []
    for j in range(PER_GROUP):
        sel = lg_e[(N_GROUPS - 1) * PER_GROUP + j]
        for g in range(N_GROUPS - 2, -1, -1):
            sel = jnp.where(gi == g, lg_e[g * PER_GROUP + j], sel)
        le.append(sel)
    pe = _softmax_rows(le)
    p1, i1 = _first_max(pe)
    p2, i2 = _first_max([jnp.where(i1 == j, -1.0, pe[j]) for j in range(PER_GROUP)])
    den = p1 + p2
    w1 = pg_top * p1 / den
    w2 = pg_top * p2 / den

    lo = jnp.minimum(i1, i2)
    hi = jnp.maximum(i1, i2)
    cls = gi * PAIRS + jnp.where(lo == 0, 0, jnp.where(lo == 1, 3, 5)) + hi - lo - 1
    w_lo = jnp.where(i1 == lo, w1, w2)
    w_hi = jnp.where(i1 == lo, w2, w1)

    crow = lax.broadcasted_iota(I32, (CLASS_ROWS, t_rows), 0)
    hit = crow == cls
    onehot = jnp.where(hit, 1.0, 0.0)
    before = jnp.dot(onehot.astype(BF16), tri_ref[...], preferred_element_type=F32)
    before = before + base_ref[:, 0:1]
    rank = jnp.sum(jnp.where(hit, before, 0.0), axis=0, keepdims=True)
    base_ref[...] = base_ref[...] + jnp.sum(onehot, axis=1, keepdims=True)
    cnt_ref[...] = base_ref[...].astype(I32)

    mod_id = jnp.zeros_like(w1) + mod_id.astype(F32)
    zero = jnp.zeros_like(w1)
    info_ref[...] = jnp.concatenate([cls.astype(F32), rank, zero, zero, zero, zero, zero, zero], axis=0)
    side = jnp.concatenate([w_lo, w_hi, mod_id, jnp.zeros((LANES - 3, t_rows), F32)], axis=0).T
    xext_ref[:, 0:D] = x
    xext_ref[:, D:XEXT] = side


def _route_specs(tile, step_map):
    const = lambda shape: pl.BlockSpec(shape, lambda *i: (0,) * len(shape))
    smem = pl.BlockSpec(memory_space=pltpu.SMEM)
    in_specs = [const((DEPTH, D)), const((2, ROUTE_ROWS, D)), smem, smem, const((tile, tile))]
    out_specs = [pl.BlockSpec((tile, XEXT), lambda *i: (step_map(*i), 0)),
                 pl.BlockSpec((SUBLANES, tile), lambda *i: (0, step_map(*i))),
                 const((CLASS_ROWS, LANES))]
    out_shape = [jax.ShapeDtypeStruct((N_TOK, XEXT), F32),
                 jax.ShapeDtypeStruct((SUBLANES, N_TOK), F32),
                 jax.ShapeDtypeStruct((CLASS_ROWS, LANES), I32)]
    return in_specs, out_specs, out_shape, pltpu.VMEM((CLASS_ROWS, LANES), F32)


def _route_operands(g, wr, e_b, g_b, tile):
    tri = jnp.asarray(np.arange(tile)[:, None] < np.arange(tile)[None, :], BF16)
    return g, wr, e_b, g_b, tri


def _plan_body(cnt_ref, info_ref, pos_ref, lo_ref, hi_ref, sa_ref, sb_ref, newa_ref, newb_ref, nu_ref, ch_ref,
               toff_ref, seen_ref):
    tm_shift = TM.bit_length() - 1

    def per_class(c, first_tile):
        cnt = cnt_ref[c, 0]
        tiles = lax.shift_right_logical(cnt + (TM - 1), tm_shift)
        toff_ref[c] = first_tile * TM
        group = lax.div(c, PAIRS)
        pair = c - group * PAIRS
        lo = (pair >= 3).astype(I32) + (pair >= 5).astype(I32)
        hi = jnp.where(pair < 3, pair + 1, jnp.where(pair < 5, pair - 1, 3))

        def per_tile(k, carry):
            n = first_tile + k
            lo_ref[n] = lo
            hi_ref[n] = hi
            sa_ref[n] = group * PER_GROUP + lo
            sb_ref[n] = group * PER_GROUP + hi
            rows = jnp.minimum(cnt - k * TM, TM)
            ch_ref[n] = lax.shift_right_logical(rows + (CH - 1), CH.bit_length() - 1)
            return carry

        lax.fori_loop(0, tiles, per_tile, 0)
        return first_tile + tiles

    n_used = lax.fori_loop(0, N_CLASS, per_class, 0)
    nu_ref[0] = n_used

    for e in range(N_EXPERTS):
        seen_ref[e] = 0

    def per_used(n, slots):
        e_lo, e_hi = sa_ref[n], sb_ref[n]
        new_a = seen_ref[e_lo] == 0
        seen_ref[e_lo] = 1
        new_b = seen_ref[e_hi] == 0
        seen_ref[e_hi] = 1
        newa_ref[n] = new_a.astype(I32)
        newb_ref[n] = new_b.astype(I32)
        slot_a = jnp.where(new_a, e_lo, slots[0])
        slot_b = jnp.where(new_b, e_hi, slots[1])
        sa_ref[n] = slot_a
        sb_ref[n] = slot_b
        return slot_a, slot_b

    slot_a, slot_b = lax.fori_loop(0, n_used, per_used, (jnp.int32(0), jnp.int32(0)))

    def unused(n, carry):
        lo_ref[n] = 0
        hi_ref[n] = 0
        sa_ref[n] = slot_a
        sb_ref[n] = slot_b
        newa_ref[n] = 0
        newb_ref[n] = 0
        ch_ref[n] = 0
        return carry

    lax.fori_loop(n_used, NT_FFN, unused, 0)

    cls = info_ref[0:1, :]
    first_row = jnp.zeros(cls.shape, I32)
    for c in range(N_CLASS):
        first_row = jnp.where(cls == float(c), toff_ref[c], first_row)
    pos = first_row + info_ref[1:2, :].astype(I32)
    pos_ref[...] = jnp.broadcast_to(pos, pos_ref.shape)


def _plan(info, counts, layer):
    smem = pl.BlockSpec(memory_space=pltpu.SMEM)
    per_tile = jax.ShapeDtypeStruct((NT_FFN,), I32)
    pos, lo, hi, sa, sb, newa, newb, n_used, chunks = pl.pallas_call(
        _plan_body,
        grid=(1,),
        in_specs=[smem, pl.BlockSpec((SUBLANES, N_TOK), lambda i: (0, 0))],
        out_specs=[pl.BlockSpec((SUBLANES, N_TOK), lambda i: (0, 0))] + [smem] * 8,
        out_shape=[jax.ShapeDtypeStruct((SUBLANES, N_TOK), I32)] + [per_tile] * 6
                  + [jax.ShapeDtypeStruct((1,), I32), per_tile],
        scratch_shapes=[pltpu.SMEM((CLASS_ROWS,), I32), pltpu.SMEM((N_EXPERTS,), I32)],
        compiler_params=_cparams(1),
        name=f"plan{layer}",
    )(counts, info)
    return pos[0], lo, hi, (sa, sb, newa, newb), n_used, chunks


OCT = TM // SUBLANES


def _ffn_body(layer, pos_ref, lo_ref, hi_ref, sa_ref, sb_ref, newa_ref, newb_ref, nu_ref, ch_ref,
              xext_hbm, mod_ref, g_ref, w1a_ref, w1b_ref, w3a_ref, w3b_ref, w2a_ref, w2b_ref,
              out_hbm, src_ref, dst_ref, xbuf, ybuf, wb1, wb3, wb2, gsem, ssem):
    n = pl.program_id(0)
    n_used = nu_ref[0]

    def gather_copy(tile, s, c, j):
        return pltpu.make_async_copy(
            xext_hbm.at[pl.ds(src_ref[tile * TM + c * CH + j], 1)],
            xbuf.at[s, c * (CH // SUBLANES) + j // SUBLANES, pl.ds(j % SUBLANES, 1)], gsem.at[s])

    def scatter_copy(tile, s, c, j):
        return pltpu.make_async_copy(
            ybuf.at[s, c * (CH // SUBLANES) + j // SUBLANES, pl.ds(j % SUBLANES, 1)],
            out_hbm.at[pl.ds(dst_ref[tile * TM + c * CH + j], 1)], ssem.at[s])

    def start_rows(copy, tile, s):
        def chunk(c, carry):
            for j in range(CH):
                copy(tile, s, c, j).start()
            return carry
        lax.fori_loop(0, ch_ref[tile], chunk, 0)

    def wait_rows(src, dst, sem, tile):
        def chunk(c, carry):
            pltpu.make_async_copy(src, dst, sem).wait()
            return carry
        lax.fori_loop(0, ch_ref[tile], chunk, 0)

    def wait_gather(tile, s):
        rows = xbuf.at[s, pl.ds(0, CH // SUBLANES)]
        wait_rows(rows, rows, gsem.at[s], tile)

    def wait_scatter(tile, s):
        rows = ybuf.at[s, pl.ds(0, CH // SUBLANES)]
        wait_rows(rows, rows, ssem.at[s], tile)

    @pl.when(n == 0)
    def _():
        def pad_rows(tile, carry):
            @pl.when(ch_ref[tile] > 0)
            def _():
                first = tile * TM + (ch_ref[tile] - 1) * CH
                for j in range(CH):
                    src_ref[first + j] = 0
                    dst_ref[first + j] = N_TOK + ((first + j) & (2 * TM - 1))
            return carry

        lax.fori_loop(0, NT_FFN, pad_rows, 0)

        def put(t, carry):
            p = pos_ref[t]
            src_ref[p] = t
            dst_ref[p] = t
            return carry

        lax.fori_loop(0, N_TOK, put, 0, unroll=8)

        xbuf[...] = jnp.zeros_like(xbuf)
        ybuf[...] = jnp.zeros_like(ybuf)
        for s in range(2):
            dumps = [pltpu.make_async_copy(
                ybuf.at[s, q], out_hbm.at[pl.ds(N_TOK + s * TM + q * SUBLANES, SUBLANES)], ssem.at[s])
                for q in range(OCT)]
            for dump in dumps:
                dump.start()
            for dump in dumps:
                dump.wait()
        start_rows(gather_copy, 0, 0)

    def step(slot):
        @pl.when(n + 1 < n_used)
        def _():
            start_rows(gather_copy, n + 1, 1 - slot)

        wait_gather(n, slot)

        @pl.when(n >= 2)
        def _():
            wait_scatter(n - 2, slot)

        e_lo = lo_ref[n]
        e_hi = hi_ref[n]

        @pl.when(newa_ref[n] == 1)
        def _():
            wb1[e_lo] = w1a_ref[0, 0].astype(BF16)
            wb3[e_lo] = w3a_ref[0, 0].astype(BF16)
            wb2[e_lo] = w2a_ref[0, 0].astype(BF16)

        @pl.when(newb_ref[n] == 1)
        def _():
            wb1[e_hi] = w1b_ref[0, 0].astype(BF16)
            wb3[e_hi] = w3b_ref[0, 0].astype(BF16)
            wb2[e_hi] = w2b_ref[0, 0].astype(BF16)

        xe = xbuf[slot].reshape(TM, XEXT)
        x = xe[:, 0:D]
        w_lo = xe[:, D:D + 1]
        w_hi = xe[:, D + 1:D + 2]
        mod_id = xe[:, D + 2:D + 3]

        def pick(lo, hi):
            return jnp.where(mod_id < 0.5, mod_ref[0][:, lo:hi],
                             jnp.where(mod_id < 1.5, mod_ref[1][:, lo:hi], mod_ref[2][:, lo:hi]))

        h = _modulate(x, g_ref[layer:layer + 1, :], pick(3 * D, 4 * D), pick(4 * D, 5 * D)).astype(BF16)

        def act(e, w):
            h1 = jnp.dot(h, wb1[e], preferred_element_type=F32)
            h3 = jnp.dot(h, wb3[e], preferred_element_type=F32)
            return ((h1 * jax.nn.sigmoid(h1)) * h3 * w).astype(BF16)

        y = (jnp.dot(act(e_lo, w_lo), wb2[e_lo], preferred_element_type=F32)
             + jnp.dot(act(e_hi, w_hi), wb2[e_hi], preferred_element_type=F32))
        ybuf[slot] = (x + pick(5 * D, 6 * D) * y).reshape(OCT, SUBLANES, D)
        start_rows(scatter_copy, n, slot)

        @pl.when(n == n_used - 1)
        def _():
            @pl.when(n >= 1)
            def _():
                wait_scatter(n - 1, 1 - slot)
            wait_scatter(n, slot)

    for s in range(2):
        @pl.when((n < n_used) & (n % 2 == s))
        def _():
            step(s)


def _ffn(pos, lo, hi, slots, n_used, chunks, xext, mods, layer, g, w1, w3, w2):
    a_map = lambda n, p, lo, hi, sa, sb, na, nb, nu, ch: (layer, sa[n], 0, 0)
    b_map = lambda n, p, lo, hi, sa, sb, na, nb, nu, ch: (layer, sb[n], 0, 0)
    up = lambda imap: pl.BlockSpec((1, 1, D, D_EXPERT), imap)
    down = lambda imap: pl.BlockSpec((1, 1, D_EXPERT, D), imap)
    return pl.pallas_call(
        functools.partial(_ffn_body, layer),
        grid_spec=pltpu.PrefetchScalarGridSpec(
            num_scalar_prefetch=9, grid=(NT_FFN,),
            in_specs=[pl.BlockSpec(memory_space=pl.ANY),
                      pl.BlockSpec((SUBLANES, 1, 6 * D), lambda n, *_: (layer, 0, 0)),
                      pl.BlockSpec((DEPTH, D), lambda n, *_: (0, 0)),
                      up(a_map), up(b_map), up(a_map), up(b_map), down(a_map), down(b_map)],
            out_specs=pl.BlockSpec(memory_space=pl.ANY),
            scratch_shapes=[pltpu.SMEM((P_FFN,), I32), pltpu.SMEM((P_FFN,), I32),
                            pltpu.VMEM((2, OCT, SUBLANES, XEXT), F32),
                            pltpu.VMEM((2, OCT, SUBLANES, D), F32),
                            pltpu.VMEM((PER_GROUP, D, D_EXPERT), BF16),
                            pltpu.VMEM((PER_GROUP, D, D_EXPERT), BF16),
                            pltpu.VMEM((PER_GROUP, D_EXPERT, D), BF16),
                            pltpu.SemaphoreType.DMA((2,)), pltpu.SemaphoreType.DMA((2,))]),
        out_shape=jax.ShapeDtypeStruct((N_TOK + 2 * TM, D), F32),
        compiler_params=_cparams(1, VMEM_LIMIT),
        name=f"ffn{layer}",
    )(pos, lo, hi, *slots, n_used, chunks, xext, mods, g, w1, w1, w3, w3, w2, w2)


def _moe(routed, mods, layer, g, w1, w3, w2):
    xext, info, counts = routed
    pos, lo, hi, slots, n_used, chunks = _plan(info, counts, layer)
    return _ffn(pos, lo, hi, slots, n_used, chunks, xext, mods, layer, g, w1, w3, w2)


FG = 256


def _l1_in_body(x_ref, mod_ref, g_ref, w_ref, c_ref, s_ref, zc_ref, zs_ref, wb_ref):
    _bf16_once(w_ref, wb_ref)
    m = mod_ref[0]
    h = _modulate(x_ref[...], g_ref[1:2, :], m[:, 0:D], m[:, D:2 * D])
    z = jnp.dot(h.astype(BF16), wb_ref[...], preferred_element_type=F32).astype(BF16)
    for g in range(D // FG):
        zg = z[:, g * FG:(g + 1) * FG]
        zc_ref[:, g * FG:(g + 1) * FG] = jnp.dot(zg, c_ref[...], preferred_element_type=F32).astype(BF16)
        zs_ref[:, g * FG:(g + 1) * FG] = jnp.dot(zg, s_ref[...], preferred_element_type=F32).astype(BF16)


def _l1_in(x, mods, g, w, c256, s256):
    const = lambda shape: pl.BlockSpec(shape, lambda i: (0,) * len(shape))
    return pl.pallas_call(
        _l1_in_body,
        grid=(NT,),
        in_specs=[pl.BlockSpec((TB, D), lambda i: (i, 0)),
                  pl.BlockSpec((1, 1, 6 * D),
                               lambda i: (SUBLANES + _mod_row(i, NT_CTX, T_LAT // TB), 0, 0)),
                  const((DEPTH, D)), const((D, D)), const((FG, FG)), const((FG, FG))],
        out_specs=[pl.BlockSpec((TB, D), lambda i: (i, 0)), pl.BlockSpec((TB, D), lambda i: (i, 0))],
        out_shape=[jax.ShapeDtypeStruct((N_TOK, D), BF16), jax.ShapeDtypeStruct((N_TOK, D), BF16)],
        scratch_shapes=[pltpu.VMEM((D, D), BF16)],
        compiler_params=_cparams(1),
        name="l1_in",
    )(x, mods, g, w, c256, s256)


def _l1_out_body(zc_t_ref, zs_t_ref, zc_q_ref, zs_q_ref, c256_ref, s256_ref, c1k_ref, s1k_ref,
                 x_ref, mod_ref, wo_ref, g_ref, wr_ref, eb_ref, gb_ref, tri_ref,
                 xext_ref, info_ref, cnt_ref, f_ref, wb_ref, base_ref):
    i = pl.program_id(0)
    _bf16_once(wo_ref, wb_ref)

    @pl.when(i < NT_CTX)
    def _():
        for q in range(TB // T_CTX):
            rows = slice(q * T_CTX, (q + 1) * T_CTX)
            f = (jnp.dot(c256_ref[...], zc_t_ref[rows, :], preferred_element_type=F32)
                 - jnp.dot(s256_ref[...], zs_t_ref[rows, :], preferred_element_type=F32))
            f_ref[rows, :] = f.astype(BF16)

    @pl.when(i >= NT_CTX)
    def _():
        f = (jnp.dot(c1k_ref[...], zc_q_ref[...], preferred_element_type=F32)
             - jnp.dot(s1k_ref[...], zs_q_ref[...], preferred_element_type=F32))
        f_ref[...] = f.astype(BF16)

    out = jnp.dot(f_ref[...], wb_ref[...], preferred_element_type=F32)
    x3 = x_ref[...] + mod_ref[0][:, 2 * D:3 * D] * out
    _route_tile(x3, mod_ref[0], _mod_row(i, NT_CTX, T_LAT // TB), 1, g_ref, wr_ref, eb_ref, gb_ref, tri_ref,
                xext_ref, info_ref, cnt_ref, base_ref)


def _l1_out(zc, zs, c256, s256, c1k, s1k, x, mods, w_out, route_ops):
    const = lambda shape: pl.BlockSpec(shape, lambda i: (0,) * len(shape))
    r_in, r_out, r_shape, r_scratch = _route_specs(TB, lambda i: i)
    tile_map = lambda i: (jnp.minimum(i, NT_CTX - 1), 0)
    seq_map = lambda i: (N_CTX // T_LAT + jnp.maximum(i - NT_CTX, 0) // (T_LAT // TB), 0)
    row_map = lambda i: (jnp.maximum(i - NT_CTX, 0) % (T_LAT // TB), 0)
    return pl.pallas_call(
        _l1_out_body,
        grid=(NT,),
        in_specs=[pl.BlockSpec((TB, D), tile_map), pl.BlockSpec((TB, D), tile_map),
                  pl.BlockSpec((T_LAT, D), seq_map), pl.BlockSpec((T_LAT, D), seq_map),
                  const((T_CTX, T_CTX)), const((T_CTX, T_CTX)),
                  pl.BlockSpec((TB, T_LAT), row_map), pl.BlockSpec((TB, T_LAT), row_map),
                  pl.BlockSpec((TB, D), lambda i: (i, 0)),
                  pl.BlockSpec((1, 1, 6 * D),
                               lambda i: (SUBLANES + _mod_row(i, NT_CTX, T_LAT // TB), 0, 0)),
                  const((D, D))] + r_in,
        out_specs=r_out,
        out_shape=r_shape,
        scratch_shapes=[pltpu.VMEM((TB, D), BF16), pltpu.VMEM((D, D), BF16), r_scratch],
        compiler_params=_cparams(1, VMEM_LIMIT),
        name="l1_out",
    )(zc, zs, zc, zs, c256, s256, c1k, s1k, x, mods, w_out, *route_ops)


def kernel(x_prompt, x_sample, cache_k, cache_v, c, c_ctx, ada_w, ada_b, norm_mix, norm_ffn, a_w_in, a_q_norm, a_k_norm, a_sink, pool_w, pool_scale, a_w_out, f_w_in, f_w_out, router_g_w, router_g_b, router_e_w, router_e_b, moe_w1, moe_w3, moe_w2):
    xp = x_prompt.reshape(N_CTX, D)
    xs = x_sample.reshape(N_LAT, D)

    mods = _adaln(c_ctx, c, ada_w, ada_b)

    tabs = _rope_tables()
    lane = np.arange(LANES)
    bd = jnp.asarray((lane[:, None] // HEAD_DIM) == (lane[None, :] // HEAD_DIM), BF16)
    c256, s256 = _dft_tables(T_CTX)
    c1k, s1k = _dft_tables(T_LAT)

    rw = jnp.swapaxes(jnp.concatenate([router_e_w, router_g_w], axis=2), 1, 2)
    rw = jnp.pad(rw, ((0, 0), (0, ROUTE_ROWS - rw.shape[1]), (0, 0)))
    rw_hi = rw.astype(BF16)
    rw_split = jnp.stack([rw_hi, (rw - rw_hi.astype(F32)).astype(BF16)], axis=1)

    def route_operands(l, tile):
        return _route_operands(norm_ffn, rw_split[l], router_e_b, router_g_b, tile)

    q, k, v, u, new_k, new_v = _l0_in(xp, xs, mods, norm_mix, a_w_in[0], a_q_norm, a_k_norm, bd, tabs)
    o_ctx = _ctx_attn(q, k, v, a_sink)
    ck = cache_k[:, 0].reshape(N_LAT_B, PAST, KV_W)
    cv = cache_v[:, 0].reshape(N_LAT_B, PAST, KV_W)
    o_lat = _lat_attn(q, k, v, ck, cv, a_sink)
    routed = _l0_out(o_ctx, o_lat, u, xp, xs, mods, pool_w[0], pool_scale[0][None, :], a_w_out[0],
                     route_operands(0, TB_MIX))
    x2 = _moe(routed, mods, 0, norm_ffn, moe_w1, moe_w3, moe_w2)

    zc, zs = _l1_in(x2, mods, norm_mix, f_w_in[0], c256, s256)
    routed = _l1_out(zc, zs, c256, s256, c1k, s1k, x2, mods, f_w_out[0], route_operands(1, TB))
    x4 = _moe(routed, mods, 1, norm_ffn, moe_w1, moe_w3, moe_w2)

    def cache_entry(t):
        t = t.reshape(N_CTX_B, 1, KV_W // HEAD_DIM, HEAD_DIM, T_CTX)
        return jnp.transpose(t, (0, 1, 4, 2, 3))

    new_k, new_v = cache_entry(new_k), cache_entry(new_v)
    return (x4[:N_CTX].reshape(N_CTX_B, T_CTX, D), x4[N_CTX:N_TOK].reshape(N_LAT_B, T_LAT, D),
            new_k, new_v)
```

```python
import functools

import numpy as np
import jax
import jax.numpy as jnp
from jax import lax
from jax.experimental import pallas as pl
from jax.experimental.pallas import tpu as pltpu

F32 = jnp.float32
BF16 = jnp.bfloat16
I32 = jnp.int32

D = 1024
DEPTH = 2
N_CTX_B, T_CTX = 16, 256
N_LAT_B, T_LAT = 2, 1024
N_CTX = N_CTX_B * T_CTX
N_LAT = N_LAT_B * T_LAT
N_TOK = N_CTX + N_LAT
PAST = 512
GRID_W = 64
HEAD_DIM = 64
N_HEADS = 8
ATTN_W = 512
KV_W = 128
POOL_W = 512
POOL_WINDOWS = (2, 4, 8, 16)
MIX_IN = ATTN_W + 2 * KV_W + POOL_W
WINDOW = 128
N_GROUPS = 4
PER_GROUP = 4
N_EXPERTS = 16
D_EXPERT = 512
ROPE_THETA = 10000.0
EPS = 1e-6
NEG = -1e30

LANES = 128
SUBLANES = 8
TB = 512
NT = N_TOK // TB
NT_CTX = N_CTX // TB
TB_MIX = 1024
TM = 256
PAIRS = 6
N_CLASS = N_GROUPS * PAIRS
CLASS_ROWS = 32
NT_FFN = N_TOK // TM + N_CLASS
P_FFN = NT_FFN * TM
CH = 32
XEXT = D + LANES
ROUTE_ROWS = 32

VMEM_LIMIT = 56 * 1024 * 1024


def _cparams(n_axes=1, vmem=None):
    return pltpu.CompilerParams(dimension_semantics=("arbitrary",) * n_axes,
                                vmem_limit_bytes=vmem)


def _modulate(x, g, shift, scale):
    ms = jnp.mean(x * x, axis=-1, keepdims=True)
    return (x * lax.rsqrt(ms + EPS) * g) * (1.0 + scale) + shift


def _bf16_once(w_ref, wb_ref):
    @pl.when(pl.program_id(0) == 0)
    def _():
        wb_ref[...] = w_ref[...].astype(BF16)


def _mod_row(tile, tiles_ctx, tiles_per_lat):
    return (tile >= tiles_ctx).astype(I32) + (tile >= tiles_ctx + tiles_per_lat).astype(I32)


def _rope_tables():
    t = np.arange(T_LAT)
    row = (t // GRID_W).astype(np.float64)
    col = (t % GRID_W).astype(np.float64)
    nf = HEAD_DIM // 4
    freqs = ROPE_THETA ** (-np.arange(nf, dtype=np.float64) / nf)
    d = np.arange(HEAD_DIM)
    pos = np.where(d[None, :] < HEAD_DIM // 2, row[:, None], col[:, None])
    ang = pos * freqs[d % nf][None, :]
    first = (d % (HEAD_DIM // 2)) < nf
    cos = np.cos(ang)
    sin_a = np.where(first[None, :], -np.sin(ang), 0.0)
    sin_b = np.where(first[None, :], 0.0, np.sin(ang))
    ident = (np.ones((TB, HEAD_DIM)), np.zeros((TB, HEAD_DIM)), np.zeros((TB, HEAD_DIM)))
    out = []
    for tab, idt in zip((cos, sin_a, sin_b), ident):
        full = np.concatenate([tab, idt], axis=0)
        out.append(jnp.asarray(np.tile(full, (1, LANES // HEAD_DIM)), F32))
    return out


def _dft_tables(t):
    m = np.outer(np.arange(t), np.arange(t)) % t
    ang = 2.0 * np.pi * m / t
    s = 1.0 / np.sqrt(t)
    return jnp.asarray(np.cos(ang) * s, F32).astype(BF16), jnp.asarray(np.sin(ang) * s, F32).astype(BF16)


def _adaln_body(cctx_ref, c_ref, w_ref, b_ref, o_ref):
    layer = pl.program_id(0)
    c = jnp.concatenate([cctx_ref[...], c_ref[...], jnp.zeros((SUBLANES - 1 - N_LAT_B, D), F32)], axis=0)
    s = (c * jax.nn.sigmoid(c)).astype(BF16)
    m = jnp.dot(s, w_ref[0].astype(BF16), preferred_element_type=F32) + b_ref[pl.ds(layer, 1), :]
    for r in range(SUBLANES):
        o_ref[r] = m[r:r + 1]


def _adaln(c_ctx, c, ada_w, ada_b):
    tn = 1536
    return pl.pallas_call(
        _adaln_body,
        grid=(DEPTH, 6 * D // tn),
        in_specs=[pl.BlockSpec((1, D), lambda l, j: (0, 0)),
                  pl.BlockSpec((N_LAT_B, D), lambda l, j: (0, 0)),
                  pl.BlockSpec((1, D, tn), lambda l, j: (l, 0, j)),
                  pl.BlockSpec((DEPTH, tn), lambda l, j: (0, j))],
        out_specs=pl.BlockSpec((SUBLANES, 1, tn), lambda l, j: (l, 0, j)),
        out_shape=jax.ShapeDtypeStruct((DEPTH * SUBLANES, 1, 6 * D), F32),
        compiler_params=_cparams(2),
        name="adaln",
    )(c_ctx.reshape(1, D), c, ada_w, ada_b)


def _l0_in_body(xp_ref, xs_ref, mod_ref, g_ref, w_ref, qg_ref, kg_ref, bd_ref,
                cos_ref, sa_ref, sb_ref, q_ref, k_ref, v_ref, u_ref, kc_ref, vc_ref, wb_ref):
    i = pl.program_id(0)
    _bf16_once(w_ref, wb_ref)
    x = jnp.where(i < NT_CTX, xp_ref[...], xs_ref[...])
    m = mod_ref[0]
    h = _modulate(x, g_ref[0:1, :], m[:, 0:D], m[:, D:2 * D])
    z = jnp.dot(h.astype(BF16), wb_ref[...], preferred_element_type=F32)
    cos, sa, sb, bd = cos_ref[...], sa_ref[...], sb_ref[...], bd_ref[...]
    qg = jnp.concatenate([qg_ref[...]] * (LANES // HEAD_DIM), axis=1)
    kg = jnp.concatenate([kg_ref[...]] * (LANES // HEAD_DIM), axis=1)

    def head_norm_rope(zz, gain):
        ss = jnp.dot((zz * zz).astype(BF16), bd, preferred_element_type=F32)
        y = zz * lax.rsqrt(ss * (1.0 / HEAD_DIM) + EPS) * gain
        return (y * cos + pltpu.roll(y, LANES - 16, axis=1) * sa
                + pltpu.roll(y, 16, axis=1) * sb)

    for s in range(ATTN_W // LANES):
        qs = head_norm_rope(z[:, s * LANES:(s + 1) * LANES], qg)
        q_ref[:, s * LANES:(s + 1) * LANES] = (qs * (HEAD_DIM ** -0.5)).astype(BF16)
    k = head_norm_rope(z[:, ATTN_W:ATTN_W + KV_W], kg)
    v = z[:, ATTN_W + KV_W:ATTN_W + 2 * KV_W]
    k_ref[...] = k
    v_ref[...] = v
    u_ref[...] = z[:, ATTN_W + 2 * KV_W:MIX_IN]

    @pl.when(i < NT_CTX)
    def _():
        for q in range(TB // T_CTX):
            kc_ref[q] = k[q * T_CTX:(q + 1) * T_CTX, :].T
            vc_ref[q] = v[q * T_CTX:(q + 1) * T_CTX, :].T


def _l0_in(xp, xs, mods, g, w_in, qg, kg, bd, tabs):
    tab_spec = pl.BlockSpec(
        (TB, LANES), lambda i: (jnp.where(i < NT_CTX, T_LAT // TB, (i - NT_CTX) % (T_LAT // TB)), 0))
    const = lambda shape: pl.BlockSpec(shape, lambda i: (0,) * len(shape))
    return pl.pallas_call(
        _l0_in_body,
        grid=(NT,),
        in_specs=[pl.BlockSpec((TB, D), lambda i: (jnp.minimum(i, NT_CTX - 1), 0)),
                  pl.BlockSpec((TB, D), lambda i: (jnp.maximum(i - NT_CTX, 0), 0)),
                  pl.BlockSpec((1, 1, 6 * D), lambda i: (_mod_row(i, NT_CTX, T_LAT // TB), 0, 0)),
                  const((DEPTH, D)), const((D, MIX_IN)), const((1, HEAD_DIM)), const((1, HEAD_DIM)),
                  const((LANES, LANES)), tab_spec, tab_spec, tab_spec],
        out_specs=[pl.BlockSpec((TB, ATTN_W), lambda i: (i, 0)),
                   pl.BlockSpec((TB, KV_W), lambda i: (i, 0)),
                   pl.BlockSpec((TB, KV_W), lambda i: (i, 0)),
                   pl.BlockSpec((TB, POOL_W), lambda i: (i, 0)),
                   pl.BlockSpec((TB // T_CTX, KV_W, T_CTX), lambda i: (jnp.minimum(i, NT_CTX - 1), 0, 0)),
                   pl.BlockSpec((TB // T_CTX, KV_W, T_CTX), lambda i: (jnp.minimum(i, NT_CTX - 1), 0, 0))],
        out_shape=[jax.ShapeDtypeStruct((N_TOK, ATTN_W), BF16),
                   jax.ShapeDtypeStruct((N_TOK, KV_W), F32),
                   jax.ShapeDtypeStruct((N_TOK, KV_W), F32),
                   jax.ShapeDtypeStruct((N_TOK, POOL_W), F32),
                   jax.ShapeDtypeStruct((N_CTX_B, KV_W, T_CTX), F32),
                   jax.ShapeDtypeStruct((N_CTX_B, KV_W, T_CTX), F32)],
        scratch_shapes=[pltpu.VMEM((D, MIX_IN), BF16)],
        compiler_params=_cparams(1),
        name="l0_in",
    )(xp, xs, mods, g, w_in, qg, kg, bd, *tabs)


def _head_halves(x):
    z = jnp.zeros_like(x)
    return jnp.concatenate([x, z], axis=1), jnp.concatenate([z, x], axis=1)


_NT_DIMS = (((1,), (1,)), ((), ()))


def _ones_halves(x):
    one = jnp.ones_like(x)
    return jnp.concatenate([x, one], axis=1), jnp.concatenate([one, x], axis=1)


def _sink_attend(scores, values, sk, half):
    mx = sk
    for sc in scores:
        mx = jnp.maximum(mx, jnp.max(sc, axis=-1, keepdims=True))
    acc = None
    for sc, val in zip(scores, values):
        part = jnp.dot(jnp.exp(sc - mx).astype(BF16), val, preferred_element_type=F32)
        acc = part if acc is None else acc + part
    ones_lane = HEAD_DIM * (1 - half)
    den = acc[:, ones_lane:ones_lane + 1] + jnp.exp(sk - mx)
    return acc * (1.0 / den)


def _sink_col(sink_ref, heads, rows):
    return jnp.concatenate([jnp.full((rows, 1), sink_ref[0, h], F32) for h in heads], axis=0)


CTX_SEQS = 8


def _ctx_attn_body(q_ref, k_ref, v_ref, sink_ref, o_ref):
    lo = lax.broadcasted_iota(I32, (T_CTX, LANES), 1) < HEAD_DIM
    for b, j in ((b, j) for b in range(CTX_SEQS) for j in range(KV_W // HEAD_DIM)):
        seq = slice(b * T_CTX, (b + 1) * T_CTX)
        kj = k_ref[seq, j * HEAD_DIM:(j + 1) * HEAD_DIM].astype(BF16)
        vj = v_ref[seq, j * HEAD_DIM:(j + 1) * HEAD_DIM].astype(BF16)
        k_halves = _head_halves(kj)
        vd = jnp.concatenate([vj, vj], axis=1)
        q2 = jnp.concatenate([q_ref[seq, (2 * j) * LANES:(2 * j + 1) * LANES],
                              q_ref[seq, (2 * j + 1) * LANES:(2 * j + 2) * LANES]], axis=0)
        outs = []
        for half in range(2):
            sc = lax.dot_general(q2, k_halves[half], _NT_DIMS, preferred_element_type=F32)
            sk = _sink_col(sink_ref, (4 * j + half, 4 * j + 2 + half), T_CTX)
            mx = jnp.maximum(sk, jnp.max(sc, axis=-1, keepdims=True))
            p = jnp.exp(sc - mx)
            inv = 1.0 / (jnp.exp(sk - mx) + jnp.sum(p, axis=-1, keepdims=True))
            outs.append(jnp.dot((p * inv).astype(BF16), vd, preferred_element_type=F32))
        for s2 in range(2):
            rows = slice(s2 * T_CTX, (s2 + 1) * T_CTX)
            o_ref[seq, (2 * j + s2) * LANES:(2 * j + s2 + 1) * LANES] = (
                jnp.where(lo, outs[0][rows], outs[1][rows]).astype(BF16))


def _ctx_attn(q, k, v, sink):
    rows = CTX_SEQS * T_CTX
    return pl.pallas_call(
        _ctx_attn_body,
        grid=(N_CTX_B // CTX_SEQS,),
        in_specs=[pl.BlockSpec((rows, ATTN_W), lambda b: (b, 0)),
                  pl.BlockSpec((rows, KV_W), lambda b: (b, 0)),
                  pl.BlockSpec((rows, KV_W), lambda b: (b, 0)),
                  pl.BlockSpec(memory_space=pltpu.SMEM)],
        out_specs=pl.BlockSpec((rows, ATTN_W), lambda b: (b, 0)),
        out_shape=jax.ShapeDtypeStruct((N_CTX, ATTN_W), BF16),
        compiler_params=_cparams(1),
        name="ctx_attn",
    )(q, k, v, sink)


QB = 128
SPAN = QB + 2 * WINDOW


LAT_QBLOCKS = 4


def _lat_attn_body(q_ref, k_ref, v_ref, ck_ref, cv_ref, sink_ref, o_ref):
    ck = ck_ref[0].astype(BF16)
    cv = cv_ref[0].astype(BF16)
    lo = lax.broadcasted_iota(I32, (QB, LANES), 1) < HEAD_DIM
    kv_heads = [slice(j * HEAD_DIM, (j + 1) * HEAD_DIM) for j in range(KV_W // HEAD_DIM)]
    ck_halves = [_head_halves(ck[:, sl]) for sl in kv_heads]
    cv_halves = [_ones_halves(cv[:, sl]) for sl in kv_heads]
    for r in range(LAT_QBLOCKS):
        start = (pl.program_id(1) * LAT_QBLOCKS + r) * QB
        qrows = slice(r * QB, (r + 1) * QB)
        kws, vws = [], []
        for c in (-1, 0, 1):
            cs = pl.multiple_of(jnp.clip(start + c * QB, 0, T_LAT - QB), QB)
            kws.append(k_ref[pl.ds(cs, QB), :])
            vws.append(v_ref[pl.ds(cs, QB), :])
        kw = jnp.concatenate(kws, axis=0).astype(BF16)
        vw = jnp.concatenate(vws, axis=0).astype(BF16)
        qpos = start + (lax.broadcasted_iota(I32, (2 * QB, SPAN), 0) & (QB - 1))
        kpos = start - WINDOW + lax.broadcasted_iota(I32, (2 * QB, SPAN), 1)
        valid = (kpos >= 0) & (kpos < T_LAT) & (jnp.abs(qpos - kpos) <= WINDOW)
        for j, sl in enumerate(kv_heads):
            kw_halves = _head_halves(kw[:, sl])
            vw_halves = _ones_halves(vw[:, sl])
            q2 = jnp.concatenate([q_ref[qrows, (2 * j) * LANES:(2 * j + 1) * LANES],
                                  q_ref[qrows, (2 * j + 1) * LANES:(2 * j + 2) * LANES]], axis=0)
            outs = []
            for half in range(2):
                s_win = lax.dot_general(q2, kw_halves[half], _NT_DIMS, preferred_element_type=F32)
                s_win = jnp.where(valid, s_win, NEG)
                s_ctx = lax.dot_general(q2, ck_halves[j][half], _NT_DIMS, preferred_element_type=F32)
                sk = _sink_col(sink_ref, (4 * j + half, 4 * j + 2 + half), QB)
                outs.append(_sink_attend([s_win, s_ctx], [vw_halves[half], cv_halves[j][half]], sk, half))
            for s2 in range(2):
                rows = slice(s2 * QB, (s2 + 1) * QB)
                o_ref[qrows, (2 * j + s2) * LANES:(2 * j + s2 + 1) * LANES] = (
                    jnp.where(lo, outs[0][rows], outs[1][rows]).astype(BF16))


def _lat_attn(q, k, v, ck, cv, sink):
    lat0 = N_CTX // T_LAT
    qrows = LAT_QBLOCKS * QB
    return pl.pallas_call(
        _lat_attn_body,
        grid=(N_LAT_B, T_LAT // qrows),
        in_specs=[pl.BlockSpec((qrows, ATTN_W), lambda b, i: (N_CTX // qrows + b * (T_LAT // qrows) + i, 0)),
                  pl.BlockSpec((T_LAT, KV_W), lambda b, i: (lat0 + b, 0)),
                  pl.BlockSpec((T_LAT, KV_W), lambda b, i: (lat0 + b, 0)),
                  pl.BlockSpec((1, PAST, KV_W), lambda b, i: (b, 0, 0)),
                  pl.BlockSpec((1, PAST, KV_W), lambda b, i: (b, 0, 0)),
                  pl.BlockSpec(memory_space=pltpu.SMEM)],
        out_specs=pl.BlockSpec((qrows, ATTN_W), lambda b, i: (b * (T_LAT // qrows) + i, 0)),
        out_shape=jax.ShapeDtypeStruct((N_LAT, ATTN_W), BF16),
        compiler_params=_cparams(2),
        name="lat_attn",
    )(q, k, v, ck, cv, sink)


def _l0_out_body(oc_ref, ol_ref, u_ref, xp_ref, xs_ref, mod_ref, pw_ref, ps_ref, wo_ref,
                 g_ref, wr_ref, eb_ref, gb_ref, tri_ref, xext_ref, info_ref, cnt_ref, wb_ref, base_ref):
    i = pl.program_id(0)
    _bf16_once(wo_ref, wb_ref)
    is_ctx = i < N_CTX // TB_MIX
    o = jnp.where(is_ctx, oc_ref[...], ol_ref[...])
    x = jnp.where(is_ctx, xp_ref[...], xs_ref[...])
    tseq = jnp.where(is_ctx, T_CTX, T_LAT)
    pos = lax.broadcasted_iota(I32, (TB_MIX, LANES), 0) & (tseq - 1)

    def ahead(a, k):
        return jnp.where(pos + k < tseq, pltpu.roll(a, TB_MIX - k, axis=0), 0.0)

    def behind(a, k):
        return jnp.where(pos - k >= 0, pltpu.roll(a, k, axis=0), 0.0)

    ys = []
    for g, win in enumerate(POOL_WINDOWS):
        hw = win // 2
        ug = u_ref[:, g * LANES:(g + 1) * LANES]
        fwd, bwd, k = ug, behind(ug, 1), 1
        while k < hw:
            fwd = fwd + ahead(fwd, k)
            bwd = bwd + behind(bwd, k)
            k *= 2
        acc = fwd + bwd
        cnt = (jnp.minimum(pos + hw, tseq) - jnp.maximum(pos - hw, 0)).astype(F32)
        pooled = acc / cnt - ug
        ys.append(jnp.dot(pooled.astype(BF16), pw_ref[g].astype(BF16), preferred_element_type=F32))
    y = jnp.concatenate(ys, axis=1) * ps_ref[...]
    out = (jnp.dot(o, wb_ref[0:ATTN_W, :], preferred_element_type=F32)
           + jnp.dot(y.astype(BF16), wb_ref[ATTN_W:ATTN_W + POOL_W, :], preferred_element_type=F32))
    x1 = x + mod_ref[0][:, 2 * D:3 * D] * out
    _route_tile(x1, mod_ref[0], _mod_row(i, N_CTX // TB_MIX, 1), 0, g_ref, wr_ref, eb_ref, gb_ref, tri_ref,
                xext_ref, info_ref, cnt_ref, base_ref)


def _l0_out(o_ctx, o_lat, u, xp, xs, mods, pool_w, pool_scale, w_out, route_ops):
    ntc = N_CTX // TB_MIX
    r_in, r_out, r_shape, r_scratch = _route_specs(TB_MIX, lambda i: i)
    const = lambda shape: pl.BlockSpec(shape, lambda i: (0,) * len(shape))
    ctx_map = lambda i: (jnp.minimum(i, ntc - 1), 0)
    lat_map = lambda i: (jnp.maximum(i - ntc, 0), 0)
    return pl.pallas_call(
        _l0_out_body,
        grid=(N_TOK // TB_MIX,),
        in_specs=[pl.BlockSpec((TB_MIX, ATTN_W), ctx_map),
                  pl.BlockSpec((TB_MIX, ATTN_W), lat_map),
                  pl.BlockSpec((TB_MIX, POOL_W), lambda i: (i, 0)),
                  pl.BlockSpec((TB_MIX, D), ctx_map),
                  pl.BlockSpec((TB_MIX, D), lat_map),
                  pl.BlockSpec((1, 1, 6 * D), lambda i: (_mod_row(i, ntc, 1), 0, 0)),
                  const((len(POOL_WINDOWS), LANES, LANES)), const((1, POOL_W)), const((D, D))] + r_in,
        out_specs=r_out,
        out_shape=r_shape,
        scratch_shapes=[pltpu.VMEM((D, D), BF16), r_scratch],
        compiler_params=_cparams(1, VMEM_LIMIT),
        name="l0_out",
    )(o_ctx, o_lat, u, xp, xs, mods, pool_w, pool_scale, w_out, *route_ops)


def _first_max(vals):
    best, idx = vals[0], jnp.zeros(vals[0].shape, I32)
    for r in range(1, len(vals)):
        better = vals[r] > best
        idx = jnp.where(better, r, idx)
        best = jnp.where(better, vals[r], best)
    return best, idx


def _softmax_rows(rows):
    mx = functools.reduce(jnp.maximum, rows)
    ex = [jnp.exp(r - mx) for r in rows]
    tot = functools.reduce(lambda a, b: a + b, ex)
    return [e / tot for e in ex]


def _route_tile(x, m, mod_id, layer, g_ref, wr_ref, eb_ref, gb_ref, tri_ref, xext_ref, info_ref, cnt_ref,
                base_ref):
    t_rows = x.shape[0]

    @pl.when(pl.program_id(0) == 0)
    def _():
        base_ref[...] = jnp.zeros_like(base_ref)

    h = _modulate(x, g_ref[layer:layer + 1, :], m[:, 3 * D:4 * D], m[:, 4 * D:5 * D])

    hh = h.astype(BF16)
    hl = (h - hh.astype(F32)).astype(BF16)
    wh, wl = wr_ref[0], wr_ref[1]
    lg = (lax.dot_general(wh, hh, _NT_DIMS, preferred_element_type=F32)
          + lax.dot_general(wl, hh, _NT_DIMS, preferred_element_type=F32)
          + lax.dot_general(wh, hl, _NT_DIMS, preferred_element_type=F32))
    lg_e = [lg[e:e + 1] + eb_ref[layer, e] for e in range(N_EXPERTS)]
    lg_g = [lg[N_EXPERTS + g:N_EXPERTS + g + 1] + gb_ref[layer, g] for g in range(N_GROUPS)]

    pg = _softmax_rows(lg_g)
    pg_top, gi = _first_max(pg)
    le = []
    for j in range(PER_GROUP):
        sel = lg_e[(N_GROUPS - 1) * PER_GROUP + j]
        for g in range(N_GROUPS - 2, -1, -1):
            sel = jnp.where(gi == g, lg_e[g * PER_GROUP + j], sel)
        le.append(sel)
    pe = _softmax_rows(le)
    p1, i1 = _first_max(pe)
    p2, i2 = _first_max([jnp.where(i1 == j, -1.0, pe[j]) for j in range(PER_GROUP)])
    den = p1 + p2
    w1 = pg_top * p1 / den
    w2 = pg_top * p2 / den

    lo = jnp.minimum(i1, i2)
    hi = jnp.maximum(i1, i2)
    cls = gi * PAIRS + jnp.where(lo == 0, 0, jnp.where(lo == 1, 3, 5)) + hi - lo - 1
    w_lo = jnp.where(i1 == lo, w1, w2)
    w_hi = jnp.where(i1 == lo, w2, w1)

    crow = lax.broadcasted_iota(I32, (CLASS_ROWS, t_rows), 0)
    hit = crow == cls
    onehot = jnp.where(hit, 1.0, 0.0)
    before = jnp.dot(onehot.astype(BF16), tri_ref[...], preferred_element_type=F32)
    before = before + base_ref[:, 0:1]
    rank = jnp.sum(jnp.where(hit, before, 0.0), axis=0, keepdims=True)
    base_ref[...] = base_ref[...] + jnp.sum(onehot, axis=1, keepdims=True)
    cnt_ref[...] = base_ref[...].astype(I32)

    mod_id = jnp.zeros_like(w1) + mod_id.astype(F32)
    zero = jnp.zeros_like(w1)
    info_ref[...] = jnp.concatenate([cls.astype(F32), rank, zero, zero, zero, zero, zero, zero], axis=0)
    side = jnp.concatenate([w_lo, w_hi, mod_id, jnp.zeros((LANES - 3, t_rows), F32)], axis=0).T
    xext_ref[:, 0:D] = x
    xext_ref[:, D:XEXT] = side


def _route_specs(tile, step_map):
    const = lambda shape: pl.BlockSpec(shape, lambda *i: (0,) * len(shape))
    smem = pl.BlockSpec(memory_space=pltpu.SMEM)
    in_specs = [const((DEPTH, D)), const((2, ROUTE_ROWS, D)), smem, smem, const((tile, tile))]
    out_specs = [pl.BlockSpec((tile, XEXT), lambda *i: (step_map(*i), 0)),
                 pl.BlockSpec((SUBLANES, tile), lambda *i: (0, step_map(*i))),
                 const((CLASS_ROWS, LANES))]
    out_shape = [jax.ShapeDtypeStruct((N_TOK, XEXT), F32),
                 jax.ShapeDtypeStruct((SUBLANES, N_TOK), F32),
                 jax.ShapeDtypeStruct((CLASS_ROWS, LANES), I32)]
    return in_specs, out_specs, out_shape, pltpu.VMEM((CLASS_ROWS, LANES), F32)


def _route_operands(g, wr, e_b, g_b, tile):
    tri = jnp.asarray(np.arange(tile)[:, None] < np.arange(tile)[None, :], BF16)
    return g, wr, e_b, g_b, tri


def _plan_body(cnt_ref, info_ref, pos_ref, lo_ref, hi_ref, sa_ref, sb_ref, newa_ref, newb_ref, nu_ref, ch_ref,
               toff_ref, seen_ref):
    tm_shift = TM.bit_length() - 1

    def per_class(c, first_tile):
        cnt = cnt_ref[c, 0]
        tiles = lax.shift_right_logical(cnt + (TM - 1), tm_shift)
        toff_ref[c] = first_tile * TM
        group = lax.div(c, PAIRS)
        pair = c - group * PAIRS
        lo = (pair >= 3).astype(I32) + (pair >= 5).astype(I32)
        hi = jnp.where(pair < 3, pair + 1, jnp.where(pair < 5, pair - 1, 3))

        def per_tile(k, carry):
            n = first_tile + k
            lo_ref[n] = lo
            hi_ref[n] = hi
            sa_ref[n] = group * PER_GROUP + lo
            sb_ref[n] = group * PER_GROUP + hi
            rows = jnp.minimum(cnt - k * TM, TM)
            ch_ref[n] = lax.shift_right_logical(rows + (CH - 1), CH.bit_length() - 1)
            return carry

        lax.fori_loop(0, tiles, per_tile, 0)
        return first_tile + tiles

    n_used = lax.fori_loop(0, N_CLASS, per_class, 0)
    nu_ref[0] = n_used

    for e in range(N_EXPERTS):
        seen_ref[e] = 0

    def per_used(n, slots):
        e_lo, e_hi = sa_ref[n], sb_ref[n]
        new_a = seen_ref[e_lo] == 0
        seen_ref[e_lo] = 1
        new_b = seen_ref[e_hi] == 0
        seen_ref[e_hi] = 1
        newa_ref[n] = new_a.astype(I32)
        newb_ref[n] = new_b.astype(I32)
        slot_a = jnp.where(new_a, e_lo, slots[0])
        slot_b = jnp.where(new_b, e_hi, slots[1])
        sa_ref[n] = slot_a
        sb_ref[n] = slot_b
        return slot_a, slot_b

    slot_a, slot_b = lax.fori_loop(0, n_used, per_used, (jnp.int32(0), jnp.int32(0)))

    def unused(n, carry):
        lo_ref[n] = 0
        hi_ref[n] = 0
        sa_ref[n] = slot_a
        sb_ref[n] = slot_b
        newa_ref[n] = 0
        newb_ref[n] = 0
        ch_ref[n] = 0
        return carry

    lax.fori_loop(n_used, NT_FFN, unused, 0)

    cls = info_ref[0:1, :]
    first_row = jnp.zeros(cls.shape, I32)
    for c in range(N_CLASS):
        first_row = jnp.where(cls == float(c), toff_ref[c], first_row)
    pos = first_row + info_ref[1:2, :].astype(I32)
    pos_ref[...] = jnp.broadcast_to(pos, pos_ref.shape)


def _plan(info, counts, layer):
    smem = pl.BlockSpec(memory_space=pltpu.SMEM)
    per_tile = jax.ShapeDtypeStruct((NT_FFN,), I32)
    pos, lo, hi, sa, sb, newa, newb, n_used, chunks = pl.pallas_call(
        _plan_body,
        grid=(1,),
        in_specs=[smem, pl.BlockSpec((SUBLANES, N_TOK), lambda i: (0, 0))],
        out_specs=[pl.BlockSpec((SUBLANES, N_TOK), lambda i: (0, 0))] + [smem] * 8,
        out_shape=[jax.ShapeDtypeStruct((SUBLANES, N_TOK), I32)] + [per_tile] * 6
                  + [jax.ShapeDtypeStruct((1,), I32), per_tile],
        scratch_shapes=[pltpu.SMEM((CLASS_ROWS,), I32), pltpu.SMEM((N_EXPERTS,), I32)],
        compiler_params=_cparams(1),
        name=f"plan{layer}",
    )(counts, info)
    return pos[0], lo, hi, (sa, sb, newa, newb), n_used, chunks


OCT = TM // SUBLANES


def _ffn_body(layer, pos_ref, lo_ref, hi_ref, sa_ref, sb_ref, newa_ref, newb_ref, nu_ref, ch_ref,
              xext_hbm, mod_ref, g_ref, w1a_ref, w1b_ref, w3a_ref, w3b_ref, w2a_ref, w2b_ref,
              out_hbm, src_ref, dst_ref, xbuf, ybuf, wb1, wb3, wb2, gsem, ssem):
    n = pl.program_id(0)
    n_used = nu_ref[0]

    def gather_copy(tile, s, c, j):
        return pltpu.make_async_copy(
            xext_hbm.at[pl.ds(src_ref[tile * TM + c * CH + j], 1)],
            xbuf.at[s, c * (CH // SUBLANES) + j // SUBLANES, pl.ds(j % SUBLANES, 1)], gsem.at[s])

    def scatter_copy(tile, s, c, j):
        return pltpu.make_async_copy(
            ybuf.at[s, c * (CH // SUBLANES) + j // SUBLANES, pl.ds(j % SUBLANES, 1)],
            out_hbm.at[pl.ds(dst_ref[tile * TM + c * CH + j], 1)], ssem.at[s])

    def start_rows(copy, tile, s):
        n_chunks = ch_ref[tile]
        for c in range(TM // CH):
            @pl.when(c < n_chunks)
            def _():
                for j in range(CH):
                    copy(tile, s, c, j).start()

    def wait_rows(src, dst, sem, tile):
        def chunk(c, carry):
            pltpu.make_async_copy(src, dst, sem).wait()
            return carry
        lax.fori_loop(0, ch_ref[tile], chunk, 0)

    def wait_gather(tile, s):
        rows = xbuf.at[s, pl.ds(0, CH // SUBLANES)]
        wait_rows(rows, rows, gsem.at[s], tile)

    def wait_scatter(tile, s):
        rows = ybuf.at[s, pl.ds(0, CH // SUBLANES)]
        wait_rows(rows, rows, ssem.at[s], tile)

    @pl.when(n == 0)
    def _():
        def pad_rows(tile, carry):
            @pl.when(ch_ref[tile] > 0)
            def _():
                first = tile * TM + (ch_ref[tile] - 1) * CH
                for j in range(CH):
                    src_ref[first + j] = 0
                    dst_ref[first + j] = N_TOK + ((first + j) & (2 * TM - 1))
            return carry

        lax.fori_loop(0, NT_FFN, pad_rows, 0)

        def put(t, carry):
            p = pos_ref[t]
            src_ref[p] = t
            dst_ref[p] = t
            return carry

        lax.fori_loop(0, N_TOK, put, 0, unroll=8)

        xbuf[...] = jnp.zeros_like(xbuf)
        ybuf[...] = jnp.zeros_like(ybuf)
        for s in range(2):
            dumps = [pltpu.make_async_copy(
                ybuf.at[s, q], out_hbm.at[pl.ds(N_TOK + s * TM + q * SUBLANES, SUBLANES)], ssem.at[s])
                for q in range(OCT)]
            for dump in dumps:
                dump.start()
            for dump in dumps:
                dump.wait()
        start_rows(gather_copy, 0, 0)

    def step(slot):
        @pl.when(n + 1 < n_used)
        def _():
            start_rows(gather_copy, n + 1, 1 - slot)

        wait_gather(n, slot)

        @pl.when(n >= 2)
        def _():
            wait_scatter(n - 2, slot)

        e_lo = lo_ref[n]
        e_hi = hi_ref[n]

        @pl.when(newa_ref[n] == 1)
        def _():
            wb1[e_lo] = w1a_ref[0, 0].astype(BF16)
            wb3[e_lo] = w3a_ref[0, 0].astype(BF16)
            wb2[e_lo] = w2a_ref[0, 0].astype(BF16)

        @pl.when(newb_ref[n] == 1)
        def _():
            wb1[e_hi] = w1b_ref[0, 0].astype(BF16)
            wb3[e_hi] = w3b_ref[0, 0].astype(BF16)
            wb2[e_hi] = w2b_ref[0, 0].astype(BF16)

        xe = xbuf[slot].reshape(TM, XEXT)
        x = xe[:, 0:D]
        w_lo = xe[:, D:D + 1]
        w_hi = xe[:, D + 1:D + 2]
        mod_id = xe[:, D + 2:D + 3]

        def pick(lo, hi):
            return jnp.where(mod_id < 0.5, mod_ref[0][:, lo:hi],
                             jnp.where(mod_id < 1.5, mod_ref[1][:, lo:hi], mod_ref[2][:, lo:hi]))

        h = _modulate(x, g_ref[layer:layer + 1, :], pick(3 * D, 4 * D), pick(4 * D, 5 * D)).astype(BF16)

        def act(e, w):
            h1 = jnp.dot(h, wb1[e], preferred_element_type=F32)
            h3 = jnp.dot(h, wb3[e], preferred_element_type=F32)
            return ((h1 * jax.nn.sigmoid(h1)) * h3 * w).astype(BF16)

        y = (jnp.dot(act(e_lo, w_lo), wb2[e_lo], preferred_element_type=F32)
             + jnp.dot(act(e_hi, w_hi), wb2[e_hi], preferred_element_type=F32))
        ybuf[slot] = (x + pick(5 * D, 6 * D) * y).reshape(OCT, SUBLANES, D)
        start_rows(scatter_copy, n, slot)

        @pl.when(n == n_used - 1)
        def _():
            @pl.when(n >= 1)
            def _():
                wait_scatter(n - 1, 1 - slot)
            wait_scatter(n, slot)

    for s in range(2):
        @pl.when((n < n_used) & (n % 2 == s))
        def _():
            step(s)


def _ffn(pos, lo, hi, slots, n_used, chunks, xext, mods, layer, g, w1, w3, w2):
    a_map = lambda n, p, lo, hi, sa, sb, na, nb, nu, ch: (layer, sa[n], 0, 0)
    b_map = lambda n, p, lo, hi, sa, sb, na, nb, nu, ch: (layer, sb[n], 0, 0)
    up = lambda imap: pl.BlockSpec((1, 1, D, D_EXPERT), imap)
    down = lambda imap: pl.BlockSpec((1, 1, D_EXPERT, D), imap)
    return pl.pallas_call(
        functools.partial(_ffn_body, layer),
        grid_spec=pltpu.PrefetchScalarGridSpec(
            num_scalar_prefetch=9, grid=(NT_FFN,),
            in_specs=[pl.BlockSpec(memory_space=pl.ANY),
                      pl.BlockSpec((SUBLANES, 1, 6 * D), lambda n, *_: (layer, 0, 0)),
                      pl.BlockSpec((DEPTH, D), lambda n, *_: (0, 0)),
                      up(a_map), up(b_map), up(a_map), up(b_map), down(a_map), down(b_map)],
            out_specs=pl.BlockSpec(memory_space=pl.ANY),
            scratch_shapes=[pltpu.SMEM((P_FFN,), I32), pltpu.SMEM((P_FFN,), I32),
                            pltpu.VMEM((2, OCT, SUBLANES, XEXT), F32),
                            pltpu.VMEM((2, OCT, SUBLANES, D), F32),
                            pltpu.VMEM((PER_GROUP, D, D_EXPERT), BF16),
                            pltpu.VMEM((PER_GROUP, D, D_EXPERT), BF16),
                            pltpu.VMEM((PER_GROUP, D_EXPERT, D), BF16),
                            pltpu.SemaphoreType.DMA((2,)), pltpu.SemaphoreType.DMA((2,))]),
        out_shape=jax.ShapeDtypeStruct((N_TOK + 2 * TM, D), F32),
        compiler_params=_cparams(1, VMEM_LIMIT),
        name=f"ffn{layer}",
    )(pos, lo, hi, *slots, n_used, chunks, xext, mods, g, w1, w1, w3, w3, w2, w2)


def _moe(routed, mods, layer, g, w1, w3, w2):
    xext, info, counts = routed
    pos, lo, hi, slots, n_used, chunks = _plan(info, counts, layer)
    return _ffn(pos, lo, hi, slots, n_used, chunks, xext, mods, layer, g, w1, w3, w2)


FG = 256


def _l1_in_body(x_ref, mod_ref, g_ref, w_ref, c_ref, s_ref, zc_ref, zs_ref, wb_ref):
    _bf16_once(w_ref, wb_ref)
    m = mod_ref[0]
    h = _modulate(x_ref[...], g_ref[1:2, :], m[:, 0:D], m[:, D:2 * D])
    z = jnp.dot(h.astype(BF16), wb_ref[...], preferred_element_type=F32).astype(BF16)
    for g in range(D // FG):
        zg = z[:, g * FG:(g + 1) * FG]
        zc_ref[:, g * FG:(g + 1) * FG] = jnp.dot(zg, c_ref[...], preferred_element_type=F32).astype(BF16)
        zs_ref[:, g * FG:(g + 1) * FG] = jnp.dot(zg, s_ref[...], preferred_element_type=F32).astype(BF16)


def _l1_in(x, mods, g, w, c256, s256):
    const = lambda shape: pl.BlockSpec(shape, lambda i: (0,) * len(shape))
    return pl.pallas_call(
        _l1_in_body,
        grid=(NT,),
        in_specs=[pl.BlockSpec((TB, D), lambda i: (i, 0)),
                  pl.BlockSpec((1, 1, 6 * D),
                               lambda i: (SUBLANES + _mod_row(i, NT_CTX, T_LAT // TB), 0, 0)),
                  const((DEPTH, D)), const((D, D)), const((FG, FG)), const((FG, FG))],
        out_specs=[pl.BlockSpec((TB, D), lambda i: (i, 0)), pl.BlockSpec((TB, D), lambda i: (i, 0))],
        out_shape=[jax.ShapeDtypeStruct((N_TOK, D), BF16), jax.ShapeDtypeStruct((N_TOK, D), BF16)],
        scratch_shapes=[pltpu.VMEM((D, D), BF16)],
        compiler_params=_cparams(1),
        name="l1_in",
    )(x, mods, g, w, c256, s256)


def _l1_out_body(zc_t_ref, zs_t_ref, zc_q_ref, zs_q_ref, c256_ref, s256_ref, c1k_ref, s1k_ref,
                 x_ref, mod_ref, wo_ref, g_ref, wr_ref, eb_ref, gb_ref, tri_ref,
                 xext_ref, info_ref, cnt_ref, f_ref, wb_ref, base_ref):
    i = pl.program_id(0)
    _bf16_once(wo_ref, wb_ref)

    @pl.when(i < NT_CTX)
    def _():
        for q in range(TB // T_CTX):
            rows = slice(q * T_CTX, (q + 1) * T_CTX)
            f = (jnp.dot(c256_ref[...], zc_t_ref[rows, :], preferred_element_type=F32)
                 - jnp.dot(s256_ref[...], zs_t_ref[rows, :], preferred_element_type=F32))
            f_ref[rows, :] = f.astype(BF16)

    @pl.when(i >= NT_CTX)
    def _():
        f = (jnp.dot(c1k_ref[...], zc_q_ref[...], preferred_element_type=F32)
             - jnp.dot(s1k_ref[...], zs_q_ref[...], preferred_element_type=F32))
        f_ref[...] = f.astype(BF16)

    out = jnp.dot(f_ref[...], wb_ref[...], preferred_element_type=F32)
    x3 = x_ref[...] + mod_ref[0][:, 2 * D:3 * D] * out
    _route_tile(x3, mod_ref[0], _mod_row(i, NT_CTX, T_LAT // TB), 1, g_ref, wr_ref, eb_ref, gb_ref, tri_ref,
                xext_ref, info_ref, cnt_ref, base_ref)


def _l1_out(zc, zs, c256, s256, c1k, s1k, x, mods, w_out, route_ops):
    const = lambda shape: pl.BlockSpec(shape, lambda i: (0,) * len(shape))
    r_in, r_out, r_shape, r_scratch = _route_specs(TB, lambda i: i)
    tile_map = lambda i: (jnp.minimum(i, NT_CTX - 1), 0)
    seq_map = lambda i: (N_CTX // T_LAT + jnp.maximum(i - NT_CTX, 0) // (T_LAT // TB), 0)
    row_map = lambda i: (jnp.maximum(i - NT_CTX, 0) % (T_LAT // TB), 0)
    return pl.pallas_call(
        _l1_out_body,
        grid=(NT,),
        in_specs=[pl.BlockSpec((TB, D), tile_map), pl.BlockSpec((TB, D), tile_map),
                  pl.BlockSpec((T_LAT, D), seq_map), pl.BlockSpec((T_LAT, D), seq_map),
                  const((T_CTX, T_CTX)), const((T_CTX, T_CTX)),
                  pl.BlockSpec((TB, T_LAT), row_map), pl.BlockSpec((TB, T_LAT), row_map),
                  pl.BlockSpec((TB, D), lambda i: (i, 0)),
                  pl.BlockSpec((1, 1, 6 * D),
                               lambda i: (SUBLANES + _mod_row(i, NT_CTX, T_LAT // TB), 0, 0)),
                  const((D, D))] + r_in,
        out_specs=r_out,
        out_shape=r_shape,
        scratch_shapes=[pltpu.VMEM((TB, D), BF16), pltpu.VMEM((D, D), BF16), r_scratch],
        compiler_params=_cparams(1, VMEM_LIMIT),
        name="l1_out",
    )(zc, zs, zc, zs, c256, s256, c1k, s1k, x, mods, w_out, *route_ops)


def kernel(x_prompt, x_sample, cache_k, cache_v, c, c_ctx, ada_w, ada_b, norm_mix, norm_ffn, a_w_in, a_q_norm, a_k_norm, a_sink, pool_w, pool_scale, a_w_out, f_w_in, f_w_out, router_g_w, router_g_b, router_e_w, router_e_b, moe_w1, moe_w3, moe_w2):
    xp = x_prompt.reshape(N_CTX, D)
    xs = x_sample.reshape(N_LAT, D)

    mods = _adaln(c_ctx, c, ada_w, ada_b)

    tabs = _rope_tables()
    lane = np.arange(LANES)
    bd = jnp.asarray((lane[:, None] // HEAD_DIM) == (lane[None, :] // HEAD_DIM), BF16)
    c256, s256 = _dft_tables(T_CTX)
    c1k, s1k = _dft_tables(T_LAT)

    rw = jnp.swapaxes(jnp.concatenate([router_e_w, router_g_w], axis=2), 1, 2)
    rw = jnp.pad(rw, ((0, 0), (0, ROUTE_ROWS - rw.shape[1]), (0, 0)))
    rw_hi = rw.astype(BF16)
    rw_split = jnp.stack([rw_hi, (rw - rw_hi.astype(F32)).astype(BF16)], axis=1)

    def route_operands(l, tile):
        return _route_operands(norm_ffn, rw_split[l], router_e_b, router_g_b, tile)

    q, k, v, u, new_k, new_v = _l0_in(xp, xs, mods, norm_mix, a_w_in[0], a_q_norm, a_k_norm, bd, tabs)
    o_ctx = _ctx_attn(q, k, v, a_sink)
    ck = cache_k[:, 0].reshape(N_LAT_B, PAST, KV_W)
    cv = cache_v[:, 0].reshape(N_LAT_B, PAST, KV_W)
    o_lat = _lat_attn(q, k, v, ck, cv, a_sink)
    routed = _l0_out(o_ctx, o_lat, u, xp, xs, mods, pool_w[0], pool_scale[0][None, :], a_w_out[0],
                     route_operands(0, TB_MIX))
    x2 = _moe(routed, mods, 0, norm_ffn, moe_w1, moe_w3, moe_w2)

    zc, zs = _l1_in(x2, mods, norm_mix, f_w_in[0], c256, s256)
    routed = _l1_out(zc, zs, c256, s256, c1k, s1k, x2, mods, f_w_out[0], route_operands(1, TB))
    x4 = _moe(routed, mods, 1, norm_ffn, moe_w1, moe_w3, moe_w2)

    def cache_entry(t):
        t = t.reshape(N_CTX_B, 1, KV_W // HEAD_DIM, HEAD_DIM, T_CTX)
        return jnp.transpose(t, (0, 1, 4, 2, 3))

    new_k, new_v = cache_entry(new_k), cache_entry(new_v)
    return (x4[:N_CTX].reshape(N_CTX_B, T_CTX, D), x4[N_CTX:N_TOK].reshape(N_LAT_B, T_LAT, D),
            new_k, new_v)
```

```python
import functools

import numpy as np
import jax
import jax.numpy as jnp
from jax import lax
from jax.experimental import pallas as pl
from jax.experimental.pallas import tpu as pltpu

F32 = jnp.float32
BF16 = jnp.bfloat16
I32 = jnp.int32

D = 1024
DEPTH = 2
N_CTX_B, T_CTX = 16, 256
N_LAT_B, T_LAT = 2, 1024
N_CTX = N_CTX_B * T_CTX
N_LAT = N_LAT_B * T_LAT
N_TOK = N_CTX + N_LAT
PAST = 512
GRID_W = 64
HEAD_DIM = 64
N_HEADS = 8
ATTN_W = 512
KV_W = 128
POOL_W = 512
POOL_WINDOWS = (2, 4, 8, 16)
MIX_IN = ATTN_W + 2 * KV_W + POOL_W
WINDOW = 128
N_GROUPS = 4
PER_GROUP = 4
N_EXPERTS = 16
D_EXPERT = 512
ROPE_THETA = 10000.0
EPS = 1e-6
NEG = -1e30

LANES = 128
SUBLANES = 8
TB = 512
NT = N_TOK // TB
NT_CTX = N_CTX // TB
TB_MIX = 1024
TM = 256
PAIRS = 6
N_CLASS = N_GROUPS * PAIRS
CLASS_ROWS = 32
NT_FFN = N_TOK // TM + N_CLASS
P_FFN = NT_FFN * TM
CH = 32
XEXT = D + LANES
ROUTE_ROWS = 32

VMEM_LIMIT = 56 * 1024 * 1024


def _cparams(n_axes=1, vmem=None):
    return pltpu.CompilerParams(dimension_semantics=("arbitrary",) * n_axes,
                                vmem_limit_bytes=vmem)


def _modulate(x, g, shift, scale):
    ms = jnp.mean(x * x, axis=-1, keepdims=True)
    return (x * lax.rsqrt(ms + EPS) * g) * (1.0 + scale) + shift


def _bf16_once(w_ref, wb_ref):
    @pl.when(pl.program_id(0) == 0)
    def _():
        wb_ref[...] = w_ref[...].astype(BF16)


def _mod_row(tile, tiles_ctx, tiles_per_lat):
    return (tile >= tiles_ctx).astype(I32) + (tile >= tiles_ctx + tiles_per_lat).astype(I32)


def _rope_tables():
    t = np.arange(T_LAT)
    row = (t // GRID_W).astype(np.float64)
    col = (t % GRID_W).astype(np.float64)
    nf = HEAD_DIM // 4
    freqs = ROPE_THETA ** (-np.arange(nf, dtype=np.float64) / nf)
    d = np.arange(HEAD_DIM)
    pos = np.where(d[None, :] < HEAD_DIM // 2, row[:, None], col[:, None])
    ang = pos * freqs[d % nf][None, :]
    first = (d % (HEAD_DIM // 2)) < nf
    cos = np.cos(ang)
    sin_a = np.where(first[None, :], -np.sin(ang), 0.0)
    sin_b = np.where(first[None, :], 0.0, np.sin(ang))
    ident = (np.ones((TB, HEAD_DIM)), np.zeros((TB, HEAD_DIM)), np.zeros((TB, HEAD_DIM)))
    out = []
    for tab, idt in zip((cos, sin_a, sin_b), ident):
        full = np.concatenate([tab, idt], axis=0)
        out.append(jnp.asarray(np.tile(full, (1, LANES // HEAD_DIM)), F32))
    return out


def _dft_tables(t):
    m = np.outer(np.arange(t), np.arange(t)) % t
    ang = 2.0 * np.pi * m / t
    s = 1.0 / np.sqrt(t)
    return jnp.asarray(np.cos(ang) * s, F32).astype(BF16), jnp.asarray(np.sin(ang) * s, F32).astype(BF16)


def _adaln_body(cctx_ref, c_ref, w_ref, b_ref, o_ref):
    layer = pl.program_id(0)
    c = jnp.concatenate([cctx_ref[...], c_ref[...], jnp.zeros((SUBLANES - 1 - N_LAT_B, D), F32)], axis=0)
    s = (c * jax.nn.sigmoid(c)).astype(BF16)
    m = jnp.dot(s, w_ref[0].astype(BF16), preferred_element_type=F32) + b_ref[pl.ds(layer, 1), :]
    for r in range(SUBLANES):
        o_ref[r] = m[r:r + 1]


def _adaln(c_ctx, c, ada_w, ada_b):
    tn = 1536
    return pl.pallas_call(
        _adaln_body,
        grid=(DEPTH, 6 * D // tn),
        in_specs=[pl.BlockSpec((1, D), lambda l, j: (0, 0)),
                  pl.BlockSpec((N_LAT_B, D), lambda l, j: (0, 0)),
                  pl.BlockSpec((1, D, tn), lambda l, j: (l, 0, j)),
                  pl.BlockSpec((DEPTH, tn), lambda l, j: (0, j))],
        out_specs=pl.BlockSpec((SUBLANES, 1, tn), lambda l, j: (l, 0, j)),
        out_shape=jax.ShapeDtypeStruct((DEPTH * SUBLANES, 1, 6 * D), F32),
        compiler_params=_cparams(2),
        name="adaln",
    )(c_ctx.reshape(1, D), c, ada_w, ada_b)


def _l0_in_body(xp_ref, xs_ref, mod_ref, g_ref, w_ref, qg_ref, kg_ref, bd_ref,
                cos_ref, sa_ref, sb_ref, q_ref, k_ref, v_ref, u_ref, kc_ref, vc_ref, wb_ref):
    i = pl.program_id(0)
    _bf16_once(w_ref, wb_ref)
    x = jnp.where(i < NT_CTX, xp_ref[...], xs_ref[...])
    m = mod_ref[0]
    h = _modulate(x, g_ref[0:1, :], m[:, 0:D], m[:, D:2 * D])
    z = jnp.dot(h.astype(BF16), wb_ref[...], preferred_element_type=F32)
    cos, sa, sb, bd = cos_ref[...], sa_ref[...], sb_ref[...], bd_ref[...]
    qg = jnp.concatenate([qg_ref[...]] * (LANES // HEAD_DIM), axis=1)
    kg = jnp.concatenate([kg_ref[...]] * (LANES // HEAD_DIM), axis=1)

    def head_norm_rope(zz, gain):
        ss = jnp.dot((zz * zz).astype(BF16), bd, preferred_element_type=F32)
        y = zz * lax.rsqrt(ss * (1.0 / HEAD_DIM) + EPS) * gain
        return (y * cos + pltpu.roll(y, LANES - 16, axis=1) * sa
                + pltpu.roll(y, 16, axis=1) * sb)

    for s in range(ATTN_W // LANES):
        qs = head_norm_rope(z[:, s * LANES:(s + 1) * LANES], qg)
        q_ref[:, s * LANES:(s + 1) * LANES] = (qs * (HEAD_DIM ** -0.5)).astype(BF16)
    k = head_norm_rope(z[:, ATTN_W:ATTN_W + KV_W], kg)
    v = z[:, ATTN_W + KV_W:ATTN_W + 2 * KV_W]
    k_ref[...] = k
    v_ref[...] = v
    u_ref[...] = z[:, ATTN_W + 2 * KV_W:MIX_IN]

    @pl.when(i < NT_CTX)
    def _():
        for q in range(TB // T_CTX):
            kc_ref[q] = k[q * T_CTX:(q + 1) * T_CTX, :].T
            vc_ref[q] = v[q * T_CTX:(q + 1) * T_CTX, :].T


def _l0_in(xp, xs, mods, g, w_in, qg, kg, bd, tabs):
    tab_spec = pl.BlockSpec(
        (TB, LANES), lambda i: (jnp.where(i < NT_CTX, T_LAT // TB, (i - NT_CTX) % (T_LAT // TB)), 0))
    const = lambda shape: pl.BlockSpec(shape, lambda i: (0,) * len(shape))
    return pl.pallas_call(
        _l0_in_body,
        grid=(NT,),
        in_specs=[pl.BlockSpec((TB, D), lambda i: (jnp.minimum(i, NT_CTX - 1), 0)),
                  pl.BlockSpec((TB, D), lambda i: (jnp.maximum(i - NT_CTX, 0), 0)),
                  pl.BlockSpec((1, 1, 6 * D), lambda i: (_mod_row(i, NT_CTX, T_LAT // TB), 0, 0)),
                  const((DEPTH, D)), const((D, MIX_IN)), const((1, HEAD_DIM)), const((1, HEAD_DIM)),
                  const((LANES, LANES)), tab_spec, tab_spec, tab_spec],
        out_specs=[pl.BlockSpec((TB, ATTN_W), lambda i: (i, 0)),
                   pl.BlockSpec((TB, KV_W), lambda i: (i, 0)),
                   pl.BlockSpec((TB, KV_W), lambda i: (i, 0)),
                   pl.BlockSpec((TB, POOL_W), lambda i: (i, 0)),
                   pl.BlockSpec((TB // T_CTX, KV_W, T_CTX), lambda i: (jnp.minimum(i, NT_CTX - 1), 0, 0)),
                   pl.BlockSpec((TB // T_CTX, KV_W, T_CTX), lambda i: (jnp.minimum(i, NT_CTX - 1), 0, 0))],
        out_shape=[jax.ShapeDtypeStruct((N_TOK, ATTN_W), BF16),
                   jax.ShapeDtypeStruct((N_TOK, KV_W), F32),
                   jax.ShapeDtypeStruct((N_TOK, KV_W), F32),
                   jax.ShapeDtypeStruct((N_TOK, POOL_W), F32),
                   jax.ShapeDtypeStruct((N_CTX_B, KV_W, T_CTX), F32),
                   jax.ShapeDtypeStruct((N_CTX_B, KV_W, T_CTX), F32)],
        scratch_shapes=[pltpu.VMEM((D, MIX_IN), BF16)],
        compiler_params=_cparams(1),
        name="l0_in",
    )(xp, xs, mods, g, w_in, qg, kg, bd, *tabs)


def _head_halves(x):
    z = jnp.zeros_like(x)
    return jnp.concatenate([x, z], axis=1), jnp.concatenate([z, x], axis=1)


_NT_DIMS = (((1,), (1,)), ((), ()))


def _ones_halves(x):
    one = jnp.ones_like(x)
    return jnp.concatenate([x, one], axis=1), jnp.concatenate([one, x], axis=1)


def _sink_attend(scores, values, sk, half):
    mx = sk
    for sc in scores:
        mx = jnp.maximum(mx, jnp.max(sc, axis=-1, keepdims=True))
    acc = None
    for sc, val in zip(scores, values):
        part = jnp.dot(jnp.exp(sc - mx).astype(BF16), val, preferred_element_type=F32)
        acc = part if acc is None else acc + part
    ones_lane = HEAD_DIM * (1 - half)
    den = acc[:, ones_lane:ones_lane + 1] + jnp.exp(sk - mx)
    return acc * (1.0 / den)


def _sink_col(sink_ref, heads, rows):
    return jnp.concatenate([jnp.full((rows, 1), sink_ref[0, h], F32) for h in heads], axis=0)


CTX_SEQS = 8


def _ctx_attn_body(q_ref, k_ref, v_ref, sink_ref, o_ref):
    lo = lax.broadcasted_iota(I32, (T_CTX, LANES), 1) < HEAD_DIM
    for b, j in ((b, j) for b in range(CTX_SEQS) for j in range(KV_W // HEAD_DIM)):
        seq = slice(b * T_CTX, (b + 1) * T_CTX)
        kj = k_ref[seq, j * HEAD_DIM:(j + 1) * HEAD_DIM].astype(BF16)
        vj = v_ref[seq, j * HEAD_DIM:(j + 1) * HEAD_DIM].astype(BF16)
        k_halves = _head_halves(kj)
        vd = jnp.concatenate([vj, vj], axis=1)
        q2 = jnp.concatenate([q_ref[seq, (2 * j) * LANES:(2 * j + 1) * LANES],
                              q_ref[seq, (2 * j + 1) * LANES:(2 * j + 2) * LANES]], axis=0)
        outs = []
        for half in range(2):
            sc = lax.dot_general(q2, k_halves[half], _NT_DIMS, preferred_element_type=F32)
            sk = _sink_col(sink_ref, (4 * j + half, 4 * j + 2 + half), T_CTX)
            mx = jnp.maximum(sk, jnp.max(sc, axis=-1, keepdims=True))
            p = jnp.exp(sc - mx)
            inv = 1.0 / (jnp.exp(sk - mx) + jnp.sum(p, axis=-1, keepdims=True))
            outs.append(jnp.dot((p * inv).astype(BF16), vd, preferred_element_type=F32))
        for s2 in range(2):
            rows = slice(s2 * T_CTX, (s2 + 1) * T_CTX)
            o_ref[seq, (2 * j + s2) * LANES:(2 * j + s2 + 1) * LANES] = (
                jnp.where(lo, outs[0][rows], outs[1][rows]).astype(BF16))


def _ctx_attn(q, k, v, sink):
    rows = CTX_SEQS * T_CTX
    return pl.pallas_call(
        _ctx_attn_body,
        grid=(N_CTX_B // CTX_SEQS,),
        in_specs=[pl.BlockSpec((rows, ATTN_W), lambda b: (b, 0)),
                  pl.BlockSpec((rows, KV_W), lambda b: (b, 0)),
                  pl.BlockSpec((rows, KV_W), lambda b: (b, 0)),
                  pl.BlockSpec(memory_space=pltpu.SMEM)],
        out_specs=pl.BlockSpec((rows, ATTN_W), lambda b: (b, 0)),
        out_shape=jax.ShapeDtypeStruct((N_CTX, ATTN_W), BF16),
        compiler_params=_cparams(1),
        name="ctx_attn",
    )(q, k, v, sink)


QB = 128
SPAN = QB + 2 * WINDOW


LAT_QBLOCKS = 4


def _lat_attn_body(q_ref, k_ref, v_ref, ck_ref, cv_ref, sink_ref, o_ref):
    ck = ck_ref[0].astype(BF16)
    cv = cv_ref[0].astype(BF16)
    lo = lax.broadcasted_iota(I32, (QB, LANES), 1) < HEAD_DIM
    kv_heads = [slice(j * HEAD_DIM, (j + 1) * HEAD_DIM) for j in range(KV_W // HEAD_DIM)]
    ck_halves = [_head_halves(ck[:, sl]) for sl in kv_heads]
    cv_halves = [_ones_halves(cv[:, sl]) for sl in kv_heads]
    for r in range(LAT_QBLOCKS):
        start = (pl.program_id(1) * LAT_QBLOCKS + r) * QB
        qrows = slice(r * QB, (r + 1) * QB)
        kws, vws = [], []
        for c in (-1, 0, 1):
            cs = pl.multiple_of(jnp.clip(start + c * QB, 0, T_LAT - QB), QB)
            kws.append(k_ref[pl.ds(cs, QB), :])
            vws.append(v_ref[pl.ds(cs, QB), :])
        kw = jnp.concatenate(kws, axis=0).astype(BF16)
        vw = jnp.concatenate(vws, axis=0).astype(BF16)
        qpos = start + (lax.broadcasted_iota(I32, (2 * QB, SPAN), 0) & (QB - 1))
        kpos = start - WINDOW + lax.broadcasted_iota(I32, (2 * QB, SPAN), 1)
        valid = (kpos >= 0) & (kpos < T_LAT) & (jnp.abs(qpos - kpos) <= WINDOW)
        for j, sl in enumerate(kv_heads):
            kw_halves = _head_halves(kw[:, sl])
            vw_halves = _ones_halves(vw[:, sl])
            q2 = jnp.concatenate([q_ref[qrows, (2 * j) * LANES:(2 * j + 1) * LANES],
                                  q_ref[qrows, (2 * j + 1) * LANES:(2 * j + 2) * LANES]], axis=0)
            outs = []
            for half in range(2):
                s_win = lax.dot_general(q2, kw_halves[half], _NT_DIMS, preferred_element_type=F32)
                s_win = jnp.where(valid, s_win, NEG)
                s_ctx = lax.dot_general(q2, ck_halves[j][half], _NT_DIMS, preferred_element_type=F32)
                sk = _sink_col(sink_ref, (4 * j + half, 4 * j + 2 + half), QB)
                outs.append(_sink_attend([s_win, s_ctx], [vw_halves[half], cv_halves[j][half]], sk, half))
            for s2 in range(2):
                rows = slice(s2 * QB, (s2 + 1) * QB)
                o_ref[qrows, (2 * j + s2) * LANES:(2 * j + s2 + 1) * LANES] = (
                    jnp.where(lo, outs[0][rows], outs[1][rows]).astype(BF16))


def _lat_attn(q, k, v, ck, cv, sink):
    lat0 = N_CTX // T_LAT
    qrows = LAT_QBLOCKS * QB
    return pl.pallas_call(
        _lat_attn_body,
        grid=(N_LAT_B, T_LAT // qrows),
        in_specs=[pl.BlockSpec((qrows, ATTN_W), lambda b, i: (N_CTX // qrows + b * (T_LAT // qrows) + i, 0)),
                  pl.BlockSpec((T_LAT, KV_W), lambda b, i: (lat0 + b, 0)),
                  pl.BlockSpec((T_LAT, KV_W), lambda b, i: (lat0 + b, 0)),
                  pl.BlockSpec((1, PAST, KV_W), lambda b, i: (b, 0, 0)),
                  pl.BlockSpec((1, PAST, KV_W), lambda b, i: (b, 0, 0)),
                  pl.BlockSpec(memory_space=pltpu.SMEM)],
        out_specs=pl.BlockSpec((qrows, ATTN_W), lambda b, i: (b * (T_LAT // qrows) + i, 0)),
        out_shape=jax.ShapeDtypeStruct((N_LAT, ATTN_W), BF16),
        compiler_params=_cparams(2),
        name="lat_attn",
    )(q, k, v, ck, cv, sink)


def _l0_out_body(oc_ref, ol_ref, u_ref, xp_ref, xs_ref, mod_ref, pw_ref, ps_ref, wo_ref,
                 g_ref, wr_ref, eb_ref, gb_ref, tri_ref, xext_ref, info_ref, cnt_ref, wb_ref, base_ref):
    i = pl.program_id(0)
    _bf16_once(wo_ref, wb_ref)
    is_ctx = i < N_CTX // TB_MIX
    o = jnp.where(is_ctx, oc_ref[...], ol_ref[...])
    x = jnp.where(is_ctx, xp_ref[...], xs_ref[...])
    tseq = jnp.where(is_ctx, T_CTX, T_LAT)
    pos = lax.broadcasted_iota(I32, (TB_MIX, LANES), 0) & (tseq - 1)

    def ahead(a, k):
        return jnp.where(pos + k < tseq, pltpu.roll(a, TB_MIX - k, axis=0), 0.0)

    def behind(a, k):
        return jnp.where(pos - k >= 0, pltpu.roll(a, k, axis=0), 0.0)

    ys = []
    for g, win in enumerate(POOL_WINDOWS):
        hw = win // 2
        ug = u_ref[:, g * LANES:(g + 1) * LANES]
        fwd, bwd, k = ug, behind(ug, 1), 1
        while k < hw:
            fwd = fwd + ahead(fwd, k)
            bwd = bwd + behind(bwd, k)
            k *= 2
        acc = fwd + bwd
        cnt = (jnp.minimum(pos + hw, tseq) - jnp.maximum(pos - hw, 0)).astype(F32)
        pooled = acc / cnt - ug
        ys.append(jnp.dot(pooled.astype(BF16), pw_ref[g].astype(BF16), preferred_element_type=F32))
    y = jnp.concatenate(ys, axis=1) * ps_ref[...]
    out = (jnp.dot(o, wb_ref[0:ATTN_W, :], preferred_element_type=F32)
           + jnp.dot(y.astype(BF16), wb_ref[ATTN_W:ATTN_W + POOL_W, :], preferred_element_type=F32))
    x1 = x + mod_ref[0][:, 2 * D:3 * D] * out
    _route_tile(x1, mod_ref[0], _mod_row(i, N_CTX // TB_MIX, 1), 0, g_ref, wr_ref, eb_ref, gb_ref, tri_ref,
                xext_ref, info_ref, cnt_ref, base_ref)


def _l0_out(o_ctx, o_lat, u, xp, xs, mods, pool_w, pool_scale, w_out, route_ops):
    ntc = N_CTX // TB_MIX
    r_in, r_out, r_shape, r_scratch = _route_specs(TB_MIX, lambda i: i)
    const = lambda shape: pl.BlockSpec(shape, lambda i: (0,) * len(shape))
    ctx_map = lambda i: (jnp.minimum(i, ntc - 1), 0)
    lat_map = lambda i: (jnp.maximum(i - ntc, 0), 0)
    return pl.pallas_call(
        _l0_out_body,
        grid=(N_TOK // TB_MIX,),
        in_specs=[pl.BlockSpec((TB_MIX, ATTN_W), ctx_map),
                  pl.BlockSpec((TB_MIX, ATTN_W), lat_map),
                  pl.BlockSpec((TB_MIX, POOL_W), lambda i: (i, 0)),
                  pl.BlockSpec((TB_MIX, D), ctx_map),
                  pl.BlockSpec((TB_MIX, D), lat_map),
                  pl.BlockSpec((1, 1, 6 * D), lambda i: (_mod_row(i, ntc, 1), 0, 0)),
                  const((len(POOL_WINDOWS), LANES, LANES)), const((1, POOL_W)), const((D, D))] + r_in,
        out_specs=r_out,
        out_shape=r_shape,
        scratch_shapes=[pltpu.VMEM((D, D), BF16), r_scratch],
        compiler_params=_cparams(1, VMEM_LIMIT),
        name="l0_out",
    )(o_ctx, o_lat, u, xp, xs, mods, pool_w, pool_scale, w_out, *route_ops)


def _first_max(vals):
    best, idx = vals[0], jnp.zeros(vals[0].shape, I32)
    for r in range(1, len(vals)):
        better = vals[r] > best
        idx = jnp.where(better, r, idx)
        best = jnp.where(better, vals[r], best)
    return best, idx


def _softmax_rows(rows):
    mx = functools.reduce(jnp.maximum, rows)
    ex = [jnp.exp(r - mx) for r in rows]
    tot = functools.reduce(lambda a, b: a + b, ex)
    return [e / tot for e in ex]


def _route_tile(x, m, mod_id, layer, g_ref, wr_ref, eb_ref, gb_ref, tri_ref, xext_ref, info_ref, cnt_ref,
                base_ref):
    t_rows = x.shape[0]

    @pl.when(pl.program_id(0) == 0)
    def _():
        base_ref[...] = jnp.zeros_like(base_ref)

    h = _modulate(x, g_ref[layer:layer + 1, :], m[:, 3 * D:4 * D], m[:, 4 * D:5 * D])

    hh = h.astype(BF16)
    hl = (h - hh.astype(F32)).astype(BF16)
    wh, wl = wr_ref[0], wr_ref[1]
    lg = (lax.dot_general(wh, hh, _NT_DIMS, preferred_element_type=F32)
          + lax.dot_general(wl, hh, _NT_DIMS, preferred_element_type=F32)
          + lax.dot_general(wh, hl, _NT_DIMS, preferred_element_type=F32))
    lg_e = [lg[e:e + 1] + eb_ref[layer, e] for e in range(N_EXPERTS)]
    lg_g = [lg[N_EXPERTS + g:N_EXPERTS + g + 1] + gb_ref[layer, g] for g in range(N_GROUPS)]

    pg = _softmax_rows(lg_g)
    pg_top, gi = _first_max(pg)
    le = []
    for j in range(PER_GROUP):
        sel = lg_e[(N_GROUPS - 1) * PER_GROUP + j]
        for g in range(N_GROUPS - 2, -1, -1):
            sel = jnp.where(gi == g, lg_e[g * PER_GROUP + j], sel)
        le.append(sel)
    pe = _softmax_rows(le)
    p1, i1 = _first_max(pe)
    p2, i2 = _first_max([jnp.where(i1 == j, -1.0, pe[j]) for j in range(PER_GROUP)])
    den = p1 + p2
    w1 = pg_top * p1 / den
    w2 = pg_top * p2 / den

    lo = jnp.minimum(i1, i2)
    hi = jnp.maximum(i1, i2)
    cls = gi * PAIRS + jnp.where(lo == 0, 0, jnp.where(lo == 1, 3, 5)) + hi - lo - 1
    w_lo = jnp.where(i1 == lo, w1, w2)
    w_hi = jnp.where(i1 == lo, w2, w1)

    crow = lax.broadcasted_iota(I32, (CLASS_ROWS, t_rows), 0)
    hit = crow == cls
    onehot = jnp.where(hit, 1.0, 0.0)
    before = jnp.dot(onehot.astype(BF16), tri_ref[...], preferred_element_type=F32)
    before = before + base_ref[:, 0:1]
    rank = jnp.sum(jnp.where(hit, before, 0.0), axis=0, keepdims=True)
    base_ref[...] = base_ref[...] + jnp.sum(onehot, axis=1, keepdims=True)
    cnt_ref[...] = base_ref[...].astype(I32)

    mod_id = jnp.zeros_like(w1) + mod_id.astype(F32)
    zero = jnp.zeros_like(w1)
    info_ref[...] = jnp.concatenate([cls.astype(F32), rank, zero, zero, zero, zero, zero, zero], axis=0)
    side = jnp.concatenate([w_lo, w_hi, mod_id, jnp.zeros((LANES - 3, t_rows), F32)], axis=0).T
    xext_ref[:, 0:D] = x
    xext_ref[:, D:XEXT] = side


def _route_specs(tile, step_map):
    const = lambda shape: pl.BlockSpec(shape, lambda *i: (0,) * len(shape))
    smem = pl.BlockSpec(memory_space=pltpu.SMEM)
    in_specs = [const((DEPTH, D)), const((2, ROUTE_ROWS, D)), smem, smem, const((tile, tile))]
    out_specs = [pl.BlockSpec((tile, XEXT), lambda *i: (step_map(*i), 0)),
                 pl.BlockSpec((SUBLANES, tile), lambda *i: (0, step_map(*i))),
                 const((CLASS_ROWS, LANES))]
    out_shape = [jax.ShapeDtypeStruct((N_TOK, XEXT), F32),
                 jax.ShapeDtypeStruct((SUBLANES, N_TOK), F32),
                 jax.ShapeDtypeStruct((CLASS_ROWS, LANES), I32)]
    return in_specs, out_specs, out_shape, pltpu.VMEM((CLASS_ROWS, LANES), F32)


def _route_operands(g, wr, e_b, g_b, tile):
    tri = jnp.asarray(np.arange(tile)[:, None] < np.arange(tile)[None, :], BF16)
    return g, wr, e_b, g_b, tri


def _plan_body(cnt_ref, info_ref, pos_ref, lo_ref, hi_ref, sa_ref, sb_ref, newa_ref, newb_ref, nu_ref, ch_ref,
               toff_ref, seen_ref):
    tm_shift = TM.bit_length() - 1

    def per_class(c, first_tile):
        cnt = cnt_ref[c, 0]
        tiles = lax.shift_right_logical(cnt + (TM - 1), tm_shift)
        toff_ref[c] = first_tile * TM
        group = lax.div(c, PAIRS)
        pair = c - group * PAIRS
        lo = (pair >= 3).astype(I32) + (pair >= 5).astype(I32)
        hi = jnp.where(pair < 3, pair + 1, jnp.where(pair < 5, pair - 1, 3))

        def per_tile(k, carry):
            n = first_tile + k
            lo_ref[n] = lo
            hi_ref[n] = hi
            sa_ref[n] = group * PER_GROUP + lo
            sb_ref[n] = group * PER_GROUP + hi
            rows = jnp.minimum(cnt - k * TM, TM)
            ch_ref[n] = lax.shift_right_logical(rows + (CH - 1), CH.bit_length() - 1)
            return carry

        lax.fori_loop(0, tiles, per_tile, 0)
        return first_tile + tiles

    n_used = lax.fori_loop(0, N_CLASS, per_class, 0)
    nu_ref[0] = n_used

    for e in range(N_EXPERTS):
        seen_ref[e] = 0

    def per_used(n, slots):
        e_lo, e_hi = sa_ref[n], sb_ref[n]
        new_a = seen_ref[e_lo] == 0
        seen_ref[e_lo] = 1
        new_b = seen_ref[e_hi] == 0
        seen_ref[e_hi] = 1
        newa_ref[n] = new_a.astype(I32)
        newb_ref[n] = new_b.astype(I32)
        slot_a = jnp.where(new_a, e_lo, slots[0])
        slot_b = jnp.where(new_b, e_hi, slots[1])
        sa_ref[n] = slot_a
        sb_ref[n] = slot_b
        return slot_a, slot_b

    slot_a, slot_b = lax.fori_loop(0, n_used, per_used, (jnp.int32(0), jnp.int32(0)))

    def unused(n, carry):
        lo_ref[n] = 0
        hi_ref[n] = 0
        sa_ref[n] = slot_a
        sb_ref[n] = slot_b
        newa_ref[n] = 0
        newb_ref[n] = 0
        ch_ref[n] = 0
        return carry

    lax.fori_loop(n_used, NT_FFN, unused, 0)

    cls = info_ref[0:1, :]
    first_row = jnp.zeros(cls.shape, I32)
    for c in range(N_CLASS):
        first_row = jnp.where(cls == float(c), toff_ref[c], first_row)
    pos = first_row + info_ref[1:2, :].astype(I32)
    pos_ref[...] = jnp.broadcast_to(pos, pos_ref.shape)


def _plan(info, counts, layer):
    smem = pl.BlockSpec(memory_space=pltpu.SMEM)
    per_tile = jax.ShapeDtypeStruct((NT_FFN,), I32)
    pos, lo, hi, sa, sb, newa, newb, n_used, chunks = pl.pallas_call(
        _plan_body,
        grid=(1,),
        in_specs=[smem, pl.BlockSpec((SUBLANES, N_TOK), lambda i: (0, 0))],
        out_specs=[pl.BlockSpec((SUBLANES, N_TOK), lambda i: (0, 0))] + [smem] * 8,
        out_shape=[jax.ShapeDtypeStruct((SUBLANES, N_TOK), I32)] + [per_tile] * 6
                  + [jax.ShapeDtypeStruct((1,), I32), per_tile],
        scratch_shapes=[pltpu.SMEM((CLASS_ROWS,), I32), pltpu.SMEM((N_EXPERTS,), I32)],
        compiler_params=_cparams(1),
        name=f"plan{layer}",
    )(counts, info)
    return pos[0], lo, hi, (sa, sb, newa, newb), n_used, chunks


OCT = TM // SUBLANES


def _ffn_body(layer, pos_ref, lo_ref, hi_ref, sa_ref, sb_ref, newa_ref, newb_ref, nu_ref, ch_ref,
              xext_hbm, mod_ref, g_ref, w1a_ref, w1b_ref, w3a_ref, w3b_ref, w2a_ref, w2b_ref,
              out_hbm, row_ref, tail_ref, xbuf, ybuf, wb1, wb3, wb2, gsem, ssem):
    n = pl.program_id(0)
    n_used = nu_ref[0]

    def gather_copy(tile, s, c, j):
        return pltpu.make_async_copy(
            xext_hbm.at[pl.ds(row_ref[tile * TM + c * CH + j], 1)],
            xbuf.at[s, c * (CH // SUBLANES) + j // SUBLANES, pl.ds(j % SUBLANES, 1)], gsem.at[s])

    def scatter_copy(tile, s, c, j, last):
        row = tail_ref[tile * CH + j] if last else row_ref[tile * TM + c * CH + j]
        return pltpu.make_async_copy(
            ybuf.at[s, c * (CH // SUBLANES) + j // SUBLANES, pl.ds(j % SUBLANES, 1)],
            out_hbm.at[pl.ds(row, 1)], ssem.at[s])

    def start_gather(tile, s):
        n_chunks = ch_ref[tile]
        for c in range(TM // CH):
            @pl.when(c < n_chunks)
            def _():
                for j in range(CH):
                    gather_copy(tile, s, c, j).start()

    def start_scatter(tile, s):
        n_chunks = ch_ref[tile]
        for c in range(TM // CH):
            for last in (False, True):
                @pl.when((c == n_chunks - 1) if last else (c < n_chunks - 1))
                def _():
                    for j in range(CH):
                        scatter_copy(tile, s, c, j, last).start()

    def wait_rows(src, dst, sem, tile):
        def chunk(c, carry):
            pltpu.make_async_copy(src, dst, sem).wait()
            return carry
        lax.fori_loop(0, ch_ref[tile], chunk, 0)

    def wait_gather(tile, s):
        rows = xbuf.at[s, pl.ds(0, CH // SUBLANES)]
        wait_rows(rows, rows, gsem.at[s], tile)

    def wait_scatter(tile, s):
        rows = ybuf.at[s, pl.ds(0, CH // SUBLANES)]
        wait_rows(rows, rows, ssem.at[s], tile)

    @pl.when(n == 0)
    def _():
        def mark_padding(tile, carry):
            @pl.when(ch_ref[tile] > 0)
            def _():
                first = tile * TM + (ch_ref[tile] - 1) * CH
                for j in range(CH):
                    row_ref[first + j] = N_TOK
            return carry

        lax.fori_loop(0, n_used, mark_padding, 0)

        def put(t, carry):
            row_ref[pos_ref[t]] = t
            return carry

        lax.fori_loop(0, N_TOK, put, 0, unroll=8)

        def last_chunk(tile, carry):
            @pl.when(ch_ref[tile] > 0)
            def _():
                first = tile * TM + (ch_ref[tile] - 1) * CH
                for j in range(CH):
                    token = row_ref[first + j]
                    padding = token == N_TOK
                    tail_ref[tile * CH + j] = jnp.where(padding, N_TOK + ((first + j) & (2 * TM - 1)), token)
                    row_ref[first + j] = jnp.where(padding, 0, token)
            return carry

        lax.fori_loop(0, n_used, last_chunk, 0)

        xbuf[...] = jnp.zeros_like(xbuf)
        ybuf[...] = jnp.zeros_like(ybuf)
        for s in range(2):
            dumps = [pltpu.make_async_copy(
                ybuf.at[s, q], out_hbm.at[pl.ds(N_TOK + s * TM + q * SUBLANES, SUBLANES)], ssem.at[s])
                for q in range(OCT)]
            for dump in dumps:
                dump.start()
            for dump in dumps:
                dump.wait()
        start_gather(0, 0)

    def step(slot):
        @pl.when(n + 1 < n_used)
        def _():
            start_gather(n + 1, 1 - slot)

        wait_gather(n, slot)

        @pl.when(n >= 2)
        def _():
            wait_scatter(n - 2, slot)

        e_lo = lo_ref[n]
        e_hi = hi_ref[n]

        @pl.when(newa_ref[n] == 1)
        def _():
            wb1[e_lo] = w1a_ref[0, 0].astype(BF16)
            wb3[e_lo] = w3a_ref[0, 0].astype(BF16)
            wb2[e_lo] = w2a_ref[0, 0].astype(BF16)

        @pl.when(newb_ref[n] == 1)
        def _():
            wb1[e_hi] = w1b_ref[0, 0].astype(BF16)
            wb3[e_hi] = w3b_ref[0, 0].astype(BF16)
            wb2[e_hi] = w2b_ref[0, 0].astype(BF16)

        xe = xbuf[slot].reshape(TM, XEXT)
        x = xe[:, 0:D]
        w_lo = xe[:, D:D + 1]
        w_hi = xe[:, D + 1:D + 2]
        mod_id = xe[:, D + 2:D + 3]

        def pick(lo, hi):
            return jnp.where(mod_id < 0.5, mod_ref[0][:, lo:hi],
                             jnp.where(mod_id < 1.5, mod_ref[1][:, lo:hi], mod_ref[2][:, lo:hi]))

        h = _modulate(x, g_ref[layer:layer + 1, :], pick(3 * D, 4 * D), pick(4 * D, 5 * D)).astype(BF16)

        def act(e, w):
            h1 = jnp.dot(h, wb1[e], preferred_element_type=F32)
            h3 = jnp.dot(h, wb3[e], preferred_element_type=F32)
            return ((h1 * jax.nn.sigmoid(h1)) * h3 * w).astype(BF16)

        y = (jnp.dot(act(e_lo, w_lo), wb2[e_lo], preferred_element_type=F32)
             + jnp.dot(act(e_hi, w_hi), wb2[e_hi], preferred_element_type=F32))
        ybuf[slot] = (x + pick(5 * D, 6 * D) * y).reshape(OCT, SUBLANES, D)
        start_scatter(n, slot)

        @pl.when(n == n_used - 1)
        def _():
            @pl.when(n >= 1)
            def _():
                wait_scatter(n - 1, 1 - slot)
            wait_scatter(n, slot)

    for s in range(2):
        @pl.when((n < n_used) & (n % 2 == s))
        def _():
            step(s)


def _ffn(pos, lo, hi, slots, n_used, chunks, xext, mods, layer, g, w1, w3, w2):
    a_map = lambda n, p, lo, hi, sa, sb, na, nb, nu, ch: (layer, sa[n], 0, 0)
    b_map = lambda n, p, lo, hi, sa, sb, na, nb, nu, ch: (layer, sb[n], 0, 0)
    up = lambda imap: pl.BlockSpec((1, 1, D, D_EXPERT), imap)
    down = lambda imap: pl.BlockSpec((1, 1, D_EXPERT, D), imap)
    return pl.pallas_call(
        functools.partial(_ffn_body, layer),
        grid_spec=pltpu.PrefetchScalarGridSpec(
            num_scalar_prefetch=9, grid=(NT_FFN,),
            in_specs=[pl.BlockSpec(memory_space=pl.ANY),
                      pl.BlockSpec((SUBLANES, 1, 6 * D), lambda n, *_: (layer, 0, 0)),
                      pl.BlockSpec((DEPTH, D), lambda n, *_: (0, 0)),
                      up(a_map), up(b_map), up(a_map), up(b_map), down(a_map), down(b_map)],
            out_specs=pl.BlockSpec(memory_space=pl.ANY),
            scratch_shapes=[pltpu.SMEM((P_FFN,), I32), pltpu.SMEM((NT_FFN * CH,), I32),
                            pltpu.VMEM((2, OCT, SUBLANES, XEXT), F32),
                            pltpu.VMEM((2, OCT, SUBLANES, D), F32),
                            pltpu.VMEM((PER_GROUP, D, D_EXPERT), BF16),
                            pltpu.VMEM((PER_GROUP, D, D_EXPERT), BF16),
                            pltpu.VMEM((PER_GROUP, D_EXPERT, D), BF16),
                            pltpu.SemaphoreType.DMA((2,)), pltpu.SemaphoreType.DMA((2,))]),
        out_shape=jax.ShapeDtypeStruct((N_TOK + 2 * TM, D), F32),
        compiler_params=_cparams(1, VMEM_LIMIT),
        name=f"ffn{layer}",
    )(pos, lo, hi, *slots, n_used, chunks, xext, mods, g, w1, w1, w3, w3, w2, w2)


def _moe(routed, mods, layer, g, w1, w3, w2):
    xext, info, counts = routed
    pos, lo, hi, slots, n_used, chunks = _plan(info, counts, layer)
    return _ffn(pos, lo, hi, slots, n_used, chunks, xext, mods, layer, g, w1, w3, w2)


FG = 256


def _l1_in_body(x_ref, mod_ref, g_ref, w_ref, c_ref, s_ref, zc_ref, zs_ref, wb_ref):
    _bf16_once(w_ref, wb_ref)
    m = mod_ref[0]
    h = _modulate(x_ref[...], g_ref[1:2, :], m[:, 0:D], m[:, D:2 * D])
    z = jnp.dot(h.astype(BF16), wb_ref[...], preferred_element_type=F32).astype(BF16)
    for g in range(D // FG):
        zg = z[:, g * FG:(g + 1) * FG]
        zc_ref[:, g * FG:(g + 1) * FG] = jnp.dot(zg, c_ref[...], preferred_element_type=F32).astype(BF16)
        zs_ref[:, g * FG:(g + 1) * FG] = jnp.dot(zg, s_ref[...], preferred_element_type=F32).astype(BF16)


def _l1_in(x, mods, g, w, c256, s256):
    const = lambda shape: pl.BlockSpec(shape, lambda i: (0,) * len(shape))
    return pl.pallas_call(
        _l1_in_body,
        grid=(NT,),
        in_specs=[pl.BlockSpec((TB, D), lambda i: (i, 0)),
                  pl.BlockSpec((1, 1, 6 * D),
                               lambda i: (SUBLANES + _mod_row(i, NT_CTX, T_LAT // TB), 0, 0)),
                  const((DEPTH, D)), const((D, D)), const((FG, FG)), const((FG, FG))],
        out_specs=[pl.BlockSpec((TB, D), lambda i: (i, 0)), pl.BlockSpec((TB, D), lambda i: (i, 0))],
        out_shape=[jax.ShapeDtypeStruct((N_TOK, D), BF16), jax.ShapeDtypeStruct((N_TOK, D), BF16)],
        scratch_shapes=[pltpu.VMEM((D, D), BF16)],
        compiler_params=_cparams(1),
        name="l1_in",
    )(x, mods, g, w, c256, s256)


def _l1_out_body(zc_t_ref, zs_t_ref, zc_q_ref, zs_q_ref, c256_ref, s256_ref, c1k_ref, s1k_ref,
                 x_ref, mod_ref, wo_ref, g_ref, wr_ref, eb_ref, gb_ref, tri_ref,
                 xext_ref, info_ref, cnt_ref, f_ref, wb_ref, base_ref):
    i = pl.program_id(0)
    _bf16_once(wo_ref, wb_ref)

    @pl.when(i < NT_CTX)
    def _():
        for q in range(TB // T_CTX):
            rows = slice(q * T_CTX, (q + 1) * T_CTX)
            f = (jnp.dot(c256_ref[...], zc_t_ref[rows, :], preferred_element_type=F32)
                 - jnp.dot(s256_ref[...], zs_t_ref[rows, :], preferred_element_type=F32))
            f_ref[rows, :] = f.astype(BF16)

    @pl.when(i >= NT_CTX)
    def _():
        f = (jnp.dot(c1k_ref[...], zc_q_ref[...], preferred_element_type=F32)
             - jnp.dot(s1k_ref[...], zs_q_ref[...], preferred_element_type=F32))
        f_ref[...] = f.astype(BF16)

    out = jnp.dot(f_ref[...], wb_ref[...], preferred_element_type=F32)
    x3 = x_ref[...] + mod_ref[0][:, 2 * D:3 * D] * out
    _route_tile(x3, mod_ref[0], _mod_row(i, NT_CTX, T_LAT // TB), 1, g_ref, wr_ref, eb_ref, gb_ref, tri_ref,
                xext_ref, info_ref, cnt_ref, base_ref)


def _l1_out(zc, zs, c256, s256, c1k, s1k, x, mods, w_out, route_ops):
    const = lambda shape: pl.BlockSpec(shape, lambda i: (0,) * len(shape))
    r_in, r_out, r_shape, r_scratch = _route_specs(TB, lambda i: i)
    tile_map = lambda i: (jnp.minimum(i, NT_CTX - 1), 0)
    seq_map = lambda i: (N_CTX // T_LAT + jnp.maximum(i - NT_CTX, 0) // (T_LAT // TB), 0)
    row_map = lambda i: (jnp.maximum(i - NT_CTX, 0) % (T_LAT // TB), 0)
    return pl.pallas_call(
        _l1_out_body,
        grid=(NT,),
        in_specs=[pl.BlockSpec((TB, D), tile_map), pl.BlockSpec((TB, D), tile_map),
                  pl.BlockSpec((T_LAT, D), seq_map), pl.BlockSpec((T_LAT, D), seq_map),
                  const((T_CTX, T_CTX)), const((T_CTX, T_CTX)),
                  pl.BlockSpec((TB, T_LAT), row_map), pl.BlockSpec((TB, T_LAT), row_map),
                  pl.BlockSpec((TB, D), lambda i: (i, 0)),
                  pl.BlockSpec((1, 1, 6 * D),
                               lambda i: (SUBLANES + _mod_row(i, NT_CTX, T_LAT // TB), 0, 0)),
                  const((D, D))] + r_in,
        out_specs=r_out,
        out_shape=r_shape,
        scratch_shapes=[pltpu.VMEM((TB, D), BF16), pltpu.VMEM((D, D), BF16), r_scratch],
        compiler_params=_cparams(1, VMEM_LIMIT),
        name="l1_out",
    )(zc, zs, zc, zs, c256, s256, c1k, s1k, x, mods, w_out, *route_ops)


def kernel(x_prompt, x_sample, cache_k, cache_v, c, c_ctx, ada_w, ada_b, norm_mix, norm_ffn, a_w_in, a_q_norm, a_k_norm, a_sink, pool_w, pool_scale, a_w_out, f_w_in, f_w_out, router_g_w, router_g_b, router_e_w, router_e_b, moe_w1, moe_w3, moe_w2):
    xp = x_prompt.reshape(N_CTX, D)
    xs = x_sample.reshape(N_LAT, D)

    mods = _adaln(c_ctx, c, ada_w, ada_b)

    tabs = _rope_tables()
    lane = np.arange(LANES)
    bd = jnp.asarray((lane[:, None] // HEAD_DIM) == (lane[None, :] // HEAD_DIM), BF16)
    c256, s256 = _dft_tables(T_CTX)
    c1k, s1k = _dft_tables(T_LAT)

    rw = jnp.swapaxes(jnp.concatenate([router_e_w, router_g_w], axis=2), 1, 2)
    rw = jnp.pad(rw, ((0, 0), (0, ROUTE_ROWS - rw.shape[1]), (0, 0)))
    rw_hi = rw.astype(BF16)
    rw_split = jnp.stack([rw_hi, (rw - rw_hi.astype(F32)).astype(BF16)], axis=1)

    def route_operands(l, tile):
        return _route_operands(norm_ffn, rw_split[l], router_e_b, router_g_b, tile)

    q, k, v, u, new_k, new_v = _l0_in(xp, xs, mods, norm_mix, a_w_in[0], a_q_norm, a_k_norm, bd, tabs)
    o_ctx = _ctx_attn(q, k, v, a_sink)
    ck = cache_k[:, 0].reshape(N_LAT_B, PAST, KV_W)
    cv = cache_v[:, 0].reshape(N_LAT_B, PAST, KV_W)
    o_lat = _lat_attn(q, k, v, ck, cv, a_sink)
    routed = _l0_out(o_ctx, o_lat, u, xp, xs, mods, pool_w[0], pool_scale[0][None, :], a_w_out[0],
                     route_operands(0, TB_MIX))
    x2 = _moe(routed, mods, 0, norm_ffn, moe_w1, moe_w3, moe_w2)

    zc, zs = _l1_in(x2, mods, norm_mix, f_w_in[0], c256, s256)
    routed = _l1_out(zc, zs, c256, s256, c1k, s1k, x2, mods, f_w_out[0], route_operands(1, TB))
    x4 = _moe(routed, mods, 1, norm_ffn, moe_w1, moe_w3, moe_w2)

    def cache_entry(t):
        t = t.reshape(N_CTX_B, 1, KV_W // HEAD_DIM, HEAD_DIM, T_CTX)
        return jnp.transpose(t, (0, 1, 4, 2, 3))

    new_k, new_v = cache_entry(new_k), cache_entry(new_v)
    return (x4[:N_CTX].reshape(N_CTX_B, T_CTX, D), x4[N_CTX:N_TOK].reshape(N_LAT_B, T_LAT, D),
            new_k, new_v)
```

```python
import functools

import numpy as np
import jax
import jax.numpy as jnp
from jax import lax
from jax.experimental import pallas as pl
from jax.experimental.pallas import tpu as pltpu

F32 = jnp.float32
BF16 = jnp.bfloat16
I32 = jnp.int32

D = 1024
DEPTH = 2
N_CTX_B, T_CTX = 16, 256
N_LAT_B, T_LAT = 2, 1024
N_CTX = N_CTX_B * T_CTX
N_LAT = N_LAT_B * T_LAT
N_TOK = N_CTX + N_LAT
PAST = 512
GRID_W = 64
HEAD_DIM = 64
N_HEADS = 8
ATTN_W = 512
KV_W = 128
POOL_W = 512
POOL_WINDOWS = (2, 4, 8, 16)
MIX_IN = ATTN_W + 2 * KV_W + POOL_W
WINDOW = 128
N_GROUPS = 4
PER_GROUP = 4
N_EXPERTS = 16
D_EXPERT = 512
ROPE_THETA = 10000.0
EPS = 1e-6
NEG = -1e30

LANES = 128
SUBLANES = 8
TB = 512
NT = N_TOK // TB
NT_CTX = N_CTX // TB
TB_MIX = 1024
TM = 256
PAIRS = 6
N_CLASS = N_GROUPS * PAIRS
CLASS_ROWS = 32
NT_FFN = N_TOK // TM + N_CLASS
P_FFN = NT_FFN * TM
CH = 32
XEXT = D + LANES
ROUTE_ROWS = 32

VMEM_LIMIT = 56 * 1024 * 1024


def _cparams(n_axes=1, vmem=None):
    return pltpu.CompilerParams(dimension_semantics=("arbitrary",) * n_axes,
                                vmem_limit_bytes=vmem)


def _modulate(x, g, shift, scale):
    ms = jnp.mean(x * x, axis=-1, keepdims=True)
    return (x * lax.rsqrt(ms + EPS) * g) * (1.0 + scale) + shift


def _bf16_once(w_ref, wb_ref):
    @pl.when(pl.program_id(0) == 0)
    def _():
        wb_ref[...] = w_ref[...].astype(BF16)


def _mod_row(tile, tiles_ctx, tiles_per_lat):
    return (tile >= tiles_ctx).astype(I32) + (tile >= tiles_ctx + tiles_per_lat).astype(I32)


def _rope_tables():
    t = np.arange(T_LAT)
    row = (t // GRID_W).astype(np.float64)
    col = (t % GRID_W).astype(np.float64)
    nf = HEAD_DIM // 4
    freqs = ROPE_THETA ** (-np.arange(nf, dtype=np.float64) / nf)
    d = np.arange(HEAD_DIM)
    pos = np.where(d[None, :] < HEAD_DIM // 2, row[:, None], col[:, None])
    ang = pos * freqs[d % nf][None, :]
    first = (d % (HEAD_DIM // 2)) < nf
    cos = np.cos(ang)
    sin_a = np.where(first[None, :], -np.sin(ang), 0.0)
    sin_b = np.where(first[None, :], 0.0, np.sin(ang))
    ident = (np.ones((TB, HEAD_DIM)), np.zeros((TB, HEAD_DIM)), np.zeros((TB, HEAD_DIM)))
    out = []
    for tab, idt in zip((cos, sin_a, sin_b), ident):
        full = np.concatenate([tab, idt], axis=0)
        out.append(jnp.asarray(np.tile(full, (1, LANES // HEAD_DIM)), F32))
    return out


def _dft_tables(t):
    m = np.outer(np.arange(t), np.arange(t)) % t
    ang = 2.0 * np.pi * m / t
    s = 1.0 / np.sqrt(t)
    return jnp.asarray(np.cos(ang) * s, F32).astype(BF16), jnp.asarray(np.sin(ang) * s, F32).astype(BF16)


def _adaln_rows(layer, cctx_ref, c_ref, w_ref, b_ref, o_ref):
    c = jnp.concatenate([cctx_ref[...], c_ref[...], jnp.zeros((SUBLANES - 1 - N_LAT_B, D), F32)], axis=0)
    s = (c * jax.nn.sigmoid(c)).astype(BF16)
    m = jnp.dot(s, w_ref[0].astype(BF16), preferred_element_type=F32) + b_ref[layer:layer + 1, :]
    for r in range(SUBLANES):
        o_ref[r] = m[r:r + 1]


def _adaln_specs(layer, tn, step):
    in_specs = [pl.BlockSpec((1, D), lambda *i: (0, 0)),
                pl.BlockSpec((N_LAT_B, D), lambda *i: (0, 0)),
                pl.BlockSpec((1, D, tn), lambda *i: (layer, 0, step(*i))),
                pl.BlockSpec((DEPTH, tn), lambda *i: (0, step(*i)))]
    out_spec = pl.BlockSpec((SUBLANES, 1, tn), lambda *i: (0, 0, step(*i)))
    return in_specs, out_spec, jax.ShapeDtypeStruct((SUBLANES, 1, 6 * D), F32)


def _adaln(c_ctx, c, ada_w, ada_b):
    tn = 1536
    in_specs, out_spec, out_shape = _adaln_specs(0, tn, lambda j: j)
    return pl.pallas_call(
        functools.partial(_adaln_rows, 0),
        grid=(6 * D // tn,),
        in_specs=in_specs,
        out_specs=out_spec,
        out_shape=out_shape,
        compiler_params=_cparams(1),
        name="adaln",
    )(c_ctx.reshape(1, D), c, ada_w, ada_b)


def _l0_in_body(xp_ref, xs_ref, mod_ref, g_ref, w_ref, qg_ref, kg_ref, bd_ref,
                cos_ref, sa_ref, sb_ref, q_ref, k_ref, v_ref, u_ref, kc_ref, vc_ref, wb_ref):
    i = pl.program_id(0)
    _bf16_once(w_ref, wb_ref)
    x = jnp.where(i < NT_CTX, xp_ref[...], xs_ref[...])
    m = mod_ref[0]
    h = _modulate(x, g_ref[0:1, :], m[:, 0:D], m[:, D:2 * D])
    z = jnp.dot(h.astype(BF16), wb_ref[...], preferred_element_type=F32)
    cos, sa, sb, bd = cos_ref[...], sa_ref[...], sb_ref[...], bd_ref[...]
    qg = jnp.concatenate([qg_ref[...]] * (LANES // HEAD_DIM), axis=1)
    kg = jnp.concatenate([kg_ref[...]] * (LANES // HEAD_DIM), axis=1)

    def head_norm_rope(zz, gain):
        ss = jnp.dot((zz * zz).astype(BF16), bd, preferred_element_type=F32)
        y = zz * lax.rsqrt(ss * (1.0 / HEAD_DIM) + EPS) * gain
        return (y * cos + pltpu.roll(y, LANES - 16, axis=1) * sa
                + pltpu.roll(y, 16, axis=1) * sb)

    for s in range(ATTN_W // LANES):
        qs = head_norm_rope(z[:, s * LANES:(s + 1) * LANES], qg)
        q_ref[:, s * LANES:(s + 1) * LANES] = (qs * (HEAD_DIM ** -0.5)).astype(BF16)
    k = head_norm_rope(z[:, ATTN_W:ATTN_W + KV_W], kg)
    v = z[:, ATTN_W + KV_W:ATTN_W + 2 * KV_W]
    k_ref[...] = k
    v_ref[...] = v
    u_ref[...] = z[:, ATTN_W + 2 * KV_W:MIX_IN]

    @pl.when(i < NT_CTX)
    def _():
        for q in range(TB // T_CTX):
            kc_ref[q] = k[q * T_CTX:(q + 1) * T_CTX, :].T
            vc_ref[q] = v[q * T_CTX:(q + 1) * T_CTX, :].T


def _l0_in(xp, xs, mods, g, w_in, qg, kg, bd, tabs):
    tab_spec = pl.BlockSpec(
        (TB, LANES), lambda i: (jnp.where(i < NT_CTX, T_LAT // TB, (i - NT_CTX) % (T_LAT // TB)), 0))
    const = lambda shape: pl.BlockSpec(shape, lambda i: (0,) * len(shape))
    return pl.pallas_call(
        _l0_in_body,
        grid=(NT,),
        in_specs=[pl.BlockSpec((TB, D), lambda i: (jnp.minimum(i, NT_CTX - 1), 0)),
                  pl.BlockSpec((TB, D), lambda i: (jnp.maximum(i - NT_CTX, 0), 0)),
                  pl.BlockSpec((1, 1, 6 * D), lambda i: (_mod_row(i, NT_CTX, T_LAT // TB), 0, 0)),
                  const((DEPTH, D)), const((D, MIX_IN)), const((1, HEAD_DIM)), const((1, HEAD_DIM)),
                  const((LANES, LANES)), tab_spec, tab_spec, tab_spec],
        out_specs=[pl.BlockSpec((TB, ATTN_W), lambda i: (i, 0)),
                   pl.BlockSpec((TB, KV_W), lambda i: (i, 0)),
                   pl.BlockSpec((TB, KV_W), lambda i: (i, 0)),
                   pl.BlockSpec((TB, POOL_W), lambda i: (i, 0)),
                   pl.BlockSpec((TB // T_CTX, KV_W, T_CTX), lambda i: (jnp.minimum(i, NT_CTX - 1), 0, 0)),
                   pl.BlockSpec((TB // T_CTX, KV_W, T_CTX), lambda i: (jnp.minimum(i, NT_CTX - 1), 0, 0))],
        out_shape=[jax.ShapeDtypeStruct((N_TOK, ATTN_W), BF16),
                   jax.ShapeDtypeStruct((N_TOK, KV_W), F32),
                   jax.ShapeDtypeStruct((N_TOK, KV_W), F32),
                   jax.ShapeDtypeStruct((N_TOK, POOL_W), F32),
                   jax.ShapeDtypeStruct((N_CTX_B, KV_W, T_CTX), F32),
                   jax.ShapeDtypeStruct((N_CTX_B, KV_W, T_CTX), F32)],
        scratch_shapes=[pltpu.VMEM((D, MIX_IN), BF16)],
        compiler_params=_cparams(1),
        name="l0_in",
    )(xp, xs, mods, g, w_in, qg, kg, bd, *tabs)


def _head_halves(x):
    z = jnp.zeros_like(x)
    return jnp.concatenate([x, z], axis=1), jnp.concatenate([z, x], axis=1)


_NT_DIMS = (((1,), (1,)), ((), ()))


def _ones_halves(x):
    one = jnp.ones_like(x)
    return jnp.concatenate([x, one], axis=1), jnp.concatenate([one, x], axis=1)


def _sink_attend(scores, values, sk, half):
    mx = sk
    for sc in scores:
        mx = jnp.maximum(mx, jnp.max(sc, axis=-1, keepdims=True))
    acc = None
    for sc, val in zip(scores, values):
        part = jnp.dot(jnp.exp(sc - mx).astype(BF16), val, preferred_element_type=F32)
        acc = part if acc is None else acc + part
    ones_lane = HEAD_DIM * (1 - half)
    den = acc[:, ones_lane:ones_lane + 1] + jnp.exp(sk - mx)
    return acc * (1.0 / den)


def _sink_col(sink_ref, heads, rows):
    return jnp.concatenate([jnp.full((rows, 1), sink_ref[0, h], F32) for h in heads], axis=0)


CTX_SEQS = 8


def _ctx_attn_body(q_ref, k_ref, v_ref, sink_ref, cctx_ref, c_ref, w_ref, b_ref, o_ref, m_ref):
    _adaln_rows(1, cctx_ref, c_ref, w_ref, b_ref, m_ref)
    lo = lax.broadcasted_iota(I32, (T_CTX, LANES), 1) < HEAD_DIM
    for b, j in ((b, j) for b in range(CTX_SEQS) for j in range(KV_W // HEAD_DIM)):
        seq = slice(b * T_CTX, (b + 1) * T_CTX)
        kj = k_ref[seq, j * HEAD_DIM:(j + 1) * HEAD_DIM].astype(BF16)
        vj = v_ref[seq, j * HEAD_DIM:(j + 1) * HEAD_DIM].astype(BF16)
        k_halves = _head_halves(kj)
        vd = jnp.concatenate([vj, vj], axis=1)
        q2 = jnp.concatenate([q_ref[seq, (2 * j) * LANES:(2 * j + 1) * LANES],
                              q_ref[seq, (2 * j + 1) * LANES:(2 * j + 2) * LANES]], axis=0)
        outs = []
        for half in range(2):
            sc = lax.dot_general(q2, k_halves[half], _NT_DIMS, preferred_element_type=F32)
            sk = _sink_col(sink_ref, (4 * j + half, 4 * j + 2 + half), T_CTX)
            mx = jnp.maximum(sk, jnp.max(sc, axis=-1, keepdims=True))
            p = jnp.exp(sc - mx)
            inv = 1.0 / (jnp.exp(sk - mx) + jnp.sum(p, axis=-1, keepdims=True))
            outs.append(jnp.dot((p * inv).astype(BF16), vd, preferred_element_type=F32))
        for s2 in range(2):
            rows = slice(s2 * T_CTX, (s2 + 1) * T_CTX)
            o_ref[seq, (2 * j + s2) * LANES:(2 * j + s2 + 1) * LANES] = (
                jnp.where(lo, outs[0][rows], outs[1][rows]).astype(BF16))


def _ctx_attn(q, k, v, sink, c_ctx, c, ada_w, ada_b):
    rows = CTX_SEQS * T_CTX
    steps = N_CTX_B // CTX_SEQS
    a_in, a_out, a_shape = _adaln_specs(1, 6 * D // steps, lambda b: b)
    return pl.pallas_call(
        _ctx_attn_body,
        grid=(steps,),
        in_specs=[pl.BlockSpec((rows, ATTN_W), lambda b: (b, 0)),
                  pl.BlockSpec((rows, KV_W), lambda b: (b, 0)),
                  pl.BlockSpec((rows, KV_W), lambda b: (b, 0)),
                  pl.BlockSpec(memory_space=pltpu.SMEM)] + a_in,
        out_specs=[pl.BlockSpec((rows, ATTN_W), lambda b: (b, 0)), a_out],
        out_shape=[jax.ShapeDtypeStruct((N_CTX, ATTN_W), BF16), a_shape],
        compiler_params=_cparams(1, VMEM_LIMIT),
        name="ctx_attn",
    )(q, k, v, sink, c_ctx.reshape(1, D), c, ada_w, ada_b)


QB = 128
SPAN = QB + 2 * WINDOW


LAT_QBLOCKS = 4


def _lat_attn_body(q_ref, k_ref, v_ref, ck_ref, cv_ref, sink_ref, o_ref):
    ck = ck_ref[0].astype(BF16)
    cv = cv_ref[0].astype(BF16)
    lo = lax.broadcasted_iota(I32, (QB, LANES), 1) < HEAD_DIM
    kv_heads = [slice(j * HEAD_DIM, (j + 1) * HEAD_DIM) for j in range(KV_W // HEAD_DIM)]
    ck_halves = [_head_halves(ck[:, sl]) for sl in kv_heads]
    cv_halves = [_ones_halves(cv[:, sl]) for sl in kv_heads]
    for r in range(LAT_QBLOCKS):
        start = (pl.program_id(1) * LAT_QBLOCKS + r) * QB
        qrows = slice(r * QB, (r + 1) * QB)
        kws, vws = [], []
        for c in (-1, 0, 1):
            cs = pl.multiple_of(jnp.clip(start + c * QB, 0, T_LAT - QB), QB)
            kws.append(k_ref[pl.ds(cs, QB), :])
            vws.append(v_ref[pl.ds(cs, QB), :])
        kw = jnp.concatenate(kws, axis=0).astype(BF16)
        vw = jnp.concatenate(vws, axis=0).astype(BF16)
        qpos = start + (lax.broadcasted_iota(I32, (2 * QB, SPAN), 0) & (QB - 1))
        kpos = start - WINDOW + lax.broadcasted_iota(I32, (2 * QB, SPAN), 1)
        valid = (kpos >= 0) & (kpos < T_LAT) & (jnp.abs(qpos - kpos) <= WINDOW)
        for j, sl in enumerate(kv_heads):
            kw_halves = _head_halves(kw[:, sl])
            vw_halves = _ones_halves(vw[:, sl])
            q2 = jnp.concatenate([q_ref[qrows, (2 * j) * LANES:(2 * j + 1) * LANES],
                                  q_ref[qrows, (2 * j + 1) * LANES:(2 * j + 2) * LANES]], axis=0)
            outs = []
            for half in range(2):
                s_win = lax.dot_general(q2, kw_halves[half], _NT_DIMS, preferred_element_type=F32)
                s_win = jnp.where(valid, s_win, NEG)
                s_ctx = lax.dot_general(q2, ck_halves[j][half], _NT_DIMS, preferred_element_type=F32)
                sk = _sink_col(sink_ref, (4 * j + half, 4 * j + 2 + half), QB)
                outs.append(_sink_attend([s_win, s_ctx], [vw_halves[half], cv_halves[j][half]], sk, half))
            for s2 in range(2):
                rows = slice(s2 * QB, (s2 + 1) * QB)
                o_ref[qrows, (2 * j + s2) * LANES:(2 * j + s2 + 1) * LANES] = (
                    jnp.where(lo, outs[0][rows], outs[1][rows]).astype(BF16))


def _lat_attn(q, k, v, ck, cv, sink):
    lat0 = N_CTX // T_LAT
    qrows = LAT_QBLOCKS * QB
    return pl.pallas_call(
        _lat_attn_body,
        grid=(N_LAT_B, T_LAT // qrows),
        in_specs=[pl.BlockSpec((qrows, ATTN_W), lambda b, i: (N_CTX // qrows + b * (T_LAT // qrows) + i, 0)),
                  pl.BlockSpec((T_LAT, KV_W), lambda b, i: (lat0 + b, 0)),
                  pl.BlockSpec((T_LAT, KV_W), lambda b, i: (lat0 + b, 0)),
                  pl.BlockSpec((1, PAST, KV_W), lambda b, i: (b, 0, 0)),
                  pl.BlockSpec((1, PAST, KV_W), lambda b, i: (b, 0, 0)),
                  pl.BlockSpec(memory_space=pltpu.SMEM)],
        out_specs=pl.BlockSpec((qrows, ATTN_W), lambda b, i: (b * (T_LAT // qrows) + i, 0)),
        out_shape=jax.ShapeDtypeStruct((N_LAT, ATTN_W), BF16),
        compiler_params=_cparams(2),
        name="lat_attn",
    )(q, k, v, ck, cv, sink)


def _l0_out_body(oc_ref, ol_ref, u_ref, xp_ref, xs_ref, mod_ref, pw_ref, ps_ref, wo_ref,
                 g_ref, wr_ref, eb_ref, gb_ref, tri_ref, xext_ref, info_ref, cnt_ref, wb_ref, base_ref):
    i = pl.program_id(0)
    _bf16_once(wo_ref, wb_ref)
    is_ctx = i < N_CTX // TB_MIX
    o = jnp.where(is_ctx, oc_ref[...], ol_ref[...])
    x = jnp.where(is_ctx, xp_ref[...], xs_ref[...])
    tseq = jnp.where(is_ctx, T_CTX, T_LAT)
    pos = lax.broadcasted_iota(I32, (TB_MIX, LANES), 0) & (tseq - 1)

    def ahead(a, k):
        return jnp.where(pos + k < tseq, pltpu.roll(a, TB_MIX - k, axis=0), 0.0)

    def behind(a, k):
        return jnp.where(pos - k >= 0, pltpu.roll(a, k, axis=0), 0.0)

    ys = []
    for g, win in enumerate(POOL_WINDOWS):
        hw = win // 2
        ug = u_ref[:, g * LANES:(g + 1) * LANES]
        fwd, bwd, k = ug, behind(ug, 1), 1
        while k < hw:
            fwd = fwd + ahead(fwd, k)
            bwd = bwd + behind(bwd, k)
            k *= 2
        acc = fwd + bwd
        cnt = (jnp.minimum(pos + hw, tseq) - jnp.maximum(pos - hw, 0)).astype(F32)
        pooled = acc / cnt - ug
        ys.append(jnp.dot(pooled.astype(BF16), pw_ref[g].astype(BF16), preferred_element_type=F32))
    y = jnp.concatenate(ys, axis=1) * ps_ref[...]
    out = (jnp.dot(o, wb_ref[0:ATTN_W, :], preferred_element_type=F32)
           + jnp.dot(y.astype(BF16), wb_ref[ATTN_W:ATTN_W + POOL_W, :], preferred_element_type=F32))
    x1 = x + mod_ref[0][:, 2 * D:3 * D] * out
    _route_tile(x1, mod_ref[0], _mod_row(i, N_CTX // TB_MIX, 1), 0, g_ref, wr_ref, eb_ref, gb_ref, tri_ref,
                xext_ref, info_ref, cnt_ref, base_ref)


def _l0_out(o_ctx, o_lat, u, xp, xs, mods, pool_w, pool_scale, w_out, route_ops):
    ntc = N_CTX // TB_MIX
    r_in, r_out, r_shape, r_scratch = _route_specs(TB_MIX, lambda i: i)
    const = lambda shape: pl.BlockSpec(shape, lambda i: (0,) * len(shape))
    ctx_map = lambda i: (jnp.minimum(i, ntc - 1), 0)
    lat_map = lambda i: (jnp.maximum(i - ntc, 0), 0)
    return pl.pallas_call(
        _l0_out_body,
        grid=(N_TOK // TB_MIX,),
        in_specs=[pl.BlockSpec((TB_MIX, ATTN_W), ctx_map),
                  pl.BlockSpec((TB_MIX, ATTN_W), lat_map),
                  pl.BlockSpec((TB_MIX, POOL_W), lambda i: (i, 0)),
                  pl.BlockSpec((TB_MIX, D), ctx_map),
                  pl.BlockSpec((TB_MIX, D), lat_map),
                  pl.BlockSpec((1, 1, 6 * D), lambda i: (_mod_row(i, ntc, 1), 0, 0)),
                  const((len(POOL_WINDOWS), LANES, LANES)), const((1, POOL_W)), const((D, D))] + r_in,
        out_specs=r_out,
        out_shape=r_shape,
        scratch_shapes=[pltpu.VMEM((D, D), BF16), r_scratch],
        compiler_params=_cparams(1, VMEM_LIMIT),
        name="l0_out",
    )(o_ctx, o_lat, u, xp, xs, mods, pool_w, pool_scale, w_out, *route_ops)


def _first_max(vals):
    best, idx = vals[0], jnp.zeros(vals[0].shape, I32)
    for r in range(1, len(vals)):
        better = vals[r] > best
        idx = jnp.where(better, r, idx)
        best = jnp.where(better, vals[r], best)
    return best, idx


def _softmax_rows(rows):
    mx = functools.reduce(jnp.maximum, rows)
    ex = [jnp.exp(r - mx) for r in rows]
    tot = functools.reduce(lambda a, b: a + b, ex)
    return [e / tot for e in ex]


def _route_tile(x, m, mod_id, layer, g_ref, wr_ref, eb_ref, gb_ref, tri_ref, xext_ref, info_ref, cnt_ref,
                base_ref):
    t_rows = x.shape[0]

    @pl.when(pl.program_id(0) == 0)
    def _():
        base_ref[...] = jnp.zeros_like(base_ref)

    h = _modulate(x, g_ref[layer:layer + 1, :], m[:, 3 * D:4 * D], m[:, 4 * D:5 * D])

    hh = h.astype(BF16)
    hl = (h - hh.astype(F32)).astype(BF16)
    wh, wl = wr_ref[0], wr_ref[1]
    lg = (lax.dot_general(wh, hh, _NT_DIMS, preferred_element_type=F32)
          + lax.dot_general(wl, hh, _NT_DIMS, preferred_element_type=F32)
          + lax.dot_general(wh, hl, _NT_DIMS, preferred_element_type=F32))
    lg_e = [lg[e:e + 1] + eb_ref[layer, e] for e in range(N_EXPERTS)]
    lg_g = [lg[N_EXPERTS + g:N_EXPERTS + g + 1] + gb_ref[layer, g] for g in range(N_GROUPS)]

    pg = _softmax_rows(lg_g)
    pg_top, gi = _first_max(pg)
    le = []
    for j in range(PER_GROUP):
        sel = lg_e[(N_GROUPS - 1) * PER_GROUP + j]
        for g in range(N_GROUPS - 2, -1, -1):
            sel = jnp.where(gi == g, lg_e[g * PER_GROUP + j], sel)
        le.append(sel)
    pe = _softmax_rows(le)
    p1, i1 = _first_max(pe)
    p2, i2 = _first_max([jnp.where(i1 == j, -1.0, pe[j]) for j in range(PER_GROUP)])
    den = p1 + p2
    w1 = pg_top * p1 / den
    w2 = pg_top * p2 / den

    lo = jnp.minimum(i1, i2)
    hi = jnp.maximum(i1, i2)
    cls = gi * PAIRS + jnp.where(lo == 0, 0, jnp.where(lo == 1, 3, 5)) + hi - lo - 1
    w_lo = jnp.where(i1 == lo, w1, w2)
    w_hi = jnp.where(i1 == lo, w2, w1)

    crow = lax.broadcasted_iota(I32, (CLASS_ROWS, t_rows), 0)
    hit = crow == cls
    onehot = jnp.where(hit, 1.0, 0.0)
    before = jnp.dot(onehot.astype(BF16), tri_ref[...], preferred_element_type=F32)
    before = before + base_ref[:, 0:1]
    rank = jnp.sum(jnp.where(hit, before, 0.0), axis=0, keepdims=True)
    base_ref[...] = base_ref[...] + jnp.sum(onehot, axis=1, keepdims=True)
    cnt_ref[...] = base_ref[...].astype(I32)

    mod_id = jnp.zeros_like(w1) + mod_id.astype(F32)
    zero = jnp.zeros_like(w1)
    info_ref[...] = jnp.concatenate([cls.astype(F32), rank, zero, zero, zero, zero, zero, zero], axis=0)
    side = jnp.concatenate([w_lo, w_hi, mod_id, jnp.zeros((LANES - 3, t_rows), F32)], axis=0).T
    xext_ref[:, 0:D] = x
    xext_ref[:, D:XEXT] = side


def _route_specs(tile, step_map):
    const = lambda shape: pl.BlockSpec(shape, lambda *i: (0,) * len(shape))
    smem = pl.BlockSpec(memory_space=pltpu.SMEM)
    in_specs = [const((DEPTH, D)), const((2, ROUTE_ROWS, D)), smem, smem, const((tile, tile))]
    out_specs = [pl.BlockSpec((tile, XEXT), lambda *i: (step_map(*i), 0)),
                 pl.BlockSpec((SUBLANES, tile), lambda *i: (0, step_map(*i))),
                 const((CLASS_ROWS, LANES))]
    out_shape = [jax.ShapeDtypeStruct((N_TOK, XEXT), F32),
                 jax.ShapeDtypeStruct((SUBLANES, N_TOK), F32),
                 jax.ShapeDtypeStruct((CLASS_ROWS, LANES), I32)]
    return in_specs, out_specs, out_shape, pltpu.VMEM((CLASS_ROWS, LANES), F32)


def _route_operands(g, wr, e_b, g_b, tile):
    tri = jnp.asarray(np.arange(tile)[:, None] < np.arange(tile)[None, :], BF16)
    return g, wr, e_b, g_b, tri


def _plan_body(cnt_ref, info_ref, pos_ref, lo_ref, hi_ref, sa_ref, sb_ref, newa_ref, newb_ref, nu_ref, ch_ref,
               toff_ref, seen_ref):
    tm_shift = TM.bit_length() - 1

    def per_class(c, first_tile):
        cnt = cnt_ref[c, 0]
        tiles = lax.shift_right_logical(cnt + (TM - 1), tm_shift)
        toff_ref[c] = first_tile * TM
        group = lax.div(c, PAIRS)
        pair = c - group * PAIRS
        lo = (pair >= 3).astype(I32) + (pair >= 5).astype(I32)
        hi = jnp.where(pair < 3, pair + 1, jnp.where(pair < 5, pair - 1, 3))

        def per_tile(k, carry):
            n = first_tile + k
            lo_ref[n] = lo
            hi_ref[n] = hi
            sa_ref[n] = group * PER_GROUP + lo
            sb_ref[n] = group * PER_GROUP + hi
            rows = jnp.minimum(cnt - k * TM, TM)
            ch_ref[n] = lax.shift_right_logical(rows + (CH - 1), CH.bit_length() - 1)
            return carry

        lax.fori_loop(0, tiles, per_tile, 0)
        return first_tile + tiles

    n_used = lax.fori_loop(0, N_CLASS, per_class, 0)
    nu_ref[0] = n_used

    for e in range(N_EXPERTS):
        seen_ref[e] = 0

    def per_used(n, slots):
        e_lo, e_hi = sa_ref[n], sb_ref[n]
        new_a = seen_ref[e_lo] == 0
        seen_ref[e_lo] = 1
        new_b = seen_ref[e_hi] == 0
        seen_ref[e_hi] = 1
        newa_ref[n] = new_a.astype(I32)
        newb_ref[n] = new_b.astype(I32)
        slot_a = jnp.where(new_a, e_lo, slots[0])
        slot_b = jnp.where(new_b, e_hi, slots[1])
        sa_ref[n] = slot_a
        sb_ref[n] = slot_b
        return slot_a, slot_b

    slot_a, slot_b = lax.fori_loop(0, n_used, per_used, (jnp.int32(0), jnp.int32(0)))

    def unused(n, carry):
        lo_ref[n] = 0
        hi_ref[n] = 0
        sa_ref[n] = slot_a
        sb_ref[n] = slot_b
        newa_ref[n] = 0
        newb_ref[n] = 0
        ch_ref[n] = 0
        return carry

    lax.fori_loop(n_used, NT_FFN, unused, 0)

    cls = info_ref[0:1, :]
    first_row = jnp.zeros(cls.shape, I32)
    for c in range(N_CLASS):
        first_row = jnp.where(cls == float(c), toff_ref[c], first_row)
    pos = first_row + info_ref[1:2, :].astype(I32)
    pos_ref[...] = jnp.broadcast_to(pos, pos_ref.shape)


def _plan(info, counts, layer):
    smem = pl.BlockSpec(memory_space=pltpu.SMEM)
    per_tile = jax.ShapeDtypeStruct((NT_FFN,), I32)
    pos, lo, hi, sa, sb, newa, newb, n_used, chunks = pl.pallas_call(
        _plan_body,
        grid=(1,),
        in_specs=[smem, pl.BlockSpec((SUBLANES, N_TOK), lambda i: (0, 0))],
        out_specs=[pl.BlockSpec((SUBLANES, N_TOK), lambda i: (0, 0))] + [smem] * 8,
        out_shape=[jax.ShapeDtypeStruct((SUBLANES, N_TOK), I32)] + [per_tile] * 6
                  + [jax.ShapeDtypeStruct((1,), I32), per_tile],
        scratch_shapes=[pltpu.SMEM((CLASS_ROWS,), I32), pltpu.SMEM((N_EXPERTS,), I32)],
        compiler_params=_cparams(1),
        name=f"plan{layer}",
    )(counts, info)
    return pos[0], lo, hi, (sa, sb, newa, newb), n_used, chunks


OCT = TM // SUBLANES


def _ffn_body(layer, pos_ref, lo_ref, hi_ref, sa_ref, sb_ref, newa_ref, newb_ref, nu_ref, ch_ref,
              xext_hbm, mod_ref, g_ref, w1a_ref, w1b_ref, w3a_ref, w3b_ref, w2a_ref, w2b_ref,
              out_hbm, src_ref, dst_ref, xbuf, ybuf, wb1, wb3, wb2, gsem, ssem):
    n = pl.program_id(0)
    n_used = nu_ref[0]

    def gather_copy(tile, s, c, j):
        return pltpu.make_async_copy(
            xext_hbm.at[pl.ds(src_ref[tile * TM + c * CH + j], 1)],
            xbuf.at[s, c * (CH // SUBLANES) + j // SUBLANES, pl.ds(j % SUBLANES, 1)], gsem.at[s])

    def scatter_copy(tile, s, c, j):
        return pltpu.make_async_copy(
            ybuf.at[s, c * (CH // SUBLANES) + j // SUBLANES, pl.ds(j % SUBLANES, 1)],
            out_hbm.at[pl.ds(dst_ref[tile * TM + c * CH + j], 1)], ssem.at[s])

    def start_rows(copy, tile, s):
        n_chunks = ch_ref[tile]
        for c in range(TM // CH):
            @pl.when(c < n_chunks)
            def _():
                for j in range(CH):
                    copy(tile, s, c, j).start()

    def wait_rows(src, dst, sem, tile):
        def chunk(c, carry):
            pltpu.make_async_copy(src, dst, sem).wait()
            return carry
        lax.fori_loop(0, ch_ref[tile], chunk, 0)

    def wait_gather(tile, s):
        rows = xbuf.at[s, pl.ds(0, CH // SUBLANES)]
        wait_rows(rows, rows, gsem.at[s], tile)

    def wait_scatter(tile, s):
        rows = ybuf.at[s, pl.ds(0, CH // SUBLANES)]
        wait_rows(rows, rows, ssem.at[s], tile)

    @pl.when(n == 0)
    def _():
        def pad_rows(tile, carry):
            @pl.when(ch_ref[tile] > 0)
            def _():
                first = tile * TM + (ch_ref[tile] - 1) * CH
                for j in range(CH):
                    src_ref[first + j] = 0
                    dst_ref[first + j] = N_TOK + ((first + j) & (2 * TM - 1))
            return carry

        lax.fori_loop(0, NT_FFN, pad_rows, 0)

        def put(t, carry):
            p = pos_ref[t]
            src_ref[p] = t
            dst_ref[p] = t
            return carry

        lax.fori_loop(0, N_TOK, put, 0, unroll=8)

        xbuf[...] = jnp.zeros_like(xbuf)
        ybuf[...] = jnp.zeros_like(ybuf)
        for s in range(2):
            dumps = [pltpu.make_async_copy(
                ybuf.at[s, q], out_hbm.at[pl.ds(N_TOK + s * TM + q * SUBLANES, SUBLANES)], ssem.at[s])
                for q in range(OCT)]
            for dump in dumps:
                dump.start()
            for dump in dumps:
                dump.wait()
        start_rows(gather_copy, 0, 0)

    def step(slot):
        @pl.when(n + 1 < n_used)
        def _():
            start_rows(gather_copy, n + 1, 1 - slot)

        wait_gather(n, slot)

        @pl.when(n >= 2)
        def _():
            wait_scatter(n - 2, slot)

        e_lo = lo_ref[n]
        e_hi = hi_ref[n]

        @pl.when(newa_ref[n] == 1)
        def _():
            wb1[e_lo] = w1a_ref[0, 0].astype(BF16)
            wb3[e_lo] = w3a_ref[0, 0].astype(BF16)
            wb2[e_lo] = w2a_ref[0, 0].astype(BF16)

        @pl.when(newb_ref[n] == 1)
        def _():
            wb1[e_hi] = w1b_ref[0, 0].astype(BF16)
            wb3[e_hi] = w3b_ref[0, 0].astype(BF16)
            wb2[e_hi] = w2b_ref[0, 0].astype(BF16)

        xe = xbuf[slot].reshape(TM, XEXT)
        x = xe[:, 0:D]
        w_lo = xe[:, D:D + 1]
        w_hi = xe[:, D + 1:D + 2]
        mod_id = xe[:, D + 2:D + 3]

        def pick(lo, hi):
            return jnp.where(mod_id < 0.5, mod_ref[0][:, lo:hi],
                             jnp.where(mod_id < 1.5, mod_ref[1][:, lo:hi], mod_ref[2][:, lo:hi]))

        h = _modulate(x, g_ref[layer:layer + 1, :], pick(3 * D, 4 * D), pick(4 * D, 5 * D)).astype(BF16)

        def act(e, w):
            h1 = jnp.dot(h, wb1[e], preferred_element_type=F32)
            h3 = jnp.dot(h, wb3[e], preferred_element_type=F32)
            return ((h1 * jax.nn.sigmoid(h1)) * h3 * w).astype(BF16)

        y = (jnp.dot(act(e_lo, w_lo), wb2[e_lo], preferred_element_type=F32)
             + jnp.dot(act(e_hi, w_hi), wb2[e_hi], preferred_element_type=F32))
        ybuf[slot] = (x + pick(5 * D, 6 * D) * y).reshape(OCT, SUBLANES, D)
        start_rows(scatter_copy, n, slot)

        @pl.when(n == n_used - 1)
        def _():
            @pl.when(n >= 1)
            def _():
                wait_scatter(n - 1, 1 - slot)
            wait_scatter(n, slot)

    for s in range(2):
        @pl.when((n < n_used) & (n % 2 == s))
        def _():
            step(s)


def _ffn(pos, lo, hi, slots, n_used, chunks, xext, mods, layer, g, w1, w3, w2):
    a_map = lambda n, p, lo, hi, sa, sb, na, nb, nu, ch: (layer, sa[n], 0, 0)
    b_map = lambda n, p, lo, hi, sa, sb, na, nb, nu, ch: (layer, sb[n], 0, 0)
    up = lambda imap: pl.BlockSpec((1, 1, D, D_EXPERT), imap)
    down = lambda imap: pl.BlockSpec((1, 1, D_EXPERT, D), imap)
    return pl.pallas_call(
        functools.partial(_ffn_body, layer),
        grid_spec=pltpu.PrefetchScalarGridSpec(
            num_scalar_prefetch=9, grid=(NT_FFN,),
            in_specs=[pl.BlockSpec(memory_space=pl.ANY),
                      pl.BlockSpec((SUBLANES, 1, 6 * D), lambda n, *_: (0, 0, 0)),
                      pl.BlockSpec((DEPTH, D), lambda n, *_: (0, 0)),
                      up(a_map), up(b_map), up(a_map), up(b_map), down(a_map), down(b_map)],
            out_specs=pl.BlockSpec(memory_space=pl.ANY),
            scratch_shapes=[pltpu.SMEM((P_FFN,), I32), pltpu.SMEM((P_FFN,), I32),
                            pltpu.VMEM((2, OCT, SUBLANES, XEXT), F32),
                            pltpu.VMEM((2, OCT, SUBLANES, D), F32),
                            pltpu.VMEM((PER_GROUP, D, D_EXPERT), BF16),
                            pltpu.VMEM((PER_GROUP, D, D_EXPERT), BF16),
                            pltpu.VMEM((PER_GROUP, D_EXPERT, D), BF16),
                            pltpu.SemaphoreType.DMA((2,)), pltpu.SemaphoreType.DMA((2,))]),
        out_shape=jax.ShapeDtypeStruct((N_TOK + 2 * TM, D), F32),
        compiler_params=_cparams(1, VMEM_LIMIT),
        name=f"ffn{layer}",
    )(pos, lo, hi, *slots, n_used, chunks, xext, mods, g, w1, w1, w3, w3, w2, w2)


def _moe(routed, mods, layer, g, w1, w3, w2):
    xext, info, counts = routed
    pos, lo, hi, slots, n_used, chunks = _plan(info, counts, layer)
    return _ffn(pos, lo, hi, slots, n_used, chunks, xext, mods, layer, g, w1, w3, w2)


FG = 256


def _l1_in_body(x_ref, mod_ref, g_ref, w_ref, c_ref, s_ref, zc_ref, zs_ref, wb_ref):
    _bf16_once(w_ref, wb_ref)
    m = mod_ref[0]
    h = _modulate(x_ref[...], g_ref[1:2, :], m[:, 0:D], m[:, D:2 * D])
    z = jnp.dot(h.astype(BF16), wb_ref[...], preferred_element_type=F32).astype(BF16)
    for g in range(D // FG):
        zg = z[:, g * FG:(g + 1) * FG]
        zc_ref[:, g * FG:(g + 1) * FG] = jnp.dot(zg, c_ref[...], preferred_element_type=F32).astype(BF16)
        zs_ref[:, g * FG:(g + 1) * FG] = jnp.dot(zg, s_ref[...], preferred_element_type=F32).astype(BF16)


def _l1_in(x, mods, g, w, c256, s256):
    const = lambda shape: pl.BlockSpec(shape, lambda i: (0,) * len(shape))
    return pl.pallas_call(
        _l1_in_body,
        grid=(NT,),
        in_specs=[pl.BlockSpec((TB, D), lambda i: (i, 0)),
                  pl.BlockSpec((1, 1, 6 * D),
                               lambda i: (_mod_row(i, NT_CTX, T_LAT // TB), 0, 0)),
                  const((DEPTH, D)), const((D, D)), const((FG, FG)), const((FG, FG))],
        out_specs=[pl.BlockSpec((TB, D), lambda i: (i, 0)), pl.BlockSpec((TB, D), lambda i: (i, 0))],
        out_shape=[jax.ShapeDtypeStruct((N_TOK, D), BF16), jax.ShapeDtypeStruct((N_TOK, D), BF16)],
        scratch_shapes=[pltpu.VMEM((D, D), BF16)],
        compiler_params=_cparams(1),
        name="l1_in",
    )(x, mods, g, w, c256, s256)


def _l1_out_body(zc_t_ref, zs_t_ref, zc_q_ref, zs_q_ref, c256_ref, s256_ref, c1k_ref, s1k_ref,
                 x_ref, mod_ref, wo_ref, g_ref, wr_ref, eb_ref, gb_ref, tri_ref,
                 xext_ref, info_ref, cnt_ref, f_ref, wb_ref, base_ref):
    i = pl.program_id(0)
    _bf16_once(wo_ref, wb_ref)

    @pl.when(i < NT_CTX)
    def _():
        for q in range(TB // T_CTX):
            rows = slice(q * T_CTX, (q + 1) * T_CTX)
            f = (jnp.dot(c256_ref[...], zc_t_ref[rows, :], preferred_element_type=F32)
                 - jnp.dot(s256_ref[...], zs_t_ref[rows, :], preferred_element_type=F32))
            f_ref[rows, :] = f.astype(BF16)

    @pl.when(i >= NT_CTX)
    def _():
        f = (jnp.dot(c1k_ref[...], zc_q_ref[...], preferred_element_type=F32)
             - jnp.dot(s1k_ref[...], zs_q_ref[...], preferred_element_type=F32))
        f_ref[...] = f.astype(BF16)

    out = jnp.dot(f_ref[...], wb_ref[...], preferred_element_type=F32)
    x3 = x_ref[...] + mod_ref[0][:, 2 * D:3 * D] * out
    _route_tile(x3, mod_ref[0], _mod_row(i, NT_CTX, T_LAT // TB), 1, g_ref, wr_ref, eb_ref, gb_ref, tri_ref,
                xext_ref, info_ref, cnt_ref, base_ref)


def _l1_out(zc, zs, c256, s256, c1k, s1k, x, mods, w_out, route_ops):
    const = lambda shape: pl.BlockSpec(shape, lambda i: (0,) * len(shape))
    r_in, r_out, r_shape, r_scratch = _route_specs(TB, lambda i: i)
    tile_map = lambda i: (jnp.minimum(i, NT_CTX - 1), 0)
    seq_map = lambda i: (N_CTX // T_LAT + jnp.maximum(i - NT_CTX, 0) // (T_LAT // TB), 0)
    row_map = lambda i: (jnp.maximum(i - NT_CTX, 0) % (T_LAT // TB), 0)
    return pl.pallas_call(
        _l1_out_body,
        grid=(NT,),
        in_specs=[pl.BlockSpec((TB, D), tile_map), pl.BlockSpec((TB, D), tile_map),
                  pl.BlockSpec((T_LAT, D), seq_map), pl.BlockSpec((T_LAT, D), seq_map),
                  const((T_CTX, T_CTX)), const((T_CTX, T_CTX)),
                  pl.BlockSpec((TB, T_LAT), row_map), pl.BlockSpec((TB, T_LAT), row_map),
                  pl.BlockSpec((TB, D), lambda i: (i, 0)),
                  pl.BlockSpec((1, 1, 6 * D),
                               lambda i: (_mod_row(i, NT_CTX, T_LAT // TB), 0, 0)),
                  const((D, D))] + r_in,
        out_specs=r_out,
        out_shape=r_shape,
        scratch_shapes=[pltpu.VMEM((TB, D), BF16), pltpu.VMEM((D, D), BF16), r_scratch],
        compiler_params=_cparams(1, VMEM_LIMIT),
        name="l1_out",
    )(zc, zs, zc, zs, c256, s256, c1k, s1k, x, mods, w_out, *route_ops)


def kernel(x_prompt, x_sample, cache_k, cache_v, c, c_ctx, ada_w, ada_b, norm_mix, norm_ffn, a_w_in, a_q_norm, a_k_norm, a_sink, pool_w, pool_scale, a_w_out, f_w_in, f_w_out, router_g_w, router_g_b, router_e_w, router_e_b, moe_w1, moe_w3, moe_w2):
    xp = x_prompt.reshape(N_CTX, D)
    xs = x_sample.reshape(N_LAT, D)

    mods = _adaln(c_ctx, c, ada_w, ada_b)

    tabs = _rope_tables()
    lane = np.arange(LANES)
    bd = jnp.asarray((lane[:, None] // HEAD_DIM) == (lane[None, :] // HEAD_DIM), BF16)
    c256, s256 = _dft_tables(T_CTX)
    c1k, s1k = _dft_tables(T_LAT)

    rw = jnp.swapaxes(jnp.concatenate([router_e_w, router_g_w], axis=2), 1, 2)
    rw = jnp.pad(rw, ((0, 0), (0, ROUTE_ROWS - rw.shape[1]), (0, 0)))
    rw_hi = rw.astype(BF16)
    rw_split = jnp.stack([rw_hi, (rw - rw_hi.astype(F32)).astype(BF16)], axis=1)

    def route_operands(l, tile):
        return _route_operands(norm_ffn, rw_split[l], router_e_b, router_g_b, tile)

    q, k, v, u, new_k, new_v = _l0_in(xp, xs, mods, norm_mix, a_w_in[0], a_q_norm, a_k_norm, bd, tabs)
    o_ctx, mods1 = _ctx_attn(q, k, v, a_sink, c_ctx, c, ada_w, ada_b)
    ck = cache_k[:, 0].reshape(N_LAT_B, PAST, KV_W)
    cv = cache_v[:, 0].reshape(N_LAT_B, PAST, KV_W)
    o_lat = _lat_attn(q, k, v, ck, cv, a_sink)
    routed = _l0_out(o_ctx, o_lat, u, xp, xs, mods, pool_w[0], pool_scale[0][None, :], a_w_out[0],
                     route_operands(0, TB_MIX))
    x2 = _moe(routed, mods, 0, norm_ffn, moe_w1, moe_w3, moe_w2)

    zc, zs = _l1_in(x2, mods1, norm_mix, f_w_in[0], c256, s256)
    routed = _l1_out(zc, zs, c256, s256, c1k, s1k, x2, mods1, f_w_out[0], route_operands(1, TB))
    x4 = _moe(routed, mods1, 1, norm_ffn, moe_w1, moe_w3, moe_w2)

    def cache_entry(t):
        t = t.reshape(N_CTX_B, 1, KV_W // HEAD_DIM, HEAD_DIM, T_CTX)
        return jnp.transpose(t, (0, 1, 4, 2, 3))

    new_k, new_v = cache_entry(new_k), cache_entry(new_v)
    return (x4[:N_CTX].reshape(N_CTX_B, T_CTX, D), x4[N_CTX:N_TOK].reshape(N_LAT_B, T_LAT, D),
            new_k, new_v)
```
